```python
import math
import jax, jax.numpy as jnp
from jax import lax
import numpy as np

D_MODEL = 1024
BATCH = 8
SEQ = 4096
DEPTH = 1

N_META = 16
D_FF = 2816
D_CONV = 512
CONV_WIDTH = 31
D_SSM = 512
SSM_GROUP = 16
N_SSM_GROUPS = D_SSM // SSM_GROUP
SSM_STATE = 64
N_BRANCHES = 2
D_IN = 2 * D_CONV + D_SSM + N_BRANCHES * D_MODEL
DT_MIN = 1e-3
DT_MAX = 1e-1
EPS = 1e-6

kernel_name = "hybrid_meta_conformer_conv_s5_gated_macaron"


def rms_norm(x, g):
    xf = x.astype(jnp.float32)
    y = xf * lax.rsqrt(jnp.mean(xf * xf, axis=-1, keepdims=True) + EPS)
    return (y * g.astype(jnp.float32)).astype(x.dtype)


def swiglu_ffn(x, w1, w3, w2):
    return (jax.nn.silu(x @ w1) * (x @ w3)) @ w2


def conformer_conv_branch(a, dw, dw_b, ln_g, ln_b, w_proj):
    v, g = jnp.split(a, 2, axis=-1)
    z = v * jax.nn.sigmoid(g)
    z = lax.conv_general_dilated(
        z, dw[:, None, :].astype(z.dtype), window_strides=(1,),
        padding=((CONV_WIDTH - 1, 0),),
        dimension_numbers=("NWC", "WIO", "NWC"),
        feature_group_count=D_CONV) + dw_b
    zf = z.astype(jnp.float32)
    mu = jnp.mean(zf, axis=-1, keepdims=True)
    var = jnp.mean(jnp.square(zf - mu), axis=-1, keepdims=True)
    zf = (zf - mu) * lax.rsqrt(var + EPS) * ln_g.astype(jnp.float32) + ln_b.astype(jnp.float32)
    z = jax.nn.silu(zf).astype(a.dtype)
    return z @ w_proj


def s5_branch(u, lam_re, lam_im, log_dt, b_re, b_im, c_re, c_im, d_skip, w_v, w_g):
    bsz, seq_len, _ = u.shape
    uf = u.astype(jnp.float32).reshape(bsz, seq_len, N_SSM_GROUPS, SSM_GROUP)
    lam = lax.complex(lam_re.astype(jnp.float32), lam_im.astype(jnp.float32))
    dt = jnp.exp(log_dt.astype(jnp.float32))[:, None]
    lam_bar = jnp.exp(lam * dt)
    b = lax.complex(b_re.astype(jnp.float32), b_im.astype(jnp.float32))
    b_bar = ((lam_bar - 1.0) / lam)[..., None] * b
    bu = jnp.einsum("blgh,gph->blgp", uf.astype(jnp.complex64), b_bar)
    a = jnp.broadcast_to(lam_bar, bu.shape)

    def combine(e1, e2):
        a1, s1 = e1
        a2, s2 = e2
        return a1 * a2, a2 * s1 + s2

    _, states = lax.associative_scan(combine, (a, bu), axis=1)
    c = lax.complex(c_re.astype(jnp.float32), c_im.astype(jnp.float32))
    y = jnp.einsum("blgp,ghp->blgh", states, c).real \
        + d_skip.astype(jnp.float32).reshape(N_SSM_GROUPS, SSM_GROUP) * uf
    y = jax.nn.gelu(y.reshape(bsz, seq_len, D_SSM)).astype(u.dtype)
    return (y @ w_v) * jax.nn.sigmoid(y @ w_g)


def _fwd_setup_inputs(seed: int = 0) -> dict:
    key = jax.random.key(seed)
    ks = jax.random.split(key, 32)
    f32 = jnp.float32
    nrm = lambda k, shape, scale: jax.random.normal(k, shape, f32) * scale
    gain = lambda k, shape: 1.0 + 0.02 * jax.random.normal(k, shape, f32)
    G, P, H = N_SSM_GROUPS, SSM_STATE, SSM_GROUP
    n_idx = jnp.arange(P, dtype=f32)
    return {
        "x": nrm(ks[0], (BATCH, SEQ, D_MODEL), 1.0),
        "meta_tokens": nrm(ks[1], (N_META, D_MODEL), 1.0),
        "ffn1_norm": gain(ks[2], (DEPTH, D_MODEL)),
        "ffn1_w1": nrm(ks[3], (DEPTH, D_MODEL, D_FF), D_MODEL ** -0.5),
        "ffn1_w3": nrm(ks[4], (DEPTH, D_MODEL, D_FF), D_MODEL ** -0.5),
        "ffn1_w2": nrm(ks[5], (DEPTH, D_FF, D_MODEL), D_FF ** -0.5),
        "mix_norm": gain(ks[6], (DEPTH, D_MODEL)),
        "w_in": nrm(ks[7], (DEPTH, D_MODEL, D_IN), D_MODEL ** -0.5),
        "b_gate": nrm(ks[8], (DEPTH, N_BRANCHES * D_MODEL), 0.01),
        "conv_dw": nrm(ks[9], (DEPTH, CONV_WIDTH, D_CONV), CONV_WIDTH ** -0.5),
        "conv_dw_b": nrm(ks[10], (DEPTH, D_CONV), 0.01),
        "conv_ln_g": gain(ks[11], (DEPTH, D_CONV)),
        "conv_ln_b": nrm(ks[12], (DEPTH, D_CONV), 0.01),
        "conv_proj": nrm(ks[13], (DEPTH, D_CONV, D_MODEL), D_CONV ** -0.5),
        "ssm_lam_re": -0.5 + 0.01 * jax.random.normal(ks[14], (DEPTH, G, P), f32),
        "ssm_lam_im": math.pi * n_idx + 0.01 * jax.random.normal(ks[15], (DEPTH, G, P), f32),
        "ssm_log_dt": jax.random.uniform(ks[16], (DEPTH, G), f32,
                                         minval=math.log(DT_MIN), maxval=math.log(DT_MAX)),
        "ssm_b_re": nrm(ks[17], (DEPTH, G, P, H), (2.0 * H) ** -0.5),
        "ssm_b_im": nrm(ks[18], (DEPTH, G, P, H), (2.0 * H) ** -0.5),
        "ssm_c_re": nrm(ks[19], (DEPTH, G, H, P), (2.0 * P) ** -0.5),
        "ssm_c_im": nrm(ks[20], (DEPTH, G, H, P), (2.0 * P) ** -0.5),
        "ssm_d": nrm(ks[21], (DEPTH, D_SSM), 1.0),
        "ssm_w_v": nrm(ks[22], (DEPTH, D_SSM, D_MODEL), D_SSM ** -0.5),
        "ssm_w_g": nrm(ks[23], (DEPTH, D_SSM, D_MODEL), D_SSM ** -0.5),
        "w_out": nrm(ks[24], (DEPTH, D_MODEL, D_MODEL), D_MODEL ** -0.5),
        "ffn2_norm": gain(ks[25], (DEPTH, D_MODEL)),
        "ffn2_w1": nrm(ks[26], (DEPTH, D_MODEL, D_FF), D_MODEL ** -0.5),
        "ffn2_w3": nrm(ks[27], (DEPTH, D_MODEL, D_FF), D_MODEL ** -0.5),
        "ffn2_w2": nrm(ks[28], (DEPTH, D_FF, D_MODEL), D_FF ** -0.5),
        "final_norm": gain(ks[29], (D_MODEL,)),
    }


def _fwd_reference(x, meta_tokens, ffn1_norm, ffn1_w1, ffn1_w3, ffn1_w2, mix_norm, w_in, b_gate,
              conv_dw, conv_dw_b, conv_ln_g, conv_ln_b, conv_proj,
              ssm_lam_re, ssm_lam_im, ssm_log_dt, ssm_b_re, ssm_b_im, ssm_c_re, ssm_c_im,
              ssm_d, ssm_w_v, ssm_w_g, w_out, ffn2_norm, ffn2_w1, ffn2_w3, ffn2_w2, final_norm):
    bsz = x.shape[0]
    meta = jnp.broadcast_to(meta_tokens[None].astype(x.dtype), (bsz, N_META, D_MODEL))
    h = jnp.concatenate([meta, x], axis=1)
    for l in range(DEPTH):
        h = h + 0.5 * swiglu_ffn(rms_norm(h, ffn1_norm[l]), ffn1_w1[l], ffn1_w3[l], ffn1_w2[l])
        u = rms_norm(h, mix_norm[l])
        proj = u @ w_in[l]
        conv_in, ssm_in, gate_in = jnp.split(proj, [2 * D_CONV, 2 * D_CONV + D_SSM], axis=-1)
        y_conv = conformer_conv_branch(conv_in, conv_dw[l], conv_dw_b[l], conv_ln_g[l],
                                       conv_ln_b[l], conv_proj[l])
        y_ssm = s5_branch(ssm_in, ssm_lam_re[l], ssm_lam_im[l], ssm_log_dt[l], ssm_b_re[l],
                          ssm_b_im[l], ssm_c_re[l], ssm_c_im[l], ssm_d[l], ssm_w_v[l], ssm_w_g[l])
        g_conv, g_ssm = jnp.split(jax.nn.sigmoid(gate_in + b_gate[l]), 2, axis=-1)
        h = h + (g_conv * y_conv + g_ssm * y_ssm) @ w_out[l]
        h = h + 0.5 * swiglu_ffn(rms_norm(h, ffn2_norm[l]), ffn2_w1[l], ffn2_w3[l], ffn2_w2[l])
    h = rms_norm(h, final_norm)
    return h[:, N_META:]


import jax as _jax
import jax.numpy as _jnp

TWIN_FORMAT = 'train_step'
FWD_PARAMS = ['x', 'meta_tokens', 'ffn1_norm', 'ffn1_w1', 'ffn1_w3', 'ffn1_w2', 'mix_norm', 'w_in', 'b_gate', 'conv_dw', 'conv_dw_b', 'conv_ln_g', 'conv_ln_b', 'conv_proj', 'ssm_lam_re', 'ssm_lam_im', 'ssm_log_dt', 'ssm_b_re', 'ssm_b_im', 'ssm_c_re', 'ssm_c_im', 'ssm_d', 'ssm_w_v', 'ssm_w_g', 'w_out', 'ffn2_norm', 'ffn2_w1', 'ffn2_w3', 'ffn2_w2', 'final_norm']
TWIN_WEIGHTS = ['meta_tokens', 'ffn1_norm', 'ffn1_w1', 'ffn1_w3', 'ffn1_w2', 'mix_norm', 'w_in', 'b_gate', 'conv_dw', 'conv_dw_b', 'conv_ln_g', 'conv_ln_b', 'conv_proj', 'ssm_lam_re', 'ssm_lam_im', 'ssm_log_dt', 'ssm_b_re', 'ssm_b_im', 'ssm_c_re', 'ssm_c_im', 'ssm_d', 'ssm_w_v', 'ssm_w_g', 'w_out', 'ffn2_norm', 'ffn2_w1', 'ffn2_w3', 'ffn2_w2', 'final_norm']
TWIN_DIFF_INPUT = 'x'
TWIN_INPUTS = ['x', 'meta_tokens', 'ffn1_norm', 'ffn1_w1', 'ffn1_w3', 'ffn1_w2', 'mix_norm', 'w_in', 'b_gate', 'conv_dw', 'conv_dw_b', 'conv_ln_g', 'conv_ln_b', 'conv_proj', 'ssm_lam_re', 'ssm_lam_im', 'ssm_log_dt', 'ssm_b_re', 'ssm_b_im', 'ssm_c_re', 'ssm_c_im', 'ssm_d', 'ssm_w_v', 'ssm_w_g', 'w_out', 'ffn2_norm', 'ffn2_w1', 'ffn2_w3', 'ffn2_w2', 'final_norm', 'loss_target', 'm_meta_tokens', 'm_ffn1_norm', 'm_ffn1_w1', 'm_ffn1_w3', 'm_ffn1_w2', 'm_mix_norm', 'm_w_in', 'm_b_gate', 'm_conv_dw', 'm_conv_dw_b', 'm_conv_ln_g', 'm_conv_ln_b', 'm_conv_proj', 'm_ssm_lam_re', 'm_ssm_lam_im', 'm_ssm_log_dt', 'm_ssm_b_re', 'm_ssm_b_im', 'm_ssm_c_re', 'm_ssm_c_im', 'm_ssm_d', 'm_ssm_w_v', 'm_ssm_w_g', 'm_w_out', 'm_ffn2_norm', 'm_ffn2_w1', 'm_ffn2_w3', 'm_ffn2_w2', 'm_final_norm', 'v_meta_tokens', 'v_ffn1_norm', 'v_ffn1_w1', 'v_ffn1_w3', 'v_ffn1_w2', 'v_mix_norm', 'v_w_in', 'v_b_gate', 'v_conv_dw', 'v_conv_dw_b', 'v_conv_ln_g', 'v_conv_ln_b', 'v_conv_proj', 'v_ssm_lam_re', 'v_ssm_lam_im', 'v_ssm_log_dt', 'v_ssm_b_re', 'v_ssm_b_im', 'v_ssm_c_re', 'v_ssm_c_im', 'v_ssm_d', 'v_ssm_w_v', 'v_ssm_w_g', 'v_w_out', 'v_ffn2_norm', 'v_ffn2_w1', 'v_ffn2_w3', 'v_ffn2_w2', 'v_final_norm']
TWIN_OUTPUTS = ['loss', 'grad_x', 'grad_meta_tokens', 'grad_ffn1_norm', 'grad_ffn1_w1', 'grad_ffn1_w3', 'grad_ffn1_w2', 'grad_mix_norm', 'grad_w_in', 'grad_b_gate', 'grad_conv_dw', 'grad_conv_dw_b', 'grad_conv_ln_g', 'grad_conv_ln_b', 'grad_conv_proj', 'grad_ssm_lam_re', 'grad_ssm_lam_im', 'grad_ssm_log_dt', 'grad_ssm_b_re', 'grad_ssm_b_im', 'grad_ssm_c_re', 'grad_ssm_c_im', 'grad_ssm_d', 'grad_ssm_w_v', 'grad_ssm_w_g', 'grad_w_out', 'grad_ffn2_norm', 'grad_ffn2_w1', 'grad_ffn2_w3', 'grad_ffn2_w2', 'grad_final_norm', 'delta_meta_tokens', 'delta_ffn1_norm', 'delta_ffn1_w1', 'delta_ffn1_w3', 'delta_ffn1_w2', 'delta_mix_norm', 'delta_w_in', 'delta_b_gate', 'delta_conv_dw', 'delta_conv_dw_b', 'delta_conv_ln_g', 'delta_conv_ln_b', 'delta_conv_proj', 'delta_ssm_lam_re', 'delta_ssm_lam_im', 'delta_ssm_log_dt', 'delta_ssm_b_re', 'delta_ssm_b_im', 'delta_ssm_c_re', 'delta_ssm_c_im', 'delta_ssm_d', 'delta_ssm_w_v', 'delta_ssm_w_g', 'delta_w_out', 'delta_ffn2_norm', 'delta_ffn2_w1', 'delta_ffn2_w3', 'delta_ffn2_w2', 'delta_final_norm', 'new_m_meta_tokens', 'new_m_ffn1_norm', 'new_m_ffn1_w1', 'new_m_ffn1_w3', 'new_m_ffn1_w2', 'new_m_mix_norm', 'new_m_w_in', 'new_m_b_gate', 'new_m_conv_dw', 'new_m_conv_dw_b', 'new_m_conv_ln_g', 'new_m_conv_ln_b', 'new_m_conv_proj', 'new_m_ssm_lam_re', 'new_m_ssm_lam_im', 'new_m_ssm_log_dt', 'new_m_ssm_b_re', 'new_m_ssm_b_im', 'new_m_ssm_c_re', 'new_m_ssm_c_im', 'new_m_ssm_d', 'new_m_ssm_w_v', 'new_m_ssm_w_g', 'new_m_w_out', 'new_m_ffn2_norm', 'new_m_ffn2_w1', 'new_m_ffn2_w3', 'new_m_ffn2_w2', 'new_m_final_norm', 'new_v_meta_tokens', 'new_v_ffn1_norm', 'new_v_ffn1_w1', 'new_v_ffn1_w3', 'new_v_ffn1_w2', 'new_v_mix_norm', 'new_v_w_in', 'new_v_b_gate', 'new_v_conv_dw', 'new_v_conv_dw_b', 'new_v_conv_ln_g', 'new_v_conv_ln_b', 'new_v_conv_proj', 'new_v_ssm_lam_re', 'new_v_ssm_lam_im', 'new_v_ssm_log_dt', 'new_v_ssm_b_re', 'new_v_ssm_b_im', 'new_v_ssm_c_re', 'new_v_ssm_c_im', 'new_v_ssm_d', 'new_v_ssm_w_v', 'new_v_ssm_w_g', 'new_v_w_out', 'new_v_ffn2_norm', 'new_v_ffn2_w1', 'new_v_ffn2_w3', 'new_v_ffn2_w2', 'new_v_final_norm']
TWIN_LEAF_KINDS = {'loss': 'loss', 'grad_x': 'grad_x', 'grad_meta_tokens': 'grad_w', 'grad_ffn1_norm': 'grad_w', 'grad_ffn1_w1': 'grad_w', 'grad_ffn1_w3': 'grad_w', 'grad_ffn1_w2': 'grad_w', 'grad_mix_norm': 'grad_w', 'grad_w_in': 'grad_w', 'grad_b_gate': 'grad_w', 'grad_conv_dw': 'grad_w', 'grad_conv_dw_b': 'grad_w', 'grad_conv_ln_g': 'grad_w', 'grad_conv_ln_b': 'grad_w', 'grad_conv_proj': 'grad_w', 'grad_ssm_lam_re': 'grad_w', 'grad_ssm_lam_im': 'grad_w', 'grad_ssm_log_dt': 'grad_w', 'grad_ssm_b_re': 'grad_w', 'grad_ssm_b_im': 'grad_w', 'grad_ssm_c_re': 'grad_w', 'grad_ssm_c_im': 'grad_w', 'grad_ssm_d': 'grad_w', 'grad_ssm_w_v': 'grad_w', 'grad_ssm_w_g': 'grad_w', 'grad_w_out': 'grad_w', 'grad_ffn2_norm': 'grad_w', 'grad_ffn2_w1': 'grad_w', 'grad_ffn2_w3': 'grad_w', 'grad_ffn2_w2': 'grad_w', 'grad_final_norm': 'grad_w', 'delta_meta_tokens': 'delta_w', 'delta_ffn1_norm': 'delta_w', 'delta_ffn1_w1': 'delta_w', 'delta_ffn1_w3': 'delta_w', 'delta_ffn1_w2': 'delta_w', 'delta_mix_norm': 'delta_w', 'delta_w_in': 'delta_w', 'delta_b_gate': 'delta_w', 'delta_conv_dw': 'delta_w', 'delta_conv_dw_b': 'delta_w', 'delta_conv_ln_g': 'delta_w', 'delta_conv_ln_b': 'delta_w', 'delta_conv_proj': 'delta_w', 'delta_ssm_lam_re': 'delta_w', 'delta_ssm_lam_im': 'delta_w', 'delta_ssm_log_dt': 'delta_w', 'delta_ssm_b_re': 'delta_w', 'delta_ssm_b_im': 'delta_w', 'delta_ssm_c_re': 'delta_w', 'delta_ssm_c_im': 'delta_w', 'delta_ssm_d': 'delta_w', 'delta_ssm_w_v': 'delta_w', 'delta_ssm_w_g': 'delta_w', 'delta_w_out': 'delta_w', 'delta_ffn2_norm': 'delta_w', 'delta_ffn2_w1': 'delta_w', 'delta_ffn2_w3': 'delta_w', 'delta_ffn2_w2': 'delta_w', 'delta_final_norm': 'delta_w', 'new_m_meta_tokens': 'new_m', 'new_m_ffn1_norm': 'new_m', 'new_m_ffn1_w1': 'new_m', 'new_m_ffn1_w3': 'new_m', 'new_m_ffn1_w2': 'new_m', 'new_m_mix_norm': 'new_m', 'new_m_w_in': 'new_m', 'new_m_b_gate': 'new_m', 'new_m_conv_dw': 'new_m', 'new_m_conv_dw_b': 'new_m', 'new_m_conv_ln_g': 'new_m', 'new_m_conv_ln_b': 'new_m', 'new_m_conv_proj': 'new_m', 'new_m_ssm_lam_re': 'new_m', 'new_m_ssm_lam_im': 'new_m', 'new_m_ssm_log_dt': 'new_m', 'new_m_ssm_b_re': 'new_m', 'new_m_ssm_b_im': 'new_m', 'new_m_ssm_c_re': 'new_m', 'new_m_ssm_c_im': 'new_m', 'new_m_ssm_d': 'new_m', 'new_m_ssm_w_v': 'new_m', 'new_m_ssm_w_g': 'new_m', 'new_m_w_out': 'new_m', 'new_m_ffn2_norm': 'new_m', 'new_m_ffn2_w1': 'new_m', 'new_m_ffn2_w3': 'new_m', 'new_m_ffn2_w2': 'new_m', 'new_m_final_norm': 'new_m', 'new_v_meta_tokens': 'new_v', 'new_v_ffn1_norm': 'new_v', 'new_v_ffn1_w1': 'new_v', 'new_v_ffn1_w3': 'new_v', 'new_v_ffn1_w2': 'new_v', 'new_v_mix_norm': 'new_v', 'new_v_w_in': 'new_v', 'new_v_b_gate': 'new_v', 'new_v_conv_dw': 'new_v', 'new_v_conv_dw_b': 'new_v', 'new_v_conv_ln_g': 'new_v', 'new_v_conv_ln_b': 'new_v', 'new_v_conv_proj': 'new_v', 'new_v_ssm_lam_re': 'new_v', 'new_v_ssm_lam_im': 'new_v', 'new_v_ssm_log_dt': 'new_v', 'new_v_ssm_b_re': 'new_v', 'new_v_ssm_b_im': 'new_v', 'new_v_ssm_c_re': 'new_v', 'new_v_ssm_c_im': 'new_v', 'new_v_ssm_d': 'new_v', 'new_v_ssm_w_v': 'new_v', 'new_v_ssm_w_g': 'new_v', 'new_v_w_out': 'new_v', 'new_v_ffn2_norm': 'new_v', 'new_v_ffn2_w1': 'new_v', 'new_v_ffn2_w3': 'new_v', 'new_v_ffn2_w2': 'new_v', 'new_v_final_norm': 'new_v'}


def _forward(args):
    return _fwd_reference(*[args[k] for k in FWD_PARAMS])


def _output_shape():
    def fwd():
        inp = _fwd_setup_inputs(0)
        return _fwd_reference(*[inp[k] for k in FWD_PARAMS])
    out = _jax.eval_shape(fwd)
    return out.shape, out.dtype

N_MICROBATCH = 1
ADAM_LR = 0.001
ADAM_B1 = 0.9
ADAM_B2 = 0.999
ADAM_EPS = 1e-08
ADAM_WD = 0.01
ADAM_STEP = 10
PER_EXAMPLE_BATCH_AXIS = {'x': 0, 'loss_target': 0}
SHARED_INPUTS = []
_WEIGHT_DTYPES = {'meta_tokens': _jnp.float32, 'ffn1_norm': _jnp.float32, 'ffn1_w1': _jnp.float32, 'ffn1_w3': _jnp.float32, 'ffn1_w2': _jnp.float32, 'mix_norm': _jnp.float32, 'w_in': _jnp.float32, 'b_gate': _jnp.float32, 'conv_dw': _jnp.float32, 'conv_dw_b': _jnp.float32, 'conv_ln_g': _jnp.float32, 'conv_ln_b': _jnp.float32, 'conv_proj': _jnp.float32, 'ssm_lam_re': _jnp.float32, 'ssm_lam_im': _jnp.float32, 'ssm_log_dt': _jnp.float32, 'ssm_b_re': _jnp.float32, 'ssm_b_im': _jnp.float32, 'ssm_c_re': _jnp.float32, 'ssm_c_im': _jnp.float32, 'ssm_d': _jnp.float32, 'ssm_w_v': _jnp.float32, 'ssm_w_g': _jnp.float32, 'w_out': _jnp.float32, 'ffn2_norm': _jnp.float32, 'ffn2_w1': _jnp.float32, 'ffn2_w3': _jnp.float32, 'ffn2_w2': _jnp.float32, 'final_norm': _jnp.float32}
MOMENT_SCALE = {'meta_tokens': 3.620790e-03, 'ffn1_norm': 7.740499e-02, 'ffn1_w1': 3.326595e-02, 'ffn1_w3': 3.232232e-02, 'ffn1_w2': 5.346233e-02, 'mix_norm': 7.163866e-02, 'w_in': 3.898478e-02, 'b_gate': 1.744961e-02, 'conv_dw': 7.858337e-02, 'conv_dw_b': 1.676619e-01, 'conv_ln_g': 1.020981e-01, 'conv_ln_b': 9.079593e-02, 'conv_proj': 5.512203e-02, 'ssm_lam_re': 2.177051e-03, 'ssm_lam_im': 2.204682e-03, 'ssm_log_dt': 1.438557e+00, 'ssm_b_re': 1.543282e-03, 'ssm_b_im': 1.531397e-03, 'ssm_c_re': 3.052959e-03, 'ssm_c_im': 3.043748e-03, 'ssm_d': 5.100360e-02, 'ssm_w_v': 3.213898e-02, 'ssm_w_g': 9.359025e-03, 'w_out': 6.373468e-02, 'ffn2_norm': 6.834739e-02, 'ffn2_w1': 2.863885e-02, 'ffn2_w3': 2.780341e-02, 'ffn2_w2': 4.600337e-02, 'final_norm': 3.198526e+01}


def _to_microbatches(a, axis):
    t = _jnp.moveaxis(a, axis, 0)
    t = t.reshape((N_MICROBATCH, t.shape[0] // N_MICROBATCH) + t.shape[1:])
    return _jnp.moveaxis(t, 1, axis + 1)


def setup_inputs(seed: int = 0) -> dict:
    inp = _fwd_setup_inputs(seed)
    key = _jax.random.fold_in(_jax.random.key(seed), 7919)
    shape, _ = _output_shape()
    out = dict(inp)
    out["loss_target"] = _jax.random.normal(_jax.random.fold_in(key, 0), shape, _jnp.float32)
    for i, name in enumerate(TWIN_WEIGHTS):
        w = inp[name].astype(_jnp.float32)
        if MOMENT_SCALE is None:
            s = _jnp.sqrt(_jnp.mean(_jnp.square(w)) + 1e-30)
        else:
            s = MOMENT_SCALE[name]
        km, kv = _jax.random.split(_jax.random.fold_in(key, i + 1))
        out[name] = w
        out["m_" + name] = s * _jax.random.normal(km, w.shape, _jnp.float32)
        out["v_" + name] = (s * s) * _jax.random.uniform(kv, w.shape, _jnp.float32, 0.5, 1.5)
    if N_MICROBATCH > 1:
        for name, axis in PER_EXAMPLE_BATCH_AXIS.items():
            out[name] = _to_microbatches(out[name], axis)
    return {'x': out['x'], 'meta_tokens': out['meta_tokens'], 'ffn1_norm': out['ffn1_norm'], 'ffn1_w1': out['ffn1_w1'], 'ffn1_w3': out['ffn1_w3'], 'ffn1_w2': out['ffn1_w2'], 'mix_norm': out['mix_norm'], 'w_in': out['w_in'], 'b_gate': out['b_gate'], 'conv_dw': out['conv_dw'], 'conv_dw_b': out['conv_dw_b'], 'conv_ln_g': out['conv_ln_g'], 'conv_ln_b': out['conv_ln_b'], 'conv_proj': out['conv_proj'], 'ssm_lam_re': out['ssm_lam_re'], 'ssm_lam_im': out['ssm_lam_im'], 'ssm_log_dt': out['ssm_log_dt'], 'ssm_b_re': out['ssm_b_re'], 'ssm_b_im': out['ssm_b_im'], 'ssm_c_re': out['ssm_c_re'], 'ssm_c_im': out['ssm_c_im'], 'ssm_d': out['ssm_d'], 'ssm_w_v': out['ssm_w_v'], 'ssm_w_g': out['ssm_w_g'], 'w_out': out['w_out'], 'ffn2_norm': out['ffn2_norm'], 'ffn2_w1': out['ffn2_w1'], 'ffn2_w3': out['ffn2_w3'], 'ffn2_w2': out['ffn2_w2'], 'final_norm': out['final_norm'], 'loss_target': out['loss_target'], 'm_meta_tokens': out['m_meta_tokens'], 'm_ffn1_norm': out['m_ffn1_norm'], 'm_ffn1_w1': out['m_ffn1_w1'], 'm_ffn1_w3': out['m_ffn1_w3'], 'm_ffn1_w2': out['m_ffn1_w2'], 'm_mix_norm': out['m_mix_norm'], 'm_w_in': out['m_w_in'], 'm_b_gate': out['m_b_gate'], 'm_conv_dw': out['m_conv_dw'], 'm_conv_dw_b': out['m_conv_dw_b'], 'm_conv_ln_g': out['m_conv_ln_g'], 'm_conv_ln_b': out['m_conv_ln_b'], 'm_conv_proj': out['m_conv_proj'], 'm_ssm_lam_re': out['m_ssm_lam_re'], 'm_ssm_lam_im': out['m_ssm_lam_im'], 'm_ssm_log_dt': out['m_ssm_log_dt'], 'm_ssm_b_re': out['m_ssm_b_re'], 'm_ssm_b_im': out['m_ssm_b_im'], 'm_ssm_c_re': out['m_ssm_c_re'], 'm_ssm_c_im': out['m_ssm_c_im'], 'm_ssm_d': out['m_ssm_d'], 'm_ssm_w_v': out['m_ssm_w_v'], 'm_ssm_w_g': out['m_ssm_w_g'], 'm_w_out': out['m_w_out'], 'm_ffn2_norm': out['m_ffn2_norm'], 'm_ffn2_w1': out['m_ffn2_w1'], 'm_ffn2_w3': out['m_ffn2_w3'], 'm_ffn2_w2': out['m_ffn2_w2'], 'm_final_norm': out['m_final_norm'], 'v_meta_tokens': out['v_meta_tokens'], 'v_ffn1_norm': out['v_ffn1_norm'], 'v_ffn1_w1': out['v_ffn1_w1'], 'v_ffn1_w3': out['v_ffn1_w3'], 'v_ffn1_w2': out['v_ffn1_w2'], 'v_mix_norm': out['v_mix_norm'], 'v_w_in': out['v_w_in'], 'v_b_gate': out['v_b_gate'], 'v_conv_dw': out['v_conv_dw'], 'v_conv_dw_b': out['v_conv_dw_b'], 'v_conv_ln_g': out['v_conv_ln_g'], 'v_conv_ln_b': out['v_conv_ln_b'], 'v_conv_proj': out['v_conv_proj'], 'v_ssm_lam_re': out['v_ssm_lam_re'], 'v_ssm_lam_im': out['v_ssm_lam_im'], 'v_ssm_log_dt': out['v_ssm_log_dt'], 'v_ssm_b_re': out['v_ssm_b_re'], 'v_ssm_b_im': out['v_ssm_b_im'], 'v_ssm_c_re': out['v_ssm_c_re'], 'v_ssm_c_im': out['v_ssm_c_im'], 'v_ssm_d': out['v_ssm_d'], 'v_ssm_w_v': out['v_ssm_w_v'], 'v_ssm_w_g': out['v_ssm_w_g'], 'v_w_out': out['v_w_out'], 'v_ffn2_norm': out['v_ffn2_norm'], 'v_ffn2_w1': out['v_ffn2_w1'], 'v_ffn2_w3': out['v_ffn2_w3'], 'v_ffn2_w2': out['v_ffn2_w2'], 'v_final_norm': out['v_final_norm']}


def _loss(weights, diff, rest, loss_target):
    with _jax.named_scope("forward"):
        args = {**rest, TWIN_DIFF_INPUT: diff, **{k: w.astype(_WEIGHT_DTYPES[k]) for k, w in weights.items()}}
        y = _forward(args)
    with _jax.named_scope("loss_head"):
        err = _jnp.square(y.astype(_jnp.float32) - loss_target)
        return 0.5 * _jnp.sum(_jnp.mean(err, axis=-1)) if err.ndim else 0.5 * err


def _adamw(w, g, m, v):
    m = ADAM_B1 * m + (1.0 - ADAM_B1) * g
    v = ADAM_B2 * v + (1.0 - ADAM_B2) * _jnp.square(g)
    m_hat = m / (1.0 - ADAM_B1 ** ADAM_STEP)
    v_hat = v / (1.0 - ADAM_B2 ** ADAM_STEP)
    delta = -ADAM_LR * (m_hat / (_jnp.sqrt(v_hat) + ADAM_EPS) + ADAM_WD * w)
    return delta, m, v


def reference(x, meta_tokens, ffn1_norm, ffn1_w1, ffn1_w3, ffn1_w2, mix_norm, w_in, b_gate, conv_dw, conv_dw_b, conv_ln_g, conv_ln_b, conv_proj, ssm_lam_re, ssm_lam_im, ssm_log_dt, ssm_b_re, ssm_b_im, ssm_c_re, ssm_c_im, ssm_d, ssm_w_v, ssm_w_g, w_out, ffn2_norm, ffn2_w1, ffn2_w3, ffn2_w2, final_norm, loss_target, m_meta_tokens, m_ffn1_norm, m_ffn1_w1, m_ffn1_w3, m_ffn1_w2, m_mix_norm, m_w_in, m_b_gate, m_conv_dw, m_conv_dw_b, m_conv_ln_g, m_conv_ln_b, m_conv_proj, m_ssm_lam_re, m_ssm_lam_im, m_ssm_log_dt, m_ssm_b_re, m_ssm_b_im, m_ssm_c_re, m_ssm_c_im, m_ssm_d, m_ssm_w_v, m_ssm_w_g, m_w_out, m_ffn2_norm, m_ffn2_w1, m_ffn2_w3, m_ffn2_w2, m_final_norm, v_meta_tokens, v_ffn1_norm, v_ffn1_w1, v_ffn1_w3, v_ffn1_w2, v_mix_norm, v_w_in, v_b_gate, v_conv_dw, v_conv_dw_b, v_conv_ln_g, v_conv_ln_b, v_conv_proj, v_ssm_lam_re, v_ssm_lam_im, v_ssm_log_dt, v_ssm_b_re, v_ssm_b_im, v_ssm_c_re, v_ssm_c_im, v_ssm_d, v_ssm_w_v, v_ssm_w_g, v_w_out, v_ffn2_norm, v_ffn2_w1, v_ffn2_w3, v_ffn2_w2, v_final_norm):
    given = dict(x=x, meta_tokens=meta_tokens, ffn1_norm=ffn1_norm, ffn1_w1=ffn1_w1, ffn1_w3=ffn1_w3, ffn1_w2=ffn1_w2, mix_norm=mix_norm, w_in=w_in, b_gate=b_gate, conv_dw=conv_dw, conv_dw_b=conv_dw_b, conv_ln_g=conv_ln_g, conv_ln_b=conv_ln_b, conv_proj=conv_proj, ssm_lam_re=ssm_lam_re, ssm_lam_im=ssm_lam_im, ssm_log_dt=ssm_log_dt, ssm_b_re=ssm_b_re, ssm_b_im=ssm_b_im, ssm_c_re=ssm_c_re, ssm_c_im=ssm_c_im, ssm_d=ssm_d, ssm_w_v=ssm_w_v, ssm_w_g=ssm_w_g, w_out=w_out, ffn2_norm=ffn2_norm, ffn2_w1=ffn2_w1, ffn2_w3=ffn2_w3, ffn2_w2=ffn2_w2, final_norm=final_norm, loss_target=loss_target, m_meta_tokens=m_meta_tokens, m_ffn1_norm=m_ffn1_norm, m_ffn1_w1=m_ffn1_w1, m_ffn1_w3=m_ffn1_w3, m_ffn1_w2=m_ffn1_w2, m_mix_norm=m_mix_norm, m_w_in=m_w_in, m_b_gate=m_b_gate, m_conv_dw=m_conv_dw, m_conv_dw_b=m_conv_dw_b, m_conv_ln_g=m_conv_ln_g, m_conv_ln_b=m_conv_ln_b, m_conv_proj=m_conv_proj, m_ssm_lam_re=m_ssm_lam_re, m_ssm_lam_im=m_ssm_lam_im, m_ssm_log_dt=m_ssm_log_dt, m_ssm_b_re=m_ssm_b_re, m_ssm_b_im=m_ssm_b_im, m_ssm_c_re=m_ssm_c_re, m_ssm_c_im=m_ssm_c_im, m_ssm_d=m_ssm_d, m_ssm_w_v=m_ssm_w_v, m_ssm_w_g=m_ssm_w_g, m_w_out=m_w_out, m_ffn2_norm=m_ffn2_norm, m_ffn2_w1=m_ffn2_w1, m_ffn2_w3=m_ffn2_w3, m_ffn2_w2=m_ffn2_w2, m_final_norm=m_final_norm, v_meta_tokens=v_meta_tokens, v_ffn1_norm=v_ffn1_norm, v_ffn1_w1=v_ffn1_w1, v_ffn1_w3=v_ffn1_w3, v_ffn1_w2=v_ffn1_w2, v_mix_norm=v_mix_norm, v_w_in=v_w_in, v_b_gate=v_b_gate, v_conv_dw=v_conv_dw, v_conv_dw_b=v_conv_dw_b, v_conv_ln_g=v_conv_ln_g, v_conv_ln_b=v_conv_ln_b, v_conv_proj=v_conv_proj, v_ssm_lam_re=v_ssm_lam_re, v_ssm_lam_im=v_ssm_lam_im, v_ssm_log_dt=v_ssm_log_dt, v_ssm_b_re=v_ssm_b_re, v_ssm_b_im=v_ssm_b_im, v_ssm_c_re=v_ssm_c_re, v_ssm_c_im=v_ssm_c_im, v_ssm_d=v_ssm_d, v_ssm_w_v=v_ssm_w_v, v_ssm_w_g=v_ssm_w_g, v_w_out=v_w_out, v_ffn2_norm=v_ffn2_norm, v_ffn2_w1=v_ffn2_w1, v_ffn2_w3=v_ffn2_w3, v_ffn2_w2=v_ffn2_w2, v_final_norm=v_final_norm)
    weights = {n: given[n] for n in TWIN_WEIGHTS}
    shared = {n: given[n] for n in SHARED_INPUTS}
    per_example = {n: given[n] for n in ['x']}
    grad_fn = _jax.value_and_grad(_loss, argnums=(0, 1))

    def one_microbatch(ex, loss_target):
        ex = dict(ex)
        diff = ex.pop(TWIN_DIFF_INPUT)
        return grad_fn(weights, diff, {**shared, **ex}, loss_target)

    if N_MICROBATCH == 1:
        loss, (grad_w, grad_x) = one_microbatch(per_example, given["loss_target"])
    else:
        def body(carry, xs):
            loss_sum, grad_sum = carry
            l_k, (gw_k, gx_k) = one_microbatch(xs[0], xs[1])
            with _jax.named_scope("update"):
                return (loss_sum + l_k, _jax.tree.map(_jnp.add, grad_sum, gw_k)), gx_k

        init = (_jnp.zeros((), _jnp.float32), _jax.tree.map(_jnp.zeros_like, weights))
        (loss, grad_w), grad_x = _jax.lax.scan(body, init, (per_example, given["loss_target"]))
    with _jax.named_scope("update"):
        delta_w, new_m, new_v = {}, {}, {}
        for n in TWIN_WEIGHTS:
            delta_w[n], new_m[n], new_v[n] = _adamw(weights[n], grad_w[n], given["m_" + n], given["v_" + n])
    return (loss, grad_x, *[grad_w[n] for n in TWIN_WEIGHTS], *[delta_w[n] for n in TWIN_WEIGHTS],
            *[new_m[n] for n in TWIN_WEIGHTS], *[new_v[n] for n in TWIN_WEIGHTS])
```

```python
import math

import jax
import jax.numpy as jnp
from jax import lax
from jax.experimental import pallas as pl
from jax.experimental.pallas import tpu as pltpu

F32 = jnp.float32
BF16 = jnp.bfloat16

D = 1024
NSH = 4
F = 2816
FS = F // NSH
DC = 512
DS = 512
KW = 31
KWP = 32
NMETA = 16
FRONT = 128
G, P, H = 32, 64, 16
NST = G * P
NQ = 4
QS = NST // NQ
QU = DS // NQ
NSEG = 8
EPS = 1e-6
LR, B1, B2, AEPS, WD, STEP = 1e-3, 0.9, 0.999, 1e-8, 0.01, 10
VMEM_LIMIT = 58 * 1024 * 1024
MESH = pl.DeviceIdType.MESH
ANY = pl.BlockSpec(memory_space=pl.ANY)


def _params(*sem):
    return pltpu.CompilerParams(dimension_semantics=sem, vmem_limit_bytes=VMEM_LIMIT)


def _res(shape):
    nd = len(shape)
    return pl.BlockSpec(shape, lambda *_: (0,) * nd, pipeline_mode=pl.Buffered(1))


def _tile(n, cap, mult=16):
    best = None
    for t in range(mult, min(n, cap) + 1, mult):
        if n % t == 0:
            best = t
    assert best is not None, (n, cap, mult)
    return best


def _dot(a, b):
    return jnp.dot(a, b, preferred_element_type=F32)


def _dot_nt(a, b):
    return lax.dot_general(a, b, (((1,), (1,)), ((), ())), preferred_element_type=F32)


def _dot_tn(a, b):
    return lax.dot_general(a, b, (((0,), (0,)), ((), ())), preferred_element_type=F32)


def _sigmoid(x):
    return 1.0 / (1.0 + jnp.exp(-x))


_GC = math.sqrt(2.0 / math.pi)
_GA = 0.044715


def _gelu(x):
    return 0.5 * x * (1.0 + jnp.tanh(_GC * (x + _GA * x * x * x)))


def _gelu_grad(x):
    t = jnp.tanh(_GC * (x + _GA * x * x * x))
    return 0.5 * (1.0 + t) + 0.5 * x * (1.0 - t * t) * _GC * (1.0 + 3.0 * _GA * x * x)


def _rms(hv, g):
    r = lax.rsqrt(jnp.mean(hv * hv, axis=-1, keepdims=True) + EPS)
    return hv * r * g, r


def _rms_bwd(dn, hv, r, g):
    xh = hv * r
    dxh = dn * g
    return r * (dxh - xh * jnp.mean(dxh * xh, axis=-1, keepdims=True)), xh


def _acc_rows(ref, part, first):
    @pl.when(first)
    def _():
        ref[...] = part

    @pl.when(jnp.logical_not(first))
    def _():
        ref[...] += part


def _coords():
    return lax.axis_index("x"), lax.axis_index("y"), lax.axis_index("c")


def _flip(v, d):
    return 1 - v if d else v


def _run(local, remote):
    for cp in local + remote:
        cp.start()
    for cp in remote:
        cp.wait()
    for cp in local:
        cp.wait()


def _gather_chips(shards, name):
    n = len(shards)
    rel = ((1, 0), (0, 1), (1, 1))

    def body(*refs):
        ins, outs = refs[:n], refs[n:2 * n]
        send, recv, loc = refs[2 * n:]
        x, y, c = _coords()
        me = 2 * x + y
        local, remote = [], []
        for t in range(n):
            local.append(pltpu.make_async_copy(ins[t], outs[t].at[me], loc.at[t]))
            for k, (dx, dy) in enumerate(rel):
                remote.append(pltpu.make_async_remote_copy(
                    src_ref=ins[t], dst_ref=outs[t].at[me],
                    send_sem=send.at[3 * t + k], recv_sem=recv.at[3 * t + k],
                    device_id=(_flip(x, dx), _flip(y, dy), c), device_id_type=MESH))
        _run(local, remote)

    return pl.pallas_call(
        body, name=name,
        out_shape=[jax.ShapeDtypeStruct((NSH,) + s.shape, s.dtype) for s in shards],
        in_specs=[ANY] * n, out_specs=[ANY] * n,
        scratch_shapes=[pltpu.SemaphoreType.DMA((3 * n,)), pltpu.SemaphoreType.DMA((3 * n,)),
                        pltpu.SemaphoreType.DMA((n,))],
    )(*shards)


_REL7 = tuple((dx, dy, dc) for dx in (0, 1) for dy in (0, 1) for dc in (0, 1))[1:]


def _gather_all(a, name):
    def body(a_ref, o_ref, send, recv, loc):
        x, y, c = _coords()
        me = 4 * x + 2 * y + c
        local = [pltpu.make_async_copy(a_ref, o_ref.at[me], loc.at[0])]
        remote = [pltpu.make_async_remote_copy(
            src_ref=a_ref, dst_ref=o_ref.at[me], send_sem=send.at[k], recv_sem=recv.at[k],
            device_id=(_flip(x, dx), _flip(y, dy), _flip(c, dc)), device_id_type=MESH)
            for k, (dx, dy, dc) in enumerate(_REL7)]
        _run(local, remote)

    return pl.pallas_call(
        body, name=name,
        out_shape=jax.ShapeDtypeStruct((8,) + a.shape, a.dtype),
        in_specs=[ANY], out_specs=ANY,
        scratch_shapes=[pltpu.SemaphoreType.DMA((7,)), pltpu.SemaphoreType.DMA((7,)),
                        pltpu.SemaphoreType.DMA((1,))],
    )(a)


def _scatter_halves(grads, name):
    n = len(grads)

    def body(*refs):
        ins, outs = refs[:n], refs[n:2 * n]
        send, recv, loc = refs[2 * n:]
        x, y, c = _coords()
        me = 4 * x + 2 * y + c
        local, remote = [], []
        for t in range(n):
            half = grads[t].shape[1] // 2
            local.append(pltpu.make_async_copy(
                ins[t].at[2 * x + y, pl.ds(c * half, half)], outs[t].at[me], loc.at[t]))
            for k, (dx, dy, dc) in enumerate(_REL7):
                px, py, pc = _flip(x, dx), _flip(y, dy), _flip(c, dc)
                remote.append(pltpu.make_async_remote_copy(
                    src_ref=ins[t].at[2 * px + py, pl.ds(pc * half, half)], dst_ref=outs[t].at[me],
                    send_sem=send.at[7 * t + k], recv_sem=recv.at[7 * t + k],
                    device_id=(px, py, pc), device_id_type=MESH))
        _run(local, remote)

    return pl.pallas_call(
        body, name=name,
        out_shape=[jax.ShapeDtypeStruct((8, g.shape[1] // 2, g.shape[2]), g.dtype) for g in grads],
        in_specs=[ANY] * n, out_specs=[ANY] * n,
        scratch_shapes=[pltpu.SemaphoreType.DMA((7 * n,)), pltpu.SemaphoreType.DMA((7 * n,)),
                        pltpu.SemaphoreType.DMA((n,))],
    )(*grads)


def _swap_halves(halves, name):
    n = len(halves)

    def body(*refs):
        ins, outs = refs[:n], refs[n:2 * n]
        send, recv, loc = refs[2 * n:]
        x, y, c = _coords()
        local = [pltpu.make_async_copy(ins[t], outs[t].at[c], loc.at[t]) for t in range(n)]
        remote = [pltpu.make_async_remote_copy(
            src_ref=ins[t], dst_ref=outs[t].at[c], send_sem=send.at[t], recv_sem=recv.at[t],
            device_id=(x, y, 1 - c), device_id_type=MESH) for t in range(n)]
        _run(local, remote)

    return pl.pallas_call(
        body, name=name,
        out_shape=[jax.ShapeDtypeStruct((2,) + h.shape, h.dtype) for h in halves],
        in_specs=[ANY] * n, out_specs=[ANY] * n,
        scratch_shapes=[pltpu.SemaphoreType.DMA((n,)), pltpu.SemaphoreType.DMA((n,)),
                        pltpu.SemaphoreType.DMA((n,))],
    )(*halves)


def _sum8(r, name):
    _, R, C = r.shape
    tr = _tile(R, max(16, (1 << 19) // C))

    def body(r_ref, o_ref):
        acc = r_ref[0].astype(F32)
        for k in range(1, 8):
            acc = acc + r_ref[k].astype(F32)
        o_ref[...] = acc

    return pl.pallas_call(
        body, name=name, grid=(R // tr,),
        out_shape=jax.ShapeDtypeStruct((R, C), F32),
        in_specs=[pl.BlockSpec((8, tr, C), lambda i: (0, i, 0))],
        out_specs=pl.BlockSpec((tr, C), lambda i: (i, 0)),
        compiler_params=_params("parallel"),
    )(r)


def _adamw(w, g, m, v, name):
    R, C = w.shape
    tr = _tile(R, max(8, (1 << 18) // C), 8)
    c1 = 1.0 / (1.0 - B1 ** STEP)
    c2 = 1.0 / (1.0 - B2 ** STEP)

    def body(w_ref, g_ref, m_ref, v_ref, d_ref, nm_ref, nv_ref):
        gv = g_ref[...]
        nm = B1 * m_ref[...] + (1.0 - B1) * gv
        nv = B2 * v_ref[...] + (1.0 - B2) * gv * gv
        nm_ref[...] = nm
        nv_ref[...] = nv
        d_ref[...] = -LR * ((nm * c1) / (jnp.sqrt(nv * c2) + AEPS) + WD * w_ref[...])

    spec = pl.BlockSpec((tr, C), lambda i: (i, 0))
    return pl.pallas_call(
        body, name=name, grid=(R // tr,),
        out_shape=[jax.ShapeDtypeStruct((R, C), F32)] * 3,
        in_specs=[spec] * 4, out_specs=[spec] * 3,
        compiler_params=_params("parallel"),
    )(w, g, m, v)


def _ffn_fwd(h, g, w1, w3, w2, name):
    L = h.shape[0]
    tm = _tile(L, 528)

    def body(h_ref, g_ref, w1_ref, w3_ref, w2_ref, o_ref, a_ref, b_ref, n_s, acc_s):
        j = pl.program_id(1)

        @pl.when(j == 0)
        def _():
            hv = h_ref[...]
            n, _ = _rms(hv, g_ref[...])
            n_s[...] = n.astype(BF16)
            acc_s[...] = hv

        n = n_s[...]
        a = _dot(n, w1_ref[j])
        b = _dot(n, w3_ref[j])
        a_ref[0] = a.astype(BF16)
        b_ref[0] = b.astype(BF16)
        s = (a * _sigmoid(a) * b).astype(BF16)
        acc_s[...] += 0.5 * _dot(s, w2_ref[j])

        @pl.when(j == NSH - 1)
        def _():
            o_ref[...] = acc_s[...]

    row = pl.BlockSpec((tm, D), lambda i, j: (i, 0))
    hid = pl.BlockSpec((1, tm, FS), lambda i, j: (j, i, 0))
    return pl.pallas_call(
        body, name=name, grid=(L // tm, NSH),
        out_shape=[jax.ShapeDtypeStruct((L, D), F32),
                   jax.ShapeDtypeStruct((NSH, L, FS), BF16), jax.ShapeDtypeStruct((NSH, L, FS), BF16)],
        in_specs=[row, _res((1, D)), _res((NSH, D, FS)), _res((NSH, D, FS)), _res((NSH, FS, D))],
        out_specs=[row, hid, hid],
        scratch_shapes=[pltpu.VMEM((tm, D), BF16), pltpu.VMEM((tm, D), F32)],
        compiler_params=_params("arbitrary", "arbitrary"),
    )(h, g, w1, w3, w2)


def _ffn_bwd(h, g, dout, a, b, w1, w3, w2, name):
    L = h.shape[0]
    tm = _tile(L, 352)

    def body(h_ref, g_ref, do_ref, a_ref, b_ref, w1_ref, w3_ref, w2_ref,
             dh_ref, da_ref, db_ref, s_ref, n_ref, dg_ref, dob_s, dn_s):
        i, j = pl.program_id(0), pl.program_id(1)

        @pl.when(j == 0)
        def _():
            n, _ = _rms(h_ref[...], g_ref[...])
            n_ref[...] = n.astype(BF16)
            dob_s[...] = do_ref[...].astype(BF16)
            dn_s[...] = jnp.zeros_like(dn_s)

        av = a_ref[0].astype(F32)
        bv = b_ref[0].astype(F32)
        sig = _sigmoid(av)
        sa = av * sig
        ds = 0.5 * _dot_nt(dob_s[...], w2_ref[j])
        s_ref[0] = (sa * bv).astype(BF16)
        da = (ds * bv * sig * (1.0 + av * (1.0 - sig))).astype(BF16)
        db = (ds * sa).astype(BF16)
        da_ref[0] = da
        db_ref[0] = db
        dn_s[...] += _dot_nt(da, w1_ref[j]) + _dot_nt(db, w3_ref[j])

        @pl.when(j == NSH - 1)
        def _():
            hv = h_ref[...]
            gv = g_ref[...]
            r = lax.rsqrt(jnp.mean(hv * hv, axis=-1, keepdims=True) + EPS)
            dn = dn_s[...]
            dx, xh = _rms_bwd(dn, hv, r, gv)
            dh_ref[...] = do_ref[...] + dx
            _acc_rows(dg_ref, jnp.sum(dn * xh, axis=0, keepdims=True), i == 0)

    row = pl.BlockSpec((tm, D), lambda i, j: (i, 0))
    hid = pl.BlockSpec((1, tm, FS), lambda i, j: (j, i, 0))
    return pl.pallas_call(
        body, name=name, grid=(L // tm, NSH),
        out_shape=[jax.ShapeDtypeStruct((L, D), F32)]
        + [jax.ShapeDtypeStruct((NSH, L, FS), BF16)] * 3
        + [jax.ShapeDtypeStruct((L, D), BF16), jax.ShapeDtypeStruct((1, D), F32)],
        in_specs=[row, _res((1, D)), row, hid, hid,
                  _res((NSH, D, FS)), _res((NSH, D, FS)), _res((NSH, FS, D))],
        out_specs=[row, hid, hid, hid, row, pl.BlockSpec((1, D), lambda i, j: (0, 0))],
        scratch_shapes=[pltpu.VMEM((tm, D), BF16), pltpu.VMEM((tm, D), F32)],
        compiler_params=_params("arbitrary", "arbitrary"),
    )(h, g, dout, a, b, w1, w3, w2)


def _wgrad(xm, ym, name, scale=1.0):
    xs, ys = xm.ndim == 3, ym.ndim == 3
    assert not (xs and ys)
    L = xm.shape[-2]
    K, N = xm.shape[-1], ym.shape[-1]
    tl = _tile(L, 1056)
    nl = L // tl
    if xs or ys:
        tn, grid_n = N, NSH
    else:
        tn = _tile(N, 1024, 128)
        grid_n = N // tn

    def body(x_ref, y_ref, o_ref, acc_s):
        l = pl.program_id(1)
        xv = x_ref[0] if xs else x_ref[...]
        yv = y_ref[0] if ys else y_ref[...]
        part = _dot_tn(xv.astype(BF16), yv.astype(BF16))
        _acc_rows(acc_s, part, l == 0)

        @pl.when(l == nl - 1)
        def _():
            res = (acc_s[...] * scale).astype(BF16)
            if xs or ys:
                o_ref[0] = res
            else:
                o_ref[...] = res

    if xs:
        x_spec = pl.BlockSpec((1, tl, K), lambda n, l: (n, l, 0))
        y_spec = pl.BlockSpec((tl, N), lambda n, l: (l, 0))
        o_spec = pl.BlockSpec((1, K, N), lambda n, l: (n, 0, 0))
        o_shape = (NSH, K, N)
    elif ys:
        x_spec = pl.BlockSpec((tl, K), lambda n, l: (l, 0))
        y_spec = pl.BlockSpec((1, tl, N), lambda n, l: (n, l, 0))
        o_spec = pl.BlockSpec((1, K, N), lambda n, l: (n, 0, 0))
        o_shape = (NSH, K, N)
    else:
        x_spec = pl.BlockSpec((tl, K), lambda n, l: (l, 0))
        y_spec = pl.BlockSpec((tl, tn), lambda n, l: (l, n))
        o_spec = pl.BlockSpec((K, tn), lambda n, l: (0, n))
        o_shape = (K, N)
    return pl.pallas_call(
        body, name=name, grid=(grid_n, nl),
        out_shape=jax.ShapeDtypeStruct(o_shape, BF16),
        in_specs=[x_spec, y_spec], out_specs=o_spec,
        scratch_shapes=[pltpu.VMEM((K, tn), F32)],
        compiler_params=_params("parallel", "arbitrary"),
    )(xm, ym)


def _mix_in_fwd(h, g, w_in, b_gate, name):
    L = h.shape[0]
    tm = _tile(L, 528)

    def body(h_ref, g_ref, w_ref, bg_ref, vg_ref, uf_ref, gt_ref):
        u, _ = _rms(h_ref[...], g_ref[...])
        ub = u.astype(BF16)
        vg_ref[...] = _dot(ub, w_ref[:, 0:2 * DC]).astype(BF16)
        uf_ref[...] = _dot(ub, w_ref[:, 2 * DC:2 * DC + DS]).astype(BF16)
        gt_ref[...] = _sigmoid(_dot(ub, w_ref[:, 2 * DC + DS:]) + bg_ref[...]).astype(BF16)

    def row(n):
        return pl.BlockSpec((tm, n), lambda i: (i, 0))

    din = w_in.shape[1]
    return pl.pallas_call(
        body, name=name, grid=(L // tm,),
        out_shape=[jax.ShapeDtypeStruct((L, 2 * DC), BF16), jax.ShapeDtypeStruct((L, DS), BF16),
                   jax.ShapeDtypeStruct((L, 2 * D), BF16)],
        in_specs=[row(D), _res((1, D)), _res((D, din)), _res((1, 2 * D))],
        out_specs=[row(2 * DC), row(DS), row(2 * D)],
        compiler_params=_params("parallel"),
    )(h, g, w_in, b_gate)


def _mix_in_bwd(h, g, dres, dv, dgl, duf, dgate, w_in, name):
    L = h.shape[0]
    tm = _tile(L, 528)

    def body(h_ref, g_ref, dr_ref, dv_ref, dgl_ref, duf_ref, dgt_ref, w_ref, dh_ref, u_ref, dgm_ref):
        i = pl.program_id(0)
        hv = h_ref[...]
        gv = g_ref[...]
        u, r = _rms(hv, gv)
        u_ref[...] = u.astype(BF16)
        du = (_dot_nt(dv_ref[...], w_ref[:, 0:DC]) + _dot_nt(dgl_ref[...], w_ref[:, DC:2 * DC])
              + _dot_nt(duf_ref[...], w_ref[:, 2 * DC:2 * DC + DS])
              + _dot_nt(dgt_ref[...], w_ref[:, 2 * DC + DS:]))
        dx, xh = _rms_bwd(du, hv, r, gv)
        dh_ref[...] = dr_ref[...] + dx
        _acc_rows(dgm_ref, jnp.sum(du * xh, axis=0, keepdims=True), i == 0)

    def row(n):
        return pl.BlockSpec((tm, n), lambda i: (i, 0))

    din = w_in.shape[1]
    return pl.pallas_call(
        body, name=name, grid=(L // tm,),
        out_shape=[jax.ShapeDtypeStruct((L, D), F32), jax.ShapeDtypeStruct((L, D), BF16),
                   jax.ShapeDtypeStruct((1, D), F32)],
        in_specs=[row(D), _res((1, D)), row(D), row(DC), row(DC), row(DS), row(2 * D), _res((D, din))],
        out_specs=[row(D), row(D), pl.BlockSpec((1, D), lambda i: (0, 0))],
        compiler_params=_params("arbitrary"),
    )(h, g, dres, dv, dgl, duf, dgate, w_in)


def _conv_fwd(vg, dw, dwb, name):
    L = vg.shape[0]
    nc = DC // 128

    def body(v_ref, g_ref, dw_ref, dwb_ref, z_ref, zp_s):
        zp_s[0:KWP, :] = jnp.zeros((KWP, 128), F32)
        zp_s[KWP:, :] = v_ref[...].astype(F32) * _sigmoid(g_ref[...].astype(F32))
        acc = jnp.broadcast_to(dwb_ref[...], (L, 128))
        for k in range(KW):
            acc = acc + dw_ref[k:k + 1, :] * zp_s[pl.ds(k + 2, L), :]
        z_ref[...] = acc

    return pl.pallas_call(
        body, name=name, grid=(nc,),
        out_shape=jax.ShapeDtypeStruct((L, DC), F32),
        in_specs=[pl.BlockSpec((L, 128), lambda c: (0, c)), pl.BlockSpec((L, 128), lambda c: (0, nc + c)),
                  pl.BlockSpec((KWP, 128), lambda c: (0, c)), pl.BlockSpec((1, 128), lambda c: (0, c))],
        out_specs=pl.BlockSpec((L, 128), lambda c: (0, c)),
        scratch_shapes=[pltpu.VMEM((L + KWP, 128), F32)],
        compiler_params=_params("parallel"),
    )(vg, vg, dw, dwb)


def _conv_bwd(dz1, vg, dw, name):
    L = vg.shape[0]
    nc = DC // 128

    def body(dz_ref, v_ref, g_ref, dw_ref, dv_ref, dg_ref, ddw_ref, ddwb_ref, zp_s, dzp_s):
        vv = v_ref[...].astype(F32)
        sg = _sigmoid(g_ref[...].astype(F32))
        zp_s[0:KWP, :] = jnp.zeros((KWP, 128), F32)
        zp_s[KWP:, :] = vv * sg
        dz = dz_ref[...]
        dzp_s[0:L, :] = dz
        dzp_s[L:, :] = jnp.zeros((KWP, 128), F32)
        ddwb_ref[...] = jnp.sum(dz, axis=0, keepdims=True)
        acc = jnp.zeros((L, 128), F32)
        for k in range(KW):
            acc = acc + dw_ref[k:k + 1, :] * dzp_s[pl.ds(KW - 1 - k, L), :]
            ddw_ref[k:k + 1, :] = jnp.sum(dz * zp_s[pl.ds(k + 2, L), :], axis=0, keepdims=True)
        ddw_ref[KW:KWP, :] = jnp.zeros((KWP - KW, 128), F32)
        dv_ref[...] = (acc * sg).astype(BF16)
        dg_ref[...] = (acc * vv * sg * (1.0 - sg)).astype(BF16)

    col = pl.BlockSpec((L, 128), lambda c: (0, c))
    return pl.pallas_call(
        body, name=name, grid=(nc,),
        out_shape=[jax.ShapeDtypeStruct((L, DC), BF16), jax.ShapeDtypeStruct((L, DC), BF16),
                   jax.ShapeDtypeStruct((KWP, DC), F32), jax.ShapeDtypeStruct((1, DC), F32)],
        in_specs=[col, col, pl.BlockSpec((L, 128), lambda c: (0, nc + c)),
                  pl.BlockSpec((KWP, 128), lambda c: (0, c))],
        out_specs=[col, col, pl.BlockSpec((KWP, 128), lambda c: (0, c)), pl.BlockSpec((1, 128), lambda c: (0, c))],
        scratch_shapes=[pltpu.VMEM((L + KWP, 128), F32), pltpu.VMEM((L + KWP, 128), F32)],
        compiler_params=_params("parallel"),
    )(dz1, vg, vg, dw)


NLB = QS // 128


def _lb_store(ref, rows, val):
    for cb in range(NLB):
        ref[cb, rows, :] = val[:, cb * 128:(cb + 1) * 128]


def _lb_load(ref, rows):
    return jnp.concatenate([ref[cb, rows, :] for cb in range(NLB)], axis=1)


def _scan(xr_ref, xi_ref, base, T, ar, ai, atr, ati, reverse, sr_ref=None, si_ref=None):
    W = ar.shape[1]
    ar, ai = jnp.broadcast_to(ar, (NSEG, W)), jnp.broadcast_to(ai, (NSEG, W))
    atr, ati = jnp.broadcast_to(atr, (NSEG, W)), jnp.broadcast_to(ati, (NSEG, W))
    zero = jnp.zeros((NSEG, W), F32)

    def rows(t):
        tt = T - 1 - t if reverse else t
        return pl.ds(base + tt, NSEG, stride=T)

    def step(t, carry):
        sr, si = carry
        idx = rows(t)
        return (ar * sr - ai * si + _lb_load(xr_ref, idx), ar * si + ai * sr + _lb_load(xi_ref, idx))

    er, ei = lax.fori_loop(0, T, step, (zero, zero))
    seg = lax.broadcasted_iota(jnp.int32, (NSEG, W), 0)
    edge = seg == (NSEG - 1 if reverse else 0)
    cr, ci = zero, zero
    for _ in range(NSEG - 1):
        nr = atr * cr - ati * ci + er
        ni = atr * ci + ati * cr + ei
        shift = NSEG - 1 if reverse else 1
        cr = jnp.where(edge, 0.0, pltpu.roll(nr, shift, 0))
        ci = jnp.where(edge, 0.0, pltpu.roll(ni, shift, 0))

    if sr_ref is None:
        def step2(t, carry):
            sr, si = step(t, carry)
            idx = rows(t)
            _lb_store(xr_ref, idx, sr)
            _lb_store(xi_ref, idx, si)
            return sr, si

        lax.fori_loop(0, T, step2, (cr, ci))
        return None

    def step3(t, carry):
        sr, si, qr, qi = carry
        sr, si = step(t, (sr, si))
        idx = rows(t)
        _lb_store(xr_ref, idx, sr)
        _lb_store(xi_ref, idx, si)
        prev = pl.ds(NSEG - 1 + T - 1 - t, NSEG, stride=T)
        pr, pi = _lb_load(sr_ref, prev), _lb_load(si_ref, prev)
        return sr, si, qr + sr * pr + si * pi, qi + si * pr - sr * pi

    _, _, qr, qi = lax.fori_loop(0, T, step3, (cr, ci, zero, zero))
    return jnp.sum(qr, axis=0, keepdims=True), jnp.sum(qi, axis=0, keepdims=True)


def _ssm_fwd(uf, bre, bim, cre, cim, lamp, dsk, name):
    L = uf.shape[0]
    T = L // NSEG

    def body(u_ref, bre_ref, bim_ref, cre_ref, cim_ref, lam_ref, d_ref, y_ref, sr_s, si_s):
        for k in range(NSEG):
            sl = slice(k * T, (k + 1) * T)
            uk = u_ref[sl, :]
            _lb_store(sr_s, sl, _dot(uk, bre_ref[...]))
            _lb_store(si_s, sl, _dot(uk, bim_ref[...]))
        _scan(sr_s, si_s, 0, T, lam_ref[0:1, :], lam_ref[1:2, :], lam_ref[2:3, :], lam_ref[3:4, :], False)
        for k in range(NSEG):
            sl = slice(k * T, (k + 1) * T)
            y_ref[sl, :] = (_dot(_lb_load(sr_s, sl).astype(BF16), cre_ref[...])
                            - _dot(_lb_load(si_s, sl).astype(BF16), cim_ref[...])
                            + d_ref[...] * u_ref[sl, :].astype(F32))

    return pl.pallas_call(
        body, name=name, grid=(NQ,),
        out_shape=jax.ShapeDtypeStruct((L, DS), F32),
        in_specs=[pl.BlockSpec((L, QU), lambda q: (0, q)),
                  pl.BlockSpec((QU, QS), lambda q: (q, q)), pl.BlockSpec((QU, QS), lambda q: (q, q)),
                  pl.BlockSpec((QS, QU), lambda q: (q, q)), pl.BlockSpec((QS, QU), lambda q: (q, q)),
                  pl.BlockSpec((8, QS), lambda q: (0, q)), pl.BlockSpec((1, QU), lambda q: (0, q))],
        out_specs=pl.BlockSpec((L, QU), lambda q: (0, q)),
        scratch_shapes=[pltpu.VMEM((NLB, L, 128), F32), pltpu.VMEM((NLB, L, 128), F32)],
        compiler_params=_params("parallel"),
    )(uf, bre, bim, cre, cim, lamp, dsk)


def _ssm_bwd(uf, dyss, bre, bim, cre, cim, lamp, dsk, name):
    L = uf.shape[0]
    T = L // NSEG

    def body(u_ref, dy_ref, bre_ref, bim_ref, cre_ref, cim_ref, lam_ref, d_ref,
             du_ref, dbre_ref, dbim_ref, dcre_ref, dcim_ref, dlam_ref, dd_ref, sr_s, si_s, gr_s, gi_s):
        _lb_store(sr_s, slice(0, NSEG), jnp.zeros((NSEG, QS), F32))
        _lb_store(si_s, slice(0, NSEG), jnp.zeros((NSEG, QS), F32))
        for k in range(NSEG):
            sl = slice(k * T, (k + 1) * T)
            ss = slice(NSEG + k * T, NSEG + (k + 1) * T)
            uk = u_ref[sl, :]
            dyk = dy_ref[sl, :].astype(BF16)
            _lb_store(sr_s, ss, _dot(uk, bre_ref[...]))
            _lb_store(si_s, ss, _dot(uk, bim_ref[...]))
            _lb_store(gr_s, sl, _dot_nt(dyk, cre_ref[...]))
            _lb_store(gi_s, sl, -_dot_nt(dyk, cim_ref[...]))
        ar, ai, atr, ati = lam_ref[0:1, :], lam_ref[1:2, :], lam_ref[2:3, :], lam_ref[3:4, :]
        _scan(sr_s, si_s, NSEG, T, ar, ai, atr, ati, False)
        qr, qi = _scan(gr_s, gi_s, 0, T, ar, -ai, atr, -ati, True, sr_s, si_s)
        dlam_ref[0] = jnp.concatenate([qr, qi, jnp.zeros((6, QS), F32)], axis=0)
        dbre = jnp.zeros((QU, QS), F32)
        dbim = jnp.zeros((QU, QS), F32)
        dcre = jnp.zeros((QS, QU), F32)
        dcim = jnp.zeros((QS, QU), F32)
        dd = jnp.zeros((1, QU), F32)
        for k in range(NSEG):
            sl = slice(k * T, (k + 1) * T)
            ss = slice(NSEG + k * T, NSEG + (k + 1) * T)
            uk = u_ref[sl, :]
            dyk = dy_ref[sl, :]
            dyb = dyk.astype(BF16)
            grb = _lb_load(gr_s, sl).astype(BF16)
            gib = _lb_load(gi_s, sl).astype(BF16)
            du_ref[sl, :] = (_dot_nt(grb, bre_ref[...]) + _dot_nt(gib, bim_ref[...])
                             + dyk * d_ref[...]).astype(BF16)
            dbre = dbre + _dot_tn(uk, grb)
            dbim = dbim + _dot_tn(uk, gib)
            dcre = dcre + _dot_tn(_lb_load(sr_s, ss).astype(BF16), dyb)
            dcim = dcim - _dot_tn(_lb_load(si_s, ss).astype(BF16), dyb)
            dd = dd + jnp.sum(dyk * uk.astype(F32), axis=0, keepdims=True)
        dbre_ref[0] = dbre
        dbim_ref[0] = dbim
        dcre_ref[0] = dcre
        dcim_ref[0] = dcim
        dd_ref[...] = dd

    col = pl.BlockSpec((L, QU), lambda q: (0, q))
    bsp = pl.BlockSpec((QU, QS), lambda q: (q, q))
    csp = pl.BlockSpec((QS, QU), lambda q: (q, q))
    return pl.pallas_call(
        body, name=name, grid=(NQ,),
        out_shape=[jax.ShapeDtypeStruct((L, DS), BF16),
                   jax.ShapeDtypeStruct((NQ, QU, QS), F32), jax.ShapeDtypeStruct((NQ, QU, QS), F32),
                   jax.ShapeDtypeStruct((NQ, QS, QU), F32), jax.ShapeDtypeStruct((NQ, QS, QU), F32),
                   jax.ShapeDtypeStruct((NQ, 8, QS), F32), jax.ShapeDtypeStruct((1, DS), F32)],
        in_specs=[col, col, bsp, bsp, csp, csp,
                  pl.BlockSpec((8, QS), lambda q: (0, q)), pl.BlockSpec((1, QU), lambda q: (0, q))],
        out_specs=[col,
                   pl.BlockSpec((1, QU, QS), lambda q: (q, 0, 0)), pl.BlockSpec((1, QU, QS), lambda q: (q, 0, 0)),
                   pl.BlockSpec((1, QS, QU), lambda q: (q, 0, 0)), pl.BlockSpec((1, QS, QU), lambda q: (q, 0, 0)),
                   pl.BlockSpec((1, 8, QS), lambda q: (q, 0, 0)), pl.BlockSpec((1, QU), lambda q: (0, q))],
        scratch_shapes=[pltpu.VMEM((NLB, L + NSEG, 128), F32), pltpu.VMEM((NLB, L + NSEG, 128), F32),
                        pltpu.VMEM((NLB, L, 128), F32), pltpu.VMEM((NLB, L, 128), F32)],
        compiler_params=_params("parallel"),
    )(uf, dyss, bre, bim, cre, cim, lamp, dsk)


def _branches(z1_ref, yss_ref, gt_ref, lng_ref, lnb_ref, wp_ref, wv_ref, wg_ref):
    zf = z1_ref[...]
    mu = jnp.mean(zf, axis=-1, keepdims=True)
    zc = zf - mu
    rstd = lax.rsqrt(jnp.mean(zc * zc, axis=-1, keepdims=True) + EPS)
    zn = zc * rstd
    z2 = zn * lng_ref[...] + lnb_ref[...]
    sz = _sigmoid(z2)
    z3 = (z2 * sz).astype(BF16)
    y_conv = _dot(z3, wp_ref[...])
    yss = yss_ref[...]
    yg = _gelu(yss).astype(BF16)
    sv = _dot(yg, wv_ref[...])
    sig = _sigmoid(_dot(yg, wg_ref[...]))
    y_ssm = sv * sig
    gc = gt_ref[:, 0:D].astype(F32)
    gs = gt_ref[:, D:2 * D].astype(F32)
    m = gc * y_conv + gs * y_ssm
    return dict(rstd=rstd, zn=zn, z2=z2, sz=sz, z3=z3, y_conv=y_conv, yss=yss, yg=yg, sv=sv, sig=sig,
                y_ssm=y_ssm, gc=gc, gs=gs, m=m)


def _merge_fwd(h, z1, yss, gate, lng, lnb, wp, wv, wg, wo, name):
    L = h.shape[0]
    tm = _tile(L, 528)

    def body(h_ref, z1_ref, yss_ref, gt_ref, lng_ref, lnb_ref, wp_ref, wv_ref, wg_ref, wo_ref, o_ref):
        f = _branches(z1_ref, yss_ref, gt_ref, lng_ref, lnb_ref, wp_ref, wv_ref, wg_ref)
        o_ref[...] = h_ref[...] + _dot(f["m"].astype(BF16), wo_ref[...])

    def row(n):
        return pl.BlockSpec((tm, n), lambda i: (i, 0))

    return pl.pallas_call(
        body, name=name, grid=(L // tm,),
        out_shape=jax.ShapeDtypeStruct((L, D), F32),
        in_specs=[row(D), row(DC), row(DS), row(2 * D), _res((1, DC)), _res((1, DC)),
                  _res((DC, D)), _res((DS, D)), _res((DS, D)), _res((D, D))],
        out_specs=row(D),
        compiler_params=_params("parallel"),
    )(h, z1, yss, gate, lng, lnb, wp, wv, wg, wo)


def _merge_bwd(dh, z1, yss, gate, lng, lnb, wp, wv, wg, wo, name):
    L = dh.shape[0]
    tm = _tile(L, 352)

    def body(dh_ref, z1_ref, yss_ref, gt_ref, lng_ref, lnb_ref, wp_ref, wv_ref, wg_ref, wo_ref,
             m_ref, dgt_ref, dyc_ref, z3_ref, dz1_ref, yg_ref, dsv_ref, dsg_ref, dyss_ref,
             dbg_ref, dlng_ref, dlnb_ref):
        i = pl.program_id(0)
        f = _branches(z1_ref, yss_ref, gt_ref, lng_ref, lnb_ref, wp_ref, wv_ref, wg_ref)
        gc, gs, sig, sv = f["gc"], f["gs"], f["sig"], f["sv"]
        m_ref[...] = f["m"].astype(BF16)
        z3_ref[...] = f["z3"]
        yg_ref[...] = f["yg"]
        dm = _dot_nt(dh_ref[...].astype(BF16), wo_ref[...])
        dgc = (dm * f["y_conv"] * gc * (1.0 - gc)).astype(BF16)
        dgs = (dm * f["y_ssm"] * gs * (1.0 - gs)).astype(BF16)
        dgt_ref[:, 0:D] = dgc
        dgt_ref[:, D:2 * D] = dgs
        part = jnp.concatenate([jnp.sum(dgc.astype(F32), axis=0, keepdims=True),
                                jnp.sum(dgs.astype(F32), axis=0, keepdims=True)], axis=1)
        _acc_rows(dbg_ref, part, i == 0)
        dyc = (dm * gc).astype(BF16)
        dyc_ref[...] = dyc
        dys = dm * gs
        dsv = (dys * sig).astype(BF16)
        dsg = (dys * sv * sig * (1.0 - sig)).astype(BF16)
        dsv_ref[...] = dsv
        dsg_ref[...] = dsg
        dyg = _dot_nt(dsv, wv_ref[...]) + _dot_nt(dsg, wg_ref[...])
        dyss_ref[...] = dyg * _gelu_grad(f["yss"])
        dz3 = _dot_nt(dyc, wp_ref[...])
        z2, sz, zn = f["z2"], f["sz"], f["zn"]
        dz2 = dz3 * sz * (1.0 + z2 * (1.0 - sz))
        _acc_rows(dlng_ref, jnp.sum(dz2 * zn, axis=0, keepdims=True), i == 0)
        _acc_rows(dlnb_ref, jnp.sum(dz2, axis=0, keepdims=True), i == 0)
        dzn = dz2 * lng_ref[...]
        dz1_ref[...] = f["rstd"] * (dzn - jnp.mean(dzn, axis=-1, keepdims=True)
                                    - zn * jnp.mean(dzn * zn, axis=-1, keepdims=True))

    def row(n):
        return pl.BlockSpec((tm, n), lambda i: (i, 0))

    def tot(n):
        return pl.BlockSpec((1, n), lambda i: (0, 0))

    return pl.pallas_call(
        body, name=name, grid=(L // tm,),
        out_shape=[jax.ShapeDtypeStruct((L, D), BF16), jax.ShapeDtypeStruct((L, 2 * D), BF16),
                   jax.ShapeDtypeStruct((L, D), BF16), jax.ShapeDtypeStruct((L, DC), BF16),
                   jax.ShapeDtypeStruct((L, DC), F32), jax.ShapeDtypeStruct((L, DS), BF16),
                   jax.ShapeDtypeStruct((L, D), BF16), jax.ShapeDtypeStruct((L, D), BF16),
                   jax.ShapeDtypeStruct((L, DS), F32),
                   jax.ShapeDtypeStruct((1, 2 * D), F32), jax.ShapeDtypeStruct((1, DC), F32),
                   jax.ShapeDtypeStruct((1, DC), F32)],
        in_specs=[row(D), row(DC), row(DS), row(2 * D), _res((1, DC)), _res((1, DC)),
                  _res((DC, D)), _res((DS, D)), _res((DS, D)), _res((D, D))],
        out_specs=[row(D), row(2 * D), row(D), row(DC), row(DC), row(DS), row(D), row(D), row(DS),
                   tot(2 * D), tot(DC), tot(DC)],
        compiler_params=_params("arbitrary"),
    )(dh, z1, yss, gate, lng, lnb, wp, wv, wg, wo)


def _final(h, g, tgt, name):
    L = h.shape[0]
    tm = _tile(L, 528)

    def body(h_ref, g_ref, t_ref, dh_ref, loss_ref, dg_ref):
        i = pl.program_id(0)
        hv = h_ref[...]
        gv = g_ref[...]
        y, r = _rms(hv, gv)
        row = i * tm + lax.broadcasted_iota(jnp.int32, (tm, 1), 0)
        e = jnp.where(row >= FRONT, y - t_ref[...], 0.0)
        dy = e * (1.0 / D)
        part = 0.5 * jnp.sum(jnp.sum(e * dy, axis=1, keepdims=True), axis=0, keepdims=True)
        dx, xh = _rms_bwd(dy, hv, r, gv)
        dh_ref[...] = dx
        _acc_rows(loss_ref, part, i == 0)
        _acc_rows(dg_ref, jnp.sum(dy * xh, axis=0, keepdims=True), i == 0)

    row = pl.BlockSpec((tm, D), lambda i: (i, 0))
    return pl.pallas_call(
        body, name=name, grid=(L // tm,),
        out_shape=[jax.ShapeDtypeStruct((L, D), F32), jax.ShapeDtypeStruct((1, 1), F32),
                   jax.ShapeDtypeStruct((1, D), F32)],
        in_specs=[row, _res((1, D)), row],
        out_specs=[row, pl.BlockSpec((1, 1), lambda i: (0, 0)), pl.BlockSpec((1, D), lambda i: (0, 0))],
        compiler_params=_params("arbitrary"),
    )(h, g, tgt)


def _ssm_disc(lam_re, lam_im, log_dt, b_re, b_im):
    lam = lax.complex(lam_re, lam_im)
    dt = jnp.exp(log_dt)[:, None]
    lam_bar = jnp.exp(lam * dt)
    bbar = ((lam_bar - 1.0) / lam)[..., None] * lax.complex(b_re, b_im)
    return jnp.real(lam_bar), jnp.imag(lam_bar), jnp.real(bbar), jnp.imag(bbar)


def _bdiag_in(m):
    return jnp.einsum("gph,gk->ghkp", m, jnp.eye(G, dtype=m.dtype)).reshape(G * H, G * P)


def _bdiag_out(m):
    return jnp.einsum("ghp,gk->gpkh", m, jnp.eye(G, dtype=m.dtype)).reshape(G * P, G * H)


def _diag_blocks(m4):
    return jnp.einsum("qiaib->qiab", m4).reshape(G, m4.shape[2], m4.shape[4])


def _pack(parts, rows_mult=8):
    flat = jnp.concatenate([p.reshape(-1).astype(F32) for p in parts])
    n = flat.shape[0]
    tot = -(-n // (128 * rows_mult)) * (128 * rows_mult)
    return jnp.pad(flat, (0, tot - n)).reshape(tot // 128, 128)


def _unpack(buf, shapes):
    flat = buf.reshape(-1)
    out, o = [], 0
    for s in shapes:
        n = math.prod(s)
        out.append(flat[o:o + n].reshape(s))
        o += n
    return out


def kernel(x, meta_tokens, ffn1_norm, ffn1_w1, ffn1_w3, ffn1_w2, mix_norm, w_in, b_gate, conv_dw, conv_dw_b, conv_ln_g, conv_ln_b, conv_proj, ssm_lam_re, ssm_lam_im, ssm_log_dt, ssm_b_re, ssm_b_im, ssm_c_re, ssm_c_im, ssm_d, ssm_w_v, ssm_w_g, w_out, ffn2_norm, ffn2_w1, ffn2_w3, ffn2_w2, final_norm, loss_target, m_meta_tokens, m_ffn1_norm, m_ffn1_w1, m_ffn1_w3, m_ffn1_w2, m_mix_norm, m_w_in, m_b_gate, m_conv_dw, m_conv_dw_b, m_conv_ln_g, m_conv_ln_b, m_conv_proj, m_ssm_lam_re, m_ssm_lam_im, m_ssm_log_dt, m_ssm_b_re, m_ssm_b_im, m_ssm_c_re, m_ssm_c_im, m_ssm_d, m_ssm_w_v, m_ssm_w_g, m_w_out, m_ffn2_norm, m_ffn2_w1, m_ffn2_w3, m_ffn2_w2, m_final_norm, v_meta_tokens, v_ffn1_norm, v_ffn1_w1, v_ffn1_w3, v_ffn1_w2, v_mix_norm, v_w_in, v_b_gate, v_conv_dw, v_conv_dw_b, v_conv_ln_g, v_conv_ln_b, v_conv_proj, v_ssm_lam_re, v_ssm_lam_im, v_ssm_log_dt, v_ssm_b_re, v_ssm_b_im, v_ssm_c_re, v_ssm_c_im, v_ssm_d, v_ssm_w_v, v_ssm_w_g, v_w_out, v_ffn2_norm, v_ffn2_w1, v_ffn2_w3, v_ffn2_w2, v_final_norm):
    args = dict(locals())
    names = ["meta_tokens", "ffn1_norm", "ffn1_w1", "ffn1_w3", "ffn1_w2", "mix_norm", "w_in", "b_gate",
             "conv_dw", "conv_dw_b", "conv_ln_g", "conv_ln_b", "conv_proj", "ssm_lam_re", "ssm_lam_im",
             "ssm_log_dt", "ssm_b_re", "ssm_b_im", "ssm_c_re", "ssm_c_im", "ssm_d", "ssm_w_v", "ssm_w_g",
             "w_out", "ffn2_norm", "ffn2_w1", "ffn2_w3", "ffn2_w2", "final_norm"]
    big = ["ffn1_w1", "ffn1_w3", "ffn1_w2", "w_in", "conv_proj", "ssm_w_v", "ssm_w_g", "w_out",
           "ffn2_w1", "ffn2_w3", "ffn2_w2"]
    small = [n for n in names if n not in big]

    xs = x[0]
    S = xs.shape[0]
    L = FRONT + S
    T = L // NSEG
    jx, jy = lax.axis_index("x"), lax.axis_index("y")
    chip = 2 * jx + jy

    sm = _gather_all(_pack([meta_tokens, conv_dw[0]]), "gather_small")[0::2].reshape(NSH, -1)
    nmt = NMETA * (D // NSH)
    ndw = KW * (DC // NSH)
    meta_full = sm[:, :nmt].reshape(NSH, NMETA, D // NSH).transpose(1, 0, 2).reshape(NMETA, D)
    dw_full = sm[:, nmt:nmt + ndw].reshape(NSH, KW, DC // NSH).transpose(1, 0, 2).reshape(KW, DC)
    dw_pad = jnp.pad(dw_full, ((0, KWP - KW), (0, 0)))
    gw = _gather_chips([args[n][0].astype(BF16) for n in big], "gather_weights")
    gw = dict(zip(big, gw))

    def cols(w):
        return w.transpose(1, 0, 2).reshape(w.shape[1], -1)

    w_in_f = cols(gw["w_in"])
    wp_f, wv_f, wg_f = cols(gw["conv_proj"]), cols(gw["ssm_w_v"]), cols(gw["ssm_w_g"])
    wo_f = gw["w_out"].reshape(D, D)

    disc_in = (ssm_lam_re[0], ssm_lam_im[0], ssm_log_dt[0], ssm_b_re[0], ssm_b_im[0])
    (lbr, lbi, bbr, bbi), disc_vjp = jax.vjp(_ssm_disc, *disc_in)
    lam_t = jnp.exp(lax.complex(ssm_lam_re[0], ssm_lam_im[0]) * (jnp.exp(ssm_log_dt[0])[:, None] * T))
    lamp = jnp.concatenate([lbr.reshape(1, NST), lbi.reshape(1, NST), jnp.real(lam_t).reshape(1, NST),
                            jnp.imag(lam_t).reshape(1, NST), jnp.zeros((4, NST), F32)], axis=0)
    bre_bd, bim_bd = _bdiag_in(bbr).astype(BF16), _bdiag_in(bbi).astype(BF16)
    cre_bd, cim_bd = _bdiag_out(ssm_c_re[0]).astype(BF16), _bdiag_out(ssm_c_im[0]).astype(BF16)

    h0 = jnp.concatenate([jnp.zeros((FRONT - NMETA, D), F32), meta_full, xs], axis=0)
    tgt = jnp.pad(loss_target[0], ((FRONT, 0), (0, 0)))
    h1, a1, b1 = _ffn_fwd(h0, ffn1_norm, gw["ffn1_w1"], gw["ffn1_w3"], gw["ffn1_w2"], "ffn1_fwd")
    vg, uf, gate = _mix_in_fwd(h1, mix_norm, w_in_f, b_gate, "mix_in_fwd")
    z1 = _conv_fwd(vg, dw_pad, conv_dw_b, "conv_fwd")
    yss = _ssm_fwd(uf, bre_bd, bim_bd, cre_bd, cim_bd, lamp, ssm_d, "ssm_fwd")
    h2 = _merge_fwd(h1, z1, yss, gate, conv_ln_g, conv_ln_b, wp_f, wv_f, wg_f, wo_f, "merge_fwd")
    h3, a2, b2 = _ffn_fwd(h2, ffn2_norm, gw["ffn2_w1"], gw["ffn2_w3"], gw["ffn2_w2"], "ffn2_fwd")

    dh3, loss_part, d_final = _final(h3, final_norm.reshape(1, D), tgt, "final")
    dh2, da2, db2, s2, n2, d_ffn2_norm = _ffn_bwd(
        h2, ffn2_norm, dh3, a2, b2, gw["ffn2_w1"], gw["ffn2_w3"], gw["ffn2_w2"], "ffn2_bwd")
    gbig = {}
    gbig["ffn2_w1"] = _wgrad(n2, da2, "ffn2_dw1")
    gbig["ffn2_w3"] = _wgrad(n2, db2, "ffn2_dw3")
    gbig["ffn2_w2"] = _wgrad(s2, dh3, "ffn2_dw2", 0.5)
    (m_b, dgate, dyc, z3, dz1, yg, dsv, dsg, dyss, d_b_gate, d_ln_g, d_ln_b) = _merge_bwd(
        dh2, z1, yss, gate, conv_ln_g, conv_ln_b, wp_f, wv_f, wg_f, wo_f, "merge_bwd")
    gbig["w_out"] = _wgrad(m_b, dh2, "dw_out").reshape(NSH, D // NSH, D)

    def shard_cols(gm):
        return gm.reshape(gm.shape[0], NSH, -1).transpose(1, 0, 2)

    gbig["conv_proj"] = shard_cols(_wgrad(z3, dyc, "dw_proj"))
    gbig["ssm_w_v"] = shard_cols(_wgrad(yg, dsv, "dw_v"))
    gbig["ssm_w_g"] = shard_cols(_wgrad(yg, dsg, "dw_g"))
    dv, dgl, ddw, d_dw_b = _conv_bwd(dz1, vg, dw_pad, "conv_bwd")
    duf, dbre, dbim, dcre, dcim, dlam, d_ssm_d = _ssm_bwd(
        uf, dyss, bre_bd, bim_bd, cre_bd, cim_bd, lamp, ssm_d, "ssm_bwd")
    dh1, u_b, d_mix_norm = _mix_in_bwd(h1, mix_norm, dh2, dv, dgl, duf, dgate, w_in_f, "mix_in_bwd")
    gbig["w_in"] = shard_cols(jnp.concatenate(
        [_wgrad(u_b, dv, "dw_in_v"), _wgrad(u_b, dgl, "dw_in_g"), _wgrad(u_b, duf, "dw_in_u"),
         _wgrad(u_b, dgate, "dw_in_gate")], axis=1))
    dh0, da1, db1, s1, n1, d_ffn1_norm = _ffn_bwd(
        h0, ffn1_norm, dh1, a1, b1, gw["ffn1_w1"], gw["ffn1_w3"], gw["ffn1_w2"], "ffn1_bwd")
    gbig["ffn1_w1"] = _wgrad(n1, da1, "ffn1_dw1")
    gbig["ffn1_w3"] = _wgrad(n1, db1, "ffn1_dw3")
    gbig["ffn1_w2"] = _wgrad(s1, dh1, "ffn1_dw2", 0.5)
    grad_x = dh0[FRONT:][None]

    d_bbr = _diag_blocks(dbre.reshape(NQ, 8, H, 8, P)).transpose(0, 2, 1)
    d_bbi = _diag_blocks(dbim.reshape(NQ, 8, H, 8, P)).transpose(0, 2, 1)
    d_c_re = _diag_blocks(dcre.reshape(NQ, 8, P, 8, H)).transpose(0, 2, 1)
    d_c_im = _diag_blocks(dcim.reshape(NQ, 8, P, 8, H)).transpose(0, 2, 1)
    d_lbr = dlam[:, 0, :].reshape(G, P)
    d_lbi = dlam[:, 1, :].reshape(G, P)
    d_lam_re, d_lam_im, d_log_dt, d_b_re, d_b_im = disc_vjp((d_lbr, d_lbi, d_bbr, d_bbi))

    recv = _scatter_halves([gbig[n] for n in big], "scatter_grads")
    halves = [_sum8(r, "sum_" + n) for n, r in zip(big, recv)]
    full = _swap_halves(halves, "swap_halves")
    out_g, out_d, out_m, out_v = {}, {}, {}, {}
    for n, f in zip(big, full):
        shp = args[n].shape
        g2 = f.reshape(f.shape[0] * f.shape[1], f.shape[2])
        d2, m2, v2 = _adamw(args[n].reshape(g2.shape), g2, args["m_" + n].reshape(g2.shape),
                            args["v_" + n].reshape(g2.shape), "adamw_" + n)
        out_g[n], out_d[n], out_m[n], out_v[n] = (t.reshape(shp) for t in (g2, d2, m2, v2))

    sg = {"meta_tokens": dh0[FRONT - NMETA:FRONT], "ffn1_norm": d_ffn1_norm, "mix_norm": d_mix_norm,
          "b_gate": d_b_gate, "conv_dw": ddw[:KW], "conv_dw_b": d_dw_b, "conv_ln_g": d_ln_g, "conv_ln_b": d_ln_b,
          "ssm_lam_re": d_lam_re, "ssm_lam_im": d_lam_im, "ssm_log_dt": d_log_dt, "ssm_b_re": d_b_re,
          "ssm_b_im": d_b_im, "ssm_c_re": d_c_re, "ssm_c_im": d_c_im, "ssm_d": d_ssm_d,
          "ffn2_norm": d_ffn2_norm, "final_norm": d_final}
    sshapes = [sg[n].shape for n in small]
    tot = _sum8(_gather_all(_pack([sg[n] for n in small]), "gather_small_grads"), "sum_small")
    sgr = dict(zip(small, _unpack(tot, sshapes)))
    sgr["meta_tokens"] = lax.dynamic_slice_in_dim(sgr["meta_tokens"], chip * (D // NSH), D // NSH, axis=1)
    sgr["conv_dw"] = lax.dynamic_slice_in_dim(sgr["conv_dw"], chip * (DC // NSH), DC // NSH, axis=1)
    pshapes = [args[n].shape for n in small]
    d_s, m_s, v_s = _adamw(_pack([args[n] for n in small]), _pack([sgr[n] for n in small]),
                           _pack([args["m_" + n] for n in small]), _pack([args["v_" + n] for n in small]),
                           "adamw_small")
    for n, g_, d_, m_, v_ in zip(small, [sgr[n] for n in small], _unpack(d_s, pshapes),
                                 _unpack(m_s, pshapes), _unpack(v_s, pshapes)):
        out_g[n], out_d[n], out_m[n], out_v[n] = g_.reshape(args[n].shape), d_, m_, v_

    loss = lax.psum(loss_part[0, 0], ("x", "y", "c"))
    return (loss, grad_x, *[out_g[n] for n in names], *[out_d[n] for n in names],
            *[out_m[n] for n in names], *[out_v[n] for n in names])
```

```python
import math

import jax
import jax.numpy as jnp
from jax import lax
from jax.experimental import pallas as pl
from jax.experimental.pallas import tpu as pltpu

F32 = jnp.float32
BF16 = jnp.bfloat16

D = 1024
NSH = 4
F = 2816
FS = F // NSH
DC = 512
DS = 512
KW = 31
KWP = 32
NMETA = 16
FRONT = 128
G, P, H = 32, 64, 16
NST = G * P
NQ = 4
QS = NST // NQ
QU = DS // NQ
NSEG = 8
EPS = 1e-6
LR, B1, B2, AEPS, WD, STEP = 1e-3, 0.9, 0.999, 1e-8, 0.01, 10
VMEM_LIMIT = 58 * 1024 * 1024
MESH = pl.DeviceIdType.MESH
ANY = pl.BlockSpec(memory_space=pl.ANY)


def _params(*sem):
    return pltpu.CompilerParams(dimension_semantics=sem, vmem_limit_bytes=VMEM_LIMIT)


def _res(shape):
    nd = len(shape)
    return pl.BlockSpec(shape, lambda *_: (0,) * nd, pipeline_mode=pl.Buffered(1))


def _tile(n, cap, mult=16):
    best = None
    for t in range(mult, min(n, cap) + 1, mult):
        if n % t == 0:
            best = t
    assert best is not None, (n, cap, mult)
    return best


def _dot(a, b):
    return jnp.dot(a, b, preferred_element_type=F32)


def _dot_nt(a, b):
    return lax.dot_general(a, b, (((1,), (1,)), ((), ())), preferred_element_type=F32)


def _dot_tn(a, b):
    return lax.dot_general(a, b, (((0,), (0,)), ((), ())), preferred_element_type=F32)


def _sigmoid(x):
    return 1.0 / (1.0 + jnp.exp(-x))


_GC = math.sqrt(2.0 / math.pi)
_GA = 0.044715


def _gelu(x):
    return 0.5 * x * (1.0 + jnp.tanh(_GC * (x + _GA * x * x * x)))


def _gelu_grad(x):
    t = jnp.tanh(_GC * (x + _GA * x * x * x))
    return 0.5 * (1.0 + t) + 0.5 * x * (1.0 - t * t) * _GC * (1.0 + 3.0 * _GA * x * x)


def _rms(hv, g):
    r = lax.rsqrt(jnp.mean(hv * hv, axis=-1, keepdims=True) + EPS)
    return hv * r * g, r


def _rms_bwd(dn, hv, r, g):
    xh = hv * r
    dxh = dn * g
    return r * (dxh - xh * jnp.mean(dxh * xh, axis=-1, keepdims=True)), xh


def _acc_rows(ref, part, first):
    @pl.when(first)
    def _():
        ref[...] = part

    @pl.when(jnp.logical_not(first))
    def _():
        ref[...] += part


def _coords():
    return lax.axis_index("x"), lax.axis_index("y"), lax.axis_index("c")


def _flip(v, d):
    return 1 - v if d else v


def _run(local, remote):
    for cp in local + remote:
        cp.start()
    for cp in remote:
        cp.wait()
    for cp in local:
        cp.wait()


def _gather_chips(shards, name):
    n = len(shards)
    rel = ((1, 0), (0, 1), (1, 1))

    def body(*refs):
        ins, outs = refs[:n], refs[n:2 * n]
        send, recv, loc = refs[2 * n:2 * n + 3]
        x, y, c = _coords()
        me = 2 * x + y
        land = refs[2 * n + 3:]
        own = [pltpu.make_async_copy(ins[t], outs[t].at[me], loc.at[t]) for t in range(n)]
        remote, store = [], []
        for t in range(n):
            for k, (dx, dy) in enumerate(rel):
                px, py = _flip(x, dx), _flip(y, dy)
                remote.append(pltpu.make_async_remote_copy(
                    src_ref=ins[t], dst_ref=land[t].at[k],
                    send_sem=send.at[3 * t + k], recv_sem=recv.at[3 * t + k],
                    device_id=(px, py, c), device_id_type=MESH))
                store.append(pltpu.make_async_copy(land[t].at[k], outs[t].at[2 * px + py], loc.at[n + 3 * t + k]))
        for cp in own + remote:
            cp.start()
        for cp, st in zip(remote, store):
            cp.wait()
            st.start()
        for cp in own + store:
            cp.wait()

    return pl.pallas_call(
        body, name=name,
        out_shape=[jax.ShapeDtypeStruct((NSH,) + s.shape, s.dtype) for s in shards],
        in_specs=[ANY] * n, out_specs=[ANY] * n,
        scratch_shapes=[pltpu.SemaphoreType.DMA((3 * n,)), pltpu.SemaphoreType.DMA((3 * n,)),
                        pltpu.SemaphoreType.DMA((4 * n,))]
        + [pltpu.VMEM((3,) + s.shape, s.dtype) for s in shards],
        compiler_params=pltpu.CompilerParams(vmem_limit_bytes=VMEM_LIMIT),
    )(*shards)


_REL7 = tuple((dx, dy, dc) for dx in (0, 1) for dy in (0, 1) for dc in (0, 1))[1:]


def _gather_all(a, name):
    def body(a_ref, o_ref, send, recv, loc):
        x, y, c = _coords()
        me = 4 * x + 2 * y + c
        local = [pltpu.make_async_copy(a_ref, o_ref.at[me], loc.at[0])]
        remote = [pltpu.make_async_remote_copy(
            src_ref=a_ref, dst_ref=o_ref.at[me], send_sem=send.at[k], recv_sem=recv.at[k],
            device_id=(_flip(x, dx), _flip(y, dy), _flip(c, dc)), device_id_type=MESH)
            for k, (dx, dy, dc) in enumerate(_REL7)]
        _run(local, remote)

    return pl.pallas_call(
        body, name=name,
        out_shape=jax.ShapeDtypeStruct((8,) + a.shape, a.dtype),
        in_specs=[ANY], out_specs=ANY,
        scratch_shapes=[pltpu.SemaphoreType.DMA((7,)), pltpu.SemaphoreType.DMA((7,)),
                        pltpu.SemaphoreType.DMA((1,))],
    )(a)


def _scatter_halves(grads, name):
    n = len(grads)

    def body(*refs):
        ins, outs = refs[:n], refs[n:2 * n]
        send, recv, loc = refs[2 * n:]
        x, y, c = _coords()
        me = 4 * x + 2 * y + c
        local, remote = [], []
        for t in range(n):
            half = grads[t].shape[1] // 2
            local.append(pltpu.make_async_copy(
                ins[t].at[2 * x + y, pl.ds(c * half, half)], outs[t].at[me], loc.at[t]))
            for k, (dx, dy, dc) in enumerate(_REL7):
                px, py, pc = _flip(x, dx), _flip(y, dy), _flip(c, dc)
                remote.append(pltpu.make_async_remote_copy(
                    src_ref=ins[t].at[2 * px + py, pl.ds(pc * half, half)], dst_ref=outs[t].at[me],
                    send_sem=send.at[7 * t + k], recv_sem=recv.at[7 * t + k],
                    device_id=(px, py, pc), device_id_type=MESH))
        _run(local, remote)

    return pl.pallas_call(
        body, name=name,
        out_shape=[jax.ShapeDtypeStruct((8, g.shape[1] // 2, g.shape[2]), g.dtype) for g in grads],
        in_specs=[ANY] * n, out_specs=[ANY] * n,
        scratch_shapes=[pltpu.SemaphoreType.DMA((7 * n,)), pltpu.SemaphoreType.DMA((7 * n,)),
                        pltpu.SemaphoreType.DMA((n,))],
    )(*grads)


def _swap_halves(halves, name):
    n = len(halves)

    def body(*refs):
        ins, outs = refs[:n], refs[n:2 * n]
        send, recv, loc = refs[2 * n:2 * n + 3]
        stage, land = refs[2 * n + 3:3 * n + 3], refs[3 * n + 3:]
        x, y, c = _coords()
        own = [pltpu.make_async_copy(ins[t], outs[t].at[c], loc.at[t]) for t in range(n)]
        load = [pltpu.make_async_copy(ins[t], stage[t], loc.at[n + t]) for t in range(n)]
        remote = [pltpu.make_async_remote_copy(
            src_ref=stage[t], dst_ref=land[t], send_sem=send.at[t], recv_sem=recv.at[t],
            device_id=(x, y, 1 - c), device_id_type=MESH) for t in range(n)]
        store = [pltpu.make_async_copy(land[t], outs[t].at[1 - c], loc.at[2 * n + t]) for t in range(n)]
        for cp in load + own:
            cp.start()
        for t in range(n):
            load[t].wait()
            remote[t].start()
        for t in range(n):
            remote[t].wait()
            store[t].start()
        for cp in own + store:
            cp.wait()

    return pl.pallas_call(
        body, name=name,
        out_shape=[jax.ShapeDtypeStruct((2,) + h.shape, h.dtype) for h in halves],
        in_specs=[ANY] * n, out_specs=[ANY] * n,
        scratch_shapes=[pltpu.SemaphoreType.DMA((n,)), pltpu.SemaphoreType.DMA((n,)),
                        pltpu.SemaphoreType.DMA((3 * n,))]
        + [pltpu.VMEM(h.shape, h.dtype) for h in halves] * 2,
        compiler_params=pltpu.CompilerParams(vmem_limit_bytes=VMEM_LIMIT),
    )(*halves)


def _sum8(r, name):
    _, R, C = r.shape
    tr = _tile(R, max(16, (1 << 19) // C))

    def body(r_ref, o_ref):
        acc = r_ref[0].astype(F32)
        for k in range(1, 8):
            acc = acc + r_ref[k].astype(F32)
        o_ref[...] = acc

    return pl.pallas_call(
        body, name=name, grid=(R // tr,),
        out_shape=jax.ShapeDtypeStruct((R, C), F32),
        in_specs=[pl.BlockSpec((8, tr, C), lambda i: (0, i, 0))],
        out_specs=pl.BlockSpec((tr, C), lambda i: (i, 0)),
        compiler_params=_params("parallel"),
    )(r)


def _adamw(w, g, m, v, name):
    R, C = w.shape
    tr = _tile(R, max(8, (1 << 18) // C), 8)
    c1 = 1.0 / (1.0 - B1 ** STEP)
    c2 = 1.0 / (1.0 - B2 ** STEP)

    def body(w_ref, g_ref, m_ref, v_ref, d_ref, nm_ref, nv_ref):
        gv = g_ref[...]
        nm = B1 * m_ref[...] + (1.0 - B1) * gv
        nv = B2 * v_ref[...] + (1.0 - B2) * gv * gv
        nm_ref[...] = nm
        nv_ref[...] = nv
        d_ref[...] = -LR * ((nm * c1) / (jnp.sqrt(nv * c2) + AEPS) + WD * w_ref[...])

    spec = pl.BlockSpec((tr, C), lambda i: (i, 0))
    return pl.pallas_call(
        body, name=name, grid=(R // tr,),
        out_shape=[jax.ShapeDtypeStruct((R, C), F32)] * 3,
        in_specs=[spec] * 4, out_specs=[spec] * 3,
        compiler_params=_params("parallel"),
    )(w, g, m, v)


def _ffn_fwd(h, g, w1, w3, w2, name):
    L = h.shape[0]
    tm = _tile(L, 528)

    def body(h_ref, g_ref, w1_ref, w3_ref, w2_ref, o_ref, a_ref, b_ref, n_s, acc_s):
        j = pl.program_id(1)

        @pl.when(j == 0)
        def _():
            hv = h_ref[...]
            n, _ = _rms(hv, g_ref[...])
            n_s[...] = n.astype(BF16)
            acc_s[...] = hv

        n = n_s[...]
        a = _dot(n, w1_ref[j])
        b = _dot(n, w3_ref[j])
        a_ref[0] = a.astype(BF16)
        b_ref[0] = b.astype(BF16)
        s = (a * _sigmoid(a) * b).astype(BF16)
        acc_s[...] += 0.5 * _dot(s, w2_ref[j])

        @pl.when(j == NSH - 1)
        def _():
            o_ref[...] = acc_s[...]

    row = pl.BlockSpec((tm, D), lambda i, j: (i, 0))
    hid = pl.BlockSpec((1, tm, FS), lambda i, j: (j, i, 0))
    return pl.pallas_call(
        body, name=name, grid=(L // tm, NSH),
        out_shape=[jax.ShapeDtypeStruct((L, D), F32),
                   jax.ShapeDtypeStruct((NSH, L, FS), BF16), jax.ShapeDtypeStruct((NSH, L, FS), BF16)],
        in_specs=[row, _res((1, D)), _res((NSH, D, FS)), _res((NSH, D, FS)), _res((NSH, FS, D))],
        out_specs=[row, hid, hid],
        scratch_shapes=[pltpu.VMEM((tm, D), BF16), pltpu.VMEM((tm, D), F32)],
        compiler_params=_params("arbitrary", "arbitrary"),
    )(h, g, w1, w3, w2)


def _ffn_bwd(h, g, dout, a, b, w1, w3, w2, name):
    L = h.shape[0]
    tm = _tile(L, 352)

    def body(h_ref, g_ref, do_ref, a_ref, b_ref, w1_ref, w3_ref, w2_ref,
             dh_ref, da_ref, db_ref, s_ref, n_ref, dg_ref, dob_s, dn_s):
        i, j = pl.program_id(0), pl.program_id(1)

        @pl.when(j == 0)
        def _():
            n, _ = _rms(h_ref[...], g_ref[...])
            n_ref[...] = n.astype(BF16)
            dob_s[...] = do_ref[...].astype(BF16)
            dn_s[...] = jnp.zeros_like(dn_s)

        av = a_ref[0].astype(F32)
        bv = b_ref[0].astype(F32)
        sig = _sigmoid(av)
        sa = av * sig
        ds = 0.5 * _dot_nt(dob_s[...], w2_ref[j])
        s_ref[0] = (sa * bv).astype(BF16)
        da = (ds * bv * sig * (1.0 + av * (1.0 - sig))).astype(BF16)
        db = (ds * sa).astype(BF16)
        da_ref[0] = da
        db_ref[0] = db
        dn_s[...] += _dot_nt(da, w1_ref[j]) + _dot_nt(db, w3_ref[j])

        @pl.when(j == NSH - 1)
        def _():
            hv = h_ref[...]
            gv = g_ref[...]
            r = lax.rsqrt(jnp.mean(hv * hv, axis=-1, keepdims=True) + EPS)
            dn = dn_s[...]
            dx, xh = _rms_bwd(dn, hv, r, gv)
            dh_ref[...] = do_ref[...] + dx
            _acc_rows(dg_ref, jnp.sum(dn * xh, axis=0, keepdims=True), i == 0)

    row = pl.BlockSpec((tm, D), lambda i, j: (i, 0))
    hid = pl.BlockSpec((1, tm, FS), lambda i, j: (j, i, 0))
    return pl.pallas_call(
        body, name=name, grid=(L // tm, NSH),
        out_shape=[jax.ShapeDtypeStruct((L, D), F32)]
        + [jax.ShapeDtypeStruct((NSH, L, FS), BF16)] * 3
        + [jax.ShapeDtypeStruct((L, D), BF16), jax.ShapeDtypeStruct((1, D), F32)],
        in_specs=[row, _res((1, D)), row, hid, hid,
                  _res((NSH, D, FS)), _res((NSH, D, FS)), _res((NSH, FS, D))],
        out_specs=[row, hid, hid, hid, row, pl.BlockSpec((1, D), lambda i, j: (0, 0))],
        scratch_shapes=[pltpu.VMEM((tm, D), BF16), pltpu.VMEM((tm, D), F32)],
        compiler_params=_params("arbitrary", "arbitrary"),
    )(h, g, dout, a, b, w1, w3, w2)


def _wgrad(xm, ym, name, scale=1.0):
    xs, ys = xm.ndim == 3, ym.ndim == 3
    assert not (xs and ys)
    L = xm.shape[-2]
    K, N = xm.shape[-1], ym.shape[-1]
    tl = _tile(L, 1056)
    nl = L // tl
    if xs or ys:
        tn, grid_n = N, NSH
    else:
        tn = _tile(N, 1024, 128)
        grid_n = N // tn

    def body(x_ref, y_ref, o_ref, acc_s):
        l = pl.program_id(1)
        xv = x_ref[0] if xs else x_ref[...]
        yv = y_ref[0] if ys else y_ref[...]
        part = _dot_tn(xv.astype(BF16), yv.astype(BF16))
        _acc_rows(acc_s, part, l == 0)

        @pl.when(l == nl - 1)
        def _():
            res = (acc_s[...] * scale).astype(BF16)
            if xs or ys:
                o_ref[0] = res
            else:
                o_ref[...] = res

    if xs:
        x_spec = pl.BlockSpec((1, tl, K), lambda n, l: (n, l, 0))
        y_spec = pl.BlockSpec((tl, N), lambda n, l: (l, 0))
        o_spec = pl.BlockSpec((1, K, N), lambda n, l: (n, 0, 0))
        o_shape = (NSH, K, N)
    elif ys:
        x_spec = pl.BlockSpec((tl, K), lambda n, l: (l, 0))
        y_spec = pl.BlockSpec((1, tl, N), lambda n, l: (n, l, 0))
        o_spec = pl.BlockSpec((1, K, N), lambda n, l: (n, 0, 0))
        o_shape = (NSH, K, N)
    else:
        x_spec = pl.BlockSpec((tl, K), lambda n, l: (l, 0))
        y_spec = pl.BlockSpec((tl, tn), lambda n, l: (l, n))
        o_spec = pl.BlockSpec((K, tn), lambda n, l: (0, n))
        o_shape = (K, N)
    return pl.pallas_call(
        body, name=name, grid=(grid_n, nl),
        out_shape=jax.ShapeDtypeStruct(o_shape, BF16),
        in_specs=[x_spec, y_spec], out_specs=o_spec,
        scratch_shapes=[pltpu.VMEM((K, tn), F32)],
        compiler_params=_params("parallel", "arbitrary"),
    )(xm, ym)


def _mix_in_fwd(h, g, w_in, b_gate, name):
    L = h.shape[0]
    tm = _tile(L, 528)

    def body(h_ref, g_ref, w_ref, bg_ref, vg_ref, uf_ref, gt_ref):
        u, _ = _rms(h_ref[...], g_ref[...])
        ub = u.astype(BF16)
        vg_ref[...] = _dot(ub, w_ref[:, 0:2 * DC]).astype(BF16)
        uf_ref[...] = _dot(ub, w_ref[:, 2 * DC:2 * DC + DS]).astype(BF16)
        gt_ref[...] = _sigmoid(_dot(ub, w_ref[:, 2 * DC + DS:]) + bg_ref[...]).astype(BF16)

    def row(n):
        return pl.BlockSpec((tm, n), lambda i: (i, 0))

    din = w_in.shape[1]
    return pl.pallas_call(
        body, name=name, grid=(L // tm,),
        out_shape=[jax.ShapeDtypeStruct((L, 2 * DC), BF16), jax.ShapeDtypeStruct((L, DS), BF16),
                   jax.ShapeDtypeStruct((L, 2 * D), BF16)],
        in_specs=[row(D), _res((1, D)), _res((D, din)), _res((1, 2 * D))],
        out_specs=[row(2 * DC), row(DS), row(2 * D)],
        compiler_params=_params("parallel"),
    )(h, g, w_in, b_gate)


def _mix_in_bwd(h, g, dres, dv, dgl, duf, dgate, w_in, name):
    L = h.shape[0]
    tm = _tile(L, 528)

    def body(h_ref, g_ref, dr_ref, dv_ref, dgl_ref, duf_ref, dgt_ref, w_ref, dh_ref, u_ref, dgm_ref):
        i = pl.program_id(0)
        hv = h_ref[...]
        gv = g_ref[...]
        u, r = _rms(hv, gv)
        u_ref[...] = u.astype(BF16)
        du = (_dot_nt(dv_ref[...], w_ref[:, 0:DC]) + _dot_nt(dgl_ref[...], w_ref[:, DC:2 * DC])
              + _dot_nt(duf_ref[...], w_ref[:, 2 * DC:2 * DC + DS])
              + _dot_nt(dgt_ref[...], w_ref[:, 2 * DC + DS:]))
        dx, xh = _rms_bwd(du, hv, r, gv)
        dh_ref[...] = dr_ref[...] + dx
        _acc_rows(dgm_ref, jnp.sum(du * xh, axis=0, keepdims=True), i == 0)

    def row(n):
        return pl.BlockSpec((tm, n), lambda i: (i, 0))

    din = w_in.shape[1]
    return pl.pallas_call(
        body, name=name, grid=(L // tm,),
        out_shape=[jax.ShapeDtypeStruct((L, D), F32), jax.ShapeDtypeStruct((L, D), BF16),
                   jax.ShapeDtypeStruct((1, D), F32)],
        in_specs=[row(D), _res((1, D)), row(D), row(DC), row(DC), row(DS), row(2 * D), _res((D, din))],
        out_specs=[row(D), row(D), pl.BlockSpec((1, D), lambda i: (0, 0))],
        compiler_params=_params("arbitrary"),
    )(h, g, dres, dv, dgl, duf, dgate, w_in)


def _conv_fwd(vg, dw, dwb, name):
    L = vg.shape[0]
    nc = DC // 128

    def body(v_ref, g_ref, dw_ref, dwb_ref, z_ref, zp_s):
        zp_s[0:KWP, :] = jnp.zeros((KWP, 128), F32)
        zp_s[KWP:, :] = v_ref[...].astype(F32) * _sigmoid(g_ref[...].astype(F32))
        acc = jnp.broadcast_to(dwb_ref[...], (L, 128))
        for k in range(KW):
            acc = acc + dw_ref[k:k + 1, :] * zp_s[pl.ds(k + 2, L), :]
        z_ref[...] = acc

    return pl.pallas_call(
        body, name=name, grid=(nc,),
        out_shape=jax.ShapeDtypeStruct((L, DC), F32),
        in_specs=[pl.BlockSpec((L, 128), lambda c: (0, c)), pl.BlockSpec((L, 128), lambda c: (0, nc + c)),
                  pl.BlockSpec((KWP, 128), lambda c: (0, c)), pl.BlockSpec((1, 128), lambda c: (0, c))],
        out_specs=pl.BlockSpec((L, 128), lambda c: (0, c)),
        scratch_shapes=[pltpu.VMEM((L + KWP, 128), F32)],
        compiler_params=_params("parallel"),
    )(vg, vg, dw, dwb)


def _conv_bwd(dz1, vg, dw, name):
    L = vg.shape[0]
    nc = DC // 128

    def body(dz_ref, v_ref, g_ref, dw_ref, dv_ref, dg_ref, ddw_ref, ddwb_ref, zp_s, dzp_s):
        vv = v_ref[...].astype(F32)
        sg = _sigmoid(g_ref[...].astype(F32))
        zp_s[0:KWP, :] = jnp.zeros((KWP, 128), F32)
        zp_s[KWP:, :] = vv * sg
        dz = dz_ref[...]
        dzp_s[0:L, :] = dz
        dzp_s[L:, :] = jnp.zeros((KWP, 128), F32)
        ddwb_ref[...] = jnp.sum(dz, axis=0, keepdims=True)
        acc = jnp.zeros((L, 128), F32)
        for k in range(KW):
            acc = acc + dw_ref[k:k + 1, :] * dzp_s[pl.ds(KW - 1 - k, L), :]
            ddw_ref[k:k + 1, :] = jnp.sum(dz * zp_s[pl.ds(k + 2, L), :], axis=0, keepdims=True)
        ddw_ref[KW:KWP, :] = jnp.zeros((KWP - KW, 128), F32)
        dv_ref[...] = (acc * sg).astype(BF16)
        dg_ref[...] = (acc * vv * sg * (1.0 - sg)).astype(BF16)

    col = pl.BlockSpec((L, 128), lambda c: (0, c))
    return pl.pallas_call(
        body, name=name, grid=(nc,),
        out_shape=[jax.ShapeDtypeStruct((L, DC), BF16), jax.ShapeDtypeStruct((L, DC), BF16),
                   jax.ShapeDtypeStruct((KWP, DC), F32), jax.ShapeDtypeStruct((1, DC), F32)],
        in_specs=[col, col, pl.BlockSpec((L, 128), lambda c: (0, nc + c)),
                  pl.BlockSpec((KWP, 128), lambda c: (0, c))],
        out_specs=[col, col, pl.BlockSpec((KWP, 128), lambda c: (0, c)), pl.BlockSpec((1, 128), lambda c: (0, c))],
        scratch_shapes=[pltpu.VMEM((L + KWP, 128), F32), pltpu.VMEM((L + KWP, 128), F32)],
        compiler_params=_params("parallel"),
    )(dz1, vg, vg, dw)


NLB = QS // 128


def _lb_store(ref, rows, val):
    for cb in range(NLB):
        ref[cb, rows, :] = val[:, cb * 128:(cb + 1) * 128]


def _lb_load(ref, rows):
    return jnp.concatenate([ref[cb, rows, :] for cb in range(NLB)], axis=1)


def _scan(xr_ref, xi_ref, base, T, ar, ai, atr, ati, reverse, sr_ref=None, si_ref=None):
    W = ar.shape[1]
    ar, ai = jnp.broadcast_to(ar, (NSEG, W)), jnp.broadcast_to(ai, (NSEG, W))
    atr, ati = jnp.broadcast_to(atr, (NSEG, W)), jnp.broadcast_to(ati, (NSEG, W))
    zero = jnp.zeros((NSEG, W), F32)

    def rows(t):
        tt = T - 1 - t if reverse else t
        return pl.ds(base + tt, NSEG, stride=T)

    def step(t, carry):
        sr, si = carry
        idx = rows(t)
        return (ar * sr - ai * si + _lb_load(xr_ref, idx), ar * si + ai * sr + _lb_load(xi_ref, idx))

    er, ei = lax.fori_loop(0, T, step, (zero, zero))
    seg = lax.broadcasted_iota(jnp.int32, (NSEG, W), 0)
    edge = seg == (NSEG - 1 if reverse else 0)
    cr, ci = zero, zero
    for _ in range(NSEG - 1):
        nr = atr * cr - ati * ci + er
        ni = atr * ci + ati * cr + ei
        shift = NSEG - 1 if reverse else 1
        cr = jnp.where(edge, 0.0, pltpu.roll(nr, shift, 0))
        ci = jnp.where(edge, 0.0, pltpu.roll(ni, shift, 0))

    if sr_ref is None:
        def step2(t, carry):
            sr, si = step(t, carry)
            idx = rows(t)
            _lb_store(xr_ref, idx, sr)
            _lb_store(xi_ref, idx, si)
            return sr, si

        lax.fori_loop(0, T, step2, (cr, ci))
        return None

    def step3(t, carry):
        sr, si, qr, qi = carry
        sr, si = step(t, (sr, si))
        idx = rows(t)
        _lb_store(xr_ref, idx, sr)
        _lb_store(xi_ref, idx, si)
        prev = pl.ds(NSEG - 1 + T - 1 - t, NSEG, stride=T)
        pr, pi = _lb_load(sr_ref, prev), _lb_load(si_ref, prev)
        return sr, si, qr + sr * pr + si * pi, qi + si * pr - sr * pi

    _, _, qr, qi = lax.fori_loop(0, T, step3, (cr, ci, zero, zero))
    return jnp.sum(qr, axis=0, keepdims=True), jnp.sum(qi, axis=0, keepdims=True)


def _ssm_fwd(uf, bre, bim, cre, cim, lamp, dsk, name):
    L = uf.shape[0]
    T = L // NSEG

    def body(u_ref, bre_ref, bim_ref, cre_ref, cim_ref, lam_ref, d_ref, y_ref, sr_s, si_s):
        for k in range(NSEG):
            sl = slice(k * T, (k + 1) * T)
            uk = u_ref[sl, :]
            _lb_store(sr_s, sl, _dot(uk, bre_ref[...]))
            _lb_store(si_s, sl, _dot(uk, bim_ref[...]))
        _scan(sr_s, si_s, 0, T, lam_ref[0:1, :], lam_ref[1:2, :], lam_ref[2:3, :], lam_ref[3:4, :], False)
        for k in range(NSEG):
            sl = slice(k * T, (k + 1) * T)
            y_ref[sl, :] = (_dot(_lb_load(sr_s, sl).astype(BF16), cre_ref[...])
                            - _dot(_lb_load(si_s, sl).astype(BF16), cim_ref[...])
                            + d_ref[...] * u_ref[sl, :].astype(F32))

    return pl.pallas_call(
        body, name=name, grid=(NQ,),
        out_shape=jax.ShapeDtypeStruct((L, DS), F32),
        in_specs=[pl.BlockSpec((L, QU), lambda q: (0, q)),
                  pl.BlockSpec((QU, QS), lambda q: (q, q)), pl.BlockSpec((QU, QS), lambda q: (q, q)),
                  pl.BlockSpec((QS, QU), lambda q: (q, q)), pl.BlockSpec((QS, QU), lambda q: (q, q)),
                  pl.BlockSpec((8, QS), lambda q: (0, q)), pl.BlockSpec((1, QU), lambda q: (0, q))],
        out_specs=pl.BlockSpec((L, QU), lambda q: (0, q)),
        scratch_shapes=[pltpu.VMEM((NLB, L, 128), F32), pltpu.VMEM((NLB, L, 128), F32)],
        compiler_params=_params("parallel"),
    )(uf, bre, bim, cre, cim, lamp, dsk)


def _ssm_bwd(uf, dyss, bre, bim, cre, cim, lamp, dsk, name):
    L = uf.shape[0]
    T = L // NSEG

    def body(u_ref, dy_ref, bre_ref, bim_ref, cre_ref, cim_ref, lam_ref, d_ref,
             du_ref, dbre_ref, dbim_ref, dcre_ref, dcim_ref, dlam_ref, dd_ref, sr_s, si_s, gr_s, gi_s):
        _lb_store(sr_s, slice(0, NSEG), jnp.zeros((NSEG, QS), F32))
        _lb_store(si_s, slice(0, NSEG), jnp.zeros((NSEG, QS), F32))
        for k in range(NSEG):
            sl = slice(k * T, (k + 1) * T)
            ss = slice(NSEG + k * T, NSEG + (k + 1) * T)
            uk = u_ref[sl, :]
            dyk = dy_ref[sl, :].astype(BF16)
            _lb_store(sr_s, ss, _dot(uk, bre_ref[...]))
            _lb_store(si_s, ss, _dot(uk, bim_ref[...]))
            _lb_store(gr_s, sl, _dot_nt(dyk, cre_ref[...]))
            _lb_store(gi_s, sl, -_dot_nt(dyk, cim_ref[...]))
        ar, ai, atr, ati = lam_ref[0:1, :], lam_ref[1:2, :], lam_ref[2:3, :], lam_ref[3:4, :]
        _scan(sr_s, si_s, NSEG, T, ar, ai, atr, ati, False)
        qr, qi = _scan(gr_s, gi_s, 0, T, ar, -ai, atr, -ati, True, sr_s, si_s)
        dlam_ref[0] = jnp.concatenate([qr, qi, jnp.zeros((6, QS), F32)], axis=0)
        dbre = jnp.zeros((QU, QS), F32)
        dbim = jnp.zeros((QU, QS), F32)
        dcre = jnp.zeros((QS, QU), F32)
        dcim = jnp.zeros((QS, QU), F32)
        dd = jnp.zeros((1, QU), F32)
        for k in range(NSEG):
            sl = slice(k * T, (k + 1) * T)
            ss = slice(NSEG + k * T, NSEG + (k + 1) * T)
            uk = u_ref[sl, :]
            dyk = dy_ref[sl, :]
            dyb = dyk.astype(BF16)
            grb = _lb_load(gr_s, sl).astype(BF16)
            gib = _lb_load(gi_s, sl).astype(BF16)
            du_ref[sl, :] = (_dot_nt(grb, bre_ref[...]) + _dot_nt(gib, bim_ref[...])
                             + dyk * d_ref[...]).astype(BF16)
            dbre = dbre + _dot_tn(uk, grb)
            dbim = dbim + _dot_tn(uk, gib)
            dcre = dcre + _dot_tn(_lb_load(sr_s, ss).astype(BF16), dyb)
            dcim = dcim - _dot_tn(_lb_load(si_s, ss).astype(BF16), dyb)
            dd = dd + jnp.sum(dyk * uk.astype(F32), axis=0, keepdims=True)
        dbre_ref[0] = dbre
        dbim_ref[0] = dbim
        dcre_ref[0] = dcre
        dcim_ref[0] = dcim
        dd_ref[...] = dd

    col = pl.BlockSpec((L, QU), lambda q: (0, q))
    bsp = pl.BlockSpec((QU, QS), lambda q: (q, q))
    csp = pl.BlockSpec((QS, QU), lambda q: (q, q))
    return pl.pallas_call(
        body, name=name, grid=(NQ,),
        out_shape=[jax.ShapeDtypeStruct((L, DS), BF16),
                   jax.ShapeDtypeStruct((NQ, QU, QS), F32), jax.ShapeDtypeStruct((NQ, QU, QS), F32),
                   jax.ShapeDtypeStruct((NQ, QS, QU), F32), jax.ShapeDtypeStruct((NQ, QS, QU), F32),
                   jax.ShapeDtypeStruct((NQ, 8, QS), F32), jax.ShapeDtypeStruct((1, DS), F32)],
        in_specs=[col, col, bsp, bsp, csp, csp,
                  pl.BlockSpec((8, QS), lambda q: (0, q)), pl.BlockSpec((1, QU), lambda q: (0, q))],
        out_specs=[col,
                   pl.BlockSpec((1, QU, QS), lambda q: (q, 0, 0)), pl.BlockSpec((1, QU, QS), lambda q: (q, 0, 0)),
                   pl.BlockSpec((1, QS, QU), lambda q: (q, 0, 0)), pl.BlockSpec((1, QS, QU), lambda q: (q, 0, 0)),
                   pl.BlockSpec((1, 8, QS), lambda q: (q, 0, 0)), pl.BlockSpec((1, QU), lambda q: (0, q))],
        scratch_shapes=[pltpu.VMEM((NLB, L + NSEG, 128), F32), pltpu.VMEM((NLB, L + NSEG, 128), F32),
                        pltpu.VMEM((NLB, L, 128), F32), pltpu.VMEM((NLB, L, 128), F32)],
        compiler_params=_params("parallel"),
    )(uf, dyss, bre, bim, cre, cim, lamp, dsk)


def _branches(z1_ref, yss_ref, gt_ref, lng_ref, lnb_ref, wp_ref, wv_ref, wg_ref):
    zf = z1_ref[...]
    mu = jnp.mean(zf, axis=-1, keepdims=True)
    zc = zf - mu
    rstd = lax.rsqrt(jnp.mean(zc * zc, axis=-1, keepdims=True) + EPS)
    zn = zc * rstd
    z2 = zn * lng_ref[...] + lnb_ref[...]
    sz = _sigmoid(z2)
    z3 = (z2 * sz).astype(BF16)
    y_conv = _dot(z3, wp_ref[...])
    yss = yss_ref[...]
    yg = _gelu(yss).astype(BF16)
    sv = _dot(yg, wv_ref[...])
    sig = _sigmoid(_dot(yg, wg_ref[...]))
    y_ssm = sv * sig
    gc = gt_ref[:, 0:D].astype(F32)
    gs = gt_ref[:, D:2 * D].astype(F32)
    m = gc * y_conv + gs * y_ssm
    return dict(rstd=rstd, zn=zn, z2=z2, sz=sz, z3=z3, y_conv=y_conv, yss=yss, yg=yg, sv=sv, sig=sig,
                y_ssm=y_ssm, gc=gc, gs=gs, m=m)


def _merge_fwd(h, z1, yss, gate, lng, lnb, wp, wv, wg, wo, name):
    L = h.shape[0]
    tm = _tile(L, 528)

    def body(h_ref, z1_ref, yss_ref, gt_ref, lng_ref, lnb_ref, wp_ref, wv_ref, wg_ref, wo_ref, o_ref):
        f = _branches(z1_ref, yss_ref, gt_ref, lng_ref, lnb_ref, wp_ref, wv_ref, wg_ref)
        o_ref[...] = h_ref[...] + _dot(f["m"].astype(BF16), wo_ref[...])

    def row(n):
        return pl.BlockSpec((tm, n), lambda i: (i, 0))

    return pl.pallas_call(
        body, name=name, grid=(L // tm,),
        out_shape=jax.ShapeDtypeStruct((L, D), F32),
        in_specs=[row(D), row(DC), row(DS), row(2 * D), _res((1, DC)), _res((1, DC)),
                  _res((DC, D)), _res((DS, D)), _res((DS, D)), _res((D, D))],
        out_specs=row(D),
        compiler_params=_params("parallel"),
    )(h, z1, yss, gate, lng, lnb, wp, wv, wg, wo)


def _merge_bwd(dh, z1, yss, gate, lng, lnb, wp, wv, wg, wo, name):
    L = dh.shape[0]
    tm = _tile(L, 352)

    def body(dh_ref, z1_ref, yss_ref, gt_ref, lng_ref, lnb_ref, wp_ref, wv_ref, wg_ref, wo_ref,
             m_ref, dgt_ref, dyc_ref, z3_ref, dz1_ref, yg_ref, dsv_ref, dsg_ref, dyss_ref,
             dbg_ref, dlng_ref, dlnb_ref):
        i = pl.program_id(0)
        f = _branches(z1_ref, yss_ref, gt_ref, lng_ref, lnb_ref, wp_ref, wv_ref, wg_ref)
        gc, gs, sig, sv = f["gc"], f["gs"], f["sig"], f["sv"]
        m_ref[...] = f["m"].astype(BF16)
        z3_ref[...] = f["z3"]
        yg_ref[...] = f["yg"]
        dm = _dot_nt(dh_ref[...].astype(BF16), wo_ref[...])
        dgc = (dm * f["y_conv"] * gc * (1.0 - gc)).astype(BF16)
        dgs = (dm * f["y_ssm"] * gs * (1.0 - gs)).astype(BF16)
        dgt_ref[:, 0:D] = dgc
        dgt_ref[:, D:2 * D] = dgs
        part = jnp.concatenate([jnp.sum(dgc.astype(F32), axis=0, keepdims=True),
                                jnp.sum(dgs.astype(F32), axis=0, keepdims=True)], axis=1)
        _acc_rows(dbg_ref, part, i == 0)
        dyc = (dm * gc).astype(BF16)
        dyc_ref[...] = dyc
        dys = dm * gs
        dsv = (dys * sig).astype(BF16)
        dsg = (dys * sv * sig * (1.0 - sig)).astype(BF16)
        dsv_ref[...] = dsv
        dsg_ref[...] = dsg
        dyg = _dot_nt(dsv, wv_ref[...]) + _dot_nt(dsg, wg_ref[...])
        dyss_ref[...] = dyg * _gelu_grad(f["yss"])
        dz3 = _dot_nt(dyc, wp_ref[...])
        z2, sz, zn = f["z2"], f["sz"], f["zn"]
        dz2 = dz3 * sz * (1.0 + z2 * (1.0 - sz))
        _acc_rows(dlng_ref, jnp.sum(dz2 * zn, axis=0, keepdims=True), i == 0)
        _acc_rows(dlnb_ref, jnp.sum(dz2, axis=0, keepdims=True), i == 0)
        dzn = dz2 * lng_ref[...]
        dz1_ref[...] = f["rstd"] * (dzn - jnp.mean(dzn, axis=-1, keepdims=True)
                                    - zn * jnp.mean(dzn * zn, axis=-1, keepdims=True))

    def row(n):
        return pl.BlockSpec((tm, n), lambda i: (i, 0))

    def tot(n):
        return pl.BlockSpec((1, n), lambda i: (0, 0))

    return pl.pallas_call(
        body, name=name, grid=(L // tm,),
        out_shape=[jax.ShapeDtypeStruct((L, D), BF16), jax.ShapeDtypeStruct((L, 2 * D), BF16),
                   jax.ShapeDtypeStruct((L, D), BF16), jax.ShapeDtypeStruct((L, DC), BF16),
                   jax.ShapeDtypeStruct((L, DC), F32), jax.ShapeDtypeStruct((L, DS), BF16),
                   jax.ShapeDtypeStruct((L, D), BF16), jax.ShapeDtypeStruct((L, D), BF16),
                   jax.ShapeDtypeStruct((L, DS), F32),
                   jax.ShapeDtypeStruct((1, 2 * D), F32), jax.ShapeDtypeStruct((1, DC), F32),
                   jax.ShapeDtypeStruct((1, DC), F32)],
        in_specs=[row(D), row(DC), row(DS), row(2 * D), _res((1, DC)), _res((1, DC)),
                  _res((DC, D)), _res((DS, D)), _res((DS, D)), _res((D, D))],
        out_specs=[row(D), row(2 * D), row(D), row(DC), row(DC), row(DS), row(D), row(D), row(DS),
                   tot(2 * D), tot(DC), tot(DC)],
        compiler_params=_params("arbitrary"),
    )(dh, z1, yss, gate, lng, lnb, wp, wv, wg, wo)


def _final(h, g, tgt, name):
    L = h.shape[0]
    tm = _tile(L, 528)

    def body(h_ref, g_ref, t_ref, dh_ref, loss_ref, dg_ref):
        i = pl.program_id(0)
        hv = h_ref[...]
        gv = g_ref[...]
        y, r = _rms(hv, gv)
        row = i * tm + lax.broadcasted_iota(jnp.int32, (tm, 1), 0)
        e = jnp.where(row >= FRONT, y - t_ref[...], 0.0)
        dy = e * (1.0 / D)
        part = 0.5 * jnp.sum(jnp.sum(e * dy, axis=1, keepdims=True), axis=0, keepdims=True)
        dx, xh = _rms_bwd(dy, hv, r, gv)
        dh_ref[...] = dx
        _acc_rows(loss_ref, part, i == 0)
        _acc_rows(dg_ref, jnp.sum(dy * xh, axis=0, keepdims=True), i == 0)

    row = pl.BlockSpec((tm, D), lambda i: (i, 0))
    return pl.pallas_call(
        body, name=name, grid=(L // tm,),
        out_shape=[jax.ShapeDtypeStruct((L, D), F32), jax.ShapeDtypeStruct((1, 1), F32),
                   jax.ShapeDtypeStruct((1, D), F32)],
        in_specs=[row, _res((1, D)), row],
        out_specs=[row, pl.BlockSpec((1, 1), lambda i: (0, 0)), pl.BlockSpec((1, D), lambda i: (0, 0))],
        compiler_params=_params("arbitrary"),
    )(h, g, tgt)


def _ssm_disc(lam_re, lam_im, log_dt, b_re, b_im):
    lam = lax.complex(lam_re, lam_im)
    dt = jnp.exp(log_dt)[:, None]
    lam_bar = jnp.exp(lam * dt)
    bbar = ((lam_bar - 1.0) / lam)[..., None] * lax.complex(b_re, b_im)
    return jnp.real(lam_bar), jnp.imag(lam_bar), jnp.real(bbar), jnp.imag(bbar)


def _bdiag_in(m):
    return jnp.einsum("gph,gk->ghkp", m, jnp.eye(G, dtype=m.dtype)).reshape(G * H, G * P)


def _bdiag_out(m):
    return jnp.einsum("ghp,gk->gpkh", m, jnp.eye(G, dtype=m.dtype)).reshape(G * P, G * H)


def _diag_blocks(m4):
    return jnp.einsum("qiaib->qiab", m4).reshape(G, m4.shape[2], m4.shape[4])


def _pack(parts, rows_mult=8):
    flat = jnp.concatenate([p.reshape(-1).astype(F32) for p in parts])
    n = flat.shape[0]
    tot = -(-n // (128 * rows_mult)) * (128 * rows_mult)
    return jnp.pad(flat, (0, tot - n)).reshape(tot // 128, 128)


def _unpack(buf, shapes):
    flat = buf.reshape(-1)
    out, o = [], 0
    for s in shapes:
        n = math.prod(s)
        out.append(flat[o:o + n].reshape(s))
        o += n
    return out


def kernel(x, meta_tokens, ffn1_norm, ffn1_w1, ffn1_w3, ffn1_w2, mix_norm, w_in, b_gate, conv_dw, conv_dw_b, conv_ln_g, conv_ln_b, conv_proj, ssm_lam_re, ssm_lam_im, ssm_log_dt, ssm_b_re, ssm_b_im, ssm_c_re, ssm_c_im, ssm_d, ssm_w_v, ssm_w_g, w_out, ffn2_norm, ffn2_w1, ffn2_w3, ffn2_w2, final_norm, loss_target, m_meta_tokens, m_ffn1_norm, m_ffn1_w1, m_ffn1_w3, m_ffn1_w2, m_mix_norm, m_w_in, m_b_gate, m_conv_dw, m_conv_dw_b, m_conv_ln_g, m_conv_ln_b, m_conv_proj, m_ssm_lam_re, m_ssm_lam_im, m_ssm_log_dt, m_ssm_b_re, m_ssm_b_im, m_ssm_c_re, m_ssm_c_im, m_ssm_d, m_ssm_w_v, m_ssm_w_g, m_w_out, m_ffn2_norm, m_ffn2_w1, m_ffn2_w3, m_ffn2_w2, m_final_norm, v_meta_tokens, v_ffn1_norm, v_ffn1_w1, v_ffn1_w3, v_ffn1_w2, v_mix_norm, v_w_in, v_b_gate, v_conv_dw, v_conv_dw_b, v_conv_ln_g, v_conv_ln_b, v_conv_proj, v_ssm_lam_re, v_ssm_lam_im, v_ssm_log_dt, v_ssm_b_re, v_ssm_b_im, v_ssm_c_re, v_ssm_c_im, v_ssm_d, v_ssm_w_v, v_ssm_w_g, v_w_out, v_ffn2_norm, v_ffn2_w1, v_ffn2_w3, v_ffn2_w2, v_final_norm):
    args = dict(locals())
    names = ["meta_tokens", "ffn1_norm", "ffn1_w1", "ffn1_w3", "ffn1_w2", "mix_norm", "w_in", "b_gate",
             "conv_dw", "conv_dw_b", "conv_ln_g", "conv_ln_b", "conv_proj", "ssm_lam_re", "ssm_lam_im",
             "ssm_log_dt", "ssm_b_re", "ssm_b_im", "ssm_c_re", "ssm_c_im", "ssm_d", "ssm_w_v", "ssm_w_g",
             "w_out", "ffn2_norm", "ffn2_w1", "ffn2_w3", "ffn2_w2", "final_norm"]
    big = ["ffn1_w1", "ffn1_w3", "ffn1_w2", "w_in", "conv_proj", "ssm_w_v", "ssm_w_g", "w_out",
           "ffn2_w1", "ffn2_w3", "ffn2_w2"]
    small = [n for n in names if n not in big]

    xs = x[0]
    S = xs.shape[0]
    L = FRONT + S
    T = L // NSEG
    jx, jy = lax.axis_index("x"), lax.axis_index("y")
    chip = 2 * jx + jy

    sm = _gather_all(_pack([meta_tokens, conv_dw[0]]), "gather_small")[0::2].reshape(NSH, -1)
    nmt = NMETA * (D // NSH)
    ndw = KW * (DC // NSH)
    meta_full = sm[:, :nmt].reshape(NSH, NMETA, D // NSH).transpose(1, 0, 2).reshape(NMETA, D)
    dw_full = sm[:, nmt:nmt + ndw].reshape(NSH, KW, DC // NSH).transpose(1, 0, 2).reshape(KW, DC)
    dw_pad = jnp.pad(dw_full, ((0, KWP - KW), (0, 0)))
    gw = _gather_chips([args[n][0].astype(BF16) for n in big], "gather_weights")
    gw = dict(zip(big, gw))

    def cols(w):
        return w.transpose(1, 0, 2).reshape(w.shape[1], -1)

    w_in_f = cols(gw["w_in"])
    wp_f, wv_f, wg_f = cols(gw["conv_proj"]), cols(gw["ssm_w_v"]), cols(gw["ssm_w_g"])
    wo_f = gw["w_out"].reshape(D, D)

    disc_in = (ssm_lam_re[0], ssm_lam_im[0], ssm_log_dt[0], ssm_b_re[0], ssm_b_im[0])
    (lbr, lbi, bbr, bbi), disc_vjp = jax.vjp(_ssm_disc, *disc_in)
    lam_t = jnp.exp(lax.complex(ssm_lam_re[0], ssm_lam_im[0]) * (jnp.exp(ssm_log_dt[0])[:, None] * T))
    lamp = jnp.concatenate([lbr.reshape(1, NST), lbi.reshape(1, NST), jnp.real(lam_t).reshape(1, NST),
                            jnp.imag(lam_t).reshape(1, NST), jnp.zeros((4, NST), F32)], axis=0)
    bre_bd, bim_bd = _bdiag_in(bbr).astype(BF16), _bdiag_in(bbi).astype(BF16)
    cre_bd, cim_bd = _bdiag_out(ssm_c_re[0]).astype(BF16), _bdiag_out(ssm_c_im[0]).astype(BF16)

    h0 = jnp.concatenate([jnp.zeros((FRONT - NMETA, D), F32), meta_full, xs], axis=0)
    tgt = jnp.pad(loss_target[0], ((FRONT, 0), (0, 0)))
    h1, a1, b1 = _ffn_fwd(h0, ffn1_norm, gw["ffn1_w1"], gw["ffn1_w3"], gw["ffn1_w2"], "ffn1_fwd")
    vg, uf, gate = _mix_in_fwd(h1, mix_norm, w_in_f, b_gate, "mix_in_fwd")
    z1 = _conv_fwd(vg, dw_pad, conv_dw_b, "conv_fwd")
    yss = _ssm_fwd(uf, bre_bd, bim_bd, cre_bd, cim_bd, lamp, ssm_d, "ssm_fwd")
    h2 = _merge_fwd(h1, z1, yss, gate, conv_ln_g, conv_ln_b, wp_f, wv_f, wg_f, wo_f, "merge_fwd")
    h3, a2, b2 = _ffn_fwd(h2, ffn2_norm, gw["ffn2_w1"], gw["ffn2_w3"], gw["ffn2_w2"], "ffn2_fwd")

    dh3, loss_part, d_final = _final(h3, final_norm.reshape(1, D), tgt, "final")
    dh2, da2, db2, s2, n2, d_ffn2_norm = _ffn_bwd(
        h2, ffn2_norm, dh3, a2, b2, gw["ffn2_w1"], gw["ffn2_w3"], gw["ffn2_w2"], "ffn2_bwd")
    gbig = {}
    gbig["ffn2_w1"] = _wgrad(n2, da2, "ffn2_dw1")
    gbig["ffn2_w3"] = _wgrad(n2, db2, "ffn2_dw3")
    gbig["ffn2_w2"] = _wgrad(s2, dh3, "ffn2_dw2", 0.5)
    (m_b, dgate, dyc, z3, dz1, yg, dsv, dsg, dyss, d_b_gate, d_ln_g, d_ln_b) = _merge_bwd(
        dh2, z1, yss, gate, conv_ln_g, conv_ln_b, wp_f, wv_f, wg_f, wo_f, "merge_bwd")
    gbig["w_out"] = _wgrad(m_b, dh2, "dw_out").reshape(NSH, D // NSH, D)

    def shard_cols(gm):
        return gm.reshape(gm.shape[0], NSH, -1).transpose(1, 0, 2)

    gbig["conv_proj"] = shard_cols(_wgrad(z3, dyc, "dw_proj"))
    gbig["ssm_w_v"] = shard_cols(_wgrad(yg, dsv, "dw_v"))
    gbig["ssm_w_g"] = shard_cols(_wgrad(yg, dsg, "dw_g"))
    dv, dgl, ddw, d_dw_b = _conv_bwd(dz1, vg, dw_pad, "conv_bwd")
    duf, dbre, dbim, dcre, dcim, dlam, d_ssm_d = _ssm_bwd(
        uf, dyss, bre_bd, bim_bd, cre_bd, cim_bd, lamp, ssm_d, "ssm_bwd")
    dh1, u_b, d_mix_norm = _mix_in_bwd(h1, mix_norm, dh2, dv, dgl, duf, dgate, w_in_f, "mix_in_bwd")
    gbig["w_in"] = shard_cols(jnp.concatenate(
        [_wgrad(u_b, dv, "dw_in_v"), _wgrad(u_b, dgl, "dw_in_g"), _wgrad(u_b, duf, "dw_in_u"),
         _wgrad(u_b, dgate, "dw_in_gate")], axis=1))
    dh0, da1, db1, s1, n1, d_ffn1_norm = _ffn_bwd(
        h0, ffn1_norm, dh1, a1, b1, gw["ffn1_w1"], gw["ffn1_w3"], gw["ffn1_w2"], "ffn1_bwd")
    gbig["ffn1_w1"] = _wgrad(n1, da1, "ffn1_dw1")
    gbig["ffn1_w3"] = _wgrad(n1, db1, "ffn1_dw3")
    gbig["ffn1_w2"] = _wgrad(s1, dh1, "ffn1_dw2", 0.5)
    grad_x = dh0[FRONT:][None]

    d_bbr = _diag_blocks(dbre.reshape(NQ, 8, H, 8, P)).transpose(0, 2, 1)
    d_bbi = _diag_blocks(dbim.reshape(NQ, 8, H, 8, P)).transpose(0, 2, 1)
    d_c_re = _diag_blocks(dcre.reshape(NQ, 8, P, 8, H)).transpose(0, 2, 1)
    d_c_im = _diag_blocks(dcim.reshape(NQ, 8, P, 8, H)).transpose(0, 2, 1)
    d_lbr = dlam[:, 0, :].reshape(G, P)
    d_lbi = dlam[:, 1, :].reshape(G, P)
    d_lam_re, d_lam_im, d_log_dt, d_b_re, d_b_im = disc_vjp((d_lbr, d_lbi, d_bbr, d_bbi))

    recv = _scatter_halves([gbig[n] for n in big], "scatter_grads")
    halves = [_sum8(r, "sum_" + n) for n, r in zip(big, recv)]
    full = _swap_halves(halves, "swap_halves")
    out_g, out_d, out_m, out_v = {}, {}, {}, {}
    for n, f in zip(big, full):
        shp = args[n].shape
        g2 = f.reshape(f.shape[0] * f.shape[1], f.shape[2])
        d2, m2, v2 = _adamw(args[n].reshape(g2.shape), g2, args["m_" + n].reshape(g2.shape),
                            args["v_" + n].reshape(g2.shape), "adamw_" + n)
        out_g[n], out_d[n], out_m[n], out_v[n] = (t.reshape(shp) for t in (g2, d2, m2, v2))

    sg = {"meta_tokens": dh0[FRONT - NMETA:FRONT], "ffn1_norm": d_ffn1_norm, "mix_norm": d_mix_norm,
          "b_gate": d_b_gate, "conv_dw": ddw[:KW], "conv_dw_b": d_dw_b, "conv_ln_g": d_ln_g, "conv_ln_b": d_ln_b,
          "ssm_lam_re": d_lam_re, "ssm_lam_im": d_lam_im, "ssm_log_dt": d_log_dt, "ssm_b_re": d_b_re,
          "ssm_b_im": d_b_im, "ssm_c_re": d_c_re, "ssm_c_im": d_c_im, "ssm_d": d_ssm_d,
          "ffn2_norm": d_ffn2_norm, "final_norm": d_final}
    sshapes = [sg[n].shape for n in small]
    tot = _sum8(_gather_all(_pack([sg[n] for n in small]), "gather_small_grads"), "sum_small")
    sgr = dict(zip(small, _unpack(tot, sshapes)))
    sgr["meta_tokens"] = lax.dynamic_slice_in_dim(sgr["meta_tokens"], chip * (D // NSH), D // NSH, axis=1)
    sgr["conv_dw"] = lax.dynamic_slice_in_dim(sgr["conv_dw"], chip * (DC // NSH), DC // NSH, axis=1)
    pshapes = [args[n].shape for n in small]
    d_s, m_s, v_s = _adamw(_pack([args[n] for n in small]), _pack([sgr[n] for n in small]),
                           _pack([args["m_" + n] for n in small]), _pack([args["v_" + n] for n in small]),
                           "adamw_small")
    for n, g_, d_, m_, v_ in zip(small, [sgr[n] for n in small], _unpack(d_s, pshapes),
                                 _unpack(m_s, pshapes), _unpack(v_s, pshapes)):
        out_g[n], out_d[n], out_m[n], out_v[n] = g_.reshape(args[n].shape), d_, m_, v_

    loss = lax.psum(loss_part[0, 0], ("x", "y", "c"))
    return (loss, grad_x, *[out_g[n] for n in names], *[out_d[n] for n in names],
            *[out_m[n] for n in names], *[out_v[n] for n in names])
```

```python
import math

import jax
import jax.numpy as jnp
from jax import lax
from jax.experimental import pallas as pl
from jax.experimental.pallas import tpu as pltpu

F32 = jnp.float32
BF16 = jnp.bfloat16

D = 1024
NSH = 4
F = 2816
FS = F // NSH
DC = 512
DS = 512
KW = 31
KWP = 32
NMETA = 16
FRONT = 128
G, P, H = 32, 64, 16
NST = G * P
NQ = 4
QS = NST // NQ
QU = DS // NQ
NSEG = 8
EPS = 1e-6
LR, B1, B2, AEPS, WD, STEP = 1e-3, 0.9, 0.999, 1e-8, 0.01, 10
VMEM_LIMIT = 58 * 1024 * 1024
MESH = pl.DeviceIdType.MESH
ANY = pl.BlockSpec(memory_space=pl.ANY)


def _params(*sem):
    return pltpu.CompilerParams(dimension_semantics=sem, vmem_limit_bytes=VMEM_LIMIT)


def _res(shape):
    nd = len(shape)
    return pl.BlockSpec(shape, lambda *_: (0,) * nd, pipeline_mode=pl.Buffered(1))


def _tile(n, cap, mult=16):
    best = None
    for t in range(mult, min(n, cap) + 1, mult):
        if n % t == 0:
            best = t
    assert best is not None, (n, cap, mult)
    return best


def _dot(a, b):
    return jnp.dot(a, b, preferred_element_type=F32)


def _dot_nt(a, b):
    return lax.dot_general(a, b, (((1,), (1,)), ((), ())), preferred_element_type=F32)


def _dot_tn(a, b):
    return lax.dot_general(a, b, (((0,), (0,)), ((), ())), preferred_element_type=F32)


def _sigmoid(x):
    return 1.0 / (1.0 + jnp.exp(-x))


_GC = math.sqrt(2.0 / math.pi)
_GA = 0.044715


def _gelu(x):
    return 0.5 * x * (1.0 + jnp.tanh(_GC * (x + _GA * x * x * x)))


def _gelu_grad(x):
    t = jnp.tanh(_GC * (x + _GA * x * x * x))
    return 0.5 * (1.0 + t) + 0.5 * x * (1.0 - t * t) * _GC * (1.0 + 3.0 * _GA * x * x)


def _rms(hv, g):
    r = lax.rsqrt(jnp.mean(hv * hv, axis=-1, keepdims=True) + EPS)
    return hv * r * g, r


def _rms_bwd(dn, hv, r, g):
    xh = hv * r
    dxh = dn * g
    return r * (dxh - xh * jnp.mean(dxh * xh, axis=-1, keepdims=True)), xh


def _acc_rows(ref, part, first):
    @pl.when(first)
    def _():
        ref[...] = part

    @pl.when(jnp.logical_not(first))
    def _():
        ref[...] += part


def _coords():
    return lax.axis_index("x"), lax.axis_index("y"), lax.axis_index("c")


def _flip(v, d):
    return 1 - v if d else v


def _run(local, remote):
    for cp in local + remote:
        cp.start()
    for cp in remote:
        cp.wait()
    for cp in local:
        cp.wait()


_REL3 = ((1, 0), (0, 1), (1, 1))


def _gather_chips(shards, name):
    n = len(shards)

    def body(*refs):
        ins, outs = refs[:n], refs[n:2 * n]
        send, recv, fsend, frecv, loc = refs[2 * n:]
        x, y, c = _coords()
        me = 2 * x + y
        own = [pltpu.make_async_copy(ins[t], outs[t].at[me], loc.at[t]) for t in range(n)]
        first, passed = [], []
        for t in range(n):
            half = shards[t].shape[0] // 2
            mine, theirs = pl.ds(c * half, half), pl.ds((1 - c) * half, half)
            for k, (dx, dy) in enumerate(_REL3):
                px, py = _flip(x, dx), _flip(y, dy)
                first.append(pltpu.make_async_remote_copy(
                    src_ref=ins[t].at[mine], dst_ref=outs[t].at[me, mine],
                    send_sem=send.at[3 * t + k], recv_sem=recv.at[3 * t + k],
                    device_id=(px, py, c), device_id_type=MESH))
                passed.append((
                    pltpu.make_async_remote_copy(
                        src_ref=outs[t].at[2 * px + py, mine], dst_ref=outs[t].at[2 * px + py, mine],
                        send_sem=fsend.at[3 * t + k], recv_sem=frecv.at[3 * t + k],
                        device_id=(x, y, 1 - c), device_id_type=MESH),
                    pltpu.make_async_remote_copy(
                        src_ref=outs[t].at[2 * px + py, theirs], dst_ref=outs[t].at[2 * px + py, theirs],
                        send_sem=fsend.at[3 * t + k], recv_sem=frecv.at[3 * t + k],
                        device_id=(x, y, 1 - c), device_id_type=MESH)))
        for cp in own + first:
            cp.start()
        for cp, (fwd, _) in zip(first, passed):
            cp.wait_recv()
            fwd.start()
        for cp, (fwd, back) in zip(first, passed):
            cp.wait_send()
            fwd.wait_send()
            back.wait_recv()
        for cp in own:
            cp.wait()

    return pl.pallas_call(
        body, name=name,
        out_shape=[jax.ShapeDtypeStruct((NSH,) + s.shape, s.dtype) for s in shards],
        in_specs=[ANY] * n, out_specs=[ANY] * n,
        scratch_shapes=[pltpu.SemaphoreType.DMA((3 * n,)) for _ in range(4)] + [pltpu.SemaphoreType.DMA((n,))],
    )(*shards)


_REL7 = tuple((dx, dy, dc) for dx in (0, 1) for dy in (0, 1) for dc in (0, 1))[1:]


def _gather_all(a, name):
    def body(a_ref, o_ref, send, recv, loc):
        x, y, c = _coords()
        me = 4 * x + 2 * y + c
        local = [pltpu.make_async_copy(a_ref, o_ref.at[me], loc.at[0])]
        remote = [pltpu.make_async_remote_copy(
            src_ref=a_ref, dst_ref=o_ref.at[me], send_sem=send.at[k], recv_sem=recv.at[k],
            device_id=(_flip(x, dx), _flip(y, dy), _flip(c, dc)), device_id_type=MESH)
            for k, (dx, dy, dc) in enumerate(_REL7)]
        _run(local, remote)

    return pl.pallas_call(
        body, name=name,
        out_shape=jax.ShapeDtypeStruct((8,) + a.shape, a.dtype),
        in_specs=[ANY], out_specs=ANY,
        scratch_shapes=[pltpu.SemaphoreType.DMA((7,)), pltpu.SemaphoreType.DMA((7,)),
                        pltpu.SemaphoreType.DMA((1,))],
    )(a)


def _pair_exchange(grads, name):
    n = len(grads)

    def body(*refs):
        ins, outs = refs[:n], refs[n:2 * n]
        send, recv = refs[2 * n:]
        x, y, c = _coords()
        remote = []
        for t in range(n):
            half = grads[t].shape[1] // 2
            remote.append(pltpu.make_async_remote_copy(
                src_ref=ins[t].at[:, pl.ds((1 - c) * half, half)], dst_ref=outs[t],
                send_sem=send.at[t], recv_sem=recv.at[t],
                device_id=(x, y, 1 - c), device_id_type=MESH))
        _run([], remote)

    return pl.pallas_call(
        body, name=name,
        out_shape=[jax.ShapeDtypeStruct((NSH, g.shape[1] // 2, g.shape[2]), g.dtype) for g in grads],
        in_specs=[ANY] * n, out_specs=[ANY] * n,
        scratch_shapes=[pltpu.SemaphoreType.DMA((n,)), pltpu.SemaphoreType.DMA((n,))],
    )(*grads)


def _scatter_chips(sums, name):
    n = len(sums)

    def body(*refs):
        ins, outs = refs[:n], refs[n:2 * n]
        send, recv, loc = refs[2 * n:]
        x, y, c = _coords()
        me = 2 * x + y
        local = [pltpu.make_async_copy(ins[t].at[me], outs[t].at[me], loc.at[t]) for t in range(n)]
        remote = []
        for t in range(n):
            for k, (dx, dy) in enumerate(_REL3):
                px, py = _flip(x, dx), _flip(y, dy)
                remote.append(pltpu.make_async_remote_copy(
                    src_ref=ins[t].at[2 * px + py], dst_ref=outs[t].at[me],
                    send_sem=send.at[3 * t + k], recv_sem=recv.at[3 * t + k],
                    device_id=(px, py, c), device_id_type=MESH))
        _run(local, remote)

    return pl.pallas_call(
        body, name=name,
        out_shape=[jax.ShapeDtypeStruct(s.shape, s.dtype) for s in sums],
        in_specs=[ANY] * n, out_specs=[ANY] * n,
        scratch_shapes=[pltpu.SemaphoreType.DMA((3 * n,)), pltpu.SemaphoreType.DMA((3 * n,)),
                        pltpu.SemaphoreType.DMA((n,))],
    )(*sums)


def _swap_halves(halves, name):
    n = len(halves)

    def body(*refs):
        ins, outs = refs[:n], refs[n:2 * n]
        send, recv, loc = refs[2 * n:]
        x, y, c = _coords()
        local = [pltpu.make_async_copy(ins[t], outs[t].at[c], loc.at[t]) for t in range(n)]
        remote = [pltpu.make_async_remote_copy(
            src_ref=ins[t], dst_ref=outs[t].at[c], send_sem=send.at[t], recv_sem=recv.at[t],
            device_id=(x, y, 1 - c), device_id_type=MESH) for t in range(n)]
        _run(local, remote)

    return pl.pallas_call(
        body, name=name,
        out_shape=[jax.ShapeDtypeStruct((2,) + h.shape, h.dtype) for h in halves],
        in_specs=[ANY] * n, out_specs=[ANY] * n,
        scratch_shapes=[pltpu.SemaphoreType.DMA((n,)), pltpu.SemaphoreType.DMA((n,)),
                        pltpu.SemaphoreType.DMA((n,))],
    )(*halves)


def _sum_slots(r, name):
    K, R, C = r.shape
    tr = _tile(R, max(16, (1 << 22) // (K * C)))

    def body(r_ref, o_ref):
        acc = r_ref[0].astype(F32)
        for k in range(1, K):
            acc = acc + r_ref[k].astype(F32)
        o_ref[...] = acc

    return pl.pallas_call(
        body, name=name, grid=(R // tr,),
        out_shape=jax.ShapeDtypeStruct((R, C), F32),
        in_specs=[pl.BlockSpec((K, tr, C), lambda i: (0, i, 0))],
        out_specs=pl.BlockSpec((tr, C), lambda i: (i, 0)),
        compiler_params=_params("parallel"),
    )(r)


def _add_pair(a, b, name):
    R, C = a.shape
    tr = _tile(R, max(16, (1 << 20) // C))

    def body(a_ref, b_ref, o_ref):
        o_ref[...] = (a_ref[...].astype(F32) + b_ref[...].astype(F32)).astype(BF16)

    spec = pl.BlockSpec((tr, C), lambda i: (i, 0))
    return pl.pallas_call(
        body, name=name, grid=(R // tr,),
        out_shape=jax.ShapeDtypeStruct((R, C), BF16),
        in_specs=[spec, spec], out_specs=spec,
        compiler_params=_params("parallel"),
    )(a, b)


def _adamw(w, g, m, v, name):
    R, C = w.shape
    tr = _tile(R, max(8, (1 << 18) // C), 8)
    c1 = 1.0 / (1.0 - B1 ** STEP)
    c2 = 1.0 / (1.0 - B2 ** STEP)

    def body(w_ref, g_ref, m_ref, v_ref, d_ref, nm_ref, nv_ref):
        gv = g_ref[...]
        nm = B1 * m_ref[...] + (1.0 - B1) * gv
        nv = B2 * v_ref[...] + (1.0 - B2) * gv * gv
        nm_ref[...] = nm
        nv_ref[...] = nv
        d_ref[...] = -LR * ((nm * c1) / (jnp.sqrt(nv * c2) + AEPS) + WD * w_ref[...])

    spec = pl.BlockSpec((tr, C), lambda i: (i, 0))
    return pl.pallas_call(
        body, name=name, grid=(R // tr,),
        out_shape=[jax.ShapeDtypeStruct((R, C), F32)] * 3,
        in_specs=[spec] * 4, out_specs=[spec] * 3,
        compiler_params=_params("parallel"),
    )(w, g, m, v)


def _ffn_fwd(h, g, w1, w3, w2, name):
    L = h.shape[0]
    tm = _tile(L, 528)

    def body(h_ref, g_ref, w1_ref, w3_ref, w2_ref, o_ref, a_ref, b_ref, n_s, acc_s):
        j = pl.program_id(1)

        @pl.when(j == 0)
        def _():
            hv = h_ref[...]
            n, _ = _rms(hv, g_ref[...])
            n_s[...] = n.astype(BF16)
            acc_s[...] = hv

        n = n_s[...]
        a = _dot_nt(n, w1_ref[j])
        b = _dot_nt(n, w3_ref[j])
        a_ref[0] = a.astype(BF16)
        b_ref[0] = b.astype(BF16)
        s = (a * _sigmoid(a) * b).astype(BF16)
        acc_s[...] += 0.5 * _dot(s, w2_ref[j])

        @pl.when(j == NSH - 1)
        def _():
            o_ref[...] = acc_s[...]

    row = pl.BlockSpec((tm, D), lambda i, j: (i, 0))
    hid = pl.BlockSpec((1, tm, FS), lambda i, j: (j, i, 0))
    return pl.pallas_call(
        body, name=name, grid=(L // tm, NSH),
        out_shape=[jax.ShapeDtypeStruct((L, D), F32),
                   jax.ShapeDtypeStruct((NSH, L, FS), BF16), jax.ShapeDtypeStruct((NSH, L, FS), BF16)],
        in_specs=[row, _res((1, D)), _res((NSH, FS, D)), _res((NSH, FS, D)), _res((NSH, FS, D))],
        out_specs=[row, hid, hid],
        scratch_shapes=[pltpu.VMEM((tm, D), BF16), pltpu.VMEM((tm, D), F32)],
        compiler_params=_params("arbitrary", "arbitrary"),
    )(h, g, w1, w3, w2)


def _ffn_bwd(h, g, dout, a, b, w1, w3, w2, name):
    L = h.shape[0]
    tm = _tile(L, 352)

    def body(h_ref, g_ref, do_ref, a_ref, b_ref, w1_ref, w3_ref, w2_ref,
             dh_ref, da_ref, db_ref, s_ref, n_ref, dg_ref, dob_s, dn_s):
        i, j = pl.program_id(0), pl.program_id(1)

        @pl.when(j == 0)
        def _():
            n, _ = _rms(h_ref[...], g_ref[...])
            n_ref[...] = n.astype(BF16)
            dob_s[...] = do_ref[...].astype(BF16)
            dn_s[...] = jnp.zeros_like(dn_s)

        av = a_ref[0].astype(F32)
        bv = b_ref[0].astype(F32)
        sig = _sigmoid(av)
        sa = av * sig
        ds = 0.5 * _dot_nt(dob_s[...], w2_ref[j])
        s_ref[0] = (sa * bv).astype(BF16)
        da = (ds * bv * sig * (1.0 + av * (1.0 - sig))).astype(BF16)
        db = (ds * sa).astype(BF16)
        da_ref[0] = da
        db_ref[0] = db
        dn_s[...] += _dot(da, w1_ref[j]) + _dot(db, w3_ref[j])

        @pl.when(j == NSH - 1)
        def _():
            hv = h_ref[...]
            gv = g_ref[...]
            r = lax.rsqrt(jnp.mean(hv * hv, axis=-1, keepdims=True) + EPS)
            dn = dn_s[...]
            dx, xh = _rms_bwd(dn, hv, r, gv)
            dh_ref[...] = do_ref[...] + dx
            _acc_rows(dg_ref, jnp.sum(dn * xh, axis=0, keepdims=True), i == 0)

    row = pl.BlockSpec((tm, D), lambda i, j: (i, 0))
    hid = pl.BlockSpec((1, tm, FS), lambda i, j: (j, i, 0))
    return pl.pallas_call(
        body, name=name, grid=(L // tm, NSH),
        out_shape=[jax.ShapeDtypeStruct((L, D), F32)]
        + [jax.ShapeDtypeStruct((NSH, L, FS), BF16)] * 3
        + [jax.ShapeDtypeStruct((L, D), BF16), jax.ShapeDtypeStruct((1, D), F32)],
        in_specs=[row, _res((1, D)), row, hid, hid,
                  _res((NSH, FS, D)), _res((NSH, FS, D)), _res((NSH, FS, D))],
        out_specs=[row, hid, hid, hid, row, pl.BlockSpec((1, D), lambda i, j: (0, 0))],
        scratch_shapes=[pltpu.VMEM((tm, D), BF16), pltpu.VMEM((tm, D), F32)],
        compiler_params=_params("arbitrary", "arbitrary"),
    )(h, g, dout, a, b, w1, w3, w2)


def _wgrad(xm, ym, name, scale=1.0):
    xs, ys = xm.ndim == 3, ym.ndim == 3
    assert not (xs and ys)
    L = xm.shape[-2]
    K, N = xm.shape[-1], ym.shape[-1]
    tl = _tile(L, 1056)
    nl = L // tl
    if xs or ys:
        tn, grid_n = N, NSH
    else:
        tn = _tile(N, 1024, 128)
        grid_n = N // tn

    def body(x_ref, y_ref, o_ref, acc_s):
        l = pl.program_id(1)
        xv = x_ref[0] if xs else x_ref[...]
        yv = y_ref[0] if ys else y_ref[...]
        part = _dot_tn(xv.astype(BF16), yv.astype(BF16))
        _acc_rows(acc_s, part, l == 0)

        @pl.when(l == nl - 1)
        def _():
            res = (acc_s[...] * scale).astype(BF16)
            if xs or ys:
                o_ref[0] = res
            else:
                o_ref[...] = res

    if xs:
        x_spec = pl.BlockSpec((1, tl, K), lambda n, l: (n, l, 0))
        y_spec = pl.BlockSpec((tl, N), lambda n, l: (l, 0))
        o_spec = pl.BlockSpec((1, K, N), lambda n, l: (n, 0, 0))
        o_shape = (NSH, K, N)
    elif ys:
        x_spec = pl.BlockSpec((tl, K), lambda n, l: (l, 0))
        y_spec = pl.BlockSpec((1, tl, N), lambda n, l: (n, l, 0))
        o_spec = pl.BlockSpec((1, K, N), lambda n, l: (n, 0, 0))
        o_shape = (NSH, K, N)
    else:
        x_spec = pl.BlockSpec((tl, K), lambda n, l: (l, 0))
        y_spec = pl.BlockSpec((tl, tn), lambda n, l: (l, n))
        o_spec = pl.BlockSpec((K, tn), lambda n, l: (0, n))
        o_shape = (K, N)
    return pl.pallas_call(
        body, name=name, grid=(grid_n, nl),
        out_shape=jax.ShapeDtypeStruct(o_shape, BF16),
        in_specs=[x_spec, y_spec], out_specs=o_spec,
        scratch_shapes=[pltpu.VMEM((K, tn), F32)],
        compiler_params=_params("parallel", "arbitrary"),
    )(xm, ym)


def _mix_in_fwd(h, g, w_in, b_gate, name):
    L = h.shape[0]
    tm = _tile(L, 528)

    def body(h_ref, g_ref, w_ref, bg_ref, vg_ref, uf_ref, gt_ref):
        u, _ = _rms(h_ref[...], g_ref[...])
        ub = u.astype(BF16)
        vg_ref[...] = _dot(ub, w_ref[:, 0:2 * DC]).astype(BF16)
        uf_ref[...] = _dot(ub, w_ref[:, 2 * DC:2 * DC + DS]).astype(BF16)
        gt_ref[...] = _sigmoid(_dot(ub, w_ref[:, 2 * DC + DS:]) + bg_ref[...]).astype(BF16)

    def row(n):
        return pl.BlockSpec((tm, n), lambda i: (i, 0))

    din = w_in.shape[1]
    return pl.pallas_call(
        body, name=name, grid=(L // tm,),
        out_shape=[jax.ShapeDtypeStruct((L, 2 * DC), BF16), jax.ShapeDtypeStruct((L, DS), BF16),
                   jax.ShapeDtypeStruct((L, 2 * D), BF16)],
        in_specs=[row(D), _res((1, D)), _res((D, din)), _res((1, 2 * D))],
        out_specs=[row(2 * DC), row(DS), row(2 * D)],
        compiler_params=_params("parallel"),
    )(h, g, w_in, b_gate)


def _mix_in_bwd(h, g, dres, dv, dgl, duf, dgate, w_in, name):
    L = h.shape[0]
    tm = _tile(L, 528)

    def body(h_ref, g_ref, dr_ref, dv_ref, dgl_ref, duf_ref, dgt_ref, w_ref, dh_ref, u_ref, dgm_ref):
        i = pl.program_id(0)
        hv = h_ref[...]
        gv = g_ref[...]
        u, r = _rms(hv, gv)
        u_ref[...] = u.astype(BF16)
        du = (_dot_nt(dv_ref[...], w_ref[:, 0:DC]) + _dot_nt(dgl_ref[...], w_ref[:, DC:2 * DC])
              + _dot_nt(duf_ref[...], w_ref[:, 2 * DC:2 * DC + DS])
              + _dot_nt(dgt_ref[...], w_ref[:, 2 * DC + DS:]))
        dx, xh = _rms_bwd(du, hv, r, gv)
        dh_ref[...] = dr_ref[...] + dx
        _acc_rows(dgm_ref, jnp.sum(du * xh, axis=0, keepdims=True), i == 0)

    def row(n):
        return pl.BlockSpec((tm, n), lambda i: (i, 0))

    din = w_in.shape[1]
    return pl.pallas_call(
        body, name=name, grid=(L // tm,),
        out_shape=[jax.ShapeDtypeStruct((L, D), F32), jax.ShapeDtypeStruct((L, D), BF16),
                   jax.ShapeDtypeStruct((1, D), F32)],
        in_specs=[row(D), _res((1, D)), row(D), row(DC), row(DC), row(DS), row(2 * D), _res((D, din))],
        out_specs=[row(D), row(D), pl.BlockSpec((1, D), lambda i: (0, 0))],
        compiler_params=_params("arbitrary"),
    )(h, g, dres, dv, dgl, duf, dgate, w_in)


def _conv_fwd(vg, dw, dwb, name):
    L = vg.shape[0]
    nc = DC // 128

    def body(v_ref, g_ref, dw_ref, dwb_ref, z_ref, zp_s):
        zp_s[0:KWP, :] = jnp.zeros((KWP, 128), F32)
        zp_s[KWP:, :] = v_ref[...].astype(F32) * _sigmoid(g_ref[...].astype(F32))
        acc = jnp.broadcast_to(dwb_ref[...], (L, 128))
        for k in range(KW):
            acc = acc + dw_ref[k:k + 1, :] * zp_s[pl.ds(k + 2, L), :]
        z_ref[...] = acc

    return pl.pallas_call(
        body, name=name, grid=(nc,),
        out_shape=jax.ShapeDtypeStruct((L, DC), F32),
        in_specs=[pl.BlockSpec((L, 128), lambda c: (0, c)), pl.BlockSpec((L, 128), lambda c: (0, nc + c)),
                  pl.BlockSpec((KWP, 128), lambda c: (0, c)), pl.BlockSpec((1, 128), lambda c: (0, c))],
        out_specs=pl.BlockSpec((L, 128), lambda c: (0, c)),
        scratch_shapes=[pltpu.VMEM((L + KWP, 128), F32)],
        compiler_params=_params("parallel"),
    )(vg, vg, dw, dwb)


def _conv_bwd(dz1, vg, dw, name):
    L = vg.shape[0]
    nc = DC // 128

    def body(dz_ref, v_ref, g_ref, dw_ref, dv_ref, dg_ref, ddw_ref, ddwb_ref, zp_s, dzp_s):
        vv = v_ref[...].astype(F32)
        sg = _sigmoid(g_ref[...].astype(F32))
        zp_s[0:KWP, :] = jnp.zeros((KWP, 128), F32)
        zp_s[KWP:, :] = vv * sg
        dz = dz_ref[...]
        dzp_s[0:L, :] = dz
        dzp_s[L:, :] = jnp.zeros((KWP, 128), F32)
        ddwb_ref[...] = jnp.sum(dz, axis=0, keepdims=True)
        acc = jnp.zeros((L, 128), F32)
        for k in range(KW):
            acc = acc + dw_ref[k:k + 1, :] * dzp_s[pl.ds(KW - 1 - k, L), :]
            ddw_ref[k:k + 1, :] = jnp.sum(dz * zp_s[pl.ds(k + 2, L), :], axis=0, keepdims=True)
        ddw_ref[KW:KWP, :] = jnp.zeros((KWP - KW, 128), F32)
        dv_ref[...] = (acc * sg).astype(BF16)
        dg_ref[...] = (acc * vv * sg * (1.0 - sg)).astype(BF16)

    col = pl.BlockSpec((L, 128), lambda c: (0, c))
    return pl.pallas_call(
        body, name=name, grid=(nc,),
        out_shape=[jax.ShapeDtypeStruct((L, DC), BF16), jax.ShapeDtypeStruct((L, DC), BF16),
                   jax.ShapeDtypeStruct((KWP, DC), F32), jax.ShapeDtypeStruct((1, DC), F32)],
        in_specs=[col, col, pl.BlockSpec((L, 128), lambda c: (0, nc + c)),
                  pl.BlockSpec((KWP, 128), lambda c: (0, c))],
        out_specs=[col, col, pl.BlockSpec((KWP, 128), lambda c: (0, c)), pl.BlockSpec((1, 128), lambda c: (0, c))],
        scratch_shapes=[pltpu.VMEM((L + KWP, 128), F32), pltpu.VMEM((L + KWP, 128), F32)],
        compiler_params=_params("parallel"),
    )(dz1, vg, vg, dw)


NLB = QS // 128


def _lb_store(ref, rows, val):
    for cb in range(NLB):
        ref[cb, rows, :] = val[:, cb * 128:(cb + 1) * 128]


def _lb_load(ref, rows):
    return jnp.concatenate([ref[cb, rows, :] for cb in range(NLB)], axis=1)


def _scan(xr_ref, xi_ref, base, T, ar, ai, atr, ati, reverse, sr_ref=None, si_ref=None):
    W = ar.shape[1]
    ar, ai = jnp.broadcast_to(ar, (NSEG, W)), jnp.broadcast_to(ai, (NSEG, W))
    atr, ati = jnp.broadcast_to(atr, (NSEG, W)), jnp.broadcast_to(ati, (NSEG, W))
    zero = jnp.zeros((NSEG, W), F32)

    def rows(t):
        tt = T - 1 - t if reverse else t
        return pl.ds(base + tt, NSEG, stride=T)

    def step(t, carry):
        sr, si = carry
        idx = rows(t)
        return (ar * sr - ai * si + _lb_load(xr_ref, idx), ar * si + ai * sr + _lb_load(xi_ref, idx))

    er, ei = lax.fori_loop(0, T, step, (zero, zero))
    seg = lax.broadcasted_iota(jnp.int32, (NSEG, W), 0)
    edge = seg == (NSEG - 1 if reverse else 0)
    cr, ci = zero, zero
    for _ in range(NSEG - 1):
        nr = atr * cr - ati * ci + er
        ni = atr * ci + ati * cr + ei
        shift = NSEG - 1 if reverse else 1
        cr = jnp.where(edge, 0.0, pltpu.roll(nr, shift, 0))
        ci = jnp.where(edge, 0.0, pltpu.roll(ni, shift, 0))

    if sr_ref is None:
        def step2(t, carry):
            sr, si = step(t, carry)
            idx = rows(t)
            _lb_store(xr_ref, idx, sr)
            _lb_store(xi_ref, idx, si)
            return sr, si

        lax.fori_loop(0, T, step2, (cr, ci))
        return None

    def step3(t, carry):
        sr, si, qr, qi = carry
        sr, si = step(t, (sr, si))
        idx = rows(t)
        _lb_store(xr_ref, idx, sr)
        _lb_store(xi_ref, idx, si)
        prev = pl.ds(NSEG - 1 + T - 1 - t, NSEG, stride=T)
        pr, pi = _lb_load(sr_ref, prev), _lb_load(si_ref, prev)
        return sr, si, qr + sr * pr + si * pi, qi + si * pr - sr * pi

    _, _, qr, qi = lax.fori_loop(0, T, step3, (cr, ci, zero, zero))
    return jnp.sum(qr, axis=0, keepdims=True), jnp.sum(qi, axis=0, keepdims=True)


def _ssm_fwd(uf, bre, bim, cre, cim, lamp, dsk, name):
    L = uf.shape[0]
    T = L // NSEG

    def body(u_ref, bre_ref, bim_ref, cre_ref, cim_ref, lam_ref, d_ref, y_ref, sr_s, si_s):
        for k in range(NSEG):
            sl = slice(k * T, (k + 1) * T)
            uk = u_ref[sl, :]
            _lb_store(sr_s, sl, _dot(uk, bre_ref[...]))
            _lb_store(si_s, sl, _dot(uk, bim_ref[...]))
        _scan(sr_s, si_s, 0, T, lam_ref[0:1, :], lam_ref[1:2, :], lam_ref[2:3, :], lam_ref[3:4, :], False)
        for k in range(NSEG):
            sl = slice(k * T, (k + 1) * T)
            y_ref[sl, :] = (_dot(_lb_load(sr_s, sl).astype(BF16), cre_ref[...])
                            - _dot(_lb_load(si_s, sl).astype(BF16), cim_ref[...])
                            + d_ref[...] * u_ref[sl, :].astype(F32))

    return pl.pallas_call(
        body, name=name, grid=(NQ,),
        out_shape=jax.ShapeDtypeStruct((L, DS), F32),
        in_specs=[pl.BlockSpec((L, QU), lambda q: (0, q)),
                  pl.BlockSpec((QU, QS), lambda q: (q, q)), pl.BlockSpec((QU, QS), lambda q: (q, q)),
                  pl.BlockSpec((QS, QU), lambda q: (q, q)), pl.BlockSpec((QS, QU), lambda q: (q, q)),
                  pl.BlockSpec((8, QS), lambda q: (0, q)), pl.BlockSpec((1, QU), lambda q: (0, q))],
        out_specs=pl.BlockSpec((L, QU), lambda q: (0, q)),
        scratch_shapes=[pltpu.VMEM((NLB, L, 128), F32), pltpu.VMEM((NLB, L, 128), F32)],
        compiler_params=_params("parallel"),
    )(uf, bre, bim, cre, cim, lamp, dsk)


def _ssm_bwd(uf, dyss, bre, bim, cre, cim, lamp, dsk, name):
    L = uf.shape[0]
    T = L // NSEG

    def body(u_ref, dy_ref, bre_ref, bim_ref, cre_ref, cim_ref, lam_ref, d_ref,
             du_ref, dbre_ref, dbim_ref, dcre_ref, dcim_ref, dlam_ref, dd_ref, sr_s, si_s, gr_s, gi_s):
        _lb_store(sr_s, slice(0, NSEG), jnp.zeros((NSEG, QS), F32))
        _lb_store(si_s, slice(0, NSEG), jnp.zeros((NSEG, QS), F32))
        for k in range(NSEG):
            sl = slice(k * T, (k + 1) * T)
            ss = slice(NSEG + k * T, NSEG + (k + 1) * T)
            uk = u_ref[sl, :]
            dyk = dy_ref[sl, :].astype(BF16)
            _lb_store(sr_s, ss, _dot(uk, bre_ref[...]))
            _lb_store(si_s, ss, _dot(uk, bim_ref[...]))
            _lb_store(gr_s, sl, _dot_nt(dyk, cre_ref[...]))
            _lb_store(gi_s, sl, -_dot_nt(dyk, cim_ref[...]))
        ar, ai, atr, ati = lam_ref[0:1, :], lam_ref[1:2, :], lam_ref[2:3, :], lam_ref[3:4, :]
        _scan(sr_s, si_s, NSEG, T, ar, ai, atr, ati, False)
        qr, qi = _scan(gr_s, gi_s, 0, T, ar, -ai, atr, -ati, True, sr_s, si_s)
        dlam_ref[0] = jnp.concatenate([qr, qi, jnp.zeros((6, QS), F32)], axis=0)
        dbre = jnp.zeros((QU, QS), F32)
        dbim = jnp.zeros((QU, QS), F32)
        dcre = jnp.zeros((QS, QU), F32)
        dcim = jnp.zeros((QS, QU), F32)
        dd = jnp.zeros((1, QU), F32)
        for k in range(NSEG):
            sl = slice(k * T, (k + 1) * T)
            ss = slice(NSEG + k * T, NSEG + (k + 1) * T)
            uk = u_ref[sl, :]
            dyk = dy_ref[sl, :]
            dyb = dyk.astype(BF16)
            grb = _lb_load(gr_s, sl).astype(BF16)
            gib = _lb_load(gi_s, sl).astype(BF16)
            du_ref[sl, :] = (_dot_nt(grb, bre_ref[...]) + _dot_nt(gib, bim_ref[...])
                             + dyk * d_ref[...]).astype(BF16)
            dbre = dbre + _dot_tn(uk, grb)
            dbim = dbim + _dot_tn(uk, gib)
            dcre = dcre + _dot_tn(_lb_load(sr_s, ss).astype(BF16), dyb)
            dcim = dcim - _dot_tn(_lb_load(si_s, ss).astype(BF16), dyb)
            dd = dd + jnp.sum(dyk * uk.astype(F32), axis=0, keepdims=True)
        dbre_ref[0] = dbre
        dbim_ref[0] = dbim
        dcre_ref[0] = dcre
        dcim_ref[0] = dcim
        dd_ref[...] = dd

    col = pl.BlockSpec((L, QU), lambda q: (0, q))
    bsp = pl.BlockSpec((QU, QS), lambda q: (q, q))
    csp = pl.BlockSpec((QS, QU), lambda q: (q, q))
    return pl.pallas_call(
        body, name=name, grid=(NQ,),
        out_shape=[jax.ShapeDtypeStruct((L, DS), BF16),
                   jax.ShapeDtypeStruct((NQ, QU, QS), F32), jax.ShapeDtypeStruct((NQ, QU, QS), F32),
                   jax.ShapeDtypeStruct((NQ, QS, QU), F32), jax.ShapeDtypeStruct((NQ, QS, QU), F32),
                   jax.ShapeDtypeStruct((NQ, 8, QS), F32), jax.ShapeDtypeStruct((1, DS), F32)],
        in_specs=[col, col, bsp, bsp, csp, csp,
                  pl.BlockSpec((8, QS), lambda q: (0, q)), pl.BlockSpec((1, QU), lambda q: (0, q))],
        out_specs=[col,
                   pl.BlockSpec((1, QU, QS), lambda q: (q, 0, 0)), pl.BlockSpec((1, QU, QS), lambda q: (q, 0, 0)),
                   pl.BlockSpec((1, QS, QU), lambda q: (q, 0, 0)), pl.BlockSpec((1, QS, QU), lambda q: (q, 0, 0)),
                   pl.BlockSpec((1, 8, QS), lambda q: (q, 0, 0)), pl.BlockSpec((1, QU), lambda q: (0, q))],
        scratch_shapes=[pltpu.VMEM((NLB, L + NSEG, 128), F32), pltpu.VMEM((NLB, L + NSEG, 128), F32),
                        pltpu.VMEM((NLB, L, 128), F32), pltpu.VMEM((NLB, L, 128), F32)],
        compiler_params=_params("parallel"),
    )(uf, dyss, bre, bim, cre, cim, lamp, dsk)


def _branches(z1_ref, yss_ref, gt_ref, lng_ref, lnb_ref, wp_ref, wv_ref, wg_ref):
    zf = z1_ref[...]
    mu = jnp.mean(zf, axis=-1, keepdims=True)
    zc = zf - mu
    rstd = lax.rsqrt(jnp.mean(zc * zc, axis=-1, keepdims=True) + EPS)
    zn = zc * rstd
    z2 = zn * lng_ref[...] + lnb_ref[...]
    sz = _sigmoid(z2)
    z3 = (z2 * sz).astype(BF16)
    y_conv = _dot(z3, wp_ref[...])
    yss = yss_ref[...]
    yg = _gelu(yss).astype(BF16)
    sv = _dot(yg, wv_ref[...])
    sig = _sigmoid(_dot(yg, wg_ref[...]))
    y_ssm = sv * sig
    gc = gt_ref[:, 0:D].astype(F32)
    gs = gt_ref[:, D:2 * D].astype(F32)
    m = gc * y_conv + gs * y_ssm
    return dict(rstd=rstd, zn=zn, z2=z2, sz=sz, z3=z3, y_conv=y_conv, yss=yss, yg=yg, sv=sv, sig=sig,
                y_ssm=y_ssm, gc=gc, gs=gs, m=m)


def _merge_fwd(h, z1, yss, gate, lng, lnb, wp, wv, wg, wo, name):
    L = h.shape[0]
    tm = _tile(L, 528)

    def body(h_ref, z1_ref, yss_ref, gt_ref, lng_ref, lnb_ref, wp_ref, wv_ref, wg_ref, wo_ref, o_ref):
        f = _branches(z1_ref, yss_ref, gt_ref, lng_ref, lnb_ref, wp_ref, wv_ref, wg_ref)
        o_ref[...] = h_ref[...] + _dot(f["m"].astype(BF16), wo_ref[...])

    def row(n):
        return pl.BlockSpec((tm, n), lambda i: (i, 0))

    return pl.pallas_call(
        body, name=name, grid=(L // tm,),
        out_shape=jax.ShapeDtypeStruct((L, D), F32),
        in_specs=[row(D), row(DC), row(DS), row(2 * D), _res((1, DC)), _res((1, DC)),
                  _res((DC, D)), _res((DS, D)), _res((DS, D)), _res((D, D))],
        out_specs=row(D),
        compiler_params=_params("parallel"),
    )(h, z1, yss, gate, lng, lnb, wp, wv, wg, wo)


def _merge_bwd(dh, z1, yss, gate, lng, lnb, wp, wv, wg, wo, name):
    L = dh.shape[0]
    tm = _tile(L, 352)

    def body(dh_ref, z1_ref, yss_ref, gt_ref, lng_ref, lnb_ref, wp_ref, wv_ref, wg_ref, wo_ref,
             m_ref, dgt_ref, dyc_ref, z3_ref, dz1_ref, yg_ref, dsv_ref, dsg_ref, dyss_ref,
             dbg_ref, dlng_ref, dlnb_ref):
        i = pl.program_id(0)
        f = _branches(z1_ref, yss_ref, gt_ref, lng_ref, lnb_ref, wp_ref, wv_ref, wg_ref)
        gc, gs, sig, sv = f["gc"], f["gs"], f["sig"], f["sv"]
        m_ref[...] = f["m"].astype(BF16)
        z3_ref[...] = f["z3"]
        yg_ref[...] = f["yg"]
        dm = _dot_nt(dh_ref[...].astype(BF16), wo_ref[...])
        dgc = (dm * f["y_conv"] * gc * (1.0 - gc)).astype(BF16)
        dgs = (dm * f["y_ssm"] * gs * (1.0 - gs)).astype(BF16)
        dgt_ref[:, 0:D] = dgc
        dgt_ref[:, D:2 * D] = dgs
        part = jnp.concatenate([jnp.sum(dgc.astype(F32), axis=0, keepdims=True),
                                jnp.sum(dgs.astype(F32), axis=0, keepdims=True)], axis=1)
        _acc_rows(dbg_ref, part, i == 0)
        dyc = (dm * gc).astype(BF16)
        dyc_ref[...] = dyc
        dys = dm * gs
        dsv = (dys * sig).astype(BF16)
        dsg = (dys * sv * sig * (1.0 - sig)).astype(BF16)
        dsv_ref[...] = dsv
        dsg_ref[...] = dsg
        dyg = _dot_nt(dsv, wv_ref[...]) + _dot_nt(dsg, wg_ref[...])
        dyss_ref[...] = dyg * _gelu_grad(f["yss"])
        dz3 = _dot_nt(dyc, wp_ref[...])
        z2, sz, zn = f["z2"], f["sz"], f["zn"]
        dz2 = dz3 * sz * (1.0 + z2 * (1.0 - sz))
        _acc_rows(dlng_ref, jnp.sum(dz2 * zn, axis=0, keepdims=True), i == 0)
        _acc_rows(dlnb_ref, jnp.sum(dz2, axis=0, keepdims=True), i == 0)
        dzn = dz2 * lng_ref[...]
        dz1_ref[...] = f["rstd"] * (dzn - jnp.mean(dzn, axis=-1, keepdims=True)
                                    - zn * jnp.mean(dzn * zn, axis=-1, keepdims=True))

    def row(n):
        return pl.BlockSpec((tm, n), lambda i: (i, 0))

    def tot(n):
        return pl.BlockSpec((1, n), lambda i: (0, 0))

    return pl.pallas_call(
        body, name=name, grid=(L // tm,),
        out_shape=[jax.ShapeDtypeStruct((L, D), BF16), jax.ShapeDtypeStruct((L, 2 * D), BF16),
                   jax.ShapeDtypeStruct((L, D), BF16), jax.ShapeDtypeStruct((L, DC), BF16),
                   jax.ShapeDtypeStruct((L, DC), F32), jax.ShapeDtypeStruct((L, DS), BF16),
                   jax.ShapeDtypeStruct((L, D), BF16), jax.ShapeDtypeStruct((L, D), BF16),
                   jax.ShapeDtypeStruct((L, DS), F32),
                   jax.ShapeDtypeStruct((1, 2 * D), F32), jax.ShapeDtypeStruct((1, DC), F32),
                   jax.ShapeDtypeStruct((1, DC), F32)],
        in_specs=[row(D), row(DC), row(DS), row(2 * D), _res((1, DC)), _res((1, DC)),
                  _res((DC, D)), _res((DS, D)), _res((DS, D)), _res((D, D))],
        out_specs=[row(D), row(2 * D), row(D), row(DC), row(DC), row(DS), row(D), row(D), row(DS),
                   tot(2 * D), tot(DC), tot(DC)],
        compiler_params=_params("arbitrary"),
    )(dh, z1, yss, gate, lng, lnb, wp, wv, wg, wo)


def _final(h, g, tgt, name):
    L = h.shape[0]
    tm = _tile(L, 528)

    def body(h_ref, g_ref, t_ref, dh_ref, loss_ref, dg_ref):
        i = pl.program_id(0)
        hv = h_ref[...]
        gv = g_ref[...]
        y, r = _rms(hv, gv)
        row = i * tm + lax.broadcasted_iota(jnp.int32, (tm, 1), 0)
        e = jnp.where(row >= FRONT, y - t_ref[...], 0.0)
        dy = e * (1.0 / D)
        part = 0.5 * jnp.sum(jnp.sum(e * dy, axis=1, keepdims=True), axis=0, keepdims=True)
        dx, xh = _rms_bwd(dy, hv, r, gv)
        dh_ref[...] = dx
        _acc_rows(loss_ref, part, i == 0)
        _acc_rows(dg_ref, jnp.sum(dy * xh, axis=0, keepdims=True), i == 0)

    row = pl.BlockSpec((tm, D), lambda i: (i, 0))
    return pl.pallas_call(
        body, name=name, grid=(L // tm,),
        out_shape=[jax.ShapeDtypeStruct((L, D), F32), jax.ShapeDtypeStruct((1, 1), F32),
                   jax.ShapeDtypeStruct((1, D), F32)],
        in_specs=[row, _res((1, D)), row],
        out_specs=[row, pl.BlockSpec((1, 1), lambda i: (0, 0)), pl.BlockSpec((1, D), lambda i: (0, 0))],
        compiler_params=_params("arbitrary"),
    )(h, g, tgt)


def _ssm_disc(lam_re, lam_im, log_dt, b_re, b_im):
    lam = lax.complex(lam_re, lam_im)
    dt = jnp.exp(log_dt)[:, None]
    lam_bar = jnp.exp(lam * dt)
    bbar = ((lam_bar - 1.0) / lam)[..., None] * lax.complex(b_re, b_im)
    return jnp.real(lam_bar), jnp.imag(lam_bar), jnp.real(bbar), jnp.imag(bbar)


def _bdiag_in(m):
    return jnp.einsum("gph,gk->ghkp", m, jnp.eye(G, dtype=m.dtype)).reshape(G * H, G * P)


def _bdiag_out(m):
    return jnp.einsum("ghp,gk->gpkh", m, jnp.eye(G, dtype=m.dtype)).reshape(G * P, G * H)


def _diag_blocks(m4):
    return jnp.einsum("qiaib->qiab", m4).reshape(G, m4.shape[2], m4.shape[4])


def _pack(parts, rows_mult=8):
    flat = jnp.concatenate([p.reshape(-1).astype(F32) for p in parts])
    n = flat.shape[0]
    tot = -(-n // (128 * rows_mult)) * (128 * rows_mult)
    return jnp.pad(flat, (0, tot - n)).reshape(tot // 128, 128)


def _unpack(buf, shapes):
    flat = buf.reshape(-1)
    out, o = [], 0
    for s in shapes:
        n = math.prod(s)
        out.append(flat[o:o + n].reshape(s))
        o += n
    return out


def kernel(x, meta_tokens, ffn1_norm, ffn1_w1, ffn1_w3, ffn1_w2, mix_norm, w_in, b_gate, conv_dw, conv_dw_b, conv_ln_g, conv_ln_b, conv_proj, ssm_lam_re, ssm_lam_im, ssm_log_dt, ssm_b_re, ssm_b_im, ssm_c_re, ssm_c_im, ssm_d, ssm_w_v, ssm_w_g, w_out, ffn2_norm, ffn2_w1, ffn2_w3, ffn2_w2, final_norm, loss_target, m_meta_tokens, m_ffn1_norm, m_ffn1_w1, m_ffn1_w3, m_ffn1_w2, m_mix_norm, m_w_in, m_b_gate, m_conv_dw, m_conv_dw_b, m_conv_ln_g, m_conv_ln_b, m_conv_proj, m_ssm_lam_re, m_ssm_lam_im, m_ssm_log_dt, m_ssm_b_re, m_ssm_b_im, m_ssm_c_re, m_ssm_c_im, m_ssm_d, m_ssm_w_v, m_ssm_w_g, m_w_out, m_ffn2_norm, m_ffn2_w1, m_ffn2_w3, m_ffn2_w2, m_final_norm, v_meta_tokens, v_ffn1_norm, v_ffn1_w1, v_ffn1_w3, v_ffn1_w2, v_mix_norm, v_w_in, v_b_gate, v_conv_dw, v_conv_dw_b, v_conv_ln_g, v_conv_ln_b, v_conv_proj, v_ssm_lam_re, v_ssm_lam_im, v_ssm_log_dt, v_ssm_b_re, v_ssm_b_im, v_ssm_c_re, v_ssm_c_im, v_ssm_d, v_ssm_w_v, v_ssm_w_g, v_w_out, v_ffn2_norm, v_ffn2_w1, v_ffn2_w3, v_ffn2_w2, v_final_norm):
    args = dict(locals())
    names = ["meta_tokens", "ffn1_norm", "ffn1_w1", "ffn1_w3", "ffn1_w2", "mix_norm", "w_in", "b_gate",
             "conv_dw", "conv_dw_b", "conv_ln_g", "conv_ln_b", "conv_proj", "ssm_lam_re", "ssm_lam_im",
             "ssm_log_dt", "ssm_b_re", "ssm_b_im", "ssm_c_re", "ssm_c_im", "ssm_d", "ssm_w_v", "ssm_w_g",
             "w_out", "ffn2_norm", "ffn2_w1", "ffn2_w3", "ffn2_w2", "final_norm"]
    big = ["ffn1_w1", "ffn1_w3", "ffn1_w2", "w_in", "conv_proj", "ssm_w_v", "ssm_w_g", "w_out",
           "ffn2_w1", "ffn2_w3", "ffn2_w2"]
    small = [n for n in names if n not in big]

    xs = x[0]
    S = xs.shape[0]
    L = FRONT + S
    T = L // NSEG
    jx, jy = lax.axis_index("x"), lax.axis_index("y")
    chip = 2 * jx + jy

    sm = _gather_all(_pack([meta_tokens, conv_dw[0]]), "gather_small")[0::2].reshape(NSH, -1)
    nmt = NMETA * (D // NSH)
    ndw = KW * (DC // NSH)
    meta_full = sm[:, :nmt].reshape(NSH, NMETA, D // NSH).transpose(1, 0, 2).reshape(NMETA, D)
    dw_full = sm[:, nmt:nmt + ndw].reshape(NSH, KW, DC // NSH).transpose(1, 0, 2).reshape(KW, DC)
    dw_pad = jnp.pad(dw_full, ((0, KWP - KW), (0, 0)))
    tposed = ("ffn1_w1", "ffn1_w3", "ffn2_w1", "ffn2_w3")
    gw = _gather_chips([(args[n][0].T if n in tposed else args[n][0]).astype(BF16) for n in big],
                       "gather_weights")
    gw = dict(zip(big, gw))

    def cols(w):
        return w.transpose(1, 0, 2).reshape(w.shape[1], -1)

    w_in_f = cols(gw["w_in"])
    wp_f, wv_f, wg_f = cols(gw["conv_proj"]), cols(gw["ssm_w_v"]), cols(gw["ssm_w_g"])
    wo_f = gw["w_out"].reshape(D, D)

    disc_in = (ssm_lam_re[0], ssm_lam_im[0], ssm_log_dt[0], ssm_b_re[0], ssm_b_im[0])
    (lbr, lbi, bbr, bbi), disc_vjp = jax.vjp(_ssm_disc, *disc_in)
    lam_t = jnp.exp(lax.complex(ssm_lam_re[0], ssm_lam_im[0]) * (jnp.exp(ssm_log_dt[0])[:, None] * T))
    lamp = jnp.concatenate([lbr.reshape(1, NST), lbi.reshape(1, NST), jnp.real(lam_t).reshape(1, NST),
                            jnp.imag(lam_t).reshape(1, NST), jnp.zeros((4, NST), F32)], axis=0)
    bre_bd, bim_bd = _bdiag_in(bbr).astype(BF16), _bdiag_in(bbi).astype(BF16)
    cre_bd, cim_bd = _bdiag_out(ssm_c_re[0]).astype(BF16), _bdiag_out(ssm_c_im[0]).astype(BF16)

    h0 = jnp.concatenate([jnp.zeros((FRONT - NMETA, D), F32), meta_full, xs], axis=0)
    tgt = jnp.pad(loss_target[0], ((FRONT, 0), (0, 0)))
    h1, a1, b1 = _ffn_fwd(h0, ffn1_norm, gw["ffn1_w1"], gw["ffn1_w3"], gw["ffn1_w2"], "ffn1_fwd")
    vg, uf, gate = _mix_in_fwd(h1, mix_norm, w_in_f, b_gate, "mix_in_fwd")
    z1 = _conv_fwd(vg, dw_pad, conv_dw_b, "conv_fwd")
    yss = _ssm_fwd(uf, bre_bd, bim_bd, cre_bd, cim_bd, lamp, ssm_d, "ssm_fwd")
    h2 = _merge_fwd(h1, z1, yss, gate, conv_ln_g, conv_ln_b, wp_f, wv_f, wg_f, wo_f, "merge_fwd")
    h3, a2, b2 = _ffn_fwd(h2, ffn2_norm, gw["ffn2_w1"], gw["ffn2_w3"], gw["ffn2_w2"], "ffn2_fwd")

    dh3, loss_part, d_final = _final(h3, final_norm.reshape(1, D), tgt, "final")
    dh2, da2, db2, s2, n2, d_ffn2_norm = _ffn_bwd(
        h2, ffn2_norm, dh3, a2, b2, gw["ffn2_w1"], gw["ffn2_w3"], gw["ffn2_w2"], "ffn2_bwd")
    gbig = {}
    gbig["ffn2_w1"] = _wgrad(da2, n2, "ffn2_dw1")
    gbig["ffn2_w3"] = _wgrad(db2, n2, "ffn2_dw3")
    gbig["ffn2_w2"] = _wgrad(s2, dh3, "ffn2_dw2", 0.5)
    (m_b, dgate, dyc, z3, dz1, yg, dsv, dsg, dyss, d_b_gate, d_ln_g, d_ln_b) = _merge_bwd(
        dh2, z1, yss, gate, conv_ln_g, conv_ln_b, wp_f, wv_f, wg_f, wo_f, "merge_bwd")
    gbig["w_out"] = _wgrad(m_b, dh2, "dw_out").reshape(NSH, D // NSH, D)

    def shard_cols(gm):
        return gm.reshape(gm.shape[0], NSH, -1).transpose(1, 0, 2)

    gbig["conv_proj"] = shard_cols(_wgrad(z3, dyc, "dw_proj"))
    gbig["ssm_w_v"] = shard_cols(_wgrad(yg, dsv, "dw_v"))
    gbig["ssm_w_g"] = shard_cols(_wgrad(yg, dsg, "dw_g"))
    dv, dgl, ddw, d_dw_b = _conv_bwd(dz1, vg, dw_pad, "conv_bwd")
    duf, dbre, dbim, dcre, dcim, dlam, d_ssm_d = _ssm_bwd(
        uf, dyss, bre_bd, bim_bd, cre_bd, cim_bd, lamp, ssm_d, "ssm_bwd")
    dh1, u_b, d_mix_norm = _mix_in_bwd(h1, mix_norm, dh2, dv, dgl, duf, dgate, w_in_f, "mix_in_bwd")
    gbig["w_in"] = shard_cols(jnp.concatenate(
        [_wgrad(u_b, dv, "dw_in_v"), _wgrad(u_b, dgl, "dw_in_g"), _wgrad(u_b, duf, "dw_in_u"),
         _wgrad(u_b, dgate, "dw_in_gate")], axis=1))
    dh0, da1, db1, s1, n1, d_ffn1_norm = _ffn_bwd(
        h0, ffn1_norm, dh1, a1, b1, gw["ffn1_w1"], gw["ffn1_w3"], gw["ffn1_w2"], "ffn1_bwd")
    gbig["ffn1_w1"] = _wgrad(da1, n1, "ffn1_dw1")
    gbig["ffn1_w3"] = _wgrad(db1, n1, "ffn1_dw3")
    gbig["ffn1_w2"] = _wgrad(s1, dh1, "ffn1_dw2", 0.5)
    grad_x = dh0[FRONT:][None]

    d_bbr = _diag_blocks(dbre.reshape(NQ, 8, H, 8, P)).transpose(0, 2, 1)
    d_bbi = _diag_blocks(dbim.reshape(NQ, 8, H, 8, P)).transpose(0, 2, 1)
    d_c_re = _diag_blocks(dcre.reshape(NQ, 8, P, 8, H)).transpose(0, 2, 1)
    d_c_im = _diag_blocks(dcim.reshape(NQ, 8, P, 8, H)).transpose(0, 2, 1)
    d_lbr = dlam[:, 0, :].reshape(G, P)
    d_lbi = dlam[:, 1, :].reshape(G, P)
    d_lam_re, d_lam_im, d_log_dt, d_b_re, d_b_im = disc_vjp((d_lbr, d_lbi, d_bbr, d_bbi))

    glist = [gbig[n] for n in big]
    core = lax.axis_index("c")
    sib = _pair_exchange(glist, "pair_exchange")
    pair = []
    for n, g_, s_ in zip(big, glist, sib):
        half = g_.shape[1] // 2
        mine = lax.dynamic_slice_in_dim(g_, core * half, half, axis=1)
        pair.append(_add_pair(mine.reshape(NSH * half, -1), s_.reshape(NSH * half, -1),
                              "pair_" + n).reshape(s_.shape))
    recv = _scatter_chips(pair, "scatter_grads")
    halves = [_sum_slots(r, "sum_" + n) for n, r in zip(big, recv)]
    full = _swap_halves(halves, "swap_halves")
    out_g, out_d, out_m, out_v = {}, {}, {}, {}
    for n, f in zip(big, full):
        shp = args[n].shape
        g2 = f.reshape(f.shape[0] * f.shape[1], f.shape[2])
        if n in tposed:
            g2 = g2.T
        d2, m2, v2 = _adamw(args[n].reshape(g2.shape), g2, args["m_" + n].reshape(g2.shape),
                            args["v_" + n].reshape(g2.shape), "adamw_" + n)
        out_g[n], out_d[n], out_m[n], out_v[n] = (t.reshape(shp) for t in (g2, d2, m2, v2))

    sg = {"meta_tokens": dh0[FRONT - NMETA:FRONT], "ffn1_norm": d_ffn1_norm, "mix_norm": d_mix_norm,
          "b_gate": d_b_gate, "conv_dw": ddw[:KW], "conv_dw_b": d_dw_b, "conv_ln_g": d_ln_g, "conv_ln_b": d_ln_b,
          "ssm_lam_re": d_lam_re, "ssm_lam_im": d_lam_im, "ssm_log_dt": d_log_dt, "ssm_b_re": d_b_re,
          "ssm_b_im": d_b_im, "ssm_c_re": d_c_re, "ssm_c_im": d_c_im, "ssm_d": d_ssm_d,
          "ffn2_norm": d_ffn2_norm, "final_norm": d_final}
    sshapes = [sg[n].shape for n in small]
    tot = _sum_slots(_gather_all(_pack([sg[n] for n in small]), "gather_small_grads"), "sum_small")
    sgr = dict(zip(small, _unpack(tot, sshapes)))
    sgr["meta_tokens"] = lax.dynamic_slice_in_dim(sgr["meta_tokens"], chip * (D // NSH), D // NSH, axis=1)
    sgr["conv_dw"] = lax.dynamic_slice_in_dim(sgr["conv_dw"], chip * (DC // NSH), DC // NSH, axis=1)
    pshapes = [args[n].shape for n in small]
    d_s, m_s, v_s = _adamw(_pack([args[n] for n in small]), _pack([sgr[n] for n in small]),
                           _pack([args["m_" + n] for n in small]), _pack([args["v_" + n] for n in small]),
                           "adamw_small")
    for n, g_, d_, m_, v_ in zip(small, [sgr[n] for n in small], _unpack(d_s, pshapes),
                                 _unpack(m_s, pshapes), _unpack(v_s, pshapes)):
        out_g[n], out_d[n], out_m[n], out_v[n] = g_.reshape(args[n].shape), d_, m_, v_

    loss = lax.psum(loss_part[0, 0], ("x", "y", "c"))
    return (loss, grad_x, *[out_g[n] for n in names], *[out_d[n] for n in names],
            *[out_m[n] for n in names], *[out_v[n] for n in names])
```

```python
import math

import jax
import jax.numpy as jnp
from jax import lax
from jax.experimental import pallas as pl
from jax.experimental.pallas import tpu as pltpu

F32 = jnp.float32
BF16 = jnp.bfloat16

D = 1024
NSH = 4
F = 2816
FS = F // NSH
DC = 512
DS = 512
KW = 31
KWP = 32
NMETA = 16
FRONT = 128
G, P, H = 32, 64, 16
NST = G * P
NQ = 4
QS = NST // NQ
QU = DS // NQ
NSEG = 8
EPS = 1e-6
LR, B1, B2, AEPS, WD, STEP = 1e-3, 0.9, 0.999, 1e-8, 0.01, 10
VMEM_LIMIT = 58 * 1024 * 1024
MESH = pl.DeviceIdType.MESH
ANY = pl.BlockSpec(memory_space=pl.ANY)


def _params(*sem):
    return pltpu.CompilerParams(dimension_semantics=sem, vmem_limit_bytes=VMEM_LIMIT)


def _res(shape):
    nd = len(shape)
    return pl.BlockSpec(shape, lambda *_: (0,) * nd, pipeline_mode=pl.Buffered(1))


def _tile(n, cap, mult=16):
    best = None
    for t in range(mult, min(n, cap) + 1, mult):
        if n % t == 0:
            best = t
    assert best is not None, (n, cap, mult)
    return best


def _dot(a, b):
    return jnp.dot(a, b, preferred_element_type=F32)


def _dot_nt(a, b):
    return lax.dot_general(a, b, (((1,), (1,)), ((), ())), preferred_element_type=F32)


def _dot_tn(a, b):
    return lax.dot_general(a, b, (((0,), (0,)), ((), ())), preferred_element_type=F32)


def _sigmoid(x):
    return 1.0 / (1.0 + jnp.exp(-x))


_GC = math.sqrt(2.0 / math.pi)
_GA = 0.044715


def _gelu(x):
    return 0.5 * x * (1.0 + jnp.tanh(_GC * (x + _GA * x * x * x)))


def _gelu_grad(x):
    t = jnp.tanh(_GC * (x + _GA * x * x * x))
    return 0.5 * (1.0 + t) + 0.5 * x * (1.0 - t * t) * _GC * (1.0 + 3.0 * _GA * x * x)


def _rms(hv, g):
    r = lax.rsqrt(jnp.mean(hv * hv, axis=-1, keepdims=True) + EPS)
    return hv * r * g, r


def _rms_bwd(dn, hv, r, g):
    xh = hv * r
    dxh = dn * g
    return r * (dxh - xh * jnp.mean(dxh * xh, axis=-1, keepdims=True)), xh


def _acc_rows(ref, part, first):
    @pl.when(first)
    def _():
        ref[...] = part

    @pl.when(jnp.logical_not(first))
    def _():
        ref[...] += part


def _coords():
    return lax.axis_index("x"), lax.axis_index("y"), lax.axis_index("c")


def _flip(v, d):
    return 1 - v if d else v


def _run(local, remote):
    for cp in local + remote:
        cp.start()
    for cp in remote:
        cp.wait()
    for cp in local:
        cp.wait()


def _via_vmem(src, dst, stage, sems, i):
    return (pltpu.make_async_copy(src, stage, sems.at[2 * i]), pltpu.make_async_copy(stage, dst, sems.at[2 * i + 1]))


def _run_staged(staged, remote):
    for load, _ in staged:
        load.start()
    for cp in remote:
        cp.start()
    for load, store in staged:
        load.wait()
        store.start()
    for cp in remote:
        cp.wait()
    for _, store in staged:
        store.wait()


_REL3 = ((1, 0), (0, 1), (1, 1))


def _gather_chips(shards, name):
    n = len(shards)

    def body(*refs):
        ins, outs = refs[:n], refs[n:2 * n]
        send, recv, fsend, frecv, loc = refs[2 * n:2 * n + 5]
        stage = refs[2 * n + 5:]
        x, y, c = _coords()
        me = 2 * x + y
        own = [_via_vmem(ins[t], outs[t].at[me], stage[t], loc, t) for t in range(n)]
        first, passed = [], []
        for t in range(n):
            half = shards[t].shape[0] // 2
            mine, theirs = pl.ds(c * half, half), pl.ds((1 - c) * half, half)
            for k, (dx, dy) in enumerate(_REL3):
                px, py = _flip(x, dx), _flip(y, dy)
                first.append(pltpu.make_async_remote_copy(
                    src_ref=ins[t].at[mine], dst_ref=outs[t].at[me, mine],
                    send_sem=send.at[3 * t + k], recv_sem=recv.at[3 * t + k],
                    device_id=(px, py, c), device_id_type=MESH))
                passed.append((
                    pltpu.make_async_remote_copy(
                        src_ref=outs[t].at[2 * px + py, mine], dst_ref=outs[t].at[2 * px + py, mine],
                        send_sem=fsend.at[3 * t + k], recv_sem=frecv.at[3 * t + k],
                        device_id=(x, y, 1 - c), device_id_type=MESH),
                    pltpu.make_async_remote_copy(
                        src_ref=outs[t].at[2 * px + py, theirs], dst_ref=outs[t].at[2 * px + py, theirs],
                        send_sem=fsend.at[3 * t + k], recv_sem=frecv.at[3 * t + k],
                        device_id=(x, y, 1 - c), device_id_type=MESH)))
        for load, _ in own:
            load.start()
        for cp in first:
            cp.start()
        for load, store in own:
            load.wait()
            store.start()
        for cp, (fwd, _) in zip(first, passed):
            cp.wait_recv()
            fwd.start()
        for cp, (fwd, back) in zip(first, passed):
            cp.wait_send()
            fwd.wait_send()
            back.wait_recv()
        for _, store in own:
            store.wait()

    return pl.pallas_call(
        body, name=name,
        out_shape=[jax.ShapeDtypeStruct((NSH,) + s.shape, s.dtype) for s in shards],
        in_specs=[ANY] * n, out_specs=[ANY] * n,
        scratch_shapes=[pltpu.SemaphoreType.DMA((3 * n,)) for _ in range(4)] + [pltpu.SemaphoreType.DMA((2 * n,))]
        + [pltpu.VMEM(s.shape, s.dtype) for s in shards],
        compiler_params=pltpu.CompilerParams(vmem_limit_bytes=VMEM_LIMIT),
    )(*shards)


_REL7 = tuple((dx, dy, dc) for dx in (0, 1) for dy in (0, 1) for dc in (0, 1))[1:]


def _gather_all(a, name):
    def body(a_ref, o_ref, send, recv, loc):
        x, y, c = _coords()
        me = 4 * x + 2 * y + c
        local = [pltpu.make_async_copy(a_ref, o_ref.at[me], loc.at[0])]
        remote = [pltpu.make_async_remote_copy(
            src_ref=a_ref, dst_ref=o_ref.at[me], send_sem=send.at[k], recv_sem=recv.at[k],
            device_id=(_flip(x, dx), _flip(y, dy), _flip(c, dc)), device_id_type=MESH)
            for k, (dx, dy, dc) in enumerate(_REL7)]
        _run(local, remote)

    return pl.pallas_call(
        body, name=name,
        out_shape=jax.ShapeDtypeStruct((8,) + a.shape, a.dtype),
        in_specs=[ANY], out_specs=ANY,
        scratch_shapes=[pltpu.SemaphoreType.DMA((7,)), pltpu.SemaphoreType.DMA((7,)),
                        pltpu.SemaphoreType.DMA((1,))],
    )(a)


def _pair_exchange(grads, name):
    n = len(grads)

    def body(*refs):
        ins, outs = refs[:n], refs[n:2 * n]
        send, recv = refs[2 * n:]
        x, y, c = _coords()
        remote = []
        for t in range(n):
            half = grads[t].shape[1] // 2
            remote.append(pltpu.make_async_remote_copy(
                src_ref=ins[t].at[:, pl.ds((1 - c) * half, half)], dst_ref=outs[t],
                send_sem=send.at[t], recv_sem=recv.at[t],
                device_id=(x, y, 1 - c), device_id_type=MESH))
        _run([], remote)

    return pl.pallas_call(
        body, name=name,
        out_shape=[jax.ShapeDtypeStruct((NSH, g.shape[1] // 2, g.shape[2]), g.dtype) for g in grads],
        in_specs=[ANY] * n, out_specs=[ANY] * n,
        scratch_shapes=[pltpu.SemaphoreType.DMA((n,)), pltpu.SemaphoreType.DMA((n,))],
    )(*grads)


def _scatter_chips(sums, name):
    n = len(sums)

    def body(*refs):
        ins, outs = refs[:n], refs[n:2 * n]
        send, recv, loc = refs[2 * n:2 * n + 3]
        stage = refs[2 * n + 3:]
        x, y, c = _coords()
        me = 2 * x + y
        local = [_via_vmem(ins[t].at[me], outs[t].at[me], stage[t], loc, t) for t in range(n)]
        remote = []
        for t in range(n):
            for k, (dx, dy) in enumerate(_REL3):
                px, py = _flip(x, dx), _flip(y, dy)
                remote.append(pltpu.make_async_remote_copy(
                    src_ref=ins[t].at[2 * px + py], dst_ref=outs[t].at[me],
                    send_sem=send.at[3 * t + k], recv_sem=recv.at[3 * t + k],
                    device_id=(px, py, c), device_id_type=MESH))
        _run_staged(local, remote)

    return pl.pallas_call(
        body, name=name,
        out_shape=[jax.ShapeDtypeStruct(s.shape, s.dtype) for s in sums],
        in_specs=[ANY] * n, out_specs=[ANY] * n,
        scratch_shapes=[pltpu.SemaphoreType.DMA((3 * n,)), pltpu.SemaphoreType.DMA((3 * n,)),
                        pltpu.SemaphoreType.DMA((2 * n,))]
        + [pltpu.VMEM(s.shape[1:], s.dtype) for s in sums],
        compiler_params=pltpu.CompilerParams(vmem_limit_bytes=VMEM_LIMIT),
    )(*sums)


def _swap_halves(halves, name):
    n = len(halves)

    def body(*refs):
        ins, outs = refs[:n], refs[n:2 * n]
        send, recv, loc = refs[2 * n:2 * n + 3]
        stage = refs[2 * n + 3:]
        x, y, c = _coords()
        local = [_via_vmem(ins[t], outs[t].at[c], stage[t], loc, t) for t in range(n)]
        remote = [pltpu.make_async_remote_copy(
            src_ref=ins[t], dst_ref=outs[t].at[c], send_sem=send.at[t], recv_sem=recv.at[t],
            device_id=(x, y, 1 - c), device_id_type=MESH) for t in range(n)]
        _run_staged(local, remote)

    return pl.pallas_call(
        body, name=name,
        out_shape=[jax.ShapeDtypeStruct((2,) + h.shape, h.dtype) for h in halves],
        in_specs=[ANY] * n, out_specs=[ANY] * n,
        scratch_shapes=[pltpu.SemaphoreType.DMA((n,)), pltpu.SemaphoreType.DMA((n,)),
                        pltpu.SemaphoreType.DMA((2 * n,))]
        + [pltpu.VMEM(h.shape, h.dtype) for h in halves],
        compiler_params=pltpu.CompilerParams(vmem_limit_bytes=VMEM_LIMIT),
    )(*halves)


def _sum_slots(r, name):
    K, R, C = r.shape
    tr = _tile(R, max(16, (1 << 22) // (K * C)))

    def body(r_ref, o_ref):
        acc = r_ref[0].astype(F32)
        for k in range(1, K):
            acc = acc + r_ref[k].astype(F32)
        o_ref[...] = acc

    return pl.pallas_call(
        body, name=name, grid=(R // tr,),
        out_shape=jax.ShapeDtypeStruct((R, C), F32),
        in_specs=[pl.BlockSpec((K, tr, C), lambda i: (0, i, 0))],
        out_specs=pl.BlockSpec((tr, C), lambda i: (i, 0)),
        compiler_params=_params("parallel"),
    )(r)


def _add_pair(a, b, name):
    R, C = a.shape
    tr = _tile(R, max(16, (1 << 20) // C))

    def body(a_ref, b_ref, o_ref):
        o_ref[...] = (a_ref[...].astype(F32) + b_ref[...].astype(F32)).astype(BF16)

    spec = pl.BlockSpec((tr, C), lambda i: (i, 0))
    return pl.pallas_call(
        body, name=name, grid=(R // tr,),
        out_shape=jax.ShapeDtypeStruct((R, C), BF16),
        in_specs=[spec, spec], out_specs=spec,
        compiler_params=_params("parallel"),
    )(a, b)


def _adamw(w, g, m, v, name):
    R, C = w.shape
    tr = _tile(R, max(8, (1 << 18) // C), 8)
    c1 = 1.0 / (1.0 - B1 ** STEP)
    c2 = 1.0 / (1.0 - B2 ** STEP)

    def body(w_ref, g_ref, m_ref, v_ref, d_ref, nm_ref, nv_ref):
        gv = g_ref[...]
        nm = B1 * m_ref[...] + (1.0 - B1) * gv
        nv = B2 * v_ref[...] + (1.0 - B2) * gv * gv
        nm_ref[...] = nm
        nv_ref[...] = nv
        d_ref[...] = -LR * ((nm * c1) / (jnp.sqrt(nv * c2) + AEPS) + WD * w_ref[...])

    spec = pl.BlockSpec((tr, C), lambda i: (i, 0))
    return pl.pallas_call(
        body, name=name, grid=(R // tr,),
        out_shape=[jax.ShapeDtypeStruct((R, C), F32)] * 3,
        in_specs=[spec] * 4, out_specs=[spec] * 3,
        compiler_params=_params("parallel"),
    )(w, g, m, v)


def _ffn_fwd(h, g, w1, w3, w2, name):
    L = h.shape[0]
    tm = _tile(L, 528)

    def body(h_ref, g_ref, w1_ref, w3_ref, w2_ref, o_ref, a_ref, b_ref, n_s, acc_s):
        j = pl.program_id(1)

        @pl.when(j == 0)
        def _():
            hv = h_ref[...]
            n, _ = _rms(hv, g_ref[...])
            n_s[...] = n.astype(BF16)
            acc_s[...] = hv

        n = n_s[...]
        a = _dot_nt(n, w1_ref[j])
        b = _dot_nt(n, w3_ref[j])
        a_ref[0] = a.astype(BF16)
        b_ref[0] = b.astype(BF16)
        s = (a * _sigmoid(a) * b).astype(BF16)
        acc_s[...] += 0.5 * _dot(s, w2_ref[j])

        @pl.when(j == NSH - 1)
        def _():
            o_ref[...] = acc_s[...]

    row = pl.BlockSpec((tm, D), lambda i, j: (i, 0))
    hid = pl.BlockSpec((1, tm, FS), lambda i, j: (j, i, 0))
    return pl.pallas_call(
        body, name=name, grid=(L // tm, NSH),
        out_shape=[jax.ShapeDtypeStruct((L, D), F32),
                   jax.ShapeDtypeStruct((NSH, L, FS), BF16), jax.ShapeDtypeStruct((NSH, L, FS), BF16)],
        in_specs=[row, _res((1, D)), _res((NSH, FS, D)), _res((NSH, FS, D)), _res((NSH, FS, D))],
        out_specs=[row, hid, hid],
        scratch_shapes=[pltpu.VMEM((tm, D), BF16), pltpu.VMEM((tm, D), F32)],
        compiler_params=_params("arbitrary", "arbitrary"),
    )(h, g, w1, w3, w2)


def _ffn_bwd(h, g, dout, a, b, w1, w3, w2, name):
    L = h.shape[0]
    tm = _tile(L, 352)

    def body(h_ref, g_ref, do_ref, a_ref, b_ref, w1_ref, w3_ref, w2_ref,
             dh_ref, da_ref, db_ref, s_ref, n_ref, dg_ref, dob_s, dn_s):
        i, j = pl.program_id(0), pl.program_id(1)

        @pl.when(j == 0)
        def _():
            n, _ = _rms(h_ref[...], g_ref[...])
            n_ref[...] = n.astype(BF16)
            dob_s[...] = do_ref[...].astype(BF16)
            dn_s[...] = jnp.zeros_like(dn_s)

        av = a_ref[0].astype(F32)
        bv = b_ref[0].astype(F32)
        sig = _sigmoid(av)
        sa = av * sig
        ds = 0.5 * _dot_nt(dob_s[...], w2_ref[j])
        s_ref[0] = (sa * bv).astype(BF16)
        da = (ds * bv * sig * (1.0 + av * (1.0 - sig))).astype(BF16)
        db = (ds * sa).astype(BF16)
        da_ref[0] = da
        db_ref[0] = db
        dn_s[...] += _dot(da, w1_ref[j]) + _dot(db, w3_ref[j])

        @pl.when(j == NSH - 1)
        def _():
            hv = h_ref[...]
            gv = g_ref[...]
            r = lax.rsqrt(jnp.mean(hv * hv, axis=-1, keepdims=True) + EPS)
            dn = dn_s[...]
            dx, xh = _rms_bwd(dn, hv, r, gv)
            dh_ref[...] = do_ref[...] + dx
            _acc_rows(dg_ref, jnp.sum(dn * xh, axis=0, keepdims=True), i == 0)

    row = pl.BlockSpec((tm, D), lambda i, j: (i, 0))
    hid = pl.BlockSpec((1, tm, FS), lambda i, j: (j, i, 0))
    return pl.pallas_call(
        body, name=name, grid=(L // tm, NSH),
        out_shape=[jax.ShapeDtypeStruct((L, D), F32)]
        + [jax.ShapeDtypeStruct((NSH, L, FS), BF16)] * 3
        + [jax.ShapeDtypeStruct((L, D), BF16), jax.ShapeDtypeStruct((1, D), F32)],
        in_specs=[row, _res((1, D)), row, hid, hid,
                  _res((NSH, FS, D)), _res((NSH, FS, D)), _res((NSH, FS, D))],
        out_specs=[row, hid, hid, hid, row, pl.BlockSpec((1, D), lambda i, j: (0, 0))],
        scratch_shapes=[pltpu.VMEM((tm, D), BF16), pltpu.VMEM((tm, D), F32)],
        compiler_params=_params("arbitrary", "arbitrary"),
    )(h, g, dout, a, b, w1, w3, w2)


def _wgrad(xm, ym, name, scale=1.0):
    xs, ys = xm.ndim == 3, ym.ndim == 3
    assert not (xs and ys)
    L = xm.shape[-2]
    K, N = xm.shape[-1], ym.shape[-1]
    tl = _tile(L, 1056)
    nl = L // tl
    if xs or ys:
        tn, grid_n = N, NSH
    else:
        tn = _tile(N, 1024, 128)
        grid_n = N // tn

    def body(x_ref, y_ref, o_ref, acc_s):
        l = pl.program_id(1)
        xv = x_ref[0] if xs else x_ref[...]
        yv = y_ref[0] if ys else y_ref[...]
        part = _dot_tn(xv.astype(BF16), yv.astype(BF16))
        _acc_rows(acc_s, part, l == 0)

        @pl.when(l == nl - 1)
        def _():
            res = (acc_s[...] * scale).astype(BF16)
            if xs or ys:
                o_ref[0] = res
            else:
                o_ref[...] = res

    if xs:
        x_spec = pl.BlockSpec((1, tl, K), lambda n, l: (n, l, 0))
        y_spec = pl.BlockSpec((tl, N), lambda n, l: (l, 0))
        o_spec = pl.BlockSpec((1, K, N), lambda n, l: (n, 0, 0))
        o_shape = (NSH, K, N)
    elif ys:
        x_spec = pl.BlockSpec((tl, K), lambda n, l: (l, 0))
        y_spec = pl.BlockSpec((1, tl, N), lambda n, l: (n, l, 0))
        o_spec = pl.BlockSpec((1, K, N), lambda n, l: (n, 0, 0))
        o_shape = (NSH, K, N)
    else:
        x_spec = pl.BlockSpec((tl, K), lambda n, l: (l, 0))
        y_spec = pl.BlockSpec((tl, tn), lambda n, l: (l, n))
        o_spec = pl.BlockSpec((K, tn), lambda n, l: (0, n))
        o_shape = (K, N)
    return pl.pallas_call(
        body, name=name, grid=(grid_n, nl),
        out_shape=jax.ShapeDtypeStruct(o_shape, BF16),
        in_specs=[x_spec, y_spec], out_specs=o_spec,
        scratch_shapes=[pltpu.VMEM((K, tn), F32)],
        compiler_params=_params("parallel", "arbitrary"),
    )(xm, ym)


def _mix_in_fwd(h, g, w_in, b_gate, name):
    L = h.shape[0]
    tm = _tile(L, 528)

    def body(h_ref, g_ref, w_ref, bg_ref, vg_ref, uf_ref, gt_ref):
        u, _ = _rms(h_ref[...], g_ref[...])
        ub = u.astype(BF16)
        vg_ref[...] = _dot(ub, w_ref[:, 0:2 * DC]).astype(BF16)
        uf_ref[...] = _dot(ub, w_ref[:, 2 * DC:2 * DC + DS]).astype(BF16)
        gt_ref[...] = _sigmoid(_dot(ub, w_ref[:, 2 * DC + DS:]) + bg_ref[...]).astype(BF16)

    def row(n):
        return pl.BlockSpec((tm, n), lambda i: (i, 0))

    din = w_in.shape[1]
    return pl.pallas_call(
        body, name=name, grid=(L // tm,),
        out_shape=[jax.ShapeDtypeStruct((L, 2 * DC), BF16), jax.ShapeDtypeStruct((L, DS), BF16),
                   jax.ShapeDtypeStruct((L, 2 * D), BF16)],
        in_specs=[row(D), _res((1, D)), _res((D, din)), _res((1, 2 * D))],
        out_specs=[row(2 * DC), row(DS), row(2 * D)],
        compiler_params=_params("parallel"),
    )(h, g, w_in, b_gate)


def _mix_in_bwd(h, g, dres, dv, dgl, duf, dgate, w_in, name):
    L = h.shape[0]
    tm = _tile(L, 528)

    def body(h_ref, g_ref, dr_ref, dv_ref, dgl_ref, duf_ref, dgt_ref, w_ref, dh_ref, u_ref, dgm_ref):
        i = pl.program_id(0)
        hv = h_ref[...]
        gv = g_ref[...]
        u, r = _rms(hv, gv)
        u_ref[...] = u.astype(BF16)
        du = (_dot_nt(dv_ref[...], w_ref[:, 0:DC]) + _dot_nt(dgl_ref[...], w_ref[:, DC:2 * DC])
              + _dot_nt(duf_ref[...], w_ref[:, 2 * DC:2 * DC + DS])
              + _dot_nt(dgt_ref[...], w_ref[:, 2 * DC + DS:]))
        dx, xh = _rms_bwd(du, hv, r, gv)
        dh_ref[...] = dr_ref[...] + dx
        _acc_rows(dgm_ref, jnp.sum(du * xh, axis=0, keepdims=True), i == 0)

    def row(n):
        return pl.BlockSpec((tm, n), lambda i: (i, 0))

    din = w_in.shape[1]
    return pl.pallas_call(
        body, name=name, grid=(L // tm,),
        out_shape=[jax.ShapeDtypeStruct((L, D), F32), jax.ShapeDtypeStruct((L, D), BF16),
                   jax.ShapeDtypeStruct((1, D), F32)],
        in_specs=[row(D), _res((1, D)), row(D), row(DC), row(DC), row(DS), row(2 * D), _res((D, din))],
        out_specs=[row(D), row(D), pl.BlockSpec((1, D), lambda i: (0, 0))],
        compiler_params=_params("arbitrary"),
    )(h, g, dres, dv, dgl, duf, dgate, w_in)


def _conv_fwd(vg, dw, dwb, name):
    L = vg.shape[0]
    nc = DC // 128

    def body(v_ref, g_ref, dw_ref, dwb_ref, z_ref, zp_s):
        zp_s[0:KWP, :] = jnp.zeros((KWP, 128), F32)
        zp_s[KWP:, :] = v_ref[...].astype(F32) * _sigmoid(g_ref[...].astype(F32))
        acc = jnp.broadcast_to(dwb_ref[...], (L, 128))
        for k in range(KW):
            acc = acc + dw_ref[k:k + 1, :] * zp_s[pl.ds(k + 2, L), :]
        z_ref[...] = acc

    return pl.pallas_call(
        body, name=name, grid=(nc,),
        out_shape=jax.ShapeDtypeStruct((L, DC), F32),
        in_specs=[pl.BlockSpec((L, 128), lambda c: (0, c)), pl.BlockSpec((L, 128), lambda c: (0, nc + c)),
                  pl.BlockSpec((KWP, 128), lambda c: (0, c)), pl.BlockSpec((1, 128), lambda c: (0, c))],
        out_specs=pl.BlockSpec((L, 128), lambda c: (0, c)),
        scratch_shapes=[pltpu.VMEM((L + KWP, 128), F32)],
        compiler_params=_params("parallel"),
    )(vg, vg, dw, dwb)


def _conv_bwd(dz1, vg, dw, name):
    L = vg.shape[0]
    nc = DC // 128

    def body(dz_ref, v_ref, g_ref, dw_ref, dv_ref, dg_ref, ddw_ref, ddwb_ref, zp_s, dzp_s):
        vv = v_ref[...].astype(F32)
        sg = _sigmoid(g_ref[...].astype(F32))
        zp_s[0:KWP, :] = jnp.zeros((KWP, 128), F32)
        zp_s[KWP:, :] = vv * sg
        dz = dz_ref[...]
        dzp_s[0:L, :] = dz
        dzp_s[L:, :] = jnp.zeros((KWP, 128), F32)
        ddwb_ref[...] = jnp.sum(dz, axis=0, keepdims=True)
        acc = jnp.zeros((L, 128), F32)
        for k in range(KW):
            acc = acc + dw_ref[k:k + 1, :] * dzp_s[pl.ds(KW - 1 - k, L), :]
            ddw_ref[k:k + 1, :] = jnp.sum(dz * zp_s[pl.ds(k + 2, L), :], axis=0, keepdims=True)
        ddw_ref[KW:KWP, :] = jnp.zeros((KWP - KW, 128), F32)
        dv_ref[...] = (acc * sg).astype(BF16)
        dg_ref[...] = (acc * vv * sg * (1.0 - sg)).astype(BF16)

    col = pl.BlockSpec((L, 128), lambda c: (0, c))
    return pl.pallas_call(
        body, name=name, grid=(nc,),
        out_shape=[jax.ShapeDtypeStruct((L, DC), BF16), jax.ShapeDtypeStruct((L, DC), BF16),
                   jax.ShapeDtypeStruct((KWP, DC), F32), jax.ShapeDtypeStruct((1, DC), F32)],
        in_specs=[col, col, pl.BlockSpec((L, 128), lambda c: (0, nc + c)),
                  pl.BlockSpec((KWP, 128), lambda c: (0, c))],
        out_specs=[col, col, pl.BlockSpec((KWP, 128), lambda c: (0, c)), pl.BlockSpec((1, 128), lambda c: (0, c))],
        scratch_shapes=[pltpu.VMEM((L + KWP, 128), F32), pltpu.VMEM((L + KWP, 128), F32)],
        compiler_params=_params("parallel"),
    )(dz1, vg, vg, dw)


NLB = QS // 128


def _lb_store(ref, rows, val):
    for cb in range(NLB):
        ref[cb, rows, :] = val[:, cb * 128:(cb + 1) * 128]


def _lb_load(ref, rows):
    return jnp.concatenate([ref[cb, rows, :] for cb in range(NLB)], axis=1)


def _scan(xr_ref, xi_ref, base, T, ar, ai, atr, ati, reverse, sr_ref=None, si_ref=None):
    W = ar.shape[1]
    ar, ai = jnp.broadcast_to(ar, (NSEG, W)), jnp.broadcast_to(ai, (NSEG, W))
    atr, ati = jnp.broadcast_to(atr, (NSEG, W)), jnp.broadcast_to(ati, (NSEG, W))
    zero = jnp.zeros((NSEG, W), F32)

    def rows(t):
        tt = T - 1 - t if reverse else t
        return pl.ds(base + tt, NSEG, stride=T)

    def step(t, carry):
        sr, si = carry
        idx = rows(t)
        return (ar * sr - ai * si + _lb_load(xr_ref, idx), ar * si + ai * sr + _lb_load(xi_ref, idx))

    er, ei = lax.fori_loop(0, T, step, (zero, zero))
    seg = lax.broadcasted_iota(jnp.int32, (NSEG, W), 0)
    edge = seg == (NSEG - 1 if reverse else 0)
    cr, ci = zero, zero
    for _ in range(NSEG - 1):
        nr = atr * cr - ati * ci + er
        ni = atr * ci + ati * cr + ei
        shift = NSEG - 1 if reverse else 1
        cr = jnp.where(edge, 0.0, pltpu.roll(nr, shift, 0))
        ci = jnp.where(edge, 0.0, pltpu.roll(ni, shift, 0))

    if sr_ref is None:
        def step2(t, carry):
            sr, si = step(t, carry)
            idx = rows(t)
            _lb_store(xr_ref, idx, sr)
            _lb_store(xi_ref, idx, si)
            return sr, si

        lax.fori_loop(0, T, step2, (cr, ci))
        return None

    def step3(t, carry):
        sr, si, qr, qi = carry
        sr, si = step(t, (sr, si))
        idx = rows(t)
        _lb_store(xr_ref, idx, sr)
        _lb_store(xi_ref, idx, si)
        prev = pl.ds(NSEG - 1 + T - 1 - t, NSEG, stride=T)
        pr, pi = _lb_load(sr_ref, prev), _lb_load(si_ref, prev)
        return sr, si, qr + sr * pr + si * pi, qi + si * pr - sr * pi

    _, _, qr, qi = lax.fori_loop(0, T, step3, (cr, ci, zero, zero))
    return jnp.sum(qr, axis=0, keepdims=True), jnp.sum(qi, axis=0, keepdims=True)


def _ssm_fwd(uf, bre, bim, cre, cim, lamp, dsk, name):
    L = uf.shape[0]
    T = L // NSEG

    def body(u_ref, bre_ref, bim_ref, cre_ref, cim_ref, lam_ref, d_ref, y_ref, sr_s, si_s):
        for k in range(NSEG):
            sl = slice(k * T, (k + 1) * T)
            uk = u_ref[sl, :]
            _lb_store(sr_s, sl, _dot(uk, bre_ref[...]))
            _lb_store(si_s, sl, _dot(uk, bim_ref[...]))
        _scan(sr_s, si_s, 0, T, lam_ref[0:1, :], lam_ref[1:2, :], lam_ref[2:3, :], lam_ref[3:4, :], False)
        for k in range(NSEG):
            sl = slice(k * T, (k + 1) * T)
            y_ref[sl, :] = (_dot(_lb_load(sr_s, sl).astype(BF16), cre_ref[...])
                            - _dot(_lb_load(si_s, sl).astype(BF16), cim_ref[...])
                            + d_ref[...] * u_ref[sl, :].astype(F32))

    return pl.pallas_call(
        body, name=name, grid=(NQ,),
        out_shape=jax.ShapeDtypeStruct((L, DS), F32),
        in_specs=[pl.BlockSpec((L, QU), lambda q: (0, q)),
                  pl.BlockSpec((QU, QS), lambda q: (q, q)), pl.BlockSpec((QU, QS), lambda q: (q, q)),
                  pl.BlockSpec((QS, QU), lambda q: (q, q)), pl.BlockSpec((QS, QU), lambda q: (q, q)),
                  pl.BlockSpec((8, QS), lambda q: (0, q)), pl.BlockSpec((1, QU), lambda q: (0, q))],
        out_specs=pl.BlockSpec((L, QU), lambda q: (0, q)),
        scratch_shapes=[pltpu.VMEM((NLB, L, 128), F32), pltpu.VMEM((NLB, L, 128), F32)],
        compiler_params=_params("parallel"),
    )(uf, bre, bim, cre, cim, lamp, dsk)


def _ssm_bwd(uf, dyss, bre, bim, cre, cim, lamp, dsk, name):
    L = uf.shape[0]
    T = L // NSEG

    def body(u_ref, dy_ref, bre_ref, bim_ref, cre_ref, cim_ref, lam_ref, d_ref,
             du_ref, dbre_ref, dbim_ref, dcre_ref, dcim_ref, dlam_ref, dd_ref, sr_s, si_s, gr_s, gi_s):
        _lb_store(sr_s, slice(0, NSEG), jnp.zeros((NSEG, QS), F32))
        _lb_store(si_s, slice(0, NSEG), jnp.zeros((NSEG, QS), F32))
        for k in range(NSEG):
            sl = slice(k * T, (k + 1) * T)
            ss = slice(NSEG + k * T, NSEG + (k + 1) * T)
            uk = u_ref[sl, :]
            dyk = dy_ref[sl, :].astype(BF16)
            _lb_store(sr_s, ss, _dot(uk, bre_ref[...]))
            _lb_store(si_s, ss, _dot(uk, bim_ref[...]))
            _lb_store(gr_s, sl, _dot_nt(dyk, cre_ref[...]))
            _lb_store(gi_s, sl, -_dot_nt(dyk, cim_ref[...]))
        ar, ai, atr, ati = lam_ref[0:1, :], lam_ref[1:2, :], lam_ref[2:3, :], lam_ref[3:4, :]
        _scan(sr_s, si_s, NSEG, T, ar, ai, atr, ati, False)
        qr, qi = _scan(gr_s, gi_s, 0, T, ar, -ai, atr, -ati, True, sr_s, si_s)
        dlam_ref[0] = jnp.concatenate([qr, qi, jnp.zeros((6, QS), F32)], axis=0)
        dbre = jnp.zeros((QU, QS), F32)
        dbim = jnp.zeros((QU, QS), F32)
        dcre = jnp.zeros((QS, QU), F32)
        dcim = jnp.zeros((QS, QU), F32)
        dd = jnp.zeros((1, QU), F32)
        for k in range(NSEG):
            sl = slice(k * T, (k + 1) * T)
            ss = slice(NSEG + k * T, NSEG + (k + 1) * T)
            uk = u_ref[sl, :]
            dyk = dy_ref[sl, :]
            dyb = dyk.astype(BF16)
            grb = _lb_load(gr_s, sl).astype(BF16)
            gib = _lb_load(gi_s, sl).astype(BF16)
            du_ref[sl, :] = (_dot_nt(grb, bre_ref[...]) + _dot_nt(gib, bim_ref[...])
                             + dyk * d_ref[...]).astype(BF16)
            dbre = dbre + _dot_tn(uk, grb)
            dbim = dbim + _dot_tn(uk, gib)
            dcre = dcre + _dot_tn(_lb_load(sr_s, ss).astype(BF16), dyb)
            dcim = dcim - _dot_tn(_lb_load(si_s, ss).astype(BF16), dyb)
            dd = dd + jnp.sum(dyk * uk.astype(F32), axis=0, keepdims=True)
        dbre_ref[0] = dbre
        dbim_ref[0] = dbim
        dcre_ref[0] = dcre
        dcim_ref[0] = dcim
        dd_ref[...] = dd

    col = pl.BlockSpec((L, QU), lambda q: (0, q))
    bsp = pl.BlockSpec((QU, QS), lambda q: (q, q))
    csp = pl.BlockSpec((QS, QU), lambda q: (q, q))
    return pl.pallas_call(
        body, name=name, grid=(NQ,),
        out_shape=[jax.ShapeDtypeStruct((L, DS), BF16),
                   jax.ShapeDtypeStruct((NQ, QU, QS), F32), jax.ShapeDtypeStruct((NQ, QU, QS), F32),
                   jax.ShapeDtypeStruct((NQ, QS, QU), F32), jax.ShapeDtypeStruct((NQ, QS, QU), F32),
                   jax.ShapeDtypeStruct((NQ, 8, QS), F32), jax.ShapeDtypeStruct((1, DS), F32)],
        in_specs=[col, col, bsp, bsp, csp, csp,
                  pl.BlockSpec((8, QS), lambda q: (0, q)), pl.BlockSpec((1, QU), lambda q: (0, q))],
        out_specs=[col,
                   pl.BlockSpec((1, QU, QS), lambda q: (q, 0, 0)), pl.BlockSpec((1, QU, QS), lambda q: (q, 0, 0)),
                   pl.BlockSpec((1, QS, QU), lambda q: (q, 0, 0)), pl.BlockSpec((1, QS, QU), lambda q: (q, 0, 0)),
                   pl.BlockSpec((1, 8, QS), lambda q: (q, 0, 0)), pl.BlockSpec((1, QU), lambda q: (0, q))],
        scratch_shapes=[pltpu.VMEM((NLB, L + NSEG, 128), F32), pltpu.VMEM((NLB, L + NSEG, 128), F32),
                        pltpu.VMEM((NLB, L, 128), F32), pltpu.VMEM((NLB, L, 128), F32)],
        compiler_params=_params("parallel"),
    )(uf, dyss, bre, bim, cre, cim, lamp, dsk)


def _branches(z1_ref, yss_ref, gt_ref, lng_ref, lnb_ref, wp_ref, wv_ref, wg_ref):
    zf = z1_ref[...]
    mu = jnp.mean(zf, axis=-1, keepdims=True)
    zc = zf - mu
    rstd = lax.rsqrt(jnp.mean(zc * zc, axis=-1, keepdims=True) + EPS)
    zn = zc * rstd
    z2 = zn * lng_ref[...] + lnb_ref[...]
    sz = _sigmoid(z2)
    z3 = (z2 * sz).astype(BF16)
    y_conv = _dot(z3, wp_ref[...])
    yss = yss_ref[...]
    yg = _gelu(yss).astype(BF16)
    sv = _dot(yg, wv_ref[...])
    sig = _sigmoid(_dot(yg, wg_ref[...]))
    y_ssm = sv * sig
    gc = gt_ref[:, 0:D].astype(F32)
    gs = gt_ref[:, D:2 * D].astype(F32)
    m = gc * y_conv + gs * y_ssm
    return dict(rstd=rstd, zn=zn, z2=z2, sz=sz, z3=z3, y_conv=y_conv, yss=yss, yg=yg, sv=sv, sig=sig,
                y_ssm=y_ssm, gc=gc, gs=gs, m=m)


def _merge_fwd(h, z1, yss, gate, lng, lnb, wp, wv, wg, wo, name):
    L = h.shape[0]
    tm = _tile(L, 528)

    def body(h_ref, z1_ref, yss_ref, gt_ref, lng_ref, lnb_ref, wp_ref, wv_ref, wg_ref, wo_ref, o_ref):
        f = _branches(z1_ref, yss_ref, gt_ref, lng_ref, lnb_ref, wp_ref, wv_ref, wg_ref)
        o_ref[...] = h_ref[...] + _dot(f["m"].astype(BF16), wo_ref[...])

    def row(n):
        return pl.BlockSpec((tm, n), lambda i: (i, 0))

    return pl.pallas_call(
        body, name=name, grid=(L // tm,),
        out_shape=jax.ShapeDtypeStruct((L, D), F32),
        in_specs=[row(D), row(DC), row(DS), row(2 * D), _res((1, DC)), _res((1, DC)),
                  _res((DC, D)), _res((DS, D)), _res((DS, D)), _res((D, D))],
        out_specs=row(D),
        compiler_params=_params("parallel"),
    )(h, z1, yss, gate, lng, lnb, wp, wv, wg, wo)


def _merge_bwd(dh, z1, yss, gate, lng, lnb, wp, wv, wg, wo, name):
    L = dh.shape[0]
    tm = _tile(L, 352)

    def body(dh_ref, z1_ref, yss_ref, gt_ref, lng_ref, lnb_ref, wp_ref, wv_ref, wg_ref, wo_ref,
             m_ref, dgt_ref, dyc_ref, z3_ref, dz1_ref, yg_ref, dsv_ref, dsg_ref, dyss_ref,
             dbg_ref, dlng_ref, dlnb_ref):
        i = pl.program_id(0)
        f = _branches(z1_ref, yss_ref, gt_ref, lng_ref, lnb_ref, wp_ref, wv_ref, wg_ref)
        gc, gs, sig, sv = f["gc"], f["gs"], f["sig"], f["sv"]
        m_ref[...] = f["m"].astype(BF16)
        z3_ref[...] = f["z3"]
        yg_ref[...] = f["yg"]
        dm = _dot_nt(dh_ref[...].astype(BF16), wo_ref[...])
        dgc = (dm * f["y_conv"] * gc * (1.0 - gc)).astype(BF16)
        dgs = (dm * f["y_ssm"] * gs * (1.0 - gs)).astype(BF16)
        dgt_ref[:, 0:D] = dgc
        dgt_ref[:, D:2 * D] = dgs
        part = jnp.concatenate([jnp.sum(dgc.astype(F32), axis=0, keepdims=True),
                                jnp.sum(dgs.astype(F32), axis=0, keepdims=True)], axis=1)
        _acc_rows(dbg_ref, part, i == 0)
        dyc = (dm * gc).astype(BF16)
        dyc_ref[...] = dyc
        dys = dm * gs
        dsv = (dys * sig).astype(BF16)
        dsg = (dys * sv * sig * (1.0 - sig)).astype(BF16)
        dsv_ref[...] = dsv
        dsg_ref[...] = dsg
        dyg = _dot_nt(dsv, wv_ref[...]) + _dot_nt(dsg, wg_ref[...])
        dyss_ref[...] = dyg * _gelu_grad(f["yss"])
        dz3 = _dot_nt(dyc, wp_ref[...])
        z2, sz, zn = f["z2"], f["sz"], f["zn"]
        dz2 = dz3 * sz * (1.0 + z2 * (1.0 - sz))
        _acc_rows(dlng_ref, jnp.sum(dz2 * zn, axis=0, keepdims=True), i == 0)
        _acc_rows(dlnb_ref, jnp.sum(dz2, axis=0, keepdims=True), i == 0)
        dzn = dz2 * lng_ref[...]
        dz1_ref[...] = f["rstd"] * (dzn - jnp.mean(dzn, axis=-1, keepdims=True)
                                    - zn * jnp.mean(dzn * zn, axis=-1, keepdims=True))

    def row(n):
        return pl.BlockSpec((tm, n), lambda i: (i, 0))

    def tot(n):
        return pl.BlockSpec((1, n), lambda i: (0, 0))

    return pl.pallas_call(
        body, name=name, grid=(L // tm,),
        out_shape=[jax.ShapeDtypeStruct((L, D), BF16), jax.ShapeDtypeStruct((L, 2 * D), BF16),
                   jax.ShapeDtypeStruct((L, D), BF16), jax.ShapeDtypeStruct((L, DC), BF16),
                   jax.ShapeDtypeStruct((L, DC), F32), jax.ShapeDtypeStruct((L, DS), BF16),
                   jax.ShapeDtypeStruct((L, D), BF16), jax.ShapeDtypeStruct((L, D), BF16),
                   jax.ShapeDtypeStruct((L, DS), F32),
                   jax.ShapeDtypeStruct((1, 2 * D), F32), jax.ShapeDtypeStruct((1, DC), F32),
                   jax.ShapeDtypeStruct((1, DC), F32)],
        in_specs=[row(D), row(DC), row(DS), row(2 * D), _res((1, DC)), _res((1, DC)),
                  _res((DC, D)), _res((DS, D)), _res((DS, D)), _res((D, D))],
        out_specs=[row(D), row(2 * D), row(D), row(DC), row(DC), row(DS), row(D), row(D), row(DS),
                   tot(2 * D), tot(DC), tot(DC)],
        compiler_params=_params("arbitrary"),
    )(dh, z1, yss, gate, lng, lnb, wp, wv, wg, wo)


def _final(h, g, tgt, name):
    L = h.shape[0]
    tm = _tile(L, 528)

    def body(h_ref, g_ref, t_ref, dh_ref, loss_ref, dg_ref):
        i = pl.program_id(0)
        hv = h_ref[...]
        gv = g_ref[...]
        y, r = _rms(hv, gv)
        row = i * tm + lax.broadcasted_iota(jnp.int32, (tm, 1), 0)
        e = jnp.where(row >= FRONT, y - t_ref[...], 0.0)
        dy = e * (1.0 / D)
        part = 0.5 * jnp.sum(jnp.sum(e * dy, axis=1, keepdims=True), axis=0, keepdims=True)
        dx, xh = _rms_bwd(dy, hv, r, gv)
        dh_ref[...] = dx
        _acc_rows(loss_ref, part, i == 0)
        _acc_rows(dg_ref, jnp.sum(dy * xh, axis=0, keepdims=True), i == 0)

    row = pl.BlockSpec((tm, D), lambda i: (i, 0))
    return pl.pallas_call(
        body, name=name, grid=(L // tm,),
        out_shape=[jax.ShapeDtypeStruct((L, D), F32), jax.ShapeDtypeStruct((1, 1), F32),
                   jax.ShapeDtypeStruct((1, D), F32)],
        in_specs=[row, _res((1, D)), row],
        out_specs=[row, pl.BlockSpec((1, 1), lambda i: (0, 0)), pl.BlockSpec((1, D), lambda i: (0, 0))],
        compiler_params=_params("arbitrary"),
    )(h, g, tgt)


def _ssm_disc(lam_re, lam_im, log_dt, b_re, b_im):
    lam = lax.complex(lam_re, lam_im)
    dt = jnp.exp(log_dt)[:, None]
    lam_bar = jnp.exp(lam * dt)
    bbar = ((lam_bar - 1.0) / lam)[..., None] * lax.complex(b_re, b_im)
    return jnp.real(lam_bar), jnp.imag(lam_bar), jnp.real(bbar), jnp.imag(bbar)


def _bdiag_in(m):
    return jnp.einsum("gph,gk->ghkp", m, jnp.eye(G, dtype=m.dtype)).reshape(G * H, G * P)


def _bdiag_out(m):
    return jnp.einsum("ghp,gk->gpkh", m, jnp.eye(G, dtype=m.dtype)).reshape(G * P, G * H)


def _diag_blocks(m4):
    return jnp.einsum("qiaib->qiab", m4).reshape(G, m4.shape[2], m4.shape[4])


def _pack(parts, rows_mult=8):
    flat = jnp.concatenate([p.reshape(-1).astype(F32) for p in parts])
    n = flat.shape[0]
    tot = -(-n // (128 * rows_mult)) * (128 * rows_mult)
    return jnp.pad(flat, (0, tot - n)).reshape(tot // 128, 128)


def _unpack(buf, shapes):
    flat = buf.reshape(-1)
    out, o = [], 0
    for s in shapes:
        n = math.prod(s)
        out.append(flat[o:o + n].reshape(s))
        o += n
    return out


def kernel(x, meta_tokens, ffn1_norm, ffn1_w1, ffn1_w3, ffn1_w2, mix_norm, w_in, b_gate, conv_dw, conv_dw_b, conv_ln_g, conv_ln_b, conv_proj, ssm_lam_re, ssm_lam_im, ssm_log_dt, ssm_b_re, ssm_b_im, ssm_c_re, ssm_c_im, ssm_d, ssm_w_v, ssm_w_g, w_out, ffn2_norm, ffn2_w1, ffn2_w3, ffn2_w2, final_norm, loss_target, m_meta_tokens, m_ffn1_norm, m_ffn1_w1, m_ffn1_w3, m_ffn1_w2, m_mix_norm, m_w_in, m_b_gate, m_conv_dw, m_conv_dw_b, m_conv_ln_g, m_conv_ln_b, m_conv_proj, m_ssm_lam_re, m_ssm_lam_im, m_ssm_log_dt, m_ssm_b_re, m_ssm_b_im, m_ssm_c_re, m_ssm_c_im, m_ssm_d, m_ssm_w_v, m_ssm_w_g, m_w_out, m_ffn2_norm, m_ffn2_w1, m_ffn2_w3, m_ffn2_w2, m_final_norm, v_meta_tokens, v_ffn1_norm, v_ffn1_w1, v_ffn1_w3, v_ffn1_w2, v_mix_norm, v_w_in, v_b_gate, v_conv_dw, v_conv_dw_b, v_conv_ln_g, v_conv_ln_b, v_conv_proj, v_ssm_lam_re, v_ssm_lam_im, v_ssm_log_dt, v_ssm_b_re, v_ssm_b_im, v_ssm_c_re, v_ssm_c_im, v_ssm_d, v_ssm_w_v, v_ssm_w_g, v_w_out, v_ffn2_norm, v_ffn2_w1, v_ffn2_w3, v_ffn2_w2, v_final_norm):
    args = dict(locals())
    names = ["meta_tokens", "ffn1_norm", "ffn1_w1", "ffn1_w3", "ffn1_w2", "mix_norm", "w_in", "b_gate",
             "conv_dw", "conv_dw_b", "conv_ln_g", "conv_ln_b", "conv_proj", "ssm_lam_re", "ssm_lam_im",
             "ssm_log_dt", "ssm_b_re", "ssm_b_im", "ssm_c_re", "ssm_c_im", "ssm_d", "ssm_w_v", "ssm_w_g",
             "w_out", "ffn2_norm", "ffn2_w1", "ffn2_w3", "ffn2_w2", "final_norm"]
    big = ["ffn1_w1", "ffn1_w3", "ffn1_w2", "w_in", "conv_proj", "ssm_w_v", "ssm_w_g", "w_out",
           "ffn2_w1", "ffn2_w3", "ffn2_w2"]
    small = [n for n in names if n not in big]

    xs = x[0]
    S = xs.shape[0]
    L = FRONT + S
    T = L // NSEG
    jx, jy = lax.axis_index("x"), lax.axis_index("y")
    chip = 2 * jx + jy

    sm = _gather_all(_pack([meta_tokens, conv_dw[0]]), "gather_small")[0::2].reshape(NSH, -1)
    nmt = NMETA * (D // NSH)
    ndw = KW * (DC // NSH)
    meta_full = sm[:, :nmt].reshape(NSH, NMETA, D // NSH).transpose(1, 0, 2).reshape(NMETA, D)
    dw_full = sm[:, nmt:nmt + ndw].reshape(NSH, KW, DC // NSH).transpose(1, 0, 2).reshape(KW, DC)
    dw_pad = jnp.pad(dw_full, ((0, KWP - KW), (0, 0)))
    tposed = ("ffn1_w1", "ffn1_w3", "ffn2_w1", "ffn2_w3")
    gw = _gather_chips([(args[n][0].T if n in tposed else args[n][0]).astype(BF16) for n in big],
                       "gather_weights")
    gw = dict(zip(big, gw))

    def cols(w):
        return w.transpose(1, 0, 2).reshape(w.shape[1], -1)

    w_in_f = cols(gw["w_in"])
    wp_f, wv_f, wg_f = cols(gw["conv_proj"]), cols(gw["ssm_w_v"]), cols(gw["ssm_w_g"])
    wo_f = gw["w_out"].reshape(D, D)

    disc_in = (ssm_lam_re[0], ssm_lam_im[0], ssm_log_dt[0], ssm_b_re[0], ssm_b_im[0])
    (lbr, lbi, bbr, bbi), disc_vjp = jax.vjp(_ssm_disc, *disc_in)
    lam_t = jnp.exp(lax.complex(ssm_lam_re[0], ssm_lam_im[0]) * (jnp.exp(ssm_log_dt[0])[:, None] * T))
    lamp = jnp.concatenate([lbr.reshape(1, NST), lbi.reshape(1, NST), jnp.real(lam_t).reshape(1, NST),
                            jnp.imag(lam_t).reshape(1, NST), jnp.zeros((4, NST), F32)], axis=0)
    bre_bd, bim_bd = _bdiag_in(bbr).astype(BF16), _bdiag_in(bbi).astype(BF16)
    cre_bd, cim_bd = _bdiag_out(ssm_c_re[0]).astype(BF16), _bdiag_out(ssm_c_im[0]).astype(BF16)

    h0 = jnp.concatenate([jnp.zeros((FRONT - NMETA, D), F32), meta_full, xs], axis=0)
    tgt = jnp.pad(loss_target[0], ((FRONT, 0), (0, 0)))
    h1, a1, b1 = _ffn_fwd(h0, ffn1_norm, gw["ffn1_w1"], gw["ffn1_w3"], gw["ffn1_w2"], "ffn1_fwd")
    vg, uf, gate = _mix_in_fwd(h1, mix_norm, w_in_f, b_gate, "mix_in_fwd")
    z1 = _conv_fwd(vg, dw_pad, conv_dw_b, "conv_fwd")
    yss = _ssm_fwd(uf, bre_bd, bim_bd, cre_bd, cim_bd, lamp, ssm_d, "ssm_fwd")
    h2 = _merge_fwd(h1, z1, yss, gate, conv_ln_g, conv_ln_b, wp_f, wv_f, wg_f, wo_f, "merge_fwd")
    h3, a2, b2 = _ffn_fwd(h2, ffn2_norm, gw["ffn2_w1"], gw["ffn2_w3"], gw["ffn2_w2"], "ffn2_fwd")

    dh3, loss_part, d_final = _final(h3, final_norm.reshape(1, D), tgt, "final")
    dh2, da2, db2, s2, n2, d_ffn2_norm = _ffn_bwd(
        h2, ffn2_norm, dh3, a2, b2, gw["ffn2_w1"], gw["ffn2_w3"], gw["ffn2_w2"], "ffn2_bwd")
    gbig = {}
    gbig["ffn2_w1"] = _wgrad(da2, n2, "ffn2_dw1")
    gbig["ffn2_w3"] = _wgrad(db2, n2, "ffn2_dw3")
    gbig["ffn2_w2"] = _wgrad(s2, dh3, "ffn2_dw2", 0.5)
    (m_b, dgate, dyc, z3, dz1, yg, dsv, dsg, dyss, d_b_gate, d_ln_g, d_ln_b) = _merge_bwd(
        dh2, z1, yss, gate, conv_ln_g, conv_ln_b, wp_f, wv_f, wg_f, wo_f, "merge_bwd")
    gbig["w_out"] = _wgrad(m_b, dh2, "dw_out").reshape(NSH, D // NSH, D)

    def shard_cols(gm):
        return gm.reshape(gm.shape[0], NSH, -1).transpose(1, 0, 2)

    gbig["conv_proj"] = shard_cols(_wgrad(z3, dyc, "dw_proj"))
    gbig["ssm_w_v"] = shard_cols(_wgrad(yg, dsv, "dw_v"))
    gbig["ssm_w_g"] = shard_cols(_wgrad(yg, dsg, "dw_g"))
    dv, dgl, ddw, d_dw_b = _conv_bwd(dz1, vg, dw_pad, "conv_bwd")
    duf, dbre, dbim, dcre, dcim, dlam, d_ssm_d = _ssm_bwd(
        uf, dyss, bre_bd, bim_bd, cre_bd, cim_bd, lamp, ssm_d, "ssm_bwd")
    dh1, u_b, d_mix_norm = _mix_in_bwd(h1, mix_norm, dh2, dv, dgl, duf, dgate, w_in_f, "mix_in_bwd")
    gbig["w_in"] = shard_cols(jnp.concatenate(
        [_wgrad(u_b, dv, "dw_in_v"), _wgrad(u_b, dgl, "dw_in_g"), _wgrad(u_b, duf, "dw_in_u"),
         _wgrad(u_b, dgate, "dw_in_gate")], axis=1))
    dh0, da1, db1, s1, n1, d_ffn1_norm = _ffn_bwd(
        h0, ffn1_norm, dh1, a1, b1, gw["ffn1_w1"], gw["ffn1_w3"], gw["ffn1_w2"], "ffn1_bwd")
    gbig["ffn1_w1"] = _wgrad(da1, n1, "ffn1_dw1")
    gbig["ffn1_w3"] = _wgrad(db1, n1, "ffn1_dw3")
    gbig["ffn1_w2"] = _wgrad(s1, dh1, "ffn1_dw2", 0.5)
    grad_x = dh0[FRONT:][None]

    d_bbr = _diag_blocks(dbre.reshape(NQ, 8, H, 8, P)).transpose(0, 2, 1)
    d_bbi = _diag_blocks(dbim.reshape(NQ, 8, H, 8, P)).transpose(0, 2, 1)
    d_c_re = _diag_blocks(dcre.reshape(NQ, 8, P, 8, H)).transpose(0, 2, 1)
    d_c_im = _diag_blocks(dcim.reshape(NQ, 8, P, 8, H)).transpose(0, 2, 1)
    d_lbr = dlam[:, 0, :].reshape(G, P)
    d_lbi = dlam[:, 1, :].reshape(G, P)
    d_lam_re, d_lam_im, d_log_dt, d_b_re, d_b_im = disc_vjp((d_lbr, d_lbi, d_bbr, d_bbi))

    glist = [gbig[n] for n in big]
    core = lax.axis_index("c")
    sib = _pair_exchange(glist, "pair_exchange")
    pair = []
    for n, g_, s_ in zip(big, glist, sib):
        half = g_.shape[1] // 2
        mine = lax.dynamic_slice_in_dim(g_, core * half, half, axis=1)
        pair.append(_add_pair(mine.reshape(NSH * half, -1), s_.reshape(NSH * half, -1),
                              "pair_" + n).reshape(s_.shape))
    recv = _scatter_chips(pair, "scatter_grads")
    halves = [_sum_slots(r, "sum_" + n) for n, r in zip(big, recv)]
    full = _swap_halves(halves, "swap_halves")
    out_g, out_d, out_m, out_v = {}, {}, {}, {}
    for n, f in zip(big, full):
        shp = args[n].shape
        g2 = f.reshape(f.shape[0] * f.shape[1], f.shape[2])
        if n in tposed:
            g2 = g2.T
        d2, m2, v2 = _adamw(args[n].reshape(g2.shape), g2, args["m_" + n].reshape(g2.shape),
                            args["v_" + n].reshape(g2.shape), "adamw_" + n)
        out_g[n], out_d[n], out_m[n], out_v[n] = (t.reshape(shp) for t in (g2, d2, m2, v2))

    sg = {"meta_tokens": dh0[FRONT - NMETA:FRONT], "ffn1_norm": d_ffn1_norm, "mix_norm": d_mix_norm,
          "b_gate": d_b_gate, "conv_dw": ddw[:KW], "conv_dw_b": d_dw_b, "conv_ln_g": d_ln_g, "conv_ln_b": d_ln_b,
          "ssm_lam_re": d_lam_re, "ssm_lam_im": d_lam_im, "ssm_log_dt": d_log_dt, "ssm_b_re": d_b_re,
          "ssm_b_im": d_b_im, "ssm_c_re": d_c_re, "ssm_c_im": d_c_im, "ssm_d": d_ssm_d,
          "ffn2_norm": d_ffn2_norm, "final_norm": d_final}
    sshapes = [sg[n].shape for n in small]
    tot = _sum_slots(_gather_all(_pack([sg[n] for n in small]), "gather_small_grads"), "sum_small")
    sgr = dict(zip(small, _unpack(tot, sshapes)))
    sgr["meta_tokens"] = lax.dynamic_slice_in_dim(sgr["meta_tokens"], chip * (D // NSH), D // NSH, axis=1)
    sgr["conv_dw"] = lax.dynamic_slice_in_dim(sgr["conv_dw"], chip * (DC // NSH), DC // NSH, axis=1)
    pshapes = [args[n].shape for n in small]
    d_s, m_s, v_s = _adamw(_pack([args[n] for n in small]), _pack([sgr[n] for n in small]),
                           _pack([args["m_" + n] for n in small]), _pack([args["v_" + n] for n in small]),
                           "adamw_small")
    for n, g_, d_, m_, v_ in zip(small, [sgr[n] for n in small], _unpack(d_s, pshapes),
                                 _unpack(m_s, pshapes), _unpack(v_s, pshapes)):
        out_g[n], out_d[n], out_m[n], out_v[n] = g_.reshape(args[n].shape), d_, m_, v_

    loss = lax.psum(loss_part[0, 0], ("x", "y", "c"))
    return (loss, grad_x, *[out_g[n] for n in names], *[out_d[n] for n in names],
            *[out_m[n] for n in names], *[out_v[n] for n in names])
```

```python
import math

import jax
import jax.numpy as jnp
from jax import lax
from jax.experimental import pallas as pl
from jax.experimental.pallas import tpu as pltpu

F32 = jnp.float32
BF16 = jnp.bfloat16

D = 1024
NSH = 4
F = 2816
FS = F // NSH
DC = 512
DS = 512
DIN = 2 * DC + DS + 2 * D
WS = DIN // NSH
KW = 31
KWP = 32
NMETA = 16
FRONT = 128
G, P, H = 32, 64, 16
NST = G * P
NQ = 4
QS = NST // NQ
QU = DS // NQ
NSEG = 32
NGRP = NSEG // 8
NCH = 8
SOFF = 8
EPS = 1e-6
LR, B1, B2, AEPS, WD, STEP = 1e-3, 0.9, 0.999, 1e-8, 0.01, 10
VMEM_LIMIT = 58 * 1024 * 1024
MESH = pl.DeviceIdType.MESH
ANY = pl.BlockSpec(memory_space=pl.ANY)


def _params(*sem):
    return pltpu.CompilerParams(dimension_semantics=sem, vmem_limit_bytes=VMEM_LIMIT)


def _res(shape):
    nd = len(shape)
    return pl.BlockSpec(shape, lambda *_: (0,) * nd, pipeline_mode=pl.Buffered(1))


def _tile(n, cap, mult=16):
    best = None
    for t in range(mult, min(n, cap) + 1, mult):
        if n % t == 0:
            best = t
    assert best is not None, (n, cap, mult)
    return best


def _dot(a, b):
    return jnp.dot(a, b, preferred_element_type=F32)


def _dot_nt(a, b):
    return lax.dot_general(a, b, (((1,), (1,)), ((), ())), preferred_element_type=F32)


def _dot_tn(a, b):
    return lax.dot_general(a, b, (((0,), (0,)), ((), ())), preferred_element_type=F32)


def _sigmoid(x):
    return 1.0 / (1.0 + jnp.exp(-x))


_GC = math.sqrt(2.0 / math.pi)
_GA = 0.044715


def _gelu(x):
    return 0.5 * x * (1.0 + jnp.tanh(_GC * (x + _GA * x * x * x)))


def _gelu_grad(x):
    t = jnp.tanh(_GC * (x + _GA * x * x * x))
    return 0.5 * (1.0 + t) + 0.5 * x * (1.0 - t * t) * _GC * (1.0 + 3.0 * _GA * x * x)


def _rms(hv, g):
    r = lax.rsqrt(jnp.mean(hv * hv, axis=-1, keepdims=True) + EPS)
    return hv * r * g, r


def _rms_bwd(dn, hv, r, g):
    xh = hv * r
    dxh = dn * g
    return r * (dxh - xh * jnp.mean(dxh * xh, axis=-1, keepdims=True)), xh


def _acc_rows(ref, part, first):
    @pl.when(first)
    def _():
        ref[...] = part

    @pl.when(jnp.logical_not(first))
    def _():
        ref[...] += part


def _coords():
    return lax.axis_index("x"), lax.axis_index("y"), lax.axis_index("c")


def _flip(v, d):
    return 1 - v if d else v


def _run(local, remote):
    for cp in local + remote:
        cp.start()
    for cp in remote:
        cp.wait()
    for cp in local:
        cp.wait()


def _via_vmem(src, dst, stage, sems, i):
    return (pltpu.make_async_copy(src, stage, sems.at[2 * i]), pltpu.make_async_copy(stage, dst, sems.at[2 * i + 1]))


def _run_staged(staged, remote):
    for load, _ in staged:
        load.start()
    for cp in remote:
        cp.start()
    for load, store in staged:
        load.wait()
        store.start()
    for cp in remote:
        cp.wait()
    for _, store in staged:
        store.wait()


_REL3 = ((1, 0), (0, 1), (1, 1))


def _gather_chips(shards, name):
    n = len(shards)

    def body(*refs):
        ins, outs = refs[:n], refs[n:2 * n]
        send, recv, fsend, frecv, loc = refs[2 * n:2 * n + 5]
        stage = refs[2 * n + 5:]
        x, y, c = _coords()
        me = 2 * x + y
        own = [_via_vmem(ins[t], outs[t].at[me], stage[t], loc, t) for t in range(n)]
        first, passed = [], []
        for t in range(n):
            half = shards[t].shape[0] // 2
            mine, theirs = pl.ds(c * half, half), pl.ds((1 - c) * half, half)
            for k, (dx, dy) in enumerate(_REL3):
                px, py = _flip(x, dx), _flip(y, dy)
                first.append(pltpu.make_async_remote_copy(
                    src_ref=ins[t].at[mine], dst_ref=outs[t].at[me, mine],
                    send_sem=send.at[3 * t + k], recv_sem=recv.at[3 * t + k],
                    device_id=(px, py, c), device_id_type=MESH))
                passed.append((
                    pltpu.make_async_remote_copy(
                        src_ref=outs[t].at[2 * px + py, mine], dst_ref=outs[t].at[2 * px + py, mine],
                        send_sem=fsend.at[3 * t + k], recv_sem=frecv.at[3 * t + k],
                        device_id=(x, y, 1 - c), device_id_type=MESH),
                    pltpu.make_async_remote_copy(
                        src_ref=outs[t].at[2 * px + py, theirs], dst_ref=outs[t].at[2 * px + py, theirs],
                        send_sem=fsend.at[3 * t + k], recv_sem=frecv.at[3 * t + k],
                        device_id=(x, y, 1 - c), device_id_type=MESH)))
        for load, _ in own:
            load.start()
        for cp in first:
            cp.start()
        for load, store in own:
            load.wait()
            store.start()
        for cp, (fwd, _) in zip(first, passed):
            cp.wait_recv()
            fwd.start()
        for cp, (fwd, back) in zip(first, passed):
            cp.wait_send()
            fwd.wait_send()
            back.wait_recv()
        for _, store in own:
            store.wait()

    return pl.pallas_call(
        body, name=name,
        out_shape=[jax.ShapeDtypeStruct((NSH,) + s.shape, s.dtype) for s in shards],
        in_specs=[ANY] * n, out_specs=[ANY] * n,
        scratch_shapes=[pltpu.SemaphoreType.DMA((3 * n,)) for _ in range(4)] + [pltpu.SemaphoreType.DMA((2 * n,))]
        + [pltpu.VMEM(s.shape, s.dtype) for s in shards],
        compiler_params=pltpu.CompilerParams(vmem_limit_bytes=VMEM_LIMIT),
    )(*shards)


_REL7 = tuple((dx, dy, dc) for dx in (0, 1) for dy in (0, 1) for dc in (0, 1))[1:]


def _gather_all(a, name):
    def body(a_ref, o_ref, send, recv, loc):
        x, y, c = _coords()
        me = 4 * x + 2 * y + c
        local = [pltpu.make_async_copy(a_ref, o_ref.at[me], loc.at[0])]
        remote = [pltpu.make_async_remote_copy(
            src_ref=a_ref, dst_ref=o_ref.at[me], send_sem=send.at[k], recv_sem=recv.at[k],
            device_id=(_flip(x, dx), _flip(y, dy), _flip(c, dc)), device_id_type=MESH)
            for k, (dx, dy, dc) in enumerate(_REL7)]
        _run(local, remote)

    return pl.pallas_call(
        body, name=name,
        out_shape=jax.ShapeDtypeStruct((8,) + a.shape, a.dtype),
        in_specs=[ANY], out_specs=ANY,
        scratch_shapes=[pltpu.SemaphoreType.DMA((7,)), pltpu.SemaphoreType.DMA((7,)),
                        pltpu.SemaphoreType.DMA((1,))],
    )(a)


def _pair_exchange(grads, name):
    n = len(grads)

    def body(*refs):
        ins, outs = refs[:n], refs[n:2 * n]
        send, recv = refs[2 * n:]
        x, y, c = _coords()
        remote = []
        for t in range(n):
            half = grads[t].shape[1] // 2
            remote.append(pltpu.make_async_remote_copy(
                src_ref=ins[t].at[:, pl.ds((1 - c) * half, half)], dst_ref=outs[t],
                send_sem=send.at[t], recv_sem=recv.at[t],
                device_id=(x, y, 1 - c), device_id_type=MESH))
        _run([], remote)

    return pl.pallas_call(
        body, name=name,
        out_shape=[jax.ShapeDtypeStruct((NSH, g.shape[1] // 2, g.shape[2]), g.dtype) for g in grads],
        in_specs=[ANY] * n, out_specs=[ANY] * n,
        scratch_shapes=[pltpu.SemaphoreType.DMA((n,)), pltpu.SemaphoreType.DMA((n,))],
    )(*grads)


def _scatter_chips(sums, name):
    n = len(sums)

    def body(*refs):
        ins, outs = refs[:n], refs[n:2 * n]
        send, recv, loc = refs[2 * n:2 * n + 3]
        stage = refs[2 * n + 3:]
        x, y, c = _coords()
        me = 2 * x + y
        local = [_via_vmem(ins[t].at[me], outs[t].at[me], stage[t], loc, t) for t in range(n)]
        remote = []
        for t in range(n):
            for k, (dx, dy) in enumerate(_REL3):
                px, py = _flip(x, dx), _flip(y, dy)
                remote.append(pltpu.make_async_remote_copy(
                    src_ref=ins[t].at[2 * px + py], dst_ref=outs[t].at[me],
                    send_sem=send.at[3 * t + k], recv_sem=recv.at[3 * t + k],
                    device_id=(px, py, c), device_id_type=MESH))
        _run_staged(local, remote)

    return pl.pallas_call(
        body, name=name,
        out_shape=[jax.ShapeDtypeStruct(s.shape, s.dtype) for s in sums],
        in_specs=[ANY] * n, out_specs=[ANY] * n,
        scratch_shapes=[pltpu.SemaphoreType.DMA((3 * n,)), pltpu.SemaphoreType.DMA((3 * n,)),
                        pltpu.SemaphoreType.DMA((2 * n,))]
        + [pltpu.VMEM(s.shape[1:], s.dtype) for s in sums],
        compiler_params=pltpu.CompilerParams(vmem_limit_bytes=VMEM_LIMIT),
    )(*sums)


def _swap_halves(halves, name):
    n = len(halves)

    def body(*refs):
        ins, outs = refs[:n], refs[n:2 * n]
        send, recv, loc = refs[2 * n:2 * n + 3]
        stage = refs[2 * n + 3:]
        x, y, c = _coords()
        local = [_via_vmem(ins[t], outs[t].at[c], stage[t], loc, t) for t in range(n)]
        remote = [pltpu.make_async_remote_copy(
            src_ref=ins[t], dst_ref=outs[t].at[c], send_sem=send.at[t], recv_sem=recv.at[t],
            device_id=(x, y, 1 - c), device_id_type=MESH) for t in range(n)]
        _run_staged(local, remote)

    return pl.pallas_call(
        body, name=name,
        out_shape=[jax.ShapeDtypeStruct((2,) + h.shape, h.dtype) for h in halves],
        in_specs=[ANY] * n, out_specs=[ANY] * n,
        scratch_shapes=[pltpu.SemaphoreType.DMA((n,)), pltpu.SemaphoreType.DMA((n,)),
                        pltpu.SemaphoreType.DMA((2 * n,))]
        + [pltpu.VMEM(h.shape, h.dtype) for h in halves],
        compiler_params=pltpu.CompilerParams(vmem_limit_bytes=VMEM_LIMIT),
    )(*halves)


def _sum_slots(r, name):
    K, R, C = r.shape
    tr = _tile(R, max(16, (1 << 22) // (K * C)))

    def body(r_ref, o_ref):
        acc = r_ref[0].astype(F32)
        for k in range(1, K):
            acc = acc + r_ref[k].astype(F32)
        o_ref[...] = acc

    return pl.pallas_call(
        body, name=name, grid=(R // tr,),
        out_shape=jax.ShapeDtypeStruct((R, C), F32),
        in_specs=[pl.BlockSpec((K, tr, C), lambda i: (0, i, 0))],
        out_specs=pl.BlockSpec((tr, C), lambda i: (i, 0)),
        compiler_params=_params("parallel"),
    )(r)


def _add_pair(a, b, name):
    R, C = a.shape
    tr = _tile(R, max(16, (1 << 20) // C))

    def body(a_ref, b_ref, o_ref):
        o_ref[...] = (a_ref[...].astype(F32) + b_ref[...].astype(F32)).astype(BF16)

    spec = pl.BlockSpec((tr, C), lambda i: (i, 0))
    return pl.pallas_call(
        body, name=name, grid=(R // tr,),
        out_shape=jax.ShapeDtypeStruct((R, C), BF16),
        in_specs=[spec, spec], out_specs=spec,
        compiler_params=_params("parallel"),
    )(a, b)


def _adamw(w, g, m, v, name):
    R, C = w.shape
    tr = _tile(R, max(8, (1 << 18) // C), 8)
    c1 = 1.0 / (1.0 - B1 ** STEP)
    c2 = 1.0 / (1.0 - B2 ** STEP)

    def body(w_ref, g_ref, m_ref, v_ref, d_ref, nm_ref, nv_ref):
        gv = g_ref[...]
        nm = B1 * m_ref[...] + (1.0 - B1) * gv
        nv = B2 * v_ref[...] + (1.0 - B2) * gv * gv
        nm_ref[...] = nm
        nv_ref[...] = nv
        d_ref[...] = -LR * ((nm * c1) / (jnp.sqrt(nv * c2) + AEPS) + WD * w_ref[...])

    spec = pl.BlockSpec((tr, C), lambda i: (i, 0))
    return pl.pallas_call(
        body, name=name, grid=(R // tr,),
        out_shape=[jax.ShapeDtypeStruct((R, C), F32)] * 3,
        in_specs=[spec] * 4, out_specs=[spec] * 3,
        compiler_params=_params("parallel"),
    )(w, g, m, v)


def _ffn_fwd(h, g, w1, w3, w2, name):
    L = h.shape[0]
    tm = _tile(L, 528)

    def body(h_ref, g_ref, w1_ref, w3_ref, w2_ref, o_ref, a_ref, b_ref, n_s, acc_s):
        j = pl.program_id(1)

        @pl.when(j == 0)
        def _():
            hv = h_ref[...]
            n, _ = _rms(hv, g_ref[...])
            n_s[...] = n.astype(BF16)
            acc_s[...] = hv

        n = n_s[...]
        a = _dot_nt(n, w1_ref[j])
        b = _dot_nt(n, w3_ref[j])
        a_ref[0] = a.astype(BF16)
        b_ref[0] = b.astype(BF16)
        s = (a * _sigmoid(a) * b).astype(BF16)
        acc_s[...] += 0.5 * _dot(s, w2_ref[j])

        @pl.when(j == NSH - 1)
        def _():
            o_ref[...] = acc_s[...]

    row = pl.BlockSpec((tm, D), lambda i, j: (i, 0))
    hid = pl.BlockSpec((1, tm, FS), lambda i, j: (j, i, 0))
    return pl.pallas_call(
        body, name=name, grid=(L // tm, NSH),
        out_shape=[jax.ShapeDtypeStruct((L, D), F32),
                   jax.ShapeDtypeStruct((NSH, L, FS), BF16), jax.ShapeDtypeStruct((NSH, L, FS), BF16)],
        in_specs=[row, _res((1, D)), _res((NSH, FS, D)), _res((NSH, FS, D)), _res((NSH, FS, D))],
        out_specs=[row, hid, hid],
        scratch_shapes=[pltpu.VMEM((tm, D), BF16), pltpu.VMEM((tm, D), F32)],
        compiler_params=_params("arbitrary", "arbitrary"),
    )(h, g, w1, w3, w2)


def _ffn_bwd(h, g, dout, a, b, w1, w3, w2, name):
    L = h.shape[0]
    tm = _tile(L, 352)

    def body(h_ref, g_ref, do_ref, a_ref, b_ref, w1_ref, w3_ref, w2_ref,
             dh_ref, da_ref, db_ref, s_ref, n_ref, dg_ref, dob_s, dn_s):
        i, j = pl.program_id(0), pl.program_id(1)

        @pl.when(j == 0)
        def _():
            n, _ = _rms(h_ref[...], g_ref[...])
            n_ref[...] = n.astype(BF16)
            dob_s[...] = do_ref[...].astype(BF16)
            dn_s[...] = jnp.zeros_like(dn_s)

        av = a_ref[0].astype(F32)
        bv = b_ref[0].astype(F32)
        sig = _sigmoid(av)
        sa = av * sig
        ds = 0.5 * _dot_nt(dob_s[...], w2_ref[j])
        s_ref[0] = (sa * bv).astype(BF16)
        da = (ds * bv * sig * (1.0 + av * (1.0 - sig))).astype(BF16)
        db = (ds * sa).astype(BF16)
        da_ref[0] = da
        db_ref[0] = db
        dn_s[...] += _dot(da, w1_ref[j]) + _dot(db, w3_ref[j])

        @pl.when(j == NSH - 1)
        def _():
            hv = h_ref[...]
            gv = g_ref[...]
            r = lax.rsqrt(jnp.mean(hv * hv, axis=-1, keepdims=True) + EPS)
            dn = dn_s[...]
            dx, xh = _rms_bwd(dn, hv, r, gv)
            dh_ref[...] = do_ref[...] + dx
            _acc_rows(dg_ref, jnp.sum(dn * xh, axis=0, keepdims=True), i == 0)

    row = pl.BlockSpec((tm, D), lambda i, j: (i, 0))
    hid = pl.BlockSpec((1, tm, FS), lambda i, j: (j, i, 0))
    return pl.pallas_call(
        body, name=name, grid=(L // tm, NSH),
        out_shape=[jax.ShapeDtypeStruct((L, D), F32)]
        + [jax.ShapeDtypeStruct((NSH, L, FS), BF16)] * 3
        + [jax.ShapeDtypeStruct((L, D), BF16), jax.ShapeDtypeStruct((1, D), F32)],
        in_specs=[row, _res((1, D)), row, hid, hid,
                  _res((NSH, FS, D)), _res((NSH, FS, D)), _res((NSH, FS, D))],
        out_specs=[row, hid, hid, hid, row, pl.BlockSpec((1, D), lambda i, j: (0, 0))],
        scratch_shapes=[pltpu.VMEM((tm, D), BF16), pltpu.VMEM((tm, D), F32)],
        compiler_params=_params("arbitrary", "arbitrary"),
    )(h, g, dout, a, b, w1, w3, w2)


def _wgrad(xm, ym, name, scale=1.0):
    xs, ys = xm.ndim == 3, ym.ndim == 3
    assert not (xs and ys)
    L = xm.shape[-2]
    K, N = xm.shape[-1], ym.shape[-1]
    tl = _tile(L, 1056)
    nl = L // tl
    if xs or ys:
        tn, grid_n = N, NSH
    else:
        tn = _tile(N, 1024, 128)
        grid_n = N // tn

    def body(x_ref, y_ref, o_ref, acc_s):
        l = pl.program_id(1)
        xv = x_ref[0] if xs else x_ref[...]
        yv = y_ref[0] if ys else y_ref[...]
        part = _dot_tn(xv.astype(BF16), yv.astype(BF16))
        _acc_rows(acc_s, part, l == 0)

        @pl.when(l == nl - 1)
        def _():
            res = (acc_s[...] * scale).astype(BF16)
            if xs or ys:
                o_ref[0] = res
            else:
                o_ref[...] = res

    if xs:
        x_spec = pl.BlockSpec((1, tl, K), lambda n, l: (n, l, 0))
        y_spec = pl.BlockSpec((tl, N), lambda n, l: (l, 0))
        o_spec = pl.BlockSpec((1, K, N), lambda n, l: (n, 0, 0))
        o_shape = (NSH, K, N)
    elif ys:
        x_spec = pl.BlockSpec((tl, K), lambda n, l: (l, 0))
        y_spec = pl.BlockSpec((1, tl, N), lambda n, l: (n, l, 0))
        o_spec = pl.BlockSpec((1, K, N), lambda n, l: (n, 0, 0))
        o_shape = (NSH, K, N)
    else:
        x_spec = pl.BlockSpec((tl, K), lambda n, l: (l, 0))
        y_spec = pl.BlockSpec((tl, tn), lambda n, l: (l, n))
        o_spec = pl.BlockSpec((K, tn), lambda n, l: (0, n))
        o_shape = (K, N)
    return pl.pallas_call(
        body, name=name, grid=(grid_n, nl),
        out_shape=jax.ShapeDtypeStruct(o_shape, BF16),
        in_specs=[x_spec, y_spec], out_specs=o_spec,
        scratch_shapes=[pltpu.VMEM((K, tn), F32)],
        compiler_params=_params("parallel", "arbitrary"),
    )(xm, ym)


def _mix_in_fwd(h, g, w_in, b_gate, name):
    L = h.shape[0]
    tm = _tile(L, 528)

    def body(h_ref, g_ref, w_ref, bg_ref, vg_ref, uf_ref, gt_ref):
        u, _ = _rms(h_ref[...], g_ref[...])
        ub = u.astype(BF16)
        p = [_dot(ub, w_ref[j]) for j in range(NSH)]
        a0, a1 = 2 * DC - WS, 2 * DC + DS - WS
        vg_ref[:, 0:WS] = p[0].astype(BF16)
        vg_ref[:, WS:2 * DC] = p[1][:, 0:a0].astype(BF16)
        uf_ref[...] = p[1][:, a0:a1].astype(BF16)
        gin = jnp.concatenate([p[1][:, a1:], p[2], p[3]], axis=1)
        gt_ref[...] = _sigmoid(gin + bg_ref[...]).astype(BF16)

    def row(n):
        return pl.BlockSpec((tm, n), lambda i: (i, 0))

    return pl.pallas_call(
        body, name=name, grid=(L // tm,),
        out_shape=[jax.ShapeDtypeStruct((L, 2 * DC), BF16), jax.ShapeDtypeStruct((L, DS), BF16),
                   jax.ShapeDtypeStruct((L, 2 * D), BF16)],
        in_specs=[row(D), _res((1, D)), _res((NSH, D, WS)), _res((1, 2 * D))],
        out_specs=[row(2 * DC), row(DS), row(2 * D)],
        compiler_params=_params("parallel"),
    )(h, g, w_in, b_gate)


def _mix_in_bwd(h, g, dres, dv, dgl, duf, dgate, w_in, name):
    L = h.shape[0]
    tm = _tile(L, 528)

    def body(h_ref, g_ref, dr_ref, dv_ref, dgl_ref, duf_ref, dgt_ref, w_ref, dh_ref, u_ref, dp_ref, dgm_ref):
        i = pl.program_id(0)
        hv = h_ref[...]
        gv = g_ref[...]
        u, r = _rms(hv, gv)
        u_ref[...] = u.astype(BF16)
        a0, a1 = 2 * DC - WS, 2 * DC + DS - WS
        b0 = WS - a1
        dp = [jnp.concatenate([dv_ref[...], dgl_ref[:, 0:WS - DC]], axis=1),
              jnp.concatenate([dgl_ref[:, WS - DC:], duf_ref[...], dgt_ref[:, 0:b0]], axis=1),
              dgt_ref[:, b0:b0 + WS], dgt_ref[:, b0 + WS:]]
        du = jnp.zeros((tm, D), F32)
        for j in range(NSH):
            dp_ref[j] = dp[j]
            du = du + _dot_nt(dp[j], w_ref[j])
        dx, xh = _rms_bwd(du, hv, r, gv)
        dh_ref[...] = dr_ref[...] + dx
        _acc_rows(dgm_ref, jnp.sum(du * xh, axis=0, keepdims=True), i == 0)

    def row(n):
        return pl.BlockSpec((tm, n), lambda i: (i, 0))

    return pl.pallas_call(
        body, name=name, grid=(L // tm,),
        out_shape=[jax.ShapeDtypeStruct((L, D), F32), jax.ShapeDtypeStruct((L, D), BF16),
                   jax.ShapeDtypeStruct((NSH, L, WS), BF16), jax.ShapeDtypeStruct((1, D), F32)],
        in_specs=[row(D), _res((1, D)), row(D), row(DC), row(DC), row(DS), row(2 * D), _res((NSH, D, WS))],
        out_specs=[row(D), row(D), pl.BlockSpec((NSH, tm, WS), lambda i: (0, i, 0)),
                   pl.BlockSpec((1, D), lambda i: (0, 0))],
        compiler_params=_params("arbitrary"),
    )(h, g, dres, dv, dgl, duf, dgate, w_in)


def _conv_fwd(vg, dw, dwb, name):
    L = vg.shape[0]
    nc = DC // 128

    def body(v_ref, g_ref, dw_ref, dwb_ref, z_ref, zp_s):
        zp_s[0:KWP, :] = jnp.zeros((KWP, 128), F32)
        zp_s[KWP:, :] = v_ref[...].astype(F32) * _sigmoid(g_ref[...].astype(F32))
        acc = jnp.broadcast_to(dwb_ref[...], (L, 128))
        for k in range(KW):
            acc = acc + dw_ref[k:k + 1, :] * zp_s[pl.ds(k + 2, L), :]
        z_ref[...] = acc

    return pl.pallas_call(
        body, name=name, grid=(nc,),
        out_shape=jax.ShapeDtypeStruct((L, DC), F32),
        in_specs=[pl.BlockSpec((L, 128), lambda c: (0, c)), pl.BlockSpec((L, 128), lambda c: (0, nc + c)),
                  pl.BlockSpec((KWP, 128), lambda c: (0, c)), pl.BlockSpec((1, 128), lambda c: (0, c))],
        out_specs=pl.BlockSpec((L, 128), lambda c: (0, c)),
        scratch_shapes=[pltpu.VMEM((L + KWP, 128), F32)],
        compiler_params=_params("parallel"),
    )(vg, vg, dw, dwb)


def _conv_bwd(dz1, vg, dw, name):
    L = vg.shape[0]
    nc = DC // 128

    def body(dz_ref, v_ref, g_ref, dw_ref, dv_ref, dg_ref, ddw_ref, ddwb_ref, zp_s, dzp_s):
        vv = v_ref[...].astype(F32)
        sg = _sigmoid(g_ref[...].astype(F32))
        zp_s[0:KWP, :] = jnp.zeros((KWP, 128), F32)
        zp_s[KWP:, :] = vv * sg
        dz = dz_ref[...]
        dzp_s[0:L, :] = dz
        dzp_s[L:, :] = jnp.zeros((KWP, 128), F32)
        ddwb_ref[...] = jnp.sum(dz, axis=0, keepdims=True)
        acc = jnp.zeros((L, 128), F32)
        for k in range(KW):
            acc = acc + dw_ref[k:k + 1, :] * dzp_s[pl.ds(KW - 1 - k, L), :]
            ddw_ref[k:k + 1, :] = jnp.sum(dz * zp_s[pl.ds(k + 2, L), :], axis=0, keepdims=True)
        ddw_ref[KW:KWP, :] = jnp.zeros((KWP - KW, 128), F32)
        dv_ref[...] = (acc * sg).astype(BF16)
        dg_ref[...] = (acc * vv * sg * (1.0 - sg)).astype(BF16)

    col = pl.BlockSpec((L, 128), lambda c: (0, c))
    return pl.pallas_call(
        body, name=name, grid=(nc,),
        out_shape=[jax.ShapeDtypeStruct((L, DC), BF16), jax.ShapeDtypeStruct((L, DC), BF16),
                   jax.ShapeDtypeStruct((KWP, DC), F32), jax.ShapeDtypeStruct((1, DC), F32)],
        in_specs=[col, col, pl.BlockSpec((L, 128), lambda c: (0, nc + c)),
                  pl.BlockSpec((KWP, 128), lambda c: (0, c))],
        out_specs=[col, col, pl.BlockSpec((KWP, 128), lambda c: (0, c)), pl.BlockSpec((1, 128), lambda c: (0, c))],
        scratch_shapes=[pltpu.VMEM((L + KWP, 128), F32), pltpu.VMEM((L + KWP, 128), F32)],
        compiler_params=_params("parallel"),
    )(dz1, vg, vg, dw)


NLB = QS // 128


def _lb_store(ref, rows, val):
    for cb in range(NLB):
        ref[cb, rows, :] = val[:, cb * 128:(cb + 1) * 128]


def _lb_load(ref, rows):
    return jnp.concatenate([ref[cb, rows, :] for cb in range(NLB)], axis=1)


def _scan(xr_ref, xi_ref, base, T, ar, ai, atr, ati, reverse):
    W = ar.shape[1]
    ar, ai, atr, ati = (jnp.broadcast_to(v, (8, W)) for v in (ar, ai, atr, ati))
    zero = jnp.zeros((8, W), F32)

    def rows(t, g):
        tt = T - 1 - t if reverse else t
        return pl.ds(base + g * 8 * T + tt, 8, stride=T)

    def make_step(store):
        def step(t, carry):
            out = []
            for g in range(NGRP):
                sr, si = carry[2 * g], carry[2 * g + 1]
                idx = rows(t, g)
                nr = ar * sr - ai * si + _lb_load(xr_ref, idx)
                ni = ar * si + ai * sr + _lb_load(xi_ref, idx)
                if store:
                    _lb_store(xr_ref, idx, nr)
                    _lb_store(xi_ref, idx, ni)
                out += [nr, ni]
            return tuple(out)
        return step

    ends = lax.fori_loop(0, T, make_step(False), (zero,) * (2 * NGRP))
    sub = lax.broadcasted_iota(jnp.int32, (8, W), 0)
    edge = sub == (7 if reverse else 0)
    shift, last = (7, 0) if reverse else (1, 7)
    inr, ini = jnp.zeros((1, W), F32), jnp.zeros((1, W), F32)
    starts = [None] * (2 * NGRP)
    for g in (reversed(range(NGRP)) if reverse else range(NGRP)):
        er, ei = ends[2 * g], ends[2 * g + 1]
        cr, ci = jnp.where(edge, inr, 0.0), jnp.where(edge, ini, 0.0)
        for _ in range(7):
            nr = atr * cr - ati * ci + er
            ni = atr * ci + ati * cr + ei
            cr = jnp.where(edge, inr, pltpu.roll(nr, shift, 0))
            ci = jnp.where(edge, ini, pltpu.roll(ni, shift, 0))
        starts[2 * g], starts[2 * g + 1] = cr, ci
        inr = (atr * cr - ati * ci + er)[last:last + 1]
        ini = (atr * ci + ati * cr + ei)[last:last + 1]
    lax.fori_loop(0, T, make_step(True), tuple(starts))


def _ssm_fwd(uf, bre, bim, cre, cim, lamp, dsk, name):
    L = uf.shape[0]
    T = L // NSEG
    tc = L // NCH

    def body(u_ref, bre_ref, bim_ref, cre_ref, cim_ref, lam_ref, d_ref, y_ref, sr_s, si_s):
        for k in range(NCH):
            sl = slice(k * tc, (k + 1) * tc)
            uk = u_ref[sl, :]
            _lb_store(sr_s, sl, _dot(uk, bre_ref[...]))
            _lb_store(si_s, sl, _dot(uk, bim_ref[...]))
        _scan(sr_s, si_s, 0, T, lam_ref[0:1, :], lam_ref[1:2, :], lam_ref[2:3, :], lam_ref[3:4, :], False)
        for k in range(NCH):
            sl = slice(k * tc, (k + 1) * tc)
            y_ref[sl, :] = (_dot(_lb_load(sr_s, sl).astype(BF16), cre_ref[...])
                            - _dot(_lb_load(si_s, sl).astype(BF16), cim_ref[...])
                            + d_ref[...] * u_ref[sl, :].astype(F32))

    return pl.pallas_call(
        body, name=name, grid=(NQ,),
        out_shape=jax.ShapeDtypeStruct((L, DS), F32),
        in_specs=[pl.BlockSpec((L, QU), lambda q: (0, q)),
                  pl.BlockSpec((QU, QS), lambda q: (q, q)), pl.BlockSpec((QU, QS), lambda q: (q, q)),
                  pl.BlockSpec((QS, QU), lambda q: (q, q)), pl.BlockSpec((QS, QU), lambda q: (q, q)),
                  pl.BlockSpec((8, QS), lambda q: (0, q)), pl.BlockSpec((1, QU), lambda q: (0, q))],
        out_specs=pl.BlockSpec((L, QU), lambda q: (0, q)),
        scratch_shapes=[pltpu.VMEM((NLB, L, 128), F32), pltpu.VMEM((NLB, L, 128), F32)],
        compiler_params=_params("parallel"),
    )(uf, bre, bim, cre, cim, lamp, dsk)


def _ssm_bwd(uf, dyss, bre, bim, cre, cim, lamp, dsk, name):
    L = uf.shape[0]
    T = L // NSEG
    tc = L // NCH

    def body(u_ref, dy_ref, bre_ref, bim_ref, cre_ref, cim_ref, lam_ref, d_ref,
             du_ref, dbre_ref, dbim_ref, dcre_ref, dcim_ref, dlam_ref, dd_ref, sr_s, si_s, gr_s, gi_s):
        _lb_store(sr_s, slice(0, SOFF), jnp.zeros((SOFF, QS), F32))
        _lb_store(si_s, slice(0, SOFF), jnp.zeros((SOFF, QS), F32))
        for k in range(NCH):
            sl = slice(k * tc, (k + 1) * tc)
            ss = slice(SOFF + k * tc, SOFF + (k + 1) * tc)
            uk = u_ref[sl, :]
            dyk = dy_ref[sl, :].astype(BF16)
            _lb_store(sr_s, ss, _dot(uk, bre_ref[...]))
            _lb_store(si_s, ss, _dot(uk, bim_ref[...]))
            _lb_store(gr_s, sl, _dot_nt(dyk, cre_ref[...]))
            _lb_store(gi_s, sl, -_dot_nt(dyk, cim_ref[...]))
        ar, ai, atr, ati = lam_ref[0:1, :], lam_ref[1:2, :], lam_ref[2:3, :], lam_ref[3:4, :]
        _scan(sr_s, si_s, SOFF, T, ar, ai, atr, ati, False)
        _scan(gr_s, gi_s, 0, T, ar, -ai, atr, -ati, True)
        dbre = jnp.zeros((QU, QS), F32)
        dbim = jnp.zeros((QU, QS), F32)
        dcre = jnp.zeros((QS, QU), F32)
        dcim = jnp.zeros((QS, QU), F32)
        dd = jnp.zeros((1, QU), F32)
        qr = jnp.zeros((1, QS), F32)
        qi = jnp.zeros((1, QS), F32)
        for k in range(NCH):
            sl = slice(k * tc, (k + 1) * tc)
            ss = slice(SOFF + k * tc, SOFF + (k + 1) * tc)
            sp = slice(SOFF - 1 + k * tc, SOFF - 1 + (k + 1) * tc)
            uk = u_ref[sl, :]
            dyk = dy_ref[sl, :]
            dyb = dyk.astype(BF16)
            gr, gi = _lb_load(gr_s, sl), _lb_load(gi_s, sl)
            pr, pi = _lb_load(sr_s, sp), _lb_load(si_s, sp)
            qr = qr + jnp.sum(gr * pr + gi * pi, axis=0, keepdims=True)
            qi = qi + jnp.sum(gi * pr - gr * pi, axis=0, keepdims=True)
            grb, gib = gr.astype(BF16), gi.astype(BF16)
            du_ref[sl, :] = (_dot_nt(grb, bre_ref[...]) + _dot_nt(gib, bim_ref[...])
                             + dyk * d_ref[...]).astype(BF16)
            dbre = dbre + _dot_tn(uk, grb)
            dbim = dbim + _dot_tn(uk, gib)
            dcre = dcre + _dot_tn(_lb_load(sr_s, ss).astype(BF16), dyb)
            dcim = dcim - _dot_tn(_lb_load(si_s, ss).astype(BF16), dyb)
            dd = dd + jnp.sum(dyk * uk.astype(F32), axis=0, keepdims=True)
        dlam_ref[0] = jnp.concatenate([qr, qi, jnp.zeros((6, QS), F32)], axis=0)
        dbre_ref[0] = dbre
        dbim_ref[0] = dbim
        dcre_ref[0] = dcre
        dcim_ref[0] = dcim
        dd_ref[...] = dd

    col = pl.BlockSpec((L, QU), lambda q: (0, q))
    bsp = pl.BlockSpec((QU, QS), lambda q: (q, q))
    csp = pl.BlockSpec((QS, QU), lambda q: (q, q))
    return pl.pallas_call(
        body, name=name, grid=(NQ,),
        out_shape=[jax.ShapeDtypeStruct((L, DS), BF16),
                   jax.ShapeDtypeStruct((NQ, QU, QS), F32), jax.ShapeDtypeStruct((NQ, QU, QS), F32),
                   jax.ShapeDtypeStruct((NQ, QS, QU), F32), jax.ShapeDtypeStruct((NQ, QS, QU), F32),
                   jax.ShapeDtypeStruct((NQ, 8, QS), F32), jax.ShapeDtypeStruct((1, DS), F32)],
        in_specs=[col, col, bsp, bsp, csp, csp,
                  pl.BlockSpec((8, QS), lambda q: (0, q)), pl.BlockSpec((1, QU), lambda q: (0, q))],
        out_specs=[col,
                   pl.BlockSpec((1, QU, QS), lambda q: (q, 0, 0)), pl.BlockSpec((1, QU, QS), lambda q: (q, 0, 0)),
                   pl.BlockSpec((1, QS, QU), lambda q: (q, 0, 0)), pl.BlockSpec((1, QS, QU), lambda q: (q, 0, 0)),
                   pl.BlockSpec((1, 8, QS), lambda q: (q, 0, 0)), pl.BlockSpec((1, QU), lambda q: (0, q))],
        scratch_shapes=[pltpu.VMEM((NLB, L + SOFF, 128), F32), pltpu.VMEM((NLB, L + SOFF, 128), F32),
                        pltpu.VMEM((NLB, L, 128), F32), pltpu.VMEM((NLB, L, 128), F32)],
        compiler_params=_params("parallel"),
    )(uf, dyss, bre, bim, cre, cim, lamp, dsk)


def _branches(z1_ref, yss_ref, gt_ref, lng_ref, lnb_ref, wp_ref, wv_ref, wg_ref):
    zf = z1_ref[...]
    mu = jnp.mean(zf, axis=-1, keepdims=True)
    zc = zf - mu
    rstd = lax.rsqrt(jnp.mean(zc * zc, axis=-1, keepdims=True) + EPS)
    zn = zc * rstd
    z2 = zn * lng_ref[...] + lnb_ref[...]
    sz = _sigmoid(z2)
    z3 = (z2 * sz).astype(BF16)
    y_conv = _dot(z3, wp_ref[...])
    yss = yss_ref[...]
    yg = _gelu(yss).astype(BF16)
    sv = _dot(yg, wv_ref[...])
    sig = _sigmoid(_dot(yg, wg_ref[...]))
    y_ssm = sv * sig
    gc = gt_ref[:, 0:D].astype(F32)
    gs = gt_ref[:, D:2 * D].astype(F32)
    m = gc * y_conv + gs * y_ssm
    return dict(rstd=rstd, zn=zn, z2=z2, sz=sz, z3=z3, y_conv=y_conv, yss=yss, yg=yg, sv=sv, sig=sig,
                y_ssm=y_ssm, gc=gc, gs=gs, m=m)


def _merge_fwd(h, z1, yss, gate, lng, lnb, wp, wv, wg, wo, name):
    L = h.shape[0]
    tm = _tile(L, 528)

    def body(h_ref, z1_ref, yss_ref, gt_ref, lng_ref, lnb_ref, wp_ref, wv_ref, wg_ref, wo_ref, o_ref):
        f = _branches(z1_ref, yss_ref, gt_ref, lng_ref, lnb_ref, wp_ref, wv_ref, wg_ref)
        o_ref[...] = h_ref[...] + _dot(f["m"].astype(BF16), wo_ref[...])

    def row(n):
        return pl.BlockSpec((tm, n), lambda i: (i, 0))

    return pl.pallas_call(
        body, name=name, grid=(L // tm,),
        out_shape=jax.ShapeDtypeStruct((L, D), F32),
        in_specs=[row(D), row(DC), row(DS), row(2 * D), _res((1, DC)), _res((1, DC)),
                  _res((DC, D)), _res((DS, D)), _res((DS, D)), _res((D, D))],
        out_specs=row(D),
        compiler_params=_params("parallel"),
    )(h, z1, yss, gate, lng, lnb, wp, wv, wg, wo)


def _merge_bwd(dh, z1, yss, gate, lng, lnb, wp, wv, wg, wo, name):
    L = dh.shape[0]
    tm = _tile(L, 352)

    def body(dh_ref, z1_ref, yss_ref, gt_ref, lng_ref, lnb_ref, wp_ref, wv_ref, wg_ref, wo_ref,
             m_ref, dgt_ref, dyc_ref, z3_ref, dz1_ref, yg_ref, dsv_ref, dsg_ref, dyss_ref,
             dbg_ref, dlng_ref, dlnb_ref):
        i = pl.program_id(0)
        f = _branches(z1_ref, yss_ref, gt_ref, lng_ref, lnb_ref, wp_ref, wv_ref, wg_ref)
        gc, gs, sig, sv = f["gc"], f["gs"], f["sig"], f["sv"]
        m_ref[...] = f["m"].astype(BF16)
        z3_ref[...] = f["z3"]
        yg_ref[...] = f["yg"]
        dm = _dot_nt(dh_ref[...].astype(BF16), wo_ref[...])
        dgc = (dm * f["y_conv"] * gc * (1.0 - gc)).astype(BF16)
        dgs = (dm * f["y_ssm"] * gs * (1.0 - gs)).astype(BF16)
        dgt_ref[:, 0:D] = dgc
        dgt_ref[:, D:2 * D] = dgs
        part = jnp.concatenate([jnp.sum(dgc.astype(F32), axis=0, keepdims=True),
                                jnp.sum(dgs.astype(F32), axis=0, keepdims=True)], axis=1)
        _acc_rows(dbg_ref, part, i == 0)
        dyc = (dm * gc).astype(BF16)
        dyc_ref[...] = dyc
        dys = dm * gs
        dsv = (dys * sig).astype(BF16)
        dsg = (dys * sv * sig * (1.0 - sig)).astype(BF16)
        dsv_ref[...] = dsv
        dsg_ref[...] = dsg
        dyg = _dot_nt(dsv, wv_ref[...]) + _dot_nt(dsg, wg_ref[...])
        dyss_ref[...] = dyg * _gelu_grad(f["yss"])
        dz3 = _dot_nt(dyc, wp_ref[...])
        z2, sz, zn = f["z2"], f["sz"], f["zn"]
        dz2 = dz3 * sz * (1.0 + z2 * (1.0 - sz))
        _acc_rows(dlng_ref, jnp.sum(dz2 * zn, axis=0, keepdims=True), i == 0)
        _acc_rows(dlnb_ref, jnp.sum(dz2, axis=0, keepdims=True), i == 0)
        dzn = dz2 * lng_ref[...]
        dz1_ref[...] = f["rstd"] * (dzn - jnp.mean(dzn, axis=-1, keepdims=True)
                                    - zn * jnp.mean(dzn * zn, axis=-1, keepdims=True))

    def row(n):
        return pl.BlockSpec((tm, n), lambda i: (i, 0))

    def tot(n):
        return pl.BlockSpec((1, n), lambda i: (0, 0))

    return pl.pallas_call(
        body, name=name, grid=(L // tm,),
        out_shape=[jax.ShapeDtypeStruct((L, D), BF16), jax.ShapeDtypeStruct((L, 2 * D), BF16),
                   jax.ShapeDtypeStruct((L, D), BF16), jax.ShapeDtypeStruct((L, DC), BF16),
                   jax.ShapeDtypeStruct((L, DC), F32), jax.ShapeDtypeStruct((L, DS), BF16),
                   jax.ShapeDtypeStruct((L, D), BF16), jax.ShapeDtypeStruct((L, D), BF16),
                   jax.ShapeDtypeStruct((L, DS), F32),
                   jax.ShapeDtypeStruct((1, 2 * D), F32), jax.ShapeDtypeStruct((1, DC), F32),
                   jax.ShapeDtypeStruct((1, DC), F32)],
        in_specs=[row(D), row(DC), row(DS), row(2 * D), _res((1, DC)), _res((1, DC)),
                  _res((DC, D)), _res((DS, D)), _res((DS, D)), _res((D, D))],
        out_specs=[row(D), row(2 * D), row(D), row(DC), row(DC), row(DS), row(D), row(D), row(DS),
                   tot(2 * D), tot(DC), tot(DC)],
        compiler_params=_params("arbitrary"),
    )(dh, z1, yss, gate, lng, lnb, wp, wv, wg, wo)


def _final(h, g, tgt, name):
    L = h.shape[0]
    tm = _tile(L, 528)

    def body(h_ref, g_ref, t_ref, dh_ref, loss_ref, dg_ref):
        i = pl.program_id(0)
        hv = h_ref[...]
        gv = g_ref[...]
        y, r = _rms(hv, gv)
        row = i * tm + lax.broadcasted_iota(jnp.int32, (tm, 1), 0)
        e = jnp.where(row >= FRONT, y - t_ref[...], 0.0)
        dy = e * (1.0 / D)
        part = 0.5 * jnp.sum(jnp.sum(e * dy, axis=1, keepdims=True), axis=0, keepdims=True)
        dx, xh = _rms_bwd(dy, hv, r, gv)
        dh_ref[...] = dx
        _acc_rows(loss_ref, part, i == 0)
        _acc_rows(dg_ref, jnp.sum(dy * xh, axis=0, keepdims=True), i == 0)

    row = pl.BlockSpec((tm, D), lambda i: (i, 0))
    return pl.pallas_call(
        body, name=name, grid=(L // tm,),
        out_shape=[jax.ShapeDtypeStruct((L, D), F32), jax.ShapeDtypeStruct((1, 1), F32),
                   jax.ShapeDtypeStruct((1, D), F32)],
        in_specs=[row, _res((1, D)), row],
        out_specs=[row, pl.BlockSpec((1, 1), lambda i: (0, 0)), pl.BlockSpec((1, D), lambda i: (0, 0))],
        compiler_params=_params("arbitrary"),
    )(h, g, tgt)


def _ssm_disc(lam_re, lam_im, log_dt, b_re, b_im):
    lam = lax.complex(lam_re, lam_im)
    dt = jnp.exp(log_dt)[:, None]
    lam_bar = jnp.exp(lam * dt)
    bbar = ((lam_bar - 1.0) / lam)[..., None] * lax.complex(b_re, b_im)
    return jnp.real(lam_bar), jnp.imag(lam_bar), jnp.real(bbar), jnp.imag(bbar)


def _bdiag_in(m):
    return jnp.einsum("gph,gk->ghkp", m, jnp.eye(G, dtype=m.dtype)).reshape(G * H, G * P)


def _bdiag_out(m):
    return jnp.einsum("ghp,gk->gpkh", m, jnp.eye(G, dtype=m.dtype)).reshape(G * P, G * H)


def _diag_blocks(m4):
    return jnp.einsum("qiaib->qiab", m4).reshape(G, m4.shape[2], m4.shape[4])


def _pack(parts, rows_mult=8):
    flat = jnp.concatenate([p.reshape(-1).astype(F32) for p in parts])
    n = flat.shape[0]
    tot = -(-n // (128 * rows_mult)) * (128 * rows_mult)
    return jnp.pad(flat, (0, tot - n)).reshape(tot // 128, 128)


def _unpack(buf, shapes):
    flat = buf.reshape(-1)
    out, o = [], 0
    for s in shapes:
        n = math.prod(s)
        out.append(flat[o:o + n].reshape(s))
        o += n
    return out


def kernel(x, meta_tokens, ffn1_norm, ffn1_w1, ffn1_w3, ffn1_w2, mix_norm, w_in, b_gate, conv_dw, conv_dw_b, conv_ln_g, conv_ln_b, conv_proj, ssm_lam_re, ssm_lam_im, ssm_log_dt, ssm_b_re, ssm_b_im, ssm_c_re, ssm_c_im, ssm_d, ssm_w_v, ssm_w_g, w_out, ffn2_norm, ffn2_w1, ffn2_w3, ffn2_w2, final_norm, loss_target, m_meta_tokens, m_ffn1_norm, m_ffn1_w1, m_ffn1_w3, m_ffn1_w2, m_mix_norm, m_w_in, m_b_gate, m_conv_dw, m_conv_dw_b, m_conv_ln_g, m_conv_ln_b, m_conv_proj, m_ssm_lam_re, m_ssm_lam_im, m_ssm_log_dt, m_ssm_b_re, m_ssm_b_im, m_ssm_c_re, m_ssm_c_im, m_ssm_d, m_ssm_w_v, m_ssm_w_g, m_w_out, m_ffn2_norm, m_ffn2_w1, m_ffn2_w3, m_ffn2_w2, m_final_norm, v_meta_tokens, v_ffn1_norm, v_ffn1_w1, v_ffn1_w3, v_ffn1_w2, v_mix_norm, v_w_in, v_b_gate, v_conv_dw, v_conv_dw_b, v_conv_ln_g, v_conv_ln_b, v_conv_proj, v_ssm_lam_re, v_ssm_lam_im, v_ssm_log_dt, v_ssm_b_re, v_ssm_b_im, v_ssm_c_re, v_ssm_c_im, v_ssm_d, v_ssm_w_v, v_ssm_w_g, v_w_out, v_ffn2_norm, v_ffn2_w1, v_ffn2_w3, v_ffn2_w2, v_final_norm):
    args = dict(locals())
    names = ["meta_tokens", "ffn1_norm", "ffn1_w1", "ffn1_w3", "ffn1_w2", "mix_norm", "w_in", "b_gate",
             "conv_dw", "conv_dw_b", "conv_ln_g", "conv_ln_b", "conv_proj", "ssm_lam_re", "ssm_lam_im",
             "ssm_log_dt", "ssm_b_re", "ssm_b_im", "ssm_c_re", "ssm_c_im", "ssm_d", "ssm_w_v", "ssm_w_g",
             "w_out", "ffn2_norm", "ffn2_w1", "ffn2_w3", "ffn2_w2", "final_norm"]
    big = ["ffn1_w1", "ffn1_w3", "ffn1_w2", "w_in", "conv_proj", "ssm_w_v", "ssm_w_g", "w_out",
           "ffn2_w1", "ffn2_w3", "ffn2_w2"]
    small = [n for n in names if n not in big]

    xs = x[0]
    S = xs.shape[0]
    L = FRONT + S
    T = L // NSEG
    jx, jy = lax.axis_index("x"), lax.axis_index("y")
    chip = 2 * jx + jy

    sm = _gather_all(_pack([meta_tokens, conv_dw[0]]), "gather_small")[0::2].reshape(NSH, -1)
    nmt = NMETA * (D // NSH)
    ndw = KW * (DC // NSH)
    meta_full = sm[:, :nmt].reshape(NSH, NMETA, D // NSH).transpose(1, 0, 2).reshape(NMETA, D)
    dw_full = sm[:, nmt:nmt + ndw].reshape(NSH, KW, DC // NSH).transpose(1, 0, 2).reshape(KW, DC)
    dw_pad = jnp.pad(dw_full, ((0, KWP - KW), (0, 0)))
    tposed = ("ffn1_w1", "ffn1_w3", "ffn2_w1", "ffn2_w3")
    gw = _gather_chips([(args[n][0].T if n in tposed else args[n][0]).astype(BF16) for n in big],
                       "gather_weights")
    gw = dict(zip(big, gw))

    def cols(w):
        return w.transpose(1, 0, 2).reshape(w.shape[1], -1)

    w_in_f = gw["w_in"]
    wp_f, wv_f, wg_f = cols(gw["conv_proj"]), cols(gw["ssm_w_v"]), cols(gw["ssm_w_g"])
    wo_f = gw["w_out"].reshape(D, D)

    disc_in = (ssm_lam_re[0], ssm_lam_im[0], ssm_log_dt[0], ssm_b_re[0], ssm_b_im[0])
    (lbr, lbi, bbr, bbi), disc_vjp = jax.vjp(_ssm_disc, *disc_in)
    lam_t = jnp.exp(lax.complex(ssm_lam_re[0], ssm_lam_im[0]) * (jnp.exp(ssm_log_dt[0])[:, None] * T))
    lamp = jnp.concatenate([lbr.reshape(1, NST), lbi.reshape(1, NST), jnp.real(lam_t).reshape(1, NST),
                            jnp.imag(lam_t).reshape(1, NST), jnp.zeros((4, NST), F32)], axis=0)
    bre_bd, bim_bd = _bdiag_in(bbr).astype(BF16), _bdiag_in(bbi).astype(BF16)
    cre_bd, cim_bd = _bdiag_out(ssm_c_re[0]).astype(BF16), _bdiag_out(ssm_c_im[0]).astype(BF16)

    h0 = jnp.concatenate([jnp.zeros((FRONT - NMETA, D), F32), meta_full, xs], axis=0)
    tgt = jnp.pad(loss_target[0], ((FRONT, 0), (0, 0)))
    h1, a1, b1 = _ffn_fwd(h0, ffn1_norm, gw["ffn1_w1"], gw["ffn1_w3"], gw["ffn1_w2"], "ffn1_fwd")
    vg, uf, gate = _mix_in_fwd(h1, mix_norm, w_in_f, b_gate, "mix_in_fwd")
    z1 = _conv_fwd(vg, dw_pad, conv_dw_b, "conv_fwd")
    yss = _ssm_fwd(uf, bre_bd, bim_bd, cre_bd, cim_bd, lamp, ssm_d, "ssm_fwd")
    h2 = _merge_fwd(h1, z1, yss, gate, conv_ln_g, conv_ln_b, wp_f, wv_f, wg_f, wo_f, "merge_fwd")
    h3, a2, b2 = _ffn_fwd(h2, ffn2_norm, gw["ffn2_w1"], gw["ffn2_w3"], gw["ffn2_w2"], "ffn2_fwd")

    dh3, loss_part, d_final = _final(h3, final_norm.reshape(1, D), tgt, "final")
    dh2, da2, db2, s2, n2, d_ffn2_norm = _ffn_bwd(
        h2, ffn2_norm, dh3, a2, b2, gw["ffn2_w1"], gw["ffn2_w3"], gw["ffn2_w2"], "ffn2_bwd")
    gbig = {}
    gbig["ffn2_w1"] = _wgrad(da2, n2, "ffn2_dw1")
    gbig["ffn2_w3"] = _wgrad(db2, n2, "ffn2_dw3")
    gbig["ffn2_w2"] = _wgrad(s2, dh3, "ffn2_dw2", 0.5)
    (m_b, dgate, dyc, z3, dz1, yg, dsv, dsg, dyss, d_b_gate, d_ln_g, d_ln_b) = _merge_bwd(
        dh2, z1, yss, gate, conv_ln_g, conv_ln_b, wp_f, wv_f, wg_f, wo_f, "merge_bwd")
    gbig["w_out"] = _wgrad(m_b, dh2, "dw_out").reshape(NSH, D // NSH, D)

    def shard_cols(gm):
        return gm.reshape(gm.shape[0], NSH, -1).transpose(1, 0, 2)

    gbig["conv_proj"] = shard_cols(_wgrad(z3, dyc, "dw_proj"))
    gbig["ssm_w_v"] = shard_cols(_wgrad(yg, dsv, "dw_v"))
    gbig["ssm_w_g"] = shard_cols(_wgrad(yg, dsg, "dw_g"))
    dv, dgl, ddw, d_dw_b = _conv_bwd(dz1, vg, dw_pad, "conv_bwd")
    duf, dbre, dbim, dcre, dcim, dlam, d_ssm_d = _ssm_bwd(
        uf, dyss, bre_bd, bim_bd, cre_bd, cim_bd, lamp, ssm_d, "ssm_bwd")
    dh1, u_b, dproj, d_mix_norm = _mix_in_bwd(h1, mix_norm, dh2, dv, dgl, duf, dgate, w_in_f, "mix_in_bwd")
    gbig["w_in"] = _wgrad(u_b, dproj, "dw_in")
    dh0, da1, db1, s1, n1, d_ffn1_norm = _ffn_bwd(
        h0, ffn1_norm, dh1, a1, b1, gw["ffn1_w1"], gw["ffn1_w3"], gw["ffn1_w2"], "ffn1_bwd")
    gbig["ffn1_w1"] = _wgrad(da1, n1, "ffn1_dw1")
    gbig["ffn1_w3"] = _wgrad(db1, n1, "ffn1_dw3")
    gbig["ffn1_w2"] = _wgrad(s1, dh1, "ffn1_dw2", 0.5)
    grad_x = dh0[FRONT:][None]

    d_bbr = _diag_blocks(dbre.reshape(NQ, 8, H, 8, P)).transpose(0, 2, 1)
    d_bbi = _diag_blocks(dbim.reshape(NQ, 8, H, 8, P)).transpose(0, 2, 1)
    d_c_re = _diag_blocks(dcre.reshape(NQ, 8, P, 8, H)).transpose(0, 2, 1)
    d_c_im = _diag_blocks(dcim.reshape(NQ, 8, P, 8, H)).transpose(0, 2, 1)
    d_lbr = dlam[:, 0, :].reshape(G, P)
    d_lbi = dlam[:, 1, :].reshape(G, P)
    d_lam_re, d_lam_im, d_log_dt, d_b_re, d_b_im = disc_vjp((d_lbr, d_lbi, d_bbr, d_bbi))

    glist = [gbig[n] for n in big]
    core = lax.axis_index("c")
    sib = _pair_exchange(glist, "pair_exchange")
    pair = []
    for n, g_, s_ in zip(big, glist, sib):
        half = g_.shape[1] // 2
        mine = lax.dynamic_slice_in_dim(g_, core * half, half, axis=1)
        pair.append(_add_pair(mine.reshape(NSH * half, -1), s_.reshape(NSH * half, -1),
                              "pair_" + n).reshape(s_.shape))
    recv = _scatter_chips(pair, "scatter_grads")
    halves = [_sum_slots(r, "sum_" + n) for n, r in zip(big, recv)]
    full = _swap_halves(halves, "swap_halves")
    out_g, out_d, out_m, out_v = {}, {}, {}, {}
    for n, f in zip(big, full):
        shp = args[n].shape
        g2 = f.reshape(f.shape[0] * f.shape[1], f.shape[2])
        if n in tposed:
            g2 = g2.T
        d2, m2, v2 = _adamw(args[n].reshape(g2.shape), g2, args["m_" + n].reshape(g2.shape),
                            args["v_" + n].reshape(g2.shape), "adamw_" + n)
        out_g[n], out_d[n], out_m[n], out_v[n] = (t.reshape(shp) for t in (g2, d2, m2, v2))

    sg = {"meta_tokens": dh0[FRONT - NMETA:FRONT], "ffn1_norm": d_ffn1_norm, "mix_norm": d_mix_norm,
          "b_gate": d_b_gate, "conv_dw": ddw[:KW], "conv_dw_b": d_dw_b, "conv_ln_g": d_ln_g, "conv_ln_b": d_ln_b,
          "ssm_lam_re": d_lam_re, "ssm_lam_im": d_lam_im, "ssm_log_dt": d_log_dt, "ssm_b_re": d_b_re,
          "ssm_b_im": d_b_im, "ssm_c_re": d_c_re, "ssm_c_im": d_c_im, "ssm_d": d_ssm_d,
          "ffn2_norm": d_ffn2_norm, "final_norm": d_final}
    sshapes = [sg[n].shape for n in small]
    tot = _sum_slots(_gather_all(_pack([sg[n] for n in small]), "gather_small_grads"), "sum_small")
    sgr = dict(zip(small, _unpack(tot, sshapes)))
    sgr["meta_tokens"] = lax.dynamic_slice_in_dim(sgr["meta_tokens"], chip * (D // NSH), D // NSH, axis=1)
    sgr["conv_dw"] = lax.dynamic_slice_in_dim(sgr["conv_dw"], chip * (DC // NSH), DC // NSH, axis=1)
    pshapes = [args[n].shape for n in small]
    d_s, m_s, v_s = _adamw(_pack([args[n] for n in small]), _pack([sgr[n] for n in small]),
                           _pack([args["m_" + n] for n in small]), _pack([args["v_" + n] for n in small]),
                           "adamw_small")
    for n, g_, d_, m_, v_ in zip(small, [sgr[n] for n in small], _unpack(d_s, pshapes),
                                 _unpack(m_s, pshapes), _unpack(v_s, pshapes)):
        out_g[n], out_d[n], out_m[n], out_v[n] = g_.reshape(args[n].shape), d_, m_, v_

    loss = lax.psum(loss_part[0, 0], ("x", "y", "c"))
    return (loss, grad_x, *[out_g[n] for n in names], *[out_d[n] for n in names],
            *[out_m[n] for n in names], *[out_v[n] for n in names])
```

```python
import math

import jax
import jax.numpy as jnp
from jax import lax
from jax.experimental import pallas as pl
from jax.experimental.pallas import tpu as pltpu

F32 = jnp.float32
BF16 = jnp.bfloat16

D = 1024
NSH = 4
F = 2816
FS = F // NSH
DC = 512
DS = 512
DIN = 2 * DC + DS + 2 * D
WS = DIN // NSH
KW = 31
KWP = 32
NMETA = 16
FRONT = 128
G, P, H = 32, 64, 16
NST = G * P
NQ = 4
QS = NST // NQ
QU = DS // NQ
NSEG = 32
NGRP = NSEG // 8
NCH = 8
SOFF = 8
EPS = 1e-6
LR, B1, B2, AEPS, WD, STEP = 1e-3, 0.9, 0.999, 1e-8, 0.01, 10
VMEM_LIMIT = 58 * 1024 * 1024
MESH = pl.DeviceIdType.MESH
ANY = pl.BlockSpec(memory_space=pl.ANY)


def _params(*sem):
    return pltpu.CompilerParams(dimension_semantics=sem, vmem_limit_bytes=VMEM_LIMIT)


def _res(shape):
    nd = len(shape)
    return pl.BlockSpec(shape, lambda *_: (0,) * nd, pipeline_mode=pl.Buffered(1))


def _tile(n, cap, mult=16):
    best = None
    for t in range(mult, min(n, cap) + 1, mult):
        if n % t == 0:
            best = t
    assert best is not None, (n, cap, mult)
    return best


def _dot(a, b):
    return jnp.dot(a, b, preferred_element_type=F32)


def _dot_nt(a, b):
    return lax.dot_general(a, b, (((1,), (1,)), ((), ())), preferred_element_type=F32)


def _dot_tn(a, b):
    return lax.dot_general(a, b, (((0,), (0,)), ((), ())), preferred_element_type=F32)


def _sigmoid(x):
    return 1.0 / (1.0 + jnp.exp(-x))


_GC = math.sqrt(2.0 / math.pi)
_GA = 0.044715


def _gelu(x):
    return 0.5 * x * (1.0 + jnp.tanh(_GC * (x + _GA * x * x * x)))


def _gelu_grad(x):
    t = jnp.tanh(_GC * (x + _GA * x * x * x))
    return 0.5 * (1.0 + t) + 0.5 * x * (1.0 - t * t) * _GC * (1.0 + 3.0 * _GA * x * x)


def _rms(hv, g):
    r = lax.rsqrt(jnp.mean(hv * hv, axis=-1, keepdims=True) + EPS)
    return hv * r * g, r


def _rms_bwd(dn, hv, r, g):
    xh = hv * r
    dxh = dn * g
    return r * (dxh - xh * jnp.mean(dxh * xh, axis=-1, keepdims=True)), xh


def _acc_rows(ref, part, first):
    @pl.when(first)
    def _():
        ref[...] = part

    @pl.when(jnp.logical_not(first))
    def _():
        ref[...] += part


def _coords():
    return lax.axis_index("x"), lax.axis_index("y"), lax.axis_index("c")


def _flip(v, d):
    return 1 - v if d else v


def _run(local, remote):
    for cp in local + remote:
        cp.start()
    for cp in remote:
        cp.wait()
    for cp in local:
        cp.wait()


def _via_vmem(src, dst, stage, sems, i):
    return (pltpu.make_async_copy(src, stage, sems.at[2 * i]), pltpu.make_async_copy(stage, dst, sems.at[2 * i + 1]))


def _run_staged(staged, remote):
    for load, _ in staged:
        load.start()
    for cp in remote:
        cp.start()
    for load, store in staged:
        load.wait()
        store.start()
    for cp in remote:
        cp.wait()
    for _, store in staged:
        store.wait()


_REL3 = ((1, 0), (0, 1), (1, 1))


class _Behind:
    def __init__(self, arrays, out_shapes, scratch, build):
        self.arrays, self.out_shapes, self.scratch, self.build = list(arrays), list(out_shapes), list(scratch), build

    def start(self, ins, outs, scr):
        staged, remote = self.build(ins, outs, scr)
        for load, _ in staged:
            load.start()
        for cp in remote:
            cp.start()

    def finish(self, ins, outs, scr):
        staged, remote = self.build(ins, outs, scr)
        for load, store in staged:
            load.wait()
            store.start()
        for cp in remote:
            cp.wait()
        for _, store in staged:
            store.wait()


def _call(body, comm, *, name, grid, in_specs, out_specs, out_shape, scratch_shapes=(), params):
    in_specs, out_specs, out_shape = list(in_specs), list(out_specs), list(out_shape)
    scratch_shapes = list(scratch_shapes)
    if comm is None:
        f = pl.pallas_call(body, name=name, grid=grid, in_specs=in_specs, out_specs=out_specs,
                           out_shape=out_shape, scratch_shapes=scratch_shapes, compiler_params=params)
        return lambda *args: (f(*args), [])
    ni, no, ns = len(in_specs), len(out_specs), len(scratch_shapes)
    ci, co = len(comm.arrays), len(comm.out_shapes)

    def hosted(*refs):
        ins, cin = refs[:ni], refs[ni:ni + ci]
        outs, cout = refs[ni + ci:ni + ci + no], refs[ni + ci + no:ni + ci + no + co]
        scr, cscr = refs[ni + ci + no + co:ni + ci + no + co + ns], refs[ni + ci + no + co + ns:]
        first = last = None
        for axis, size in enumerate(grid):
            i = pl.program_id(axis)
            first = (i == 0) if first is None else jnp.logical_and(first, i == 0)
            last = (i == size - 1) if last is None else jnp.logical_and(last, i == size - 1)

        @pl.when(first)
        def _():
            comm.start(cin, cout, cscr)

        body(*ins, *outs, *scr)

        @pl.when(last)
        def _():
            comm.finish(cin, cout, cscr)

    f = pl.pallas_call(hosted, name=name, grid=grid, in_specs=in_specs + [ANY] * ci,
                       out_specs=out_specs + [ANY] * co, out_shape=out_shape + comm.out_shapes,
                       scratch_shapes=scratch_shapes + comm.scratch,
                       compiler_params=_params(*(("arbitrary",) * len(grid))))

    def run(*args):
        res = f(*args, *comm.arrays)
        return res[:no], res[no:]

    return run


def _gather_half_behind(shards):
    n = len(shards)

    def build(ins, outs, scr):
        send, recv, loc = scr[:3]
        stage = scr[3:]
        x, y, c = _coords()
        me = 2 * x + y
        staged = [_via_vmem(ins[t], outs[t].at[me], stage[t], loc, t) for t in range(n)]
        remote = []
        for t in range(n):
            half = shards[t].shape[0] // 2
            mine = pl.ds(c * half, half)
            for k, (dx, dy) in enumerate(_REL3):
                remote.append(pltpu.make_async_remote_copy(
                    src_ref=ins[t].at[mine], dst_ref=outs[t].at[me, mine],
                    send_sem=send.at[3 * t + k], recv_sem=recv.at[3 * t + k],
                    device_id=(_flip(x, dx), _flip(y, dy), c), device_id_type=MESH))
        return staged, remote

    return _Behind(shards, [jax.ShapeDtypeStruct((NSH,) + s.shape, s.dtype) for s in shards],
                   [pltpu.SemaphoreType.DMA((3 * n,)), pltpu.SemaphoreType.DMA((3 * n,)),
                    pltpu.SemaphoreType.DMA((2 * n,))] + [pltpu.VMEM(s.shape, s.dtype) for s in shards], build)


def _pass_halves(gathered, name):
    n = len(gathered)

    def body(*refs):
        outs = refs[n:2 * n]
        send, recv = refs[2 * n:]
        x, y, c = _coords()
        remote = []
        for t in range(n):
            half = gathered[t].shape[1] // 2
            mine = pl.ds(c * half, half)
            for k, (dx, dy) in enumerate(_REL3):
                slot = 2 * _flip(x, dx) + _flip(y, dy)
                remote.append(pltpu.make_async_remote_copy(
                    src_ref=outs[t].at[slot, mine], dst_ref=outs[t].at[slot, mine],
                    send_sem=send.at[3 * t + k], recv_sem=recv.at[3 * t + k],
                    device_id=(x, y, 1 - c), device_id_type=MESH))
        _run([], remote)

    return pl.pallas_call(
        body, name=name,
        out_shape=[jax.ShapeDtypeStruct(g.shape, g.dtype) for g in gathered],
        in_specs=[ANY] * n, out_specs=[ANY] * n, input_output_aliases={t: t for t in range(n)},
        scratch_shapes=[pltpu.SemaphoreType.DMA((3 * n,)), pltpu.SemaphoreType.DMA((3 * n,))],
    )(*gathered)


def _scatter_chips_behind(sums):
    n = len(sums)

    def build(ins, outs, scr):
        send, recv, loc = scr[:3]
        stage = scr[3:]
        x, y, c = _coords()
        me = 2 * x + y
        staged = [_via_vmem(ins[t].at[me], outs[t].at[me], stage[t], loc, t) for t in range(n)]
        remote = []
        for t in range(n):
            for k, (dx, dy) in enumerate(_REL3):
                px, py = _flip(x, dx), _flip(y, dy)
                remote.append(pltpu.make_async_remote_copy(
                    src_ref=ins[t].at[2 * px + py], dst_ref=outs[t].at[me],
                    send_sem=send.at[3 * t + k], recv_sem=recv.at[3 * t + k],
                    device_id=(px, py, c), device_id_type=MESH))
        return staged, remote

    return _Behind(sums, [jax.ShapeDtypeStruct(s.shape, s.dtype) for s in sums],
                   [pltpu.SemaphoreType.DMA((3 * n,)), pltpu.SemaphoreType.DMA((3 * n,)),
                    pltpu.SemaphoreType.DMA((2 * n,))] + [pltpu.VMEM(s.shape[1:], s.dtype) for s in sums], build)


def _gather_all_behind(a):
    def build(ins, outs, scr):
        send, recv, loc, stage = scr
        x, y, c = _coords()
        me = 4 * x + 2 * y + c
        staged = [_via_vmem(ins[0], outs[0].at[me], stage, loc, 0)]
        remote = [pltpu.make_async_remote_copy(
            src_ref=ins[0], dst_ref=outs[0].at[me], send_sem=send.at[k], recv_sem=recv.at[k],
            device_id=(_flip(x, dx), _flip(y, dy), _flip(c, dc)), device_id_type=MESH)
            for k, (dx, dy, dc) in enumerate(_REL7)]
        return staged, remote

    return _Behind([a], [jax.ShapeDtypeStruct((8,) + a.shape, a.dtype)],
                   [pltpu.SemaphoreType.DMA((7,)), pltpu.SemaphoreType.DMA((7,)), pltpu.SemaphoreType.DMA((2,)),
                    pltpu.VMEM(a.shape, a.dtype)], build)


def _join(*parts):
    def cut(seq, key):
        res, o = [], 0
        for p in parts:
            k = len(getattr(p, key))
            res.append(seq[o:o + k])
            o += k
        return res

    def build(ins, outs, scr):
        staged, remote = [], []
        for p, i, o, s in zip(parts, cut(ins, "arrays"), cut(outs, "out_shapes"), cut(scr, "scratch")):
            st, rm = p.build(i, o, s)
            staged += st
            remote += rm
        return staged, remote

    return _Behind(sum((p.arrays for p in parts), []), sum((p.out_shapes for p in parts), []),
                   sum((p.scratch for p in parts), []), build)


def _gather_chips(shards, name):
    n = len(shards)

    def body(*refs):
        ins, outs = refs[:n], refs[n:2 * n]
        send, recv, fsend, frecv, loc = refs[2 * n:2 * n + 5]
        stage = refs[2 * n + 5:]
        x, y, c = _coords()
        me = 2 * x + y
        own = [_via_vmem(ins[t], outs[t].at[me], stage[t], loc, t) for t in range(n)]
        first, passed = [], []
        for t in range(n):
            half = shards[t].shape[0] // 2
            mine, theirs = pl.ds(c * half, half), pl.ds((1 - c) * half, half)
            for k, (dx, dy) in enumerate(_REL3):
                px, py = _flip(x, dx), _flip(y, dy)
                first.append(pltpu.make_async_remote_copy(
                    src_ref=ins[t].at[mine], dst_ref=outs[t].at[me, mine],
                    send_sem=send.at[3 * t + k], recv_sem=recv.at[3 * t + k],
                    device_id=(px, py, c), device_id_type=MESH))
                passed.append((
                    pltpu.make_async_remote_copy(
                        src_ref=outs[t].at[2 * px + py, mine], dst_ref=outs[t].at[2 * px + py, mine],
                        send_sem=fsend.at[3 * t + k], recv_sem=frecv.at[3 * t + k],
                        device_id=(x, y, 1 - c), device_id_type=MESH),
                    pltpu.make_async_remote_copy(
                        src_ref=outs[t].at[2 * px + py, theirs], dst_ref=outs[t].at[2 * px + py, theirs],
                        send_sem=fsend.at[3 * t + k], recv_sem=frecv.at[3 * t + k],
                        device_id=(x, y, 1 - c), device_id_type=MESH)))
        for load, _ in own:
            load.start()
        for cp in first:
            cp.start()
        for load, store in own:
            load.wait()
            store.start()
        for cp, (fwd, _) in zip(first, passed):
            cp.wait_recv()
            fwd.start()
        for cp, (fwd, back) in zip(first, passed):
            cp.wait_send()
            fwd.wait_send()
            back.wait_recv()
        for _, store in own:
            store.wait()

    return pl.pallas_call(
        body, name=name,
        out_shape=[jax.ShapeDtypeStruct((NSH,) + s.shape, s.dtype) for s in shards],
        in_specs=[ANY] * n, out_specs=[ANY] * n,
        scratch_shapes=[pltpu.SemaphoreType.DMA((3 * n,)) for _ in range(4)] + [pltpu.SemaphoreType.DMA((2 * n,))]
        + [pltpu.VMEM(s.shape, s.dtype) for s in shards],
        compiler_params=pltpu.CompilerParams(vmem_limit_bytes=VMEM_LIMIT),
    )(*shards)


_REL7 = tuple((dx, dy, dc) for dx in (0, 1) for dy in (0, 1) for dc in (0, 1))[1:]


def _gather_all(a, name):
    def body(a_ref, o_ref, send, recv, loc):
        x, y, c = _coords()
        me = 4 * x + 2 * y + c
        local = [pltpu.make_async_copy(a_ref, o_ref.at[me], loc.at[0])]
        remote = [pltpu.make_async_remote_copy(
            src_ref=a_ref, dst_ref=o_ref.at[me], send_sem=send.at[k], recv_sem=recv.at[k],
            device_id=(_flip(x, dx), _flip(y, dy), _flip(c, dc)), device_id_type=MESH)
            for k, (dx, dy, dc) in enumerate(_REL7)]
        _run(local, remote)

    return pl.pallas_call(
        body, name=name,
        out_shape=jax.ShapeDtypeStruct((8,) + a.shape, a.dtype),
        in_specs=[ANY], out_specs=ANY,
        scratch_shapes=[pltpu.SemaphoreType.DMA((7,)), pltpu.SemaphoreType.DMA((7,)),
                        pltpu.SemaphoreType.DMA((1,))],
    )(a)


def _pair_exchange(grads, name):
    n = len(grads)

    def body(*refs):
        ins, outs = refs[:n], refs[n:2 * n]
        send, recv = refs[2 * n:]
        x, y, c = _coords()
        remote = []
        for t in range(n):
            half = grads[t].shape[1] // 2
            remote.append(pltpu.make_async_remote_copy(
                src_ref=ins[t].at[:, pl.ds((1 - c) * half, half)], dst_ref=outs[t],
                send_sem=send.at[t], recv_sem=recv.at[t],
                device_id=(x, y, 1 - c), device_id_type=MESH))
        _run([], remote)

    return pl.pallas_call(
        body, name=name,
        out_shape=[jax.ShapeDtypeStruct((NSH, g.shape[1] // 2, g.shape[2]), g.dtype) for g in grads],
        in_specs=[ANY] * n, out_specs=[ANY] * n,
        scratch_shapes=[pltpu.SemaphoreType.DMA((n,)), pltpu.SemaphoreType.DMA((n,))],
    )(*grads)


def _scatter_chips(sums, name):
    n = len(sums)

    def body(*refs):
        ins, outs = refs[:n], refs[n:2 * n]
        send, recv, loc = refs[2 * n:2 * n + 3]
        stage = refs[2 * n + 3:]
        x, y, c = _coords()
        me = 2 * x + y
        local = [_via_vmem(ins[t].at[me], outs[t].at[me], stage[t], loc, t) for t in range(n)]
        remote = []
        for t in range(n):
            for k, (dx, dy) in enumerate(_REL3):
                px, py = _flip(x, dx), _flip(y, dy)
                remote.append(pltpu.make_async_remote_copy(
                    src_ref=ins[t].at[2 * px + py], dst_ref=outs[t].at[me],
                    send_sem=send.at[3 * t + k], recv_sem=recv.at[3 * t + k],
                    device_id=(px, py, c), device_id_type=MESH))
        _run_staged(local, remote)

    return pl.pallas_call(
        body, name=name,
        out_shape=[jax.ShapeDtypeStruct(s.shape, s.dtype) for s in sums],
        in_specs=[ANY] * n, out_specs=[ANY] * n,
        scratch_shapes=[pltpu.SemaphoreType.DMA((3 * n,)), pltpu.SemaphoreType.DMA((3 * n,)),
                        pltpu.SemaphoreType.DMA((2 * n,))]
        + [pltpu.VMEM(s.shape[1:], s.dtype) for s in sums],
        compiler_params=pltpu.CompilerParams(vmem_limit_bytes=VMEM_LIMIT),
    )(*sums)


def _swap_halves(halves, name):
    n = len(halves)

    def body(*refs):
        ins, outs = refs[:n], refs[n:2 * n]
        send, recv, loc = refs[2 * n:2 * n + 3]
        stage = refs[2 * n + 3:]
        x, y, c = _coords()
        local = [_via_vmem(ins[t], outs[t].at[c], stage[t], loc, t) for t in range(n)]
        remote = [pltpu.make_async_remote_copy(
            src_ref=ins[t], dst_ref=outs[t].at[c], send_sem=send.at[t], recv_sem=recv.at[t],
            device_id=(x, y, 1 - c), device_id_type=MESH) for t in range(n)]
        _run_staged(local, remote)

    return pl.pallas_call(
        body, name=name,
        out_shape=[jax.ShapeDtypeStruct((2,) + h.shape, h.dtype) for h in halves],
        in_specs=[ANY] * n, out_specs=[ANY] * n,
        scratch_shapes=[pltpu.SemaphoreType.DMA((n,)), pltpu.SemaphoreType.DMA((n,)),
                        pltpu.SemaphoreType.DMA((2 * n,))]
        + [pltpu.VMEM(h.shape, h.dtype) for h in halves],
        compiler_params=pltpu.CompilerParams(vmem_limit_bytes=VMEM_LIMIT),
    )(*halves)


def _sum_slots(r, name):
    K, R, C = r.shape
    tr = _tile(R, max(16, (1 << 22) // (K * C)), 8 * (4 // r.dtype.itemsize))

    def body(r_ref, o_ref):
        acc = r_ref[0].astype(F32)
        for k in range(1, K):
            acc = acc + r_ref[k].astype(F32)
        o_ref[...] = acc

    return pl.pallas_call(
        body, name=name, grid=(R // tr,),
        out_shape=jax.ShapeDtypeStruct((R, C), F32),
        in_specs=[pl.BlockSpec((K, tr, C), lambda i: (0, i, 0))],
        out_specs=pl.BlockSpec((tr, C), lambda i: (i, 0)),
        compiler_params=_params("parallel"),
    )(r)


def _add_pair(a, b, name):
    R, C = a.shape
    tr = _tile(R, max(16, (1 << 20) // C))

    def body(a_ref, b_ref, o_ref):
        o_ref[...] = (a_ref[...].astype(F32) + b_ref[...].astype(F32)).astype(BF16)

    spec = pl.BlockSpec((tr, C), lambda i: (i, 0))
    return pl.pallas_call(
        body, name=name, grid=(R // tr,),
        out_shape=jax.ShapeDtypeStruct((R, C), BF16),
        in_specs=[spec, spec], out_specs=spec,
        compiler_params=_params("parallel"),
    )(a, b)


def _adamw(w, g, m, v, name):
    R, C = w.shape
    tr = _tile(R, max(8, (1 << 18) // C), 8)
    c1 = 1.0 / (1.0 - B1 ** STEP)
    c2 = 1.0 / (1.0 - B2 ** STEP)

    def body(w_ref, g_ref, m_ref, v_ref, d_ref, nm_ref, nv_ref):
        gv = g_ref[...]
        nm = B1 * m_ref[...] + (1.0 - B1) * gv
        nv = B2 * v_ref[...] + (1.0 - B2) * gv * gv
        nm_ref[...] = nm
        nv_ref[...] = nv
        d_ref[...] = -LR * ((nm * c1) / (jnp.sqrt(nv * c2) + AEPS) + WD * w_ref[...])

    spec = pl.BlockSpec((tr, C), lambda i: (i, 0))
    return pl.pallas_call(
        body, name=name, grid=(R // tr,),
        out_shape=[jax.ShapeDtypeStruct((R, C), F32)] * 3,
        in_specs=[spec] * 4, out_specs=[spec] * 3,
        compiler_params=_params("parallel"),
    )(w, g, m, v)


def _ffn_fwd(h, g, w1, w3, w2, name, comm=None):
    L = h.shape[0]
    tm = _tile(L, 528)

    def body(h_ref, g_ref, w1_ref, w3_ref, w2_ref, o_ref, a_ref, b_ref, n_s, acc_s):
        j = pl.program_id(1)

        @pl.when(j == 0)
        def _():
            hv = h_ref[...]
            n, _ = _rms(hv, g_ref[...])
            n_s[...] = n.astype(BF16)
            acc_s[...] = hv

        n = n_s[...]
        a = _dot_nt(n, w1_ref[j])
        b = _dot_nt(n, w3_ref[j])
        a_ref[0] = a.astype(BF16)
        b_ref[0] = b.astype(BF16)
        s = (a * _sigmoid(a) * b).astype(BF16)
        acc_s[...] += 0.5 * _dot(s, w2_ref[j])

        @pl.when(j == NSH - 1)
        def _():
            o_ref[...] = acc_s[...]

    row = pl.BlockSpec((tm, D), lambda i, j: (i, 0))
    hid = pl.BlockSpec((1, tm, FS), lambda i, j: (j, i, 0))
    return _call(
        body, comm, name=name, grid=(L // tm, NSH),
        out_shape=[jax.ShapeDtypeStruct((L, D), F32),
                   jax.ShapeDtypeStruct((NSH, L, FS), BF16), jax.ShapeDtypeStruct((NSH, L, FS), BF16)],
        in_specs=[row, _res((1, D)), _res((NSH, FS, D)), _res((NSH, FS, D)), _res((NSH, FS, D))],
        out_specs=[row, hid, hid],
        scratch_shapes=[pltpu.VMEM((tm, D), BF16), pltpu.VMEM((tm, D), F32)],
        params=_params("arbitrary", "arbitrary"),
    )(h, g, w1, w3, w2)


def _ffn_bwd(h, g, dout, a, b, w1, w3, w2, name, comm=None):
    L = h.shape[0]
    tm = _tile(L, 352)

    def body(h_ref, g_ref, do_ref, a_ref, b_ref, w1_ref, w3_ref, w2_ref,
             dh_ref, da_ref, db_ref, s_ref, n_ref, dg_ref, dob_s, dn_s):
        i, j = pl.program_id(0), pl.program_id(1)

        @pl.when(j == 0)
        def _():
            n, _ = _rms(h_ref[...], g_ref[...])
            n_ref[...] = n.astype(BF16)
            dob_s[...] = do_ref[...].astype(BF16)
            dn_s[...] = jnp.zeros_like(dn_s)

        av = a_ref[0].astype(F32)
        bv = b_ref[0].astype(F32)
        sig = _sigmoid(av)
        sa = av * sig
        ds = 0.5 * _dot_nt(dob_s[...], w2_ref[j])
        s_ref[0] = (sa * bv).astype(BF16)
        da = (ds * bv * sig * (1.0 + av * (1.0 - sig))).astype(BF16)
        db = (ds * sa).astype(BF16)
        da_ref[0] = da
        db_ref[0] = db
        dn_s[...] += _dot(da, w1_ref[j]) + _dot(db, w3_ref[j])

        @pl.when(j == NSH - 1)
        def _():
            hv = h_ref[...]
            gv = g_ref[...]
            r = lax.rsqrt(jnp.mean(hv * hv, axis=-1, keepdims=True) + EPS)
            dn = dn_s[...]
            dx, xh = _rms_bwd(dn, hv, r, gv)
            dh_ref[...] = do_ref[...] + dx
            _acc_rows(dg_ref, jnp.sum(dn * xh, axis=0, keepdims=True), i == 0)

    row = pl.BlockSpec((tm, D), lambda i, j: (i, 0))
    hid = pl.BlockSpec((1, tm, FS), lambda i, j: (j, i, 0))
    return _call(
        body, comm, name=name, grid=(L // tm, NSH),
        out_shape=[jax.ShapeDtypeStruct((L, D), F32)]
        + [jax.ShapeDtypeStruct((NSH, L, FS), BF16)] * 3
        + [jax.ShapeDtypeStruct((L, D), BF16), jax.ShapeDtypeStruct((1, D), F32)],
        in_specs=[row, _res((1, D)), row, hid, hid,
                  _res((NSH, FS, D)), _res((NSH, FS, D)), _res((NSH, FS, D))],
        out_specs=[row, hid, hid, hid, row, pl.BlockSpec((1, D), lambda i, j: (0, 0))],
        scratch_shapes=[pltpu.VMEM((tm, D), BF16), pltpu.VMEM((tm, D), F32)],
        params=_params("arbitrary", "arbitrary"),
    )(h, g, dout, a, b, w1, w3, w2)


def _wgrad(xm, ym, name, scale=1.0):
    xs, ys = xm.ndim == 3, ym.ndim == 3
    assert not (xs and ys)
    L = xm.shape[-2]
    K, N = xm.shape[-1], ym.shape[-1]
    tl = _tile(L, 1056)
    nl = L // tl
    if xs or ys:
        tn, grid_n = N, NSH
    else:
        tn = _tile(N, 1024, 128)
        grid_n = N // tn

    def body(x_ref, y_ref, o_ref, acc_s):
        l = pl.program_id(1)
        xv = x_ref[0] if xs else x_ref[...]
        yv = y_ref[0] if ys else y_ref[...]
        part = _dot_tn(xv.astype(BF16), yv.astype(BF16))
        _acc_rows(acc_s, part, l == 0)

        @pl.when(l == nl - 1)
        def _():
            res = (acc_s[...] * scale).astype(BF16)
            if xs or ys:
                o_ref[0] = res
            else:
                o_ref[...] = res

    if xs:
        x_spec = pl.BlockSpec((1, tl, K), lambda n, l: (n, l, 0))
        y_spec = pl.BlockSpec((tl, N), lambda n, l: (l, 0))
        o_spec = pl.BlockSpec((1, K, N), lambda n, l: (n, 0, 0))
        o_shape = (NSH, K, N)
    elif ys:
        x_spec = pl.BlockSpec((tl, K), lambda n, l: (l, 0))
        y_spec = pl.BlockSpec((1, tl, N), lambda n, l: (n, l, 0))
        o_spec = pl.BlockSpec((1, K, N), lambda n, l: (n, 0, 0))
        o_shape = (NSH, K, N)
    else:
        x_spec = pl.BlockSpec((tl, K), lambda n, l: (l, 0))
        y_spec = pl.BlockSpec((tl, tn), lambda n, l: (l, n))
        o_spec = pl.BlockSpec((K, tn), lambda n, l: (0, n))
        o_shape = (K, N)
    return pl.pallas_call(
        body, name=name, grid=(grid_n, nl),
        out_shape=jax.ShapeDtypeStruct(o_shape, BF16),
        in_specs=[x_spec, y_spec], out_specs=o_spec,
        scratch_shapes=[pltpu.VMEM((K, tn), F32)],
        compiler_params=_params("parallel", "arbitrary"),
    )(xm, ym)


def _mix_in_fwd(h, g, w_in, b_gate, name, comm=None):
    L = h.shape[0]
    tm = _tile(L, 528)

    def body(h_ref, g_ref, w_ref, bg_ref, vg_ref, uf_ref, gt_ref):
        u, _ = _rms(h_ref[...], g_ref[...])
        ub = u.astype(BF16)
        p = [_dot(ub, w_ref[j]) for j in range(NSH)]
        a0, a1 = 2 * DC - WS, 2 * DC + DS - WS
        vg_ref[:, 0:WS] = p[0].astype(BF16)
        vg_ref[:, WS:2 * DC] = p[1][:, 0:a0].astype(BF16)
        uf_ref[...] = p[1][:, a0:a1].astype(BF16)
        gin = jnp.concatenate([p[1][:, a1:], p[2], p[3]], axis=1)
        gt_ref[...] = _sigmoid(gin + bg_ref[...]).astype(BF16)

    def row(n):
        return pl.BlockSpec((tm, n), lambda i: (i, 0))

    return _call(
        body, comm, name=name, grid=(L // tm,),
        out_shape=[jax.ShapeDtypeStruct((L, 2 * DC), BF16), jax.ShapeDtypeStruct((L, DS), BF16),
                   jax.ShapeDtypeStruct((L, 2 * D), BF16)],
        in_specs=[row(D), _res((1, D)), _res((NSH, D, WS)), _res((1, 2 * D))],
        out_specs=[row(2 * DC), row(DS), row(2 * D)],
        params=_params("parallel"),
    )(h, g, w_in, b_gate)


def _mix_in_bwd(h, g, dres, dv, dgl, duf, dgate, w_in, name):
    L = h.shape[0]
    tm = _tile(L, 528)

    def body(h_ref, g_ref, dr_ref, dv_ref, dgl_ref, duf_ref, dgt_ref, w_ref, dh_ref, u_ref, dp_ref, dgm_ref):
        i = pl.program_id(0)
        hv = h_ref[...]
        gv = g_ref[...]
        u, r = _rms(hv, gv)
        u_ref[...] = u.astype(BF16)
        a0, a1 = 2 * DC - WS, 2 * DC + DS - WS
        b0 = WS - a1
        dp = [jnp.concatenate([dv_ref[...], dgl_ref[:, 0:WS - DC]], axis=1),
              jnp.concatenate([dgl_ref[:, WS - DC:], duf_ref[...], dgt_ref[:, 0:b0]], axis=1),
              dgt_ref[:, b0:b0 + WS], dgt_ref[:, b0 + WS:]]
        du = jnp.zeros((tm, D), F32)
        for j in range(NSH):
            dp_ref[j] = dp[j]
            du = du + _dot_nt(dp[j], w_ref[j])
        dx, xh = _rms_bwd(du, hv, r, gv)
        dh_ref[...] = dr_ref[...] + dx
        _acc_rows(dgm_ref, jnp.sum(du * xh, axis=0, keepdims=True), i == 0)

    def row(n):
        return pl.BlockSpec((tm, n), lambda i: (i, 0))

    return pl.pallas_call(
        body, name=name, grid=(L // tm,),
        out_shape=[jax.ShapeDtypeStruct((L, D), F32), jax.ShapeDtypeStruct((L, D), BF16),
                   jax.ShapeDtypeStruct((NSH, L, WS), BF16), jax.ShapeDtypeStruct((1, D), F32)],
        in_specs=[row(D), _res((1, D)), row(D), row(DC), row(DC), row(DS), row(2 * D), _res((NSH, D, WS))],
        out_specs=[row(D), row(D), pl.BlockSpec((NSH, tm, WS), lambda i: (0, i, 0)),
                   pl.BlockSpec((1, D), lambda i: (0, 0))],
        compiler_params=_params("arbitrary"),
    )(h, g, dres, dv, dgl, duf, dgate, w_in)


def _conv_fwd(vg, dw, dwb, name, comm=None):
    L = vg.shape[0]
    nc = DC // 128

    def body(v_ref, g_ref, dw_ref, dwb_ref, z_ref, zp_s):
        zp_s[0:KWP, :] = jnp.zeros((KWP, 128), F32)
        zp_s[KWP:, :] = v_ref[...].astype(F32) * _sigmoid(g_ref[...].astype(F32))
        acc = jnp.broadcast_to(dwb_ref[...], (L, 128))
        for k in range(KW):
            acc = acc + dw_ref[k:k + 1, :] * zp_s[pl.ds(k + 2, L), :]
        z_ref[...] = acc

    return _call(
        body, comm, name=name, grid=(nc,),
        out_shape=[jax.ShapeDtypeStruct((L, DC), F32)],
        in_specs=[pl.BlockSpec((L, 128), lambda c: (0, c)), pl.BlockSpec((L, 128), lambda c: (0, nc + c)),
                  pl.BlockSpec((KWP, 128), lambda c: (0, c)), pl.BlockSpec((1, 128), lambda c: (0, c))],
        out_specs=[pl.BlockSpec((L, 128), lambda c: (0, c))],
        scratch_shapes=[pltpu.VMEM((L + KWP, 128), F32)],
        params=_params("parallel"),
    )(vg, vg, dw, dwb)


def _conv_bwd(dz1, vg, dw, name):
    L = vg.shape[0]
    nc = DC // 128

    def body(dz_ref, v_ref, g_ref, dw_ref, dv_ref, dg_ref, ddw_ref, ddwb_ref, zp_s, dzp_s):
        vv = v_ref[...].astype(F32)
        sg = _sigmoid(g_ref[...].astype(F32))
        zp_s[0:KWP, :] = jnp.zeros((KWP, 128), F32)
        zp_s[KWP:, :] = vv * sg
        dz = dz_ref[...]
        dzp_s[0:L, :] = dz
        dzp_s[L:, :] = jnp.zeros((KWP, 128), F32)
        ddwb_ref[...] = jnp.sum(dz, axis=0, keepdims=True)
        acc = jnp.zeros((L, 128), F32)
        for k in range(KW):
            acc = acc + dw_ref[k:k + 1, :] * dzp_s[pl.ds(KW - 1 - k, L), :]
            ddw_ref[k:k + 1, :] = jnp.sum(dz * zp_s[pl.ds(k + 2, L), :], axis=0, keepdims=True)
        ddw_ref[KW:KWP, :] = jnp.zeros((KWP - KW, 128), F32)
        dv_ref[...] = (acc * sg).astype(BF16)
        dg_ref[...] = (acc * vv * sg * (1.0 - sg)).astype(BF16)

    col = pl.BlockSpec((L, 128), lambda c: (0, c))
    return pl.pallas_call(
        body, name=name, grid=(nc,),
        out_shape=[jax.ShapeDtypeStruct((L, DC), BF16), jax.ShapeDtypeStruct((L, DC), BF16),
                   jax.ShapeDtypeStruct((KWP, DC), F32), jax.ShapeDtypeStruct((1, DC), F32)],
        in_specs=[col, col, pl.BlockSpec((L, 128), lambda c: (0, nc + c)),
                  pl.BlockSpec((KWP, 128), lambda c: (0, c))],
        out_specs=[col, col, pl.BlockSpec((KWP, 128), lambda c: (0, c)), pl.BlockSpec((1, 128), lambda c: (0, c))],
        scratch_shapes=[pltpu.VMEM((L + KWP, 128), F32), pltpu.VMEM((L + KWP, 128), F32)],
        compiler_params=_params("parallel"),
    )(dz1, vg, vg, dw)


NLB = QS // 128


def _lb_store(ref, rows, val):
    for cb in range(NLB):
        ref[cb, rows, :] = val[:, cb * 128:(cb + 1) * 128]


def _lb_load(ref, rows):
    return jnp.concatenate([ref[cb, rows, :] for cb in range(NLB)], axis=1)


def _scan(xr_ref, xi_ref, base, T, ar, ai, atr, ati, reverse):
    W = ar.shape[1]
    ar, ai, atr, ati = (jnp.broadcast_to(v, (8, W)) for v in (ar, ai, atr, ati))
    zero = jnp.zeros((8, W), F32)

    def rows(t, g):
        tt = T - 1 - t if reverse else t
        return pl.ds(base + g * 8 * T + tt, 8, stride=T)

    def make_step(store):
        def step(t, carry):
            out = []
            for g in range(NGRP):
                sr, si = carry[2 * g], carry[2 * g + 1]
                idx = rows(t, g)
                nr = ar * sr - ai * si + _lb_load(xr_ref, idx)
                ni = ar * si + ai * sr + _lb_load(xi_ref, idx)
                if store:
                    _lb_store(xr_ref, idx, nr)
                    _lb_store(xi_ref, idx, ni)
                out += [nr, ni]
            return tuple(out)
        return step

    ends = lax.fori_loop(0, T, make_step(False), (zero,) * (2 * NGRP))
    sub = lax.broadcasted_iota(jnp.int32, (8, W), 0)
    edge = sub == (7 if reverse else 0)
    shift, last = (7, 0) if reverse else (1, 7)
    inr, ini = jnp.zeros((1, W), F32), jnp.zeros((1, W), F32)
    starts = [None] * (2 * NGRP)
    for g in (reversed(range(NGRP)) if reverse else range(NGRP)):
        er, ei = ends[2 * g], ends[2 * g + 1]
        cr, ci = jnp.where(edge, inr, 0.0), jnp.where(edge, ini, 0.0)
        for _ in range(7):
            nr = atr * cr - ati * ci + er
            ni = atr * ci + ati * cr + ei
            cr = jnp.where(edge, inr, pltpu.roll(nr, shift, 0))
            ci = jnp.where(edge, ini, pltpu.roll(ni, shift, 0))
        starts[2 * g], starts[2 * g + 1] = cr, ci
        inr = (atr * cr - ati * ci + er)[last:last + 1]
        ini = (atr * ci + ati * cr + ei)[last:last + 1]
    lax.fori_loop(0, T, make_step(True), tuple(starts))


def _ssm_fwd(uf, bre, bim, cre, cim, lamp, dsk, name, comm=None):
    L = uf.shape[0]
    T = L // NSEG
    tc = L // NCH

    def body(u_ref, bre_ref, bim_ref, cre_ref, cim_ref, lam_ref, d_ref, y_ref, sr_s, si_s):
        for k in range(NCH):
            sl = slice(k * tc, (k + 1) * tc)
            uk = u_ref[sl, :]
            _lb_store(sr_s, sl, _dot(uk, bre_ref[...]))
            _lb_store(si_s, sl, _dot(uk, bim_ref[...]))
        _scan(sr_s, si_s, 0, T, lam_ref[0:1, :], lam_ref[1:2, :], lam_ref[2:3, :], lam_ref[3:4, :], False)
        for k in range(NCH):
            sl = slice(k * tc, (k + 1) * tc)
            y_ref[sl, :] = (_dot(_lb_load(sr_s, sl).astype(BF16), cre_ref[...])
                            - _dot(_lb_load(si_s, sl).astype(BF16), cim_ref[...])
                            + d_ref[...] * u_ref[sl, :].astype(F32))

    return _call(
        body, comm, name=name, grid=(NQ,),
        out_shape=[jax.ShapeDtypeStruct((L, DS), F32)],
        in_specs=[pl.BlockSpec((L, QU), lambda q: (0, q)),
                  pl.BlockSpec((QU, QS), lambda q: (q, q)), pl.BlockSpec((QU, QS), lambda q: (q, q)),
                  pl.BlockSpec((QS, QU), lambda q: (q, q)), pl.BlockSpec((QS, QU), lambda q: (q, q)),
                  pl.BlockSpec((8, QS), lambda q: (0, q)), pl.BlockSpec((1, QU), lambda q: (0, q))],
        out_specs=[pl.BlockSpec((L, QU), lambda q: (0, q))],
        scratch_shapes=[pltpu.VMEM((NLB, L, 128), F32), pltpu.VMEM((NLB, L, 128), F32)],
        params=_params("parallel"),
    )(uf, bre, bim, cre, cim, lamp, dsk)


def _ssm_bwd(uf, dyss, bre, bim, cre, cim, lamp, dsk, name, comm=None):
    L = uf.shape[0]
    T = L // NSEG
    tc = L // NCH

    def body(u_ref, dy_ref, bre_ref, bim_ref, cre_ref, cim_ref, lam_ref, d_ref,
             du_ref, dbre_ref, dbim_ref, dcre_ref, dcim_ref, dlam_ref, dd_ref, sr_s, si_s, gr_s, gi_s):
        _lb_store(sr_s, slice(0, SOFF), jnp.zeros((SOFF, QS), F32))
        _lb_store(si_s, slice(0, SOFF), jnp.zeros((SOFF, QS), F32))
        for k in range(NCH):
            sl = slice(k * tc, (k + 1) * tc)
            ss = slice(SOFF + k * tc, SOFF + (k + 1) * tc)
            uk = u_ref[sl, :]
            dyk = dy_ref[sl, :].astype(BF16)
            _lb_store(sr_s, ss, _dot(uk, bre_ref[...]))
            _lb_store(si_s, ss, _dot(uk, bim_ref[...]))
            _lb_store(gr_s, sl, _dot_nt(dyk, cre_ref[...]))
            _lb_store(gi_s, sl, -_dot_nt(dyk, cim_ref[...]))
        ar, ai, atr, ati = lam_ref[0:1, :], lam_ref[1:2, :], lam_ref[2:3, :], lam_ref[3:4, :]
        _scan(sr_s, si_s, SOFF, T, ar, ai, atr, ati, False)
        _scan(gr_s, gi_s, 0, T, ar, -ai, atr, -ati, True)
        dbre = jnp.zeros((QU, QS), F32)
        dbim = jnp.zeros((QU, QS), F32)
        dcre = jnp.zeros((QS, QU), F32)
        dcim = jnp.zeros((QS, QU), F32)
        dd = jnp.zeros((1, QU), F32)
        qr = jnp.zeros((1, QS), F32)
        qi = jnp.zeros((1, QS), F32)
        for k in range(NCH):
            sl = slice(k * tc, (k + 1) * tc)
            ss = slice(SOFF + k * tc, SOFF + (k + 1) * tc)
            sp = slice(SOFF - 1 + k * tc, SOFF - 1 + (k + 1) * tc)
            uk = u_ref[sl, :]
            dyk = dy_ref[sl, :]
            dyb = dyk.astype(BF16)
            gr, gi = _lb_load(gr_s, sl), _lb_load(gi_s, sl)
            pr, pi = _lb_load(sr_s, sp), _lb_load(si_s, sp)
            qr = qr + jnp.sum(gr * pr + gi * pi, axis=0, keepdims=True)
            qi = qi + jnp.sum(gi * pr - gr * pi, axis=0, keepdims=True)
            grb, gib = gr.astype(BF16), gi.astype(BF16)
            du_ref[sl, :] = (_dot_nt(grb, bre_ref[...]) + _dot_nt(gib, bim_ref[...])
                             + dyk * d_ref[...]).astype(BF16)
            dbre = dbre + _dot_tn(uk, grb)
            dbim = dbim + _dot_tn(uk, gib)
            dcre = dcre + _dot_tn(_lb_load(sr_s, ss).astype(BF16), dyb)
            dcim = dcim - _dot_tn(_lb_load(si_s, ss).astype(BF16), dyb)
            dd = dd + jnp.sum(dyk * uk.astype(F32), axis=0, keepdims=True)
        dlam_ref[0] = jnp.concatenate([qr, qi, jnp.zeros((6, QS), F32)], axis=0)
        dbre_ref[0] = dbre
        dbim_ref[0] = dbim
        dcre_ref[0] = dcre
        dcim_ref[0] = dcim
        dd_ref[...] = dd

    col = pl.BlockSpec((L, QU), lambda q: (0, q))
    bsp = pl.BlockSpec((QU, QS), lambda q: (q, q))
    csp = pl.BlockSpec((QS, QU), lambda q: (q, q))
    return _call(
        body, comm, name=name, grid=(NQ,),
        out_shape=[jax.ShapeDtypeStruct((L, DS), BF16),
                   jax.ShapeDtypeStruct((NQ, QU, QS), F32), jax.ShapeDtypeStruct((NQ, QU, QS), F32),
                   jax.ShapeDtypeStruct((NQ, QS, QU), F32), jax.ShapeDtypeStruct((NQ, QS, QU), F32),
                   jax.ShapeDtypeStruct((NQ, 8, QS), F32), jax.ShapeDtypeStruct((1, DS), F32)],
        in_specs=[col, col, bsp, bsp, csp, csp,
                  pl.BlockSpec((8, QS), lambda q: (0, q)), pl.BlockSpec((1, QU), lambda q: (0, q))],
        out_specs=[col,
                   pl.BlockSpec((1, QU, QS), lambda q: (q, 0, 0)), pl.BlockSpec((1, QU, QS), lambda q: (q, 0, 0)),
                   pl.BlockSpec((1, QS, QU), lambda q: (q, 0, 0)), pl.BlockSpec((1, QS, QU), lambda q: (q, 0, 0)),
                   pl.BlockSpec((1, 8, QS), lambda q: (q, 0, 0)), pl.BlockSpec((1, QU), lambda q: (0, q))],
        scratch_shapes=[pltpu.VMEM((NLB, L + SOFF, 128), F32), pltpu.VMEM((NLB, L + SOFF, 128), F32),
                        pltpu.VMEM((NLB, L, 128), F32), pltpu.VMEM((NLB, L, 128), F32)],
        params=_params("parallel"),
    )(uf, dyss, bre, bim, cre, cim, lamp, dsk)


def _branches(z1_ref, yss_ref, gt_ref, lng_ref, lnb_ref, wp_ref, wv_ref, wg_ref):
    zf = z1_ref[...]
    mu = jnp.mean(zf, axis=-1, keepdims=True)
    zc = zf - mu
    rstd = lax.rsqrt(jnp.mean(zc * zc, axis=-1, keepdims=True) + EPS)
    zn = zc * rstd
    z2 = zn * lng_ref[...] + lnb_ref[...]
    sz = _sigmoid(z2)
    z3 = (z2 * sz).astype(BF16)
    y_conv = _dot(z3, wp_ref[...])
    yss = yss_ref[...]
    yg = _gelu(yss).astype(BF16)
    sv = _dot(yg, wv_ref[...])
    sig = _sigmoid(_dot(yg, wg_ref[...]))
    y_ssm = sv * sig
    gc = gt_ref[:, 0:D].astype(F32)
    gs = gt_ref[:, D:2 * D].astype(F32)
    m = gc * y_conv + gs * y_ssm
    return dict(rstd=rstd, zn=zn, z2=z2, sz=sz, z3=z3, y_conv=y_conv, yss=yss, yg=yg, sv=sv, sig=sig,
                y_ssm=y_ssm, gc=gc, gs=gs, m=m)


def _merge_fwd(h, z1, yss, gate, lng, lnb, wp, wv, wg, wo, name):
    L = h.shape[0]
    tm = _tile(L, 528)

    def body(h_ref, z1_ref, yss_ref, gt_ref, lng_ref, lnb_ref, wp_ref, wv_ref, wg_ref, wo_ref, o_ref):
        f = _branches(z1_ref, yss_ref, gt_ref, lng_ref, lnb_ref, wp_ref, wv_ref, wg_ref)
        o_ref[...] = h_ref[...] + _dot(f["m"].astype(BF16), wo_ref[...])

    def row(n):
        return pl.BlockSpec((tm, n), lambda i: (i, 0))

    return pl.pallas_call(
        body, name=name, grid=(L // tm,),
        out_shape=jax.ShapeDtypeStruct((L, D), F32),
        in_specs=[row(D), row(DC), row(DS), row(2 * D), _res((1, DC)), _res((1, DC)),
                  _res((DC, D)), _res((DS, D)), _res((DS, D)), _res((D, D))],
        out_specs=row(D),
        compiler_params=_params("parallel"),
    )(h, z1, yss, gate, lng, lnb, wp, wv, wg, wo)


def _merge_bwd(dh, z1, yss, gate, lng, lnb, wp, wv, wg, wo, name):
    L = dh.shape[0]
    tm = _tile(L, 352)

    def body(dh_ref, z1_ref, yss_ref, gt_ref, lng_ref, lnb_ref, wp_ref, wv_ref, wg_ref, wo_ref,
             m_ref, dgt_ref, dyc_ref, z3_ref, dz1_ref, yg_ref, dsv_ref, dsg_ref, dyss_ref,
             dbg_ref, dlng_ref, dlnb_ref):
        i = pl.program_id(0)
        f = _branches(z1_ref, yss_ref, gt_ref, lng_ref, lnb_ref, wp_ref, wv_ref, wg_ref)
        gc, gs, sig, sv = f["gc"], f["gs"], f["sig"], f["sv"]
        m_ref[...] = f["m"].astype(BF16)
        z3_ref[...] = f["z3"]
        yg_ref[...] = f["yg"]
        dm = _dot_nt(dh_ref[...].astype(BF16), wo_ref[...])
        dgc = (dm * f["y_conv"] * gc * (1.0 - gc)).astype(BF16)
        dgs = (dm * f["y_ssm"] * gs * (1.0 - gs)).astype(BF16)
        dgt_ref[:, 0:D] = dgc
        dgt_ref[:, D:2 * D] = dgs
        part = jnp.concatenate([jnp.sum(dgc.astype(F32), axis=0, keepdims=True),
                                jnp.sum(dgs.astype(F32), axis=0, keepdims=True)], axis=1)
        _acc_rows(dbg_ref, part, i == 0)
        dyc = (dm * gc).astype(BF16)
        dyc_ref[...] = dyc
        dys = dm * gs
        dsv = (dys * sig).astype(BF16)
        dsg = (dys * sv * sig * (1.0 - sig)).astype(BF16)
        dsv_ref[...] = dsv
        dsg_ref[...] = dsg
        dyg = _dot_nt(dsv, wv_ref[...]) + _dot_nt(dsg, wg_ref[...])
        dyss_ref[...] = dyg * _gelu_grad(f["yss"])
        dz3 = _dot_nt(dyc, wp_ref[...])
        z2, sz, zn = f["z2"], f["sz"], f["zn"]
        dz2 = dz3 * sz * (1.0 + z2 * (1.0 - sz))
        _acc_rows(dlng_ref, jnp.sum(dz2 * zn, axis=0, keepdims=True), i == 0)
        _acc_rows(dlnb_ref, jnp.sum(dz2, axis=0, keepdims=True), i == 0)
        dzn = dz2 * lng_ref[...]
        dz1_ref[...] = f["rstd"] * (dzn - jnp.mean(dzn, axis=-1, keepdims=True)
                                    - zn * jnp.mean(dzn * zn, axis=-1, keepdims=True))

    def row(n):
        return pl.BlockSpec((tm, n), lambda i: (i, 0))

    def tot(n):
        return pl.BlockSpec((1, n), lambda i: (0, 0))

    return pl.pallas_call(
        body, name=name, grid=(L // tm,),
        out_shape=[jax.ShapeDtypeStruct((L, D), BF16), jax.ShapeDtypeStruct((L, 2 * D), BF16),
                   jax.ShapeDtypeStruct((L, D), BF16), jax.ShapeDtypeStruct((L, DC), BF16),
                   jax.ShapeDtypeStruct((L, DC), F32), jax.ShapeDtypeStruct((L, DS), BF16),
                   jax.ShapeDtypeStruct((L, D), BF16), jax.ShapeDtypeStruct((L, D), BF16),
                   jax.ShapeDtypeStruct((L, DS), F32),
                   jax.ShapeDtypeStruct((1, 2 * D), F32), jax.ShapeDtypeStruct((1, DC), F32),
                   jax.ShapeDtypeStruct((1, DC), F32)],
        in_specs=[row(D), row(DC), row(DS), row(2 * D), _res((1, DC)), _res((1, DC)),
                  _res((DC, D)), _res((DS, D)), _res((DS, D)), _res((D, D))],
        out_specs=[row(D), row(2 * D), row(D), row(DC), row(DC), row(DS), row(D), row(D), row(DS),
                   tot(2 * D), tot(DC), tot(DC)],
        compiler_params=_params("arbitrary"),
    )(dh, z1, yss, gate, lng, lnb, wp, wv, wg, wo)


def _final(h, g, tgt, name):
    L = h.shape[0]
    tm = _tile(L, 528)

    def body(h_ref, g_ref, t_ref, dh_ref, loss_ref, dg_ref):
        i = pl.program_id(0)
        hv = h_ref[...]
        gv = g_ref[...]
        y, r = _rms(hv, gv)
        row = i * tm + lax.broadcasted_iota(jnp.int32, (tm, 1), 0)
        e = jnp.where(row >= FRONT, y - t_ref[...], 0.0)
        dy = e * (1.0 / D)
        part = 0.5 * jnp.sum(jnp.sum(e * dy, axis=1, keepdims=True), axis=0, keepdims=True)
        dx, xh = _rms_bwd(dy, hv, r, gv)
        dh_ref[...] = dx
        _acc_rows(loss_ref, part, i == 0)
        _acc_rows(dg_ref, jnp.sum(dy * xh, axis=0, keepdims=True), i == 0)

    row = pl.BlockSpec((tm, D), lambda i: (i, 0))
    return pl.pallas_call(
        body, name=name, grid=(L // tm,),
        out_shape=[jax.ShapeDtypeStruct((L, D), F32), jax.ShapeDtypeStruct((1, 1), F32),
                   jax.ShapeDtypeStruct((1, D), F32)],
        in_specs=[row, _res((1, D)), row],
        out_specs=[row, pl.BlockSpec((1, 1), lambda i: (0, 0)), pl.BlockSpec((1, D), lambda i: (0, 0))],
        compiler_params=_params("arbitrary"),
    )(h, g, tgt)


def _ssm_disc(lam_re, lam_im, log_dt, b_re, b_im):
    lam = lax.complex(lam_re, lam_im)
    dt = jnp.exp(log_dt)[:, None]
    lam_bar = jnp.exp(lam * dt)
    bbar = ((lam_bar - 1.0) / lam)[..., None] * lax.complex(b_re, b_im)
    return jnp.real(lam_bar), jnp.imag(lam_bar), jnp.real(bbar), jnp.imag(bbar)


def _bdiag_in(m):
    return jnp.einsum("gph,gk->ghkp", m, jnp.eye(G, dtype=m.dtype)).reshape(G * H, G * P)


def _bdiag_out(m):
    return jnp.einsum("ghp,gk->gpkh", m, jnp.eye(G, dtype=m.dtype)).reshape(G * P, G * H)


def _diag_blocks(m4):
    return jnp.einsum("qiaib->qiab", m4).reshape(G, m4.shape[2], m4.shape[4])


def _pack(parts, rows_mult=8):
    flat = jnp.concatenate([p.reshape(-1).astype(F32) for p in parts])
    n = flat.shape[0]
    tot = -(-n // (128 * rows_mult)) * (128 * rows_mult)
    return jnp.pad(flat, (0, tot - n)).reshape(tot // 128, 128)


def _unpack(buf, shapes):
    flat = buf.reshape(-1)
    out, o = [], 0
    for s in shapes:
        n = math.prod(s)
        out.append(flat[o:o + n].reshape(s))
        o += n
    return out


def kernel(x, meta_tokens, ffn1_norm, ffn1_w1, ffn1_w3, ffn1_w2, mix_norm, w_in, b_gate, conv_dw, conv_dw_b, conv_ln_g, conv_ln_b, conv_proj, ssm_lam_re, ssm_lam_im, ssm_log_dt, ssm_b_re, ssm_b_im, ssm_c_re, ssm_c_im, ssm_d, ssm_w_v, ssm_w_g, w_out, ffn2_norm, ffn2_w1, ffn2_w3, ffn2_w2, final_norm, loss_target, m_meta_tokens, m_ffn1_norm, m_ffn1_w1, m_ffn1_w3, m_ffn1_w2, m_mix_norm, m_w_in, m_b_gate, m_conv_dw, m_conv_dw_b, m_conv_ln_g, m_conv_ln_b, m_conv_proj, m_ssm_lam_re, m_ssm_lam_im, m_ssm_log_dt, m_ssm_b_re, m_ssm_b_im, m_ssm_c_re, m_ssm_c_im, m_ssm_d, m_ssm_w_v, m_ssm_w_g, m_w_out, m_ffn2_norm, m_ffn2_w1, m_ffn2_w3, m_ffn2_w2, m_final_norm, v_meta_tokens, v_ffn1_norm, v_ffn1_w1, v_ffn1_w3, v_ffn1_w2, v_mix_norm, v_w_in, v_b_gate, v_conv_dw, v_conv_dw_b, v_conv_ln_g, v_conv_ln_b, v_conv_proj, v_ssm_lam_re, v_ssm_lam_im, v_ssm_log_dt, v_ssm_b_re, v_ssm_b_im, v_ssm_c_re, v_ssm_c_im, v_ssm_d, v_ssm_w_v, v_ssm_w_g, v_w_out, v_ffn2_norm, v_ffn2_w1, v_ffn2_w3, v_ffn2_w2, v_final_norm):
    args = dict(locals())
    names = ["meta_tokens", "ffn1_norm", "ffn1_w1", "ffn1_w3", "ffn1_w2", "mix_norm", "w_in", "b_gate",
             "conv_dw", "conv_dw_b", "conv_ln_g", "conv_ln_b", "conv_proj", "ssm_lam_re", "ssm_lam_im",
             "ssm_log_dt", "ssm_b_re", "ssm_b_im", "ssm_c_re", "ssm_c_im", "ssm_d", "ssm_w_v", "ssm_w_g",
             "w_out", "ffn2_norm", "ffn2_w1", "ffn2_w3", "ffn2_w2", "final_norm"]
    big = ["ffn1_w1", "ffn1_w3", "ffn1_w2", "w_in", "conv_proj", "ssm_w_v", "ssm_w_g", "w_out",
           "ffn2_w1", "ffn2_w3", "ffn2_w2"]
    small = [n for n in names if n not in big]

    xs = x[0]
    S = xs.shape[0]
    L = FRONT + S
    T = L // NSEG
    jx, jy = lax.axis_index("x"), lax.axis_index("y")
    chip = 2 * jx + jy

    sm = _gather_all(_pack([meta_tokens, conv_dw[0]]), "gather_small")[0::2].reshape(NSH, -1)
    nmt = NMETA * (D // NSH)
    ndw = KW * (DC // NSH)
    meta_full = sm[:, :nmt].reshape(NSH, NMETA, D // NSH).transpose(1, 0, 2).reshape(NMETA, D)
    dw_full = sm[:, nmt:nmt + ndw].reshape(NSH, KW, DC // NSH).transpose(1, 0, 2).reshape(KW, DC)
    dw_pad = jnp.pad(dw_full, ((0, KWP - KW), (0, 0)))
    tposed = ("ffn1_w1", "ffn1_w3", "ffn2_w1", "ffn2_w3")
    grp_a = ["ffn1_w1", "ffn1_w3", "ffn1_w2"]
    grp_b = ["w_in", "conv_proj", "ssm_w_v", "ssm_w_g", "w_out"]
    grp_c = ["ffn2_w1", "ffn2_w3", "ffn2_w2"]

    def shard(n):
        return (args[n][0].T if n in tposed else args[n][0]).astype(BF16)

    gw = dict(zip(grp_a, _gather_chips([shard(n) for n in grp_a], "gather_ffn1")))

    def cols(w):
        return w.transpose(1, 0, 2).reshape(w.shape[1], -1)

    disc_in = (ssm_lam_re[0], ssm_lam_im[0], ssm_log_dt[0], ssm_b_re[0], ssm_b_im[0])
    (lbr, lbi, bbr, bbi), disc_vjp = jax.vjp(_ssm_disc, *disc_in)
    lam_t = jnp.exp(lax.complex(ssm_lam_re[0], ssm_lam_im[0]) * (jnp.exp(ssm_log_dt[0])[:, None] * T))
    lamp = jnp.concatenate([lbr.reshape(1, NST), lbi.reshape(1, NST), jnp.real(lam_t).reshape(1, NST),
                            jnp.imag(lam_t).reshape(1, NST), jnp.zeros((4, NST), F32)], axis=0)
    bre_bd, bim_bd = _bdiag_in(bbr).astype(BF16), _bdiag_in(bbi).astype(BF16)
    cre_bd, cim_bd = _bdiag_out(ssm_c_re[0]).astype(BF16), _bdiag_out(ssm_c_im[0]).astype(BF16)

    h0 = jnp.concatenate([jnp.zeros((FRONT - NMETA, D), F32), meta_full, xs], axis=0)
    tgt = jnp.pad(loss_target[0], ((FRONT, 0), (0, 0)))
    (h1, a1, b1), got = _ffn_fwd(h0, ffn1_norm, gw["ffn1_w1"], gw["ffn1_w3"], gw["ffn1_w2"], "ffn1_fwd",
                                 _gather_half_behind([shard(n) for n in grp_b]))
    gw.update(zip(grp_b, _pass_halves(got, "pass_mix")))
    w_in_f = gw["w_in"]
    wp_f, wv_f, wg_f = cols(gw["conv_proj"]), cols(gw["ssm_w_v"]), cols(gw["ssm_w_g"])
    wo_f = gw["w_out"].reshape(D, D)
    (vg, uf, gate), got1 = _mix_in_fwd(h1, mix_norm, w_in_f, b_gate, "mix_in_fwd",
                                       _gather_half_behind([shard("ffn2_w1")]))
    (z1,), got3 = _conv_fwd(vg, dw_pad, conv_dw_b, "conv_fwd", _gather_half_behind([shard("ffn2_w3")]))
    (yss,), got2 = _ssm_fwd(uf, bre_bd, bim_bd, cre_bd, cim_bd, lamp, ssm_d, "ssm_fwd",
                            _gather_half_behind([shard("ffn2_w2")]))
    gw.update(zip(grp_c, _pass_halves([got1[0], got3[0], got2[0]], "pass_ffn2")))
    h2 = _merge_fwd(h1, z1, yss, gate, conv_ln_g, conv_ln_b, wp_f, wv_f, wg_f, wo_f, "merge_fwd")
    (h3, a2, b2), _ = _ffn_fwd(h2, ffn2_norm, gw["ffn2_w1"], gw["ffn2_w3"], gw["ffn2_w2"], "ffn2_fwd")

    gbig = {}
    core = lax.axis_index("c")

    def pair_sums(group, tag):
        gl = [gbig[n] for n in group]
        sib = _pair_exchange(gl, "pair_exchange_" + tag)
        out = []
        for n, g_, s_ in zip(group, gl, sib):
            half = g_.shape[1] // 2
            mine = lax.dynamic_slice_in_dim(g_, core * half, half, axis=1)
            out.append(_add_pair(mine.reshape(NSH * half, -1), s_.reshape(NSH * half, -1),
                                 "pair_" + n).reshape(s_.shape))
        return out

    dh3, loss_part, d_final = _final(h3, final_norm.reshape(1, D), tgt, "final")
    (dh2, da2, db2, s2, n2, d_ffn2_norm), _ = _ffn_bwd(
        h2, ffn2_norm, dh3, a2, b2, gw["ffn2_w1"], gw["ffn2_w3"], gw["ffn2_w2"], "ffn2_bwd")
    gbig["ffn2_w1"] = _wgrad(da2, n2, "ffn2_dw1")
    gbig["ffn2_w3"] = _wgrad(db2, n2, "ffn2_dw3")
    gbig["ffn2_w2"] = _wgrad(s2, dh3, "ffn2_dw2", 0.5)
    pair_c = pair_sums(grp_c, "ffn2")
    (m_b, dgate, dyc, z3, dz1, yg, dsv, dsg, dyss, d_b_gate, d_ln_g, d_ln_b) = _merge_bwd(
        dh2, z1, yss, gate, conv_ln_g, conv_ln_b, wp_f, wv_f, wg_f, wo_f, "merge_bwd")
    gbig["w_out"] = _wgrad(m_b, dh2, "dw_out").reshape(NSH, D // NSH, D)

    def shard_cols(gm):
        return gm.reshape(gm.shape[0], NSH, -1).transpose(1, 0, 2)

    gbig["conv_proj"] = shard_cols(_wgrad(z3, dyc, "dw_proj"))
    gbig["ssm_w_v"] = shard_cols(_wgrad(yg, dsv, "dw_v"))
    gbig["ssm_w_g"] = shard_cols(_wgrad(yg, dsg, "dw_g"))
    dv, dgl, ddw, d_dw_b = _conv_bwd(dz1, vg, dw_pad, "conv_bwd")
    (duf, dbre, dbim, dcre, dcim, dlam, d_ssm_d), recv_c = _ssm_bwd(
        uf, dyss, bre_bd, bim_bd, cre_bd, cim_bd, lamp, ssm_d, "ssm_bwd", _scatter_chips_behind(pair_c))
    dh1, u_b, dproj, d_mix_norm = _mix_in_bwd(h1, mix_norm, dh2, dv, dgl, duf, dgate, w_in_f, "mix_in_bwd")
    gbig["w_in"] = _wgrad(u_b, dproj, "dw_in")
    pair_b = pair_sums(grp_b, "mix")

    d_bbr = _diag_blocks(dbre.reshape(NQ, 8, H, 8, P)).transpose(0, 2, 1)
    d_bbi = _diag_blocks(dbim.reshape(NQ, 8, H, 8, P)).transpose(0, 2, 1)
    d_c_re = _diag_blocks(dcre.reshape(NQ, 8, P, 8, H)).transpose(0, 2, 1)
    d_c_im = _diag_blocks(dcim.reshape(NQ, 8, P, 8, H)).transpose(0, 2, 1)
    d_lbr = dlam[:, 0, :].reshape(G, P)
    d_lbi = dlam[:, 1, :].reshape(G, P)
    d_lam_re, d_lam_im, d_log_dt, d_b_re, d_b_im = disc_vjp((d_lbr, d_lbi, d_bbr, d_bbi))

    sg = {"mix_norm": d_mix_norm, "b_gate": d_b_gate, "conv_dw": ddw[:KW], "conv_dw_b": d_dw_b,
          "conv_ln_g": d_ln_g, "conv_ln_b": d_ln_b, "ssm_lam_re": d_lam_re, "ssm_lam_im": d_lam_im,
          "ssm_log_dt": d_log_dt, "ssm_b_re": d_b_re, "ssm_b_im": d_b_im, "ssm_c_re": d_c_re, "ssm_c_im": d_c_im,
          "ssm_d": d_ssm_d, "ffn2_norm": d_ffn2_norm, "final_norm": d_final}
    late = ["meta_tokens", "ffn1_norm"]
    early = [n for n in small if n not in late]

    (dh0, da1, db1, s1, n1, d_ffn1_norm), got = _ffn_bwd(
        h0, ffn1_norm, dh1, a1, b1, gw["ffn1_w1"], gw["ffn1_w3"], gw["ffn1_w2"], "ffn1_bwd",
        _join(_scatter_chips_behind(pair_b), _gather_all_behind(_pack([sg[n] for n in early]))))
    recv_b, early_all = got[:len(grp_b)], got[len(grp_b)]
    gbig["ffn1_w1"] = _wgrad(da1, n1, "ffn1_dw1")
    gbig["ffn1_w3"] = _wgrad(db1, n1, "ffn1_dw3")
    gbig["ffn1_w2"] = _wgrad(s1, dh1, "ffn1_dw2", 0.5)
    grad_x = dh0[FRONT:][None]
    recv_a = _scatter_chips(pair_sums(grp_a, "ffn1"), "scatter_ffn1")
    sg["meta_tokens"] = dh0[FRONT - NMETA:FRONT]
    sg["ffn1_norm"] = d_ffn1_norm

    recv = dict(zip(grp_a + grp_b + grp_c, list(recv_a) + list(recv_b) + list(recv_c)))
    halves = [_sum_slots(recv[n], "sum_" + n) for n in big]
    full = _swap_halves(halves, "swap_halves")
    out_g, out_d, out_m, out_v = {}, {}, {}, {}
    for n, f in zip(big, full):
        shp = args[n].shape
        g2 = f.reshape(f.shape[0] * f.shape[1], f.shape[2])
        if n in tposed:
            g2 = g2.T
        d2, m2, v2 = _adamw(args[n].reshape(g2.shape), g2, args["m_" + n].reshape(g2.shape),
                            args["v_" + n].reshape(g2.shape), "adamw_" + n)
        out_g[n], out_d[n], out_m[n], out_v[n] = (t.reshape(shp) for t in (g2, d2, m2, v2))

    late_all = _gather_all(_pack([sg[n] for n in late]), "gather_late_grads")
    sgr = dict(zip(early, _unpack(_sum_slots(early_all, "sum_early"), [sg[n].shape for n in early])))
    sgr.update(zip(late, _unpack(_sum_slots(late_all, "sum_late"), [sg[n].shape for n in late])))
    sgr["meta_tokens"] = lax.dynamic_slice_in_dim(sgr["meta_tokens"], chip * (D // NSH), D // NSH, axis=1)
    sgr["conv_dw"] = lax.dynamic_slice_in_dim(sgr["conv_dw"], chip * (DC // NSH), DC // NSH, axis=1)
    pshapes = [args[n].shape for n in small]
    d_s, m_s, v_s = _adamw(_pack([args[n] for n in small]), _pack([sgr[n] for n in small]),
                           _pack([args["m_" + n] for n in small]), _pack([args["v_" + n] for n in small]),
                           "adamw_small")
    for n, g_, d_, m_, v_ in zip(small, [sgr[n] for n in small], _unpack(d_s, pshapes),
                                 _unpack(m_s, pshapes), _unpack(v_s, pshapes)):
        out_g[n], out_d[n], out_m[n], out_v[n] = g_.reshape(args[n].shape), d_, m_, v_

    loss = lax.psum(loss_part[0, 0], ("x", "y", "c"))
    return (loss, grad_x, *[out_g[n] for n in names], *[out_d[n] for n in names],
            *[out_m[n] for n in names], *[out_v[n] for n in names])
```

```python
import math

import jax
import jax.numpy as jnp
from jax import lax
from jax.experimental import pallas as pl
from jax.experimental.pallas import tpu as pltpu

F32 = jnp.float32
BF16 = jnp.bfloat16

D = 1024
NSH = 4
F = 2816
FS = F // NSH
DC = 512
DS = 512
DIN = 2 * DC + DS + 2 * D
WS = DIN // NSH
KW = 31
KWP = 32
NMETA = 16
FRONT = 128
G, P, H = 32, 64, 16
NST = G * P
NQ = 4
QS = NST // NQ
QU = DS // NQ
NSEG = 32
NGRP = NSEG // 8
NCH = 8
SOFF = 8
EPS = 1e-6
LR, B1, B2, AEPS, WD, STEP = 1e-3, 0.9, 0.999, 1e-8, 0.01, 10
VMEM_LIMIT = 58 * 1024 * 1024
MESH = pl.DeviceIdType.MESH
ANY = pl.BlockSpec(memory_space=pl.ANY)


def _params(*sem):
    return pltpu.CompilerParams(dimension_semantics=sem, vmem_limit_bytes=VMEM_LIMIT)


def _res(shape):
    nd = len(shape)
    return pl.BlockSpec(shape, lambda *_: (0,) * nd, pipeline_mode=pl.Buffered(1))


def _tile(n, cap, mult=16):
    best = None
    for t in range(mult, min(n, cap) + 1, mult):
        if n % t == 0:
            best = t
    assert best is not None, (n, cap, mult)
    return best


def _dot(a, b):
    return jnp.dot(a, b, preferred_element_type=F32)


def _dot_nt(a, b):
    return lax.dot_general(a, b, (((1,), (1,)), ((), ())), preferred_element_type=F32)


def _dot_tn(a, b):
    return lax.dot_general(a, b, (((0,), (0,)), ((), ())), preferred_element_type=F32)


def _sigmoid(x):
    return 1.0 / (1.0 + jnp.exp(-x))


_GC = math.sqrt(2.0 / math.pi)
_GA = 0.044715


def _gelu(x):
    return 0.5 * x * (1.0 + jnp.tanh(_GC * (x + _GA * x * x * x)))


def _gelu_grad(x):
    t = jnp.tanh(_GC * (x + _GA * x * x * x))
    return 0.5 * (1.0 + t) + 0.5 * x * (1.0 - t * t) * _GC * (1.0 + 3.0 * _GA * x * x)


def _rms(hv, g):
    r = lax.rsqrt(jnp.mean(hv * hv, axis=-1, keepdims=True) + EPS)
    return hv * r * g, r


def _rms_bwd(dn, hv, r, g):
    xh = hv * r
    dxh = dn * g
    return r * (dxh - xh * jnp.mean(dxh * xh, axis=-1, keepdims=True)), xh


def _acc_rows(ref, part, first):
    @pl.when(first)
    def _():
        ref[...] = part

    @pl.when(jnp.logical_not(first))
    def _():
        ref[...] += part


def _coords():
    return lax.axis_index("x"), lax.axis_index("y"), lax.axis_index("c")


def _flip(v, d):
    return 1 - v if d else v


def _run(local, remote):
    for cp in local + remote:
        cp.start()
    for cp in remote:
        cp.wait()
    for cp in local:
        cp.wait()


def _via_vmem(src, dst, stage, sems, i):
    return (pltpu.make_async_copy(src, stage, sems.at[2 * i]), pltpu.make_async_copy(stage, dst, sems.at[2 * i + 1]))


def _run_staged(staged, remote):
    for load, _ in staged:
        load.start()
    for cp in remote:
        cp.start()
    for load, store in staged:
        load.wait()
        store.start()
    for cp in remote:
        cp.wait()
    for _, store in staged:
        store.wait()


_REL3 = ((1, 0), (0, 1), (1, 1))


class _Behind:
    def __init__(self, arrays, out_shapes, scratch, build):
        self.arrays, self.out_shapes, self.scratch, self.build = list(arrays), list(out_shapes), list(scratch), build

    def start(self, ins, outs, scr):
        staged, remote = self.build(ins, outs, scr)
        for load, _ in staged:
            load.start()
        for cp in remote:
            cp.start()

    def finish(self, ins, outs, scr):
        staged, remote = self.build(ins, outs, scr)
        for load, store in staged:
            load.wait()
            store.start()
        for cp in remote:
            cp.wait()
        for _, store in staged:
            store.wait()


def _call(body, comm, *, name, grid, in_specs, out_specs, out_shape, scratch_shapes=(), params):
    in_specs, out_specs, out_shape = list(in_specs), list(out_specs), list(out_shape)
    scratch_shapes = list(scratch_shapes)
    if comm is None:
        f = pl.pallas_call(body, name=name, grid=grid, in_specs=in_specs, out_specs=out_specs,
                           out_shape=out_shape, scratch_shapes=scratch_shapes, compiler_params=params)
        return lambda *args: (f(*args), [])
    ni, no, ns = len(in_specs), len(out_specs), len(scratch_shapes)
    ci, co = len(comm.arrays), len(comm.out_shapes)

    def hosted(*refs):
        ins, cin = refs[:ni], refs[ni:ni + ci]
        outs, cout = refs[ni + ci:ni + ci + no], refs[ni + ci + no:ni + ci + no + co]
        scr, cscr = refs[ni + ci + no + co:ni + ci + no + co + ns], refs[ni + ci + no + co + ns:]
        first = last = None
        for axis, size in enumerate(grid):
            i = pl.program_id(axis)
            first = (i == 0) if first is None else jnp.logical_and(first, i == 0)
            last = (i == size - 1) if last is None else jnp.logical_and(last, i == size - 1)

        @pl.when(first)
        def _():
            comm.start(cin, cout, cscr)

        body(*ins, *outs, *scr)

        @pl.when(last)
        def _():
            comm.finish(cin, cout, cscr)

    f = pl.pallas_call(hosted, name=name, grid=grid, in_specs=in_specs + [ANY] * ci,
                       out_specs=out_specs + [ANY] * co, out_shape=out_shape + comm.out_shapes,
                       scratch_shapes=scratch_shapes + comm.scratch,
                       compiler_params=_params(*(("arbitrary",) * len(grid))))

    def run(*args):
        res = f(*args, *comm.arrays)
        return res[:no], res[no:]

    return run


def _gather_half_behind(shards):
    n = len(shards)

    def build(ins, outs, scr):
        send, recv, loc = scr[:3]
        stage = scr[3:]
        x, y, c = _coords()
        me = 2 * x + y
        staged = [_via_vmem(ins[t], outs[t].at[me], stage[t], loc, t) for t in range(n)]
        remote = []
        for t in range(n):
            half = shards[t].shape[0] // 2
            mine = pl.ds(c * half, half)
            for k, (dx, dy) in enumerate(_REL3):
                remote.append(pltpu.make_async_remote_copy(
                    src_ref=ins[t].at[mine], dst_ref=outs[t].at[me, mine],
                    send_sem=send.at[3 * t + k], recv_sem=recv.at[3 * t + k],
                    device_id=(_flip(x, dx), _flip(y, dy), c), device_id_type=MESH))
        return staged, remote

    return _Behind(shards, [jax.ShapeDtypeStruct((NSH,) + s.shape, s.dtype) for s in shards],
                   [pltpu.SemaphoreType.DMA((3 * n,)), pltpu.SemaphoreType.DMA((3 * n,)),
                    pltpu.SemaphoreType.DMA((2 * n,))] + [pltpu.VMEM(s.shape, s.dtype) for s in shards], build)


def _pass_halves(gathered, name):
    n = len(gathered)

    def body(*refs):
        outs = refs[n:2 * n]
        send, recv = refs[2 * n:]
        x, y, c = _coords()
        remote = []
        for t in range(n):
            half = gathered[t].shape[1] // 2
            mine = pl.ds(c * half, half)
            for k, (dx, dy) in enumerate(_REL3):
                slot = 2 * _flip(x, dx) + _flip(y, dy)
                remote.append(pltpu.make_async_remote_copy(
                    src_ref=outs[t].at[slot, mine], dst_ref=outs[t].at[slot, mine],
                    send_sem=send.at[3 * t + k], recv_sem=recv.at[3 * t + k],
                    device_id=(x, y, 1 - c), device_id_type=MESH))
        _run([], remote)

    return pl.pallas_call(
        body, name=name,
        out_shape=[jax.ShapeDtypeStruct(g.shape, g.dtype) for g in gathered],
        in_specs=[ANY] * n, out_specs=[ANY] * n, input_output_aliases={t: t for t in range(n)},
        scratch_shapes=[pltpu.SemaphoreType.DMA((3 * n,)), pltpu.SemaphoreType.DMA((3 * n,))],
    )(*gathered)


def _scatter_chips_behind(sums):
    n = len(sums)

    def build(ins, outs, scr):
        send, recv, loc = scr[:3]
        stage = scr[3:]
        x, y, c = _coords()
        me = 2 * x + y
        staged = [_via_vmem(ins[t].at[me], outs[t].at[me], stage[t], loc, t) for t in range(n)]
        remote = []
        for t in range(n):
            for k, (dx, dy) in enumerate(_REL3):
                px, py = _flip(x, dx), _flip(y, dy)
                remote.append(pltpu.make_async_remote_copy(
                    src_ref=ins[t].at[2 * px + py], dst_ref=outs[t].at[me],
                    send_sem=send.at[3 * t + k], recv_sem=recv.at[3 * t + k],
                    device_id=(px, py, c), device_id_type=MESH))
        return staged, remote

    return _Behind(sums, [jax.ShapeDtypeStruct(s.shape, s.dtype) for s in sums],
                   [pltpu.SemaphoreType.DMA((3 * n,)), pltpu.SemaphoreType.DMA((3 * n,)),
                    pltpu.SemaphoreType.DMA((2 * n,))] + [pltpu.VMEM(s.shape[1:], s.dtype) for s in sums], build)


def _gather_all_behind(a):
    def build(ins, outs, scr):
        send, recv, loc, stage = scr
        x, y, c = _coords()
        me = 4 * x + 2 * y + c
        staged = [_via_vmem(ins[0], outs[0].at[me], stage, loc, 0)]
        remote = [pltpu.make_async_remote_copy(
            src_ref=ins[0], dst_ref=outs[0].at[me], send_sem=send.at[k], recv_sem=recv.at[k],
            device_id=(_flip(x, dx), _flip(y, dy), _flip(c, dc)), device_id_type=MESH)
            for k, (dx, dy, dc) in enumerate(_REL7)]
        return staged, remote

    return _Behind([a], [jax.ShapeDtypeStruct((8,) + a.shape, a.dtype)],
                   [pltpu.SemaphoreType.DMA((7,)), pltpu.SemaphoreType.DMA((7,)), pltpu.SemaphoreType.DMA((2,)),
                    pltpu.VMEM(a.shape, a.dtype)], build)


def _join(*parts):
    def cut(seq, key):
        res, o = [], 0
        for p in parts:
            k = len(getattr(p, key))
            res.append(seq[o:o + k])
            o += k
        return res

    def build(ins, outs, scr):
        staged, remote = [], []
        for p, i, o, s in zip(parts, cut(ins, "arrays"), cut(outs, "out_shapes"), cut(scr, "scratch")):
            st, rm = p.build(i, o, s)
            staged += st
            remote += rm
        return staged, remote

    return _Behind(sum((p.arrays for p in parts), []), sum((p.out_shapes for p in parts), []),
                   sum((p.scratch for p in parts), []), build)


def _gather_chips(shards, name):
    n = len(shards)

    def body(*refs):
        ins, outs = refs[:n], refs[n:2 * n]
        send, recv, fsend, frecv, loc = refs[2 * n:2 * n + 5]
        stage = refs[2 * n + 5:]
        x, y, c = _coords()
        me = 2 * x + y
        own = [_via_vmem(ins[t], outs[t].at[me], stage[t], loc, t) for t in range(n)]
        first, passed = [], []
        for t in range(n):
            half = shards[t].shape[0] // 2
            mine, theirs = pl.ds(c * half, half), pl.ds((1 - c) * half, half)
            for k, (dx, dy) in enumerate(_REL3):
                px, py = _flip(x, dx), _flip(y, dy)
                first.append(pltpu.make_async_remote_copy(
                    src_ref=ins[t].at[mine], dst_ref=outs[t].at[me, mine],
                    send_sem=send.at[3 * t + k], recv_sem=recv.at[3 * t + k],
                    device_id=(px, py, c), device_id_type=MESH))
                passed.append((
                    pltpu.make_async_remote_copy(
                        src_ref=outs[t].at[2 * px + py, mine], dst_ref=outs[t].at[2 * px + py, mine],
                        send_sem=fsend.at[3 * t + k], recv_sem=frecv.at[3 * t + k],
                        device_id=(x, y, 1 - c), device_id_type=MESH),
                    pltpu.make_async_remote_copy(
                        src_ref=outs[t].at[2 * px + py, theirs], dst_ref=outs[t].at[2 * px + py, theirs],
                        send_sem=fsend.at[3 * t + k], recv_sem=frecv.at[3 * t + k],
                        device_id=(x, y, 1 - c), device_id_type=MESH)))
        for load, _ in own:
            load.start()
        for cp in first:
            cp.start()
        for load, store in own:
            load.wait()
            store.start()
        for cp, (fwd, _) in zip(first, passed):
            cp.wait_recv()
            fwd.start()
        for cp, (fwd, back) in zip(first, passed):
            cp.wait_send()
            fwd.wait_send()
            back.wait_recv()
        for _, store in own:
            store.wait()

    return pl.pallas_call(
        body, name=name,
        out_shape=[jax.ShapeDtypeStruct((NSH,) + s.shape, s.dtype) for s in shards],
        in_specs=[ANY] * n, out_specs=[ANY] * n,
        scratch_shapes=[pltpu.SemaphoreType.DMA((3 * n,)) for _ in range(4)] + [pltpu.SemaphoreType.DMA((2 * n,))]
        + [pltpu.VMEM(s.shape, s.dtype) for s in shards],
        compiler_params=pltpu.CompilerParams(vmem_limit_bytes=VMEM_LIMIT),
    )(*shards)


_REL7 = tuple((dx, dy, dc) for dx in (0, 1) for dy in (0, 1) for dc in (0, 1))[1:]


def _gather_all(a, name):
    def body(a_ref, o_ref, send, recv, loc):
        x, y, c = _coords()
        me = 4 * x + 2 * y + c
        local = [pltpu.make_async_copy(a_ref, o_ref.at[me], loc.at[0])]
        remote = [pltpu.make_async_remote_copy(
            src_ref=a_ref, dst_ref=o_ref.at[me], send_sem=send.at[k], recv_sem=recv.at[k],
            device_id=(_flip(x, dx), _flip(y, dy), _flip(c, dc)), device_id_type=MESH)
            for k, (dx, dy, dc) in enumerate(_REL7)]
        _run(local, remote)

    return pl.pallas_call(
        body, name=name,
        out_shape=jax.ShapeDtypeStruct((8,) + a.shape, a.dtype),
        in_specs=[ANY], out_specs=ANY,
        scratch_shapes=[pltpu.SemaphoreType.DMA((7,)), pltpu.SemaphoreType.DMA((7,)),
                        pltpu.SemaphoreType.DMA((1,))],
    )(a)


def _pair_exchange(grads, name):
    n = len(grads)

    def body(*refs):
        ins, outs = refs[:n], refs[n:2 * n]
        send, recv = refs[2 * n:]
        x, y, c = _coords()
        remote = []
        for t in range(n):
            half = grads[t].shape[1] // 2
            remote.append(pltpu.make_async_remote_copy(
                src_ref=ins[t].at[:, pl.ds((1 - c) * half, half)], dst_ref=outs[t],
                send_sem=send.at[t], recv_sem=recv.at[t],
                device_id=(x, y, 1 - c), device_id_type=MESH))
        _run([], remote)

    return pl.pallas_call(
        body, name=name,
        out_shape=[jax.ShapeDtypeStruct((NSH, g.shape[1] // 2, g.shape[2]), g.dtype) for g in grads],
        in_specs=[ANY] * n, out_specs=[ANY] * n,
        scratch_shapes=[pltpu.SemaphoreType.DMA((n,)), pltpu.SemaphoreType.DMA((n,))],
    )(*grads)


def _scatter_chips(sums, name):
    n = len(sums)

    def body(*refs):
        ins, outs = refs[:n], refs[n:2 * n]
        send, recv, loc = refs[2 * n:2 * n + 3]
        stage = refs[2 * n + 3:]
        x, y, c = _coords()
        me = 2 * x + y
        local = [_via_vmem(ins[t].at[me], outs[t].at[me], stage[t], loc, t) for t in range(n)]
        remote = []
        for t in range(n):
            for k, (dx, dy) in enumerate(_REL3):
                px, py = _flip(x, dx), _flip(y, dy)
                remote.append(pltpu.make_async_remote_copy(
                    src_ref=ins[t].at[2 * px + py], dst_ref=outs[t].at[me],
                    send_sem=send.at[3 * t + k], recv_sem=recv.at[3 * t + k],
                    device_id=(px, py, c), device_id_type=MESH))
        _run_staged(local, remote)

    return pl.pallas_call(
        body, name=name,
        out_shape=[jax.ShapeDtypeStruct(s.shape, s.dtype) for s in sums],
        in_specs=[ANY] * n, out_specs=[ANY] * n,
        scratch_shapes=[pltpu.SemaphoreType.DMA((3 * n,)), pltpu.SemaphoreType.DMA((3 * n,)),
                        pltpu.SemaphoreType.DMA((2 * n,))]
        + [pltpu.VMEM(s.shape[1:], s.dtype) for s in sums],
        compiler_params=pltpu.CompilerParams(vmem_limit_bytes=VMEM_LIMIT),
    )(*sums)


def _swap_halves(halves, name):
    n = len(halves)

    def body(*refs):
        ins, outs = refs[:n], refs[n:2 * n]
        send, recv, loc = refs[2 * n:2 * n + 3]
        stage = refs[2 * n + 3:]
        x, y, c = _coords()
        local = [_via_vmem(ins[t], outs[t].at[c], stage[t], loc, t) for t in range(n)]
        remote = [pltpu.make_async_remote_copy(
            src_ref=ins[t], dst_ref=outs[t].at[c], send_sem=send.at[t], recv_sem=recv.at[t],
            device_id=(x, y, 1 - c), device_id_type=MESH) for t in range(n)]
        _run_staged(local, remote)

    return pl.pallas_call(
        body, name=name,
        out_shape=[jax.ShapeDtypeStruct((2,) + h.shape, h.dtype) for h in halves],
        in_specs=[ANY] * n, out_specs=[ANY] * n,
        scratch_shapes=[pltpu.SemaphoreType.DMA((n,)), pltpu.SemaphoreType.DMA((n,)),
                        pltpu.SemaphoreType.DMA((2 * n,))]
        + [pltpu.VMEM(h.shape, h.dtype) for h in halves],
        compiler_params=pltpu.CompilerParams(vmem_limit_bytes=VMEM_LIMIT),
    )(*halves)


def _sum_slots(r, name):
    K, R, C = r.shape
    tr = _tile(R, max(16, (1 << 22) // (K * C)), 8 * (4 // r.dtype.itemsize))

    def body(r_ref, o_ref):
        acc = r_ref[0].astype(F32)
        for k in range(1, K):
            acc = acc + r_ref[k].astype(F32)
        o_ref[...] = acc

    return pl.pallas_call(
        body, name=name, grid=(R // tr,),
        out_shape=jax.ShapeDtypeStruct((R, C), F32),
        in_specs=[pl.BlockSpec((K, tr, C), lambda i: (0, i, 0))],
        out_specs=pl.BlockSpec((tr, C), lambda i: (i, 0)),
        compiler_params=_params("parallel"),
    )(r)


def _add_pair(g, s, core, name):
    _, half, C = s.shape
    tr = _tile(half, max(16, (1 << 19) // C))
    nb = half // tr

    def body(c_ref, g_ref, s_ref, o_ref):
        o_ref[...] = (g_ref[...].astype(F32) + s_ref[...].astype(F32)).astype(BF16)

    spec = pl.BlockSpec((1, tr, C), lambda j, i, c_ref: (j, i, 0))
    return pl.pallas_call(
        body, name=name,
        grid_spec=pltpu.PrefetchScalarGridSpec(
            num_scalar_prefetch=1, grid=(NSH, nb),
            in_specs=[pl.BlockSpec((1, tr, C), lambda j, i, c_ref: (j, c_ref[0] * nb + i, 0)), spec],
            out_specs=spec),
        out_shape=jax.ShapeDtypeStruct(s.shape, BF16),
        compiler_params=_params("parallel", "parallel"),
    )(core, g, s)


def _adamw(w, g, m, v, name):
    _, R, C = w.shape
    tr = _tile(R, max(8, (1 << 18) // C), 8)
    c1 = 1.0 / (1.0 - B1 ** STEP)
    c2 = 1.0 / (1.0 - B2 ** STEP)

    def body(w_ref, g_ref, m_ref, v_ref, d_ref, nm_ref, nv_ref):
        gv = g_ref[...]
        nm = B1 * m_ref[...] + (1.0 - B1) * gv
        nv = B2 * v_ref[...] + (1.0 - B2) * gv * gv
        nm_ref[...] = nm
        nv_ref[...] = nv
        d_ref[...] = -LR * ((nm * c1) / (jnp.sqrt(nv * c2) + AEPS) + WD * w_ref[...])

    spec = pl.BlockSpec((1, tr, C), lambda i: (0, i, 0))
    return pl.pallas_call(
        body, name=name, grid=(R // tr,),
        out_shape=[jax.ShapeDtypeStruct((1, R, C), F32)] * 3,
        in_specs=[spec] * 4, out_specs=[spec] * 3,
        compiler_params=_params("parallel"),
    )(w, g, m, v)


def _ffn_fwd(h, g, w1, w3, w2, name, comm=None):
    L = h.shape[0]
    tm = _tile(L, 528)

    def body(h_ref, g_ref, w1_ref, w3_ref, w2_ref, o_ref, a_ref, b_ref, n_s, acc_s):
        j = pl.program_id(1)

        @pl.when(j == 0)
        def _():
            hv = h_ref[...]
            n, _ = _rms(hv, g_ref[...])
            n_s[...] = n.astype(BF16)
            acc_s[...] = hv

        n = n_s[...]
        a = _dot_nt(n, w1_ref[j])
        b = _dot_nt(n, w3_ref[j])
        a_ref[0] = a.astype(BF16)
        b_ref[0] = b.astype(BF16)
        s = (a * _sigmoid(a) * b).astype(BF16)
        acc_s[...] += 0.5 * _dot(s, w2_ref[j])

        @pl.when(j == NSH - 1)
        def _():
            o_ref[...] = acc_s[...]

    row = pl.BlockSpec((tm, D), lambda i, j: (i, 0))
    hid = pl.BlockSpec((1, tm, FS), lambda i, j: (j, i, 0))
    return _call(
        body, comm, name=name, grid=(L // tm, NSH),
        out_shape=[jax.ShapeDtypeStruct((L, D), F32),
                   jax.ShapeDtypeStruct((NSH, L, FS), BF16), jax.ShapeDtypeStruct((NSH, L, FS), BF16)],
        in_specs=[row, _res((1, D)), _res((NSH, FS, D)), _res((NSH, FS, D)), _res((NSH, FS, D))],
        out_specs=[row, hid, hid],
        scratch_shapes=[pltpu.VMEM((tm, D), BF16), pltpu.VMEM((tm, D), F32)],
        params=_params("arbitrary", "arbitrary"),
    )(h, g, w1, w3, w2)


def _ffn_bwd(h, g, dout, a, b, w1, w3, w2, name, comm=None):
    L = h.shape[0]
    tm = _tile(L, 352)

    def body(h_ref, g_ref, do_ref, a_ref, b_ref, w1_ref, w3_ref, w2_ref,
             dh_ref, da_ref, db_ref, s_ref, n_ref, dg_ref, dob_s, dn_s):
        i, j = pl.program_id(0), pl.program_id(1)

        @pl.when(j == 0)
        def _():
            n, _ = _rms(h_ref[...], g_ref[...])
            n_ref[...] = n.astype(BF16)
            dob_s[...] = do_ref[...].astype(BF16)
            dn_s[...] = jnp.zeros_like(dn_s)

        av = a_ref[0].astype(F32)
        bv = b_ref[0].astype(F32)
        sig = _sigmoid(av)
        sa = av * sig
        ds = 0.5 * _dot_nt(dob_s[...], w2_ref[j])
        s_ref[0] = (sa * bv).astype(BF16)
        da = (ds * bv * sig * (1.0 + av * (1.0 - sig))).astype(BF16)
        db = (ds * sa).astype(BF16)
        da_ref[0] = da
        db_ref[0] = db
        dn_s[...] += _dot(da, w1_ref[j]) + _dot(db, w3_ref[j])

        @pl.when(j == NSH - 1)
        def _():
            hv = h_ref[...]
            gv = g_ref[...]
            r = lax.rsqrt(jnp.mean(hv * hv, axis=-1, keepdims=True) + EPS)
            dn = dn_s[...]
            dx, xh = _rms_bwd(dn, hv, r, gv)
            dh_ref[...] = do_ref[...] + dx
            _acc_rows(dg_ref, jnp.sum(dn * xh, axis=0, keepdims=True), i == 0)

    row = pl.BlockSpec((tm, D), lambda i, j: (i, 0))
    hid = pl.BlockSpec((1, tm, FS), lambda i, j: (j, i, 0))
    return _call(
        body, comm, name=name, grid=(L // tm, NSH),
        out_shape=[jax.ShapeDtypeStruct((L, D), F32)]
        + [jax.ShapeDtypeStruct((NSH, L, FS), BF16)] * 3
        + [jax.ShapeDtypeStruct((L, D), BF16), jax.ShapeDtypeStruct((1, D), F32)],
        in_specs=[row, _res((1, D)), row, hid, hid,
                  _res((NSH, FS, D)), _res((NSH, FS, D)), _res((NSH, FS, D))],
        out_specs=[row, hid, hid, hid, row, pl.BlockSpec((1, D), lambda i, j: (0, 0))],
        scratch_shapes=[pltpu.VMEM((tm, D), BF16), pltpu.VMEM((tm, D), F32)],
        params=_params("arbitrary", "arbitrary"),
    )(h, g, dout, a, b, w1, w3, w2)


def _wgrad(xm, ym, name, scale=1.0):
    xs, ys = xm.ndim == 3, ym.ndim == 3
    assert not (xs and ys)
    L = xm.shape[-2]
    K, N = xm.shape[-1], ym.shape[-1]
    tl = _tile(L, 1056)
    nl = L // tl
    if xs or ys:
        tn, grid_n = N, NSH
    else:
        tn = _tile(N, 1024, 128)
        grid_n = N // tn

    def body(x_ref, y_ref, o_ref, acc_s):
        l = pl.program_id(1)
        xv = x_ref[0] if xs else x_ref[...]
        yv = y_ref[0] if ys else y_ref[...]
        part = _dot_tn(xv.astype(BF16), yv.astype(BF16))
        _acc_rows(acc_s, part, l == 0)

        @pl.when(l == nl - 1)
        def _():
            res = (acc_s[...] * scale).astype(BF16)
            if xs or ys:
                o_ref[0] = res
            else:
                o_ref[...] = res

    if xs:
        x_spec = pl.BlockSpec((1, tl, K), lambda n, l: (n, l, 0))
        y_spec = pl.BlockSpec((tl, N), lambda n, l: (l, 0))
        o_spec = pl.BlockSpec((1, K, N), lambda n, l: (n, 0, 0))
        o_shape = (NSH, K, N)
    elif ys:
        x_spec = pl.BlockSpec((tl, K), lambda n, l: (l, 0))
        y_spec = pl.BlockSpec((1, tl, N), lambda n, l: (n, l, 0))
        o_spec = pl.BlockSpec((1, K, N), lambda n, l: (n, 0, 0))
        o_shape = (NSH, K, N)
    else:
        x_spec = pl.BlockSpec((tl, K), lambda n, l: (l, 0))
        y_spec = pl.BlockSpec((tl, tn), lambda n, l: (l, n))
        o_spec = pl.BlockSpec((K, tn), lambda n, l: (0, n))
        o_shape = (K, N)
    return pl.pallas_call(
        body, name=name, grid=(grid_n, nl),
        out_shape=jax.ShapeDtypeStruct(o_shape, BF16),
        in_specs=[x_spec, y_spec], out_specs=o_spec,
        scratch_shapes=[pltpu.VMEM((K, tn), F32)],
        compiler_params=_params("parallel", "arbitrary"),
    )(xm, ym)


def _mix_in_fwd(h, g, w_in, b_gate, name, comm=None):
    L = h.shape[0]
    tm = _tile(L, 528)

    def body(h_ref, g_ref, w_ref, bg_ref, vg_ref, uf_ref, gt_ref):
        u, _ = _rms(h_ref[...], g_ref[...])
        ub = u.astype(BF16)
        p = [_dot(ub, w_ref[j]) for j in range(NSH)]
        a0, a1 = 2 * DC - WS, 2 * DC + DS - WS
        vg_ref[:, 0:WS] = p[0].astype(BF16)
        vg_ref[:, WS:2 * DC] = p[1][:, 0:a0].astype(BF16)
        uf_ref[...] = p[1][:, a0:a1].astype(BF16)
        gin = jnp.concatenate([p[1][:, a1:], p[2], p[3]], axis=1)
        gt_ref[...] = _sigmoid(gin + bg_ref[...]).astype(BF16)

    def row(n):
        return pl.BlockSpec((tm, n), lambda i: (i, 0))

    return _call(
        body, comm, name=name, grid=(L // tm,),
        out_shape=[jax.ShapeDtypeStruct((L, 2 * DC), BF16), jax.ShapeDtypeStruct((L, DS), BF16),
                   jax.ShapeDtypeStruct((L, 2 * D), BF16)],
        in_specs=[row(D), _res((1, D)), _res((NSH, D, WS)), _res((1, 2 * D))],
        out_specs=[row(2 * DC), row(DS), row(2 * D)],
        params=_params("parallel"),
    )(h, g, w_in, b_gate)


def _mix_in_bwd(h, g, dres, dv, dgl, duf, dgate, w_in, name):
    L = h.shape[0]
    tm = _tile(L, 528)

    def body(h_ref, g_ref, dr_ref, dv_ref, dgl_ref, duf_ref, dgt_ref, w_ref, dh_ref, u_ref, dp_ref, dgm_ref):
        i = pl.program_id(0)
        hv = h_ref[...]
        gv = g_ref[...]
        u, r = _rms(hv, gv)
        u_ref[...] = u.astype(BF16)
        a0, a1 = 2 * DC - WS, 2 * DC + DS - WS
        b0 = WS - a1
        dp = [jnp.concatenate([dv_ref[...], dgl_ref[:, 0:WS - DC]], axis=1),
              jnp.concatenate([dgl_ref[:, WS - DC:], duf_ref[...], dgt_ref[:, 0:b0]], axis=1),
              dgt_ref[:, b0:b0 + WS], dgt_ref[:, b0 + WS:]]
        du = jnp.zeros((tm, D), F32)
        for j in range(NSH):
            dp_ref[j] = dp[j]
            du = du + _dot_nt(dp[j], w_ref[j])
        dx, xh = _rms_bwd(du, hv, r, gv)
        dh_ref[...] = dr_ref[...] + dx
        _acc_rows(dgm_ref, jnp.sum(du * xh, axis=0, keepdims=True), i == 0)

    def row(n):
        return pl.BlockSpec((tm, n), lambda i: (i, 0))

    return pl.pallas_call(
        body, name=name, grid=(L // tm,),
        out_shape=[jax.ShapeDtypeStruct((L, D), F32), jax.ShapeDtypeStruct((L, D), BF16),
                   jax.ShapeDtypeStruct((NSH, L, WS), BF16), jax.ShapeDtypeStruct((1, D), F32)],
        in_specs=[row(D), _res((1, D)), row(D), row(DC), row(DC), row(DS), row(2 * D), _res((NSH, D, WS))],
        out_specs=[row(D), row(D), pl.BlockSpec((NSH, tm, WS), lambda i: (0, i, 0)),
                   pl.BlockSpec((1, D), lambda i: (0, 0))],
        compiler_params=_params("arbitrary"),
    )(h, g, dres, dv, dgl, duf, dgate, w_in)


def _conv_fwd(vg, dw, dwb, name, comm=None):
    L = vg.shape[0]
    nc = DC // 128

    def body(v_ref, g_ref, dw_ref, dwb_ref, z_ref, zp_s):
        zp_s[0:KWP, :] = jnp.zeros((KWP, 128), F32)
        zp_s[KWP:, :] = v_ref[...].astype(F32) * _sigmoid(g_ref[...].astype(F32))
        acc = jnp.broadcast_to(dwb_ref[...], (L, 128))
        for k in range(KW):
            acc = acc + dw_ref[k:k + 1, :] * zp_s[pl.ds(k + 2, L), :]
        z_ref[...] = acc

    return _call(
        body, comm, name=name, grid=(nc,),
        out_shape=[jax.ShapeDtypeStruct((L, DC), F32)],
        in_specs=[pl.BlockSpec((L, 128), lambda c: (0, c)), pl.BlockSpec((L, 128), lambda c: (0, nc + c)),
                  pl.BlockSpec((KWP, 128), lambda c: (0, c)), pl.BlockSpec((1, 128), lambda c: (0, c))],
        out_specs=[pl.BlockSpec((L, 128), lambda c: (0, c))],
        scratch_shapes=[pltpu.VMEM((L + KWP, 128), F32)],
        params=_params("parallel"),
    )(vg, vg, dw, dwb)


def _conv_bwd(dz1, vg, dw, name):
    L = vg.shape[0]
    nc = DC // 128

    def body(dz_ref, v_ref, g_ref, dw_ref, dv_ref, dg_ref, ddw_ref, ddwb_ref, zp_s, dzp_s):
        vv = v_ref[...].astype(F32)
        sg = _sigmoid(g_ref[...].astype(F32))
        zp_s[0:KWP, :] = jnp.zeros((KWP, 128), F32)
        zp_s[KWP:, :] = vv * sg
        dz = dz_ref[...]
        dzp_s[0:L, :] = dz
        dzp_s[L:, :] = jnp.zeros((KWP, 128), F32)
        ddwb_ref[...] = jnp.sum(dz, axis=0, keepdims=True)
        acc = jnp.zeros((L, 128), F32)
        for k in range(KW):
            acc = acc + dw_ref[k:k + 1, :] * dzp_s[pl.ds(KW - 1 - k, L), :]
            ddw_ref[k:k + 1, :] = jnp.sum(dz * zp_s[pl.ds(k + 2, L), :], axis=0, keepdims=True)
        ddw_ref[KW:KWP, :] = jnp.zeros((KWP - KW, 128), F32)
        dv_ref[...] = (acc * sg).astype(BF16)
        dg_ref[...] = (acc * vv * sg * (1.0 - sg)).astype(BF16)

    col = pl.BlockSpec((L, 128), lambda c: (0, c))
    return pl.pallas_call(
        body, name=name, grid=(nc,),
        out_shape=[jax.ShapeDtypeStruct((L, DC), BF16), jax.ShapeDtypeStruct((L, DC), BF16),
                   jax.ShapeDtypeStruct((KWP, DC), F32), jax.ShapeDtypeStruct((1, DC), F32)],
        in_specs=[col, col, pl.BlockSpec((L, 128), lambda c: (0, nc + c)),
                  pl.BlockSpec((KWP, 128), lambda c: (0, c))],
        out_specs=[col, col, pl.BlockSpec((KWP, 128), lambda c: (0, c)), pl.BlockSpec((1, 128), lambda c: (0, c))],
        scratch_shapes=[pltpu.VMEM((L + KWP, 128), F32), pltpu.VMEM((L + KWP, 128), F32)],
        compiler_params=_params("parallel"),
    )(dz1, vg, vg, dw)


NLB = QS // 128


def _lb_store(ref, rows, val):
    for cb in range(NLB):
        ref[cb, rows, :] = val[:, cb * 128:(cb + 1) * 128]


def _lb_load(ref, rows):
    return jnp.concatenate([ref[cb, rows, :] for cb in range(NLB)], axis=1)


def _scan(xr_ref, xi_ref, base, T, ar, ai, atr, ati, reverse):
    W = ar.shape[1]
    ar, ai, atr, ati = (jnp.broadcast_to(v, (8, W)) for v in (ar, ai, atr, ati))
    zero = jnp.zeros((8, W), F32)

    def rows(t, g):
        tt = T - 1 - t if reverse else t
        return pl.ds(base + g * 8 * T + tt, 8, stride=T)

    def make_step(store):
        def step(t, carry):
            out = []
            for g in range(NGRP):
                sr, si = carry[2 * g], carry[2 * g + 1]
                idx = rows(t, g)
                nr = ar * sr - ai * si + _lb_load(xr_ref, idx)
                ni = ar * si + ai * sr + _lb_load(xi_ref, idx)
                if store:
                    _lb_store(xr_ref, idx, nr)
                    _lb_store(xi_ref, idx, ni)
                out += [nr, ni]
            return tuple(out)
        return step

    ends = lax.fori_loop(0, T, make_step(False), (zero,) * (2 * NGRP))
    sub = lax.broadcasted_iota(jnp.int32, (8, W), 0)
    edge = sub == (7 if reverse else 0)
    shift, last = (7, 0) if reverse else (1, 7)
    inr, ini = jnp.zeros((1, W), F32), jnp.zeros((1, W), F32)
    starts = [None] * (2 * NGRP)
    for g in (reversed(range(NGRP)) if reverse else range(NGRP)):
        er, ei = ends[2 * g], ends[2 * g + 1]
        cr, ci = jnp.where(edge, inr, 0.0), jnp.where(edge, ini, 0.0)
        for _ in range(7):
            nr = atr * cr - ati * ci + er
            ni = atr * ci + ati * cr + ei
            cr = jnp.where(edge, inr, pltpu.roll(nr, shift, 0))
            ci = jnp.where(edge, ini, pltpu.roll(ni, shift, 0))
        starts[2 * g], starts[2 * g + 1] = cr, ci
        inr = (atr * cr - ati * ci + er)[last:last + 1]
        ini = (atr * ci + ati * cr + ei)[last:last + 1]
    lax.fori_loop(0, T, make_step(True), tuple(starts))


def _ssm_fwd(uf, bre, bim, cre, cim, lamp, dsk, name, comm=None):
    L = uf.shape[0]
    T = L // NSEG
    tc = L // NCH

    def body(u_ref, bre_ref, bim_ref, cre_ref, cim_ref, lam_ref, d_ref, y_ref, sr_s, si_s):
        for k in range(NCH):
            sl = slice(k * tc, (k + 1) * tc)
            uk = u_ref[sl, :]
            _lb_store(sr_s, sl, _dot(uk, bre_ref[...]))
            _lb_store(si_s, sl, _dot(uk, bim_ref[...]))
        _scan(sr_s, si_s, 0, T, lam_ref[0:1, :], lam_ref[1:2, :], lam_ref[2:3, :], lam_ref[3:4, :], False)
        for k in range(NCH):
            sl = slice(k * tc, (k + 1) * tc)
            y_ref[sl, :] = (_dot(_lb_load(sr_s, sl).astype(BF16), cre_ref[...])
                            - _dot(_lb_load(si_s, sl).astype(BF16), cim_ref[...])
                            + d_ref[...] * u_ref[sl, :].astype(F32))

    return _call(
        body, comm, name=name, grid=(NQ,),
        out_shape=[jax.ShapeDtypeStruct((L, DS), F32)],
        in_specs=[pl.BlockSpec((L, QU), lambda q: (0, q)),
                  pl.BlockSpec((QU, QS), lambda q: (q, q)), pl.BlockSpec((QU, QS), lambda q: (q, q)),
                  pl.BlockSpec((QS, QU), lambda q: (q, q)), pl.BlockSpec((QS, QU), lambda q: (q, q)),
                  pl.BlockSpec((8, QS), lambda q: (0, q)), pl.BlockSpec((1, QU), lambda q: (0, q))],
        out_specs=[pl.BlockSpec((L, QU), lambda q: (0, q))],
        scratch_shapes=[pltpu.VMEM((NLB, L, 128), F32), pltpu.VMEM((NLB, L, 128), F32)],
        params=_params("parallel"),
    )(uf, bre, bim, cre, cim, lamp, dsk)


def _ssm_bwd(uf, dyss, bre, bim, cre, cim, lamp, dsk, name, comm=None):
    L = uf.shape[0]
    T = L // NSEG
    tc = L // NCH

    def body(u_ref, dy_ref, bre_ref, bim_ref, cre_ref, cim_ref, lam_ref, d_ref,
             du_ref, dbre_ref, dbim_ref, dcre_ref, dcim_ref, dlam_ref, dd_ref, sr_s, si_s, gr_s, gi_s):
        _lb_store(sr_s, slice(0, SOFF), jnp.zeros((SOFF, QS), F32))
        _lb_store(si_s, slice(0, SOFF), jnp.zeros((SOFF, QS), F32))
        for k in range(NCH):
            sl = slice(k * tc, (k + 1) * tc)
            ss = slice(SOFF + k * tc, SOFF + (k + 1) * tc)
            uk = u_ref[sl, :]
            dyk = dy_ref[sl, :].astype(BF16)
            _lb_store(sr_s, ss, _dot(uk, bre_ref[...]))
            _lb_store(si_s, ss, _dot(uk, bim_ref[...]))
            _lb_store(gr_s, sl, _dot_nt(dyk, cre_ref[...]))
            _lb_store(gi_s, sl, -_dot_nt(dyk, cim_ref[...]))
        ar, ai, atr, ati = lam_ref[0:1, :], lam_ref[1:2, :], lam_ref[2:3, :], lam_ref[3:4, :]
        _scan(sr_s, si_s, SOFF, T, ar, ai, atr, ati, False)
        _scan(gr_s, gi_s, 0, T, ar, -ai, atr, -ati, True)
        dbre = jnp.zeros((QU, QS), F32)
        dbim = jnp.zeros((QU, QS), F32)
        dcre = jnp.zeros((QS, QU), F32)
        dcim = jnp.zeros((QS, QU), F32)
        dd = jnp.zeros((1, QU), F32)
        qr = jnp.zeros((1, QS), F32)
        qi = jnp.zeros((1, QS), F32)
        for k in range(NCH):
            sl = slice(k * tc, (k + 1) * tc)
            ss = slice(SOFF + k * tc, SOFF + (k + 1) * tc)
            sp = slice(SOFF - 1 + k * tc, SOFF - 1 + (k + 1) * tc)
            uk = u_ref[sl, :]
            dyk = dy_ref[sl, :]
            dyb = dyk.astype(BF16)
            gr, gi = _lb_load(gr_s, sl), _lb_load(gi_s, sl)
            pr, pi = _lb_load(sr_s, sp), _lb_load(si_s, sp)
            qr = qr + jnp.sum(gr * pr + gi * pi, axis=0, keepdims=True)
            qi = qi + jnp.sum(gi * pr - gr * pi, axis=0, keepdims=True)
            grb, gib = gr.astype(BF16), gi.astype(BF16)
            du_ref[sl, :] = (_dot_nt(grb, bre_ref[...]) + _dot_nt(gib, bim_ref[...])
                             + dyk * d_ref[...]).astype(BF16)
            dbre = dbre + _dot_tn(uk, grb)
            dbim = dbim + _dot_tn(uk, gib)
            dcre = dcre + _dot_tn(_lb_load(sr_s, ss).astype(BF16), dyb)
            dcim = dcim - _dot_tn(_lb_load(si_s, ss).astype(BF16), dyb)
            dd = dd + jnp.sum(dyk * uk.astype(F32), axis=0, keepdims=True)
        dlam_ref[0] = jnp.concatenate([qr, qi, jnp.zeros((6, QS), F32)], axis=0)
        dbre_ref[0] = dbre
        dbim_ref[0] = dbim
        dcre_ref[0] = dcre
        dcim_ref[0] = dcim
        dd_ref[...] = dd

    col = pl.BlockSpec((L, QU), lambda q: (0, q))
    bsp = pl.BlockSpec((QU, QS), lambda q: (q, q))
    csp = pl.BlockSpec((QS, QU), lambda q: (q, q))
    return _call(
        body, comm, name=name, grid=(NQ,),
        out_shape=[jax.ShapeDtypeStruct((L, DS), BF16),
                   jax.ShapeDtypeStruct((NQ, QU, QS), F32), jax.ShapeDtypeStruct((NQ, QU, QS), F32),
                   jax.ShapeDtypeStruct((NQ, QS, QU), F32), jax.ShapeDtypeStruct((NQ, QS, QU), F32),
                   jax.ShapeDtypeStruct((NQ, 8, QS), F32), jax.ShapeDtypeStruct((1, DS), F32)],
        in_specs=[col, col, bsp, bsp, csp, csp,
                  pl.BlockSpec((8, QS), lambda q: (0, q)), pl.BlockSpec((1, QU), lambda q: (0, q))],
        out_specs=[col,
                   pl.BlockSpec((1, QU, QS), lambda q: (q, 0, 0)), pl.BlockSpec((1, QU, QS), lambda q: (q, 0, 0)),
                   pl.BlockSpec((1, QS, QU), lambda q: (q, 0, 0)), pl.BlockSpec((1, QS, QU), lambda q: (q, 0, 0)),
                   pl.BlockSpec((1, 8, QS), lambda q: (q, 0, 0)), pl.BlockSpec((1, QU), lambda q: (0, q))],
        scratch_shapes=[pltpu.VMEM((NLB, L + SOFF, 128), F32), pltpu.VMEM((NLB, L + SOFF, 128), F32),
                        pltpu.VMEM((NLB, L, 128), F32), pltpu.VMEM((NLB, L, 128), F32)],
        params=_params("parallel"),
    )(uf, dyss, bre, bim, cre, cim, lamp, dsk)


def _branches(z1_ref, yss_ref, gt_ref, lng_ref, lnb_ref, wp_ref, wv_ref, wg_ref):
    zf = z1_ref[...]
    mu = jnp.mean(zf, axis=-1, keepdims=True)
    zc = zf - mu
    rstd = lax.rsqrt(jnp.mean(zc * zc, axis=-1, keepdims=True) + EPS)
    zn = zc * rstd
    z2 = zn * lng_ref[...] + lnb_ref[...]
    sz = _sigmoid(z2)
    z3 = (z2 * sz).astype(BF16)
    y_conv = _dot(z3, wp_ref[...])
    yss = yss_ref[...]
    yg = _gelu(yss).astype(BF16)
    sv = _dot(yg, wv_ref[...])
    sig = _sigmoid(_dot(yg, wg_ref[...]))
    y_ssm = sv * sig
    gc = gt_ref[:, 0:D].astype(F32)
    gs = gt_ref[:, D:2 * D].astype(F32)
    m = gc * y_conv + gs * y_ssm
    return dict(rstd=rstd, zn=zn, z2=z2, sz=sz, z3=z3, y_conv=y_conv, yss=yss, yg=yg, sv=sv, sig=sig,
                y_ssm=y_ssm, gc=gc, gs=gs, m=m)


def _merge_fwd(h, z1, yss, gate, lng, lnb, wp, wv, wg, wo, name):
    L = h.shape[0]
    tm = _tile(L, 528)

    def body(h_ref, z1_ref, yss_ref, gt_ref, lng_ref, lnb_ref, wp_ref, wv_ref, wg_ref, wo_ref, o_ref):
        f = _branches(z1_ref, yss_ref, gt_ref, lng_ref, lnb_ref, wp_ref, wv_ref, wg_ref)
        o_ref[...] = h_ref[...] + _dot(f["m"].astype(BF16), wo_ref[...])

    def row(n):
        return pl.BlockSpec((tm, n), lambda i: (i, 0))

    return pl.pallas_call(
        body, name=name, grid=(L // tm,),
        out_shape=jax.ShapeDtypeStruct((L, D), F32),
        in_specs=[row(D), row(DC), row(DS), row(2 * D), _res((1, DC)), _res((1, DC)),
                  _res((DC, D)), _res((DS, D)), _res((DS, D)), _res((D, D))],
        out_specs=row(D),
        compiler_params=_params("parallel"),
    )(h, z1, yss, gate, lng, lnb, wp, wv, wg, wo)


def _merge_bwd(dh, z1, yss, gate, lng, lnb, wp, wv, wg, wo, name):
    L = dh.shape[0]
    tm = _tile(L, 352)

    def body(dh_ref, z1_ref, yss_ref, gt_ref, lng_ref, lnb_ref, wp_ref, wv_ref, wg_ref, wo_ref,
             m_ref, dgt_ref, dyc_ref, z3_ref, dz1_ref, yg_ref, dsv_ref, dsg_ref, dyss_ref,
             dbg_ref, dlng_ref, dlnb_ref):
        i = pl.program_id(0)
        f = _branches(z1_ref, yss_ref, gt_ref, lng_ref, lnb_ref, wp_ref, wv_ref, wg_ref)
        gc, gs, sig, sv = f["gc"], f["gs"], f["sig"], f["sv"]
        m_ref[...] = f["m"].astype(BF16)
        z3_ref[...] = f["z3"]
        yg_ref[...] = f["yg"]
        dm = _dot_nt(dh_ref[...].astype(BF16), wo_ref[...])
        dgc = (dm * f["y_conv"] * gc * (1.0 - gc)).astype(BF16)
        dgs = (dm * f["y_ssm"] * gs * (1.0 - gs)).astype(BF16)
        dgt_ref[:, 0:D] = dgc
        dgt_ref[:, D:2 * D] = dgs
        part = jnp.concatenate([jnp.sum(dgc.astype(F32), axis=0, keepdims=True),
                                jnp.sum(dgs.astype(F32), axis=0, keepdims=True)], axis=1)
        _acc_rows(dbg_ref, part, i == 0)
        dyc = (dm * gc).astype(BF16)
        dyc_ref[...] = dyc
        dys = dm * gs
        dsv = (dys * sig).astype(BF16)
        dsg = (dys * sv * sig * (1.0 - sig)).astype(BF16)
        dsv_ref[...] = dsv
        dsg_ref[...] = dsg
        dyg = _dot_nt(dsv, wv_ref[...]) + _dot_nt(dsg, wg_ref[...])
        dyss_ref[...] = dyg * _gelu_grad(f["yss"])
        dz3 = _dot_nt(dyc, wp_ref[...])
        z2, sz, zn = f["z2"], f["sz"], f["zn"]
        dz2 = dz3 * sz * (1.0 + z2 * (1.0 - sz))
        _acc_rows(dlng_ref, jnp.sum(dz2 * zn, axis=0, keepdims=True), i == 0)
        _acc_rows(dlnb_ref, jnp.sum(dz2, axis=0, keepdims=True), i == 0)
        dzn = dz2 * lng_ref[...]
        dz1_ref[...] = f["rstd"] * (dzn - jnp.mean(dzn, axis=-1, keepdims=True)
                                    - zn * jnp.mean(dzn * zn, axis=-1, keepdims=True))

    def row(n):
        return pl.BlockSpec((tm, n), lambda i: (i, 0))

    def tot(n):
        return pl.BlockSpec((1, n), lambda i: (0, 0))

    return pl.pallas_call(
        body, name=name, grid=(L // tm,),
        out_shape=[jax.ShapeDtypeStruct((L, D), BF16), jax.ShapeDtypeStruct((L, 2 * D), BF16),
                   jax.ShapeDtypeStruct((L, D), BF16), jax.ShapeDtypeStruct((L, DC), BF16),
                   jax.ShapeDtypeStruct((L, DC), F32), jax.ShapeDtypeStruct((L, DS), BF16),
                   jax.ShapeDtypeStruct((L, D), BF16), jax.ShapeDtypeStruct((L, D), BF16),
                   jax.ShapeDtypeStruct((L, DS), F32),
                   jax.ShapeDtypeStruct((1, 2 * D), F32), jax.ShapeDtypeStruct((1, DC), F32),
                   jax.ShapeDtypeStruct((1, DC), F32)],
        in_specs=[row(D), row(DC), row(DS), row(2 * D), _res((1, DC)), _res((1, DC)),
                  _res((DC, D)), _res((DS, D)), _res((DS, D)), _res((D, D))],
        out_specs=[row(D), row(2 * D), row(D), row(DC), row(DC), row(DS), row(D), row(D), row(DS),
                   tot(2 * D), tot(DC), tot(DC)],
        compiler_params=_params("arbitrary"),
    )(dh, z1, yss, gate, lng, lnb, wp, wv, wg, wo)


def _final(h, g, tgt, name):
    L = h.shape[0]
    tm = _tile(L, 528)

    def body(h_ref, g_ref, t_ref, dh_ref, loss_ref, dg_ref):
        i = pl.program_id(0)
        hv = h_ref[...]
        gv = g_ref[...]
        y, r = _rms(hv, gv)
        row = i * tm + lax.broadcasted_iota(jnp.int32, (tm, 1), 0)
        e = jnp.where(row >= FRONT, y - t_ref[...], 0.0)
        dy = e * (1.0 / D)
        part = 0.5 * jnp.sum(jnp.sum(e * dy, axis=1, keepdims=True), axis=0, keepdims=True)
        dx, xh = _rms_bwd(dy, hv, r, gv)
        dh_ref[...] = dx
        _acc_rows(loss_ref, part, i == 0)
        _acc_rows(dg_ref, jnp.sum(dy * xh, axis=0, keepdims=True), i == 0)

    row = pl.BlockSpec((tm, D), lambda i: (i, 0))
    return pl.pallas_call(
        body, name=name, grid=(L // tm,),
        out_shape=[jax.ShapeDtypeStruct((L, D), F32), jax.ShapeDtypeStruct((1, 1), F32),
                   jax.ShapeDtypeStruct((1, D), F32)],
        in_specs=[row, _res((1, D)), row],
        out_specs=[row, pl.BlockSpec((1, 1), lambda i: (0, 0)), pl.BlockSpec((1, D), lambda i: (0, 0))],
        compiler_params=_params("arbitrary"),
    )(h, g, tgt)


def _ssm_disc(lam_re, lam_im, log_dt, b_re, b_im):
    lam = lax.complex(lam_re, lam_im)
    dt = jnp.exp(log_dt)[:, None]
    lam_bar = jnp.exp(lam * dt)
    bbar = ((lam_bar - 1.0) / lam)[..., None] * lax.complex(b_re, b_im)
    return jnp.real(lam_bar), jnp.imag(lam_bar), jnp.real(bbar), jnp.imag(bbar)


def _bdiag_in(m):
    return jnp.einsum("gph,gk->ghkp", m, jnp.eye(G, dtype=m.dtype)).reshape(G * H, G * P)


def _bdiag_out(m):
    return jnp.einsum("ghp,gk->gpkh", m, jnp.eye(G, dtype=m.dtype)).reshape(G * P, G * H)


def _diag_blocks(m4):
    return jnp.einsum("qiaib->qiab", m4).reshape(G, m4.shape[2], m4.shape[4])


def _pack(parts, rows_mult=8):
    flat = jnp.concatenate([p.reshape(-1).astype(F32) for p in parts])
    n = flat.shape[0]
    tot = -(-n // (128 * rows_mult)) * (128 * rows_mult)
    return jnp.pad(flat, (0, tot - n)).reshape(tot // 128, 128)


def _unpack(buf, shapes):
    flat = buf.reshape(-1)
    out, o = [], 0
    for s in shapes:
        n = math.prod(s)
        out.append(flat[o:o + n].reshape(s))
        o += n
    return out


def kernel(x, meta_tokens, ffn1_norm, ffn1_w1, ffn1_w3, ffn1_w2, mix_norm, w_in, b_gate, conv_dw, conv_dw_b, conv_ln_g, conv_ln_b, conv_proj, ssm_lam_re, ssm_lam_im, ssm_log_dt, ssm_b_re, ssm_b_im, ssm_c_re, ssm_c_im, ssm_d, ssm_w_v, ssm_w_g, w_out, ffn2_norm, ffn2_w1, ffn2_w3, ffn2_w2, final_norm, loss_target, m_meta_tokens, m_ffn1_norm, m_ffn1_w1, m_ffn1_w3, m_ffn1_w2, m_mix_norm, m_w_in, m_b_gate, m_conv_dw, m_conv_dw_b, m_conv_ln_g, m_conv_ln_b, m_conv_proj, m_ssm_lam_re, m_ssm_lam_im, m_ssm_log_dt, m_ssm_b_re, m_ssm_b_im, m_ssm_c_re, m_ssm_c_im, m_ssm_d, m_ssm_w_v, m_ssm_w_g, m_w_out, m_ffn2_norm, m_ffn2_w1, m_ffn2_w3, m_ffn2_w2, m_final_norm, v_meta_tokens, v_ffn1_norm, v_ffn1_w1, v_ffn1_w3, v_ffn1_w2, v_mix_norm, v_w_in, v_b_gate, v_conv_dw, v_conv_dw_b, v_conv_ln_g, v_conv_ln_b, v_conv_proj, v_ssm_lam_re, v_ssm_lam_im, v_ssm_log_dt, v_ssm_b_re, v_ssm_b_im, v_ssm_c_re, v_ssm_c_im, v_ssm_d, v_ssm_w_v, v_ssm_w_g, v_w_out, v_ffn2_norm, v_ffn2_w1, v_ffn2_w3, v_ffn2_w2, v_final_norm):
    args = dict(locals())
    names = ["meta_tokens", "ffn1_norm", "ffn1_w1", "ffn1_w3", "ffn1_w2", "mix_norm", "w_in", "b_gate",
             "conv_dw", "conv_dw_b", "conv_ln_g", "conv_ln_b", "conv_proj", "ssm_lam_re", "ssm_lam_im",
             "ssm_log_dt", "ssm_b_re", "ssm_b_im", "ssm_c_re", "ssm_c_im", "ssm_d", "ssm_w_v", "ssm_w_g",
             "w_out", "ffn2_norm", "ffn2_w1", "ffn2_w3", "ffn2_w2", "final_norm"]
    big = ["ffn1_w1", "ffn1_w3", "ffn1_w2", "w_in", "conv_proj", "ssm_w_v", "ssm_w_g", "w_out",
           "ffn2_w1", "ffn2_w3", "ffn2_w2"]
    small = [n for n in names if n not in big]

    xs = x[0]
    S = xs.shape[0]
    L = FRONT + S
    T = L // NSEG
    jx, jy = lax.axis_index("x"), lax.axis_index("y")
    chip = 2 * jx + jy

    sm = _gather_all(_pack([meta_tokens, conv_dw[0]]), "gather_small")[0::2].reshape(NSH, -1)
    nmt = NMETA * (D // NSH)
    ndw = KW * (DC // NSH)
    meta_full = sm[:, :nmt].reshape(NSH, NMETA, D // NSH).transpose(1, 0, 2).reshape(NMETA, D)
    dw_full = sm[:, nmt:nmt + ndw].reshape(NSH, KW, DC // NSH).transpose(1, 0, 2).reshape(KW, DC)
    dw_pad = jnp.pad(dw_full, ((0, KWP - KW), (0, 0)))
    tposed = ("ffn1_w1", "ffn1_w3", "ffn2_w1", "ffn2_w3")

    def view(a, n):
        return jnp.swapaxes(a, 1, 2) if n in tposed else a

    grp_a = ["ffn1_w1", "ffn1_w3", "ffn1_w2"]
    grp_b = ["w_in", "conv_proj", "ssm_w_v", "ssm_w_g", "w_out"]
    grp_c = ["ffn2_w1", "ffn2_w3", "ffn2_w2"]

    def shard(n):
        return view(args[n], n)[0].astype(BF16)

    gw = dict(zip(grp_a, _gather_chips([shard(n) for n in grp_a], "gather_ffn1")))

    def cols(w):
        return w.transpose(1, 0, 2).reshape(w.shape[1], -1)

    disc_in = (ssm_lam_re[0], ssm_lam_im[0], ssm_log_dt[0], ssm_b_re[0], ssm_b_im[0])
    (lbr, lbi, bbr, bbi), disc_vjp = jax.vjp(_ssm_disc, *disc_in)
    lam_t = jnp.exp(lax.complex(ssm_lam_re[0], ssm_lam_im[0]) * (jnp.exp(ssm_log_dt[0])[:, None] * T))
    lamp = jnp.concatenate([lbr.reshape(1, NST), lbi.reshape(1, NST), jnp.real(lam_t).reshape(1, NST),
                            jnp.imag(lam_t).reshape(1, NST), jnp.zeros((4, NST), F32)], axis=0)
    bre_bd, bim_bd = _bdiag_in(bbr).astype(BF16), _bdiag_in(bbi).astype(BF16)
    cre_bd, cim_bd = _bdiag_out(ssm_c_re[0]).astype(BF16), _bdiag_out(ssm_c_im[0]).astype(BF16)

    h0 = jnp.concatenate([jnp.zeros((FRONT - NMETA, D), F32), meta_full, xs], axis=0)
    tgt = jnp.pad(loss_target[0], ((FRONT, 0), (0, 0)))
    (h1, a1, b1), got = _ffn_fwd(h0, ffn1_norm, gw["ffn1_w1"], gw["ffn1_w3"], gw["ffn1_w2"], "ffn1_fwd",
                                 _gather_half_behind([shard(n) for n in grp_b]))
    gw.update(zip(grp_b, _pass_halves(got, "pass_mix")))
    w_in_f = gw["w_in"]
    wp_f, wv_f, wg_f = cols(gw["conv_proj"]), cols(gw["ssm_w_v"]), cols(gw["ssm_w_g"])
    wo_f = gw["w_out"].reshape(D, D)
    (vg, uf, gate), got1 = _mix_in_fwd(h1, mix_norm, w_in_f, b_gate, "mix_in_fwd",
                                       _gather_half_behind([shard("ffn2_w1")]))
    (z1,), got3 = _conv_fwd(vg, dw_pad, conv_dw_b, "conv_fwd", _gather_half_behind([shard("ffn2_w3")]))
    (yss,), got2 = _ssm_fwd(uf, bre_bd, bim_bd, cre_bd, cim_bd, lamp, ssm_d, "ssm_fwd",
                            _gather_half_behind([shard("ffn2_w2")]))
    gw.update(zip(grp_c, _pass_halves([got1[0], got3[0], got2[0]], "pass_ffn2")))
    h2 = _merge_fwd(h1, z1, yss, gate, conv_ln_g, conv_ln_b, wp_f, wv_f, wg_f, wo_f, "merge_fwd")
    (h3, a2, b2), _ = _ffn_fwd(h2, ffn2_norm, gw["ffn2_w1"], gw["ffn2_w3"], gw["ffn2_w2"], "ffn2_fwd")

    gbig = {}
    core = lax.axis_index("c").astype(jnp.int32).reshape(1)

    def pair_sums(group, tag):
        gl = [gbig[n] for n in group]
        sib = _pair_exchange(gl, "pair_exchange_" + tag)
        out = []
        for n, g_, s_ in zip(group, gl, sib):
            out.append(_add_pair(g_, s_, core, "pair_" + n))
        return out

    dh3, loss_part, d_final = _final(h3, final_norm.reshape(1, D), tgt, "final")
    (dh2, da2, db2, s2, n2, d_ffn2_norm), _ = _ffn_bwd(
        h2, ffn2_norm, dh3, a2, b2, gw["ffn2_w1"], gw["ffn2_w3"], gw["ffn2_w2"], "ffn2_bwd")
    gbig["ffn2_w1"] = _wgrad(da2, n2, "ffn2_dw1")
    gbig["ffn2_w3"] = _wgrad(db2, n2, "ffn2_dw3")
    gbig["ffn2_w2"] = _wgrad(s2, dh3, "ffn2_dw2", 0.5)
    pair_c = pair_sums(grp_c, "ffn2")
    (m_b, dgate, dyc, z3, dz1, yg, dsv, dsg, dyss, d_b_gate, d_ln_g, d_ln_b) = _merge_bwd(
        dh2, z1, yss, gate, conv_ln_g, conv_ln_b, wp_f, wv_f, wg_f, wo_f, "merge_bwd")
    gbig["w_out"] = _wgrad(m_b, dh2, "dw_out").reshape(NSH, D // NSH, D)

    def shard_cols(gm):
        return gm.reshape(gm.shape[0], NSH, -1).transpose(1, 0, 2)

    gbig["conv_proj"] = shard_cols(_wgrad(z3, dyc, "dw_proj"))
    gbig["ssm_w_v"] = shard_cols(_wgrad(yg, dsv, "dw_v"))
    gbig["ssm_w_g"] = shard_cols(_wgrad(yg, dsg, "dw_g"))
    dv, dgl, ddw, d_dw_b = _conv_bwd(dz1, vg, dw_pad, "conv_bwd")
    (duf, dbre, dbim, dcre, dcim, dlam, d_ssm_d), recv_c = _ssm_bwd(
        uf, dyss, bre_bd, bim_bd, cre_bd, cim_bd, lamp, ssm_d, "ssm_bwd", _scatter_chips_behind(pair_c))
    dh1, u_b, dproj, d_mix_norm = _mix_in_bwd(h1, mix_norm, dh2, dv, dgl, duf, dgate, w_in_f, "mix_in_bwd")
    gbig["w_in"] = _wgrad(u_b, dproj, "dw_in")
    pair_b = pair_sums(grp_b, "mix")

    d_bbr = _diag_blocks(dbre.reshape(NQ, 8, H, 8, P)).transpose(0, 2, 1)
    d_bbi = _diag_blocks(dbim.reshape(NQ, 8, H, 8, P)).transpose(0, 2, 1)
    d_c_re = _diag_blocks(dcre.reshape(NQ, 8, P, 8, H)).transpose(0, 2, 1)
    d_c_im = _diag_blocks(dcim.reshape(NQ, 8, P, 8, H)).transpose(0, 2, 1)
    d_lbr = dlam[:, 0, :].reshape(G, P)
    d_lbi = dlam[:, 1, :].reshape(G, P)
    d_lam_re, d_lam_im, d_log_dt, d_b_re, d_b_im = disc_vjp((d_lbr, d_lbi, d_bbr, d_bbi))

    sg = {"mix_norm": d_mix_norm, "b_gate": d_b_gate, "conv_dw": ddw[:KW], "conv_dw_b": d_dw_b,
          "conv_ln_g": d_ln_g, "conv_ln_b": d_ln_b, "ssm_lam_re": d_lam_re, "ssm_lam_im": d_lam_im,
          "ssm_log_dt": d_log_dt, "ssm_b_re": d_b_re, "ssm_b_im": d_b_im, "ssm_c_re": d_c_re, "ssm_c_im": d_c_im,
          "ssm_d": d_ssm_d, "ffn2_norm": d_ffn2_norm, "final_norm": d_final}
    late = ["meta_tokens", "ffn1_norm"]
    early = [n for n in small if n not in late]

    (dh0, da1, db1, s1, n1, d_ffn1_norm), got = _ffn_bwd(
        h0, ffn1_norm, dh1, a1, b1, gw["ffn1_w1"], gw["ffn1_w3"], gw["ffn1_w2"], "ffn1_bwd",
        _join(_scatter_chips_behind(pair_b), _gather_all_behind(_pack([sg[n] for n in early]))))
    recv_b, early_all = got[:len(grp_b)], got[len(grp_b)]
    gbig["ffn1_w1"] = _wgrad(da1, n1, "ffn1_dw1")
    gbig["ffn1_w3"] = _wgrad(db1, n1, "ffn1_dw3")
    gbig["ffn1_w2"] = _wgrad(s1, dh1, "ffn1_dw2", 0.5)
    grad_x = dh0[FRONT:][None]
    recv_a = _scatter_chips(pair_sums(grp_a, "ffn1"), "scatter_ffn1")
    sg["meta_tokens"] = dh0[FRONT - NMETA:FRONT]
    sg["ffn1_norm"] = d_ffn1_norm

    recv = dict(zip(grp_a + grp_b + grp_c, list(recv_a) + list(recv_b) + list(recv_c)))
    halves = [_sum_slots(recv[n], "sum_" + n) for n in big]
    full = _swap_halves(halves, "swap_halves")
    out_g, out_d, out_m, out_v = {}, {}, {}, {}
    for n, f in zip(big, full):
        g3 = f.reshape(1, f.shape[0] * f.shape[1], f.shape[2])
        d3, m3, v3 = _adamw(view(args[n], n), g3, view(args["m_" + n], n), view(args["v_" + n], n), "adamw_" + n)
        out_g[n], out_d[n], out_m[n], out_v[n] = (view(t, n) for t in (g3, d3, m3, v3))

    late_all = _gather_all(_pack([sg[n] for n in late]), "gather_late_grads")
    sgr = dict(zip(early, _unpack(_sum_slots(early_all, "sum_early"), [sg[n].shape for n in early])))
    sgr.update(zip(late, _unpack(_sum_slots(late_all, "sum_late"), [sg[n].shape for n in late])))
    sgr["meta_tokens"] = lax.dynamic_slice_in_dim(sgr["meta_tokens"], chip * (D // NSH), D // NSH, axis=1)
    sgr["conv_dw"] = lax.dynamic_slice_in_dim(sgr["conv_dw"], chip * (DC // NSH), DC // NSH, axis=1)
    pshapes = [args[n].shape for n in small]
    d_s, m_s, v_s = _adamw(_pack([args[n] for n in small])[None], _pack([sgr[n] for n in small])[None],
                           _pack([args["m_" + n] for n in small])[None],
                           _pack([args["v_" + n] for n in small])[None], "adamw_small")
    for n, g_, d_, m_, v_ in zip(small, [sgr[n] for n in small], _unpack(d_s[0], pshapes),
                                 _unpack(m_s[0], pshapes), _unpack(v_s[0], pshapes)):
        out_g[n], out_d[n], out_m[n], out_v[n] = g_.reshape(args[n].shape), d_, m_, v_

    loss = lax.psum(loss_part[0, 0], ("x", "y", "c"))
    return (loss, grad_x, *[out_g[n] for n in names], *[out_d[n] for n in names],
            *[out_m[n] for n in names], *[out_v[n] for n in names])
```

```python
import math

import jax
import jax.numpy as jnp
from jax import lax
from jax.experimental import pallas as pl
from jax.experimental.pallas import tpu as pltpu

F32 = jnp.float32
BF16 = jnp.bfloat16

D = 1024
NSH = 4
F = 2816
FS = F // NSH
DC = 512
DS = 512
DIN = 2 * DC + DS + 2 * D
WS = DIN // NSH
KW = 31
KWP = 32
CONV_ROWS = 64
NMETA = 16
FRONT = 128
G, P, H = 32, 64, 16
NST = G * P
NQ = 4
QS = NST // NQ
QU = DS // NQ
NSEG = 32
NGRP = NSEG // 8
NCH = 8
SOFF = 8
EPS = 1e-6
LR, B1, B2, AEPS, WD, STEP = 1e-3, 0.9, 0.999, 1e-8, 0.01, 10
VMEM_LIMIT = 58 * 1024 * 1024
MESH = pl.DeviceIdType.MESH
ANY = pl.BlockSpec(memory_space=pl.ANY)


def _params(*sem):
    return pltpu.CompilerParams(dimension_semantics=sem, vmem_limit_bytes=VMEM_LIMIT)


def _res(shape):
    nd = len(shape)
    return pl.BlockSpec(shape, lambda *_: (0,) * nd, pipeline_mode=pl.Buffered(1))


def _tile(n, cap, mult=16):
    best = None
    for t in range(mult, min(n, cap) + 1, mult):
        if n % t == 0:
            best = t
    assert best is not None, (n, cap, mult)
    return best


def _dot(a, b):
    return jnp.dot(a, b, preferred_element_type=F32)


def _dot_nt(a, b):
    return lax.dot_general(a, b, (((1,), (1,)), ((), ())), preferred_element_type=F32)


def _dot_tn(a, b):
    return lax.dot_general(a, b, (((0,), (0,)), ((), ())), preferred_element_type=F32)


def _sigmoid(x):
    return 1.0 / (1.0 + jnp.exp(-x))


_GC = math.sqrt(2.0 / math.pi)
_GA = 0.044715


def _gelu(x):
    return 0.5 * x * (1.0 + jnp.tanh(_GC * (x + _GA * x * x * x)))


def _gelu_grad(x):
    t = jnp.tanh(_GC * (x + _GA * x * x * x))
    return 0.5 * (1.0 + t) + 0.5 * x * (1.0 - t * t) * _GC * (1.0 + 3.0 * _GA * x * x)


def _rms(hv, g):
    r = lax.rsqrt(jnp.mean(hv * hv, axis=-1, keepdims=True) + EPS)
    return hv * r * g, r


def _rms_bwd(dn, hv, r, g):
    xh = hv * r
    dxh = dn * g
    return r * (dxh - xh * jnp.mean(dxh * xh, axis=-1, keepdims=True)), xh


def _acc_rows(ref, part, first):
    @pl.when(first)
    def _():
        ref[...] = part

    @pl.when(jnp.logical_not(first))
    def _():
        ref[...] += part


def _coords():
    return lax.axis_index("x"), lax.axis_index("y"), lax.axis_index("c")


def _flip(v, d):
    return 1 - v if d else v


def _run(local, remote):
    for cp in local + remote:
        cp.start()
    for cp in remote:
        cp.wait()
    for cp in local:
        cp.wait()


def _via_vmem(src, dst, stage, sems, i):
    return (pltpu.make_async_copy(src, stage, sems.at[2 * i]), pltpu.make_async_copy(stage, dst, sems.at[2 * i + 1]))


def _run_staged(staged, remote):
    for load, _ in staged:
        load.start()
    for cp in remote:
        cp.start()
    for load, store in staged:
        load.wait()
        store.start()
    for cp in remote:
        cp.wait()
    for _, store in staged:
        store.wait()


_REL3 = ((1, 0), (0, 1), (1, 1))


class _Behind:
    def __init__(self, arrays, out_shapes, scratch, build):
        self.arrays, self.out_shapes, self.scratch, self.build = list(arrays), list(out_shapes), list(scratch), build

    def start(self, ins, outs, scr):
        staged, remote = self.build(ins, outs, scr)
        for load, _ in staged:
            load.start()
        for cp in remote:
            cp.start()

    def finish(self, ins, outs, scr):
        staged, remote = self.build(ins, outs, scr)
        for load, store in staged:
            load.wait()
            store.start()
        for cp in remote:
            cp.wait()
        for _, store in staged:
            store.wait()


def _call(body, comm, *, name, grid, in_specs, out_specs, out_shape, scratch_shapes=(), params):
    in_specs, out_specs, out_shape = list(in_specs), list(out_specs), list(out_shape)
    scratch_shapes = list(scratch_shapes)
    if comm is None:
        f = pl.pallas_call(body, name=name, grid=grid, in_specs=in_specs, out_specs=out_specs,
                           out_shape=out_shape, scratch_shapes=scratch_shapes, compiler_params=params)
        return lambda *args: (f(*args), [])
    ni, no, ns = len(in_specs), len(out_specs), len(scratch_shapes)
    ci, co = len(comm.arrays), len(comm.out_shapes)

    def hosted(*refs):
        ins, cin = refs[:ni], refs[ni:ni + ci]
        outs, cout = refs[ni + ci:ni + ci + no], refs[ni + ci + no:ni + ci + no + co]
        scr, cscr = refs[ni + ci + no + co:ni + ci + no + co + ns], refs[ni + ci + no + co + ns:]
        first = last = None
        for axis, size in enumerate(grid):
            i = pl.program_id(axis)
            first = (i == 0) if first is None else jnp.logical_and(first, i == 0)
            last = (i == size - 1) if last is None else jnp.logical_and(last, i == size - 1)

        @pl.when(first)
        def _():
            comm.start(cin, cout, cscr)

        body(*ins, *outs, *scr)

        @pl.when(last)
        def _():
            comm.finish(cin, cout, cscr)

    f = pl.pallas_call(hosted, name=name, grid=grid, in_specs=in_specs + [ANY] * ci,
                       out_specs=out_specs + [ANY] * co, out_shape=out_shape + comm.out_shapes,
                       scratch_shapes=scratch_shapes + comm.scratch,
                       compiler_params=_params(*(("arbitrary",) * len(grid))))

    def run(*args):
        res = f(*args, *comm.arrays)
        return res[:no], res[no:]

    return run


def _gather_half_behind(shards):
    n = len(shards)

    def build(ins, outs, scr):
        send, recv, loc = scr[:3]
        stage = scr[3:]
        x, y, c = _coords()
        me = 2 * x + y
        staged = [_via_vmem(ins[t], outs[t].at[me], stage[t], loc, t) for t in range(n)]
        remote = []
        for t in range(n):
            half = shards[t].shape[0] // 2
            mine = pl.ds(c * half, half)
            for k, (dx, dy) in enumerate(_REL3):
                remote.append(pltpu.make_async_remote_copy(
                    src_ref=ins[t].at[mine], dst_ref=outs[t].at[me, mine],
                    send_sem=send.at[3 * t + k], recv_sem=recv.at[3 * t + k],
                    device_id=(_flip(x, dx), _flip(y, dy), c), device_id_type=MESH))
        return staged, remote

    return _Behind(shards, [jax.ShapeDtypeStruct((NSH,) + s.shape, s.dtype) for s in shards],
                   [pltpu.SemaphoreType.DMA((3 * n,)), pltpu.SemaphoreType.DMA((3 * n,)),
                    pltpu.SemaphoreType.DMA((2 * n,))] + [pltpu.VMEM(s.shape, s.dtype) for s in shards], build)


def _pass_halves(gathered, name):
    n = len(gathered)

    def body(*refs):
        outs = refs[n:2 * n]
        send, recv = refs[2 * n:]
        x, y, c = _coords()
        remote = []
        for t in range(n):
            half = gathered[t].shape[1] // 2
            mine = pl.ds(c * half, half)
            for k, (dx, dy) in enumerate(_REL3):
                slot = 2 * _flip(x, dx) + _flip(y, dy)
                remote.append(pltpu.make_async_remote_copy(
                    src_ref=outs[t].at[slot, mine], dst_ref=outs[t].at[slot, mine],
                    send_sem=send.at[3 * t + k], recv_sem=recv.at[3 * t + k],
                    device_id=(x, y, 1 - c), device_id_type=MESH))
        _run([], remote)

    return pl.pallas_call(
        body, name=name,
        out_shape=[jax.ShapeDtypeStruct(g.shape, g.dtype) for g in gathered],
        in_specs=[ANY] * n, out_specs=[ANY] * n, input_output_aliases={t: t for t in range(n)},
        scratch_shapes=[pltpu.SemaphoreType.DMA((3 * n,)), pltpu.SemaphoreType.DMA((3 * n,))],
    )(*gathered)


def _scatter_chips_behind(sums):
    n = len(sums)

    def build(ins, outs, scr):
        send, recv, loc = scr[:3]
        stage = scr[3:]
        x, y, c = _coords()
        me = 2 * x + y
        staged = [_via_vmem(ins[t].at[me], outs[t].at[me], stage[t], loc, t) for t in range(n)]
        remote = []
        for t in range(n):
            for k, (dx, dy) in enumerate(_REL3):
                px, py = _flip(x, dx), _flip(y, dy)
                remote.append(pltpu.make_async_remote_copy(
                    src_ref=ins[t].at[2 * px + py], dst_ref=outs[t].at[me],
                    send_sem=send.at[3 * t + k], recv_sem=recv.at[3 * t + k],
                    device_id=(px, py, c), device_id_type=MESH))
        return staged, remote

    return _Behind(sums, [jax.ShapeDtypeStruct(s.shape, s.dtype) for s in sums],
                   [pltpu.SemaphoreType.DMA((3 * n,)), pltpu.SemaphoreType.DMA((3 * n,)),
                    pltpu.SemaphoreType.DMA((2 * n,))] + [pltpu.VMEM(s.shape[1:], s.dtype) for s in sums], build)


def _gather_all_behind(a):
    def build(ins, outs, scr):
        send, recv, loc, stage = scr
        x, y, c = _coords()
        me = 4 * x + 2 * y + c
        staged = [_via_vmem(ins[0], outs[0].at[me], stage, loc, 0)]
        remote = [pltpu.make_async_remote_copy(
            src_ref=ins[0], dst_ref=outs[0].at[me], send_sem=send.at[k], recv_sem=recv.at[k],
            device_id=(_flip(x, dx), _flip(y, dy), _flip(c, dc)), device_id_type=MESH)
            for k, (dx, dy, dc) in enumerate(_REL7)]
        return staged, remote

    return _Behind([a], [jax.ShapeDtypeStruct((8,) + a.shape, a.dtype)],
                   [pltpu.SemaphoreType.DMA((7,)), pltpu.SemaphoreType.DMA((7,)), pltpu.SemaphoreType.DMA((2,)),
                    pltpu.VMEM(a.shape, a.dtype)], build)


def _join(*parts):
    def cut(seq, key):
        res, o = [], 0
        for p in parts:
            k = len(getattr(p, key))
            res.append(seq[o:o + k])
            o += k
        return res

    def build(ins, outs, scr):
        staged, remote = [], []
        for p, i, o, s in zip(parts, cut(ins, "arrays"), cut(outs, "out_shapes"), cut(scr, "scratch")):
            st, rm = p.build(i, o, s)
            staged += st
            remote += rm
        return staged, remote

    return _Behind(sum((p.arrays for p in parts), []), sum((p.out_shapes for p in parts), []),
                   sum((p.scratch for p in parts), []), build)


def _gather_chips(shards, name):
    n = len(shards)

    def body(*refs):
        ins, outs = refs[:n], refs[n:2 * n]
        send, recv, fsend, frecv, loc = refs[2 * n:2 * n + 5]
        stage = refs[2 * n + 5:]
        x, y, c = _coords()
        me = 2 * x + y
        own = [_via_vmem(ins[t], outs[t].at[me], stage[t], loc, t) for t in range(n)]
        first, passed = [], []
        for t in range(n):
            half = shards[t].shape[0] // 2
            mine, theirs = pl.ds(c * half, half), pl.ds((1 - c) * half, half)
            for k, (dx, dy) in enumerate(_REL3):
                px, py = _flip(x, dx), _flip(y, dy)
                first.append(pltpu.make_async_remote_copy(
                    src_ref=ins[t].at[mine], dst_ref=outs[t].at[me, mine],
                    send_sem=send.at[3 * t + k], recv_sem=recv.at[3 * t + k],
                    device_id=(px, py, c), device_id_type=MESH))
                passed.append((
                    pltpu.make_async_remote_copy(
                        src_ref=outs[t].at[2 * px + py, mine], dst_ref=outs[t].at[2 * px + py, mine],
                        send_sem=fsend.at[3 * t + k], recv_sem=frecv.at[3 * t + k],
                        device_id=(x, y, 1 - c), device_id_type=MESH),
                    pltpu.make_async_remote_copy(
                        src_ref=outs[t].at[2 * px + py, theirs], dst_ref=outs[t].at[2 * px + py, theirs],
                        send_sem=fsend.at[3 * t + k], recv_sem=frecv.at[3 * t + k],
                        device_id=(x, y, 1 - c), device_id_type=MESH)))
        for load, _ in own:
            load.start()
        for cp in first:
            cp.start()
        for load, store in own:
            load.wait()
            store.start()
        for cp, (fwd, _) in zip(first, passed):
            cp.wait_recv()
            fwd.start()
        for cp, (fwd, back) in zip(first, passed):
            cp.wait_send()
            fwd.wait_send()
            back.wait_recv()
        for _, store in own:
            store.wait()

    return pl.pallas_call(
        body, name=name,
        out_shape=[jax.ShapeDtypeStruct((NSH,) + s.shape, s.dtype) for s in shards],
        in_specs=[ANY] * n, out_specs=[ANY] * n,
        scratch_shapes=[pltpu.SemaphoreType.DMA((3 * n,)) for _ in range(4)] + [pltpu.SemaphoreType.DMA((2 * n,))]
        + [pltpu.VMEM(s.shape, s.dtype) for s in shards],
        compiler_params=pltpu.CompilerParams(vmem_limit_bytes=VMEM_LIMIT),
    )(*shards)


_REL7 = tuple((dx, dy, dc) for dx in (0, 1) for dy in (0, 1) for dc in (0, 1))[1:]


def _gather_all(a, name):
    def body(a_ref, o_ref, send, recv, loc):
        x, y, c = _coords()
        me = 4 * x + 2 * y + c
        local = [pltpu.make_async_copy(a_ref, o_ref.at[me], loc.at[0])]
        remote = [pltpu.make_async_remote_copy(
            src_ref=a_ref, dst_ref=o_ref.at[me], send_sem=send.at[k], recv_sem=recv.at[k],
            device_id=(_flip(x, dx), _flip(y, dy), _flip(c, dc)), device_id_type=MESH)
            for k, (dx, dy, dc) in enumerate(_REL7)]
        _run(local, remote)

    return pl.pallas_call(
        body, name=name,
        out_shape=jax.ShapeDtypeStruct((8,) + a.shape, a.dtype),
        in_specs=[ANY], out_specs=ANY,
        scratch_shapes=[pltpu.SemaphoreType.DMA((7,)), pltpu.SemaphoreType.DMA((7,)),
                        pltpu.SemaphoreType.DMA((1,))],
    )(a)


def _pair_exchange(grads, name):
    n = len(grads)

    def body(*refs):
        ins, outs = refs[:n], refs[n:2 * n]
        send, recv = refs[2 * n:]
        x, y, c = _coords()
        remote = []
        for t in range(n):
            half = grads[t].shape[1] // 2
            remote.append(pltpu.make_async_remote_copy(
                src_ref=ins[t].at[:, pl.ds((1 - c) * half, half)], dst_ref=outs[t],
                send_sem=send.at[t], recv_sem=recv.at[t],
                device_id=(x, y, 1 - c), device_id_type=MESH))
        _run([], remote)

    return pl.pallas_call(
        body, name=name,
        out_shape=[jax.ShapeDtypeStruct((NSH, g.shape[1] // 2, g.shape[2]), g.dtype) for g in grads],
        in_specs=[ANY] * n, out_specs=[ANY] * n,
        scratch_shapes=[pltpu.SemaphoreType.DMA((n,)), pltpu.SemaphoreType.DMA((n,))],
    )(*grads)


def _scatter_chips(sums, name):
    n = len(sums)

    def body(*refs):
        ins, outs = refs[:n], refs[n:2 * n]
        send, recv, loc = refs[2 * n:2 * n + 3]
        stage = refs[2 * n + 3:]
        x, y, c = _coords()
        me = 2 * x + y
        local = [_via_vmem(ins[t].at[me], outs[t].at[me], stage[t], loc, t) for t in range(n)]
        remote = []
        for t in range(n):
            for k, (dx, dy) in enumerate(_REL3):
                px, py = _flip(x, dx), _flip(y, dy)
                remote.append(pltpu.make_async_remote_copy(
                    src_ref=ins[t].at[2 * px + py], dst_ref=outs[t].at[me],
                    send_sem=send.at[3 * t + k], recv_sem=recv.at[3 * t + k],
                    device_id=(px, py, c), device_id_type=MESH))
        _run_staged(local, remote)

    return pl.pallas_call(
        body, name=name,
        out_shape=[jax.ShapeDtypeStruct(s.shape, s.dtype) for s in sums],
        in_specs=[ANY] * n, out_specs=[ANY] * n,
        scratch_shapes=[pltpu.SemaphoreType.DMA((3 * n,)), pltpu.SemaphoreType.DMA((3 * n,)),
                        pltpu.SemaphoreType.DMA((2 * n,))]
        + [pltpu.VMEM(s.shape[1:], s.dtype) for s in sums],
        compiler_params=pltpu.CompilerParams(vmem_limit_bytes=VMEM_LIMIT),
    )(*sums)


def _swap_halves(halves, name):
    n = len(halves)

    def body(*refs):
        ins, outs = refs[:n], refs[n:2 * n]
        send, recv, loc = refs[2 * n:2 * n + 3]
        stage = refs[2 * n + 3:]
        x, y, c = _coords()
        local = [_via_vmem(ins[t], outs[t].at[c], stage[t], loc, t) for t in range(n)]
        remote = [pltpu.make_async_remote_copy(
            src_ref=ins[t], dst_ref=outs[t].at[c], send_sem=send.at[t], recv_sem=recv.at[t],
            device_id=(x, y, 1 - c), device_id_type=MESH) for t in range(n)]
        _run_staged(local, remote)

    return pl.pallas_call(
        body, name=name,
        out_shape=[jax.ShapeDtypeStruct((2,) + h.shape, h.dtype) for h in halves],
        in_specs=[ANY] * n, out_specs=[ANY] * n,
        scratch_shapes=[pltpu.SemaphoreType.DMA((n,)), pltpu.SemaphoreType.DMA((n,)),
                        pltpu.SemaphoreType.DMA((2 * n,))]
        + [pltpu.VMEM(h.shape, h.dtype) for h in halves],
        compiler_params=pltpu.CompilerParams(vmem_limit_bytes=VMEM_LIMIT),
    )(*halves)


def _sum_slots(r, name):
    K, R, C = r.shape
    tr = _tile(R, max(16, (1 << 22) // (K * C)), 8 * (4 // r.dtype.itemsize))

    def body(r_ref, o_ref):
        acc = r_ref[0].astype(F32)
        for k in range(1, K):
            acc = acc + r_ref[k].astype(F32)
        o_ref[...] = acc

    return pl.pallas_call(
        body, name=name, grid=(R // tr,),
        out_shape=jax.ShapeDtypeStruct((R, C), F32),
        in_specs=[pl.BlockSpec((K, tr, C), lambda i: (0, i, 0))],
        out_specs=pl.BlockSpec((tr, C), lambda i: (i, 0)),
        compiler_params=_params("parallel"),
    )(r)


def _add_pair(g, s, core, name):
    _, half, C = s.shape
    tr = _tile(half, max(16, (1 << 19) // C))
    nb = half // tr

    def body(c_ref, g_ref, s_ref, o_ref):
        o_ref[...] = (g_ref[...].astype(F32) + s_ref[...].astype(F32)).astype(BF16)

    spec = pl.BlockSpec((1, tr, C), lambda j, i, c_ref: (j, i, 0))
    return pl.pallas_call(
        body, name=name,
        grid_spec=pltpu.PrefetchScalarGridSpec(
            num_scalar_prefetch=1, grid=(NSH, nb),
            in_specs=[pl.BlockSpec((1, tr, C), lambda j, i, c_ref: (j, c_ref[0] * nb + i, 0)), spec],
            out_specs=spec),
        out_shape=jax.ShapeDtypeStruct(s.shape, BF16),
        compiler_params=_params("parallel", "parallel"),
    )(core, g, s)


def _adamw(w, g, m, v, name):
    _, R, C = w.shape
    tr = _tile(R, max(8, (1 << 18) // C), 8)
    c1 = 1.0 / (1.0 - B1 ** STEP)
    c2 = 1.0 / (1.0 - B2 ** STEP)

    def body(w_ref, g_ref, m_ref, v_ref, d_ref, nm_ref, nv_ref):
        gv = g_ref[...]
        nm = B1 * m_ref[...] + (1.0 - B1) * gv
        nv = B2 * v_ref[...] + (1.0 - B2) * gv * gv
        nm_ref[...] = nm
        nv_ref[...] = nv
        d_ref[...] = -LR * ((nm * c1) / (jnp.sqrt(nv * c2) + AEPS) + WD * w_ref[...])

    spec = pl.BlockSpec((1, tr, C), lambda i: (0, i, 0))
    return pl.pallas_call(
        body, name=name, grid=(R // tr,),
        out_shape=[jax.ShapeDtypeStruct((1, R, C), F32)] * 3,
        in_specs=[spec] * 4, out_specs=[spec] * 3,
        compiler_params=_params("parallel"),
    )(w, g, m, v)


def _ffn_fwd(h, g, w1, w3, w2, name, comm=None):
    L = h.shape[0]
    tm = _tile(L, 704)

    def body(h_ref, g_ref, w1_ref, w3_ref, w2_ref, o_ref, a_ref, b_ref, n_s, acc_s):
        j = pl.program_id(1)

        @pl.when(j == 0)
        def _():
            hv = h_ref[...]
            n, _ = _rms(hv, g_ref[...])
            n_s[...] = n.astype(BF16)
            acc_s[...] = hv

        n = n_s[...]
        a = _dot_nt(n, w1_ref[0])
        b = _dot_nt(n, w3_ref[0])
        a_ref[0] = a.astype(BF16)
        b_ref[0] = b.astype(BF16)
        s = (a * _sigmoid(a) * b).astype(BF16)
        acc_s[...] += 0.5 * _dot(s, w2_ref[0])

        @pl.when(j == NSH - 1)
        def _():
            o_ref[...] = acc_s[...]

    row = pl.BlockSpec((tm, D), lambda i, j: (i, 0))
    hid = pl.BlockSpec((1, tm, FS), lambda i, j: (j, i, 0))
    wsp = pl.BlockSpec((1, FS, D), lambda i, j: (j, 0, 0))
    return _call(
        body, comm, name=name, grid=(L // tm, NSH),
        out_shape=[jax.ShapeDtypeStruct((L, D), F32),
                   jax.ShapeDtypeStruct((NSH, L, FS), BF16), jax.ShapeDtypeStruct((NSH, L, FS), BF16)],
        in_specs=[row, _res((1, D)), wsp, wsp, wsp],
        out_specs=[row, hid, hid],
        scratch_shapes=[pltpu.VMEM((tm, D), BF16), pltpu.VMEM((tm, D), F32)],
        params=_params("arbitrary", "arbitrary"),
    )(h, g, w1, w3, w2)


def _ffn_bwd(h, g, dout, a, b, w1, w3, w2, name, comm=None):
    L = h.shape[0]
    tm = _tile(L, 528)

    def body(h_ref, g_ref, do_ref, a_ref, b_ref, w1_ref, w3_ref, w2_ref,
             dh_ref, da_ref, db_ref, s_ref, n_ref, dg_ref, dob_s, dn_s):
        i, j = pl.program_id(0), pl.program_id(1)

        @pl.when(j == 0)
        def _():
            n, _ = _rms(h_ref[...], g_ref[...])
            n_ref[...] = n.astype(BF16)
            dob_s[...] = do_ref[...].astype(BF16)
            dn_s[...] = jnp.zeros_like(dn_s)

        av = a_ref[0].astype(F32)
        bv = b_ref[0].astype(F32)
        sig = _sigmoid(av)
        sa = av * sig
        ds = 0.5 * _dot_nt(dob_s[...], w2_ref[0])
        s_ref[0] = (sa * bv).astype(BF16)
        da = (ds * bv * sig * (1.0 + av * (1.0 - sig))).astype(BF16)
        db = (ds * sa).astype(BF16)
        da_ref[0] = da
        db_ref[0] = db
        dn_s[...] += _dot(da, w1_ref[0]) + _dot(db, w3_ref[0])

        @pl.when(j == NSH - 1)
        def _():
            hv = h_ref[...]
            gv = g_ref[...]
            r = lax.rsqrt(jnp.mean(hv * hv, axis=-1, keepdims=True) + EPS)
            dn = dn_s[...]
            dx, xh = _rms_bwd(dn, hv, r, gv)
            dh_ref[...] = do_ref[...] + dx
            _acc_rows(dg_ref, jnp.sum(dn * xh, axis=0, keepdims=True), i == 0)

    row = pl.BlockSpec((tm, D), lambda i, j: (i, 0))
    hid = pl.BlockSpec((1, tm, FS), lambda i, j: (j, i, 0))
    wsp = pl.BlockSpec((1, FS, D), lambda i, j: (j, 0, 0))
    return _call(
        body, comm, name=name, grid=(L // tm, NSH),
        out_shape=[jax.ShapeDtypeStruct((L, D), F32)]
        + [jax.ShapeDtypeStruct((NSH, L, FS), BF16)] * 3
        + [jax.ShapeDtypeStruct((L, D), BF16), jax.ShapeDtypeStruct((1, D), F32)],
        in_specs=[row, _res((1, D)), row, hid, hid,
                  wsp, wsp, wsp],
        out_specs=[row, hid, hid, hid, row, pl.BlockSpec((1, D), lambda i, j: (0, 0))],
        scratch_shapes=[pltpu.VMEM((tm, D), BF16), pltpu.VMEM((tm, D), F32)],
        params=_params("arbitrary", "arbitrary"),
    )(h, g, dout, a, b, w1, w3, w2)


def _wgrad(xm, ym, name, scale=1.0):
    xs, ys = xm.ndim == 3, ym.ndim == 3
    assert not (xs and ys)
    L = xm.shape[-2]
    K, N = xm.shape[-1], ym.shape[-1]
    tl = _tile(L, 1056)
    nl = L // tl
    if xs or ys:
        tn, grid_n = N, NSH
    else:
        tn = _tile(N, 1024, 128)
        grid_n = N // tn

    def body(x_ref, y_ref, o_ref, acc_s):
        l = pl.program_id(1)
        xv = x_ref[0] if xs else x_ref[...]
        yv = y_ref[0] if ys else y_ref[...]
        part = _dot_tn(xv.astype(BF16), yv.astype(BF16))
        _acc_rows(acc_s, part, l == 0)

        @pl.when(l == nl - 1)
        def _():
            res = (acc_s[...] * scale).astype(BF16)
            if xs or ys:
                o_ref[0] = res
            else:
                o_ref[...] = res

    if xs:
        x_spec = pl.BlockSpec((1, tl, K), lambda n, l: (n, l, 0))
        y_spec = pl.BlockSpec((tl, N), lambda n, l: (l, 0))
        o_spec = pl.BlockSpec((1, K, N), lambda n, l: (n, 0, 0))
        o_shape = (NSH, K, N)
    elif ys:
        x_spec = pl.BlockSpec((tl, K), lambda n, l: (l, 0))
        y_spec = pl.BlockSpec((1, tl, N), lambda n, l: (n, l, 0))
        o_spec = pl.BlockSpec((1, K, N), lambda n, l: (n, 0, 0))
        o_shape = (NSH, K, N)
    else:
        x_spec = pl.BlockSpec((tl, K), lambda n, l: (l, 0))
        y_spec = pl.BlockSpec((tl, tn), lambda n, l: (l, n))
        o_spec = pl.BlockSpec((K, tn), lambda n, l: (0, n))
        o_shape = (K, N)
    return pl.pallas_call(
        body, name=name, grid=(grid_n, nl),
        out_shape=jax.ShapeDtypeStruct(o_shape, BF16),
        in_specs=[x_spec, y_spec], out_specs=o_spec,
        scratch_shapes=[pltpu.VMEM((K, tn), F32)],
        compiler_params=_params("parallel", "arbitrary"),
    )(xm, ym)


def _mix_in_fwd(h, g, w_in, b_gate, name, comm=None):
    L = h.shape[0]
    tm = _tile(L, 528)

    def body(h_ref, g_ref, w_ref, bg_ref, vg_ref, uf_ref, gt_ref):
        u, _ = _rms(h_ref[...], g_ref[...])
        ub = u.astype(BF16)
        p = [_dot(ub, w_ref[j]) for j in range(NSH)]
        a0, a1 = 2 * DC - WS, 2 * DC + DS - WS
        vg_ref[:, 0:WS] = p[0].astype(BF16)
        vg_ref[:, WS:2 * DC] = p[1][:, 0:a0].astype(BF16)
        uf_ref[...] = p[1][:, a0:a1].astype(BF16)
        gin = jnp.concatenate([p[1][:, a1:], p[2], p[3]], axis=1)
        gt_ref[...] = _sigmoid(gin + bg_ref[...]).astype(BF16)

    def row(n):
        return pl.BlockSpec((tm, n), lambda i: (i, 0))

    return _call(
        body, comm, name=name, grid=(L // tm,),
        out_shape=[jax.ShapeDtypeStruct((L, 2 * DC), BF16), jax.ShapeDtypeStruct((L, DS), BF16),
                   jax.ShapeDtypeStruct((L, 2 * D), BF16)],
        in_specs=[row(D), _res((1, D)), _res((NSH, D, WS)), _res((1, 2 * D))],
        out_specs=[row(2 * DC), row(DS), row(2 * D)],
        params=_params("parallel"),
    )(h, g, w_in, b_gate)


def _mix_in_bwd(h, g, dres, dv, dgl, duf, dgate, w_in, name):
    L = h.shape[0]
    tm = _tile(L, 528)

    def body(h_ref, g_ref, dr_ref, dv_ref, dgl_ref, duf_ref, dgt_ref, w_ref, dh_ref, u_ref, dp_ref, dgm_ref):
        i = pl.program_id(0)
        hv = h_ref[...]
        gv = g_ref[...]
        u, r = _rms(hv, gv)
        u_ref[...] = u.astype(BF16)
        a0, a1 = 2 * DC - WS, 2 * DC + DS - WS
        b0 = WS - a1
        dp = [jnp.concatenate([dv_ref[...], dgl_ref[:, 0:WS - DC]], axis=1),
              jnp.concatenate([dgl_ref[:, WS - DC:], duf_ref[...], dgt_ref[:, 0:b0]], axis=1),
              dgt_ref[:, b0:b0 + WS], dgt_ref[:, b0 + WS:]]
        du = jnp.zeros((tm, D), F32)
        for j in range(NSH):
            dp_ref[j] = dp[j]
            du = du + _dot_nt(dp[j], w_ref[j])
        dx, xh = _rms_bwd(du, hv, r, gv)
        dh_ref[...] = dr_ref[...] + dx
        _acc_rows(dgm_ref, jnp.sum(du * xh, axis=0, keepdims=True), i == 0)

    def row(n):
        return pl.BlockSpec((tm, n), lambda i: (i, 0))

    return pl.pallas_call(
        body, name=name, grid=(L // tm,),
        out_shape=[jax.ShapeDtypeStruct((L, D), F32), jax.ShapeDtypeStruct((L, D), BF16),
                   jax.ShapeDtypeStruct((NSH, L, WS), BF16), jax.ShapeDtypeStruct((1, D), F32)],
        in_specs=[row(D), _res((1, D)), row(D), row(DC), row(DC), row(DS), row(2 * D), _res((NSH, D, WS))],
        out_specs=[row(D), row(D), pl.BlockSpec((NSH, tm, WS), lambda i: (0, i, 0)),
                   pl.BlockSpec((1, D), lambda i: (0, 0))],
        compiler_params=_params("arbitrary"),
    )(h, g, dres, dv, dgl, duf, dgate, w_in)


def _conv_fwd(vg, dw, dwb, name, comm=None):
    L = vg.shape[0]
    nc = DC // 128

    def body(v_ref, g_ref, dw_ref, dwb_ref, z_ref, zp_s):
        zp_s[0:KWP, :] = jnp.zeros((KWP, 128), F32)
        zp_s[KWP:, :] = v_ref[...].astype(F32) * _sigmoid(g_ref[...].astype(F32))
        for r0 in range(0, L, CONV_ROWS):
            acc = jnp.broadcast_to(dwb_ref[...], (CONV_ROWS, 128))
            for k in range(KW):
                acc = acc + dw_ref[k:k + 1, :] * zp_s[pl.ds(r0 + k + 2, CONV_ROWS), :]
            z_ref[pl.ds(r0, CONV_ROWS), :] = acc

    return _call(
        body, comm, name=name, grid=(nc,),
        out_shape=[jax.ShapeDtypeStruct((L, DC), F32)],
        in_specs=[pl.BlockSpec((L, 128), lambda c: (0, c)), pl.BlockSpec((L, 128), lambda c: (0, nc + c)),
                  pl.BlockSpec((KWP, 128), lambda c: (0, c)), pl.BlockSpec((1, 128), lambda c: (0, c))],
        out_specs=[pl.BlockSpec((L, 128), lambda c: (0, c))],
        scratch_shapes=[pltpu.VMEM((L + KWP, 128), F32)],
        params=_params("parallel"),
    )(vg, vg, dw, dwb)


def _conv_bwd(dz1, vg, dw, name):
    L = vg.shape[0]
    nc = DC // 128

    def body(dz_ref, v_ref, g_ref, dw_ref, dv_ref, dg_ref, ddw_ref, ddwb_ref, zp_s, dzp_s):
        vv = v_ref[...].astype(F32)
        sg = _sigmoid(g_ref[...].astype(F32))
        zp_s[0:KWP, :] = jnp.zeros((KWP, 128), F32)
        zp_s[KWP:, :] = vv * sg
        dz = dz_ref[...]
        dzp_s[0:L, :] = dz
        dzp_s[L:, :] = jnp.zeros((KWP, 128), F32)
        ddwb_ref[...] = jnp.sum(dz, axis=0, keepdims=True)
        part = [jnp.zeros((8, 128), F32) for _ in range(KW)]
        for r0 in range(0, L, CONV_ROWS):
            rows = pl.ds(r0, CONV_ROWS)
            dzc = dz_ref[rows, :]
            acc = jnp.zeros((CONV_ROWS, 128), F32)
            for k in range(KW):
                acc = acc + dw_ref[k:k + 1, :] * dzp_s[pl.ds(r0 + KW - 1 - k, CONV_ROWS), :]
                prod = dzc * zp_s[pl.ds(r0 + k + 2, CONV_ROWS), :]
                for q in range(CONV_ROWS // 8):
                    part[k] = part[k] + prod[8 * q:8 * (q + 1), :]
            vc = v_ref[rows, :].astype(F32)
            sc = _sigmoid(g_ref[rows, :].astype(F32))
            dv_ref[rows, :] = (acc * sc).astype(BF16)
            dg_ref[rows, :] = (acc * vc * sc * (1.0 - sc)).astype(BF16)
        for k in range(KW):
            ddw_ref[k:k + 1, :] = jnp.sum(part[k], axis=0, keepdims=True)
        ddw_ref[KW:KWP, :] = jnp.zeros((KWP - KW, 128), F32)

    col = pl.BlockSpec((L, 128), lambda c: (0, c))
    return pl.pallas_call(
        body, name=name, grid=(nc,),
        out_shape=[jax.ShapeDtypeStruct((L, DC), BF16), jax.ShapeDtypeStruct((L, DC), BF16),
                   jax.ShapeDtypeStruct((KWP, DC), F32), jax.ShapeDtypeStruct((1, DC), F32)],
        in_specs=[col, col, pl.BlockSpec((L, 128), lambda c: (0, nc + c)),
                  pl.BlockSpec((KWP, 128), lambda c: (0, c))],
        out_specs=[col, col, pl.BlockSpec((KWP, 128), lambda c: (0, c)), pl.BlockSpec((1, 128), lambda c: (0, c))],
        scratch_shapes=[pltpu.VMEM((L + KWP, 128), F32), pltpu.VMEM((L + KWP, 128), F32)],
        compiler_params=_params("parallel"),
    )(dz1, vg, vg, dw)


NLB = QS // 128


def _lb_store(ref, rows, val):
    for cb in range(NLB):
        ref[cb, rows, :] = val[:, cb * 128:(cb + 1) * 128]


def _lb_load(ref, rows):
    return jnp.concatenate([ref[cb, rows, :] for cb in range(NLB)], axis=1)


def _scan(xr_ref, xi_ref, base, T, ar, ai, atr, ati, reverse):
    W = ar.shape[1]
    ar, ai, atr, ati = (jnp.broadcast_to(v, (8, W)) for v in (ar, ai, atr, ati))
    zero = jnp.zeros((8, W), F32)

    def rows(t, g):
        tt = T - 1 - t if reverse else t
        return pl.ds(base + g * 8 * T + tt, 8, stride=T)

    def make_step(store):
        def step(t, carry):
            out = []
            for g in range(NGRP):
                sr, si = carry[2 * g], carry[2 * g + 1]
                idx = rows(t, g)
                nr = ar * sr - ai * si + _lb_load(xr_ref, idx)
                ni = ar * si + ai * sr + _lb_load(xi_ref, idx)
                if store:
                    _lb_store(xr_ref, idx, nr)
                    _lb_store(xi_ref, idx, ni)
                out += [nr, ni]
            return tuple(out)
        return step

    ends = lax.fori_loop(0, T, make_step(False), (zero,) * (2 * NGRP))
    sub = lax.broadcasted_iota(jnp.int32, (8, W), 0)
    edge = sub == (7 if reverse else 0)
    shift, last = (7, 0) if reverse else (1, 7)
    inr, ini = jnp.zeros((1, W), F32), jnp.zeros((1, W), F32)
    starts = [None] * (2 * NGRP)
    for g in (reversed(range(NGRP)) if reverse else range(NGRP)):
        er, ei = ends[2 * g], ends[2 * g + 1]
        cr, ci = jnp.where(edge, inr, 0.0), jnp.where(edge, ini, 0.0)
        for _ in range(7):
            nr = atr * cr - ati * ci + er
            ni = atr * ci + ati * cr + ei
            cr = jnp.where(edge, inr, pltpu.roll(nr, shift, 0))
            ci = jnp.where(edge, ini, pltpu.roll(ni, shift, 0))
        starts[2 * g], starts[2 * g + 1] = cr, ci
        inr = (atr * cr - ati * ci + er)[last:last + 1]
        ini = (atr * ci + ati * cr + ei)[last:last + 1]
    lax.fori_loop(0, T, make_step(True), tuple(starts))


def _ssm_fwd(uf, bre, bim, cre, cim, lamp, dsk, name, comm=None):
    L = uf.shape[0]
    T = L // NSEG
    tc = L // NCH

    def body(u_ref, bre_ref, bim_ref, cre_ref, cim_ref, lam_ref, d_ref, y_ref, sr_s, si_s):
        for k in range(NCH):
            sl = slice(k * tc, (k + 1) * tc)
            uk = u_ref[sl, :]
            _lb_store(sr_s, sl, _dot(uk, bre_ref[...]))
            _lb_store(si_s, sl, _dot(uk, bim_ref[...]))
        _scan(sr_s, si_s, 0, T, lam_ref[0:1, :], lam_ref[1:2, :], lam_ref[2:3, :], lam_ref[3:4, :], False)
        for k in range(NCH):
            sl = slice(k * tc, (k + 1) * tc)
            y_ref[sl, :] = (_dot(_lb_load(sr_s, sl).astype(BF16), cre_ref[...])
                            - _dot(_lb_load(si_s, sl).astype(BF16), cim_ref[...])
                            + d_ref[...] * u_ref[sl, :].astype(F32))

    return _call(
        body, comm, name=name, grid=(NQ,),
        out_shape=[jax.ShapeDtypeStruct((L, DS), F32)],
        in_specs=[pl.BlockSpec((L, QU), lambda q: (0, q)),
                  pl.BlockSpec((QU, QS), lambda q: (q, q)), pl.BlockSpec((QU, QS), lambda q: (q, q)),
                  pl.BlockSpec((QS, QU), lambda q: (q, q)), pl.BlockSpec((QS, QU), lambda q: (q, q)),
                  pl.BlockSpec((8, QS), lambda q: (0, q)), pl.BlockSpec((1, QU), lambda q: (0, q))],
        out_specs=[pl.BlockSpec((L, QU), lambda q: (0, q))],
        scratch_shapes=[pltpu.VMEM((NLB, L, 128), F32), pltpu.VMEM((NLB, L, 128), F32)],
        params=_params("parallel"),
    )(uf, bre, bim, cre, cim, lamp, dsk)


def _ssm_bwd(uf, dyss, bre, bim, cre, cim, lamp, dsk, name, comm=None):
    L = uf.shape[0]
    T = L // NSEG
    tc = L // NCH

    def body(u_ref, dy_ref, bre_ref, bim_ref, cre_ref, cim_ref, lam_ref, d_ref,
             du_ref, dbre_ref, dbim_ref, dcre_ref, dcim_ref, dlam_ref, dd_ref, sr_s, si_s, gr_s, gi_s):
        _lb_store(sr_s, slice(0, SOFF), jnp.zeros((SOFF, QS), F32))
        _lb_store(si_s, slice(0, SOFF), jnp.zeros((SOFF, QS), F32))
        for k in range(NCH):
            sl = slice(k * tc, (k + 1) * tc)
            ss = slice(SOFF + k * tc, SOFF + (k + 1) * tc)
            uk = u_ref[sl, :]
            dyk = dy_ref[sl, :].astype(BF16)
            _lb_store(sr_s, ss, _dot(uk, bre_ref[...]))
            _lb_store(si_s, ss, _dot(uk, bim_ref[...]))
            _lb_store(gr_s, sl, _dot_nt(dyk, cre_ref[...]))
            _lb_store(gi_s, sl, -_dot_nt(dyk, cim_ref[...]))
        ar, ai, atr, ati = lam_ref[0:1, :], lam_ref[1:2, :], lam_ref[2:3, :], lam_ref[3:4, :]
        _scan(sr_s, si_s, SOFF, T, ar, ai, atr, ati, False)
        _scan(gr_s, gi_s, 0, T, ar, -ai, atr, -ati, True)
        dbre = jnp.zeros((QU, QS), F32)
        dbim = jnp.zeros((QU, QS), F32)
        dcre = jnp.zeros((QS, QU), F32)
        dcim = jnp.zeros((QS, QU), F32)
        dd = jnp.zeros((1, QU), F32)
        qr = jnp.zeros((1, QS), F32)
        qi = jnp.zeros((1, QS), F32)
        for k in range(NCH):
            sl = slice(k * tc, (k + 1) * tc)
            ss = slice(SOFF + k * tc, SOFF + (k + 1) * tc)
            sp = slice(SOFF - 1 + k * tc, SOFF - 1 + (k + 1) * tc)
            uk = u_ref[sl, :]
            dyk = dy_ref[sl, :]
            dyb = dyk.astype(BF16)
            gr, gi = _lb_load(gr_s, sl), _lb_load(gi_s, sl)
            pr, pi = _lb_load(sr_s, sp), _lb_load(si_s, sp)
            qr = qr + jnp.sum(gr * pr + gi * pi, axis=0, keepdims=True)
            qi = qi + jnp.sum(gi * pr - gr * pi, axis=0, keepdims=True)
            grb, gib = gr.astype(BF16), gi.astype(BF16)
            du_ref[sl, :] = (_dot_nt(grb, bre_ref[...]) + _dot_nt(gib, bim_ref[...])
                             + dyk * d_ref[...]).astype(BF16)
            dbre = dbre + _dot_tn(uk, grb)
            dbim = dbim + _dot_tn(uk, gib)
            dcre = dcre + _dot_tn(_lb_load(sr_s, ss).astype(BF16), dyb)
            dcim = dcim - _dot_tn(_lb_load(si_s, ss).astype(BF16), dyb)
            dd = dd + jnp.sum(dyk * uk.astype(F32), axis=0, keepdims=True)
        dlam_ref[0] = jnp.concatenate([qr, qi, jnp.zeros((6, QS), F32)], axis=0)
        dbre_ref[0] = dbre
        dbim_ref[0] = dbim
        dcre_ref[0] = dcre
        dcim_ref[0] = dcim
        dd_ref[...] = dd

    col = pl.BlockSpec((L, QU), lambda q: (0, q))
    bsp = pl.BlockSpec((QU, QS), lambda q: (q, q))
    csp = pl.BlockSpec((QS, QU), lambda q: (q, q))
    return _call(
        body, comm, name=name, grid=(NQ,),
        out_shape=[jax.ShapeDtypeStruct((L, DS), BF16),
                   jax.ShapeDtypeStruct((NQ, QU, QS), F32), jax.ShapeDtypeStruct((NQ, QU, QS), F32),
                   jax.ShapeDtypeStruct((NQ, QS, QU), F32), jax.ShapeDtypeStruct((NQ, QS, QU), F32),
                   jax.ShapeDtypeStruct((NQ, 8, QS), F32), jax.ShapeDtypeStruct((1, DS), F32)],
        in_specs=[col, col, bsp, bsp, csp, csp,
                  pl.BlockSpec((8, QS), lambda q: (0, q)), pl.BlockSpec((1, QU), lambda q: (0, q))],
        out_specs=[col,
                   pl.BlockSpec((1, QU, QS), lambda q: (q, 0, 0)), pl.BlockSpec((1, QU, QS), lambda q: (q, 0, 0)),
                   pl.BlockSpec((1, QS, QU), lambda q: (q, 0, 0)), pl.BlockSpec((1, QS, QU), lambda q: (q, 0, 0)),
                   pl.BlockSpec((1, 8, QS), lambda q: (q, 0, 0)), pl.BlockSpec((1, QU), lambda q: (0, q))],
        scratch_shapes=[pltpu.VMEM((NLB, L + SOFF, 128), F32), pltpu.VMEM((NLB, L + SOFF, 128), F32),
                        pltpu.VMEM((NLB, L, 128), F32), pltpu.VMEM((NLB, L, 128), F32)],
        params=_params("parallel"),
    )(uf, dyss, bre, bim, cre, cim, lamp, dsk)


def _branches(z1_ref, yss_ref, gt_ref, lng_ref, lnb_ref, wp_ref, wv_ref, wg_ref):
    zf = z1_ref[...]
    mu = jnp.mean(zf, axis=-1, keepdims=True)
    zc = zf - mu
    rstd = lax.rsqrt(jnp.mean(zc * zc, axis=-1, keepdims=True) + EPS)
    zn = zc * rstd
    z2 = zn * lng_ref[...] + lnb_ref[...]
    sz = _sigmoid(z2)
    z3 = (z2 * sz).astype(BF16)
    y_conv = _dot(z3, wp_ref[...])
    yss = yss_ref[...]
    yg = _gelu(yss).astype(BF16)
    sv = _dot(yg, wv_ref[...])
    sig = _sigmoid(_dot(yg, wg_ref[...]))
    y_ssm = sv * sig
    gc = gt_ref[:, 0:D].astype(F32)
    gs = gt_ref[:, D:2 * D].astype(F32)
    m = gc * y_conv + gs * y_ssm
    return dict(rstd=rstd, zn=zn, z2=z2, sz=sz, z3=z3, y_conv=y_conv, yss=yss, yg=yg, sv=sv, sig=sig,
                y_ssm=y_ssm, gc=gc, gs=gs, m=m)


def _merge_fwd(h, z1, yss, gate, lng, lnb, wp, wv, wg, wo, name):
    L = h.shape[0]
    tm = _tile(L, 528)

    def body(h_ref, z1_ref, yss_ref, gt_ref, lng_ref, lnb_ref, wp_ref, wv_ref, wg_ref, wo_ref, o_ref):
        f = _branches(z1_ref, yss_ref, gt_ref, lng_ref, lnb_ref, wp_ref, wv_ref, wg_ref)
        o_ref[...] = h_ref[...] + _dot(f["m"].astype(BF16), wo_ref[...])

    def row(n):
        return pl.BlockSpec((tm, n), lambda i: (i, 0))

    return pl.pallas_call(
        body, name=name, grid=(L // tm,),
        out_shape=jax.ShapeDtypeStruct((L, D), F32),
        in_specs=[row(D), row(DC), row(DS), row(2 * D), _res((1, DC)), _res((1, DC)),
                  _res((DC, D)), _res((DS, D)), _res((DS, D)), _res((D, D))],
        out_specs=row(D),
        compiler_params=_params("parallel"),
    )(h, z1, yss, gate, lng, lnb, wp, wv, wg, wo)


def _merge_bwd(dh, z1, yss, gate, lng, lnb, wp, wv, wg, wo, name):
    L = dh.shape[0]
    tm = _tile(L, 352)

    def body(dh_ref, z1_ref, yss_ref, gt_ref, lng_ref, lnb_ref, wp_ref, wv_ref, wg_ref, wo_ref,
             m_ref, dgt_ref, dyc_ref, z3_ref, dz1_ref, yg_ref, dsv_ref, dsg_ref, dyss_ref,
             dbg_ref, dlng_ref, dlnb_ref):
        i = pl.program_id(0)
        f = _branches(z1_ref, yss_ref, gt_ref, lng_ref, lnb_ref, wp_ref, wv_ref, wg_ref)
        gc, gs, sig, sv = f["gc"], f["gs"], f["sig"], f["sv"]
        m_ref[...] = f["m"].astype(BF16)
        z3_ref[...] = f["z3"]
        yg_ref[...] = f["yg"]
        dm = _dot_nt(dh_ref[...].astype(BF16), wo_ref[...])
        dgc = (dm * f["y_conv"] * gc * (1.0 - gc)).astype(BF16)
        dgs = (dm * f["y_ssm"] * gs * (1.0 - gs)).astype(BF16)
        dgt_ref[:, 0:D] = dgc
        dgt_ref[:, D:2 * D] = dgs
        part = jnp.concatenate([jnp.sum(dgc.astype(F32), axis=0, keepdims=True),
                                jnp.sum(dgs.astype(F32), axis=0, keepdims=True)], axis=1)
        _acc_rows(dbg_ref, part, i == 0)
        dyc = (dm * gc).astype(BF16)
        dyc_ref[...] = dyc
        dys = dm * gs
        dsv = (dys * sig).astype(BF16)
        dsg = (dys * sv * sig * (1.0 - sig)).astype(BF16)
        dsv_ref[...] = dsv
        dsg_ref[...] = dsg
        dyg = _dot_nt(dsv, wv_ref[...]) + _dot_nt(dsg, wg_ref[...])
        dyss_ref[...] = dyg * _gelu_grad(f["yss"])
        dz3 = _dot_nt(dyc, wp_ref[...])
        z2, sz, zn = f["z2"], f["sz"], f["zn"]
        dz2 = dz3 * sz * (1.0 + z2 * (1.0 - sz))
        _acc_rows(dlng_ref, jnp.sum(dz2 * zn, axis=0, keepdims=True), i == 0)
        _acc_rows(dlnb_ref, jnp.sum(dz2, axis=0, keepdims=True), i == 0)
        dzn = dz2 * lng_ref[...]
        dz1_ref[...] = f["rstd"] * (dzn - jnp.mean(dzn, axis=-1, keepdims=True)
                                    - zn * jnp.mean(dzn * zn, axis=-1, keepdims=True))

    def row(n):
        return pl.BlockSpec((tm, n), lambda i: (i, 0))

    def tot(n):
        return pl.BlockSpec((1, n), lambda i: (0, 0))

    return pl.pallas_call(
        body, name=name, grid=(L // tm,),
        out_shape=[jax.ShapeDtypeStruct((L, D), BF16), jax.ShapeDtypeStruct((L, 2 * D), BF16),
                   jax.ShapeDtypeStruct((L, D), BF16), jax.ShapeDtypeStruct((L, DC), BF16),
                   jax.ShapeDtypeStruct((L, DC), F32), jax.ShapeDtypeStruct((L, DS), BF16),
                   jax.ShapeDtypeStruct((L, D), BF16), jax.ShapeDtypeStruct((L, D), BF16),
                   jax.ShapeDtypeStruct((L, DS), F32),
                   jax.ShapeDtypeStruct((1, 2 * D), F32), jax.ShapeDtypeStruct((1, DC), F32),
                   jax.ShapeDtypeStruct((1, DC), F32)],
        in_specs=[row(D), row(DC), row(DS), row(2 * D), _res((1, DC)), _res((1, DC)),
                  _res((DC, D)), _res((DS, D)), _res((DS, D)), _res((D, D))],
        out_specs=[row(D), row(2 * D), row(D), row(DC), row(DC), row(DS), row(D), row(D), row(DS),
                   tot(2 * D), tot(DC), tot(DC)],
        compiler_params=_params("arbitrary"),
    )(dh, z1, yss, gate, lng, lnb, wp, wv, wg, wo)


def _final(h, g, tgt, name):
    L = h.shape[0]
    tm = _tile(L, 528)

    def body(h_ref, g_ref, t_ref, dh_ref, loss_ref, dg_ref):
        i = pl.program_id(0)
        hv = h_ref[...]
        gv = g_ref[...]
        y, r = _rms(hv, gv)
        row = i * tm + lax.broadcasted_iota(jnp.int32, (tm, 1), 0)
        e = jnp.where(row >= FRONT, y - t_ref[...], 0.0)
        dy = e * (1.0 / D)
        part = 0.5 * jnp.sum(jnp.sum(e * dy, axis=1, keepdims=True), axis=0, keepdims=True)
        dx, xh = _rms_bwd(dy, hv, r, gv)
        dh_ref[...] = dx
        _acc_rows(loss_ref, part, i == 0)
        _acc_rows(dg_ref, jnp.sum(dy * xh, axis=0, keepdims=True), i == 0)

    row = pl.BlockSpec((tm, D), lambda i: (i, 0))
    return pl.pallas_call(
        body, name=name, grid=(L // tm,),
        out_shape=[jax.ShapeDtypeStruct((L, D), F32), jax.ShapeDtypeStruct((1, 1), F32),
                   jax.ShapeDtypeStruct((1, D), F32)],
        in_specs=[row, _res((1, D)), row],
        out_specs=[row, pl.BlockSpec((1, 1), lambda i: (0, 0)), pl.BlockSpec((1, D), lambda i: (0, 0))],
        compiler_params=_params("arbitrary"),
    )(h, g, tgt)


def _ssm_disc(lam_re, lam_im, log_dt, b_re, b_im):
    lam = lax.complex(lam_re, lam_im)
    dt = jnp.exp(log_dt)[:, None]
    lam_bar = jnp.exp(lam * dt)
    bbar = ((lam_bar - 1.0) / lam)[..., None] * lax.complex(b_re, b_im)
    return jnp.real(lam_bar), jnp.imag(lam_bar), jnp.real(bbar), jnp.imag(bbar)


def _bdiag_in(m):
    return jnp.einsum("gph,gk->ghkp", m, jnp.eye(G, dtype=m.dtype)).reshape(G * H, G * P)


def _bdiag_out(m):
    return jnp.einsum("ghp,gk->gpkh", m, jnp.eye(G, dtype=m.dtype)).reshape(G * P, G * H)


def _diag_blocks(m4):
    return jnp.einsum("qiaib->qiab", m4).reshape(G, m4.shape[2], m4.shape[4])


def _pack(parts, rows_mult=8):
    flat = jnp.concatenate([p.reshape(-1).astype(F32) for p in parts])
    n = flat.shape[0]
    tot = -(-n // (128 * rows_mult)) * (128 * rows_mult)
    return jnp.pad(flat, (0, tot - n)).reshape(tot // 128, 128)


def _unpack(buf, shapes):
    flat = buf.reshape(-1)
    out, o = [], 0
    for s in shapes:
        n = math.prod(s)
        out.append(flat[o:o + n].reshape(s))
        o += n
    return out


def kernel(x, meta_tokens, ffn1_norm, ffn1_w1, ffn1_w3, ffn1_w2, mix_norm, w_in, b_gate, conv_dw, conv_dw_b, conv_ln_g, conv_ln_b, conv_proj, ssm_lam_re, ssm_lam_im, ssm_log_dt, ssm_b_re, ssm_b_im, ssm_c_re, ssm_c_im, ssm_d, ssm_w_v, ssm_w_g, w_out, ffn2_norm, ffn2_w1, ffn2_w3, ffn2_w2, final_norm, loss_target, m_meta_tokens, m_ffn1_norm, m_ffn1_w1, m_ffn1_w3, m_ffn1_w2, m_mix_norm, m_w_in, m_b_gate, m_conv_dw, m_conv_dw_b, m_conv_ln_g, m_conv_ln_b, m_conv_proj, m_ssm_lam_re, m_ssm_lam_im, m_ssm_log_dt, m_ssm_b_re, m_ssm_b_im, m_ssm_c_re, m_ssm_c_im, m_ssm_d, m_ssm_w_v, m_ssm_w_g, m_w_out, m_ffn2_norm, m_ffn2_w1, m_ffn2_w3, m_ffn2_w2, m_final_norm, v_meta_tokens, v_ffn1_norm, v_ffn1_w1, v_ffn1_w3, v_ffn1_w2, v_mix_norm, v_w_in, v_b_gate, v_conv_dw, v_conv_dw_b, v_conv_ln_g, v_conv_ln_b, v_conv_proj, v_ssm_lam_re, v_ssm_lam_im, v_ssm_log_dt, v_ssm_b_re, v_ssm_b_im, v_ssm_c_re, v_ssm_c_im, v_ssm_d, v_ssm_w_v, v_ssm_w_g, v_w_out, v_ffn2_norm, v_ffn2_w1, v_ffn2_w3, v_ffn2_w2, v_final_norm):
    args = dict(locals())
    names = ["meta_tokens", "ffn1_norm", "ffn1_w1", "ffn1_w3", "ffn1_w2", "mix_norm", "w_in", "b_gate",
             "conv_dw", "conv_dw_b", "conv_ln_g", "conv_ln_b", "conv_proj", "ssm_lam_re", "ssm_lam_im",
             "ssm_log_dt", "ssm_b_re", "ssm_b_im", "ssm_c_re", "ssm_c_im", "ssm_d", "ssm_w_v", "ssm_w_g",
             "w_out", "ffn2_norm", "ffn2_w1", "ffn2_w3", "ffn2_w2", "final_norm"]
    big = ["ffn1_w1", "ffn1_w3", "ffn1_w2", "w_in", "conv_proj", "ssm_w_v", "ssm_w_g", "w_out",
           "ffn2_w1", "ffn2_w3", "ffn2_w2"]
    small = [n for n in names if n not in big]

    xs = x[0]
    S = xs.shape[0]
    L = FRONT + S
    T = L // NSEG
    jx, jy = lax.axis_index("x"), lax.axis_index("y")
    chip = 2 * jx + jy

    sm = _gather_all(_pack([meta_tokens, conv_dw[0]]), "gather_small")[0::2].reshape(NSH, -1)
    nmt = NMETA * (D // NSH)
    ndw = KW * (DC // NSH)
    meta_full = sm[:, :nmt].reshape(NSH, NMETA, D // NSH).transpose(1, 0, 2).reshape(NMETA, D)
    dw_full = sm[:, nmt:nmt + ndw].reshape(NSH, KW, DC // NSH).transpose(1, 0, 2).reshape(KW, DC)
    dw_pad = jnp.pad(dw_full, ((0, KWP - KW), (0, 0)))
    tposed = ("ffn1_w1", "ffn1_w3", "ffn2_w1", "ffn2_w3")

    def view(a, n):
        return jnp.swapaxes(a, 1, 2) if n in tposed else a

    grp_a = ["ffn1_w1", "ffn1_w3", "ffn1_w2"]
    grp_b = ["w_in", "conv_proj", "ssm_w_v", "ssm_w_g", "w_out"]
    grp_c = ["ffn2_w1", "ffn2_w3", "ffn2_w2"]

    def shard(n):
        return view(args[n], n)[0].astype(BF16)

    gw = dict(zip(grp_a, _gather_chips([shard(n) for n in grp_a], "gather_ffn1")))

    def cols(w):
        return w.transpose(1, 0, 2).reshape(w.shape[1], -1)

    disc_in = (ssm_lam_re[0], ssm_lam_im[0], ssm_log_dt[0], ssm_b_re[0], ssm_b_im[0])
    (lbr, lbi, bbr, bbi), disc_vjp = jax.vjp(_ssm_disc, *disc_in)
    lam_t = jnp.exp(lax.complex(ssm_lam_re[0], ssm_lam_im[0]) * (jnp.exp(ssm_log_dt[0])[:, None] * T))
    lamp = jnp.concatenate([lbr.reshape(1, NST), lbi.reshape(1, NST), jnp.real(lam_t).reshape(1, NST),
                            jnp.imag(lam_t).reshape(1, NST), jnp.zeros((4, NST), F32)], axis=0)
    bre_bd, bim_bd = _bdiag_in(bbr).astype(BF16), _bdiag_in(bbi).astype(BF16)
    cre_bd, cim_bd = _bdiag_out(ssm_c_re[0]).astype(BF16), _bdiag_out(ssm_c_im[0]).astype(BF16)

    h0 = jnp.concatenate([jnp.zeros((FRONT - NMETA, D), F32), meta_full, xs], axis=0)
    tgt = jnp.pad(loss_target[0], ((FRONT, 0), (0, 0)))
    (h1, a1, b1), got = _ffn_fwd(h0, ffn1_norm, gw["ffn1_w1"], gw["ffn1_w3"], gw["ffn1_w2"], "ffn1_fwd",
                                 _gather_half_behind([shard(n) for n in grp_b]))
    gw.update(zip(grp_b, _pass_halves(got, "pass_mix")))
    w_in_f = gw["w_in"]
    wp_f, wv_f, wg_f = cols(gw["conv_proj"]), cols(gw["ssm_w_v"]), cols(gw["ssm_w_g"])
    wo_f = gw["w_out"].reshape(D, D)
    (vg, uf, gate), got1 = _mix_in_fwd(h1, mix_norm, w_in_f, b_gate, "mix_in_fwd",
                                       _gather_half_behind([shard("ffn2_w1")]))
    (z1,), got3 = _conv_fwd(vg, dw_pad, conv_dw_b, "conv_fwd", _gather_half_behind([shard("ffn2_w3")]))
    (yss,), got2 = _ssm_fwd(uf, bre_bd, bim_bd, cre_bd, cim_bd, lamp, ssm_d, "ssm_fwd",
                            _gather_half_behind([shard("ffn2_w2")]))
    gw.update(zip(grp_c, _pass_halves([got1[0], got3[0], got2[0]], "pass_ffn2")))
    h2 = _merge_fwd(h1, z1, yss, gate, conv_ln_g, conv_ln_b, wp_f, wv_f, wg_f, wo_f, "merge_fwd")
    (h3, a2, b2), _ = _ffn_fwd(h2, ffn2_norm, gw["ffn2_w1"], gw["ffn2_w3"], gw["ffn2_w2"], "ffn2_fwd")

    gbig = {}
    core = lax.axis_index("c").astype(jnp.int32).reshape(1)

    def pair_sums(group, tag):
        gl = [gbig[n] for n in group]
        sib = _pair_exchange(gl, "pair_exchange_" + tag)
        out = []
        for n, g_, s_ in zip(group, gl, sib):
            out.append(_add_pair(g_, s_, core, "pair_" + n))
        return out

    dh3, loss_part, d_final = _final(h3, final_norm.reshape(1, D), tgt, "final")
    (dh2, da2, db2, s2, n2, d_ffn2_norm), _ = _ffn_bwd(
        h2, ffn2_norm, dh3, a2, b2, gw["ffn2_w1"], gw["ffn2_w3"], gw["ffn2_w2"], "ffn2_bwd")
    gbig["ffn2_w1"] = _wgrad(da2, n2, "ffn2_dw1")
    gbig["ffn2_w3"] = _wgrad(db2, n2, "ffn2_dw3")
    gbig["ffn2_w2"] = _wgrad(s2, dh3, "ffn2_dw2", 0.5)
    pair_c = pair_sums(grp_c, "ffn2")
    (m_b, dgate, dyc, z3, dz1, yg, dsv, dsg, dyss, d_b_gate, d_ln_g, d_ln_b) = _merge_bwd(
        dh2, z1, yss, gate, conv_ln_g, conv_ln_b, wp_f, wv_f, wg_f, wo_f, "merge_bwd")
    gbig["w_out"] = _wgrad(m_b, dh2, "dw_out").reshape(NSH, D // NSH, D)

    def shard_cols(gm):
        return gm.reshape(gm.shape[0], NSH, -1).transpose(1, 0, 2)

    gbig["conv_proj"] = shard_cols(_wgrad(z3, dyc, "dw_proj"))
    gbig["ssm_w_v"] = shard_cols(_wgrad(yg, dsv, "dw_v"))
    gbig["ssm_w_g"] = shard_cols(_wgrad(yg, dsg, "dw_g"))
    dv, dgl, ddw, d_dw_b = _conv_bwd(dz1, vg, dw_pad, "conv_bwd")
    (duf, dbre, dbim, dcre, dcim, dlam, d_ssm_d), recv_c = _ssm_bwd(
        uf, dyss, bre_bd, bim_bd, cre_bd, cim_bd, lamp, ssm_d, "ssm_bwd", _scatter_chips_behind(pair_c))
    dh1, u_b, dproj, d_mix_norm = _mix_in_bwd(h1, mix_norm, dh2, dv, dgl, duf, dgate, w_in_f, "mix_in_bwd")
    gbig["w_in"] = _wgrad(u_b, dproj, "dw_in")
    pair_b = pair_sums(grp_b, "mix")

    d_bbr = _diag_blocks(dbre.reshape(NQ, 8, H, 8, P)).transpose(0, 2, 1)
    d_bbi = _diag_blocks(dbim.reshape(NQ, 8, H, 8, P)).transpose(0, 2, 1)
    d_c_re = _diag_blocks(dcre.reshape(NQ, 8, P, 8, H)).transpose(0, 2, 1)
    d_c_im = _diag_blocks(dcim.reshape(NQ, 8, P, 8, H)).transpose(0, 2, 1)
    d_lbr = dlam[:, 0, :].reshape(G, P)
    d_lbi = dlam[:, 1, :].reshape(G, P)
    d_lam_re, d_lam_im, d_log_dt, d_b_re, d_b_im = disc_vjp((d_lbr, d_lbi, d_bbr, d_bbi))

    sg = {"mix_norm": d_mix_norm, "b_gate": d_b_gate, "conv_dw": ddw[:KW], "conv_dw_b": d_dw_b,
          "conv_ln_g": d_ln_g, "conv_ln_b": d_ln_b, "ssm_lam_re": d_lam_re, "ssm_lam_im": d_lam_im,
          "ssm_log_dt": d_log_dt, "ssm_b_re": d_b_re, "ssm_b_im": d_b_im, "ssm_c_re": d_c_re, "ssm_c_im": d_c_im,
          "ssm_d": d_ssm_d, "ffn2_norm": d_ffn2_norm, "final_norm": d_final}
    late = ["meta_tokens", "ffn1_norm"]
    early = [n for n in small if n not in late]

    (dh0, da1, db1, s1, n1, d_ffn1_norm), got = _ffn_bwd(
        h0, ffn1_norm, dh1, a1, b1, gw["ffn1_w1"], gw["ffn1_w3"], gw["ffn1_w2"], "ffn1_bwd",
        _join(_scatter_chips_behind(pair_b), _gather_all_behind(_pack([sg[n] for n in early]))))
    recv_b, early_all = got[:len(grp_b)], got[len(grp_b)]
    gbig["ffn1_w1"] = _wgrad(da1, n1, "ffn1_dw1")
    gbig["ffn1_w3"] = _wgrad(db1, n1, "ffn1_dw3")
    gbig["ffn1_w2"] = _wgrad(s1, dh1, "ffn1_dw2", 0.5)
    grad_x = dh0[FRONT:][None]
    recv_a = _scatter_chips(pair_sums(grp_a, "ffn1"), "scatter_ffn1")
    sg["meta_tokens"] = dh0[FRONT - NMETA:FRONT]
    sg["ffn1_norm"] = d_ffn1_norm

    recv = dict(zip(grp_a + grp_b + grp_c, list(recv_a) + list(recv_b) + list(recv_c)))
    halves = [_sum_slots(recv[n], "sum_" + n) for n in big]
    full = _swap_halves(halves, "swap_halves")
    out_g, out_d, out_m, out_v = {}, {}, {}, {}
    for n, f in zip(big, full):
        g3 = f.reshape(1, f.shape[0] * f.shape[1], f.shape[2])
        d3, m3, v3 = _adamw(view(args[n], n), g3, view(args["m_" + n], n), view(args["v_" + n], n), "adamw_" + n)
        out_g[n], out_d[n], out_m[n], out_v[n] = (view(t, n) for t in (g3, d3, m3, v3))

    late_all = _gather_all(_pack([sg[n] for n in late]), "gather_late_grads")
    sgr = dict(zip(early, _unpack(_sum_slots(early_all, "sum_early"), [sg[n].shape for n in early])))
    sgr.update(zip(late, _unpack(_sum_slots(late_all, "sum_late"), [sg[n].shape for n in late])))
    sgr["meta_tokens"] = lax.dynamic_slice_in_dim(sgr["meta_tokens"], chip * (D // NSH), D // NSH, axis=1)
    sgr["conv_dw"] = lax.dynamic_slice_in_dim(sgr["conv_dw"], chip * (DC // NSH), DC // NSH, axis=1)
    pshapes = [args[n].shape for n in small]
    d_s, m_s, v_s = _adamw(_pack([args[n] for n in small])[None], _pack([sgr[n] for n in small])[None],
                           _pack([args["m_" + n] for n in small])[None],
                           _pack([args["v_" + n] for n in small])[None], "adamw_small")
    for n, g_, d_, m_, v_ in zip(small, [sgr[n] for n in small], _unpack(d_s[0], pshapes),
                                 _unpack(m_s[0], pshapes), _unpack(v_s[0], pshapes)):
        out_g[n], out_d[n], out_m[n], out_v[n] = g_.reshape(args[n].shape), d_, m_, v_

    loss = lax.psum(loss_part[0, 0], ("x", "y", "c"))
    return (loss, grad_x, *[out_g[n] for n in names], *[out_d[n] for n in names],
            *[out_m[n] for n in names], *[out_v[n] for n in names])
```

```python
import math

import jax
import jax.numpy as jnp
from jax import lax
from jax.experimental import pallas as pl
from jax.experimental.pallas import tpu as pltpu

F32 = jnp.float32
BF16 = jnp.bfloat16

D = 1024
NSH = 4
F = 2816
FS = F // NSH
DC = 512
DS = 512
DIN = 2 * DC + DS + 2 * D
WS = DIN // NSH
KW = 31
KWP = 32
CONV_ROWS = 64
NMETA = 16
FRONT = 128
G, P, H = 32, 64, 16
NST = G * P
NQ = 4
QS = NST // NQ
QU = DS // NQ
NSEG = 32
NGRP = NSEG // 8
NCH = 8
SOFF = 8
EPS = 1e-6
LR, B1, B2, AEPS, WD, STEP = 1e-3, 0.9, 0.999, 1e-8, 0.01, 10
VMEM_LIMIT = 58 * 1024 * 1024
MESH = pl.DeviceIdType.MESH
ANY = pl.BlockSpec(memory_space=pl.ANY)


def _params(*sem):
    return pltpu.CompilerParams(dimension_semantics=sem, vmem_limit_bytes=VMEM_LIMIT)


def _res(shape):
    nd = len(shape)
    return pl.BlockSpec(shape, lambda *_: (0,) * nd, pipeline_mode=pl.Buffered(1))


def _tile(n, cap, mult=16):
    best = None
    for t in range(mult, min(n, cap) + 1, mult):
        if n % t == 0:
            best = t
    assert best is not None, (n, cap, mult)
    return best


def _dot(a, b):
    return jnp.dot(a, b, preferred_element_type=F32)


def _dot_nt(a, b):
    return lax.dot_general(a, b, (((1,), (1,)), ((), ())), preferred_element_type=F32)


def _dot_tn(a, b):
    return lax.dot_general(a, b, (((0,), (0,)), ((), ())), preferred_element_type=F32)


def _sigmoid(x):
    return 1.0 / (1.0 + jnp.exp(-x))


_GC = math.sqrt(2.0 / math.pi)
_GA = 0.044715


def _gelu(x):
    return 0.5 * x * (1.0 + jnp.tanh(_GC * (x + _GA * x * x * x)))


def _gelu_grad(x):
    t = jnp.tanh(_GC * (x + _GA * x * x * x))
    return 0.5 * (1.0 + t) + 0.5 * x * (1.0 - t * t) * _GC * (1.0 + 3.0 * _GA * x * x)


def _rms(hv, g):
    r = lax.rsqrt(jnp.mean(hv * hv, axis=-1, keepdims=True) + EPS)
    return hv * r * g, r


def _rms_bwd(dn, hv, r, g):
    xh = hv * r
    dxh = dn * g
    return r * (dxh - xh * jnp.mean(dxh * xh, axis=-1, keepdims=True)), xh


def _acc_rows(ref, part, first):
    @pl.when(first)
    def _():
        ref[...] = part

    @pl.when(jnp.logical_not(first))
    def _():
        ref[...] += part


def _coords():
    return lax.axis_index("x"), lax.axis_index("y"), lax.axis_index("c")


def _flip(v, d):
    return 1 - v if d else v


def _run(local, remote):
    for cp in local + remote:
        cp.start()
    for cp in remote:
        cp.wait()
    for cp in local:
        cp.wait()


def _via_vmem(src, dst, stage, sems, i):
    return (pltpu.make_async_copy(src, stage, sems.at[2 * i]), pltpu.make_async_copy(stage, dst, sems.at[2 * i + 1]))


def _run_staged(staged, remote):
    for load, _ in staged:
        load.start()
    for cp in remote:
        cp.start()
    for load, store in staged:
        load.wait()
        store.start()
    for cp in remote:
        cp.wait()
    for _, store in staged:
        store.wait()


_REL3 = ((1, 0), (0, 1), (1, 1))


class _Behind:
    def __init__(self, arrays, out_shapes, scratch, build):
        self.arrays, self.out_shapes, self.scratch, self.build = list(arrays), list(out_shapes), list(scratch), build

    def start(self, ins, outs, scr):
        staged, remote = self.build(ins, outs, scr)
        for load, _ in staged:
            load.start()
        for cp in remote:
            cp.start()

    def finish(self, ins, outs, scr):
        staged, remote = self.build(ins, outs, scr)
        for load, store in staged:
            load.wait()
            store.start()
        for cp in remote:
            cp.wait()
        for _, store in staged:
            store.wait()


def _call(body, comm, *, name, grid, in_specs, out_specs, out_shape, scratch_shapes=(), params):
    in_specs, out_specs, out_shape = list(in_specs), list(out_specs), list(out_shape)
    scratch_shapes = list(scratch_shapes)
    if comm is None:
        f = pl.pallas_call(body, name=name, grid=grid, in_specs=in_specs, out_specs=out_specs,
                           out_shape=out_shape, scratch_shapes=scratch_shapes, compiler_params=params)
        return lambda *args: (f(*args), [])
    ni, no, ns = len(in_specs), len(out_specs), len(scratch_shapes)
    ci, co = len(comm.arrays), len(comm.out_shapes)

    def hosted(*refs):
        ins, cin = refs[:ni], refs[ni:ni + ci]
        outs, cout = refs[ni + ci:ni + ci + no], refs[ni + ci + no:ni + ci + no + co]
        scr, cscr = refs[ni + ci + no + co:ni + ci + no + co + ns], refs[ni + ci + no + co + ns:]
        first = last = None
        for axis, size in enumerate(grid):
            i = pl.program_id(axis)
            first = (i == 0) if first is None else jnp.logical_and(first, i == 0)
            last = (i == size - 1) if last is None else jnp.logical_and(last, i == size - 1)

        @pl.when(first)
        def _():
            comm.start(cin, cout, cscr)

        body(*ins, *outs, *scr)

        @pl.when(last)
        def _():
            comm.finish(cin, cout, cscr)

    f = pl.pallas_call(hosted, name=name, grid=grid, in_specs=in_specs + [ANY] * ci,
                       out_specs=out_specs + [ANY] * co, out_shape=out_shape + comm.out_shapes,
                       scratch_shapes=scratch_shapes + comm.scratch,
                       compiler_params=_params(*(("arbitrary",) * len(grid))))

    def run(*args):
        res = f(*args, *comm.arrays)
        return res[:no], res[no:]

    return run


def _gather_half_behind(shards):
    n = len(shards)

    def build(ins, outs, scr):
        send, recv, loc = scr[:3]
        stage = scr[3:]
        x, y, c = _coords()
        me = 2 * x + y
        staged = [_via_vmem(ins[t], outs[t].at[me], stage[t], loc, t) for t in range(n)]
        remote = []
        for t in range(n):
            half = shards[t].shape[0] // 2
            mine = pl.ds(c * half, half)
            for k, (dx, dy) in enumerate(_REL3):
                remote.append(pltpu.make_async_remote_copy(
                    src_ref=ins[t].at[mine], dst_ref=outs[t].at[me, mine],
                    send_sem=send.at[3 * t + k], recv_sem=recv.at[3 * t + k],
                    device_id=(_flip(x, dx), _flip(y, dy), c), device_id_type=MESH))
        return staged, remote

    return _Behind(shards, [jax.ShapeDtypeStruct((NSH,) + s.shape, s.dtype) for s in shards],
                   [pltpu.SemaphoreType.DMA((3 * n,)), pltpu.SemaphoreType.DMA((3 * n,)),
                    pltpu.SemaphoreType.DMA((2 * n,))] + [pltpu.VMEM(s.shape, s.dtype) for s in shards], build)


def _pass_halves(gathered, name, own=()):
    n, m = len(gathered), len(own)

    def body(*refs):
        shards, outs = refs[n:n + m], refs[n + m:2 * n + m]
        send, recv, loc = refs[2 * n + m:2 * n + m + 3]
        stage = refs[2 * n + m + 3:]
        x, y, c = _coords()
        staged = [_via_vmem(shards[t], outs[t].at[2 * x + y], stage[t], loc, t) for t in range(m)]
        remote = []
        for t in range(n):
            half = gathered[t].shape[1] // 2
            mine = pl.ds(c * half, half)
            for k, (dx, dy) in enumerate(_REL3):
                slot = 2 * _flip(x, dx) + _flip(y, dy)
                remote.append(pltpu.make_async_remote_copy(
                    src_ref=outs[t].at[slot, mine], dst_ref=outs[t].at[slot, mine],
                    send_sem=send.at[3 * t + k], recv_sem=recv.at[3 * t + k],
                    device_id=(x, y, 1 - c), device_id_type=MESH))
        _run_staged(staged, remote)

    return pl.pallas_call(
        body, name=name,
        out_shape=[jax.ShapeDtypeStruct(g.shape, g.dtype) for g in gathered],
        in_specs=[ANY] * (n + m), out_specs=[ANY] * n, input_output_aliases={t: t for t in range(n)},
        scratch_shapes=[pltpu.SemaphoreType.DMA((3 * n,)), pltpu.SemaphoreType.DMA((3 * n,)),
                        pltpu.SemaphoreType.DMA((max(2 * m, 1),))] + [pltpu.VMEM(s.shape, s.dtype) for s in own],
        compiler_params=pltpu.CompilerParams(vmem_limit_bytes=VMEM_LIMIT),
    )(*gathered, *own)


def _fill_own(sums, recvs, name):
    n = len(sums)

    def body(*refs):
        ins, outs = refs[:n], refs[2 * n:3 * n]
        loc = refs[3 * n]
        stage = refs[3 * n + 1:]
        x, y, _ = _coords()
        me = 2 * x + y
        _run_staged([_via_vmem(ins[t].at[me], outs[t].at[me], stage[t], loc, t) for t in range(n)], [])

    return pl.pallas_call(
        body, name=name,
        out_shape=[jax.ShapeDtypeStruct(r.shape, r.dtype) for r in recvs],
        in_specs=[ANY] * (2 * n), out_specs=[ANY] * n, input_output_aliases={n + t: t for t in range(n)},
        scratch_shapes=[pltpu.SemaphoreType.DMA((2 * n,))] + [pltpu.VMEM(s.shape[1:], s.dtype) for s in sums],
        compiler_params=pltpu.CompilerParams(vmem_limit_bytes=VMEM_LIMIT),
    )(*sums, *recvs)


def _scatter_chips_behind(sums):
    n = len(sums)

    def build(ins, outs, scr):
        send, recv, loc = scr[:3]
        stage = scr[3:]
        x, y, c = _coords()
        me = 2 * x + y
        staged = [_via_vmem(ins[t].at[me], outs[t].at[me], stage[t], loc, t) for t in range(n)]
        remote = []
        for t in range(n):
            for k, (dx, dy) in enumerate(_REL3):
                px, py = _flip(x, dx), _flip(y, dy)
                remote.append(pltpu.make_async_remote_copy(
                    src_ref=ins[t].at[2 * px + py], dst_ref=outs[t].at[me],
                    send_sem=send.at[3 * t + k], recv_sem=recv.at[3 * t + k],
                    device_id=(px, py, c), device_id_type=MESH))
        return staged, remote

    return _Behind(sums, [jax.ShapeDtypeStruct(s.shape, s.dtype) for s in sums],
                   [pltpu.SemaphoreType.DMA((3 * n,)), pltpu.SemaphoreType.DMA((3 * n,)),
                    pltpu.SemaphoreType.DMA((2 * n,))] + [pltpu.VMEM(s.shape[1:], s.dtype) for s in sums], build)


def _gather_all_behind(a):
    def build(ins, outs, scr):
        send, recv, loc, stage = scr
        x, y, c = _coords()
        me = 4 * x + 2 * y + c
        staged = [_via_vmem(ins[0], outs[0].at[me], stage, loc, 0)]
        remote = [pltpu.make_async_remote_copy(
            src_ref=ins[0], dst_ref=outs[0].at[me], send_sem=send.at[k], recv_sem=recv.at[k],
            device_id=(_flip(x, dx), _flip(y, dy), _flip(c, dc)), device_id_type=MESH)
            for k, (dx, dy, dc) in enumerate(_REL7)]
        return staged, remote

    return _Behind([a], [jax.ShapeDtypeStruct((8,) + a.shape, a.dtype)],
                   [pltpu.SemaphoreType.DMA((7,)), pltpu.SemaphoreType.DMA((7,)), pltpu.SemaphoreType.DMA((2,)),
                    pltpu.VMEM(a.shape, a.dtype)], build)


HBM = pl.BlockSpec(memory_space=pltpu.HBM)
SEM = pl.BlockSpec(memory_space=pltpu.SEMAPHORE)
EFFECT = pltpu.SideEffectType.DATAFLOW_SIDE_EFFECTING


def _chip_copies(srcs, lands, send, recv, gather):
    x, y, c = _coords()
    me = 2 * x + y
    cps = []
    for t in range(len(srcs)):
        for k, (dx, dy) in enumerate(_REL3):
            px, py = _flip(x, dx), _flip(y, dy)
            if gather:
                half = srcs[t].shape[0] // 2
                mine = pl.ds(c * half, half)
                src, dst = srcs[t].at[mine], lands[t].at[me, mine]
            else:
                src, dst = srcs[t].at[2 * px + py], lands[t].at[me]
            cps.append(pltpu.make_async_remote_copy(
                src_ref=src, dst_ref=dst, send_sem=send.at[3 * t + k], recv_sem=recv.at[3 * t + k],
                device_id=(px, py, c), device_id_type=MESH))
    return cps


def _chips_start(arrays, land_shapes, gather, name):
    n = len(arrays)

    def body(*refs):
        srcs, lands = refs[:n], refs[n:2 * n]
        send, recv = refs[2 * n], refs[2 * n + 1]
        token = refs[-1]
        for cp in _chip_copies(srcs, lands, send, recv, gather):
            cp.start()
        token[...] = jnp.zeros_like(token)

    lands = [lax.empty(s.shape, s.dtype) for s in land_shapes]
    thru = [pltpu.HBM(a.shape, a.dtype) for a in arrays] + [pltpu.HBM(s.shape, s.dtype) for s in land_shapes]
    res = pl.pallas_call(
        body, name=name,
        out_shape=(pltpu.SemaphoreType.DMA((3 * n,)), pltpu.SemaphoreType.DMA((3 * n,)), *thru,
                   jax.ShapeDtypeStruct((8, 128), F32)),
        in_specs=[HBM] * (2 * n),
        out_specs=(SEM, SEM, *([HBM] * (2 * n)), pl.BlockSpec(memory_space=pltpu.VMEM)),
        input_output_aliases={t: 2 + t for t in range(2 * n)},
        compiler_params=pltpu.CompilerParams(has_side_effects=EFFECT),
    )(*[pltpu.with_memory_space_constraint(a, pltpu.HBM) for a in arrays],
      *[pltpu.with_memory_space_constraint(z, pltpu.HBM) for z in lands])
    return res[0], res[1], list(res[2:2 + n]), list(res[2 + n:2 + 2 * n]), res[-1]


def _chips_wait(send, recv, arrays, lands, after, gather, name):
    n = len(arrays)

    def body(*refs):
        srcs, ls = refs[:n], refs[n:2 * n]
        sd, rv = refs[2 * n], refs[2 * n + 1]
        for cp in _chip_copies(srcs, ls, sd, rv, gather):
            cp.wait_send()
            cp.wait_recv()

    res = pl.pallas_call(
        body, name=name,
        out_shape=[pltpu.HBM(a.shape, a.dtype) for a in arrays] + [pltpu.HBM(z.shape, z.dtype) for z in lands],
        in_specs=[HBM] * (2 * n) + [SEM, SEM] + [ANY] * len(after), out_specs=[HBM] * (2 * n),
        input_output_aliases={t: t for t in range(2 * n)},
        compiler_params=pltpu.CompilerParams(has_side_effects=EFFECT),
    )(*arrays, *lands, send, recv, *after)
    return list(res[:n]), list(res[n:])


def _join(*parts):
    def cut(seq, key):
        res, o = [], 0
        for p in parts:
            k = len(getattr(p, key))
            res.append(seq[o:o + k])
            o += k
        return res

    def build(ins, outs, scr):
        staged, remote = [], []
        for p, i, o, s in zip(parts, cut(ins, "arrays"), cut(outs, "out_shapes"), cut(scr, "scratch")):
            st, rm = p.build(i, o, s)
            staged += st
            remote += rm
        return staged, remote

    return _Behind(sum((p.arrays for p in parts), []), sum((p.out_shapes for p in parts), []),
                   sum((p.scratch for p in parts), []), build)


def _gather_chips(shards, name):
    n = len(shards)

    def body(*refs):
        ins, outs = refs[:n], refs[n:2 * n]
        send, recv, fsend, frecv, loc = refs[2 * n:2 * n + 5]
        stage = refs[2 * n + 5:]
        x, y, c = _coords()
        me = 2 * x + y
        own = [_via_vmem(ins[t], outs[t].at[me], stage[t], loc, t) for t in range(n)]
        first, passed = [], []
        for t in range(n):
            half = shards[t].shape[0] // 2
            mine, theirs = pl.ds(c * half, half), pl.ds((1 - c) * half, half)
            for k, (dx, dy) in enumerate(_REL3):
                px, py = _flip(x, dx), _flip(y, dy)
                first.append(pltpu.make_async_remote_copy(
                    src_ref=ins[t].at[mine], dst_ref=outs[t].at[me, mine],
                    send_sem=send.at[3 * t + k], recv_sem=recv.at[3 * t + k],
                    device_id=(px, py, c), device_id_type=MESH))
                passed.append((
                    pltpu.make_async_remote_copy(
                        src_ref=outs[t].at[2 * px + py, mine], dst_ref=outs[t].at[2 * px + py, mine],
                        send_sem=fsend.at[3 * t + k], recv_sem=frecv.at[3 * t + k],
                        device_id=(x, y, 1 - c), device_id_type=MESH),
                    pltpu.make_async_remote_copy(
                        src_ref=outs[t].at[2 * px + py, theirs], dst_ref=outs[t].at[2 * px + py, theirs],
                        send_sem=fsend.at[3 * t + k], recv_sem=frecv.at[3 * t + k],
                        device_id=(x, y, 1 - c), device_id_type=MESH)))
        for load, _ in own:
            load.start()
        for cp in first:
            cp.start()
        for load, store in own:
            load.wait()
            store.start()
        for cp, (fwd, _) in zip(first, passed):
            cp.wait_recv()
            fwd.start()
        for cp, (fwd, back) in zip(first, passed):
            cp.wait_send()
            fwd.wait_send()
            back.wait_recv()
        for _, store in own:
            store.wait()

    return pl.pallas_call(
        body, name=name,
        out_shape=[jax.ShapeDtypeStruct((NSH,) + s.shape, s.dtype) for s in shards],
        in_specs=[ANY] * n, out_specs=[ANY] * n,
        scratch_shapes=[pltpu.SemaphoreType.DMA((3 * n,)) for _ in range(4)] + [pltpu.SemaphoreType.DMA((2 * n,))]
        + [pltpu.VMEM(s.shape, s.dtype) for s in shards],
        compiler_params=pltpu.CompilerParams(vmem_limit_bytes=VMEM_LIMIT),
    )(*shards)


_REL7 = tuple((dx, dy, dc) for dx in (0, 1) for dy in (0, 1) for dc in (0, 1))[1:]


def _gather_all(a, name):
    def body(a_ref, o_ref, send, recv, loc):
        x, y, c = _coords()
        me = 4 * x + 2 * y + c
        local = [pltpu.make_async_copy(a_ref, o_ref.at[me], loc.at[0])]
        remote = [pltpu.make_async_remote_copy(
            src_ref=a_ref, dst_ref=o_ref.at[me], send_sem=send.at[k], recv_sem=recv.at[k],
            device_id=(_flip(x, dx), _flip(y, dy), _flip(c, dc)), device_id_type=MESH)
            for k, (dx, dy, dc) in enumerate(_REL7)]
        _run(local, remote)

    return pl.pallas_call(
        body, name=name,
        out_shape=jax.ShapeDtypeStruct((8,) + a.shape, a.dtype),
        in_specs=[ANY], out_specs=ANY,
        scratch_shapes=[pltpu.SemaphoreType.DMA((7,)), pltpu.SemaphoreType.DMA((7,)),
                        pltpu.SemaphoreType.DMA((1,))],
    )(a)


def _pair_exchange(grads, name):
    n = len(grads)

    def body(*refs):
        ins, outs = refs[:n], refs[n:2 * n]
        send, recv = refs[2 * n:]
        x, y, c = _coords()
        remote = []
        for t in range(n):
            half = grads[t].shape[1] // 2
            remote.append(pltpu.make_async_remote_copy(
                src_ref=ins[t].at[:, pl.ds((1 - c) * half, half)], dst_ref=outs[t],
                send_sem=send.at[t], recv_sem=recv.at[t],
                device_id=(x, y, 1 - c), device_id_type=MESH))
        _run([], remote)

    return pl.pallas_call(
        body, name=name,
        out_shape=[jax.ShapeDtypeStruct((NSH, g.shape[1] // 2, g.shape[2]), g.dtype) for g in grads],
        in_specs=[ANY] * n, out_specs=[ANY] * n,
        scratch_shapes=[pltpu.SemaphoreType.DMA((n,)), pltpu.SemaphoreType.DMA((n,))],
    )(*grads)


def _scatter_chips(sums, name):
    n = len(sums)

    def body(*refs):
        ins, outs = refs[:n], refs[n:2 * n]
        send, recv, loc = refs[2 * n:2 * n + 3]
        stage = refs[2 * n + 3:]
        x, y, c = _coords()
        me = 2 * x + y
        local = [_via_vmem(ins[t].at[me], outs[t].at[me], stage[t], loc, t) for t in range(n)]
        remote = []
        for t in range(n):
            for k, (dx, dy) in enumerate(_REL3):
                px, py = _flip(x, dx), _flip(y, dy)
                remote.append(pltpu.make_async_remote_copy(
                    src_ref=ins[t].at[2 * px + py], dst_ref=outs[t].at[me],
                    send_sem=send.at[3 * t + k], recv_sem=recv.at[3 * t + k],
                    device_id=(px, py, c), device_id_type=MESH))
        _run_staged(local, remote)

    return pl.pallas_call(
        body, name=name,
        out_shape=[jax.ShapeDtypeStruct(s.shape, s.dtype) for s in sums],
        in_specs=[ANY] * n, out_specs=[ANY] * n,
        scratch_shapes=[pltpu.SemaphoreType.DMA((3 * n,)), pltpu.SemaphoreType.DMA((3 * n,)),
                        pltpu.SemaphoreType.DMA((2 * n,))]
        + [pltpu.VMEM(s.shape[1:], s.dtype) for s in sums],
        compiler_params=pltpu.CompilerParams(vmem_limit_bytes=VMEM_LIMIT),
    )(*sums)


def _swap_halves(halves, name):
    n = len(halves)

    def body(*refs):
        ins, outs = refs[:n], refs[n:2 * n]
        send, recv, loc = refs[2 * n:2 * n + 3]
        stage = refs[2 * n + 3:]
        x, y, c = _coords()
        local = [_via_vmem(ins[t], outs[t].at[c], stage[t], loc, t) for t in range(n)]
        remote = [pltpu.make_async_remote_copy(
            src_ref=ins[t], dst_ref=outs[t].at[c], send_sem=send.at[t], recv_sem=recv.at[t],
            device_id=(x, y, 1 - c), device_id_type=MESH) for t in range(n)]
        _run_staged(local, remote)

    return pl.pallas_call(
        body, name=name,
        out_shape=[jax.ShapeDtypeStruct((2,) + h.shape, h.dtype) for h in halves],
        in_specs=[ANY] * n, out_specs=[ANY] * n,
        scratch_shapes=[pltpu.SemaphoreType.DMA((n,)), pltpu.SemaphoreType.DMA((n,)),
                        pltpu.SemaphoreType.DMA((2 * n,))]
        + [pltpu.VMEM(h.shape, h.dtype) for h in halves],
        compiler_params=pltpu.CompilerParams(vmem_limit_bytes=VMEM_LIMIT),
    )(*halves)


def _sum_slots(r, name, after=None):
    K, R, C = r.shape
    tr = _tile(R, max(16, (1 << 22) // (K * C)), 8 * (4 // r.dtype.itemsize))

    def body(r_ref, *rest):
        o_ref = rest[-1]
        acc = r_ref[0].astype(F32)
        for k in range(1, K):
            acc = acc + r_ref[k].astype(F32)
        o_ref[...] = acc

    dep = [] if after is None else [after]
    return pl.pallas_call(
        body, name=name, grid=(R // tr,),
        out_shape=jax.ShapeDtypeStruct((R, C), F32),
        in_specs=[pl.BlockSpec((K, tr, C), lambda i: (0, i, 0))] + [ANY] * len(dep),
        out_specs=pl.BlockSpec((tr, C), lambda i: (i, 0)),
        compiler_params=_params("parallel"),
    )(r, *dep)


def _add_pair(g, s, core, name):
    _, half, C = s.shape
    tr = _tile(half, max(16, (1 << 19) // C))
    nb = half // tr

    def body(c_ref, g_ref, s_ref, o_ref):
        o_ref[...] = (g_ref[...].astype(F32) + s_ref[...].astype(F32)).astype(BF16)

    spec = pl.BlockSpec((1, tr, C), lambda j, i, c_ref: (j, i, 0))
    return pl.pallas_call(
        body, name=name,
        grid_spec=pltpu.PrefetchScalarGridSpec(
            num_scalar_prefetch=1, grid=(NSH, nb),
            in_specs=[pl.BlockSpec((1, tr, C), lambda j, i, c_ref: (j, c_ref[0] * nb + i, 0)), spec],
            out_specs=spec),
        out_shape=jax.ShapeDtypeStruct(s.shape, BF16),
        compiler_params=_params("parallel", "parallel"),
    )(core, g, s)


def _adamw(w, g, m, v, name):
    _, R, C = w.shape
    tr = _tile(R, max(8, (1 << 18) // C), 8)
    c1 = 1.0 / (1.0 - B1 ** STEP)
    c2 = 1.0 / (1.0 - B2 ** STEP)

    def body(w_ref, g_ref, m_ref, v_ref, d_ref, nm_ref, nv_ref):
        gv = g_ref[...]
        nm = B1 * m_ref[...] + (1.0 - B1) * gv
        nv = B2 * v_ref[...] + (1.0 - B2) * gv * gv
        nm_ref[...] = nm
        nv_ref[...] = nv
        d_ref[...] = -LR * ((nm * c1) / (jnp.sqrt(nv * c2) + AEPS) + WD * w_ref[...])

    spec = pl.BlockSpec((1, tr, C), lambda i: (0, i, 0))
    return pl.pallas_call(
        body, name=name, grid=(R // tr,),
        out_shape=[jax.ShapeDtypeStruct((1, R, C), F32)] * 3,
        in_specs=[spec] * 4, out_specs=[spec] * 3,
        compiler_params=_params("parallel"),
    )(w, g, m, v)


def _ffn_fwd(h, g, w1, w3, w2, name, comm=None):
    L = h.shape[0]
    tm = _tile(L, 704)

    def body(h_ref, g_ref, w1_ref, w3_ref, w2_ref, o_ref, a_ref, b_ref, n_s, acc_s):
        j = pl.program_id(1)

        @pl.when(j == 0)
        def _():
            hv = h_ref[...]
            n, _ = _rms(hv, g_ref[...])
            n_s[...] = n.astype(BF16)
            acc_s[...] = hv

        n = n_s[...]
        a = _dot_nt(n, w1_ref[0])
        b = _dot_nt(n, w3_ref[0])
        a_ref[0] = a.astype(BF16)
        b_ref[0] = b.astype(BF16)
        s = (a * _sigmoid(a) * b).astype(BF16)
        acc_s[...] += 0.5 * _dot(s, w2_ref[0])

        @pl.when(j == NSH - 1)
        def _():
            o_ref[...] = acc_s[...]

    row = pl.BlockSpec((tm, D), lambda i, j: (i, 0))
    hid = pl.BlockSpec((1, tm, FS), lambda i, j: (j, i, 0))
    wsp = pl.BlockSpec((1, FS, D), lambda i, j: (j, 0, 0))
    return _call(
        body, comm, name=name, grid=(L // tm, NSH),
        out_shape=[jax.ShapeDtypeStruct((L, D), F32),
                   jax.ShapeDtypeStruct((NSH, L, FS), BF16), jax.ShapeDtypeStruct((NSH, L, FS), BF16)],
        in_specs=[row, _res((1, D)), wsp, wsp, wsp],
        out_specs=[row, hid, hid],
        scratch_shapes=[pltpu.VMEM((tm, D), BF16), pltpu.VMEM((tm, D), F32)],
        params=_params("arbitrary", "arbitrary"),
    )(h, g, w1, w3, w2)


def _ffn_bwd(h, g, dout, a, b, w1, w3, w2, name, comm=None):
    L = h.shape[0]
    tm = _tile(L, 528)

    def body(h_ref, g_ref, do_ref, a_ref, b_ref, w1_ref, w3_ref, w2_ref,
             dh_ref, da_ref, db_ref, s_ref, n_ref, dg_ref, dob_s, dn_s):
        i, j = pl.program_id(0), pl.program_id(1)

        @pl.when(j == 0)
        def _():
            n, _ = _rms(h_ref[...], g_ref[...])
            n_ref[...] = n.astype(BF16)
            dob_s[...] = do_ref[...].astype(BF16)
            dn_s[...] = jnp.zeros_like(dn_s)

        av = a_ref[0].astype(F32)
        bv = b_ref[0].astype(F32)
        sig = _sigmoid(av)
        sa = av * sig
        ds = 0.5 * _dot_nt(dob_s[...], w2_ref[0])
        s_ref[0] = (sa * bv).astype(BF16)
        da = (ds * bv * sig * (1.0 + av * (1.0 - sig))).astype(BF16)
        db = (ds * sa).astype(BF16)
        da_ref[0] = da
        db_ref[0] = db
        dn_s[...] += _dot(da, w1_ref[0]) + _dot(db, w3_ref[0])

        @pl.when(j == NSH - 1)
        def _():
            hv = h_ref[...]
            gv = g_ref[...]
            r = lax.rsqrt(jnp.mean(hv * hv, axis=-1, keepdims=True) + EPS)
            dn = dn_s[...]
            dx, xh = _rms_bwd(dn, hv, r, gv)
            dh_ref[...] = do_ref[...] + dx
            _acc_rows(dg_ref, jnp.sum(dn * xh, axis=0, keepdims=True), i == 0)

    row = pl.BlockSpec((tm, D), lambda i, j: (i, 0))
    hid = pl.BlockSpec((1, tm, FS), lambda i, j: (j, i, 0))
    wsp = pl.BlockSpec((1, FS, D), lambda i, j: (j, 0, 0))
    return _call(
        body, comm, name=name, grid=(L // tm, NSH),
        out_shape=[jax.ShapeDtypeStruct((L, D), F32)]
        + [jax.ShapeDtypeStruct((NSH, L, FS), BF16)] * 3
        + [jax.ShapeDtypeStruct((L, D), BF16), jax.ShapeDtypeStruct((1, D), F32)],
        in_specs=[row, _res((1, D)), row, hid, hid,
                  wsp, wsp, wsp],
        out_specs=[row, hid, hid, hid, row, pl.BlockSpec((1, D), lambda i, j: (0, 0))],
        scratch_shapes=[pltpu.VMEM((tm, D), BF16), pltpu.VMEM((tm, D), F32)],
        params=_params("arbitrary", "arbitrary"),
    )(h, g, dout, a, b, w1, w3, w2)


def _wgrad(xm, ym, name, scale=1.0):
    xs, ys = xm.ndim == 3, ym.ndim == 3
    assert not (xs and ys)
    L = xm.shape[-2]
    K, N = xm.shape[-1], ym.shape[-1]
    tl = _tile(L, 1056)
    nl = L // tl
    if xs or ys:
        tn, grid_n = N, NSH
    else:
        tn = _tile(N, 1024, 128)
        grid_n = N // tn

    def body(x_ref, y_ref, o_ref, acc_s):
        l = pl.program_id(1)
        xv = x_ref[0] if xs else x_ref[...]
        yv = y_ref[0] if ys else y_ref[...]
        part = _dot_tn(xv.astype(BF16), yv.astype(BF16))
        _acc_rows(acc_s, part, l == 0)

        @pl.when(l == nl - 1)
        def _():
            res = (acc_s[...] * scale).astype(BF16)
            if xs or ys:
                o_ref[0] = res
            else:
                o_ref[...] = res

    if xs:
        x_spec = pl.BlockSpec((1, tl, K), lambda n, l: (n, l, 0))
        y_spec = pl.BlockSpec((tl, N), lambda n, l: (l, 0))
        o_spec = pl.BlockSpec((1, K, N), lambda n, l: (n, 0, 0))
        o_shape = (NSH, K, N)
    elif ys:
        x_spec = pl.BlockSpec((tl, K), lambda n, l: (l, 0))
        y_spec = pl.BlockSpec((1, tl, N), lambda n, l: (n, l, 0))
        o_spec = pl.BlockSpec((1, K, N), lambda n, l: (n, 0, 0))
        o_shape = (NSH, K, N)
    else:
        x_spec = pl.BlockSpec((tl, K), lambda n, l: (l, 0))
        y_spec = pl.BlockSpec((tl, tn), lambda n, l: (l, n))
        o_spec = pl.BlockSpec((K, tn), lambda n, l: (0, n))
        o_shape = (K, N)
    return pl.pallas_call(
        body, name=name, grid=(grid_n, nl),
        out_shape=jax.ShapeDtypeStruct(o_shape, BF16),
        in_specs=[x_spec, y_spec], out_specs=o_spec,
        scratch_shapes=[pltpu.VMEM((K, tn), F32)],
        compiler_params=_params("parallel", "arbitrary"),
    )(xm, ym)


def _mix_in_fwd(h, g, w_in, b_gate, name, comm=None):
    L = h.shape[0]
    tm = _tile(L, 528)

    def body(h_ref, g_ref, w_ref, bg_ref, vg_ref, uf_ref, gt_ref):
        u, _ = _rms(h_ref[...], g_ref[...])
        ub = u.astype(BF16)
        p = [_dot(ub, w_ref[j]) for j in range(NSH)]
        a0, a1 = 2 * DC - WS, 2 * DC + DS - WS
        vg_ref[:, 0:WS] = p[0].astype(BF16)
        vg_ref[:, WS:2 * DC] = p[1][:, 0:a0].astype(BF16)
        uf_ref[...] = p[1][:, a0:a1].astype(BF16)
        gin = jnp.concatenate([p[1][:, a1:], p[2], p[3]], axis=1)
        gt_ref[...] = _sigmoid(gin + bg_ref[...]).astype(BF16)

    def row(n):
        return pl.BlockSpec((tm, n), lambda i: (i, 0))

    return _call(
        body, comm, name=name, grid=(L // tm,),
        out_shape=[jax.ShapeDtypeStruct((L, 2 * DC), BF16), jax.ShapeDtypeStruct((L, DS), BF16),
                   jax.ShapeDtypeStruct((L, 2 * D), BF16)],
        in_specs=[row(D), _res((1, D)), _res((NSH, D, WS)), _res((1, 2 * D))],
        out_specs=[row(2 * DC), row(DS), row(2 * D)],
        params=_params("parallel"),
    )(h, g, w_in, b_gate)


def _mix_in_bwd(h, g, dres, dv, dgl, duf, dgate, w_in, name):
    L = h.shape[0]
    tm = _tile(L, 528)

    def body(h_ref, g_ref, dr_ref, dv_ref, dgl_ref, duf_ref, dgt_ref, w_ref, dh_ref, u_ref, dp_ref, dgm_ref):
        i = pl.program_id(0)
        hv = h_ref[...]
        gv = g_ref[...]
        u, r = _rms(hv, gv)
        u_ref[...] = u.astype(BF16)
        a0, a1 = 2 * DC - WS, 2 * DC + DS - WS
        b0 = WS - a1
        dp = [jnp.concatenate([dv_ref[...], dgl_ref[:, 0:WS - DC]], axis=1),
              jnp.concatenate([dgl_ref[:, WS - DC:], duf_ref[...], dgt_ref[:, 0:b0]], axis=1),
              dgt_ref[:, b0:b0 + WS], dgt_ref[:, b0 + WS:]]
        du = jnp.zeros((tm, D), F32)
        for j in range(NSH):
            dp_ref[j] = dp[j]
            du = du + _dot_nt(dp[j], w_ref[j])
        dx, xh = _rms_bwd(du, hv, r, gv)
        dh_ref[...] = dr_ref[...] + dx
        _acc_rows(dgm_ref, jnp.sum(du * xh, axis=0, keepdims=True), i == 0)

    def row(n):
        return pl.BlockSpec((tm, n), lambda i: (i, 0))

    return pl.pallas_call(
        body, name=name, grid=(L // tm,),
        out_shape=[jax.ShapeDtypeStruct((L, D), F32), jax.ShapeDtypeStruct((L, D), BF16),
                   jax.ShapeDtypeStruct((NSH, L, WS), BF16), jax.ShapeDtypeStruct((1, D), F32)],
        in_specs=[row(D), _res((1, D)), row(D), row(DC), row(DC), row(DS), row(2 * D), _res((NSH, D, WS))],
        out_specs=[row(D), row(D), pl.BlockSpec((NSH, tm, WS), lambda i: (0, i, 0)),
                   pl.BlockSpec((1, D), lambda i: (0, 0))],
        compiler_params=_params("arbitrary"),
    )(h, g, dres, dv, dgl, duf, dgate, w_in)


def _conv_fwd(vg, dw, dwb, name, comm=None):
    L = vg.shape[0]
    nc = DC // 128

    def body(v_ref, g_ref, dw_ref, dwb_ref, z_ref, zp_s):
        zp_s[0:KWP, :] = jnp.zeros((KWP, 128), F32)
        zp_s[KWP:, :] = v_ref[...].astype(F32) * _sigmoid(g_ref[...].astype(F32))
        for r0 in range(0, L, CONV_ROWS):
            acc = jnp.broadcast_to(dwb_ref[...], (CONV_ROWS, 128))
            for k in range(KW):
                acc = acc + dw_ref[k:k + 1, :] * zp_s[pl.ds(r0 + k + 2, CONV_ROWS), :]
            z_ref[pl.ds(r0, CONV_ROWS), :] = acc

    return _call(
        body, comm, name=name, grid=(nc,),
        out_shape=[jax.ShapeDtypeStruct((L, DC), F32)],
        in_specs=[pl.BlockSpec((L, 128), lambda c: (0, c)), pl.BlockSpec((L, 128), lambda c: (0, nc + c)),
                  pl.BlockSpec((KWP, 128), lambda c: (0, c)), pl.BlockSpec((1, 128), lambda c: (0, c))],
        out_specs=[pl.BlockSpec((L, 128), lambda c: (0, c))],
        scratch_shapes=[pltpu.VMEM((L + KWP, 128), F32)],
        params=_params("parallel"),
    )(vg, vg, dw, dwb)


def _conv_bwd(dz1, vg, dw, name):
    L = vg.shape[0]
    nc = DC // 128

    def body(dz_ref, v_ref, g_ref, dw_ref, dv_ref, dg_ref, ddw_ref, ddwb_ref, zp_s, dzp_s):
        vv = v_ref[...].astype(F32)
        sg = _sigmoid(g_ref[...].astype(F32))
        zp_s[0:KWP, :] = jnp.zeros((KWP, 128), F32)
        zp_s[KWP:, :] = vv * sg
        dz = dz_ref[...]
        dzp_s[0:L, :] = dz
        dzp_s[L:, :] = jnp.zeros((KWP, 128), F32)
        ddwb_ref[...] = jnp.sum(dz, axis=0, keepdims=True)
        part = [jnp.zeros((8, 128), F32) for _ in range(KW)]
        for r0 in range(0, L, CONV_ROWS):
            rows = pl.ds(r0, CONV_ROWS)
            dzc = dz_ref[rows, :]
            acc = jnp.zeros((CONV_ROWS, 128), F32)
            for k in range(KW):
                acc = acc + dw_ref[k:k + 1, :] * dzp_s[pl.ds(r0 + KW - 1 - k, CONV_ROWS), :]
                prod = dzc * zp_s[pl.ds(r0 + k + 2, CONV_ROWS), :]
                for q in range(CONV_ROWS // 8):
                    part[k] = part[k] + prod[8 * q:8 * (q + 1), :]
            vc = v_ref[rows, :].astype(F32)
            sc = _sigmoid(g_ref[rows, :].astype(F32))
            dv_ref[rows, :] = (acc * sc).astype(BF16)
            dg_ref[rows, :] = (acc * vc * sc * (1.0 - sc)).astype(BF16)
        for k in range(KW):
            ddw_ref[k:k + 1, :] = jnp.sum(part[k], axis=0, keepdims=True)
        ddw_ref[KW:KWP, :] = jnp.zeros((KWP - KW, 128), F32)

    col = pl.BlockSpec((L, 128), lambda c: (0, c))
    return pl.pallas_call(
        body, name=name, grid=(nc,),
        out_shape=[jax.ShapeDtypeStruct((L, DC), BF16), jax.ShapeDtypeStruct((L, DC), BF16),
                   jax.ShapeDtypeStruct((KWP, DC), F32), jax.ShapeDtypeStruct((1, DC), F32)],
        in_specs=[col, col, pl.BlockSpec((L, 128), lambda c: (0, nc + c)),
                  pl.BlockSpec((KWP, 128), lambda c: (0, c))],
        out_specs=[col, col, pl.BlockSpec((KWP, 128), lambda c: (0, c)), pl.BlockSpec((1, 128), lambda c: (0, c))],
        scratch_shapes=[pltpu.VMEM((L + KWP, 128), F32), pltpu.VMEM((L + KWP, 128), F32)],
        compiler_params=_params("parallel"),
    )(dz1, vg, vg, dw)


NLB = QS // 128


def _lb_store(ref, rows, val):
    for cb in range(NLB):
        ref[cb, rows, :] = val[:, cb * 128:(cb + 1) * 128]


def _lb_load(ref, rows):
    return jnp.concatenate([ref[cb, rows, :] for cb in range(NLB)], axis=1)


def _scan(xr_ref, xi_ref, base, T, ar, ai, atr, ati, reverse):
    W = ar.shape[1]
    ar, ai, atr, ati = (jnp.broadcast_to(v, (8, W)) for v in (ar, ai, atr, ati))
    zero = jnp.zeros((8, W), F32)

    def rows(t, g):
        tt = T - 1 - t if reverse else t
        return pl.ds(base + g * 8 * T + tt, 8, stride=T)

    def make_step(store):
        def step(t, carry):
            out = []
            for g in range(NGRP):
                sr, si = carry[2 * g], carry[2 * g + 1]
                idx = rows(t, g)
                nr = ar * sr - ai * si + _lb_load(xr_ref, idx)
                ni = ar * si + ai * sr + _lb_load(xi_ref, idx)
                if store:
                    _lb_store(xr_ref, idx, nr)
                    _lb_store(xi_ref, idx, ni)
                out += [nr, ni]
            return tuple(out)
        return step

    ends = lax.fori_loop(0, T, make_step(False), (zero,) * (2 * NGRP))
    sub = lax.broadcasted_iota(jnp.int32, (8, W), 0)
    edge = sub == (7 if reverse else 0)
    shift, last = (7, 0) if reverse else (1, 7)
    inr, ini = jnp.zeros((1, W), F32), jnp.zeros((1, W), F32)
    starts = [None] * (2 * NGRP)
    for g in (reversed(range(NGRP)) if reverse else range(NGRP)):
        er, ei = ends[2 * g], ends[2 * g + 1]
        cr, ci = jnp.where(edge, inr, 0.0), jnp.where(edge, ini, 0.0)
        for _ in range(7):
            nr = atr * cr - ati * ci + er
            ni = atr * ci + ati * cr + ei
            cr = jnp.where(edge, inr, pltpu.roll(nr, shift, 0))
            ci = jnp.where(edge, ini, pltpu.roll(ni, shift, 0))
        starts[2 * g], starts[2 * g + 1] = cr, ci
        inr = (atr * cr - ati * ci + er)[last:last + 1]
        ini = (atr * ci + ati * cr + ei)[last:last + 1]
    lax.fori_loop(0, T, make_step(True), tuple(starts))


def _ssm_fwd(uf, bre, bim, cre, cim, lamp, dsk, name, comm=None):
    L = uf.shape[0]
    T = L // NSEG
    tc = L // NCH

    def body(u_ref, bre_ref, bim_ref, cre_ref, cim_ref, lam_ref, d_ref, y_ref, sr_s, si_s):
        for k in range(NCH):
            sl = slice(k * tc, (k + 1) * tc)
            uk = u_ref[sl, :]
            _lb_store(sr_s, sl, _dot(uk, bre_ref[...]))
            _lb_store(si_s, sl, _dot(uk, bim_ref[...]))
        _scan(sr_s, si_s, 0, T, lam_ref[0:1, :], lam_ref[1:2, :], lam_ref[2:3, :], lam_ref[3:4, :], False)
        for k in range(NCH):
            sl = slice(k * tc, (k + 1) * tc)
            y_ref[sl, :] = (_dot(_lb_load(sr_s, sl).astype(BF16), cre_ref[...])
                            - _dot(_lb_load(si_s, sl).astype(BF16), cim_ref[...])
                            + d_ref[...] * u_ref[sl, :].astype(F32))

    return _call(
        body, comm, name=name, grid=(NQ,),
        out_shape=[jax.ShapeDtypeStruct((L, DS), F32)],
        in_specs=[pl.BlockSpec((L, QU), lambda q: (0, q)),
                  pl.BlockSpec((QU, QS), lambda q: (q, q)), pl.BlockSpec((QU, QS), lambda q: (q, q)),
                  pl.BlockSpec((QS, QU), lambda q: (q, q)), pl.BlockSpec((QS, QU), lambda q: (q, q)),
                  pl.BlockSpec((8, QS), lambda q: (0, q)), pl.BlockSpec((1, QU), lambda q: (0, q))],
        out_specs=[pl.BlockSpec((L, QU), lambda q: (0, q))],
        scratch_shapes=[pltpu.VMEM((NLB, L, 128), F32), pltpu.VMEM((NLB, L, 128), F32)],
        params=_params("parallel"),
    )(uf, bre, bim, cre, cim, lamp, dsk)


def _ssm_bwd(uf, dyss, bre, bim, cre, cim, lamp, dsk, name, comm=None):
    L = uf.shape[0]
    T = L // NSEG
    tc = L // NCH

    def body(u_ref, dy_ref, bre_ref, bim_ref, cre_ref, cim_ref, lam_ref, d_ref,
             du_ref, dbre_ref, dbim_ref, dcre_ref, dcim_ref, dlam_ref, dd_ref, sr_s, si_s, gr_s, gi_s):
        _lb_store(sr_s, slice(0, SOFF), jnp.zeros((SOFF, QS), F32))
        _lb_store(si_s, slice(0, SOFF), jnp.zeros((SOFF, QS), F32))
        for k in range(NCH):
            sl = slice(k * tc, (k + 1) * tc)
            ss = slice(SOFF + k * tc, SOFF + (k + 1) * tc)
            uk = u_ref[sl, :]
            dyk = dy_ref[sl, :].astype(BF16)
            _lb_store(sr_s, ss, _dot(uk, bre_ref[...]))
            _lb_store(si_s, ss, _dot(uk, bim_ref[...]))
            _lb_store(gr_s, sl, _dot_nt(dyk, cre_ref[...]))
            _lb_store(gi_s, sl, -_dot_nt(dyk, cim_ref[...]))
        ar, ai, atr, ati = lam_ref[0:1, :], lam_ref[1:2, :], lam_ref[2:3, :], lam_ref[3:4, :]
        _scan(sr_s, si_s, SOFF, T, ar, ai, atr, ati, False)
        _scan(gr_s, gi_s, 0, T, ar, -ai, atr, -ati, True)
        dbre = jnp.zeros((QU, QS), F32)
        dbim = jnp.zeros((QU, QS), F32)
        dcre = jnp.zeros((QS, QU), F32)
        dcim = jnp.zeros((QS, QU), F32)
        dd = jnp.zeros((1, QU), F32)
        qr = jnp.zeros((1, QS), F32)
        qi = jnp.zeros((1, QS), F32)
        for k in range(NCH):
            sl = slice(k * tc, (k + 1) * tc)
            ss = slice(SOFF + k * tc, SOFF + (k + 1) * tc)
            sp = slice(SOFF - 1 + k * tc, SOFF - 1 + (k + 1) * tc)
            uk = u_ref[sl, :]
            dyk = dy_ref[sl, :]
            dyb = dyk.astype(BF16)
            gr, gi = _lb_load(gr_s, sl), _lb_load(gi_s, sl)
            pr, pi = _lb_load(sr_s, sp), _lb_load(si_s, sp)
            qr = qr + jnp.sum(gr * pr + gi * pi, axis=0, keepdims=True)
            qi = qi + jnp.sum(gi * pr - gr * pi, axis=0, keepdims=True)
            grb, gib = gr.astype(BF16), gi.astype(BF16)
            du_ref[sl, :] = (_dot_nt(grb, bre_ref[...]) + _dot_nt(gib, bim_ref[...])
                             + dyk * d_ref[...]).astype(BF16)
            dbre = dbre + _dot_tn(uk, grb)
            dbim = dbim + _dot_tn(uk, gib)
            dcre = dcre + _dot_tn(_lb_load(sr_s, ss).astype(BF16), dyb)
            dcim = dcim - _dot_tn(_lb_load(si_s, ss).astype(BF16), dyb)
            dd = dd + jnp.sum(dyk * uk.astype(F32), axis=0, keepdims=True)
        dlam_ref[0] = jnp.concatenate([qr, qi, jnp.zeros((6, QS), F32)], axis=0)
        dbre_ref[0] = dbre
        dbim_ref[0] = dbim
        dcre_ref[0] = dcre
        dcim_ref[0] = dcim
        dd_ref[...] = dd

    col = pl.BlockSpec((L, QU), lambda q: (0, q))
    bsp = pl.BlockSpec((QU, QS), lambda q: (q, q))
    csp = pl.BlockSpec((QS, QU), lambda q: (q, q))
    return _call(
        body, comm, name=name, grid=(NQ,),
        out_shape=[jax.ShapeDtypeStruct((L, DS), BF16),
                   jax.ShapeDtypeStruct((NQ, QU, QS), F32), jax.ShapeDtypeStruct((NQ, QU, QS), F32),
                   jax.ShapeDtypeStruct((NQ, QS, QU), F32), jax.ShapeDtypeStruct((NQ, QS, QU), F32),
                   jax.ShapeDtypeStruct((NQ, 8, QS), F32), jax.ShapeDtypeStruct((1, DS), F32)],
        in_specs=[col, col, bsp, bsp, csp, csp,
                  pl.BlockSpec((8, QS), lambda q: (0, q)), pl.BlockSpec((1, QU), lambda q: (0, q))],
        out_specs=[col,
                   pl.BlockSpec((1, QU, QS), lambda q: (q, 0, 0)), pl.BlockSpec((1, QU, QS), lambda q: (q, 0, 0)),
                   pl.BlockSpec((1, QS, QU), lambda q: (q, 0, 0)), pl.BlockSpec((1, QS, QU), lambda q: (q, 0, 0)),
                   pl.BlockSpec((1, 8, QS), lambda q: (q, 0, 0)), pl.BlockSpec((1, QU), lambda q: (0, q))],
        scratch_shapes=[pltpu.VMEM((NLB, L + SOFF, 128), F32), pltpu.VMEM((NLB, L + SOFF, 128), F32),
                        pltpu.VMEM((NLB, L, 128), F32), pltpu.VMEM((NLB, L, 128), F32)],
        params=_params("parallel"),
    )(uf, dyss, bre, bim, cre, cim, lamp, dsk)


def _branches(z1_ref, yss_ref, gt_ref, lng_ref, lnb_ref, wp_ref, wv_ref, wg_ref):
    zf = z1_ref[...]
    mu = jnp.mean(zf, axis=-1, keepdims=True)
    zc = zf - mu
    rstd = lax.rsqrt(jnp.mean(zc * zc, axis=-1, keepdims=True) + EPS)
    zn = zc * rstd
    z2 = zn * lng_ref[...] + lnb_ref[...]
    sz = _sigmoid(z2)
    z3 = (z2 * sz).astype(BF16)
    y_conv = _dot(z3, wp_ref[...])
    yss = yss_ref[...]
    yg = _gelu(yss).astype(BF16)
    sv = _dot(yg, wv_ref[...])
    sig = _sigmoid(_dot(yg, wg_ref[...]))
    y_ssm = sv * sig
    gc = gt_ref[:, 0:D].astype(F32)
    gs = gt_ref[:, D:2 * D].astype(F32)
    m = gc * y_conv + gs * y_ssm
    return dict(rstd=rstd, zn=zn, z2=z2, sz=sz, z3=z3, y_conv=y_conv, yss=yss, yg=yg, sv=sv, sig=sig,
                y_ssm=y_ssm, gc=gc, gs=gs, m=m)


def _merge_fwd(h, z1, yss, gate, lng, lnb, wp, wv, wg, wo, name):
    L = h.shape[0]
    tm = _tile(L, 528)

    def body(h_ref, z1_ref, yss_ref, gt_ref, lng_ref, lnb_ref, wp_ref, wv_ref, wg_ref, wo_ref, o_ref):
        f = _branches(z1_ref, yss_ref, gt_ref, lng_ref, lnb_ref, wp_ref, wv_ref, wg_ref)
        o_ref[...] = h_ref[...] + _dot(f["m"].astype(BF16), wo_ref[...])

    def row(n):
        return pl.BlockSpec((tm, n), lambda i: (i, 0))

    return pl.pallas_call(
        body, name=name, grid=(L // tm,),
        out_shape=jax.ShapeDtypeStruct((L, D), F32),
        in_specs=[row(D), row(DC), row(DS), row(2 * D), _res((1, DC)), _res((1, DC)),
                  _res((DC, D)), _res((DS, D)), _res((DS, D)), _res((D, D))],
        out_specs=row(D),
        compiler_params=_params("parallel"),
    )(h, z1, yss, gate, lng, lnb, wp, wv, wg, wo)


def _merge_bwd(dh, z1, yss, gate, lng, lnb, wp, wv, wg, wo, name):
    L = dh.shape[0]
    tm = _tile(L, 352)

    def body(dh_ref, z1_ref, yss_ref, gt_ref, lng_ref, lnb_ref, wp_ref, wv_ref, wg_ref, wo_ref,
             m_ref, dgt_ref, dyc_ref, z3_ref, dz1_ref, yg_ref, dsv_ref, dsg_ref, dyss_ref,
             dbg_ref, dlng_ref, dlnb_ref):
        i = pl.program_id(0)
        f = _branches(z1_ref, yss_ref, gt_ref, lng_ref, lnb_ref, wp_ref, wv_ref, wg_ref)
        gc, gs, sig, sv = f["gc"], f["gs"], f["sig"], f["sv"]
        m_ref[...] = f["m"].astype(BF16)
        z3_ref[...] = f["z3"]
        yg_ref[...] = f["yg"]
        dm = _dot_nt(dh_ref[...].astype(BF16), wo_ref[...])
        dgc = (dm * f["y_conv"] * gc * (1.0 - gc)).astype(BF16)
        dgs = (dm * f["y_ssm"] * gs * (1.0 - gs)).astype(BF16)
        dgt_ref[:, 0:D] = dgc
        dgt_ref[:, D:2 * D] = dgs
        part = jnp.concatenate([jnp.sum(dgc.astype(F32), axis=0, keepdims=True),
                                jnp.sum(dgs.astype(F32), axis=0, keepdims=True)], axis=1)
        _acc_rows(dbg_ref, part, i == 0)
        dyc = (dm * gc).astype(BF16)
        dyc_ref[...] = dyc
        dys = dm * gs
        dsv = (dys * sig).astype(BF16)
        dsg = (dys * sv * sig * (1.0 - sig)).astype(BF16)
        dsv_ref[...] = dsv
        dsg_ref[...] = dsg
        dyg = _dot_nt(dsv, wv_ref[...]) + _dot_nt(dsg, wg_ref[...])
        dyss_ref[...] = dyg * _gelu_grad(f["yss"])
        dz3 = _dot_nt(dyc, wp_ref[...])
        z2, sz, zn = f["z2"], f["sz"], f["zn"]
        dz2 = dz3 * sz * (1.0 + z2 * (1.0 - sz))
        _acc_rows(dlng_ref, jnp.sum(dz2 * zn, axis=0, keepdims=True), i == 0)
        _acc_rows(dlnb_ref, jnp.sum(dz2, axis=0, keepdims=True), i == 0)
        dzn = dz2 * lng_ref[...]
        dz1_ref[...] = f["rstd"] * (dzn - jnp.mean(dzn, axis=-1, keepdims=True)
                                    - zn * jnp.mean(dzn * zn, axis=-1, keepdims=True))

    def row(n):
        return pl.BlockSpec((tm, n), lambda i: (i, 0))

    def tot(n):
        return pl.BlockSpec((1, n), lambda i: (0, 0))

    return pl.pallas_call(
        body, name=name, grid=(L // tm,),
        out_shape=[jax.ShapeDtypeStruct((L, D), BF16), jax.ShapeDtypeStruct((L, 2 * D), BF16),
                   jax.ShapeDtypeStruct((L, D), BF16), jax.ShapeDtypeStruct((L, DC), BF16),
                   jax.ShapeDtypeStruct((L, DC), F32), jax.ShapeDtypeStruct((L, DS), BF16),
                   jax.ShapeDtypeStruct((L, D), BF16), jax.ShapeDtypeStruct((L, D), BF16),
                   jax.ShapeDtypeStruct((L, DS), F32),
                   jax.ShapeDtypeStruct((1, 2 * D), F32), jax.ShapeDtypeStruct((1, DC), F32),
                   jax.ShapeDtypeStruct((1, DC), F32)],
        in_specs=[row(D), row(DC), row(DS), row(2 * D), _res((1, DC)), _res((1, DC)),
                  _res((DC, D)), _res((DS, D)), _res((DS, D)), _res((D, D))],
        out_specs=[row(D), row(2 * D), row(D), row(DC), row(DC), row(DS), row(D), row(D), row(DS),
                   tot(2 * D), tot(DC), tot(DC)],
        compiler_params=_params("arbitrary"),
    )(dh, z1, yss, gate, lng, lnb, wp, wv, wg, wo)


def _final(h, g, tgt, name):
    L = h.shape[0]
    tm = _tile(L, 528)

    def body(h_ref, g_ref, t_ref, dh_ref, loss_ref, dg_ref):
        i = pl.program_id(0)
        hv = h_ref[...]
        gv = g_ref[...]
        y, r = _rms(hv, gv)
        row = i * tm + lax.broadcasted_iota(jnp.int32, (tm, 1), 0)
        e = jnp.where(row >= FRONT, y - t_ref[...], 0.0)
        dy = e * (1.0 / D)
        part = 0.5 * jnp.sum(jnp.sum(e * dy, axis=1, keepdims=True), axis=0, keepdims=True)
        dx, xh = _rms_bwd(dy, hv, r, gv)
        dh_ref[...] = dx
        _acc_rows(loss_ref, part, i == 0)
        _acc_rows(dg_ref, jnp.sum(dy * xh, axis=0, keepdims=True), i == 0)

    row = pl.BlockSpec((tm, D), lambda i: (i, 0))
    return pl.pallas_call(
        body, name=name, grid=(L // tm,),
        out_shape=[jax.ShapeDtypeStruct((L, D), F32), jax.ShapeDtypeStruct((1, 1), F32),
                   jax.ShapeDtypeStruct((1, D), F32)],
        in_specs=[row, _res((1, D)), row],
        out_specs=[row, pl.BlockSpec((1, 1), lambda i: (0, 0)), pl.BlockSpec((1, D), lambda i: (0, 0))],
        compiler_params=_params("arbitrary"),
    )(h, g, tgt)


def _ssm_disc(lam_re, lam_im, log_dt, b_re, b_im):
    lam = lax.complex(lam_re, lam_im)
    dt = jnp.exp(log_dt)[:, None]
    lam_bar = jnp.exp(lam * dt)
    bbar = ((lam_bar - 1.0) / lam)[..., None] * lax.complex(b_re, b_im)
    return jnp.real(lam_bar), jnp.imag(lam_bar), jnp.real(bbar), jnp.imag(bbar)


def _bdiag_in(m):
    return jnp.einsum("gph,gk->ghkp", m, jnp.eye(G, dtype=m.dtype)).reshape(G * H, G * P)


def _bdiag_out(m):
    return jnp.einsum("ghp,gk->gpkh", m, jnp.eye(G, dtype=m.dtype)).reshape(G * P, G * H)


def _diag_blocks(m4):
    return jnp.einsum("qiaib->qiab", m4).reshape(G, m4.shape[2], m4.shape[4])


def _pack(parts, rows_mult=8):
    flat = jnp.concatenate([p.reshape(-1).astype(F32) for p in parts])
    n = flat.shape[0]
    tot = -(-n // (128 * rows_mult)) * (128 * rows_mult)
    return jnp.pad(flat, (0, tot - n)).reshape(tot // 128, 128)


def _unpack(buf, shapes):
    flat = buf.reshape(-1)
    out, o = [], 0
    for s in shapes:
        n = math.prod(s)
        out.append(flat[o:o + n].reshape(s))
        o += n
    return out


def kernel(x, meta_tokens, ffn1_norm, ffn1_w1, ffn1_w3, ffn1_w2, mix_norm, w_in, b_gate, conv_dw, conv_dw_b, conv_ln_g, conv_ln_b, conv_proj, ssm_lam_re, ssm_lam_im, ssm_log_dt, ssm_b_re, ssm_b_im, ssm_c_re, ssm_c_im, ssm_d, ssm_w_v, ssm_w_g, w_out, ffn2_norm, ffn2_w1, ffn2_w3, ffn2_w2, final_norm, loss_target, m_meta_tokens, m_ffn1_norm, m_ffn1_w1, m_ffn1_w3, m_ffn1_w2, m_mix_norm, m_w_in, m_b_gate, m_conv_dw, m_conv_dw_b, m_conv_ln_g, m_conv_ln_b, m_conv_proj, m_ssm_lam_re, m_ssm_lam_im, m_ssm_log_dt, m_ssm_b_re, m_ssm_b_im, m_ssm_c_re, m_ssm_c_im, m_ssm_d, m_ssm_w_v, m_ssm_w_g, m_w_out, m_ffn2_norm, m_ffn2_w1, m_ffn2_w3, m_ffn2_w2, m_final_norm, v_meta_tokens, v_ffn1_norm, v_ffn1_w1, v_ffn1_w3, v_ffn1_w2, v_mix_norm, v_w_in, v_b_gate, v_conv_dw, v_conv_dw_b, v_conv_ln_g, v_conv_ln_b, v_conv_proj, v_ssm_lam_re, v_ssm_lam_im, v_ssm_log_dt, v_ssm_b_re, v_ssm_b_im, v_ssm_c_re, v_ssm_c_im, v_ssm_d, v_ssm_w_v, v_ssm_w_g, v_w_out, v_ffn2_norm, v_ffn2_w1, v_ffn2_w3, v_ffn2_w2, v_final_norm):
    args = dict(locals())
    names = ["meta_tokens", "ffn1_norm", "ffn1_w1", "ffn1_w3", "ffn1_w2", "mix_norm", "w_in", "b_gate",
             "conv_dw", "conv_dw_b", "conv_ln_g", "conv_ln_b", "conv_proj", "ssm_lam_re", "ssm_lam_im",
             "ssm_log_dt", "ssm_b_re", "ssm_b_im", "ssm_c_re", "ssm_c_im", "ssm_d", "ssm_w_v", "ssm_w_g",
             "w_out", "ffn2_norm", "ffn2_w1", "ffn2_w3", "ffn2_w2", "final_norm"]
    big = ["ffn1_w1", "ffn1_w3", "ffn1_w2", "w_in", "conv_proj", "ssm_w_v", "ssm_w_g", "w_out",
           "ffn2_w1", "ffn2_w3", "ffn2_w2"]
    small = [n for n in names if n not in big]

    xs = x[0]
    S = xs.shape[0]
    L = FRONT + S
    T = L // NSEG
    jx, jy = lax.axis_index("x"), lax.axis_index("y")
    chip = 2 * jx + jy

    sm = _gather_all(_pack([meta_tokens, conv_dw[0]]), "gather_small")[0::2].reshape(NSH, -1)
    nmt = NMETA * (D // NSH)
    ndw = KW * (DC // NSH)
    meta_full = sm[:, :nmt].reshape(NSH, NMETA, D // NSH).transpose(1, 0, 2).reshape(NMETA, D)
    dw_full = sm[:, nmt:nmt + ndw].reshape(NSH, KW, DC // NSH).transpose(1, 0, 2).reshape(KW, DC)
    dw_pad = jnp.pad(dw_full, ((0, KWP - KW), (0, 0)))
    tposed = ("ffn1_w1", "ffn1_w3", "ffn2_w1", "ffn2_w3")

    def view(a, n):
        return jnp.swapaxes(a, 1, 2) if n in tposed else a

    grp_a = ["ffn1_w1", "ffn1_w3", "ffn1_w2"]
    grp_b = ["w_in", "conv_proj", "ssm_w_v", "ssm_w_g", "w_out"]
    grp_c = ["ffn2_w1", "ffn2_w3", "ffn2_w2"]

    def shard(n):
        return view(args[n], n)[0].astype(BF16)

    sh_a = [shard(n) for n in grp_a]
    ga_send, ga_recv, sh_a, land_a, ga_token = _chips_start(
        sh_a, [jax.ShapeDtypeStruct((NSH,) + s.shape, s.dtype) for s in sh_a], True, "gather_ffn1_start")

    def cols(w):
        return w.transpose(1, 0, 2).reshape(w.shape[1], -1)

    disc_in = (ssm_lam_re[0], ssm_lam_im[0], ssm_log_dt[0], ssm_b_re[0], ssm_b_im[0])
    (lbr, lbi, bbr, bbi), disc_vjp = jax.vjp(_ssm_disc, *disc_in)
    lam_t = jnp.exp(lax.complex(ssm_lam_re[0], ssm_lam_im[0]) * (jnp.exp(ssm_log_dt[0])[:, None] * T))
    lamp = jnp.concatenate([lbr.reshape(1, NST), lbi.reshape(1, NST), jnp.real(lam_t).reshape(1, NST),
                            jnp.imag(lam_t).reshape(1, NST), jnp.zeros((4, NST), F32)], axis=0)
    bre_bd, bim_bd = _bdiag_in(bbr).astype(BF16), _bdiag_in(bbi).astype(BF16)
    cre_bd, cim_bd = _bdiag_out(ssm_c_re[0]).astype(BF16), _bdiag_out(ssm_c_im[0]).astype(BF16)

    h0 = jnp.concatenate([jnp.zeros((FRONT - NMETA, D), F32), meta_full, xs + ga_token[0, 0]], axis=0)
    sh_a, land_a = _chips_wait(ga_send, ga_recv, sh_a, land_a, [h0], True, "gather_ffn1_wait")
    gw = dict(zip(grp_a, _pass_halves(land_a, "pass_ffn1", sh_a)))
    tgt = jnp.pad(loss_target[0], ((FRONT, 0), (0, 0)))
    (h1, a1, b1), got = _ffn_fwd(h0, ffn1_norm, gw["ffn1_w1"], gw["ffn1_w3"], gw["ffn1_w2"], "ffn1_fwd",
                                 _gather_half_behind([shard(n) for n in grp_b]))
    gw.update(zip(grp_b, _pass_halves(got, "pass_mix")))
    w_in_f = gw["w_in"]
    wp_f, wv_f, wg_f = cols(gw["conv_proj"]), cols(gw["ssm_w_v"]), cols(gw["ssm_w_g"])
    wo_f = gw["w_out"].reshape(D, D)
    (vg, uf, gate), got1 = _mix_in_fwd(h1, mix_norm, w_in_f, b_gate, "mix_in_fwd",
                                       _gather_half_behind([shard("ffn2_w1")]))
    (z1,), got3 = _conv_fwd(vg, dw_pad, conv_dw_b, "conv_fwd", _gather_half_behind([shard("ffn2_w3")]))
    (yss,), got2 = _ssm_fwd(uf, bre_bd, bim_bd, cre_bd, cim_bd, lamp, ssm_d, "ssm_fwd",
                            _gather_half_behind([shard("ffn2_w2")]))
    gw.update(zip(grp_c, _pass_halves([got1[0], got3[0], got2[0]], "pass_ffn2")))
    h2 = _merge_fwd(h1, z1, yss, gate, conv_ln_g, conv_ln_b, wp_f, wv_f, wg_f, wo_f, "merge_fwd")
    (h3, a2, b2), _ = _ffn_fwd(h2, ffn2_norm, gw["ffn2_w1"], gw["ffn2_w3"], gw["ffn2_w2"], "ffn2_fwd")

    gbig = {}
    core = lax.axis_index("c").astype(jnp.int32).reshape(1)

    def pair_sums(group, tag):
        gl = [gbig[n] for n in group]
        sib = _pair_exchange(gl, "pair_exchange_" + tag)
        out = []
        for n, g_, s_ in zip(group, gl, sib):
            out.append(_add_pair(g_, s_, core, "pair_" + n))
        return out

    dh3, loss_part, d_final = _final(h3, final_norm.reshape(1, D), tgt, "final")
    (dh2, da2, db2, s2, n2, d_ffn2_norm), _ = _ffn_bwd(
        h2, ffn2_norm, dh3, a2, b2, gw["ffn2_w1"], gw["ffn2_w3"], gw["ffn2_w2"], "ffn2_bwd")
    gbig["ffn2_w1"] = _wgrad(da2, n2, "ffn2_dw1")
    gbig["ffn2_w3"] = _wgrad(db2, n2, "ffn2_dw3")
    gbig["ffn2_w2"] = _wgrad(s2, dh3, "ffn2_dw2", 0.5)
    pair_c = pair_sums(grp_c, "ffn2")
    (m_b, dgate, dyc, z3, dz1, yg, dsv, dsg, dyss, d_b_gate, d_ln_g, d_ln_b) = _merge_bwd(
        dh2, z1, yss, gate, conv_ln_g, conv_ln_b, wp_f, wv_f, wg_f, wo_f, "merge_bwd")
    gbig["w_out"] = _wgrad(m_b, dh2, "dw_out").reshape(NSH, D // NSH, D)

    def shard_cols(gm):
        return gm.reshape(gm.shape[0], NSH, -1).transpose(1, 0, 2)

    gbig["conv_proj"] = shard_cols(_wgrad(z3, dyc, "dw_proj"))
    gbig["ssm_w_v"] = shard_cols(_wgrad(yg, dsv, "dw_v"))
    gbig["ssm_w_g"] = shard_cols(_wgrad(yg, dsg, "dw_g"))
    dv, dgl, ddw, d_dw_b = _conv_bwd(dz1, vg, dw_pad, "conv_bwd")
    (duf, dbre, dbim, dcre, dcim, dlam, d_ssm_d), recv_c = _ssm_bwd(
        uf, dyss, bre_bd, bim_bd, cre_bd, cim_bd, lamp, ssm_d, "ssm_bwd", _scatter_chips_behind(pair_c))
    dh1, u_b, dproj, d_mix_norm = _mix_in_bwd(h1, mix_norm, dh2, dv, dgl, duf, dgate, w_in_f, "mix_in_bwd")
    gbig["w_in"] = _wgrad(u_b, dproj, "dw_in")
    pair_b = pair_sums(grp_b, "mix")

    d_bbr = _diag_blocks(dbre.reshape(NQ, 8, H, 8, P)).transpose(0, 2, 1)
    d_bbi = _diag_blocks(dbim.reshape(NQ, 8, H, 8, P)).transpose(0, 2, 1)
    d_c_re = _diag_blocks(dcre.reshape(NQ, 8, P, 8, H)).transpose(0, 2, 1)
    d_c_im = _diag_blocks(dcim.reshape(NQ, 8, P, 8, H)).transpose(0, 2, 1)
    d_lbr = dlam[:, 0, :].reshape(G, P)
    d_lbi = dlam[:, 1, :].reshape(G, P)
    d_lam_re, d_lam_im, d_log_dt, d_b_re, d_b_im = disc_vjp((d_lbr, d_lbi, d_bbr, d_bbi))

    sg = {"mix_norm": d_mix_norm, "b_gate": d_b_gate, "conv_dw": ddw[:KW], "conv_dw_b": d_dw_b,
          "conv_ln_g": d_ln_g, "conv_ln_b": d_ln_b, "ssm_lam_re": d_lam_re, "ssm_lam_im": d_lam_im,
          "ssm_log_dt": d_log_dt, "ssm_b_re": d_b_re, "ssm_b_im": d_b_im, "ssm_c_re": d_c_re, "ssm_c_im": d_c_im,
          "ssm_d": d_ssm_d, "ffn2_norm": d_ffn2_norm, "final_norm": d_final}
    late = ["meta_tokens", "ffn1_norm"]
    early = [n for n in small if n not in late]

    (dh0, da1, db1, s1, n1, d_ffn1_norm), got = _ffn_bwd(
        h0, ffn1_norm, dh1, a1, b1, gw["ffn1_w1"], gw["ffn1_w3"], gw["ffn1_w2"], "ffn1_bwd",
        _join(_scatter_chips_behind(pair_b), _gather_all_behind(_pack([sg[n] for n in early]))))
    recv_b, early_all = got[:len(grp_b)], got[len(grp_b)]
    gbig["ffn1_w1"] = _wgrad(da1, n1, "ffn1_dw1")
    gbig["ffn1_w3"] = _wgrad(db1, n1, "ffn1_dw3")
    gbig["ffn1_w2"] = _wgrad(s1, dh1, "ffn1_dw2", 0.5)
    grad_x = dh0[FRONT:][None]
    sg["meta_tokens"] = dh0[FRONT - NMETA:FRONT]
    sg["ffn1_norm"] = d_ffn1_norm

    pair_a = pair_sums(grp_a, "ffn1")
    sa_send, sa_recv, pair_a, land_s, sa_token = _chips_start(
        pair_a, [jax.ShapeDtypeStruct(p.shape, p.dtype) for p in pair_a], False, "scatter_ffn1_start")

    out_g, out_d, out_m, out_v = {}, {}, {}, {}

    def finish(group, recvs, tag, after=None):
        halves = [_sum_slots(r, "sum_" + n, after) for n, r in zip(group, recvs)]
        for n, f in zip(group, _swap_halves(halves, "swap_" + tag)):
            g3 = f.reshape(1, f.shape[0] * f.shape[1], f.shape[2])
            d3, m3, v3 = _adamw(view(args[n], n), g3, view(args["m_" + n], n), view(args["v_" + n], n),
                                "adamw_" + n)
            out_g[n], out_d[n], out_m[n], out_v[n] = (view(t, n) for t in (g3, d3, m3, v3))

    finish(grp_b + grp_c, list(recv_b) + list(recv_c), "mix_ffn2", sa_token)

    late_all = _gather_all(_pack([sg[n] for n in late]), "gather_late_grads")
    sgr = dict(zip(early, _unpack(_sum_slots(early_all, "sum_early", sa_token), [sg[n].shape for n in early])))
    sgr.update(zip(late, _unpack(_sum_slots(late_all, "sum_late"), [sg[n].shape for n in late])))
    sgr["meta_tokens"] = lax.dynamic_slice_in_dim(sgr["meta_tokens"], chip * (D // NSH), D // NSH, axis=1)
    sgr["conv_dw"] = lax.dynamic_slice_in_dim(sgr["conv_dw"], chip * (DC // NSH), DC // NSH, axis=1)
    pshapes = [args[n].shape for n in small]
    d_s, m_s, v_s = _adamw(_pack([args[n] for n in small])[None], _pack([sgr[n] for n in small])[None],
                           _pack([args["m_" + n] for n in small])[None],
                           _pack([args["v_" + n] for n in small])[None], "adamw_small")
    for n, g_, d_, m_, v_ in zip(small, [sgr[n] for n in small], _unpack(d_s[0], pshapes),
                                 _unpack(m_s[0], pshapes), _unpack(v_s[0], pshapes)):
        out_g[n], out_d[n], out_m[n], out_v[n] = g_.reshape(args[n].shape), d_, m_, v_

    pair_a, recv_a = _chips_wait(sa_send, sa_recv, pair_a, land_s, [d_s, out_d[grp_c[-1]]], False,
                                 "scatter_ffn1_wait")
    finish(grp_a, _fill_own(pair_a, recv_a, "own_ffn1"), "ffn1")

    loss = lax.psum(loss_part[0, 0], ("x", "y", "c"))
    return (loss, grad_x, *[out_g[n] for n in names], *[out_d[n] for n in names],
            *[out_m[n] for n in names], *[out_v[n] for n in names])
```

```python
import math

import jax
import jax.numpy as jnp
from jax import lax
from jax.experimental import pallas as pl
from jax.experimental.pallas import tpu as pltpu

F32 = jnp.float32
BF16 = jnp.bfloat16

D = 1024
NSH = 4
F = 2816
FS = F // NSH
DC = 512
DS = 512
DIN = 2 * DC + DS + 2 * D
WS = DIN // NSH
KW = 31
KWP = 32
CONV_ROWS = 64
FFN_ROWS = 16
NMETA = 16
FRONT = 128
G, P, H = 32, 64, 16
NST = G * P
NQ = 4
QS = NST // NQ
QU = DS // NQ
NSEG = 32
NGRP = NSEG // 8
NCH = 8
SOFF = 8
EPS = 1e-6
LR, B1, B2, AEPS, WD, STEP = 1e-3, 0.9, 0.999, 1e-8, 0.01, 10
VMEM_LIMIT = 58 * 1024 * 1024
MESH = pl.DeviceIdType.MESH
ANY = pl.BlockSpec(memory_space=pl.ANY)


def _params(*sem):
    return pltpu.CompilerParams(dimension_semantics=sem, vmem_limit_bytes=VMEM_LIMIT)


def _res(shape):
    nd = len(shape)
    return pl.BlockSpec(shape, lambda *_: (0,) * nd, pipeline_mode=pl.Buffered(1))


def _tile(n, cap, mult=16):
    best = None
    for t in range(mult, min(n, cap) + 1, mult):
        if n % t == 0:
            best = t
    assert best is not None, (n, cap, mult)
    return best


def _dot(a, b):
    return jnp.dot(a, b, preferred_element_type=F32)


def _dot_nt(a, b):
    return lax.dot_general(a, b, (((1,), (1,)), ((), ())), preferred_element_type=F32)


def _dot_tn(a, b):
    return lax.dot_general(a, b, (((0,), (0,)), ((), ())), preferred_element_type=F32)


def _sigmoid(x):
    return 1.0 / (1.0 + jnp.exp(-x))


_GC = math.sqrt(2.0 / math.pi)
_GA = 0.044715


def _gelu(x):
    return 0.5 * x * (1.0 + jnp.tanh(_GC * (x + _GA * x * x * x)))


def _gelu_grad(x):
    t = jnp.tanh(_GC * (x + _GA * x * x * x))
    return 0.5 * (1.0 + t) + 0.5 * x * (1.0 - t * t) * _GC * (1.0 + 3.0 * _GA * x * x)


def _rms(hv, g):
    r = lax.rsqrt(jnp.mean(hv * hv, axis=-1, keepdims=True) + EPS)
    return hv * r * g, r


def _rms_bwd(dn, hv, r, g):
    xh = hv * r
    dxh = dn * g
    return r * (dxh - xh * jnp.mean(dxh * xh, axis=-1, keepdims=True)), xh


def _acc_rows(ref, part, first):
    @pl.when(first)
    def _():
        ref[...] = part

    @pl.when(jnp.logical_not(first))
    def _():
        ref[...] += part


def _coords():
    return lax.axis_index("x"), lax.axis_index("y"), lax.axis_index("c")


def _flip(v, d):
    return 1 - v if d else v


def _run(local, remote):
    for cp in local + remote:
        cp.start()
    for cp in remote:
        cp.wait()
    for cp in local:
        cp.wait()


def _via_vmem(src, dst, stage, sems, i):
    return (pltpu.make_async_copy(src, stage, sems.at[2 * i]), pltpu.make_async_copy(stage, dst, sems.at[2 * i + 1]))


def _run_staged(staged, remote):
    for load, _ in staged:
        load.start()
    for cp in remote:
        cp.start()
    for load, store in staged:
        load.wait()
        store.start()
    for cp in remote:
        cp.wait()
    for _, store in staged:
        store.wait()


_REL3 = ((1, 0), (0, 1), (1, 1))


class _Behind:
    def __init__(self, arrays, out_shapes, scratch, build):
        self.arrays, self.out_shapes, self.scratch, self.build = list(arrays), list(out_shapes), list(scratch), build

    def start(self, ins, outs, scr):
        staged, remote = self.build(ins, outs, scr)
        for load, _ in staged:
            load.start()
        for cp in remote:
            cp.start()

    def finish(self, ins, outs, scr):
        staged, remote = self.build(ins, outs, scr)
        for load, store in staged:
            load.wait()
            store.start()
        for cp in remote:
            cp.wait()
        for _, store in staged:
            store.wait()


def _call(body, comm, *, name, grid, in_specs, out_specs, out_shape, scratch_shapes=(), params):
    in_specs, out_specs, out_shape = list(in_specs), list(out_specs), list(out_shape)
    scratch_shapes = list(scratch_shapes)
    if comm is None:
        f = pl.pallas_call(body, name=name, grid=grid, in_specs=in_specs, out_specs=out_specs,
                           out_shape=out_shape, scratch_shapes=scratch_shapes, compiler_params=params)
        return lambda *args: (f(*args), [])
    ni, no, ns = len(in_specs), len(out_specs), len(scratch_shapes)
    ci, co = len(comm.arrays), len(comm.out_shapes)

    def hosted(*refs):
        ins, cin = refs[:ni], refs[ni:ni + ci]
        outs, cout = refs[ni + ci:ni + ci + no], refs[ni + ci + no:ni + ci + no + co]
        scr, cscr = refs[ni + ci + no + co:ni + ci + no + co + ns], refs[ni + ci + no + co + ns:]
        first = last = None
        for axis, size in enumerate(grid):
            i = pl.program_id(axis)
            first = (i == 0) if first is None else jnp.logical_and(first, i == 0)
            last = (i == size - 1) if last is None else jnp.logical_and(last, i == size - 1)

        @pl.when(first)
        def _():
            comm.start(cin, cout, cscr)

        body(*ins, *outs, *scr)

        @pl.when(last)
        def _():
            comm.finish(cin, cout, cscr)

    f = pl.pallas_call(hosted, name=name, grid=grid, in_specs=in_specs + [ANY] * ci,
                       out_specs=out_specs + [ANY] * co, out_shape=out_shape + comm.out_shapes,
                       scratch_shapes=scratch_shapes + comm.scratch,
                       compiler_params=_params(*(("arbitrary",) * len(grid))))

    def run(*args):
        res = f(*args, *comm.arrays)
        return res[:no], res[no:]

    return run


def _gather_half_behind(shards):
    n = len(shards)

    def build(ins, outs, scr):
        send, recv, loc = scr[:3]
        stage = scr[3:]
        x, y, c = _coords()
        me = 2 * x + y
        staged = [_via_vmem(ins[t], outs[t].at[me], stage[t], loc, t) for t in range(n)]
        remote = []
        for t in range(n):
            half = shards[t].shape[0] // 2
            mine = pl.ds(c * half, half)
            for k, (dx, dy) in enumerate(_REL3):
                remote.append(pltpu.make_async_remote_copy(
                    src_ref=ins[t].at[mine], dst_ref=outs[t].at[me, mine],
                    send_sem=send.at[3 * t + k], recv_sem=recv.at[3 * t + k],
                    device_id=(_flip(x, dx), _flip(y, dy), c), device_id_type=MESH))
        return staged, remote

    return _Behind(shards, [jax.ShapeDtypeStruct((NSH,) + s.shape, s.dtype) for s in shards],
                   [pltpu.SemaphoreType.DMA((3 * n,)), pltpu.SemaphoreType.DMA((3 * n,)),
                    pltpu.SemaphoreType.DMA((2 * n,))] + [pltpu.VMEM(s.shape, s.dtype) for s in shards], build)


def _pass_halves(gathered, name, own=()):
    n, m = len(gathered), len(own)

    def body(*refs):
        shards, outs = refs[n:n + m], refs[n + m:2 * n + m]
        send, recv, loc = refs[2 * n + m:2 * n + m + 3]
        stage = refs[2 * n + m + 3:]
        x, y, c = _coords()
        staged = [_via_vmem(shards[t], outs[t].at[2 * x + y], stage[t], loc, t) for t in range(m)]
        remote = []
        for t in range(n):
            half = gathered[t].shape[1] // 2
            mine = pl.ds(c * half, half)
            for k, (dx, dy) in enumerate(_REL3):
                slot = 2 * _flip(x, dx) + _flip(y, dy)
                remote.append(pltpu.make_async_remote_copy(
                    src_ref=outs[t].at[slot, mine], dst_ref=outs[t].at[slot, mine],
                    send_sem=send.at[3 * t + k], recv_sem=recv.at[3 * t + k],
                    device_id=(x, y, 1 - c), device_id_type=MESH))
        _run_staged(staged, remote)

    return pl.pallas_call(
        body, name=name,
        out_shape=[jax.ShapeDtypeStruct(g.shape, g.dtype) for g in gathered],
        in_specs=[ANY] * (n + m), out_specs=[ANY] * n, input_output_aliases={t: t for t in range(n)},
        scratch_shapes=[pltpu.SemaphoreType.DMA((3 * n,)), pltpu.SemaphoreType.DMA((3 * n,)),
                        pltpu.SemaphoreType.DMA((max(2 * m, 1),))] + [pltpu.VMEM(s.shape, s.dtype) for s in own],
        compiler_params=pltpu.CompilerParams(vmem_limit_bytes=VMEM_LIMIT),
    )(*gathered, *own)


def _fill_own(sums, recvs, name):
    n = len(sums)

    def body(*refs):
        ins, outs = refs[:n], refs[2 * n:3 * n]
        loc = refs[3 * n]
        stage = refs[3 * n + 1:]
        x, y, _ = _coords()
        me = 2 * x + y
        _run_staged([_via_vmem(ins[t].at[me], outs[t].at[me], stage[t], loc, t) for t in range(n)], [])

    return pl.pallas_call(
        body, name=name,
        out_shape=[jax.ShapeDtypeStruct(r.shape, r.dtype) for r in recvs],
        in_specs=[ANY] * (2 * n), out_specs=[ANY] * n, input_output_aliases={n + t: t for t in range(n)},
        scratch_shapes=[pltpu.SemaphoreType.DMA((2 * n,))] + [pltpu.VMEM(s.shape[1:], s.dtype) for s in sums],
        compiler_params=pltpu.CompilerParams(vmem_limit_bytes=VMEM_LIMIT),
    )(*sums, *recvs)


def _scatter_chips_behind(sums):
    n = len(sums)

    def build(ins, outs, scr):
        send, recv, loc = scr[:3]
        stage = scr[3:]
        x, y, c = _coords()
        me = 2 * x + y
        staged = [_via_vmem(ins[t].at[me], outs[t].at[me], stage[t], loc, t) for t in range(n)]
        remote = []
        for t in range(n):
            for k, (dx, dy) in enumerate(_REL3):
                px, py = _flip(x, dx), _flip(y, dy)
                remote.append(pltpu.make_async_remote_copy(
                    src_ref=ins[t].at[2 * px + py], dst_ref=outs[t].at[me],
                    send_sem=send.at[3 * t + k], recv_sem=recv.at[3 * t + k],
                    device_id=(px, py, c), device_id_type=MESH))
        return staged, remote

    return _Behind(sums, [jax.ShapeDtypeStruct(s.shape, s.dtype) for s in sums],
                   [pltpu.SemaphoreType.DMA((3 * n,)), pltpu.SemaphoreType.DMA((3 * n,)),
                    pltpu.SemaphoreType.DMA((2 * n,))] + [pltpu.VMEM(s.shape[1:], s.dtype) for s in sums], build)


def _gather_all_behind(a):
    def build(ins, outs, scr):
        send, recv, loc, stage = scr
        x, y, c = _coords()
        me = 4 * x + 2 * y + c
        staged = [_via_vmem(ins[0], outs[0].at[me], stage, loc, 0)]
        remote = [pltpu.make_async_remote_copy(
            src_ref=ins[0], dst_ref=outs[0].at[me], send_sem=send.at[k], recv_sem=recv.at[k],
            device_id=(_flip(x, dx), _flip(y, dy), _flip(c, dc)), device_id_type=MESH)
            for k, (dx, dy, dc) in enumerate(_REL7)]
        return staged, remote

    return _Behind([a], [jax.ShapeDtypeStruct((8,) + a.shape, a.dtype)],
                   [pltpu.SemaphoreType.DMA((7,)), pltpu.SemaphoreType.DMA((7,)), pltpu.SemaphoreType.DMA((2,)),
                    pltpu.VMEM(a.shape, a.dtype)], build)


HBM = pl.BlockSpec(memory_space=pltpu.HBM)
SEM = pl.BlockSpec(memory_space=pltpu.SEMAPHORE)
EFFECT = pltpu.SideEffectType.DATAFLOW_SIDE_EFFECTING


def _chip_copies(srcs, lands, send, recv, gather):
    x, y, c = _coords()
    me = 2 * x + y
    cps = []
    for t in range(len(srcs)):
        for k, (dx, dy) in enumerate(_REL3):
            px, py = _flip(x, dx), _flip(y, dy)
            if gather:
                half = srcs[t].shape[0] // 2
                mine = pl.ds(c * half, half)
                src, dst = srcs[t].at[mine], lands[t].at[me, mine]
            else:
                src, dst = srcs[t].at[2 * px + py], lands[t].at[me]
            cps.append(pltpu.make_async_remote_copy(
                src_ref=src, dst_ref=dst, send_sem=send.at[3 * t + k], recv_sem=recv.at[3 * t + k],
                device_id=(px, py, c), device_id_type=MESH))
    return cps


def _chips_start(arrays, land_shapes, gather, name, after=()):
    n = len(arrays)

    def body(*refs):
        srcs, lands = refs[:n], refs[n:2 * n]
        send, recv = refs[2 * n + len(after)], refs[2 * n + len(after) + 1]
        token = refs[-1]
        for cp in _chip_copies(srcs, lands, send, recv, gather):
            cp.start()
        token[...] = jnp.zeros_like(token)

    lands = [lax.empty(s.shape, s.dtype) for s in land_shapes]
    thru = [pltpu.HBM(a.shape, a.dtype) for a in arrays] + [pltpu.HBM(s.shape, s.dtype) for s in land_shapes]
    res = pl.pallas_call(
        body, name=name,
        out_shape=(pltpu.SemaphoreType.DMA((3 * n,)), pltpu.SemaphoreType.DMA((3 * n,)), *thru,
                   jax.ShapeDtypeStruct((8, 128), F32)),
        in_specs=[HBM] * (2 * n) + [ANY] * len(after),
        out_specs=(SEM, SEM, *([HBM] * (2 * n)), pl.BlockSpec(memory_space=pltpu.VMEM)),
        input_output_aliases={t: 2 + t for t in range(2 * n)},
        compiler_params=pltpu.CompilerParams(has_side_effects=EFFECT),
    )(*[pltpu.with_memory_space_constraint(a, pltpu.HBM) for a in arrays],
      *[pltpu.with_memory_space_constraint(z, pltpu.HBM) for z in lands], *after)
    return res[0], res[1], list(res[2:2 + n]), list(res[2 + n:2 + 2 * n]), res[-1]


def _chips_wait(send, recv, arrays, lands, after, gather, name):
    n = len(arrays)

    def body(*refs):
        srcs, ls = refs[:n], refs[n:2 * n]
        sd, rv = refs[2 * n], refs[2 * n + 1]
        for cp in _chip_copies(srcs, ls, sd, rv, gather):
            cp.wait_send()
            cp.wait_recv()

    res = pl.pallas_call(
        body, name=name,
        out_shape=[pltpu.HBM(a.shape, a.dtype) for a in arrays] + [pltpu.HBM(z.shape, z.dtype) for z in lands],
        in_specs=[HBM] * (2 * n) + [SEM, SEM] + [ANY] * len(after), out_specs=[HBM] * (2 * n),
        input_output_aliases={t: t for t in range(2 * n)},
        compiler_params=pltpu.CompilerParams(has_side_effects=EFFECT),
    )(*arrays, *lands, send, recv, *after)
    return list(res[:n]), list(res[n:])


def _join(*parts):
    def cut(seq, key):
        res, o = [], 0
        for p in parts:
            k = len(getattr(p, key))
            res.append(seq[o:o + k])
            o += k
        return res

    def build(ins, outs, scr):
        staged, remote = [], []
        for p, i, o, s in zip(parts, cut(ins, "arrays"), cut(outs, "out_shapes"), cut(scr, "scratch")):
            st, rm = p.build(i, o, s)
            staged += st
            remote += rm
        return staged, remote

    return _Behind(sum((p.arrays for p in parts), []), sum((p.out_shapes for p in parts), []),
                   sum((p.scratch for p in parts), []), build)


def _gather_chips(shards, name):
    n = len(shards)

    def body(*refs):
        ins, outs = refs[:n], refs[n:2 * n]
        send, recv, fsend, frecv, loc = refs[2 * n:2 * n + 5]
        stage = refs[2 * n + 5:]
        x, y, c = _coords()
        me = 2 * x + y
        own = [_via_vmem(ins[t], outs[t].at[me], stage[t], loc, t) for t in range(n)]
        first, passed = [], []
        for t in range(n):
            half = shards[t].shape[0] // 2
            mine, theirs = pl.ds(c * half, half), pl.ds((1 - c) * half, half)
            for k, (dx, dy) in enumerate(_REL3):
                px, py = _flip(x, dx), _flip(y, dy)
                first.append(pltpu.make_async_remote_copy(
                    src_ref=ins[t].at[mine], dst_ref=outs[t].at[me, mine],
                    send_sem=send.at[3 * t + k], recv_sem=recv.at[3 * t + k],
                    device_id=(px, py, c), device_id_type=MESH))
                passed.append((
                    pltpu.make_async_remote_copy(
                        src_ref=outs[t].at[2 * px + py, mine], dst_ref=outs[t].at[2 * px + py, mine],
                        send_sem=fsend.at[3 * t + k], recv_sem=frecv.at[3 * t + k],
                        device_id=(x, y, 1 - c), device_id_type=MESH),
                    pltpu.make_async_remote_copy(
                        src_ref=outs[t].at[2 * px + py, theirs], dst_ref=outs[t].at[2 * px + py, theirs],
                        send_sem=fsend.at[3 * t + k], recv_sem=frecv.at[3 * t + k],
                        device_id=(x, y, 1 - c), device_id_type=MESH)))
        for load, _ in own:
            load.start()
        for cp in first:
            cp.start()
        for load, store in own:
            load.wait()
            store.start()
        for cp, (fwd, _) in zip(first, passed):
            cp.wait_recv()
            fwd.start()
        for cp, (fwd, back) in zip(first, passed):
            cp.wait_send()
            fwd.wait_send()
            back.wait_recv()
        for _, store in own:
            store.wait()

    return pl.pallas_call(
        body, name=name,
        out_shape=[jax.ShapeDtypeStruct((NSH,) + s.shape, s.dtype) for s in shards],
        in_specs=[ANY] * n, out_specs=[ANY] * n,
        scratch_shapes=[pltpu.SemaphoreType.DMA((3 * n,)) for _ in range(4)] + [pltpu.SemaphoreType.DMA((2 * n,))]
        + [pltpu.VMEM(s.shape, s.dtype) for s in shards],
        compiler_params=pltpu.CompilerParams(vmem_limit_bytes=VMEM_LIMIT),
    )(*shards)


_REL7 = tuple((dx, dy, dc) for dx in (0, 1) for dy in (0, 1) for dc in (0, 1))[1:]


def _gather_all(a, name):
    def body(a_ref, o_ref, send, recv, loc):
        x, y, c = _coords()
        me = 4 * x + 2 * y + c
        local = [pltpu.make_async_copy(a_ref, o_ref.at[me], loc.at[0])]
        remote = [pltpu.make_async_remote_copy(
            src_ref=a_ref, dst_ref=o_ref.at[me], send_sem=send.at[k], recv_sem=recv.at[k],
            device_id=(_flip(x, dx), _flip(y, dy), _flip(c, dc)), device_id_type=MESH)
            for k, (dx, dy, dc) in enumerate(_REL7)]
        _run(local, remote)

    return pl.pallas_call(
        body, name=name,
        out_shape=jax.ShapeDtypeStruct((8,) + a.shape, a.dtype),
        in_specs=[ANY], out_specs=ANY,
        scratch_shapes=[pltpu.SemaphoreType.DMA((7,)), pltpu.SemaphoreType.DMA((7,)),
                        pltpu.SemaphoreType.DMA((1,))],
    )(a)


def _pair_exchange(grads, name):
    n = len(grads)

    def body(*refs):
        ins, outs = refs[:n], refs[n:2 * n]
        send, recv = refs[2 * n:]
        x, y, c = _coords()
        remote = []
        for t in range(n):
            half = grads[t].shape[1] // 2
            remote.append(pltpu.make_async_remote_copy(
                src_ref=ins[t].at[:, pl.ds((1 - c) * half, half)], dst_ref=outs[t],
                send_sem=send.at[t], recv_sem=recv.at[t],
                device_id=(x, y, 1 - c), device_id_type=MESH))
        _run([], remote)

    return pl.pallas_call(
        body, name=name,
        out_shape=[jax.ShapeDtypeStruct((NSH, g.shape[1] // 2, g.shape[2]), g.dtype) for g in grads],
        in_specs=[ANY] * n, out_specs=[ANY] * n,
        scratch_shapes=[pltpu.SemaphoreType.DMA((n,)), pltpu.SemaphoreType.DMA((n,))],
    )(*grads)


def _scatter_chips(sums, name):
    n = len(sums)

    def body(*refs):
        ins, outs = refs[:n], refs[n:2 * n]
        send, recv, loc = refs[2 * n:2 * n + 3]
        stage = refs[2 * n + 3:]
        x, y, c = _coords()
        me = 2 * x + y
        local = [_via_vmem(ins[t].at[me], outs[t].at[me], stage[t], loc, t) for t in range(n)]
        remote = []
        for t in range(n):
            for k, (dx, dy) in enumerate(_REL3):
                px, py = _flip(x, dx), _flip(y, dy)
                remote.append(pltpu.make_async_remote_copy(
                    src_ref=ins[t].at[2 * px + py], dst_ref=outs[t].at[me],
                    send_sem=send.at[3 * t + k], recv_sem=recv.at[3 * t + k],
                    device_id=(px, py, c), device_id_type=MESH))
        _run_staged(local, remote)

    return pl.pallas_call(
        body, name=name,
        out_shape=[jax.ShapeDtypeStruct(s.shape, s.dtype) for s in sums],
        in_specs=[ANY] * n, out_specs=[ANY] * n,
        scratch_shapes=[pltpu.SemaphoreType.DMA((3 * n,)), pltpu.SemaphoreType.DMA((3 * n,)),
                        pltpu.SemaphoreType.DMA((2 * n,))]
        + [pltpu.VMEM(s.shape[1:], s.dtype) for s in sums],
        compiler_params=pltpu.CompilerParams(vmem_limit_bytes=VMEM_LIMIT),
    )(*sums)


def _swap_halves(halves, name):
    n = len(halves)

    def body(*refs):
        ins, outs = refs[:n], refs[n:2 * n]
        send, recv, loc = refs[2 * n:2 * n + 3]
        stage = refs[2 * n + 3:]
        x, y, c = _coords()
        local = [_via_vmem(ins[t], outs[t].at[c], stage[t], loc, t) for t in range(n)]
        remote = [pltpu.make_async_remote_copy(
            src_ref=ins[t], dst_ref=outs[t].at[c], send_sem=send.at[t], recv_sem=recv.at[t],
            device_id=(x, y, 1 - c), device_id_type=MESH) for t in range(n)]
        _run_staged(local, remote)

    return pl.pallas_call(
        body, name=name,
        out_shape=[jax.ShapeDtypeStruct((2,) + h.shape, h.dtype) for h in halves],
        in_specs=[ANY] * n, out_specs=[ANY] * n,
        scratch_shapes=[pltpu.SemaphoreType.DMA((n,)), pltpu.SemaphoreType.DMA((n,)),
                        pltpu.SemaphoreType.DMA((2 * n,))]
        + [pltpu.VMEM(h.shape, h.dtype) for h in halves],
        compiler_params=pltpu.CompilerParams(vmem_limit_bytes=VMEM_LIMIT),
    )(*halves)


def _sum_slots(r, name, after=None):
    K, R, C = r.shape
    tr = _tile(R, max(16, (1 << 22) // (K * C)), 8 * (4 // r.dtype.itemsize))

    def body(r_ref, *rest):
        o_ref = rest[-1]
        acc = r_ref[0].astype(F32)
        for k in range(1, K):
            acc = acc + r_ref[k].astype(F32)
        o_ref[...] = acc

    dep = [] if after is None else [after]
    return pl.pallas_call(
        body, name=name, grid=(R // tr,),
        out_shape=jax.ShapeDtypeStruct((R, C), F32),
        in_specs=[pl.BlockSpec((K, tr, C), lambda i: (0, i, 0))] + [ANY] * len(dep),
        out_specs=pl.BlockSpec((tr, C), lambda i: (i, 0)),
        compiler_params=_params("parallel"),
    )(r, *dep)


def _add_pair(g, s, core, name):
    _, half, C = s.shape
    tr = _tile(half, max(16, (1 << 19) // C))
    nb = half // tr

    def body(c_ref, g_ref, s_ref, o_ref):
        o_ref[...] = (g_ref[...].astype(F32) + s_ref[...].astype(F32)).astype(BF16)

    spec = pl.BlockSpec((1, tr, C), lambda j, i, c_ref: (j, i, 0))
    return pl.pallas_call(
        body, name=name,
        grid_spec=pltpu.PrefetchScalarGridSpec(
            num_scalar_prefetch=1, grid=(NSH, nb),
            in_specs=[pl.BlockSpec((1, tr, C), lambda j, i, c_ref: (j, c_ref[0] * nb + i, 0)), spec],
            out_specs=spec),
        out_shape=jax.ShapeDtypeStruct(s.shape, BF16),
        compiler_params=_params("parallel", "parallel"),
    )(core, g, s)


def _adamw(w, g, m, v, name):
    _, R, C = w.shape
    tr = _tile(R, max(8, (1 << 18) // C), 8)
    c1 = 1.0 / (1.0 - B1 ** STEP)
    c2 = 1.0 / (1.0 - B2 ** STEP)

    def body(w_ref, g_ref, m_ref, v_ref, d_ref, nm_ref, nv_ref):
        gv = g_ref[...]
        nm = B1 * m_ref[...] + (1.0 - B1) * gv
        nv = B2 * v_ref[...] + (1.0 - B2) * gv * gv
        nm_ref[...] = nm
        nv_ref[...] = nv
        d_ref[...] = -LR * ((nm * c1) / (jnp.sqrt(nv * c2) + AEPS) + WD * w_ref[...])

    spec = pl.BlockSpec((1, tr, C), lambda i: (0, i, 0))
    return pl.pallas_call(
        body, name=name, grid=(R // tr,),
        out_shape=[jax.ShapeDtypeStruct((1, R, C), F32)] * 3,
        in_specs=[spec] * 4, out_specs=[spec] * 3,
        compiler_params=_params("parallel"),
    )(w, g, m, v)


def _ffn_fwd(h, g, w1, w3, w2, name, comm=None):
    L = h.shape[0]
    tm = _tile(L, 704)

    def body(h_ref, g_ref, w1_ref, w3_ref, w2_ref, o_ref, a_ref, b_ref, n_s, acc_s):
        j = pl.program_id(1)

        @pl.when(j == 0)
        def _():
            hv = h_ref[...]
            n, _ = _rms(hv, g_ref[...])
            n_s[...] = n.astype(BF16)
            acc_s[...] = hv

        n = n_s[...]
        a = _dot_nt(n, w1_ref[0])
        b = _dot_nt(n, w3_ref[0])
        a_ref[0] = a.astype(BF16)
        b_ref[0] = b.astype(BF16)
        s = (a * _sigmoid(a) * b).astype(BF16)
        acc_s[...] += 0.5 * _dot(s, w2_ref[0])

        @pl.when(j == NSH - 1)
        def _():
            o_ref[...] = acc_s[...]

    row = pl.BlockSpec((tm, D), lambda i, j: (i, 0))
    hid = pl.BlockSpec((1, tm, FS), lambda i, j: (j, i, 0))
    wsp = pl.BlockSpec((1, FS, D), lambda i, j: (j, 0, 0))
    return _call(
        body, comm, name=name, grid=(L // tm, NSH),
        out_shape=[jax.ShapeDtypeStruct((L, D), F32),
                   jax.ShapeDtypeStruct((NSH, L, FS), BF16), jax.ShapeDtypeStruct((NSH, L, FS), BF16)],
        in_specs=[row, _res((1, D)), wsp, wsp, wsp],
        out_specs=[row, hid, hid],
        scratch_shapes=[pltpu.VMEM((tm, D), BF16), pltpu.VMEM((tm, D), F32)],
        params=_params("arbitrary", "arbitrary"),
    )(h, g, w1, w3, w2)


def _ffn_bwd(h, g, dout, a, b, w1, w3, w2, name, comm=None):
    L = h.shape[0]
    tm = _tile(L, 528)

    def body(h_ref, g_ref, do_ref, a_ref, b_ref, w1_ref, w3_ref, w2_ref,
             dh_ref, da_ref, db_ref, s_ref, n_ref, dg_ref, dob_s, dn_s, ds_s):
        i, j = pl.program_id(0), pl.program_id(1)

        @pl.when(j == 0)
        def _():
            n, _ = _rms(h_ref[...], g_ref[...])
            n_ref[...] = n.astype(BF16)
            dob_s[...] = (0.5 * do_ref[...]).astype(BF16)
            dn_s[...] = jnp.zeros_like(dn_s)

        ds_s[...] = _dot_nt(dob_s[...], w2_ref[0])

        def chunk(c, carry):
            rows = pl.ds(pl.multiple_of(c * FFN_ROWS, FFN_ROWS), FFN_ROWS)
            av = a_ref[0, rows, :].astype(F32)
            bv = b_ref[0, rows, :].astype(F32)
            ds = ds_s[rows, :]
            sig = _sigmoid(av)
            sa = av * sig
            s_ref[0, rows, :] = (sa * bv).astype(BF16)
            da_ref[0, rows, :] = (ds * bv * (sig + sa * (1.0 - sig))).astype(BF16)
            db_ref[0, rows, :] = (ds * sa).astype(BF16)
            return carry

        lax.fori_loop(0, tm // FFN_ROWS, chunk, 0)
        dn_s[...] += _dot(da_ref[0], w1_ref[0]) + _dot(db_ref[0], w3_ref[0])

        @pl.when(j == NSH - 1)
        def _():
            hv = h_ref[...]
            gv = g_ref[...]
            r = lax.rsqrt(jnp.mean(hv * hv, axis=-1, keepdims=True) + EPS)
            dn = dn_s[...]
            dx, xh = _rms_bwd(dn, hv, r, gv)
            dh_ref[...] = do_ref[...] + dx
            _acc_rows(dg_ref, jnp.sum(dn * xh, axis=0, keepdims=True), i == 0)

    row = pl.BlockSpec((tm, D), lambda i, j: (i, 0))
    hid = pl.BlockSpec((1, tm, FS), lambda i, j: (j, i, 0))
    wsp = pl.BlockSpec((1, FS, D), lambda i, j: (j, 0, 0))
    return _call(
        body, comm, name=name, grid=(L // tm, NSH),
        out_shape=[jax.ShapeDtypeStruct((L, D), F32)]
        + [jax.ShapeDtypeStruct((NSH, L, FS), BF16)] * 3
        + [jax.ShapeDtypeStruct((L, D), BF16), jax.ShapeDtypeStruct((1, D), F32)],
        in_specs=[row, _res((1, D)), row, hid, hid,
                  wsp, wsp, wsp],
        out_specs=[row, hid, hid, hid, row, pl.BlockSpec((1, D), lambda i, j: (0, 0))],
        scratch_shapes=[pltpu.VMEM((tm, D), BF16), pltpu.VMEM((tm, D), F32), pltpu.VMEM((tm, FS), F32)],
        params=_params("arbitrary", "arbitrary"),
    )(h, g, dout, a, b, w1, w3, w2)


def _wgrad(xm, ym, name, scale=1.0):
    xs, ys = xm.ndim == 3, ym.ndim == 3
    assert not (xs and ys)
    L = xm.shape[-2]
    K, N = xm.shape[-1], ym.shape[-1]
    tl = _tile(L, 1056)
    nl = L // tl
    if xs or ys:
        tn, grid_n = N, NSH
    else:
        tn = _tile(N, 1024, 128)
        grid_n = N // tn

    def body(x_ref, y_ref, o_ref, acc_s):
        l = pl.program_id(1)
        xv = x_ref[0] if xs else x_ref[...]
        yv = y_ref[0] if ys else y_ref[...]
        part = _dot_tn(xv.astype(BF16), yv.astype(BF16))
        _acc_rows(acc_s, part, l == 0)

        @pl.when(l == nl - 1)
        def _():
            res = (acc_s[...] * scale).astype(BF16)
            if xs or ys:
                o_ref[0] = res
            else:
                o_ref[...] = res

    if xs:
        x_spec = pl.BlockSpec((1, tl, K), lambda n, l: (n, l, 0))
        y_spec = pl.BlockSpec((tl, N), lambda n, l: (l, 0))
        o_spec = pl.BlockSpec((1, K, N), lambda n, l: (n, 0, 0))
        o_shape = (NSH, K, N)
    elif ys:
        x_spec = pl.BlockSpec((tl, K), lambda n, l: (l, 0))
        y_spec = pl.BlockSpec((1, tl, N), lambda n, l: (n, l, 0))
        o_spec = pl.BlockSpec((1, K, N), lambda n, l: (n, 0, 0))
        o_shape = (NSH, K, N)
    else:
        x_spec = pl.BlockSpec((tl, K), lambda n, l: (l, 0))
        y_spec = pl.BlockSpec((tl, tn), lambda n, l: (l, n))
        o_spec = pl.BlockSpec((K, tn), lambda n, l: (0, n))
        o_shape = (K, N)
    return pl.pallas_call(
        body, name=name, grid=(grid_n, nl),
        out_shape=jax.ShapeDtypeStruct(o_shape, BF16),
        in_specs=[x_spec, y_spec], out_specs=o_spec,
        scratch_shapes=[pltpu.VMEM((K, tn), F32)],
        compiler_params=_params("parallel", "arbitrary"),
    )(xm, ym)


def _mix_in_fwd(h, g, w_in, b_gate, name, comm=None):
    L = h.shape[0]
    tm = _tile(L, 528)

    def body(h_ref, g_ref, w_ref, bg_ref, vg_ref, uf_ref, gt_ref):
        u, _ = _rms(h_ref[...], g_ref[...])
        ub = u.astype(BF16)
        p = [_dot(ub, w_ref[j]) for j in range(NSH)]
        a0, a1 = 2 * DC - WS, 2 * DC + DS - WS
        vg_ref[:, 0:WS] = p[0].astype(BF16)
        vg_ref[:, WS:2 * DC] = p[1][:, 0:a0].astype(BF16)
        uf_ref[...] = p[1][:, a0:a1].astype(BF16)
        gin = jnp.concatenate([p[1][:, a1:], p[2], p[3]], axis=1)
        gt_ref[...] = _sigmoid(gin + bg_ref[...]).astype(BF16)

    def row(n):
        return pl.BlockSpec((tm, n), lambda i: (i, 0))

    return _call(
        body, comm, name=name, grid=(L // tm,),
        out_shape=[jax.ShapeDtypeStruct((L, 2 * DC), BF16), jax.ShapeDtypeStruct((L, DS), BF16),
                   jax.ShapeDtypeStruct((L, 2 * D), BF16)],
        in_specs=[row(D), _res((1, D)), _res((NSH, D, WS)), _res((1, 2 * D))],
        out_specs=[row(2 * DC), row(DS), row(2 * D)],
        params=_params("parallel"),
    )(h, g, w_in, b_gate)


def _mix_in_bwd(h, g, dres, dv, dgl, duf, dgate, w_in, name):
    L = h.shape[0]
    tm = _tile(L, 528)

    def body(h_ref, g_ref, dr_ref, dv_ref, dgl_ref, duf_ref, dgt_ref, w_ref, dh_ref, u_ref, dp_ref, dgm_ref):
        i = pl.program_id(0)
        hv = h_ref[...]
        gv = g_ref[...]
        u, r = _rms(hv, gv)
        u_ref[...] = u.astype(BF16)
        a0, a1 = 2 * DC - WS, 2 * DC + DS - WS
        b0 = WS - a1
        dp = [jnp.concatenate([dv_ref[...], dgl_ref[:, 0:WS - DC]], axis=1),
              jnp.concatenate([dgl_ref[:, WS - DC:], duf_ref[...], dgt_ref[:, 0:b0]], axis=1),
              dgt_ref[:, b0:b0 + WS], dgt_ref[:, b0 + WS:]]
        du = jnp.zeros((tm, D), F32)
        for j in range(NSH):
            dp_ref[j] = dp[j]
            du = du + _dot_nt(dp[j], w_ref[j])
        dx, xh = _rms_bwd(du, hv, r, gv)
        dh_ref[...] = dr_ref[...] + dx
        _acc_rows(dgm_ref, jnp.sum(du * xh, axis=0, keepdims=True), i == 0)

    def row(n):
        return pl.BlockSpec((tm, n), lambda i: (i, 0))

    return pl.pallas_call(
        body, name=name, grid=(L // tm,),
        out_shape=[jax.ShapeDtypeStruct((L, D), F32), jax.ShapeDtypeStruct((L, D), BF16),
                   jax.ShapeDtypeStruct((NSH, L, WS), BF16), jax.ShapeDtypeStruct((1, D), F32)],
        in_specs=[row(D), _res((1, D)), row(D), row(DC), row(DC), row(DS), row(2 * D), _res((NSH, D, WS))],
        out_specs=[row(D), row(D), pl.BlockSpec((NSH, tm, WS), lambda i: (0, i, 0)),
                   pl.BlockSpec((1, D), lambda i: (0, 0))],
        compiler_params=_params("arbitrary"),
    )(h, g, dres, dv, dgl, duf, dgate, w_in)


def _conv_fwd(vg, dw, dwb, name, comm=None):
    L = vg.shape[0]
    nc = DC // 128

    def body(v_ref, g_ref, dw_ref, dwb_ref, z_ref, zp_s):
        zp_s[0:KWP, :] = jnp.zeros((KWP, 128), F32)
        zp_s[KWP:, :] = v_ref[...].astype(F32) * _sigmoid(g_ref[...].astype(F32))
        for r0 in range(0, L, CONV_ROWS):
            acc = jnp.broadcast_to(dwb_ref[...], (CONV_ROWS, 128))
            for k in range(KW):
                acc = acc + dw_ref[k:k + 1, :] * zp_s[pl.ds(r0 + k + 2, CONV_ROWS), :]
            z_ref[pl.ds(r0, CONV_ROWS), :] = acc

    return _call(
        body, comm, name=name, grid=(nc,),
        out_shape=[jax.ShapeDtypeStruct((L, DC), F32)],
        in_specs=[pl.BlockSpec((L, 128), lambda c: (0, c)), pl.BlockSpec((L, 128), lambda c: (0, nc + c)),
                  pl.BlockSpec((KWP, 128), lambda c: (0, c)), pl.BlockSpec((1, 128), lambda c: (0, c))],
        out_specs=[pl.BlockSpec((L, 128), lambda c: (0, c))],
        scratch_shapes=[pltpu.VMEM((L + KWP, 128), F32)],
        params=_params("parallel"),
    )(vg, vg, dw, dwb)


def _conv_bwd(dz1, vg, dw, name):
    L = vg.shape[0]
    nc = DC // 128

    def body(dz_ref, v_ref, g_ref, dw_ref, dv_ref, dg_ref, ddw_ref, ddwb_ref, zp_s, dzp_s):
        vv = v_ref[...].astype(F32)
        sg = _sigmoid(g_ref[...].astype(F32))
        zp_s[0:KWP, :] = jnp.zeros((KWP, 128), F32)
        zp_s[KWP:, :] = vv * sg
        dz = dz_ref[...]
        dzp_s[0:L, :] = dz
        dzp_s[L:, :] = jnp.zeros((KWP, 128), F32)
        ddwb_ref[...] = jnp.sum(dz, axis=0, keepdims=True)
        part = [jnp.zeros((8, 128), F32) for _ in range(KW)]
        for r0 in range(0, L, CONV_ROWS):
            rows = pl.ds(r0, CONV_ROWS)
            dzc = dz_ref[rows, :]
            acc = jnp.zeros((CONV_ROWS, 128), F32)
            for k in range(KW):
                acc = acc + dw_ref[k:k + 1, :] * dzp_s[pl.ds(r0 + KW - 1 - k, CONV_ROWS), :]
                prod = dzc * zp_s[pl.ds(r0 + k + 2, CONV_ROWS), :]
                for q in range(CONV_ROWS // 8):
                    part[k] = part[k] + prod[8 * q:8 * (q + 1), :]
            vc = v_ref[rows, :].astype(F32)
            sc = _sigmoid(g_ref[rows, :].astype(F32))
            dv_ref[rows, :] = (acc * sc).astype(BF16)
            dg_ref[rows, :] = (acc * vc * sc * (1.0 - sc)).astype(BF16)
        for k in range(KW):
            ddw_ref[k:k + 1, :] = jnp.sum(part[k], axis=0, keepdims=True)
        ddw_ref[KW:KWP, :] = jnp.zeros((KWP - KW, 128), F32)

    col = pl.BlockSpec((L, 128), lambda c: (0, c))
    return pl.pallas_call(
        body, name=name, grid=(nc,),
        out_shape=[jax.ShapeDtypeStruct((L, DC), BF16), jax.ShapeDtypeStruct((L, DC), BF16),
                   jax.ShapeDtypeStruct((KWP, DC), F32), jax.ShapeDtypeStruct((1, DC), F32)],
        in_specs=[col, col, pl.BlockSpec((L, 128), lambda c: (0, nc + c)),
                  pl.BlockSpec((KWP, 128), lambda c: (0, c))],
        out_specs=[col, col, pl.BlockSpec((KWP, 128), lambda c: (0, c)), pl.BlockSpec((1, 128), lambda c: (0, c))],
        scratch_shapes=[pltpu.VMEM((L + KWP, 128), F32), pltpu.VMEM((L + KWP, 128), F32)],
        compiler_params=_params("parallel"),
    )(dz1, vg, vg, dw)


NLB = QS // 128


def _lb_store(ref, rows, val):
    for cb in range(NLB):
        ref[cb, rows, :] = val[:, cb * 128:(cb + 1) * 128]


def _lb_load(ref, rows):
    return jnp.concatenate([ref[cb, rows, :] for cb in range(NLB)], axis=1)


def _scan(xr_ref, xi_ref, base, T, ar, ai, atr, ati, reverse):
    W = ar.shape[1]
    ar, ai, atr, ati = (jnp.broadcast_to(v, (8, W)) for v in (ar, ai, atr, ati))
    zero = jnp.zeros((8, W), F32)

    def rows(t, g):
        tt = T - 1 - t if reverse else t
        return pl.ds(base + g * 8 * T + tt, 8, stride=T)

    def make_step(store):
        def step(t, carry):
            out = []
            for g in range(NGRP):
                sr, si = carry[2 * g], carry[2 * g + 1]
                idx = rows(t, g)
                nr = ar * sr - ai * si + _lb_load(xr_ref, idx)
                ni = ar * si + ai * sr + _lb_load(xi_ref, idx)
                if store:
                    _lb_store(xr_ref, idx, nr)
                    _lb_store(xi_ref, idx, ni)
                out += [nr, ni]
            return tuple(out)
        return step

    ends = lax.fori_loop(0, T, make_step(False), (zero,) * (2 * NGRP))
    sub = lax.broadcasted_iota(jnp.int32, (8, W), 0)
    edge = sub == (7 if reverse else 0)
    shift, last = (7, 0) if reverse else (1, 7)
    inr, ini = jnp.zeros((1, W), F32), jnp.zeros((1, W), F32)
    starts = [None] * (2 * NGRP)
    for g in (reversed(range(NGRP)) if reverse else range(NGRP)):
        er, ei = ends[2 * g], ends[2 * g + 1]
        cr, ci = jnp.where(edge, inr, 0.0), jnp.where(edge, ini, 0.0)
        for _ in range(7):
            nr = atr * cr - ati * ci + er
            ni = atr * ci + ati * cr + ei
            cr = jnp.where(edge, inr, pltpu.roll(nr, shift, 0))
            ci = jnp.where(edge, ini, pltpu.roll(ni, shift, 0))
        starts[2 * g], starts[2 * g + 1] = cr, ci
        inr = (atr * cr - ati * ci + er)[last:last + 1]
        ini = (atr * ci + ati * cr + ei)[last:last + 1]
    lax.fori_loop(0, T, make_step(True), tuple(starts))


def _ssm_fwd(uf, bre, bim, cre, cim, lamp, dsk, name, comm=None):
    L = uf.shape[0]
    T = L // NSEG
    tc = L // NCH

    def body(u_ref, bre_ref, bim_ref, cre_ref, cim_ref, lam_ref, d_ref, y_ref, sr_s, si_s):
        for k in range(NCH):
            sl = slice(k * tc, (k + 1) * tc)
            uk = u_ref[sl, :]
            _lb_store(sr_s, sl, _dot(uk, bre_ref[...]))
            _lb_store(si_s, sl, _dot(uk, bim_ref[...]))
        _scan(sr_s, si_s, 0, T, lam_ref[0:1, :], lam_ref[1:2, :], lam_ref[2:3, :], lam_ref[3:4, :], False)
        for k in range(NCH):
            sl = slice(k * tc, (k + 1) * tc)
            y_ref[sl, :] = (_dot(_lb_load(sr_s, sl).astype(BF16), cre_ref[...])
                            - _dot(_lb_load(si_s, sl).astype(BF16), cim_ref[...])
                            + d_ref[...] * u_ref[sl, :].astype(F32))

    return _call(
        body, comm, name=name, grid=(NQ,),
        out_shape=[jax.ShapeDtypeStruct((L, DS), F32)],
        in_specs=[pl.BlockSpec((L, QU), lambda q: (0, q)),
                  pl.BlockSpec((QU, QS), lambda q: (q, q)), pl.BlockSpec((QU, QS), lambda q: (q, q)),
                  pl.BlockSpec((QS, QU), lambda q: (q, q)), pl.BlockSpec((QS, QU), lambda q: (q, q)),
                  pl.BlockSpec((8, QS), lambda q: (0, q)), pl.BlockSpec((1, QU), lambda q: (0, q))],
        out_specs=[pl.BlockSpec((L, QU), lambda q: (0, q))],
        scratch_shapes=[pltpu.VMEM((NLB, L, 128), F32), pltpu.VMEM((NLB, L, 128), F32)],
        params=_params("parallel"),
    )(uf, bre, bim, cre, cim, lamp, dsk)


def _ssm_bwd(uf, dyss, bre, bim, cre, cim, lamp, dsk, name, comm=None):
    L = uf.shape[0]
    T = L // NSEG
    tc = L // NCH

    def body(u_ref, dy_ref, bre_ref, bim_ref, cre_ref, cim_ref, lam_ref, d_ref,
             du_ref, dbre_ref, dbim_ref, dcre_ref, dcim_ref, dlam_ref, dd_ref, sr_s, si_s, gr_s, gi_s):
        _lb_store(sr_s, slice(0, SOFF), jnp.zeros((SOFF, QS), F32))
        _lb_store(si_s, slice(0, SOFF), jnp.zeros((SOFF, QS), F32))
        for k in range(NCH):
            sl = slice(k * tc, (k + 1) * tc)
            ss = slice(SOFF + k * tc, SOFF + (k + 1) * tc)
            uk = u_ref[sl, :]
            dyk = dy_ref[sl, :].astype(BF16)
            _lb_store(sr_s, ss, _dot(uk, bre_ref[...]))
            _lb_store(si_s, ss, _dot(uk, bim_ref[...]))
            _lb_store(gr_s, sl, _dot_nt(dyk, cre_ref[...]))
            _lb_store(gi_s, sl, -_dot_nt(dyk, cim_ref[...]))
        ar, ai, atr, ati = lam_ref[0:1, :], lam_ref[1:2, :], lam_ref[2:3, :], lam_ref[3:4, :]
        _scan(sr_s, si_s, SOFF, T, ar, ai, atr, ati, False)
        _scan(gr_s, gi_s, 0, T, ar, -ai, atr, -ati, True)
        dbre = jnp.zeros((QU, QS), F32)
        dbim = jnp.zeros((QU, QS), F32)
        dcre = jnp.zeros((QS, QU), F32)
        dcim = jnp.zeros((QS, QU), F32)
        dd = jnp.zeros((1, QU), F32)
        qr = jnp.zeros((1, QS), F32)
        qi = jnp.zeros((1, QS), F32)
        for k in range(NCH):
            sl = slice(k * tc, (k + 1) * tc)
            ss = slice(SOFF + k * tc, SOFF + (k + 1) * tc)
            sp = slice(SOFF - 1 + k * tc, SOFF - 1 + (k + 1) * tc)
            uk = u_ref[sl, :]
            dyk = dy_ref[sl, :]
            dyb = dyk.astype(BF16)
            gr, gi = _lb_load(gr_s, sl), _lb_load(gi_s, sl)
            pr, pi = _lb_load(sr_s, sp), _lb_load(si_s, sp)
            qr = qr + jnp.sum(gr * pr + gi * pi, axis=0, keepdims=True)
            qi = qi + jnp.sum(gi * pr - gr * pi, axis=0, keepdims=True)
            grb, gib = gr.astype(BF16), gi.astype(BF16)
            du_ref[sl, :] = (_dot_nt(grb, bre_ref[...]) + _dot_nt(gib, bim_ref[...])
                             + dyk * d_ref[...]).astype(BF16)
            dbre = dbre + _dot_tn(uk, grb)
            dbim = dbim + _dot_tn(uk, gib)
            dcre = dcre + _dot_tn(_lb_load(sr_s, ss).astype(BF16), dyb)
            dcim = dcim - _dot_tn(_lb_load(si_s, ss).astype(BF16), dyb)
            dd = dd + jnp.sum(dyk * uk.astype(F32), axis=0, keepdims=True)
        dlam_ref[0] = jnp.concatenate([qr, qi, jnp.zeros((6, QS), F32)], axis=0)
        dbre_ref[0] = dbre
        dbim_ref[0] = dbim
        dcre_ref[0] = dcre
        dcim_ref[0] = dcim
        dd_ref[...] = dd

    col = pl.BlockSpec((L, QU), lambda q: (0, q))
    bsp = pl.BlockSpec((QU, QS), lambda q: (q, q))
    csp = pl.BlockSpec((QS, QU), lambda q: (q, q))
    return _call(
        body, comm, name=name, grid=(NQ,),
        out_shape=[jax.ShapeDtypeStruct((L, DS), BF16),
                   jax.ShapeDtypeStruct((NQ, QU, QS), F32), jax.ShapeDtypeStruct((NQ, QU, QS), F32),
                   jax.ShapeDtypeStruct((NQ, QS, QU), F32), jax.ShapeDtypeStruct((NQ, QS, QU), F32),
                   jax.ShapeDtypeStruct((NQ, 8, QS), F32), jax.ShapeDtypeStruct((1, DS), F32)],
        in_specs=[col, col, bsp, bsp, csp, csp,
                  pl.BlockSpec((8, QS), lambda q: (0, q)), pl.BlockSpec((1, QU), lambda q: (0, q))],
        out_specs=[col,
                   pl.BlockSpec((1, QU, QS), lambda q: (q, 0, 0)), pl.BlockSpec((1, QU, QS), lambda q: (q, 0, 0)),
                   pl.BlockSpec((1, QS, QU), lambda q: (q, 0, 0)), pl.BlockSpec((1, QS, QU), lambda q: (q, 0, 0)),
                   pl.BlockSpec((1, 8, QS), lambda q: (q, 0, 0)), pl.BlockSpec((1, QU), lambda q: (0, q))],
        scratch_shapes=[pltpu.VMEM((NLB, L + SOFF, 128), F32), pltpu.VMEM((NLB, L + SOFF, 128), F32),
                        pltpu.VMEM((NLB, L, 128), F32), pltpu.VMEM((NLB, L, 128), F32)],
        params=_params("parallel"),
    )(uf, dyss, bre, bim, cre, cim, lamp, dsk)


def _branches(z1_ref, yss_ref, gt_ref, lng_ref, lnb_ref, wp_ref, wv_ref, wg_ref):
    zf = z1_ref[...]
    mu = jnp.mean(zf, axis=-1, keepdims=True)
    zc = zf - mu
    rstd = lax.rsqrt(jnp.mean(zc * zc, axis=-1, keepdims=True) + EPS)
    zn = zc * rstd
    z2 = zn * lng_ref[...] + lnb_ref[...]
    sz = _sigmoid(z2)
    z3 = (z2 * sz).astype(BF16)
    y_conv = _dot(z3, wp_ref[...])
    yss = yss_ref[...]
    yg = _gelu(yss).astype(BF16)
    sv = _dot(yg, wv_ref[...])
    sig = _sigmoid(_dot(yg, wg_ref[...]))
    y_ssm = sv * sig
    gc = gt_ref[:, 0:D].astype(F32)
    gs = gt_ref[:, D:2 * D].astype(F32)
    m = gc * y_conv + gs * y_ssm
    return dict(rstd=rstd, zn=zn, z2=z2, sz=sz, z3=z3, y_conv=y_conv, yss=yss, yg=yg, sv=sv, sig=sig,
                y_ssm=y_ssm, gc=gc, gs=gs, m=m)


def _merge_fwd(h, z1, yss, gate, lng, lnb, wp, wv, wg, wo, name):
    L = h.shape[0]
    tm = _tile(L, 528)

    def body(h_ref, z1_ref, yss_ref, gt_ref, lng_ref, lnb_ref, wp_ref, wv_ref, wg_ref, wo_ref, o_ref):
        f = _branches(z1_ref, yss_ref, gt_ref, lng_ref, lnb_ref, wp_ref, wv_ref, wg_ref)
        o_ref[...] = h_ref[...] + _dot(f["m"].astype(BF16), wo_ref[...])

    def row(n):
        return pl.BlockSpec((tm, n), lambda i: (i, 0))

    return pl.pallas_call(
        body, name=name, grid=(L // tm,),
        out_shape=jax.ShapeDtypeStruct((L, D), F32),
        in_specs=[row(D), row(DC), row(DS), row(2 * D), _res((1, DC)), _res((1, DC)),
                  _res((DC, D)), _res((DS, D)), _res((DS, D)), _res((D, D))],
        out_specs=row(D),
        compiler_params=_params("parallel"),
    )(h, z1, yss, gate, lng, lnb, wp, wv, wg, wo)


def _merge_bwd(dh, z1, yss, gate, lng, lnb, wp, wv, wg, wo, name):
    L = dh.shape[0]
    tm = _tile(L, 352)

    def body(dh_ref, z1_ref, yss_ref, gt_ref, lng_ref, lnb_ref, wp_ref, wv_ref, wg_ref, wo_ref,
             m_ref, dgt_ref, dyc_ref, z3_ref, dz1_ref, yg_ref, dsv_ref, dsg_ref, dyss_ref,
             dbg_ref, dlng_ref, dlnb_ref):
        i = pl.program_id(0)
        f = _branches(z1_ref, yss_ref, gt_ref, lng_ref, lnb_ref, wp_ref, wv_ref, wg_ref)
        gc, gs, sig, sv = f["gc"], f["gs"], f["sig"], f["sv"]
        m_ref[...] = f["m"].astype(BF16)
        z3_ref[...] = f["z3"]
        yg_ref[...] = f["yg"]
        dm = _dot_nt(dh_ref[...].astype(BF16), wo_ref[...])
        dgc = (dm * f["y_conv"] * gc * (1.0 - gc)).astype(BF16)
        dgs = (dm * f["y_ssm"] * gs * (1.0 - gs)).astype(BF16)
        dgt_ref[:, 0:D] = dgc
        dgt_ref[:, D:2 * D] = dgs
        part = jnp.concatenate([jnp.sum(dgc.astype(F32), axis=0, keepdims=True),
                                jnp.sum(dgs.astype(F32), axis=0, keepdims=True)], axis=1)
        _acc_rows(dbg_ref, part, i == 0)
        dyc = (dm * gc).astype(BF16)
        dyc_ref[...] = dyc
        dys = dm * gs
        dsv = (dys * sig).astype(BF16)
        dsg = (dys * sv * sig * (1.0 - sig)).astype(BF16)
        dsv_ref[...] = dsv
        dsg_ref[...] = dsg
        dyg = _dot_nt(dsv, wv_ref[...]) + _dot_nt(dsg, wg_ref[...])
        dyss_ref[...] = dyg * _gelu_grad(f["yss"])
        dz3 = _dot_nt(dyc, wp_ref[...])
        z2, sz, zn = f["z2"], f["sz"], f["zn"]
        dz2 = dz3 * sz * (1.0 + z2 * (1.0 - sz))
        _acc_rows(dlng_ref, jnp.sum(dz2 * zn, axis=0, keepdims=True), i == 0)
        _acc_rows(dlnb_ref, jnp.sum(dz2, axis=0, keepdims=True), i == 0)
        dzn = dz2 * lng_ref[...]
        dz1_ref[...] = f["rstd"] * (dzn - jnp.mean(dzn, axis=-1, keepdims=True)
                                    - zn * jnp.mean(dzn * zn, axis=-1, keepdims=True))

    def row(n):
        return pl.BlockSpec((tm, n), lambda i: (i, 0))

    def tot(n):
        return pl.BlockSpec((1, n), lambda i: (0, 0))

    return pl.pallas_call(
        body, name=name, grid=(L // tm,),
        out_shape=[jax.ShapeDtypeStruct((L, D), BF16), jax.ShapeDtypeStruct((L, 2 * D), BF16),
                   jax.ShapeDtypeStruct((L, D), BF16), jax.ShapeDtypeStruct((L, DC), BF16),
                   jax.ShapeDtypeStruct((L, DC), F32), jax.ShapeDtypeStruct((L, DS), BF16),
                   jax.ShapeDtypeStruct((L, D), BF16), jax.ShapeDtypeStruct((L, D), BF16),
                   jax.ShapeDtypeStruct((L, DS), F32),
                   jax.ShapeDtypeStruct((1, 2 * D), F32), jax.ShapeDtypeStruct((1, DC), F32),
                   jax.ShapeDtypeStruct((1, DC), F32)],
        in_specs=[row(D), row(DC), row(DS), row(2 * D), _res((1, DC)), _res((1, DC)),
                  _res((DC, D)), _res((DS, D)), _res((DS, D)), _res((D, D))],
        out_specs=[row(D), row(2 * D), row(D), row(DC), row(DC), row(DS), row(D), row(D), row(DS),
                   tot(2 * D), tot(DC), tot(DC)],
        compiler_params=_params("arbitrary"),
    )(dh, z1, yss, gate, lng, lnb, wp, wv, wg, wo)


def _final(h, g, tgt, name):
    L = h.shape[0]
    tm = _tile(L, 528)

    def body(h_ref, g_ref, t_ref, dh_ref, loss_ref, dg_ref):
        i = pl.program_id(0)
        hv = h_ref[...]
        gv = g_ref[...]
        y, r = _rms(hv, gv)
        row = i * tm + lax.broadcasted_iota(jnp.int32, (tm, 1), 0)
        e = jnp.where(row >= FRONT, y - t_ref[...], 0.0)
        dy = e * (1.0 / D)
        part = 0.5 * jnp.sum(jnp.sum(e * dy, axis=1, keepdims=True), axis=0, keepdims=True)
        dx, xh = _rms_bwd(dy, hv, r, gv)
        dh_ref[...] = dx
        _acc_rows(loss_ref, part, i == 0)
        _acc_rows(dg_ref, jnp.sum(dy * xh, axis=0, keepdims=True), i == 0)

    row = pl.BlockSpec((tm, D), lambda i: (i, 0))
    return pl.pallas_call(
        body, name=name, grid=(L // tm,),
        out_shape=[jax.ShapeDtypeStruct((L, D), F32), jax.ShapeDtypeStruct((1, 1), F32),
                   jax.ShapeDtypeStruct((1, D), F32)],
        in_specs=[row, _res((1, D)), row],
        out_specs=[row, pl.BlockSpec((1, 1), lambda i: (0, 0)), pl.BlockSpec((1, D), lambda i: (0, 0))],
        compiler_params=_params("arbitrary"),
    )(h, g, tgt)


def _ssm_disc(lam_re, lam_im, log_dt, b_re, b_im):
    lam = lax.complex(lam_re, lam_im)
    dt = jnp.exp(log_dt)[:, None]
    lam_bar = jnp.exp(lam * dt)
    bbar = ((lam_bar - 1.0) / lam)[..., None] * lax.complex(b_re, b_im)
    return jnp.real(lam_bar), jnp.imag(lam_bar), jnp.real(bbar), jnp.imag(bbar)


def _bdiag_in(m):
    return jnp.einsum("gph,gk->ghkp", m, jnp.eye(G, dtype=m.dtype)).reshape(G * H, G * P)


def _bdiag_out(m):
    return jnp.einsum("ghp,gk->gpkh", m, jnp.eye(G, dtype=m.dtype)).reshape(G * P, G * H)


def _diag_blocks(m4):
    return jnp.einsum("qiaib->qiab", m4).reshape(G, m4.shape[2], m4.shape[4])


def _pack(parts, rows_mult=8):
    flat = jnp.concatenate([p.reshape(-1).astype(F32) for p in parts])
    n = flat.shape[0]
    tot = -(-n // (128 * rows_mult)) * (128 * rows_mult)
    return jnp.pad(flat, (0, tot - n)).reshape(tot // 128, 128)


def _unpack(buf, shapes):
    flat = buf.reshape(-1)
    out, o = [], 0
    for s in shapes:
        n = math.prod(s)
        out.append(flat[o:o + n].reshape(s))
        o += n
    return out


def kernel(x, meta_tokens, ffn1_norm, ffn1_w1, ffn1_w3, ffn1_w2, mix_norm, w_in, b_gate, conv_dw, conv_dw_b, conv_ln_g, conv_ln_b, conv_proj, ssm_lam_re, ssm_lam_im, ssm_log_dt, ssm_b_re, ssm_b_im, ssm_c_re, ssm_c_im, ssm_d, ssm_w_v, ssm_w_g, w_out, ffn2_norm, ffn2_w1, ffn2_w3, ffn2_w2, final_norm, loss_target, m_meta_tokens, m_ffn1_norm, m_ffn1_w1, m_ffn1_w3, m_ffn1_w2, m_mix_norm, m_w_in, m_b_gate, m_conv_dw, m_conv_dw_b, m_conv_ln_g, m_conv_ln_b, m_conv_proj, m_ssm_lam_re, m_ssm_lam_im, m_ssm_log_dt, m_ssm_b_re, m_ssm_b_im, m_ssm_c_re, m_ssm_c_im, m_ssm_d, m_ssm_w_v, m_ssm_w_g, m_w_out, m_ffn2_norm, m_ffn2_w1, m_ffn2_w3, m_ffn2_w2, m_final_norm, v_meta_tokens, v_ffn1_norm, v_ffn1_w1, v_ffn1_w3, v_ffn1_w2, v_mix_norm, v_w_in, v_b_gate, v_conv_dw, v_conv_dw_b, v_conv_ln_g, v_conv_ln_b, v_conv_proj, v_ssm_lam_re, v_ssm_lam_im, v_ssm_log_dt, v_ssm_b_re, v_ssm_b_im, v_ssm_c_re, v_ssm_c_im, v_ssm_d, v_ssm_w_v, v_ssm_w_g, v_w_out, v_ffn2_norm, v_ffn2_w1, v_ffn2_w3, v_ffn2_w2, v_final_norm):
    args = dict(locals())
    names = ["meta_tokens", "ffn1_norm", "ffn1_w1", "ffn1_w3", "ffn1_w2", "mix_norm", "w_in", "b_gate",
             "conv_dw", "conv_dw_b", "conv_ln_g", "conv_ln_b", "conv_proj", "ssm_lam_re", "ssm_lam_im",
             "ssm_log_dt", "ssm_b_re", "ssm_b_im", "ssm_c_re", "ssm_c_im", "ssm_d", "ssm_w_v", "ssm_w_g",
             "w_out", "ffn2_norm", "ffn2_w1", "ffn2_w3", "ffn2_w2", "final_norm"]
    big = ["ffn1_w1", "ffn1_w3", "ffn1_w2", "w_in", "conv_proj", "ssm_w_v", "ssm_w_g", "w_out",
           "ffn2_w1", "ffn2_w3", "ffn2_w2"]
    small = [n for n in names if n not in big]

    xs = x[0]
    S = xs.shape[0]
    L = FRONT + S
    T = L // NSEG
    jx, jy = lax.axis_index("x"), lax.axis_index("y")
    chip = 2 * jx + jy

    sm = _gather_all(_pack([meta_tokens, conv_dw[0]]), "gather_small")[0::2].reshape(NSH, -1)
    nmt = NMETA * (D // NSH)
    ndw = KW * (DC // NSH)
    meta_full = sm[:, :nmt].reshape(NSH, NMETA, D // NSH).transpose(1, 0, 2).reshape(NMETA, D)
    dw_full = sm[:, nmt:nmt + ndw].reshape(NSH, KW, DC // NSH).transpose(1, 0, 2).reshape(KW, DC)
    dw_pad = jnp.pad(dw_full, ((0, KWP - KW), (0, 0)))
    tposed = ("ffn1_w1", "ffn1_w3", "ffn2_w1", "ffn2_w3")

    def view(a, n):
        return jnp.swapaxes(a, 1, 2) if n in tposed else a

    grp_a = ["ffn1_w1", "ffn1_w3", "ffn1_w2"]
    grp_b = ["w_in", "conv_proj", "ssm_w_v", "ssm_w_g", "w_out"]
    grp_c = ["ffn2_w1", "ffn2_w3", "ffn2_w2"]

    def shard(n):
        return view(args[n], n)[0].astype(BF16)

    sh_a = [shard(n) for n in grp_a]
    ga_send, ga_recv, sh_a, land_a, ga_token = _chips_start(
        sh_a, [jax.ShapeDtypeStruct((NSH,) + s.shape, s.dtype) for s in sh_a], True, "gather_ffn1_start")

    def cols(w):
        return w.transpose(1, 0, 2).reshape(w.shape[1], -1)

    disc_in = (ssm_lam_re[0], ssm_lam_im[0], ssm_log_dt[0], ssm_b_re[0], ssm_b_im[0])
    (lbr, lbi, bbr, bbi), disc_vjp = jax.vjp(_ssm_disc, *disc_in)
    lam_t = jnp.exp(lax.complex(ssm_lam_re[0], ssm_lam_im[0]) * (jnp.exp(ssm_log_dt[0])[:, None] * T))
    lamp = jnp.concatenate([lbr.reshape(1, NST), lbi.reshape(1, NST), jnp.real(lam_t).reshape(1, NST),
                            jnp.imag(lam_t).reshape(1, NST), jnp.zeros((4, NST), F32)], axis=0)
    bre_bd, bim_bd = _bdiag_in(bbr).astype(BF16), _bdiag_in(bbi).astype(BF16)
    cre_bd, cim_bd = _bdiag_out(ssm_c_re[0]).astype(BF16), _bdiag_out(ssm_c_im[0]).astype(BF16)

    h0 = jnp.concatenate([jnp.zeros((FRONT - NMETA, D), F32), meta_full, xs + ga_token[0, 0]], axis=0)
    sh_a, land_a = _chips_wait(ga_send, ga_recv, sh_a, land_a, [h0], True, "gather_ffn1_wait")
    gw = dict(zip(grp_a, _pass_halves(land_a, "pass_ffn1", sh_a)))
    tgt = jnp.pad(loss_target[0], ((FRONT, 0), (0, 0)))
    (h1, a1, b1), got = _ffn_fwd(h0, ffn1_norm, gw["ffn1_w1"], gw["ffn1_w3"], gw["ffn1_w2"], "ffn1_fwd",
                                 _gather_half_behind([shard(n) for n in grp_b]))
    gw.update(zip(grp_b, _pass_halves(got, "pass_mix")))
    w_in_f = gw["w_in"]
    wp_f, wv_f, wg_f = cols(gw["conv_proj"]), cols(gw["ssm_w_v"]), cols(gw["ssm_w_g"])
    wo_f = gw["w_out"].reshape(D, D)
    (vg, uf, gate), got1 = _mix_in_fwd(h1, mix_norm, w_in_f, b_gate, "mix_in_fwd",
                                       _gather_half_behind([shard("ffn2_w1")]))
    (z1,), got3 = _conv_fwd(vg, dw_pad, conv_dw_b, "conv_fwd", _gather_half_behind([shard("ffn2_w3")]))
    (yss,), got2 = _ssm_fwd(uf, bre_bd, bim_bd, cre_bd, cim_bd, lamp, ssm_d, "ssm_fwd",
                            _gather_half_behind([shard("ffn2_w2")]))
    gw.update(zip(grp_c, _pass_halves([got1[0], got3[0], got2[0]], "pass_ffn2")))
    h2 = _merge_fwd(h1, z1, yss, gate, conv_ln_g, conv_ln_b, wp_f, wv_f, wg_f, wo_f, "merge_fwd")
    (h3, a2, b2), _ = _ffn_fwd(h2, ffn2_norm, gw["ffn2_w1"], gw["ffn2_w3"], gw["ffn2_w2"], "ffn2_fwd")

    gbig = {}
    core = lax.axis_index("c").astype(jnp.int32).reshape(1)

    def pair_sums(group, tag):
        gl = [gbig[n] for n in group]
        sib = _pair_exchange(gl, "pair_exchange_" + tag)
        out = []
        for n, g_, s_ in zip(group, gl, sib):
            out.append(_add_pair(g_, s_, core, "pair_" + n))
        return out

    dh3, loss_part, d_final = _final(h3, final_norm.reshape(1, D), tgt, "final")
    (dh2, da2, db2, s2, n2, d_ffn2_norm), _ = _ffn_bwd(
        h2, ffn2_norm, dh3, a2, b2, gw["ffn2_w1"], gw["ffn2_w3"], gw["ffn2_w2"], "ffn2_bwd")
    gbig["ffn2_w1"] = _wgrad(da2, n2, "ffn2_dw1")
    gbig["ffn2_w3"] = _wgrad(db2, n2, "ffn2_dw3")
    gbig["ffn2_w2"] = _wgrad(s2, dh3, "ffn2_dw2", 0.5)
    pair_c = pair_sums(grp_c, "ffn2")
    (m_b, dgate, dyc, z3, dz1, yg, dsv, dsg, dyss, d_b_gate, d_ln_g, d_ln_b) = _merge_bwd(
        dh2, z1, yss, gate, conv_ln_g, conv_ln_b, wp_f, wv_f, wg_f, wo_f, "merge_bwd")
    gbig["w_out"] = _wgrad(m_b, dh2, "dw_out").reshape(NSH, D // NSH, D)

    def shard_cols(gm):
        return gm.reshape(gm.shape[0], NSH, -1).transpose(1, 0, 2)

    gbig["conv_proj"] = shard_cols(_wgrad(z3, dyc, "dw_proj"))
    gbig["ssm_w_v"] = shard_cols(_wgrad(yg, dsv, "dw_v"))
    gbig["ssm_w_g"] = shard_cols(_wgrad(yg, dsg, "dw_g"))
    dv, dgl, ddw, d_dw_b = _conv_bwd(dz1, vg, dw_pad, "conv_bwd")
    (duf, dbre, dbim, dcre, dcim, dlam, d_ssm_d), recv_c = _ssm_bwd(
        uf, dyss, bre_bd, bim_bd, cre_bd, cim_bd, lamp, ssm_d, "ssm_bwd", _scatter_chips_behind(pair_c))
    dh1, u_b, dproj, d_mix_norm = _mix_in_bwd(h1, mix_norm, dh2, dv, dgl, duf, dgate, w_in_f, "mix_in_bwd")
    gbig["w_in"] = _wgrad(u_b, dproj, "dw_in")
    pair_b = pair_sums(grp_b, "mix")

    d_bbr = _diag_blocks(dbre.reshape(NQ, 8, H, 8, P)).transpose(0, 2, 1)
    d_bbi = _diag_blocks(dbim.reshape(NQ, 8, H, 8, P)).transpose(0, 2, 1)
    d_c_re = _diag_blocks(dcre.reshape(NQ, 8, P, 8, H)).transpose(0, 2, 1)
    d_c_im = _diag_blocks(dcim.reshape(NQ, 8, P, 8, H)).transpose(0, 2, 1)
    d_lbr = dlam[:, 0, :].reshape(G, P)
    d_lbi = dlam[:, 1, :].reshape(G, P)
    d_lam_re, d_lam_im, d_log_dt, d_b_re, d_b_im = disc_vjp((d_lbr, d_lbi, d_bbr, d_bbi))

    sg = {"mix_norm": d_mix_norm, "b_gate": d_b_gate, "conv_dw": ddw[:KW], "conv_dw_b": d_dw_b,
          "conv_ln_g": d_ln_g, "conv_ln_b": d_ln_b, "ssm_lam_re": d_lam_re, "ssm_lam_im": d_lam_im,
          "ssm_log_dt": d_log_dt, "ssm_b_re": d_b_re, "ssm_b_im": d_b_im, "ssm_c_re": d_c_re, "ssm_c_im": d_c_im,
          "ssm_d": d_ssm_d, "ffn2_norm": d_ffn2_norm, "final_norm": d_final}
    late = ["meta_tokens", "ffn1_norm"]
    early = [n for n in small if n not in late]

    (dh0, da1, db1, s1, n1, d_ffn1_norm), got = _ffn_bwd(
        h0, ffn1_norm, dh1, a1, b1, gw["ffn1_w1"], gw["ffn1_w3"], gw["ffn1_w2"], "ffn1_bwd",
        _join(_scatter_chips_behind(pair_b), _gather_all_behind(_pack([sg[n] for n in early]))))
    recv_b, early_all = got[:len(grp_b)], got[len(grp_b)]
    gbig["ffn1_w1"] = _wgrad(da1, n1, "ffn1_dw1")
    gbig["ffn1_w3"] = _wgrad(db1, n1, "ffn1_dw3")
    gbig["ffn1_w2"] = _wgrad(s1, dh1, "ffn1_dw2", 0.5)
    grad_x = dh0[FRONT:][None]
    sg["meta_tokens"] = dh0[FRONT - NMETA:FRONT]
    sg["ffn1_norm"] = d_ffn1_norm

    pair_a = pair_sums(grp_a, "ffn1")
    late_all = _gather_all(_pack([sg[n] for n in late]), "gather_late_grads")
    sa_send, sa_recv, pair_a, land_s, sa_token = _chips_start(
        pair_a, [jax.ShapeDtypeStruct(p.shape, p.dtype) for p in pair_a], False, "scatter_ffn1_start", [late_all])

    out_g, out_d, out_m, out_v = {}, {}, {}, {}

    def finish(group, recvs, tag, after=None):
        halves = [_sum_slots(r, "sum_" + n, after) for n, r in zip(group, recvs)]
        for n, f in zip(group, _swap_halves(halves, "swap_" + tag)):
            g3 = f.reshape(1, f.shape[0] * f.shape[1], f.shape[2])
            d3, m3, v3 = _adamw(view(args[n], n), g3, view(args["m_" + n], n), view(args["v_" + n], n),
                                "adamw_" + n)
            out_g[n], out_d[n], out_m[n], out_v[n] = (view(t, n) for t in (g3, d3, m3, v3))
            done.append(d3)

    done = []
    finish(grp_b + grp_c, list(recv_b) + list(recv_c), "mix_ffn2", sa_token)

    sgr = dict(zip(early, _unpack(_sum_slots(early_all, "sum_early", sa_token), [sg[n].shape for n in early])))
    sgr.update(zip(late, _unpack(_sum_slots(late_all, "sum_late"), [sg[n].shape for n in late])))
    sgr["meta_tokens"] = lax.dynamic_slice_in_dim(sgr["meta_tokens"], chip * (D // NSH), D // NSH, axis=1)
    sgr["conv_dw"] = lax.dynamic_slice_in_dim(sgr["conv_dw"], chip * (DC // NSH), DC // NSH, axis=1)
    pshapes = [args[n].shape for n in small]
    d_s, m_s, v_s = _adamw(_pack([args[n] for n in small])[None], _pack([sgr[n] for n in small])[None],
                           _pack([args["m_" + n] for n in small])[None],
                           _pack([args["v_" + n] for n in small])[None], "adamw_small")
    for n, g_, d_, m_, v_ in zip(small, [sgr[n] for n in small], _unpack(d_s[0], pshapes),
                                 _unpack(m_s[0], pshapes), _unpack(v_s[0], pshapes)):
        out_g[n], out_d[n], out_m[n], out_v[n] = g_.reshape(args[n].shape), d_, m_, v_

    pair_a, recv_a = _chips_wait(sa_send, sa_recv, pair_a, land_s, [d_s] + done, False, "scatter_ffn1_wait")
    finish(grp_a, _fill_own(pair_a, recv_a, "own_ffn1"), "ffn1")

    loss = lax.psum(loss_part[0, 0], ("x", "y", "c"))
    return (loss, grad_x, *[out_g[n] for n in names], *[out_d[n] for n in names],
            *[out_m[n] for n in names], *[out_v[n] for n in names])
```

```python
import math

import jax
import jax.numpy as jnp
from jax import lax
from jax.experimental import pallas as pl
from jax.experimental.pallas import tpu as pltpu

F32 = jnp.float32
BF16 = jnp.bfloat16

D = 1024
NSH = 4
F = 2816
FS = F // NSH
DC = 512
DS = 512
DIN = 2 * DC + DS + 2 * D
WS = DIN // NSH
KW = 31
KWP = 32
CONV_ROWS = 64
NMETA = 16
FRONT = 128
G, P, H = 32, 64, 16
NST = G * P
NQ = 4
QS = NST // NQ
QU = DS // NQ
NSEG = 32
NGRP = NSEG // 8
NCH = 8
SOFF = 8
EPS = 1e-6
LR, B1, B2, AEPS, WD, STEP = 1e-3, 0.9, 0.999, 1e-8, 0.01, 10
VMEM_LIMIT = 58 * 1024 * 1024
MESH = pl.DeviceIdType.MESH
ANY = pl.BlockSpec(memory_space=pl.ANY)


def _params(*sem):
    return pltpu.CompilerParams(dimension_semantics=sem, vmem_limit_bytes=VMEM_LIMIT)


def _res(shape):
    nd = len(shape)
    return pl.BlockSpec(shape, lambda *_: (0,) * nd, pipeline_mode=pl.Buffered(1))


def _tile(n, cap, mult=16):
    best = None
    for t in range(mult, min(n, cap) + 1, mult):
        if n % t == 0:
            best = t
    assert best is not None, (n, cap, mult)
    return best


def _dot(a, b):
    return jnp.dot(a, b, preferred_element_type=F32)


def _dot_nt(a, b):
    return lax.dot_general(a, b, (((1,), (1,)), ((), ())), preferred_element_type=F32)


def _dot_tn(a, b):
    return lax.dot_general(a, b, (((0,), (0,)), ((), ())), preferred_element_type=F32)


def _sigmoid(x):
    return 1.0 / (1.0 + jnp.exp(-x))


_GC = math.sqrt(2.0 / math.pi)
_GA = 0.044715


def _gelu(x):
    return 0.5 * x * (1.0 + jnp.tanh(_GC * (x + _GA * x * x * x)))


def _gelu_grad(x):
    t = jnp.tanh(_GC * (x + _GA * x * x * x))
    return 0.5 * (1.0 + t) + 0.5 * x * (1.0 - t * t) * _GC * (1.0 + 3.0 * _GA * x * x)


def _rms(hv, g):
    r = lax.rsqrt(jnp.mean(hv * hv, axis=-1, keepdims=True) + EPS)
    return hv * r * g, r


def _rms_bwd(dn, hv, r, g):
    xh = hv * r
    dxh = dn * g
    return r * (dxh - xh * jnp.mean(dxh * xh, axis=-1, keepdims=True)), xh


def _acc_rows(ref, part, first):
    @pl.when(first)
    def _():
        ref[...] = part

    @pl.when(jnp.logical_not(first))
    def _():
        ref[...] += part


def _coords():
    return lax.axis_index("x"), lax.axis_index("y"), lax.axis_index("c")


def _flip(v, d):
    return 1 - v if d else v


def _run(local, remote):
    for cp in local + remote:
        cp.start()
    for cp in remote:
        cp.wait()
    for cp in local:
        cp.wait()


def _via_vmem(src, dst, stage, sems, i):
    return (pltpu.make_async_copy(src, stage, sems.at[2 * i]), pltpu.make_async_copy(stage, dst, sems.at[2 * i + 1]))


def _run_staged(staged, remote):
    for load, _ in staged:
        load.start()
    for cp in remote:
        cp.start()
    for load, store in staged:
        load.wait()
        store.start()
    for cp in remote:
        cp.wait()
    for _, store in staged:
        store.wait()


_REL3 = ((1, 0), (0, 1), (1, 1))


class _Behind:
    def __init__(self, arrays, out_shapes, scratch, build):
        self.arrays, self.out_shapes, self.scratch, self.build = list(arrays), list(out_shapes), list(scratch), build

    def start(self, ins, outs, scr):
        staged, remote = self.build(ins, outs, scr)
        for load, _ in staged:
            load.start()
        for cp in remote:
            cp.start()

    def finish(self, ins, outs, scr):
        staged, remote = self.build(ins, outs, scr)
        for load, store in staged:
            load.wait()
            store.start()
        for cp in remote:
            cp.wait()
        for _, store in staged:
            store.wait()


def _call(body, comm, *, name, grid, in_specs, out_specs, out_shape, scratch_shapes=(), params):
    in_specs, out_specs, out_shape = list(in_specs), list(out_specs), list(out_shape)
    scratch_shapes = list(scratch_shapes)
    if comm is None:
        f = pl.pallas_call(body, name=name, grid=grid, in_specs=in_specs, out_specs=out_specs,
                           out_shape=out_shape, scratch_shapes=scratch_shapes, compiler_params=params)
        return lambda *args: (f(*args), [])
    ni, no, ns = len(in_specs), len(out_specs), len(scratch_shapes)
    ci, co = len(comm.arrays), len(comm.out_shapes)

    def hosted(*refs):
        ins, cin = refs[:ni], refs[ni:ni + ci]
        outs, cout = refs[ni + ci:ni + ci + no], refs[ni + ci + no:ni + ci + no + co]
        scr, cscr = refs[ni + ci + no + co:ni + ci + no + co + ns], refs[ni + ci + no + co + ns:]
        first = last = None
        for axis, size in enumerate(grid):
            i = pl.program_id(axis)
            first = (i == 0) if first is None else jnp.logical_and(first, i == 0)
            last = (i == size - 1) if last is None else jnp.logical_and(last, i == size - 1)

        @pl.when(first)
        def _():
            comm.start(cin, cout, cscr)

        body(*ins, *outs, *scr)

        @pl.when(last)
        def _():
            comm.finish(cin, cout, cscr)

    f = pl.pallas_call(hosted, name=name, grid=grid, in_specs=in_specs + [ANY] * ci,
                       out_specs=out_specs + [ANY] * co, out_shape=out_shape + comm.out_shapes,
                       scratch_shapes=scratch_shapes + comm.scratch,
                       compiler_params=_params(*(("arbitrary",) * len(grid))))

    def run(*args):
        res = f(*args, *comm.arrays)
        return res[:no], res[no:]

    return run


def _gather_half_behind(shards):
    n = len(shards)

    def build(ins, outs, scr):
        send, recv, loc = scr[:3]
        stage = scr[3:]
        x, y, c = _coords()
        me = 2 * x + y
        staged = [_via_vmem(ins[t], outs[t].at[me], stage[t], loc, t) for t in range(n)]
        remote = []
        for t in range(n):
            half = shards[t].shape[0] // 2
            mine = pl.ds(c * half, half)
            for k, (dx, dy) in enumerate(_REL3):
                remote.append(pltpu.make_async_remote_copy(
                    src_ref=ins[t].at[mine], dst_ref=outs[t].at[me, mine],
                    send_sem=send.at[3 * t + k], recv_sem=recv.at[3 * t + k],
                    device_id=(_flip(x, dx), _flip(y, dy), c), device_id_type=MESH))
        return staged, remote

    return _Behind(shards, [jax.ShapeDtypeStruct((NSH,) + s.shape, s.dtype) for s in shards],
                   [pltpu.SemaphoreType.DMA((3 * n,)), pltpu.SemaphoreType.DMA((3 * n,)),
                    pltpu.SemaphoreType.DMA((2 * n,))] + [pltpu.VMEM(s.shape, s.dtype) for s in shards], build)


def _pass_halves(gathered, name, own=()):
    n, m = len(gathered), len(own)

    def body(*refs):
        shards, outs = refs[n:n + m], refs[n + m:2 * n + m]
        send, recv, loc = refs[2 * n + m:2 * n + m + 3]
        stage = refs[2 * n + m + 3:]
        x, y, c = _coords()
        staged = [_via_vmem(shards[t], outs[t].at[2 * x + y], stage[t], loc, t) for t in range(m)]
        remote = []
        for t in range(n):
            half = gathered[t].shape[1] // 2
            mine = pl.ds(c * half, half)
            for k, (dx, dy) in enumerate(_REL3):
                slot = 2 * _flip(x, dx) + _flip(y, dy)
                remote.append(pltpu.make_async_remote_copy(
                    src_ref=outs[t].at[slot, mine], dst_ref=outs[t].at[slot, mine],
                    send_sem=send.at[3 * t + k], recv_sem=recv.at[3 * t + k],
                    device_id=(x, y, 1 - c), device_id_type=MESH))
        _run_staged(staged, remote)

    return pl.pallas_call(
        body, name=name,
        out_shape=[jax.ShapeDtypeStruct(g.shape, g.dtype) for g in gathered],
        in_specs=[ANY] * (n + m), out_specs=[ANY] * n, input_output_aliases={t: t for t in range(n)},
        scratch_shapes=[pltpu.SemaphoreType.DMA((3 * n,)), pltpu.SemaphoreType.DMA((3 * n,)),
                        pltpu.SemaphoreType.DMA((max(2 * m, 1),))] + [pltpu.VMEM(s.shape, s.dtype) for s in own],
        compiler_params=pltpu.CompilerParams(vmem_limit_bytes=VMEM_LIMIT),
    )(*gathered, *own)


def _fill_own(sums, recvs, name):
    n = len(sums)

    def body(*refs):
        ins, outs = refs[:n], refs[2 * n:3 * n]
        loc = refs[3 * n]
        stage = refs[3 * n + 1:]
        x, y, _ = _coords()
        me = 2 * x + y
        _run_staged([_via_vmem(ins[t].at[me], outs[t].at[me], stage[t], loc, t) for t in range(n)], [])

    return pl.pallas_call(
        body, name=name,
        out_shape=[jax.ShapeDtypeStruct(r.shape, r.dtype) for r in recvs],
        in_specs=[ANY] * (2 * n), out_specs=[ANY] * n, input_output_aliases={n + t: t for t in range(n)},
        scratch_shapes=[pltpu.SemaphoreType.DMA((2 * n,))] + [pltpu.VMEM(s.shape[1:], s.dtype) for s in sums],
        compiler_params=pltpu.CompilerParams(vmem_limit_bytes=VMEM_LIMIT),
    )(*sums, *recvs)


def _scatter_chips_behind(sums):
    n = len(sums)

    def build(ins, outs, scr):
        send, recv, loc = scr[:3]
        stage = scr[3:]
        x, y, c = _coords()
        me = 2 * x + y
        staged = [_via_vmem(ins[t].at[me], outs[t].at[me], stage[t], loc, t) for t in range(n)]
        remote = []
        for t in range(n):
            for k, (dx, dy) in enumerate(_REL3):
                px, py = _flip(x, dx), _flip(y, dy)
                remote.append(pltpu.make_async_remote_copy(
                    src_ref=ins[t].at[2 * px + py], dst_ref=outs[t].at[me],
                    send_sem=send.at[3 * t + k], recv_sem=recv.at[3 * t + k],
                    device_id=(px, py, c), device_id_type=MESH))
        return staged, remote

    return _Behind(sums, [jax.ShapeDtypeStruct(s.shape, s.dtype) for s in sums],
                   [pltpu.SemaphoreType.DMA((3 * n,)), pltpu.SemaphoreType.DMA((3 * n,)),
                    pltpu.SemaphoreType.DMA((2 * n,))] + [pltpu.VMEM(s.shape[1:], s.dtype) for s in sums], build)


def _gather_all_behind(a):
    def build(ins, outs, scr):
        send, recv, loc, stage = scr
        x, y, c = _coords()
        me = 4 * x + 2 * y + c
        staged = [_via_vmem(ins[0], outs[0].at[me], stage, loc, 0)]
        remote = [pltpu.make_async_remote_copy(
            src_ref=ins[0], dst_ref=outs[0].at[me], send_sem=send.at[k], recv_sem=recv.at[k],
            device_id=(_flip(x, dx), _flip(y, dy), _flip(c, dc)), device_id_type=MESH)
            for k, (dx, dy, dc) in enumerate(_REL7)]
        return staged, remote

    return _Behind([a], [jax.ShapeDtypeStruct((8,) + a.shape, a.dtype)],
                   [pltpu.SemaphoreType.DMA((7,)), pltpu.SemaphoreType.DMA((7,)), pltpu.SemaphoreType.DMA((2,)),
                    pltpu.VMEM(a.shape, a.dtype)], build)


HBM = pl.BlockSpec(memory_space=pltpu.HBM)
SEM = pl.BlockSpec(memory_space=pltpu.SEMAPHORE)
EFFECT = pltpu.SideEffectType.DATAFLOW_SIDE_EFFECTING


def _chip_copies(srcs, lands, send, recv, gather):
    x, y, c = _coords()
    me = 2 * x + y
    cps = []
    for t in range(len(srcs)):
        for k, (dx, dy) in enumerate(_REL3):
            px, py = _flip(x, dx), _flip(y, dy)
            if gather:
                half = srcs[t].shape[0] // 2
                mine = pl.ds(c * half, half)
                src, dst = srcs[t].at[mine], lands[t].at[me, mine]
            else:
                src, dst = srcs[t].at[2 * px + py], lands[t].at[me]
            cps.append(pltpu.make_async_remote_copy(
                src_ref=src, dst_ref=dst, send_sem=send.at[3 * t + k], recv_sem=recv.at[3 * t + k],
                device_id=(px, py, c), device_id_type=MESH))
    return cps


def _chips_start(arrays, land_shapes, gather, name, after=()):
    n = len(arrays)

    def body(*refs):
        srcs, lands = refs[:n], refs[n:2 * n]
        send, recv = refs[2 * n + len(after)], refs[2 * n + len(after) + 1]
        token = refs[-1]
        for cp in _chip_copies(srcs, lands, send, recv, gather):
            cp.start()
        token[...] = jnp.zeros_like(token)

    lands = [lax.empty(s.shape, s.dtype) for s in land_shapes]
    thru = [pltpu.HBM(a.shape, a.dtype) for a in arrays] + [pltpu.HBM(s.shape, s.dtype) for s in land_shapes]
    res = pl.pallas_call(
        body, name=name,
        out_shape=(pltpu.SemaphoreType.DMA((3 * n,)), pltpu.SemaphoreType.DMA((3 * n,)), *thru,
                   jax.ShapeDtypeStruct((8, 128), F32)),
        in_specs=[HBM] * (2 * n) + [ANY] * len(after),
        out_specs=(SEM, SEM, *([HBM] * (2 * n)), pl.BlockSpec(memory_space=pltpu.VMEM)),
        input_output_aliases={t: 2 + t for t in range(2 * n)},
        compiler_params=pltpu.CompilerParams(has_side_effects=EFFECT),
    )(*[pltpu.with_memory_space_constraint(a, pltpu.HBM) for a in arrays],
      *[pltpu.with_memory_space_constraint(z, pltpu.HBM) for z in lands], *after)
    return res[0], res[1], list(res[2:2 + n]), list(res[2 + n:2 + 2 * n]), res[-1]


def _chips_wait(send, recv, arrays, lands, after, gather, name):
    n = len(arrays)

    def body(*refs):
        srcs, ls = refs[:n], refs[n:2 * n]
        sd, rv = refs[2 * n], refs[2 * n + 1]
        for cp in _chip_copies(srcs, ls, sd, rv, gather):
            cp.wait_send()
            cp.wait_recv()

    res = pl.pallas_call(
        body, name=name,
        out_shape=[pltpu.HBM(a.shape, a.dtype) for a in arrays] + [pltpu.HBM(z.shape, z.dtype) for z in lands],
        in_specs=[HBM] * (2 * n) + [SEM, SEM] + [ANY] * len(after), out_specs=[HBM] * (2 * n),
        input_output_aliases={t: t for t in range(2 * n)},
        compiler_params=pltpu.CompilerParams(has_side_effects=EFFECT),
    )(*arrays, *lands, send, recv, *after)
    return list(res[:n]), list(res[n:])


def _join(*parts):
    def cut(seq, key):
        res, o = [], 0
        for p in parts:
            k = len(getattr(p, key))
            res.append(seq[o:o + k])
            o += k
        return res

    def build(ins, outs, scr):
        staged, remote = [], []
        for p, i, o, s in zip(parts, cut(ins, "arrays"), cut(outs, "out_shapes"), cut(scr, "scratch")):
            st, rm = p.build(i, o, s)
            staged += st
            remote += rm
        return staged, remote

    return _Behind(sum((p.arrays for p in parts), []), sum((p.out_shapes for p in parts), []),
                   sum((p.scratch for p in parts), []), build)


def _gather_chips(shards, name):
    n = len(shards)

    def body(*refs):
        ins, outs = refs[:n], refs[n:2 * n]
        send, recv, fsend, frecv, loc = refs[2 * n:2 * n + 5]
        stage = refs[2 * n + 5:]
        x, y, c = _coords()
        me = 2 * x + y
        own = [_via_vmem(ins[t], outs[t].at[me], stage[t], loc, t) for t in range(n)]
        first, passed = [], []
        for t in range(n):
            half = shards[t].shape[0] // 2
            mine, theirs = pl.ds(c * half, half), pl.ds((1 - c) * half, half)
            for k, (dx, dy) in enumerate(_REL3):
                px, py = _flip(x, dx), _flip(y, dy)
                first.append(pltpu.make_async_remote_copy(
                    src_ref=ins[t].at[mine], dst_ref=outs[t].at[me, mine],
                    send_sem=send.at[3 * t + k], recv_sem=recv.at[3 * t + k],
                    device_id=(px, py, c), device_id_type=MESH))
                passed.append((
                    pltpu.make_async_remote_copy(
                        src_ref=outs[t].at[2 * px + py, mine], dst_ref=outs[t].at[2 * px + py, mine],
                        send_sem=fsend.at[3 * t + k], recv_sem=frecv.at[3 * t + k],
                        device_id=(x, y, 1 - c), device_id_type=MESH),
                    pltpu.make_async_remote_copy(
                        src_ref=outs[t].at[2 * px + py, theirs], dst_ref=outs[t].at[2 * px + py, theirs],
                        send_sem=fsend.at[3 * t + k], recv_sem=frecv.at[3 * t + k],
                        device_id=(x, y, 1 - c), device_id_type=MESH)))
        for load, _ in own:
            load.start()
        for cp in first:
            cp.start()
        for load, store in own:
            load.wait()
            store.start()
        for cp, (fwd, _) in zip(first, passed):
            cp.wait_recv()
            fwd.start()
        for cp, (fwd, back) in zip(first, passed):
            cp.wait_send()
            fwd.wait_send()
            back.wait_recv()
        for _, store in own:
            store.wait()

    return pl.pallas_call(
        body, name=name,
        out_shape=[jax.ShapeDtypeStruct((NSH,) + s.shape, s.dtype) for s in shards],
        in_specs=[ANY] * n, out_specs=[ANY] * n,
        scratch_shapes=[pltpu.SemaphoreType.DMA((3 * n,)) for _ in range(4)] + [pltpu.SemaphoreType.DMA((2 * n,))]
        + [pltpu.VMEM(s.shape, s.dtype) for s in shards],
        compiler_params=pltpu.CompilerParams(vmem_limit_bytes=VMEM_LIMIT),
    )(*shards)


_REL7 = tuple((dx, dy, dc) for dx in (0, 1) for dy in (0, 1) for dc in (0, 1))[1:]


def _gather_all(a, name):
    def body(a_ref, o_ref, send, recv, loc):
        x, y, c = _coords()
        me = 4 * x + 2 * y + c
        local = [pltpu.make_async_copy(a_ref, o_ref.at[me], loc.at[0])]
        remote = [pltpu.make_async_remote_copy(
            src_ref=a_ref, dst_ref=o_ref.at[me], send_sem=send.at[k], recv_sem=recv.at[k],
            device_id=(_flip(x, dx), _flip(y, dy), _flip(c, dc)), device_id_type=MESH)
            for k, (dx, dy, dc) in enumerate(_REL7)]
        _run(local, remote)

    return pl.pallas_call(
        body, name=name,
        out_shape=jax.ShapeDtypeStruct((8,) + a.shape, a.dtype),
        in_specs=[ANY], out_specs=ANY,
        scratch_shapes=[pltpu.SemaphoreType.DMA((7,)), pltpu.SemaphoreType.DMA((7,)),
                        pltpu.SemaphoreType.DMA((1,))],
    )(a)


def _pair_exchange(grads, name):
    n = len(grads)

    def body(*refs):
        ins, outs = refs[:n], refs[n:2 * n]
        send, recv = refs[2 * n:]
        x, y, c = _coords()
        remote = []
        for t in range(n):
            half = grads[t].shape[1] // 2
            remote.append(pltpu.make_async_remote_copy(
                src_ref=ins[t].at[:, pl.ds((1 - c) * half, half)], dst_ref=outs[t],
                send_sem=send.at[t], recv_sem=recv.at[t],
                device_id=(x, y, 1 - c), device_id_type=MESH))
        _run([], remote)

    return pl.pallas_call(
        body, name=name,
        out_shape=[jax.ShapeDtypeStruct((NSH, g.shape[1] // 2, g.shape[2]), g.dtype) for g in grads],
        in_specs=[ANY] * n, out_specs=[ANY] * n,
        scratch_shapes=[pltpu.SemaphoreType.DMA((n,)), pltpu.SemaphoreType.DMA((n,))],
    )(*grads)


def _scatter_chips(sums, name):
    n = len(sums)

    def body(*refs):
        ins, outs = refs[:n], refs[n:2 * n]
        send, recv, loc = refs[2 * n:2 * n + 3]
        stage = refs[2 * n + 3:]
        x, y, c = _coords()
        me = 2 * x + y
        local = [_via_vmem(ins[t].at[me], outs[t].at[me], stage[t], loc, t) for t in range(n)]
        remote = []
        for t in range(n):
            for k, (dx, dy) in enumerate(_REL3):
                px, py = _flip(x, dx), _flip(y, dy)
                remote.append(pltpu.make_async_remote_copy(
                    src_ref=ins[t].at[2 * px + py], dst_ref=outs[t].at[me],
                    send_sem=send.at[3 * t + k], recv_sem=recv.at[3 * t + k],
                    device_id=(px, py, c), device_id_type=MESH))
        _run_staged(local, remote)

    return pl.pallas_call(
        body, name=name,
        out_shape=[jax.ShapeDtypeStruct(s.shape, s.dtype) for s in sums],
        in_specs=[ANY] * n, out_specs=[ANY] * n,
        scratch_shapes=[pltpu.SemaphoreType.DMA((3 * n,)), pltpu.SemaphoreType.DMA((3 * n,)),
                        pltpu.SemaphoreType.DMA((2 * n,))]
        + [pltpu.VMEM(s.shape[1:], s.dtype) for s in sums],
        compiler_params=pltpu.CompilerParams(vmem_limit_bytes=VMEM_LIMIT),
    )(*sums)


def _swap_halves(halves, name):
    n = len(halves)

    def body(*refs):
        ins, outs = refs[:n], refs[n:2 * n]
        send, recv, loc = refs[2 * n:2 * n + 3]
        stage = refs[2 * n + 3:]
        x, y, c = _coords()
        local = [_via_vmem(ins[t], outs[t].at[c], stage[t], loc, t) for t in range(n)]
        remote = [pltpu.make_async_remote_copy(
            src_ref=ins[t], dst_ref=outs[t].at[c], send_sem=send.at[t], recv_sem=recv.at[t],
            device_id=(x, y, 1 - c), device_id_type=MESH) for t in range(n)]
        _run_staged(local, remote)

    return pl.pallas_call(
        body, name=name,
        out_shape=[jax.ShapeDtypeStruct((2,) + h.shape, h.dtype) for h in halves],
        in_specs=[ANY] * n, out_specs=[ANY] * n,
        scratch_shapes=[pltpu.SemaphoreType.DMA((n,)), pltpu.SemaphoreType.DMA((n,)),
                        pltpu.SemaphoreType.DMA((2 * n,))]
        + [pltpu.VMEM(h.shape, h.dtype) for h in halves],
        compiler_params=pltpu.CompilerParams(vmem_limit_bytes=VMEM_LIMIT),
    )(*halves)


def _sum_slots(r, name, after=None):
    K, R, C = r.shape
    tr = _tile(R, max(16, (1 << 22) // (K * C)), 8 * (4 // r.dtype.itemsize))

    def body(r_ref, *rest):
        o_ref = rest[-1]
        acc = r_ref[0].astype(F32)
        for k in range(1, K):
            acc = acc + r_ref[k].astype(F32)
        o_ref[...] = acc

    dep = [] if after is None else [after]
    return pl.pallas_call(
        body, name=name, grid=(R // tr,),
        out_shape=jax.ShapeDtypeStruct((R, C), F32),
        in_specs=[pl.BlockSpec((K, tr, C), lambda i: (0, i, 0))] + [ANY] * len(dep),
        out_specs=pl.BlockSpec((tr, C), lambda i: (i, 0)),
        compiler_params=_params("parallel"),
    )(r, *dep)


def _add_pair(g, s, core, name):
    _, half, C = s.shape
    tr = _tile(half, max(16, (1 << 19) // C))
    nb = half // tr

    def body(c_ref, g_ref, s_ref, o_ref):
        o_ref[...] = (g_ref[...].astype(F32) + s_ref[...].astype(F32)).astype(BF16)

    spec = pl.BlockSpec((1, tr, C), lambda j, i, c_ref: (j, i, 0))
    return pl.pallas_call(
        body, name=name,
        grid_spec=pltpu.PrefetchScalarGridSpec(
            num_scalar_prefetch=1, grid=(NSH, nb),
            in_specs=[pl.BlockSpec((1, tr, C), lambda j, i, c_ref: (j, c_ref[0] * nb + i, 0)), spec],
            out_specs=spec),
        out_shape=jax.ShapeDtypeStruct(s.shape, BF16),
        compiler_params=_params("parallel", "parallel"),
    )(core, g, s)


def _adamw(w, g, m, v, name):
    _, R, C = w.shape
    tr = _tile(R, max(8, (1 << 18) // C), 8)
    c1 = 1.0 / (1.0 - B1 ** STEP)
    c2 = 1.0 / (1.0 - B2 ** STEP)

    def body(w_ref, g_ref, m_ref, v_ref, d_ref, nm_ref, nv_ref):
        gv = g_ref[...]
        nm = B1 * m_ref[...] + (1.0 - B1) * gv
        nv = B2 * v_ref[...] + (1.0 - B2) * gv * gv
        nm_ref[...] = nm
        nv_ref[...] = nv
        d_ref[...] = -LR * ((nm * c1) / (jnp.sqrt(nv * c2) + AEPS) + WD * w_ref[...])

    spec = pl.BlockSpec((1, tr, C), lambda i: (0, i, 0))
    return pl.pallas_call(
        body, name=name, grid=(R // tr,),
        out_shape=[jax.ShapeDtypeStruct((1, R, C), F32)] * 3,
        in_specs=[spec] * 4, out_specs=[spec] * 3,
        compiler_params=_params("parallel"),
    )(w, g, m, v)


def _ffn_fwd(h, g, w1, w3, w2, name, comm=None):
    L = h.shape[0]
    tm = _tile(L, 704)

    def body(h_ref, g_ref, w1_ref, w3_ref, w2_ref, o_ref, a_ref, b_ref, n_s, acc_s):
        j = pl.program_id(1)

        @pl.when(j == 0)
        def _():
            hv = h_ref[...]
            n, _ = _rms(hv, g_ref[...])
            n_s[...] = n.astype(BF16)
            acc_s[...] = hv

        n = n_s[...]
        a = _dot_nt(n, w1_ref[0])
        b = _dot_nt(n, w3_ref[0])
        a_ref[0] = a.astype(BF16)
        b_ref[0] = b.astype(BF16)
        s = (a * _sigmoid(a) * b).astype(BF16)
        acc_s[...] += 0.5 * _dot(s, w2_ref[0])

        @pl.when(j == NSH - 1)
        def _():
            o_ref[...] = acc_s[...]

    row = pl.BlockSpec((tm, D), lambda i, j: (i, 0))
    hid = pl.BlockSpec((1, tm, FS), lambda i, j: (j, i, 0))
    wsp = pl.BlockSpec((1, FS, D), lambda i, j: (j, 0, 0))
    return _call(
        body, comm, name=name, grid=(L // tm, NSH),
        out_shape=[jax.ShapeDtypeStruct((L, D), F32),
                   jax.ShapeDtypeStruct((NSH, L, FS), BF16), jax.ShapeDtypeStruct((NSH, L, FS), BF16)],
        in_specs=[row, _res((1, D)), wsp, wsp, wsp],
        out_specs=[row, hid, hid],
        scratch_shapes=[pltpu.VMEM((tm, D), BF16), pltpu.VMEM((tm, D), F32)],
        params=_params("arbitrary", "arbitrary"),
    )(h, g, w1, w3, w2)


def _loss_head(hv, gv, tv, row0):
    y, r = _rms(hv, gv)
    row = row0 + lax.broadcasted_iota(jnp.int32, (hv.shape[0], 1), 0)
    e = jnp.where(row >= FRONT, y - tv, 0.0)
    dy = e * (1.0 / D)
    part = 0.5 * jnp.sum(jnp.sum(e * dy, axis=1, keepdims=True), axis=0, keepdims=True)
    dx, xh = _rms_bwd(dy, hv, r, gv)
    return dx, part, jnp.sum(dy * xh, axis=0, keepdims=True)


def _ffn_fwd_loss(h, g, w1, w3, w2, gf, tgt, name):
    L = h.shape[0]
    tm = _tile(L, 704)

    def body(h_ref, g_ref, w1_ref, w3_ref, w2_ref, gf_ref, t_ref, o_ref, a_ref, b_ref, loss_ref, dgf_ref,
             n_s, acc_s):
        i, j = pl.program_id(0), pl.program_id(1)

        @pl.when(j == 0)
        def _():
            hv = h_ref[...]
            n, _ = _rms(hv, g_ref[...])
            n_s[...] = n.astype(BF16)
            acc_s[...] = hv

        n = n_s[...]
        a = _dot_nt(n, w1_ref[0])
        b = _dot_nt(n, w3_ref[0])
        a_ref[0] = a.astype(BF16)
        b_ref[0] = b.astype(BF16)
        s = (a * _sigmoid(a) * b).astype(BF16)
        acc_s[...] += 0.5 * _dot(s, w2_ref[0])

        @pl.when(j == NSH - 1)
        def _():
            dx, part, dgf = _loss_head(acc_s[...], gf_ref[...], t_ref[...], i * tm)
            o_ref[...] = dx
            _acc_rows(loss_ref, part, i == 0)
            _acc_rows(dgf_ref, dgf, i == 0)

    row = pl.BlockSpec((tm, D), lambda i, j: (i, 0))
    hid = pl.BlockSpec((1, tm, FS), lambda i, j: (j, i, 0))
    wsp = pl.BlockSpec((1, FS, D), lambda i, j: (j, 0, 0))
    return pl.pallas_call(
        body, name=name, grid=(L // tm, NSH),
        out_shape=[jax.ShapeDtypeStruct((L, D), F32),
                   jax.ShapeDtypeStruct((NSH, L, FS), BF16), jax.ShapeDtypeStruct((NSH, L, FS), BF16),
                   jax.ShapeDtypeStruct((1, 1), F32), jax.ShapeDtypeStruct((1, D), F32)],
        in_specs=[row, _res((1, D)), wsp, wsp, wsp, _res((1, D)), row],
        out_specs=[row, hid, hid, pl.BlockSpec((1, 1), lambda i, j: (0, 0)),
                   pl.BlockSpec((1, D), lambda i, j: (0, 0))],
        scratch_shapes=[pltpu.VMEM((tm, D), BF16), pltpu.VMEM((tm, D), F32)],
        compiler_params=_params("arbitrary", "arbitrary"),
    )(h, g, w1, w3, w2, gf, tgt)


def _ffn_bwd(h, g, dout, a, b, w1, w3, w2, name, comm=None):
    L = h.shape[0]
    tm = _tile(L, 528)

    def body(h_ref, g_ref, do_ref, a_ref, b_ref, w1_ref, w3_ref, w2_ref,
             dh_ref, da_ref, db_ref, s_ref, n_ref, dg_ref, dob_s, dn_s):
        i, j = pl.program_id(0), pl.program_id(1)

        @pl.when(j == 0)
        def _():
            n, _ = _rms(h_ref[...], g_ref[...])
            n_ref[...] = n.astype(BF16)
            dob_s[...] = (0.5 * do_ref[...]).astype(BF16)
            dn_s[...] = jnp.zeros_like(dn_s)

        av = a_ref[0].astype(F32)
        bv = b_ref[0].astype(F32)
        sig = _sigmoid(av)
        sa = av * sig
        ds = _dot_nt(dob_s[...], w2_ref[0])
        s_ref[0] = (sa * bv).astype(BF16)
        da = (ds * bv * (sig + sa * (1.0 - sig))).astype(BF16)
        db = (ds * sa).astype(BF16)
        da_ref[0] = da
        db_ref[0] = db
        dn_s[...] += _dot(da, w1_ref[0]) + _dot(db, w3_ref[0])

        @pl.when(j == NSH - 1)
        def _():
            hv = h_ref[...]
            gv = g_ref[...]
            r = lax.rsqrt(jnp.mean(hv * hv, axis=-1, keepdims=True) + EPS)
            dn = dn_s[...]
            dx, xh = _rms_bwd(dn, hv, r, gv)
            dh_ref[...] = do_ref[...] + dx
            _acc_rows(dg_ref, jnp.sum(dn * xh, axis=0, keepdims=True), i == 0)

    row = pl.BlockSpec((tm, D), lambda i, j: (i, 0))
    hid = pl.BlockSpec((1, tm, FS), lambda i, j: (j, i, 0))
    wsp = pl.BlockSpec((1, FS, D), lambda i, j: (j, 0, 0))
    return _call(
        body, comm, name=name, grid=(L // tm, NSH),
        out_shape=[jax.ShapeDtypeStruct((L, D), F32)]
        + [jax.ShapeDtypeStruct((NSH, L, FS), BF16)] * 3
        + [jax.ShapeDtypeStruct((L, D), BF16), jax.ShapeDtypeStruct((1, D), F32)],
        in_specs=[row, _res((1, D)), row, hid, hid,
                  wsp, wsp, wsp],
        out_specs=[row, hid, hid, hid, row, pl.BlockSpec((1, D), lambda i, j: (0, 0))],
        scratch_shapes=[pltpu.VMEM((tm, D), BF16), pltpu.VMEM((tm, D), F32)],
        params=_params("arbitrary", "arbitrary"),
    )(h, g, dout, a, b, w1, w3, w2)


def _wgrad(xm, ym, name, scale=1.0):
    xs, ys = xm.ndim == 3, ym.ndim == 3
    assert not (xs and ys)
    L = xm.shape[-2]
    K, N = xm.shape[-1], ym.shape[-1]
    tl = _tile(L, 2112)
    nl = L // tl
    if xs or ys:
        tn, grid_n = N, NSH
    else:
        tn = _tile(N, 1024, 128)
        grid_n = N // tn

    def body(x_ref, y_ref, o_ref, acc_s):
        l = pl.program_id(1)
        xv = x_ref[0] if xs else x_ref[...]
        yv = y_ref[0] if ys else y_ref[...]
        part = _dot_tn(xv.astype(BF16), yv.astype(BF16))
        _acc_rows(acc_s, part, l == 0)

        @pl.when(l == nl - 1)
        def _():
            res = (acc_s[...] * scale).astype(BF16)
            if xs or ys:
                o_ref[0] = res
            else:
                o_ref[...] = res

    if xs:
        x_spec = pl.BlockSpec((1, tl, K), lambda n, l: (n, l, 0))
        y_spec = pl.BlockSpec((tl, N), lambda n, l: (l, 0))
        o_spec = pl.BlockSpec((1, K, N), lambda n, l: (n, 0, 0))
        o_shape = (NSH, K, N)
    elif ys:
        x_spec = pl.BlockSpec((tl, K), lambda n, l: (l, 0))
        y_spec = pl.BlockSpec((1, tl, N), lambda n, l: (n, l, 0))
        o_spec = pl.BlockSpec((1, K, N), lambda n, l: (n, 0, 0))
        o_shape = (NSH, K, N)
    else:
        x_spec = pl.BlockSpec((tl, K), lambda n, l: (l, 0))
        y_spec = pl.BlockSpec((tl, tn), lambda n, l: (l, n))
        o_spec = pl.BlockSpec((K, tn), lambda n, l: (0, n))
        o_shape = (K, N)
    return pl.pallas_call(
        body, name=name, grid=(grid_n, nl),
        out_shape=jax.ShapeDtypeStruct(o_shape, BF16),
        in_specs=[x_spec, y_spec], out_specs=o_spec,
        scratch_shapes=[pltpu.VMEM((K, tn), F32)],
        compiler_params=_params("parallel", "arbitrary"),
    )(xm, ym)


def _mix_in_fwd(h, g, w_in, b_gate, name, comm=None):
    L = h.shape[0]
    tm = _tile(L, 528)

    def body(h_ref, g_ref, w_ref, bg_ref, vg_ref, uf_ref, gt_ref):
        u, _ = _rms(h_ref[...], g_ref[...])
        ub = u.astype(BF16)
        p = [_dot(ub, w_ref[j]) for j in range(NSH)]
        a0, a1 = 2 * DC - WS, 2 * DC + DS - WS
        vg_ref[:, 0:WS] = p[0].astype(BF16)
        vg_ref[:, WS:2 * DC] = p[1][:, 0:a0].astype(BF16)
        uf_ref[...] = p[1][:, a0:a1].astype(BF16)
        gin = jnp.concatenate([p[1][:, a1:], p[2], p[3]], axis=1)
        gt_ref[...] = _sigmoid(gin + bg_ref[...]).astype(BF16)

    def row(n):
        return pl.BlockSpec((tm, n), lambda i: (i, 0))

    return _call(
        body, comm, name=name, grid=(L // tm,),
        out_shape=[jax.ShapeDtypeStruct((L, 2 * DC), BF16), jax.ShapeDtypeStruct((L, DS), BF16),
                   jax.ShapeDtypeStruct((L, 2 * D), BF16)],
        in_specs=[row(D), _res((1, D)), _res((NSH, D, WS)), _res((1, 2 * D))],
        out_specs=[row(2 * DC), row(DS), row(2 * D)],
        params=_params("parallel"),
    )(h, g, w_in, b_gate)


def _mix_in_bwd(h, g, dres, dv, dgl, duf, dgate, w_in, name):
    L = h.shape[0]
    tm = _tile(L, 528)

    def body(h_ref, g_ref, dr_ref, dv_ref, dgl_ref, duf_ref, dgt_ref, w_ref, dh_ref, u_ref, dp_ref, dgm_ref):
        i = pl.program_id(0)
        hv = h_ref[...]
        gv = g_ref[...]
        u, r = _rms(hv, gv)
        u_ref[...] = u.astype(BF16)
        a0, a1 = 2 * DC - WS, 2 * DC + DS - WS
        b0 = WS - a1
        dp = [jnp.concatenate([dv_ref[...], dgl_ref[:, 0:WS - DC]], axis=1),
              jnp.concatenate([dgl_ref[:, WS - DC:], duf_ref[...], dgt_ref[:, 0:b0]], axis=1),
              dgt_ref[:, b0:b0 + WS], dgt_ref[:, b0 + WS:]]
        du = jnp.zeros((tm, D), F32)
        for j in range(NSH):
            dp_ref[j] = dp[j]
            du = du + _dot_nt(dp[j], w_ref[j])
        dx, xh = _rms_bwd(du, hv, r, gv)
        dh_ref[...] = dr_ref[...] + dx
        _acc_rows(dgm_ref, jnp.sum(du * xh, axis=0, keepdims=True), i == 0)

    def row(n):
        return pl.BlockSpec((tm, n), lambda i: (i, 0))

    return pl.pallas_call(
        body, name=name, grid=(L // tm,),
        out_shape=[jax.ShapeDtypeStruct((L, D), F32), jax.ShapeDtypeStruct((L, D), BF16),
                   jax.ShapeDtypeStruct((NSH, L, WS), BF16), jax.ShapeDtypeStruct((1, D), F32)],
        in_specs=[row(D), _res((1, D)), row(D), row(DC), row(DC), row(DS), row(2 * D), _res((NSH, D, WS))],
        out_specs=[row(D), row(D), pl.BlockSpec((NSH, tm, WS), lambda i: (0, i, 0)),
                   pl.BlockSpec((1, D), lambda i: (0, 0))],
        compiler_params=_params("arbitrary"),
    )(h, g, dres, dv, dgl, duf, dgate, w_in)


def _conv_fwd(vg, dw, dwb, name, comm=None):
    L = vg.shape[0]
    nc = DC // 128

    def body(v_ref, g_ref, dw_ref, dwb_ref, z_ref, zp_s):
        zp_s[0:KWP, :] = jnp.zeros((KWP, 128), F32)
        zp_s[KWP:, :] = v_ref[...].astype(F32) * _sigmoid(g_ref[...].astype(F32))
        for r0 in range(0, L, CONV_ROWS):
            acc = jnp.broadcast_to(dwb_ref[...], (CONV_ROWS, 128))
            for k in range(KW):
                acc = acc + dw_ref[k:k + 1, :] * zp_s[pl.ds(r0 + k + 2, CONV_ROWS), :]
            z_ref[pl.ds(r0, CONV_ROWS), :] = acc

    return _call(
        body, comm, name=name, grid=(nc,),
        out_shape=[jax.ShapeDtypeStruct((L, DC), F32)],
        in_specs=[pl.BlockSpec((L, 128), lambda c: (0, c)), pl.BlockSpec((L, 128), lambda c: (0, nc + c)),
                  pl.BlockSpec((KWP, 128), lambda c: (0, c)), pl.BlockSpec((1, 128), lambda c: (0, c))],
        out_specs=[pl.BlockSpec((L, 128), lambda c: (0, c))],
        scratch_shapes=[pltpu.VMEM((L + KWP, 128), F32)],
        params=_params("parallel"),
    )(vg, vg, dw, dwb)


def _conv_bwd(dz1, vg, dw, name):
    L = vg.shape[0]
    nc = DC // 128

    def body(dz_ref, v_ref, g_ref, dw_ref, dv_ref, dg_ref, ddw_ref, ddwb_ref, zp_s, dzp_s):
        vv = v_ref[...].astype(F32)
        sg = _sigmoid(g_ref[...].astype(F32))
        zp_s[0:KWP, :] = jnp.zeros((KWP, 128), F32)
        zp_s[KWP:, :] = vv * sg
        dz = dz_ref[...]
        dzp_s[0:L, :] = dz
        dzp_s[L:, :] = jnp.zeros((KWP, 128), F32)
        ddwb_ref[...] = jnp.sum(dz, axis=0, keepdims=True)
        part = [jnp.zeros((8, 128), F32) for _ in range(KW)]
        for r0 in range(0, L, CONV_ROWS):
            rows = pl.ds(r0, CONV_ROWS)
            dzc = dz_ref[rows, :]
            acc = jnp.zeros((CONV_ROWS, 128), F32)
            for k in range(KW):
                acc = acc + dw_ref[k:k + 1, :] * dzp_s[pl.ds(r0 + KW - 1 - k, CONV_ROWS), :]
                prod = dzc * zp_s[pl.ds(r0 + k + 2, CONV_ROWS), :]
                for q in range(CONV_ROWS // 8):
                    part[k] = part[k] + prod[8 * q:8 * (q + 1), :]
            vc = v_ref[rows, :].astype(F32)
            sc = _sigmoid(g_ref[rows, :].astype(F32))
            dv_ref[rows, :] = (acc * sc).astype(BF16)
            dg_ref[rows, :] = (acc * vc * sc * (1.0 - sc)).astype(BF16)
        for k in range(KW):
            ddw_ref[k:k + 1, :] = jnp.sum(part[k], axis=0, keepdims=True)
        ddw_ref[KW:KWP, :] = jnp.zeros((KWP - KW, 128), F32)

    col = pl.BlockSpec((L, 128), lambda c: (0, c))
    return pl.pallas_call(
        body, name=name, grid=(nc,),
        out_shape=[jax.ShapeDtypeStruct((L, DC), BF16), jax.ShapeDtypeStruct((L, DC), BF16),
                   jax.ShapeDtypeStruct((KWP, DC), F32), jax.ShapeDtypeStruct((1, DC), F32)],
        in_specs=[col, col, pl.BlockSpec((L, 128), lambda c: (0, nc + c)),
                  pl.BlockSpec((KWP, 128), lambda c: (0, c))],
        out_specs=[col, col, pl.BlockSpec((KWP, 128), lambda c: (0, c)), pl.BlockSpec((1, 128), lambda c: (0, c))],
        scratch_shapes=[pltpu.VMEM((L + KWP, 128), F32), pltpu.VMEM((L + KWP, 128), F32)],
        compiler_params=_params("parallel"),
    )(dz1, vg, vg, dw)


NLB = QS // 128


def _lb_store(ref, rows, val):
    for cb in range(NLB):
        ref[cb, rows, :] = val[:, cb * 128:(cb + 1) * 128]


def _lb_load(ref, rows):
    return jnp.concatenate([ref[cb, rows, :] for cb in range(NLB)], axis=1)


def _scan(xr_ref, xi_ref, base, T, ar, ai, atr, ati, reverse):
    W = ar.shape[1]
    ar, ai, atr, ati = (jnp.broadcast_to(v, (8, W)) for v in (ar, ai, atr, ati))
    zero = jnp.zeros((8, W), F32)

    def rows(t, g):
        tt = T - 1 - t if reverse else t
        return pl.ds(base + g * 8 * T + tt, 8, stride=T)

    def make_step(store):
        def step(t, carry):
            out = []
            for g in range(NGRP):
                sr, si = carry[2 * g], carry[2 * g + 1]
                idx = rows(t, g)
                nr = ar * sr - ai * si + _lb_load(xr_ref, idx)
                ni = ar * si + ai * sr + _lb_load(xi_ref, idx)
                if store:
                    _lb_store(xr_ref, idx, nr)
                    _lb_store(xi_ref, idx, ni)
                out += [nr, ni]
            return tuple(out)
        return step

    ends = lax.fori_loop(0, T, make_step(False), (zero,) * (2 * NGRP))
    sub = lax.broadcasted_iota(jnp.int32, (8, W), 0)
    edge = sub == (7 if reverse else 0)
    shift, last = (7, 0) if reverse else (1, 7)
    inr, ini = jnp.zeros((1, W), F32), jnp.zeros((1, W), F32)
    starts = [None] * (2 * NGRP)
    for g in (reversed(range(NGRP)) if reverse else range(NGRP)):
        er, ei = ends[2 * g], ends[2 * g + 1]
        cr, ci = jnp.where(edge, inr, 0.0), jnp.where(edge, ini, 0.0)
        for _ in range(7):
            nr = atr * cr - ati * ci + er
            ni = atr * ci + ati * cr + ei
            cr = jnp.where(edge, inr, pltpu.roll(nr, shift, 0))
            ci = jnp.where(edge, ini, pltpu.roll(ni, shift, 0))
        starts[2 * g], starts[2 * g + 1] = cr, ci
        inr = (atr * cr - ati * ci + er)[last:last + 1]
        ini = (atr * ci + ati * cr + ei)[last:last + 1]
    lax.fori_loop(0, T, make_step(True), tuple(starts))


def _ssm_fwd(uf, bre, bim, cre, cim, lamp, dsk, name, comm=None):
    L = uf.shape[0]
    T = L // NSEG
    tc = L // NCH

    def body(u_ref, bre_ref, bim_ref, cre_ref, cim_ref, lam_ref, d_ref, y_ref, sr_s, si_s):
        for k in range(NCH):
            sl = slice(k * tc, (k + 1) * tc)
            uk = u_ref[sl, :]
            _lb_store(sr_s, sl, _dot(uk, bre_ref[...]))
            _lb_store(si_s, sl, _dot(uk, bim_ref[...]))
        _scan(sr_s, si_s, 0, T, lam_ref[0:1, :], lam_ref[1:2, :], lam_ref[2:3, :], lam_ref[3:4, :], False)
        for k in range(NCH):
            sl = slice(k * tc, (k + 1) * tc)
            y_ref[sl, :] = (_dot(_lb_load(sr_s, sl).astype(BF16), cre_ref[...])
                            - _dot(_lb_load(si_s, sl).astype(BF16), cim_ref[...])
                            + d_ref[...] * u_ref[sl, :].astype(F32))

    return _call(
        body, comm, name=name, grid=(NQ,),
        out_shape=[jax.ShapeDtypeStruct((L, DS), F32)],
        in_specs=[pl.BlockSpec((L, QU), lambda q: (0, q)),
                  pl.BlockSpec((QU, QS), lambda q: (q, q)), pl.BlockSpec((QU, QS), lambda q: (q, q)),
                  pl.BlockSpec((QS, QU), lambda q: (q, q)), pl.BlockSpec((QS, QU), lambda q: (q, q)),
                  pl.BlockSpec((8, QS), lambda q: (0, q)), pl.BlockSpec((1, QU), lambda q: (0, q))],
        out_specs=[pl.BlockSpec((L, QU), lambda q: (0, q))],
        scratch_shapes=[pltpu.VMEM((NLB, L, 128), F32), pltpu.VMEM((NLB, L, 128), F32)],
        params=_params("parallel"),
    )(uf, bre, bim, cre, cim, lamp, dsk)


def _ssm_bwd(uf, dyss, bre, bim, cre, cim, lamp, dsk, name, comm=None):
    L = uf.shape[0]
    T = L // NSEG
    tc = L // NCH

    def body(u_ref, dy_ref, bre_ref, bim_ref, cre_ref, cim_ref, lam_ref, d_ref,
             du_ref, dbre_ref, dbim_ref, dcre_ref, dcim_ref, dlam_ref, dd_ref, sr_s, si_s, gr_s, gi_s):
        _lb_store(sr_s, slice(0, SOFF), jnp.zeros((SOFF, QS), F32))
        _lb_store(si_s, slice(0, SOFF), jnp.zeros((SOFF, QS), F32))
        for k in range(NCH):
            sl = slice(k * tc, (k + 1) * tc)
            ss = slice(SOFF + k * tc, SOFF + (k + 1) * tc)
            uk = u_ref[sl, :]
            dyk = dy_ref[sl, :].astype(BF16)
            _lb_store(sr_s, ss, _dot(uk, bre_ref[...]))
            _lb_store(si_s, ss, _dot(uk, bim_ref[...]))
            _lb_store(gr_s, sl, _dot_nt(dyk, cre_ref[...]))
            _lb_store(gi_s, sl, -_dot_nt(dyk, cim_ref[...]))
        ar, ai, atr, ati = lam_ref[0:1, :], lam_ref[1:2, :], lam_ref[2:3, :], lam_ref[3:4, :]
        _scan(sr_s, si_s, SOFF, T, ar, ai, atr, ati, False)
        _scan(gr_s, gi_s, 0, T, ar, -ai, atr, -ati, True)
        dbre = jnp.zeros((QU, QS), F32)
        dbim = jnp.zeros((QU, QS), F32)
        dcre = jnp.zeros((QS, QU), F32)
        dcim = jnp.zeros((QS, QU), F32)
        dd = jnp.zeros((1, QU), F32)
        qr = jnp.zeros((1, QS), F32)
        qi = jnp.zeros((1, QS), F32)
        for k in range(NCH):
            sl = slice(k * tc, (k + 1) * tc)
            ss = slice(SOFF + k * tc, SOFF + (k + 1) * tc)
            sp = slice(SOFF - 1 + k * tc, SOFF - 1 + (k + 1) * tc)
            uk = u_ref[sl, :]
            dyk = dy_ref[sl, :]
            dyb = dyk.astype(BF16)
            gr, gi = _lb_load(gr_s, sl), _lb_load(gi_s, sl)
            pr, pi = _lb_load(sr_s, sp), _lb_load(si_s, sp)
            qr = qr + jnp.sum(gr * pr + gi * pi, axis=0, keepdims=True)
            qi = qi + jnp.sum(gi * pr - gr * pi, axis=0, keepdims=True)
            grb, gib = gr.astype(BF16), gi.astype(BF16)
            du_ref[sl, :] = (_dot_nt(grb, bre_ref[...]) + _dot_nt(gib, bim_ref[...])
                             + dyk * d_ref[...]).astype(BF16)
            dbre = dbre + _dot_tn(uk, grb)
            dbim = dbim + _dot_tn(uk, gib)
            dcre = dcre + _dot_tn(_lb_load(sr_s, ss).astype(BF16), dyb)
            dcim = dcim - _dot_tn(_lb_load(si_s, ss).astype(BF16), dyb)
            dd = dd + jnp.sum(dyk * uk.astype(F32), axis=0, keepdims=True)
        dlam_ref[0] = jnp.concatenate([qr, qi, jnp.zeros((6, QS), F32)], axis=0)
        dbre_ref[0] = dbre
        dbim_ref[0] = dbim
        dcre_ref[0] = dcre
        dcim_ref[0] = dcim
        dd_ref[...] = dd

    col = pl.BlockSpec((L, QU), lambda q: (0, q))
    bsp = pl.BlockSpec((QU, QS), lambda q: (q, q))
    csp = pl.BlockSpec((QS, QU), lambda q: (q, q))
    return _call(
        body, comm, name=name, grid=(NQ,),
        out_shape=[jax.ShapeDtypeStruct((L, DS), BF16),
                   jax.ShapeDtypeStruct((NQ, QU, QS), F32), jax.ShapeDtypeStruct((NQ, QU, QS), F32),
                   jax.ShapeDtypeStruct((NQ, QS, QU), F32), jax.ShapeDtypeStruct((NQ, QS, QU), F32),
                   jax.ShapeDtypeStruct((NQ, 8, QS), F32), jax.ShapeDtypeStruct((1, DS), F32)],
        in_specs=[col, col, bsp, bsp, csp, csp,
                  pl.BlockSpec((8, QS), lambda q: (0, q)), pl.BlockSpec((1, QU), lambda q: (0, q))],
        out_specs=[col,
                   pl.BlockSpec((1, QU, QS), lambda q: (q, 0, 0)), pl.BlockSpec((1, QU, QS), lambda q: (q, 0, 0)),
                   pl.BlockSpec((1, QS, QU), lambda q: (q, 0, 0)), pl.BlockSpec((1, QS, QU), lambda q: (q, 0, 0)),
                   pl.BlockSpec((1, 8, QS), lambda q: (q, 0, 0)), pl.BlockSpec((1, QU), lambda q: (0, q))],
        scratch_shapes=[pltpu.VMEM((NLB, L + SOFF, 128), F32), pltpu.VMEM((NLB, L + SOFF, 128), F32),
                        pltpu.VMEM((NLB, L, 128), F32), pltpu.VMEM((NLB, L, 128), F32)],
        params=_params("parallel"),
    )(uf, dyss, bre, bim, cre, cim, lamp, dsk)


def _branches(z1_ref, yss_ref, gt_ref, lng_ref, lnb_ref, wp_ref, wv_ref, wg_ref):
    zf = z1_ref[...]
    mu = jnp.mean(zf, axis=-1, keepdims=True)
    zc = zf - mu
    rstd = lax.rsqrt(jnp.mean(zc * zc, axis=-1, keepdims=True) + EPS)
    zn = zc * rstd
    z2 = zn * lng_ref[...] + lnb_ref[...]
    sz = _sigmoid(z2)
    z3 = (z2 * sz).astype(BF16)
    y_conv = _dot(z3, wp_ref[...])
    yss = yss_ref[...]
    yg = _gelu(yss).astype(BF16)
    sv = _dot(yg, wv_ref[...])
    sig = _sigmoid(_dot(yg, wg_ref[...]))
    y_ssm = sv * sig
    gc = gt_ref[:, 0:D].astype(F32)
    gs = gt_ref[:, D:2 * D].astype(F32)
    m = gc * y_conv + gs * y_ssm
    return dict(rstd=rstd, zn=zn, z2=z2, sz=sz, z3=z3, y_conv=y_conv, yss=yss, yg=yg, sv=sv, sig=sig,
                y_ssm=y_ssm, gc=gc, gs=gs, m=m)


def _merge_fwd(h, z1, yss, gate, lng, lnb, wp, wv, wg, wo, name):
    L = h.shape[0]
    tm = _tile(L, 528)

    def body(h_ref, z1_ref, yss_ref, gt_ref, lng_ref, lnb_ref, wp_ref, wv_ref, wg_ref, wo_ref, o_ref):
        f = _branches(z1_ref, yss_ref, gt_ref, lng_ref, lnb_ref, wp_ref, wv_ref, wg_ref)
        o_ref[...] = h_ref[...] + _dot(f["m"].astype(BF16), wo_ref[...])

    def row(n):
        return pl.BlockSpec((tm, n), lambda i: (i, 0))

    return pl.pallas_call(
        body, name=name, grid=(L // tm,),
        out_shape=jax.ShapeDtypeStruct((L, D), F32),
        in_specs=[row(D), row(DC), row(DS), row(2 * D), _res((1, DC)), _res((1, DC)),
                  _res((DC, D)), _res((DS, D)), _res((DS, D)), _res((D, D))],
        out_specs=row(D),
        compiler_params=_params("parallel"),
    )(h, z1, yss, gate, lng, lnb, wp, wv, wg, wo)


def _merge_bwd(dh, z1, yss, gate, lng, lnb, wp, wv, wg, wo, name):
    L = dh.shape[0]
    tm = _tile(L, 352)

    def body(dh_ref, z1_ref, yss_ref, gt_ref, lng_ref, lnb_ref, wp_ref, wv_ref, wg_ref, wo_ref,
             m_ref, dgt_ref, dyc_ref, z3_ref, dz1_ref, yg_ref, dsv_ref, dsg_ref, dyss_ref,
             dbg_ref, dlng_ref, dlnb_ref):
        i = pl.program_id(0)
        f = _branches(z1_ref, yss_ref, gt_ref, lng_ref, lnb_ref, wp_ref, wv_ref, wg_ref)
        gc, gs, sig, sv = f["gc"], f["gs"], f["sig"], f["sv"]
        m_ref[...] = f["m"].astype(BF16)
        z3_ref[...] = f["z3"]
        yg_ref[...] = f["yg"]
        dm = _dot_nt(dh_ref[...].astype(BF16), wo_ref[...])
        dgc = (dm * f["y_conv"] * gc * (1.0 - gc)).astype(BF16)
        dgs = (dm * f["y_ssm"] * gs * (1.0 - gs)).astype(BF16)
        dgt_ref[:, 0:D] = dgc
        dgt_ref[:, D:2 * D] = dgs
        part = jnp.concatenate([jnp.sum(dgc.astype(F32), axis=0, keepdims=True),
                                jnp.sum(dgs.astype(F32), axis=0, keepdims=True)], axis=1)
        _acc_rows(dbg_ref, part, i == 0)
        dyc = (dm * gc).astype(BF16)
        dyc_ref[...] = dyc
        dys = dm * gs
        dsv = (dys * sig).astype(BF16)
        dsg = (dys * sv * sig * (1.0 - sig)).astype(BF16)
        dsv_ref[...] = dsv
        dsg_ref[...] = dsg
        dyg = _dot_nt(dsv, wv_ref[...]) + _dot_nt(dsg, wg_ref[...])
        dyss_ref[...] = dyg * _gelu_grad(f["yss"])
        dz3 = _dot_nt(dyc, wp_ref[...])
        z2, sz, zn = f["z2"], f["sz"], f["zn"]
        dz2 = dz3 * sz * (1.0 + z2 * (1.0 - sz))
        _acc_rows(dlng_ref, jnp.sum(dz2 * zn, axis=0, keepdims=True), i == 0)
        _acc_rows(dlnb_ref, jnp.sum(dz2, axis=0, keepdims=True), i == 0)
        dzn = dz2 * lng_ref[...]
        dz1_ref[...] = f["rstd"] * (dzn - jnp.mean(dzn, axis=-1, keepdims=True)
                                    - zn * jnp.mean(dzn * zn, axis=-1, keepdims=True))

    def row(n):
        return pl.BlockSpec((tm, n), lambda i: (i, 0))

    def tot(n):
        return pl.BlockSpec((1, n), lambda i: (0, 0))

    return pl.pallas_call(
        body, name=name, grid=(L // tm,),
        out_shape=[jax.ShapeDtypeStruct((L, D), BF16), jax.ShapeDtypeStruct((L, 2 * D), BF16),
                   jax.ShapeDtypeStruct((L, D), BF16), jax.ShapeDtypeStruct((L, DC), BF16),
                   jax.ShapeDtypeStruct((L, DC), F32), jax.ShapeDtypeStruct((L, DS), BF16),
                   jax.ShapeDtypeStruct((L, D), BF16), jax.ShapeDtypeStruct((L, D), BF16),
                   jax.ShapeDtypeStruct((L, DS), F32),
                   jax.ShapeDtypeStruct((1, 2 * D), F32), jax.ShapeDtypeStruct((1, DC), F32),
                   jax.ShapeDtypeStruct((1, DC), F32)],
        in_specs=[row(D), row(DC), row(DS), row(2 * D), _res((1, DC)), _res((1, DC)),
                  _res((DC, D)), _res((DS, D)), _res((DS, D)), _res((D, D))],
        out_specs=[row(D), row(2 * D), row(D), row(DC), row(DC), row(DS), row(D), row(D), row(DS),
                   tot(2 * D), tot(DC), tot(DC)],
        compiler_params=_params("arbitrary"),
    )(dh, z1, yss, gate, lng, lnb, wp, wv, wg, wo)


def _ssm_disc(lam_re, lam_im, log_dt, b_re, b_im):
    lam = lax.complex(lam_re, lam_im)
    dt = jnp.exp(log_dt)[:, None]
    lam_bar = jnp.exp(lam * dt)
    bbar = ((lam_bar - 1.0) / lam)[..., None] * lax.complex(b_re, b_im)
    return jnp.real(lam_bar), jnp.imag(lam_bar), jnp.real(bbar), jnp.imag(bbar)


def _bdiag_in(m):
    return jnp.einsum("gph,gk->ghkp", m, jnp.eye(G, dtype=m.dtype)).reshape(G * H, G * P)


def _bdiag_out(m):
    return jnp.einsum("ghp,gk->gpkh", m, jnp.eye(G, dtype=m.dtype)).reshape(G * P, G * H)


def _diag_blocks(m4):
    return jnp.einsum("qiaib->qiab", m4).reshape(G, m4.shape[2], m4.shape[4])


def _pack(parts, rows_mult=8):
    flat = jnp.concatenate([p.reshape(-1).astype(F32) for p in parts])
    n = flat.shape[0]
    tot = -(-n // (128 * rows_mult)) * (128 * rows_mult)
    return jnp.pad(flat, (0, tot - n)).reshape(tot // 128, 128)


def _unpack(buf, shapes):
    flat = buf.reshape(-1)
    out, o = [], 0
    for s in shapes:
        n = math.prod(s)
        out.append(flat[o:o + n].reshape(s))
        o += n
    return out


def kernel(x, meta_tokens, ffn1_norm, ffn1_w1, ffn1_w3, ffn1_w2, mix_norm, w_in, b_gate, conv_dw, conv_dw_b, conv_ln_g, conv_ln_b, conv_proj, ssm_lam_re, ssm_lam_im, ssm_log_dt, ssm_b_re, ssm_b_im, ssm_c_re, ssm_c_im, ssm_d, ssm_w_v, ssm_w_g, w_out, ffn2_norm, ffn2_w1, ffn2_w3, ffn2_w2, final_norm, loss_target, m_meta_tokens, m_ffn1_norm, m_ffn1_w1, m_ffn1_w3, m_ffn1_w2, m_mix_norm, m_w_in, m_b_gate, m_conv_dw, m_conv_dw_b, m_conv_ln_g, m_conv_ln_b, m_conv_proj, m_ssm_lam_re, m_ssm_lam_im, m_ssm_log_dt, m_ssm_b_re, m_ssm_b_im, m_ssm_c_re, m_ssm_c_im, m_ssm_d, m_ssm_w_v, m_ssm_w_g, m_w_out, m_ffn2_norm, m_ffn2_w1, m_ffn2_w3, m_ffn2_w2, m_final_norm, v_meta_tokens, v_ffn1_norm, v_ffn1_w1, v_ffn1_w3, v_ffn1_w2, v_mix_norm, v_w_in, v_b_gate, v_conv_dw, v_conv_dw_b, v_conv_ln_g, v_conv_ln_b, v_conv_proj, v_ssm_lam_re, v_ssm_lam_im, v_ssm_log_dt, v_ssm_b_re, v_ssm_b_im, v_ssm_c_re, v_ssm_c_im, v_ssm_d, v_ssm_w_v, v_ssm_w_g, v_w_out, v_ffn2_norm, v_ffn2_w1, v_ffn2_w3, v_ffn2_w2, v_final_norm):
    args = dict(locals())
    names = ["meta_tokens", "ffn1_norm", "ffn1_w1", "ffn1_w3", "ffn1_w2", "mix_norm", "w_in", "b_gate",
             "conv_dw", "conv_dw_b", "conv_ln_g", "conv_ln_b", "conv_proj", "ssm_lam_re", "ssm_lam_im",
             "ssm_log_dt", "ssm_b_re", "ssm_b_im", "ssm_c_re", "ssm_c_im", "ssm_d", "ssm_w_v", "ssm_w_g",
             "w_out", "ffn2_norm", "ffn2_w1", "ffn2_w3", "ffn2_w2", "final_norm"]
    big = ["ffn1_w1", "ffn1_w3", "ffn1_w2", "w_in", "conv_proj", "ssm_w_v", "ssm_w_g", "w_out",
           "ffn2_w1", "ffn2_w3", "ffn2_w2"]
    small = [n for n in names if n not in big]

    xs = x[0]
    S = xs.shape[0]
    L = FRONT + S
    T = L // NSEG
    jx, jy = lax.axis_index("x"), lax.axis_index("y")
    chip = 2 * jx + jy

    sm = _gather_all(_pack([meta_tokens, conv_dw[0]]), "gather_small")[0::2].reshape(NSH, -1)
    nmt = NMETA * (D // NSH)
    ndw = KW * (DC // NSH)
    meta_full = sm[:, :nmt].reshape(NSH, NMETA, D // NSH).transpose(1, 0, 2).reshape(NMETA, D)
    dw_full = sm[:, nmt:nmt + ndw].reshape(NSH, KW, DC // NSH).transpose(1, 0, 2).reshape(KW, DC)
    dw_pad = jnp.pad(dw_full, ((0, KWP - KW), (0, 0)))
    tposed = ("ffn1_w1", "ffn1_w3", "ffn2_w1", "ffn2_w3")

    def view(a, n):
        return jnp.swapaxes(a, 1, 2) if n in tposed else a

    grp_a = ["ffn1_w1", "ffn1_w3", "ffn1_w2"]
    grp_b = ["w_in", "conv_proj", "ssm_w_v", "ssm_w_g", "w_out"]
    grp_c = ["ffn2_w1", "ffn2_w3", "ffn2_w2"]

    def shard(n):
        return view(args[n], n)[0].astype(BF16)

    sh_a = [shard(n) for n in grp_a]
    ga_send, ga_recv, sh_a, land_a, ga_token = _chips_start(
        sh_a, [jax.ShapeDtypeStruct((NSH,) + s.shape, s.dtype) for s in sh_a], True, "gather_ffn1_start")

    def cols(w):
        return w.transpose(1, 0, 2).reshape(w.shape[1], -1)

    disc_in = (ssm_lam_re[0], ssm_lam_im[0], ssm_log_dt[0], ssm_b_re[0], ssm_b_im[0])
    (lbr, lbi, bbr, bbi), disc_vjp = jax.vjp(_ssm_disc, *disc_in)
    lam_t = jnp.exp(lax.complex(ssm_lam_re[0], ssm_lam_im[0]) * (jnp.exp(ssm_log_dt[0])[:, None] * T))
    lamp = jnp.concatenate([lbr.reshape(1, NST), lbi.reshape(1, NST), jnp.real(lam_t).reshape(1, NST),
                            jnp.imag(lam_t).reshape(1, NST), jnp.zeros((4, NST), F32)], axis=0)
    bre_bd, bim_bd = _bdiag_in(bbr).astype(BF16), _bdiag_in(bbi).astype(BF16)
    cre_bd, cim_bd = _bdiag_out(ssm_c_re[0]).astype(BF16), _bdiag_out(ssm_c_im[0]).astype(BF16)

    h0 = jnp.concatenate([jnp.zeros((FRONT - NMETA, D), F32), meta_full, xs + ga_token[0, 0]], axis=0)
    sh_a, land_a = _chips_wait(ga_send, ga_recv, sh_a, land_a, [h0], True, "gather_ffn1_wait")
    gw = dict(zip(grp_a, _pass_halves(land_a, "pass_ffn1", sh_a)))
    tgt = jnp.pad(loss_target[0], ((FRONT, 0), (0, 0)))
    (h1, a1, b1), got = _ffn_fwd(h0, ffn1_norm, gw["ffn1_w1"], gw["ffn1_w3"], gw["ffn1_w2"], "ffn1_fwd",
                                 _gather_half_behind([shard(n) for n in grp_b]))
    gw.update(zip(grp_b, _pass_halves(got, "pass_mix")))
    w_in_f = gw["w_in"]
    wp_f, wv_f, wg_f = cols(gw["conv_proj"]), cols(gw["ssm_w_v"]), cols(gw["ssm_w_g"])
    wo_f = gw["w_out"].reshape(D, D)
    (vg, uf, gate), got1 = _mix_in_fwd(h1, mix_norm, w_in_f, b_gate, "mix_in_fwd",
                                       _gather_half_behind([shard("ffn2_w1")]))
    (z1,), got3 = _conv_fwd(vg, dw_pad, conv_dw_b, "conv_fwd", _gather_half_behind([shard("ffn2_w3")]))
    (yss,), got2 = _ssm_fwd(uf, bre_bd, bim_bd, cre_bd, cim_bd, lamp, ssm_d, "ssm_fwd",
                            _gather_half_behind([shard("ffn2_w2")]))
    gw.update(zip(grp_c, _pass_halves([got1[0], got3[0], got2[0]], "pass_ffn2")))
    h2 = _merge_fwd(h1, z1, yss, gate, conv_ln_g, conv_ln_b, wp_f, wv_f, wg_f, wo_f, "merge_fwd")
    dh3, a2, b2, loss_part, d_final = _ffn_fwd_loss(
        h2, ffn2_norm, gw["ffn2_w1"], gw["ffn2_w3"], gw["ffn2_w2"], final_norm.reshape(1, D), tgt, "ffn2_fwd_loss")

    gbig = {}
    core = lax.axis_index("c").astype(jnp.int32).reshape(1)

    def pair_sums(group, tag):
        gl = [gbig[n] for n in group]
        sib = _pair_exchange(gl, "pair_exchange_" + tag)
        out = []
        for n, g_, s_ in zip(group, gl, sib):
            out.append(_add_pair(g_, s_, core, "pair_" + n))
        return out

    (dh2, da2, db2, s2, n2, d_ffn2_norm), _ = _ffn_bwd(
        h2, ffn2_norm, dh3, a2, b2, gw["ffn2_w1"], gw["ffn2_w3"], gw["ffn2_w2"], "ffn2_bwd")
    gbig["ffn2_w1"] = _wgrad(da2, n2, "ffn2_dw1")
    gbig["ffn2_w3"] = _wgrad(db2, n2, "ffn2_dw3")
    gbig["ffn2_w2"] = _wgrad(s2, dh3, "ffn2_dw2", 0.5)
    pair_c = pair_sums(grp_c, "ffn2")
    (m_b, dgate, dyc, z3, dz1, yg, dsv, dsg, dyss, d_b_gate, d_ln_g, d_ln_b) = _merge_bwd(
        dh2, z1, yss, gate, conv_ln_g, conv_ln_b, wp_f, wv_f, wg_f, wo_f, "merge_bwd")
    gbig["w_out"] = _wgrad(m_b, dh2, "dw_out").reshape(NSH, D // NSH, D)

    def shard_cols(gm):
        return gm.reshape(gm.shape[0], NSH, -1).transpose(1, 0, 2)

    gbig["conv_proj"] = shard_cols(_wgrad(z3, dyc, "dw_proj"))
    gbig["ssm_w_v"] = shard_cols(_wgrad(yg, dsv, "dw_v"))
    gbig["ssm_w_g"] = shard_cols(_wgrad(yg, dsg, "dw_g"))
    dv, dgl, ddw, d_dw_b = _conv_bwd(dz1, vg, dw_pad, "conv_bwd")
    (duf, dbre, dbim, dcre, dcim, dlam, d_ssm_d), recv_c = _ssm_bwd(
        uf, dyss, bre_bd, bim_bd, cre_bd, cim_bd, lamp, ssm_d, "ssm_bwd", _scatter_chips_behind(pair_c))
    dh1, u_b, dproj, d_mix_norm = _mix_in_bwd(h1, mix_norm, dh2, dv, dgl, duf, dgate, w_in_f, "mix_in_bwd")
    gbig["w_in"] = _wgrad(u_b, dproj, "dw_in")
    pair_b = pair_sums(grp_b, "mix")

    d_bbr = _diag_blocks(dbre.reshape(NQ, 8, H, 8, P)).transpose(0, 2, 1)
    d_bbi = _diag_blocks(dbim.reshape(NQ, 8, H, 8, P)).transpose(0, 2, 1)
    d_c_re = _diag_blocks(dcre.reshape(NQ, 8, P, 8, H)).transpose(0, 2, 1)
    d_c_im = _diag_blocks(dcim.reshape(NQ, 8, P, 8, H)).transpose(0, 2, 1)
    d_lbr = dlam[:, 0, :].reshape(G, P)
    d_lbi = dlam[:, 1, :].reshape(G, P)
    d_lam_re, d_lam_im, d_log_dt, d_b_re, d_b_im = disc_vjp((d_lbr, d_lbi, d_bbr, d_bbi))

    sg = {"mix_norm": d_mix_norm, "b_gate": d_b_gate, "conv_dw": ddw[:KW], "conv_dw_b": d_dw_b,
          "conv_ln_g": d_ln_g, "conv_ln_b": d_ln_b, "ssm_lam_re": d_lam_re, "ssm_lam_im": d_lam_im,
          "ssm_log_dt": d_log_dt, "ssm_b_re": d_b_re, "ssm_b_im": d_b_im, "ssm_c_re": d_c_re, "ssm_c_im": d_c_im,
          "ssm_d": d_ssm_d, "ffn2_norm": d_ffn2_norm, "final_norm": d_final}
    late = ["meta_tokens", "ffn1_norm"]
    early = [n for n in small if n not in late]

    (dh0, da1, db1, s1, n1, d_ffn1_norm), got = _ffn_bwd(
        h0, ffn1_norm, dh1, a1, b1, gw["ffn1_w1"], gw["ffn1_w3"], gw["ffn1_w2"], "ffn1_bwd",
        _join(_scatter_chips_behind(pair_b), _gather_all_behind(_pack([sg[n] for n in early]))))
    recv_b, early_all = got[:len(grp_b)], got[len(grp_b)]
    gbig["ffn1_w1"] = _wgrad(da1, n1, "ffn1_dw1")
    gbig["ffn1_w3"] = _wgrad(db1, n1, "ffn1_dw3")
    gbig["ffn1_w2"] = _wgrad(s1, dh1, "ffn1_dw2", 0.5)
    grad_x = dh0[FRONT:][None]
    sg["meta_tokens"] = dh0[FRONT - NMETA:FRONT]
    sg["ffn1_norm"] = d_ffn1_norm

    pair_a = pair_sums(grp_a, "ffn1")
    late_all = _gather_all(_pack([sg[n] for n in late]), "gather_late_grads")
    sa_send, sa_recv, pair_a, land_s, sa_token = _chips_start(
        pair_a, [jax.ShapeDtypeStruct(p.shape, p.dtype) for p in pair_a], False, "scatter_ffn1_start", [late_all])

    out_g, out_d, out_m, out_v = {}, {}, {}, {}

    def finish(group, recvs, tag, after=None):
        halves = [_sum_slots(r, "sum_" + n, after) for n, r in zip(group, recvs)]
        for n, f in zip(group, _swap_halves(halves, "swap_" + tag)):
            g3 = f.reshape(1, f.shape[0] * f.shape[1], f.shape[2])
            d3, m3, v3 = _adamw(view(args[n], n), g3, view(args["m_" + n], n), view(args["v_" + n], n),
                                "adamw_" + n)
            out_g[n], out_d[n], out_m[n], out_v[n] = (view(t, n) for t in (g3, d3, m3, v3))
            done.append(d3)

    done = []
    finish(grp_b + grp_c, list(recv_b) + list(recv_c), "mix_ffn2", sa_token)

    sgr = dict(zip(early, _unpack(_sum_slots(early_all, "sum_early", sa_token), [sg[n].shape for n in early])))
    sgr.update(zip(late, _unpack(_sum_slots(late_all, "sum_late"), [sg[n].shape for n in late])))
    sgr["meta_tokens"] = lax.dynamic_slice_in_dim(sgr["meta_tokens"], chip * (D // NSH), D // NSH, axis=1)
    sgr["conv_dw"] = lax.dynamic_slice_in_dim(sgr["conv_dw"], chip * (DC // NSH), DC // NSH, axis=1)
    pshapes = [args[n].shape for n in small]
    d_s, m_s, v_s = _adamw(_pack([args[n] for n in small])[None], _pack([sgr[n] for n in small])[None],
                           _pack([args["m_" + n] for n in small])[None],
                           _pack([args["v_" + n] for n in small])[None], "adamw_small")
    for n, g_, d_, m_, v_ in zip(small, [sgr[n] for n in small], _unpack(d_s[0], pshapes),
                                 _unpack(m_s[0], pshapes), _unpack(v_s[0], pshapes)):
        out_g[n], out_d[n], out_m[n], out_v[n] = g_.reshape(args[n].shape), d_, m_, v_

    pair_a, recv_a = _chips_wait(sa_send, sa_recv, pair_a, land_s, [d_s] + done, False, "scatter_ffn1_wait")
    finish(grp_a, _fill_own(pair_a, recv_a, "own_ffn1"), "ffn1")

    loss = lax.psum(loss_part[0, 0], ("x", "y", "c"))
    return (loss, grad_x, *[out_g[n] for n in names], *[out_d[n] for n in names],
            *[out_m[n] for n in names], *[out_v[n] for n in names])
```

```python
import math

import jax
import jax.numpy as jnp
from jax import lax
from jax.experimental import pallas as pl
from jax.experimental.pallas import tpu as pltpu

F32 = jnp.float32
BF16 = jnp.bfloat16

D = 1024
NSH = 4
F = 2816
FS = F // NSH
DC = 512
DS = 512
DIN = 2 * DC + DS + 2 * D
WS = DIN // NSH
KW = 31
KWP = 32
CONV_ROWS = 64
NMETA = 16
FRONT = 128
G, P, H = 32, 64, 16
NST = G * P
NQ = 4
QS = NST // NQ
QU = DS // NQ
NSEG = 32
NGRP = NSEG // 8
NCH = 8
SOFF = 8
EPS = 1e-6
LR, B1, B2, AEPS, WD, STEP = 1e-3, 0.9, 0.999, 1e-8, 0.01, 10
VMEM_LIMIT = 58 * 1024 * 1024
MESH = pl.DeviceIdType.MESH
ANY = pl.BlockSpec(memory_space=pl.ANY)


def _params(*sem):
    return pltpu.CompilerParams(dimension_semantics=sem, vmem_limit_bytes=VMEM_LIMIT)


def _res(shape):
    nd = len(shape)
    return pl.BlockSpec(shape, lambda *_: (0,) * nd, pipeline_mode=pl.Buffered(1))


def _tile(n, cap, mult=16):
    best = None
    for t in range(mult, min(n, cap) + 1, mult):
        if n % t == 0:
            best = t
    assert best is not None, (n, cap, mult)
    return best


def _dot(a, b):
    return jnp.dot(a, b, preferred_element_type=F32)


def _dot_nt(a, b):
    return lax.dot_general(a, b, (((1,), (1,)), ((), ())), preferred_element_type=F32)


def _dot_tn(a, b):
    return lax.dot_general(a, b, (((0,), (0,)), ((), ())), preferred_element_type=F32)


def _sigmoid(x):
    return 1.0 / (1.0 + jnp.exp(-x))


_GC = math.sqrt(2.0 / math.pi)
_GA = 0.044715


def _gelu(x):
    return 0.5 * x * (1.0 + jnp.tanh(_GC * (x + _GA * x * x * x)))


def _gelu_grad(x):
    t = jnp.tanh(_GC * (x + _GA * x * x * x))
    return 0.5 * (1.0 + t) + 0.5 * x * (1.0 - t * t) * _GC * (1.0 + 3.0 * _GA * x * x)


def _rms(hv, g):
    r = lax.rsqrt(jnp.mean(hv * hv, axis=-1, keepdims=True) + EPS)
    return hv * r * g, r


def _rms_bwd(dn, hv, r, g):
    xh = hv * r
    dxh = dn * g
    return r * (dxh - xh * jnp.mean(dxh * xh, axis=-1, keepdims=True)), xh


def _acc_rows(ref, part, first):
    @pl.when(first)
    def _():
        ref[...] = part

    @pl.when(jnp.logical_not(first))
    def _():
        ref[...] += part


def _coords():
    return lax.axis_index("x"), lax.axis_index("y"), lax.axis_index("c")


def _flip(v, d):
    return 1 - v if d else v


def _run(local, remote):
    for cp in local + remote:
        cp.start()
    for cp in remote:
        cp.wait()
    for cp in local:
        cp.wait()


def _via_vmem(src, dst, stage, sems, i):
    return (pltpu.make_async_copy(src, stage, sems.at[2 * i]), pltpu.make_async_copy(stage, dst, sems.at[2 * i + 1]))


def _run_staged(staged, remote):
    for load, _ in staged:
        load.start()
    for cp in remote:
        cp.start()
    for load, store in staged:
        load.wait()
        store.start()
    for cp in remote:
        cp.wait()
    for _, store in staged:
        store.wait()


_REL3 = ((1, 0), (0, 1), (1, 1))


class _Behind:
    def __init__(self, arrays, out_shapes, scratch, build):
        self.arrays, self.out_shapes, self.scratch, self.build = list(arrays), list(out_shapes), list(scratch), build

    def start(self, ins, outs, scr):
        staged, remote = self.build(ins, outs, scr)
        for load, _ in staged:
            load.start()
        for cp in remote:
            cp.start()

    def finish(self, ins, outs, scr):
        staged, remote = self.build(ins, outs, scr)
        for load, store in staged:
            load.wait()
            store.start()
        for cp in remote:
            cp.wait()
        for _, store in staged:
            store.wait()


def _call(body, comm, *, name, grid, in_specs, out_specs, out_shape, scratch_shapes=(), params):
    in_specs, out_specs, out_shape = list(in_specs), list(out_specs), list(out_shape)
    scratch_shapes = list(scratch_shapes)
    if comm is None:
        f = pl.pallas_call(body, name=name, grid=grid, in_specs=in_specs, out_specs=out_specs,
                           out_shape=out_shape, scratch_shapes=scratch_shapes, compiler_params=params)
        return lambda *args: (f(*args), [])
    ni, no, ns = len(in_specs), len(out_specs), len(scratch_shapes)
    ci, co = len(comm.arrays), len(comm.out_shapes)

    def hosted(*refs):
        ins, cin = refs[:ni], refs[ni:ni + ci]
        outs, cout = refs[ni + ci:ni + ci + no], refs[ni + ci + no:ni + ci + no + co]
        scr, cscr = refs[ni + ci + no + co:ni + ci + no + co + ns], refs[ni + ci + no + co + ns:]
        first = last = None
        for axis, size in enumerate(grid):
            i = pl.program_id(axis)
            first = (i == 0) if first is None else jnp.logical_and(first, i == 0)
            last = (i == size - 1) if last is None else jnp.logical_and(last, i == size - 1)

        @pl.when(first)
        def _():
            comm.start(cin, cout, cscr)

        body(*ins, *outs, *scr)

        @pl.when(last)
        def _():
            comm.finish(cin, cout, cscr)

    f = pl.pallas_call(hosted, name=name, grid=grid, in_specs=in_specs + [ANY] * ci,
                       out_specs=out_specs + [ANY] * co, out_shape=out_shape + comm.out_shapes,
                       scratch_shapes=scratch_shapes + comm.scratch,
                       compiler_params=_params(*(("arbitrary",) * len(grid))))

    def run(*args):
        res = f(*args, *comm.arrays)
        return res[:no], res[no:]

    return run


def _gather_half_behind(shards):
    n = len(shards)

    def build(ins, outs, scr):
        send, recv, loc = scr[:3]
        stage = scr[3:]
        x, y, c = _coords()
        me = 2 * x + y
        staged = [_via_vmem(ins[t], outs[t].at[me], stage[t], loc, t) for t in range(n)]
        remote = []
        for t in range(n):
            half = shards[t].shape[0] // 2
            mine = pl.ds(c * half, half)
            for k, (dx, dy) in enumerate(_REL3):
                remote.append(pltpu.make_async_remote_copy(
                    src_ref=ins[t].at[mine], dst_ref=outs[t].at[me, mine],
                    send_sem=send.at[3 * t + k], recv_sem=recv.at[3 * t + k],
                    device_id=(_flip(x, dx), _flip(y, dy), c), device_id_type=MESH))
        return staged, remote

    return _Behind(shards, [jax.ShapeDtypeStruct((NSH,) + s.shape, s.dtype) for s in shards],
                   [pltpu.SemaphoreType.DMA((3 * n,)), pltpu.SemaphoreType.DMA((3 * n,)),
                    pltpu.SemaphoreType.DMA((2 * n,))] + [pltpu.VMEM(s.shape, s.dtype) for s in shards], build)


def _pass_halves(gathered, name, own=()):
    n, m = len(gathered), len(own)

    def body(*refs):
        shards, outs = refs[n:n + m], refs[n + m:2 * n + m]
        send, recv, loc = refs[2 * n + m:2 * n + m + 3]
        stage = refs[2 * n + m + 3:]
        x, y, c = _coords()
        staged = [_via_vmem(shards[t], outs[t].at[2 * x + y], stage[t], loc, t) for t in range(m)]
        remote = []
        for t in range(n):
            half = gathered[t].shape[1] // 2
            mine = pl.ds(c * half, half)
            for k, (dx, dy) in enumerate(_REL3):
                slot = 2 * _flip(x, dx) + _flip(y, dy)
                remote.append(pltpu.make_async_remote_copy(
                    src_ref=outs[t].at[slot, mine], dst_ref=outs[t].at[slot, mine],
                    send_sem=send.at[3 * t + k], recv_sem=recv.at[3 * t + k],
                    device_id=(x, y, 1 - c), device_id_type=MESH))
        _run_staged(staged, remote)

    return pl.pallas_call(
        body, name=name,
        out_shape=[jax.ShapeDtypeStruct(g.shape, g.dtype) for g in gathered],
        in_specs=[ANY] * (n + m), out_specs=[ANY] * n, input_output_aliases={t: t for t in range(n)},
        scratch_shapes=[pltpu.SemaphoreType.DMA((3 * n,)), pltpu.SemaphoreType.DMA((3 * n,)),
                        pltpu.SemaphoreType.DMA((max(2 * m, 1),))] + [pltpu.VMEM(s.shape, s.dtype) for s in own],
        compiler_params=pltpu.CompilerParams(vmem_limit_bytes=VMEM_LIMIT),
    )(*gathered, *own)


def _fill_own(sums, recvs, name):
    n = len(sums)

    def body(*refs):
        ins, outs = refs[:n], refs[2 * n:3 * n]
        loc = refs[3 * n]
        stage = refs[3 * n + 1:]
        x, y, _ = _coords()
        me = 2 * x + y
        _run_staged([_via_vmem(ins[t].at[me], outs[t].at[me], stage[t], loc, t) for t in range(n)], [])

    return pl.pallas_call(
        body, name=name,
        out_shape=[jax.ShapeDtypeStruct(r.shape, r.dtype) for r in recvs],
        in_specs=[ANY] * (2 * n), out_specs=[ANY] * n, input_output_aliases={n + t: t for t in range(n)},
        scratch_shapes=[pltpu.SemaphoreType.DMA((2 * n,))] + [pltpu.VMEM(s.shape[1:], s.dtype) for s in sums],
        compiler_params=pltpu.CompilerParams(vmem_limit_bytes=VMEM_LIMIT),
    )(*sums, *recvs)


def _scatter_chips_behind(sums):
    n = len(sums)

    def build(ins, outs, scr):
        send, recv, loc = scr[:3]
        stage = scr[3:]
        x, y, c = _coords()
        me = 2 * x + y
        staged = [_via_vmem(ins[t].at[me], outs[t].at[me], stage[t], loc, t) for t in range(n)]
        remote = []
        for t in range(n):
            for k, (dx, dy) in enumerate(_REL3):
                px, py = _flip(x, dx), _flip(y, dy)
                remote.append(pltpu.make_async_remote_copy(
                    src_ref=ins[t].at[2 * px + py], dst_ref=outs[t].at[me],
                    send_sem=send.at[3 * t + k], recv_sem=recv.at[3 * t + k],
                    device_id=(px, py, c), device_id_type=MESH))
        return staged, remote

    return _Behind(sums, [jax.ShapeDtypeStruct(s.shape, s.dtype) for s in sums],
                   [pltpu.SemaphoreType.DMA((3 * n,)), pltpu.SemaphoreType.DMA((3 * n,)),
                    pltpu.SemaphoreType.DMA((2 * n,))] + [pltpu.VMEM(s.shape[1:], s.dtype) for s in sums], build)


def _gather_all_behind(a):
    def build(ins, outs, scr):
        send, recv, loc, stage = scr
        x, y, c = _coords()
        me = 4 * x + 2 * y + c
        staged = [_via_vmem(ins[0], outs[0].at[me], stage, loc, 0)]
        remote = [pltpu.make_async_remote_copy(
            src_ref=ins[0], dst_ref=outs[0].at[me], send_sem=send.at[k], recv_sem=recv.at[k],
            device_id=(_flip(x, dx), _flip(y, dy), _flip(c, dc)), device_id_type=MESH)
            for k, (dx, dy, dc) in enumerate(_REL7)]
        return staged, remote

    return _Behind([a], [jax.ShapeDtypeStruct((8,) + a.shape, a.dtype)],
                   [pltpu.SemaphoreType.DMA((7,)), pltpu.SemaphoreType.DMA((7,)), pltpu.SemaphoreType.DMA((2,)),
                    pltpu.VMEM(a.shape, a.dtype)], build)


HBM = pl.BlockSpec(memory_space=pltpu.HBM)
SEM = pl.BlockSpec(memory_space=pltpu.SEMAPHORE)
EFFECT = pltpu.SideEffectType.DATAFLOW_SIDE_EFFECTING


def _chip_copies(srcs, lands, send, recv, gather):
    x, y, c = _coords()
    me = 2 * x + y
    cps = []
    for t in range(len(srcs)):
        for k, (dx, dy) in enumerate(_REL3):
            px, py = _flip(x, dx), _flip(y, dy)
            if gather:
                half = srcs[t].shape[0] // 2
                mine = pl.ds(c * half, half)
                src, dst = srcs[t].at[mine], lands[t].at[me, mine]
            else:
                src, dst = srcs[t].at[2 * px + py], lands[t].at[me]
            cps.append(pltpu.make_async_remote_copy(
                src_ref=src, dst_ref=dst, send_sem=send.at[3 * t + k], recv_sem=recv.at[3 * t + k],
                device_id=(px, py, c), device_id_type=MESH))
    return cps


def _chips_start(arrays, land_shapes, gather, name, after=()):
    n = len(arrays)

    def body(*refs):
        srcs, lands = refs[:n], refs[n:2 * n]
        send, recv = refs[2 * n + len(after)], refs[2 * n + len(after) + 1]
        token = refs[-1]
        for cp in _chip_copies(srcs, lands, send, recv, gather):
            cp.start()
        token[...] = jnp.zeros_like(token)

    lands = [lax.empty(s.shape, s.dtype) for s in land_shapes]
    thru = [pltpu.HBM(a.shape, a.dtype) for a in arrays] + [pltpu.HBM(s.shape, s.dtype) for s in land_shapes]
    res = pl.pallas_call(
        body, name=name,
        out_shape=(pltpu.SemaphoreType.DMA((3 * n,)), pltpu.SemaphoreType.DMA((3 * n,)), *thru,
                   jax.ShapeDtypeStruct((8, 128), F32)),
        in_specs=[HBM] * (2 * n) + [ANY] * len(after),
        out_specs=(SEM, SEM, *([HBM] * (2 * n)), pl.BlockSpec(memory_space=pltpu.VMEM)),
        input_output_aliases={t: 2 + t for t in range(2 * n)},
        compiler_params=pltpu.CompilerParams(has_side_effects=EFFECT),
    )(*[pltpu.with_memory_space_constraint(a, pltpu.HBM) for a in arrays],
      *[pltpu.with_memory_space_constraint(z, pltpu.HBM) for z in lands], *after)
    return res[0], res[1], list(res[2:2 + n]), list(res[2 + n:2 + 2 * n]), res[-1]


def _chips_wait(send, recv, arrays, lands, after, gather, name):
    n = len(arrays)

    def body(*refs):
        srcs, ls = refs[:n], refs[n:2 * n]
        sd, rv = refs[2 * n], refs[2 * n + 1]
        for cp in _chip_copies(srcs, ls, sd, rv, gather):
            cp.wait_send()
            cp.wait_recv()

    res = pl.pallas_call(
        body, name=name,
        out_shape=[pltpu.HBM(a.shape, a.dtype) for a in arrays] + [pltpu.HBM(z.shape, z.dtype) for z in lands],
        in_specs=[HBM] * (2 * n) + [SEM, SEM] + [ANY] * len(after), out_specs=[HBM] * (2 * n),
        input_output_aliases={t: t for t in range(2 * n)},
        compiler_params=pltpu.CompilerParams(has_side_effects=EFFECT),
    )(*arrays, *lands, send, recv, *after)
    return list(res[:n]), list(res[n:])


def _join(*parts):
    def cut(seq, key):
        res, o = [], 0
        for p in parts:
            k = len(getattr(p, key))
            res.append(seq[o:o + k])
            o += k
        return res

    def build(ins, outs, scr):
        staged, remote = [], []
        for p, i, o, s in zip(parts, cut(ins, "arrays"), cut(outs, "out_shapes"), cut(scr, "scratch")):
            st, rm = p.build(i, o, s)
            staged += st
            remote += rm
        return staged, remote

    return _Behind(sum((p.arrays for p in parts), []), sum((p.out_shapes for p in parts), []),
                   sum((p.scratch for p in parts), []), build)


def _gather_chips(shards, name):
    n = len(shards)

    def body(*refs):
        ins, outs = refs[:n], refs[n:2 * n]
        send, recv, fsend, frecv, loc = refs[2 * n:2 * n + 5]
        stage = refs[2 * n + 5:]
        x, y, c = _coords()
        me = 2 * x + y
        own = [_via_vmem(ins[t], outs[t].at[me], stage[t], loc, t) for t in range(n)]
        first, passed = [], []
        for t in range(n):
            half = shards[t].shape[0] // 2
            mine, theirs = pl.ds(c * half, half), pl.ds((1 - c) * half, half)
            for k, (dx, dy) in enumerate(_REL3):
                px, py = _flip(x, dx), _flip(y, dy)
                first.append(pltpu.make_async_remote_copy(
                    src_ref=ins[t].at[mine], dst_ref=outs[t].at[me, mine],
                    send_sem=send.at[3 * t + k], recv_sem=recv.at[3 * t + k],
                    device_id=(px, py, c), device_id_type=MESH))
                passed.append((
                    pltpu.make_async_remote_copy(
                        src_ref=outs[t].at[2 * px + py, mine], dst_ref=outs[t].at[2 * px + py, mine],
                        send_sem=fsend.at[3 * t + k], recv_sem=frecv.at[3 * t + k],
                        device_id=(x, y, 1 - c), device_id_type=MESH),
                    pltpu.make_async_remote_copy(
                        src_ref=outs[t].at[2 * px + py, theirs], dst_ref=outs[t].at[2 * px + py, theirs],
                        send_sem=fsend.at[3 * t + k], recv_sem=frecv.at[3 * t + k],
                        device_id=(x, y, 1 - c), device_id_type=MESH)))
        for load, _ in own:
            load.start()
        for cp in first:
            cp.start()
        for load, store in own:
            load.wait()
            store.start()
        for cp, (fwd, _) in zip(first, passed):
            cp.wait_recv()
            fwd.start()
        for cp, (fwd, back) in zip(first, passed):
            cp.wait_send()
            fwd.wait_send()
            back.wait_recv()
        for _, store in own:
            store.wait()

    return pl.pallas_call(
        body, name=name,
        out_shape=[jax.ShapeDtypeStruct((NSH,) + s.shape, s.dtype) for s in shards],
        in_specs=[ANY] * n, out_specs=[ANY] * n,
        scratch_shapes=[pltpu.SemaphoreType.DMA((3 * n,)) for _ in range(4)] + [pltpu.SemaphoreType.DMA((2 * n,))]
        + [pltpu.VMEM(s.shape, s.dtype) for s in shards],
        compiler_params=pltpu.CompilerParams(vmem_limit_bytes=VMEM_LIMIT),
    )(*shards)


_REL7 = tuple((dx, dy, dc) for dx in (0, 1) for dy in (0, 1) for dc in (0, 1))[1:]


def _gather_all(a, name):
    def body(a_ref, o_ref, send, recv, loc):
        x, y, c = _coords()
        me = 4 * x + 2 * y + c
        local = [pltpu.make_async_copy(a_ref, o_ref.at[me], loc.at[0])]
        remote = [pltpu.make_async_remote_copy(
            src_ref=a_ref, dst_ref=o_ref.at[me], send_sem=send.at[k], recv_sem=recv.at[k],
            device_id=(_flip(x, dx), _flip(y, dy), _flip(c, dc)), device_id_type=MESH)
            for k, (dx, dy, dc) in enumerate(_REL7)]
        _run(local, remote)

    return pl.pallas_call(
        body, name=name,
        out_shape=jax.ShapeDtypeStruct((8,) + a.shape, a.dtype),
        in_specs=[ANY], out_specs=ANY,
        scratch_shapes=[pltpu.SemaphoreType.DMA((7,)), pltpu.SemaphoreType.DMA((7,)),
                        pltpu.SemaphoreType.DMA((1,))],
    )(a)


def _pair_exchange(grads, name):
    n = len(grads)

    def body(*refs):
        ins, outs = refs[:n], refs[n:2 * n]
        send, recv = refs[2 * n:]
        x, y, c = _coords()
        remote = []
        for t in range(n):
            half = grads[t].shape[1] // 2
            remote.append(pltpu.make_async_remote_copy(
                src_ref=ins[t].at[:, pl.ds((1 - c) * half, half)], dst_ref=outs[t],
                send_sem=send.at[t], recv_sem=recv.at[t],
                device_id=(x, y, 1 - c), device_id_type=MESH))
        _run([], remote)

    return pl.pallas_call(
        body, name=name,
        out_shape=[jax.ShapeDtypeStruct((NSH, g.shape[1] // 2, g.shape[2]), g.dtype) for g in grads],
        in_specs=[ANY] * n, out_specs=[ANY] * n,
        scratch_shapes=[pltpu.SemaphoreType.DMA((n,)), pltpu.SemaphoreType.DMA((n,))],
    )(*grads)


def _scatter_chips(sums, name):
    n = len(sums)

    def body(*refs):
        ins, outs = refs[:n], refs[n:2 * n]
        send, recv, loc = refs[2 * n:2 * n + 3]
        stage = refs[2 * n + 3:]
        x, y, c = _coords()
        me = 2 * x + y
        local = [_via_vmem(ins[t].at[me], outs[t].at[me], stage[t], loc, t) for t in range(n)]
        remote = []
        for t in range(n):
            for k, (dx, dy) in enumerate(_REL3):
                px, py = _flip(x, dx), _flip(y, dy)
                remote.append(pltpu.make_async_remote_copy(
                    src_ref=ins[t].at[2 * px + py], dst_ref=outs[t].at[me],
                    send_sem=send.at[3 * t + k], recv_sem=recv.at[3 * t + k],
                    device_id=(px, py, c), device_id_type=MESH))
        _run_staged(local, remote)

    return pl.pallas_call(
        body, name=name,
        out_shape=[jax.ShapeDtypeStruct(s.shape, s.dtype) for s in sums],
        in_specs=[ANY] * n, out_specs=[ANY] * n,
        scratch_shapes=[pltpu.SemaphoreType.DMA((3 * n,)), pltpu.SemaphoreType.DMA((3 * n,)),
                        pltpu.SemaphoreType.DMA((2 * n,))]
        + [pltpu.VMEM(s.shape[1:], s.dtype) for s in sums],
        compiler_params=pltpu.CompilerParams(vmem_limit_bytes=VMEM_LIMIT),
    )(*sums)


def _swap_halves(halves, name):
    n = len(halves)

    def body(*refs):
        ins, outs = refs[:n], refs[n:2 * n]
        send, recv, loc = refs[2 * n:2 * n + 3]
        stage = refs[2 * n + 3:]
        x, y, c = _coords()
        local = [_via_vmem(ins[t], outs[t].at[c], stage[t], loc, t) for t in range(n)]
        remote = [pltpu.make_async_remote_copy(
            src_ref=ins[t], dst_ref=outs[t].at[c], send_sem=send.at[t], recv_sem=recv.at[t],
            device_id=(x, y, 1 - c), device_id_type=MESH) for t in range(n)]
        _run_staged(local, remote)

    return pl.pallas_call(
        body, name=name,
        out_shape=[jax.ShapeDtypeStruct((2,) + h.shape, h.dtype) for h in halves],
        in_specs=[ANY] * n, out_specs=[ANY] * n,
        scratch_shapes=[pltpu.SemaphoreType.DMA((n,)), pltpu.SemaphoreType.DMA((n,)),
                        pltpu.SemaphoreType.DMA((2 * n,))]
        + [pltpu.VMEM(h.shape, h.dtype) for h in halves],
        compiler_params=pltpu.CompilerParams(vmem_limit_bytes=VMEM_LIMIT),
    )(*halves)


def _sum_slots(r, name, after=None):
    K, R, C = r.shape
    tr = _tile(R, max(16, (1 << 22) // (K * C)), 8 * (4 // r.dtype.itemsize))

    def body(r_ref, *rest):
        o_ref = rest[-1]
        acc = r_ref[0].astype(F32)
        for k in range(1, K):
            acc = acc + r_ref[k].astype(F32)
        o_ref[...] = acc

    dep = [] if after is None else [after]
    return pl.pallas_call(
        body, name=name, grid=(R // tr,),
        out_shape=jax.ShapeDtypeStruct((R, C), F32),
        in_specs=[pl.BlockSpec((K, tr, C), lambda i: (0, i, 0))] + [ANY] * len(dep),
        out_specs=pl.BlockSpec((tr, C), lambda i: (i, 0)),
        compiler_params=_params("parallel"),
    )(r, *dep)


def _add_pair(g, s, core, name):
    _, half, C = s.shape
    tr = _tile(half, max(16, (1 << 19) // C))
    nb = half // tr

    def body(c_ref, g_ref, s_ref, o_ref):
        o_ref[...] = (g_ref[...].astype(F32) + s_ref[...].astype(F32)).astype(BF16)

    spec = pl.BlockSpec((1, tr, C), lambda j, i, c_ref: (j, i, 0))
    return pl.pallas_call(
        body, name=name,
        grid_spec=pltpu.PrefetchScalarGridSpec(
            num_scalar_prefetch=1, grid=(NSH, nb),
            in_specs=[pl.BlockSpec((1, tr, C), lambda j, i, c_ref: (j, c_ref[0] * nb + i, 0)), spec],
            out_specs=spec),
        out_shape=jax.ShapeDtypeStruct(s.shape, BF16),
        compiler_params=_params("parallel", "parallel"),
    )(core, g, s)


def _adamw(w, g, m, v, name):
    _, R, C = w.shape
    tr = _tile(R, max(8, (1 << 18) // C), 8)
    c1 = 1.0 / (1.0 - B1 ** STEP)
    c2 = 1.0 / (1.0 - B2 ** STEP)

    def body(w_ref, g_ref, m_ref, v_ref, go_ref, d_ref, nm_ref, nv_ref):
        gv = g_ref[...]
        go_ref[...] = gv
        nm = B1 * m_ref[...] + (1.0 - B1) * gv
        nv = B2 * v_ref[...] + (1.0 - B2) * gv * gv
        nm_ref[...] = nm
        nv_ref[...] = nv
        d_ref[...] = -LR * ((nm * c1) / (jnp.sqrt(nv * c2) + AEPS) + WD * w_ref[...])

    spec = pl.BlockSpec((1, tr, C), lambda i: (0, i, 0))
    return pl.pallas_call(
        body, name=name, grid=(R // tr,),
        out_shape=[jax.ShapeDtypeStruct((1, R, C), F32)] * 4,
        in_specs=[spec] * 4, out_specs=[spec] * 4,
        compiler_params=_params("parallel"),
    )(w, g, m, v)


def _ffn_fwd(h, g, w1, w3, w2, name, comm=None):
    L = h.shape[0]
    tm = _tile(L, 704)

    def body(h_ref, g_ref, w1_ref, w3_ref, w2_ref, o_ref, a_ref, b_ref, n_s, acc_s):
        j = pl.program_id(1)

        @pl.when(j == 0)
        def _():
            hv = h_ref[...]
            n, _ = _rms(hv, g_ref[...])
            n_s[...] = n.astype(BF16)
            acc_s[...] = hv

        n = n_s[...]
        a = _dot_nt(n, w1_ref[0])
        b = _dot_nt(n, w3_ref[0])
        a_ref[0] = a.astype(BF16)
        b_ref[0] = b.astype(BF16)
        s = (a * _sigmoid(a) * b).astype(BF16)
        acc_s[...] += 0.5 * _dot(s, w2_ref[0])

        @pl.when(j == NSH - 1)
        def _():
            o_ref[...] = acc_s[...]

    row = pl.BlockSpec((tm, D), lambda i, j: (i, 0))
    hid = pl.BlockSpec((1, tm, FS), lambda i, j: (j, i, 0))
    wsp = pl.BlockSpec((1, FS, D), lambda i, j: (j, 0, 0))
    return _call(
        body, comm, name=name, grid=(L // tm, NSH),
        out_shape=[jax.ShapeDtypeStruct((L, D), F32),
                   jax.ShapeDtypeStruct((NSH, L, FS), BF16), jax.ShapeDtypeStruct((NSH, L, FS), BF16)],
        in_specs=[row, _res((1, D)), wsp, wsp, wsp],
        out_specs=[row, hid, hid],
        scratch_shapes=[pltpu.VMEM((tm, D), BF16), pltpu.VMEM((tm, D), F32)],
        params=_params("arbitrary", "arbitrary"),
    )(h, g, w1, w3, w2)


def _loss_head(hv, gv, tv, row0):
    y, r = _rms(hv, gv)
    row = row0 + lax.broadcasted_iota(jnp.int32, (hv.shape[0], 1), 0)
    e = jnp.where(row >= FRONT, y - tv, 0.0)
    dy = e * (1.0 / D)
    part = 0.5 * jnp.sum(jnp.sum(e * dy, axis=1, keepdims=True), axis=0, keepdims=True)
    dx, xh = _rms_bwd(dy, hv, r, gv)
    return dx, part, jnp.sum(dy * xh, axis=0, keepdims=True)


def _ffn_fwd_loss(h, g, w1, w3, w2, gf, tgt, name):
    L = h.shape[0]
    tm = _tile(L, 704)

    def body(h_ref, g_ref, w1_ref, w3_ref, w2_ref, gf_ref, t_ref, o_ref, a_ref, b_ref, loss_ref, dgf_ref,
             n_s, acc_s):
        i, j = pl.program_id(0), pl.program_id(1)

        @pl.when(j == 0)
        def _():
            hv = h_ref[...]
            n, _ = _rms(hv, g_ref[...])
            n_s[...] = n.astype(BF16)
            acc_s[...] = hv

        n = n_s[...]
        a = _dot_nt(n, w1_ref[0])
        b = _dot_nt(n, w3_ref[0])
        a_ref[0] = a.astype(BF16)
        b_ref[0] = b.astype(BF16)
        s = (a * _sigmoid(a) * b).astype(BF16)
        acc_s[...] += 0.5 * _dot(s, w2_ref[0])

        @pl.when(j == NSH - 1)
        def _():
            dx, part, dgf = _loss_head(acc_s[...], gf_ref[...], t_ref[...], i * tm)
            o_ref[...] = dx
            _acc_rows(loss_ref, part, i == 0)
            _acc_rows(dgf_ref, dgf, i == 0)

    row = pl.BlockSpec((tm, D), lambda i, j: (i, 0))
    hid = pl.BlockSpec((1, tm, FS), lambda i, j: (j, i, 0))
    wsp = pl.BlockSpec((1, FS, D), lambda i, j: (j, 0, 0))
    return pl.pallas_call(
        body, name=name, grid=(L // tm, NSH),
        out_shape=[jax.ShapeDtypeStruct((L, D), F32),
                   jax.ShapeDtypeStruct((NSH, L, FS), BF16), jax.ShapeDtypeStruct((NSH, L, FS), BF16),
                   jax.ShapeDtypeStruct((1, 1), F32), jax.ShapeDtypeStruct((1, D), F32)],
        in_specs=[row, _res((1, D)), wsp, wsp, wsp, _res((1, D)), row],
        out_specs=[row, hid, hid, pl.BlockSpec((1, 1), lambda i, j: (0, 0)),
                   pl.BlockSpec((1, D), lambda i, j: (0, 0))],
        scratch_shapes=[pltpu.VMEM((tm, D), BF16), pltpu.VMEM((tm, D), F32)],
        compiler_params=_params("arbitrary", "arbitrary"),
    )(h, g, w1, w3, w2, gf, tgt)


def _ffn_bwd(h, g, dout, a, b, w1, w3, w2, name, comm=None):
    L = h.shape[0]
    tm = _tile(L, 528)

    def body(h_ref, g_ref, do_ref, a_ref, b_ref, w1_ref, w3_ref, w2_ref,
             dh_ref, da_ref, db_ref, s_ref, n_ref, dg_ref, dob_s, dn_s):
        i, j = pl.program_id(0), pl.program_id(1)

        @pl.when(j == 0)
        def _():
            n, _ = _rms(h_ref[...], g_ref[...])
            n_ref[...] = n.astype(BF16)
            dob_s[...] = (0.5 * do_ref[...]).astype(BF16)
            dn_s[...] = jnp.zeros_like(dn_s)

        av = a_ref[0].astype(F32)
        bv = b_ref[0].astype(F32)
        sig = _sigmoid(av)
        sa = av * sig
        ds = _dot_nt(dob_s[...], w2_ref[0])
        s_ref[0] = (sa * bv).astype(BF16)
        da = (ds * bv * (sig + sa * (1.0 - sig))).astype(BF16)
        db = (ds * sa).astype(BF16)
        da_ref[0] = da
        db_ref[0] = db
        dn_s[...] += _dot(da, w1_ref[0]) + _dot(db, w3_ref[0])

        @pl.when(j == NSH - 1)
        def _():
            hv = h_ref[...]
            gv = g_ref[...]
            r = lax.rsqrt(jnp.mean(hv * hv, axis=-1, keepdims=True) + EPS)
            dn = dn_s[...]
            dx, xh = _rms_bwd(dn, hv, r, gv)
            dh_ref[...] = do_ref[...] + dx
            _acc_rows(dg_ref, jnp.sum(dn * xh, axis=0, keepdims=True), i == 0)

    row = pl.BlockSpec((tm, D), lambda i, j: (i, 0))
    hid = pl.BlockSpec((1, tm, FS), lambda i, j: (j, i, 0))
    wsp = pl.BlockSpec((1, FS, D), lambda i, j: (j, 0, 0))
    return _call(
        body, comm, name=name, grid=(L // tm, NSH),
        out_shape=[jax.ShapeDtypeStruct((L, D), F32)]
        + [jax.ShapeDtypeStruct((NSH, L, FS), BF16)] * 3
        + [jax.ShapeDtypeStruct((L, D), BF16), jax.ShapeDtypeStruct((1, D), F32)],
        in_specs=[row, _res((1, D)), row, hid, hid,
                  wsp, wsp, wsp],
        out_specs=[row, hid, hid, hid, row, pl.BlockSpec((1, D), lambda i, j: (0, 0))],
        scratch_shapes=[pltpu.VMEM((tm, D), BF16), pltpu.VMEM((tm, D), F32)],
        params=_params("arbitrary", "arbitrary"),
    )(h, g, dout, a, b, w1, w3, w2)


def _wgrad(xm, ym, name, scale=1.0):
    xs, ys = xm.ndim == 3, ym.ndim == 3
    assert not (xs and ys)
    L = xm.shape[-2]
    K, N = xm.shape[-1], ym.shape[-1]
    tl = _tile(L, 2112)
    nl = L // tl
    if xs or ys:
        tn, grid_n = N, NSH
    else:
        tn = _tile(N, 1024, 128)
        grid_n = N // tn

    def body(x_ref, y_ref, o_ref, acc_s):
        l = pl.program_id(1)
        xv = x_ref[0] if xs else x_ref[...]
        yv = y_ref[0] if ys else y_ref[...]
        part = _dot_tn(xv.astype(BF16), yv.astype(BF16))
        _acc_rows(acc_s, part, l == 0)

        @pl.when(l == nl - 1)
        def _():
            res = (acc_s[...] * scale).astype(BF16)
            if xs or ys:
                o_ref[0] = res
            else:
                o_ref[...] = res

    if xs:
        x_spec = pl.BlockSpec((1, tl, K), lambda n, l: (n, l, 0))
        y_spec = pl.BlockSpec((tl, N), lambda n, l: (l, 0))
        o_spec = pl.BlockSpec((1, K, N), lambda n, l: (n, 0, 0))
        o_shape = (NSH, K, N)
    elif ys:
        x_spec = pl.BlockSpec((tl, K), lambda n, l: (l, 0))
        y_spec = pl.BlockSpec((1, tl, N), lambda n, l: (n, l, 0))
        o_spec = pl.BlockSpec((1, K, N), lambda n, l: (n, 0, 0))
        o_shape = (NSH, K, N)
    else:
        x_spec = pl.BlockSpec((tl, K), lambda n, l: (l, 0))
        y_spec = pl.BlockSpec((tl, tn), lambda n, l: (l, n))
        o_spec = pl.BlockSpec((K, tn), lambda n, l: (0, n))
        o_shape = (K, N)
    return pl.pallas_call(
        body, name=name, grid=(grid_n, nl),
        out_shape=jax.ShapeDtypeStruct(o_shape, BF16),
        in_specs=[x_spec, y_spec], out_specs=o_spec,
        scratch_shapes=[pltpu.VMEM((K, tn), F32)],
        compiler_params=_params("parallel", "arbitrary"),
    )(xm, ym)


def _mix_in_fwd(h, g, w_in, b_gate, name, comm=None):
    L = h.shape[0]
    tm = _tile(L, 528)

    def body(h_ref, g_ref, w_ref, bg_ref, vg_ref, uf_ref, gt_ref):
        u, _ = _rms(h_ref[...], g_ref[...])
        ub = u.astype(BF16)
        p = [_dot(ub, w_ref[j]) for j in range(NSH)]
        a0, a1 = 2 * DC - WS, 2 * DC + DS - WS
        vg_ref[:, 0:WS] = p[0].astype(BF16)
        vg_ref[:, WS:2 * DC] = p[1][:, 0:a0].astype(BF16)
        uf_ref[...] = p[1][:, a0:a1].astype(BF16)
        gin = jnp.concatenate([p[1][:, a1:], p[2], p[3]], axis=1)
        gt_ref[...] = _sigmoid(gin + bg_ref[...]).astype(BF16)

    def row(n):
        return pl.BlockSpec((tm, n), lambda i: (i, 0))

    return _call(
        body, comm, name=name, grid=(L // tm,),
        out_shape=[jax.ShapeDtypeStruct((L, 2 * DC), BF16), jax.ShapeDtypeStruct((L, DS), BF16),
                   jax.ShapeDtypeStruct((L, 2 * D), BF16)],
        in_specs=[row(D), _res((1, D)), _res((NSH, D, WS)), _res((1, 2 * D))],
        out_specs=[row(2 * DC), row(DS), row(2 * D)],
        params=_params("parallel"),
    )(h, g, w_in, b_gate)


def _mix_in_bwd(h, g, dres, dv, dgl, duf, dgate, w_in, name):
    L = h.shape[0]
    tm = _tile(L, 528)

    def body(h_ref, g_ref, dr_ref, dv_ref, dgl_ref, duf_ref, dgt_ref, w_ref, dh_ref, u_ref, dp_ref, dgm_ref):
        i = pl.program_id(0)
        hv = h_ref[...]
        gv = g_ref[...]
        u, r = _rms(hv, gv)
        u_ref[...] = u.astype(BF16)
        a0, a1 = 2 * DC - WS, 2 * DC + DS - WS
        b0 = WS - a1
        dp = [jnp.concatenate([dv_ref[...], dgl_ref[:, 0:WS - DC]], axis=1),
              jnp.concatenate([dgl_ref[:, WS - DC:], duf_ref[...], dgt_ref[:, 0:b0]], axis=1),
              dgt_ref[:, b0:b0 + WS], dgt_ref[:, b0 + WS:]]
        du = jnp.zeros((tm, D), F32)
        for j in range(NSH):
            dp_ref[j] = dp[j]
            du = du + _dot_nt(dp[j], w_ref[j])
        dx, xh = _rms_bwd(du, hv, r, gv)
        dh_ref[...] = dr_ref[...] + dx
        _acc_rows(dgm_ref, jnp.sum(du * xh, axis=0, keepdims=True), i == 0)

    def row(n):
        return pl.BlockSpec((tm, n), lambda i: (i, 0))

    return pl.pallas_call(
        body, name=name, grid=(L // tm,),
        out_shape=[jax.ShapeDtypeStruct((L, D), F32), jax.ShapeDtypeStruct((L, D), BF16),
                   jax.ShapeDtypeStruct((NSH, L, WS), BF16), jax.ShapeDtypeStruct((1, D), F32)],
        in_specs=[row(D), _res((1, D)), row(D), row(DC), row(DC), row(DS), row(2 * D), _res((NSH, D, WS))],
        out_specs=[row(D), row(D), pl.BlockSpec((NSH, tm, WS), lambda i: (0, i, 0)),
                   pl.BlockSpec((1, D), lambda i: (0, 0))],
        compiler_params=_params("arbitrary"),
    )(h, g, dres, dv, dgl, duf, dgate, w_in)


def _conv_fwd(vg, dw, dwb, name, comm=None):
    L = vg.shape[0]
    nc = DC // 128

    def body(v_ref, g_ref, dw_ref, dwb_ref, z_ref, zp_s):
        zp_s[0:KWP, :] = jnp.zeros((KWP, 128), F32)
        zp_s[KWP:, :] = v_ref[...].astype(F32) * _sigmoid(g_ref[...].astype(F32))
        for r0 in range(0, L, CONV_ROWS):
            acc = jnp.broadcast_to(dwb_ref[...], (CONV_ROWS, 128))
            for k in range(KW):
                acc = acc + dw_ref[k:k + 1, :] * zp_s[pl.ds(r0 + k + 2, CONV_ROWS), :]
            z_ref[pl.ds(r0, CONV_ROWS), :] = acc

    return _call(
        body, comm, name=name, grid=(nc,),
        out_shape=[jax.ShapeDtypeStruct((L, DC), F32)],
        in_specs=[pl.BlockSpec((L, 128), lambda c: (0, c)), pl.BlockSpec((L, 128), lambda c: (0, nc + c)),
                  pl.BlockSpec((KWP, 128), lambda c: (0, c)), pl.BlockSpec((1, 128), lambda c: (0, c))],
        out_specs=[pl.BlockSpec((L, 128), lambda c: (0, c))],
        scratch_shapes=[pltpu.VMEM((L + KWP, 128), F32)],
        params=_params("parallel"),
    )(vg, vg, dw, dwb)


def _conv_bwd(dz1, vg, dw, name):
    L = vg.shape[0]
    nc = DC // 128

    def body(dz_ref, v_ref, g_ref, dw_ref, dv_ref, dg_ref, ddw_ref, ddwb_ref, zp_s, dzp_s):
        vv = v_ref[...].astype(F32)
        sg = _sigmoid(g_ref[...].astype(F32))
        zp_s[0:KWP, :] = jnp.zeros((KWP, 128), F32)
        zp_s[KWP:, :] = vv * sg
        dz = dz_ref[...]
        dzp_s[0:L, :] = dz
        dzp_s[L:, :] = jnp.zeros((KWP, 128), F32)
        ddwb_ref[...] = jnp.sum(dz, axis=0, keepdims=True)
        part = [jnp.zeros((8, 128), F32) for _ in range(KW)]
        for r0 in range(0, L, CONV_ROWS):
            rows = pl.ds(r0, CONV_ROWS)
            dzc = dz_ref[rows, :]
            acc = jnp.zeros((CONV_ROWS, 128), F32)
            for k in range(KW):
                acc = acc + dw_ref[k:k + 1, :] * dzp_s[pl.ds(r0 + KW - 1 - k, CONV_ROWS), :]
                prod = dzc * zp_s[pl.ds(r0 + k + 2, CONV_ROWS), :]
                for q in range(CONV_ROWS // 8):
                    part[k] = part[k] + prod[8 * q:8 * (q + 1), :]
            vc = v_ref[rows, :].astype(F32)
            sc = _sigmoid(g_ref[rows, :].astype(F32))
            dv_ref[rows, :] = (acc * sc).astype(BF16)
            dg_ref[rows, :] = (acc * vc * sc * (1.0 - sc)).astype(BF16)
        for k in range(KW):
            ddw_ref[k:k + 1, :] = jnp.sum(part[k], axis=0, keepdims=True)
        ddw_ref[KW:KWP, :] = jnp.zeros((KWP - KW, 128), F32)

    col = pl.BlockSpec((L, 128), lambda c: (0, c))
    return pl.pallas_call(
        body, name=name, grid=(nc,),
        out_shape=[jax.ShapeDtypeStruct((L, DC), BF16), jax.ShapeDtypeStruct((L, DC), BF16),
                   jax.ShapeDtypeStruct((KWP, DC), F32), jax.ShapeDtypeStruct((1, DC), F32)],
        in_specs=[col, col, pl.BlockSpec((L, 128), lambda c: (0, nc + c)),
                  pl.BlockSpec((KWP, 128), lambda c: (0, c))],
        out_specs=[col, col, pl.BlockSpec((KWP, 128), lambda c: (0, c)), pl.BlockSpec((1, 128), lambda c: (0, c))],
        scratch_shapes=[pltpu.VMEM((L + KWP, 128), F32), pltpu.VMEM((L + KWP, 128), F32)],
        compiler_params=_params("parallel"),
    )(dz1, vg, vg, dw)


NLB = QS // 128


def _lb_store(ref, rows, val):
    for cb in range(NLB):
        ref[cb, rows, :] = val[:, cb * 128:(cb + 1) * 128]


def _lb_load(ref, rows):
    return jnp.concatenate([ref[cb, rows, :] for cb in range(NLB)], axis=1)


def _scan(xr_ref, xi_ref, base, T, ar, ai, atr, ati, reverse):
    W = ar.shape[1]
    ar, ai, atr, ati = (jnp.broadcast_to(v, (8, W)) for v in (ar, ai, atr, ati))
    zero = jnp.zeros((8, W), F32)

    def rows(t, g):
        tt = T - 1 - t if reverse else t
        return pl.ds(base + g * 8 * T + tt, 8, stride=T)

    def make_step(store):
        def step(t, carry):
            out = []
            for g in range(NGRP):
                sr, si = carry[2 * g], carry[2 * g + 1]
                idx = rows(t, g)
                nr = ar * sr - ai * si + _lb_load(xr_ref, idx)
                ni = ar * si + ai * sr + _lb_load(xi_ref, idx)
                if store:
                    _lb_store(xr_ref, idx, nr)
                    _lb_store(xi_ref, idx, ni)
                out += [nr, ni]
            return tuple(out)
        return step

    ends = lax.fori_loop(0, T, make_step(False), (zero,) * (2 * NGRP))
    sub = lax.broadcasted_iota(jnp.int32, (8, W), 0)
    edge = sub == (7 if reverse else 0)
    shift, last = (7, 0) if reverse else (1, 7)
    inr, ini = jnp.zeros((1, W), F32), jnp.zeros((1, W), F32)
    starts = [None] * (2 * NGRP)
    for g in (reversed(range(NGRP)) if reverse else range(NGRP)):
        er, ei = ends[2 * g], ends[2 * g + 1]
        cr, ci = jnp.where(edge, inr, 0.0), jnp.where(edge, ini, 0.0)
        for _ in range(7):
            nr = atr * cr - ati * ci + er
            ni = atr * ci + ati * cr + ei
            cr = jnp.where(edge, inr, pltpu.roll(nr, shift, 0))
            ci = jnp.where(edge, ini, pltpu.roll(ni, shift, 0))
        starts[2 * g], starts[2 * g + 1] = cr, ci
        inr = (atr * cr - ati * ci + er)[last:last + 1]
        ini = (atr * ci + ati * cr + ei)[last:last + 1]
    lax.fori_loop(0, T, make_step(True), tuple(starts))


def _ssm_fwd(uf, bre, bim, cre, cim, lamp, dsk, name, comm=None):
    L = uf.shape[0]
    T = L // NSEG
    tc = L // NCH

    def body(u_ref, bre_ref, bim_ref, cre_ref, cim_ref, lam_ref, d_ref, y_ref, sr_s, si_s):
        for k in range(NCH):
            sl = slice(k * tc, (k + 1) * tc)
            uk = u_ref[sl, :]
            _lb_store(sr_s, sl, _dot(uk, bre_ref[...]))
            _lb_store(si_s, sl, _dot(uk, bim_ref[...]))
        _scan(sr_s, si_s, 0, T, lam_ref[0:1, :], lam_ref[1:2, :], lam_ref[2:3, :], lam_ref[3:4, :], False)
        for k in range(NCH):
            sl = slice(k * tc, (k + 1) * tc)
            y_ref[sl, :] = (_dot(_lb_load(sr_s, sl).astype(BF16), cre_ref[...])
                            - _dot(_lb_load(si_s, sl).astype(BF16), cim_ref[...])
                            + d_ref[...] * u_ref[sl, :].astype(F32))

    return _call(
        body, comm, name=name, grid=(NQ,),
        out_shape=[jax.ShapeDtypeStruct((L, DS), F32)],
        in_specs=[pl.BlockSpec((L, QU), lambda q: (0, q)),
                  pl.BlockSpec((QU, QS), lambda q: (q, q)), pl.BlockSpec((QU, QS), lambda q: (q, q)),
                  pl.BlockSpec((QS, QU), lambda q: (q, q)), pl.BlockSpec((QS, QU), lambda q: (q, q)),
                  pl.BlockSpec((8, QS), lambda q: (0, q)), pl.BlockSpec((1, QU), lambda q: (0, q))],
        out_specs=[pl.BlockSpec((L, QU), lambda q: (0, q))],
        scratch_shapes=[pltpu.VMEM((NLB, L, 128), F32), pltpu.VMEM((NLB, L, 128), F32)],
        params=_params("parallel"),
    )(uf, bre, bim, cre, cim, lamp, dsk)


def _ssm_bwd(uf, dyss, bre, bim, cre, cim, lamp, dsk, name, comm=None):
    L = uf.shape[0]
    T = L // NSEG
    tc = L // NCH

    def body(u_ref, dy_ref, bre_ref, bim_ref, cre_ref, cim_ref, lam_ref, d_ref,
             du_ref, dbre_ref, dbim_ref, dcre_ref, dcim_ref, dlam_ref, dd_ref, sr_s, si_s, gr_s, gi_s):
        _lb_store(sr_s, slice(0, SOFF), jnp.zeros((SOFF, QS), F32))
        _lb_store(si_s, slice(0, SOFF), jnp.zeros((SOFF, QS), F32))
        for k in range(NCH):
            sl = slice(k * tc, (k + 1) * tc)
            ss = slice(SOFF + k * tc, SOFF + (k + 1) * tc)
            uk = u_ref[sl, :]
            dyk = dy_ref[sl, :].astype(BF16)
            _lb_store(sr_s, ss, _dot(uk, bre_ref[...]))
            _lb_store(si_s, ss, _dot(uk, bim_ref[...]))
            _lb_store(gr_s, sl, _dot_nt(dyk, cre_ref[...]))
            _lb_store(gi_s, sl, -_dot_nt(dyk, cim_ref[...]))
        ar, ai, atr, ati = lam_ref[0:1, :], lam_ref[1:2, :], lam_ref[2:3, :], lam_ref[3:4, :]
        _scan(sr_s, si_s, SOFF, T, ar, ai, atr, ati, False)
        _scan(gr_s, gi_s, 0, T, ar, -ai, atr, -ati, True)
        dbre = jnp.zeros((QU, QS), F32)
        dbim = jnp.zeros((QU, QS), F32)
        dcre = jnp.zeros((QS, QU), F32)
        dcim = jnp.zeros((QS, QU), F32)
        dd = jnp.zeros((1, QU), F32)
        qr = jnp.zeros((1, QS), F32)
        qi = jnp.zeros((1, QS), F32)
        for k in range(NCH):
            sl = slice(k * tc, (k + 1) * tc)
            ss = slice(SOFF + k * tc, SOFF + (k + 1) * tc)
            sp = slice(SOFF - 1 + k * tc, SOFF - 1 + (k + 1) * tc)
            uk = u_ref[sl, :]
            dyk = dy_ref[sl, :]
            dyb = dyk.astype(BF16)
            gr, gi = _lb_load(gr_s, sl), _lb_load(gi_s, sl)
            pr, pi = _lb_load(sr_s, sp), _lb_load(si_s, sp)
            qr = qr + jnp.sum(gr * pr + gi * pi, axis=0, keepdims=True)
            qi = qi + jnp.sum(gi * pr - gr * pi, axis=0, keepdims=True)
            grb, gib = gr.astype(BF16), gi.astype(BF16)
            du_ref[sl, :] = (_dot_nt(grb, bre_ref[...]) + _dot_nt(gib, bim_ref[...])
                             + dyk * d_ref[...]).astype(BF16)
            dbre = dbre + _dot_tn(uk, grb)
            dbim = dbim + _dot_tn(uk, gib)
            dcre = dcre + _dot_tn(_lb_load(sr_s, ss).astype(BF16), dyb)
            dcim = dcim - _dot_tn(_lb_load(si_s, ss).astype(BF16), dyb)
            dd = dd + jnp.sum(dyk * uk.astype(F32), axis=0, keepdims=True)
        dlam_ref[0] = jnp.concatenate([qr, qi, jnp.zeros((6, QS), F32)], axis=0)
        dbre_ref[0] = dbre
        dbim_ref[0] = dbim
        dcre_ref[0] = dcre
        dcim_ref[0] = dcim
        dd_ref[...] = dd

    col = pl.BlockSpec((L, QU), lambda q: (0, q))
    bsp = pl.BlockSpec((QU, QS), lambda q: (q, q))
    csp = pl.BlockSpec((QS, QU), lambda q: (q, q))
    return _call(
        body, comm, name=name, grid=(NQ,),
        out_shape=[jax.ShapeDtypeStruct((L, DS), BF16),
                   jax.ShapeDtypeStruct((NQ, QU, QS), F32), jax.ShapeDtypeStruct((NQ, QU, QS), F32),
                   jax.ShapeDtypeStruct((NQ, QS, QU), F32), jax.ShapeDtypeStruct((NQ, QS, QU), F32),
                   jax.ShapeDtypeStruct((NQ, 8, QS), F32), jax.ShapeDtypeStruct((1, DS), F32)],
        in_specs=[col, col, bsp, bsp, csp, csp,
                  pl.BlockSpec((8, QS), lambda q: (0, q)), pl.BlockSpec((1, QU), lambda q: (0, q))],
        out_specs=[col,
                   pl.BlockSpec((1, QU, QS), lambda q: (q, 0, 0)), pl.BlockSpec((1, QU, QS), lambda q: (q, 0, 0)),
                   pl.BlockSpec((1, QS, QU), lambda q: (q, 0, 0)), pl.BlockSpec((1, QS, QU), lambda q: (q, 0, 0)),
                   pl.BlockSpec((1, 8, QS), lambda q: (q, 0, 0)), pl.BlockSpec((1, QU), lambda q: (0, q))],
        scratch_shapes=[pltpu.VMEM((NLB, L + SOFF, 128), F32), pltpu.VMEM((NLB, L + SOFF, 128), F32),
                        pltpu.VMEM((NLB, L, 128), F32), pltpu.VMEM((NLB, L, 128), F32)],
        params=_params("parallel"),
    )(uf, dyss, bre, bim, cre, cim, lamp, dsk)


def _branches(z1_ref, yss_ref, gt_ref, lng_ref, lnb_ref, wp_ref, wv_ref, wg_ref):
    zf = z1_ref[...]
    mu = jnp.mean(zf, axis=-1, keepdims=True)
    zc = zf - mu
    rstd = lax.rsqrt(jnp.mean(zc * zc, axis=-1, keepdims=True) + EPS)
    zn = zc * rstd
    z2 = zn * lng_ref[...] + lnb_ref[...]
    sz = _sigmoid(z2)
    z3 = (z2 * sz).astype(BF16)
    y_conv = _dot(z3, wp_ref[...])
    yss = yss_ref[...]
    yg = _gelu(yss).astype(BF16)
    sv = _dot(yg, wv_ref[...])
    sig = _sigmoid(_dot(yg, wg_ref[...]))
    y_ssm = sv * sig
    gc = gt_ref[:, 0:D].astype(F32)
    gs = gt_ref[:, D:2 * D].astype(F32)
    m = gc * y_conv + gs * y_ssm
    return dict(rstd=rstd, zn=zn, z2=z2, sz=sz, z3=z3, y_conv=y_conv, yss=yss, yg=yg, sv=sv, sig=sig,
                y_ssm=y_ssm, gc=gc, gs=gs, m=m)


def _merge_fwd(h, z1, yss, gate, lng, lnb, wp, wv, wg, wo, name):
    L = h.shape[0]
    tm = _tile(L, 528)

    def body(h_ref, z1_ref, yss_ref, gt_ref, lng_ref, lnb_ref, wp_ref, wv_ref, wg_ref, wo_ref, o_ref):
        f = _branches(z1_ref, yss_ref, gt_ref, lng_ref, lnb_ref, wp_ref, wv_ref, wg_ref)
        o_ref[...] = h_ref[...] + _dot(f["m"].astype(BF16), wo_ref[...])

    def row(n):
        return pl.BlockSpec((tm, n), lambda i: (i, 0))

    return pl.pallas_call(
        body, name=name, grid=(L // tm,),
        out_shape=jax.ShapeDtypeStruct((L, D), F32),
        in_specs=[row(D), row(DC), row(DS), row(2 * D), _res((1, DC)), _res((1, DC)),
                  _res((DC, D)), _res((DS, D)), _res((DS, D)), _res((D, D))],
        out_specs=row(D),
        compiler_params=_params("parallel"),
    )(h, z1, yss, gate, lng, lnb, wp, wv, wg, wo)


def _merge_bwd(dh, z1, yss, gate, lng, lnb, wp, wv, wg, wo, name):
    L = dh.shape[0]
    tm = _tile(L, 352)

    def body(dh_ref, z1_ref, yss_ref, gt_ref, lng_ref, lnb_ref, wp_ref, wv_ref, wg_ref, wo_ref,
             m_ref, dgt_ref, dyc_ref, z3_ref, dz1_ref, yg_ref, dsv_ref, dsg_ref, dyss_ref,
             dbg_ref, dlng_ref, dlnb_ref):
        i = pl.program_id(0)
        f = _branches(z1_ref, yss_ref, gt_ref, lng_ref, lnb_ref, wp_ref, wv_ref, wg_ref)
        gc, gs, sig, sv = f["gc"], f["gs"], f["sig"], f["sv"]
        m_ref[...] = f["m"].astype(BF16)
        z3_ref[...] = f["z3"]
        yg_ref[...] = f["yg"]
        dm = _dot_nt(dh_ref[...].astype(BF16), wo_ref[...])
        dgc = (dm * f["y_conv"] * gc * (1.0 - gc)).astype(BF16)
        dgs = (dm * f["y_ssm"] * gs * (1.0 - gs)).astype(BF16)
        dgt_ref[:, 0:D] = dgc
        dgt_ref[:, D:2 * D] = dgs
        part = jnp.concatenate([jnp.sum(dgc.astype(F32), axis=0, keepdims=True),
                                jnp.sum(dgs.astype(F32), axis=0, keepdims=True)], axis=1)
        _acc_rows(dbg_ref, part, i == 0)
        dyc = (dm * gc).astype(BF16)
        dyc_ref[...] = dyc
        dys = dm * gs
        dsv = (dys * sig).astype(BF16)
        dsg = (dys * sv * sig * (1.0 - sig)).astype(BF16)
        dsv_ref[...] = dsv
        dsg_ref[...] = dsg
        dyg = _dot_nt(dsv, wv_ref[...]) + _dot_nt(dsg, wg_ref[...])
        dyss_ref[...] = dyg * _gelu_grad(f["yss"])
        dz3 = _dot_nt(dyc, wp_ref[...])
        z2, sz, zn = f["z2"], f["sz"], f["zn"]
        dz2 = dz3 * sz * (1.0 + z2 * (1.0 - sz))
        _acc_rows(dlng_ref, jnp.sum(dz2 * zn, axis=0, keepdims=True), i == 0)
        _acc_rows(dlnb_ref, jnp.sum(dz2, axis=0, keepdims=True), i == 0)
        dzn = dz2 * lng_ref[...]
        dz1_ref[...] = f["rstd"] * (dzn - jnp.mean(dzn, axis=-1, keepdims=True)
                                    - zn * jnp.mean(dzn * zn, axis=-1, keepdims=True))

    def row(n):
        return pl.BlockSpec((tm, n), lambda i: (i, 0))

    def tot(n):
        return pl.BlockSpec((1, n), lambda i: (0, 0))

    return pl.pallas_call(
        body, name=name, grid=(L // tm,),
        out_shape=[jax.ShapeDtypeStruct((L, D), BF16), jax.ShapeDtypeStruct((L, 2 * D), BF16),
                   jax.ShapeDtypeStruct((L, D), BF16), jax.ShapeDtypeStruct((L, DC), BF16),
                   jax.ShapeDtypeStruct((L, DC), F32), jax.ShapeDtypeStruct((L, DS), BF16),
                   jax.ShapeDtypeStruct((L, D), BF16), jax.ShapeDtypeStruct((L, D), BF16),
                   jax.ShapeDtypeStruct((L, DS), F32),
                   jax.ShapeDtypeStruct((1, 2 * D), F32), jax.ShapeDtypeStruct((1, DC), F32),
                   jax.ShapeDtypeStruct((1, DC), F32)],
        in_specs=[row(D), row(DC), row(DS), row(2 * D), _res((1, DC)), _res((1, DC)),
                  _res((DC, D)), _res((DS, D)), _res((DS, D)), _res((D, D))],
        out_specs=[row(D), row(2 * D), row(D), row(DC), row(DC), row(DS), row(D), row(D), row(DS),
                   tot(2 * D), tot(DC), tot(DC)],
        compiler_params=_params("arbitrary"),
    )(dh, z1, yss, gate, lng, lnb, wp, wv, wg, wo)


def _ssm_disc(lam_re, lam_im, log_dt, b_re, b_im):
    lam = lax.complex(lam_re, lam_im)
    dt = jnp.exp(log_dt)[:, None]
    lam_bar = jnp.exp(lam * dt)
    bbar = ((lam_bar - 1.0) / lam)[..., None] * lax.complex(b_re, b_im)
    return jnp.real(lam_bar), jnp.imag(lam_bar), jnp.real(bbar), jnp.imag(bbar)


def _bdiag_in(m):
    return jnp.einsum("gph,gk->ghkp", m, jnp.eye(G, dtype=m.dtype)).reshape(G * H, G * P)


def _bdiag_out(m):
    return jnp.einsum("ghp,gk->gpkh", m, jnp.eye(G, dtype=m.dtype)).reshape(G * P, G * H)


def _diag_blocks(m4):
    return jnp.einsum("qiaib->qiab", m4).reshape(G, m4.shape[2], m4.shape[4])


def _pack(parts, rows_mult=8):
    flat = jnp.concatenate([p.reshape(-1).astype(F32) for p in parts])
    n = flat.shape[0]
    tot = -(-n // (128 * rows_mult)) * (128 * rows_mult)
    return jnp.pad(flat, (0, tot - n)).reshape(tot // 128, 128)


def _unpack(buf, shapes):
    flat = buf.reshape(-1)
    out, o = [], 0
    for s in shapes:
        n = math.prod(s)
        out.append(flat[o:o + n].reshape(s))
        o += n
    return out


def kernel(x, meta_tokens, ffn1_norm, ffn1_w1, ffn1_w3, ffn1_w2, mix_norm, w_in, b_gate, conv_dw, conv_dw_b, conv_ln_g, conv_ln_b, conv_proj, ssm_lam_re, ssm_lam_im, ssm_log_dt, ssm_b_re, ssm_b_im, ssm_c_re, ssm_c_im, ssm_d, ssm_w_v, ssm_w_g, w_out, ffn2_norm, ffn2_w1, ffn2_w3, ffn2_w2, final_norm, loss_target, m_meta_tokens, m_ffn1_norm, m_ffn1_w1, m_ffn1_w3, m_ffn1_w2, m_mix_norm, m_w_in, m_b_gate, m_conv_dw, m_conv_dw_b, m_conv_ln_g, m_conv_ln_b, m_conv_proj, m_ssm_lam_re, m_ssm_lam_im, m_ssm_log_dt, m_ssm_b_re, m_ssm_b_im, m_ssm_c_re, m_ssm_c_im, m_ssm_d, m_ssm_w_v, m_ssm_w_g, m_w_out, m_ffn2_norm, m_ffn2_w1, m_ffn2_w3, m_ffn2_w2, m_final_norm, v_meta_tokens, v_ffn1_norm, v_ffn1_w1, v_ffn1_w3, v_ffn1_w2, v_mix_norm, v_w_in, v_b_gate, v_conv_dw, v_conv_dw_b, v_conv_ln_g, v_conv_ln_b, v_conv_proj, v_ssm_lam_re, v_ssm_lam_im, v_ssm_log_dt, v_ssm_b_re, v_ssm_b_im, v_ssm_c_re, v_ssm_c_im, v_ssm_d, v_ssm_w_v, v_ssm_w_g, v_w_out, v_ffn2_norm, v_ffn2_w1, v_ffn2_w3, v_ffn2_w2, v_final_norm):
    args = dict(locals())
    names = ["meta_tokens", "ffn1_norm", "ffn1_w1", "ffn1_w3", "ffn1_w2", "mix_norm", "w_in", "b_gate",
             "conv_dw", "conv_dw_b", "conv_ln_g", "conv_ln_b", "conv_proj", "ssm_lam_re", "ssm_lam_im",
             "ssm_log_dt", "ssm_b_re", "ssm_b_im", "ssm_c_re", "ssm_c_im", "ssm_d", "ssm_w_v", "ssm_w_g",
             "w_out", "ffn2_norm", "ffn2_w1", "ffn2_w3", "ffn2_w2", "final_norm"]
    big = ["ffn1_w1", "ffn1_w3", "ffn1_w2", "w_in", "conv_proj", "ssm_w_v", "ssm_w_g", "w_out",
           "ffn2_w1", "ffn2_w3", "ffn2_w2"]
    small = [n for n in names if n not in big]

    xs = x[0]
    S = xs.shape[0]
    L = FRONT + S
    T = L // NSEG
    jx, jy = lax.axis_index("x"), lax.axis_index("y")
    chip = 2 * jx + jy

    sm = _gather_all(_pack([meta_tokens, conv_dw[0]]), "gather_small")[0::2].reshape(NSH, -1)
    nmt = NMETA * (D // NSH)
    ndw = KW * (DC // NSH)
    meta_full = sm[:, :nmt].reshape(NSH, NMETA, D // NSH).transpose(1, 0, 2).reshape(NMETA, D)
    dw_full = sm[:, nmt:nmt + ndw].reshape(NSH, KW, DC // NSH).transpose(1, 0, 2).reshape(KW, DC)
    dw_pad = jnp.pad(dw_full, ((0, KWP - KW), (0, 0)))
    tposed = ("ffn1_w1", "ffn1_w3", "ffn2_w1", "ffn2_w3")

    def view(a, n):
        return jnp.swapaxes(a, 1, 2) if n in tposed else a

    grp_a = ["ffn1_w1", "ffn1_w3", "ffn1_w2"]
    grp_b = ["w_in", "conv_proj", "ssm_w_v", "ssm_w_g", "w_out"]
    grp_c = ["ffn2_w1", "ffn2_w3", "ffn2_w2"]

    def shard(n):
        return view(args[n], n)[0].astype(BF16)

    sh_a = [shard(n) for n in grp_a]
    ga_send, ga_recv, sh_a, land_a, ga_token = _chips_start(
        sh_a, [jax.ShapeDtypeStruct((NSH,) + s.shape, s.dtype) for s in sh_a], True, "gather_ffn1_start")

    def cols(w):
        return w.transpose(1, 0, 2).reshape(w.shape[1], -1)

    disc_in = (ssm_lam_re[0], ssm_lam_im[0], ssm_log_dt[0], ssm_b_re[0], ssm_b_im[0])
    (lbr, lbi, bbr, bbi), disc_vjp = jax.vjp(_ssm_disc, *disc_in)
    lam_t = jnp.exp(lax.complex(ssm_lam_re[0], ssm_lam_im[0]) * (jnp.exp(ssm_log_dt[0])[:, None] * T))
    lamp = jnp.concatenate([lbr.reshape(1, NST), lbi.reshape(1, NST), jnp.real(lam_t).reshape(1, NST),
                            jnp.imag(lam_t).reshape(1, NST), jnp.zeros((4, NST), F32)], axis=0)
    bre_bd, bim_bd = _bdiag_in(bbr).astype(BF16), _bdiag_in(bbi).astype(BF16)
    cre_bd, cim_bd = _bdiag_out(ssm_c_re[0]).astype(BF16), _bdiag_out(ssm_c_im[0]).astype(BF16)

    h0 = lax.dynamic_update_slice(jnp.pad(xs, ((FRONT, 0), (0, 0))), meta_full + ga_token[0:1, 0:1],
                                  (FRONT - NMETA, 0))
    sh_a, land_a = _chips_wait(ga_send, ga_recv, sh_a, land_a, [h0], True, "gather_ffn1_wait")
    gw = dict(zip(grp_a, _pass_halves(land_a, "pass_ffn1", sh_a)))
    tgt = jnp.pad(loss_target[0], ((FRONT, 0), (0, 0)))
    (h1, a1, b1), got = _ffn_fwd(h0, ffn1_norm, gw["ffn1_w1"], gw["ffn1_w3"], gw["ffn1_w2"], "ffn1_fwd",
                                 _gather_half_behind([shard(n) for n in grp_b]))
    gw.update(zip(grp_b, _pass_halves(got, "pass_mix")))
    w_in_f = gw["w_in"]
    wp_f, wv_f, wg_f = cols(gw["conv_proj"]), cols(gw["ssm_w_v"]), cols(gw["ssm_w_g"])
    wo_f = gw["w_out"].reshape(D, D)
    (vg, uf, gate), got1 = _mix_in_fwd(h1, mix_norm, w_in_f, b_gate, "mix_in_fwd",
                                       _gather_half_behind([shard("ffn2_w1")]))
    (z1,), got3 = _conv_fwd(vg, dw_pad, conv_dw_b, "conv_fwd", _gather_half_behind([shard("ffn2_w3")]))
    (yss,), got2 = _ssm_fwd(uf, bre_bd, bim_bd, cre_bd, cim_bd, lamp, ssm_d, "ssm_fwd",
                            _gather_half_behind([shard("ffn2_w2")]))
    gw.update(zip(grp_c, _pass_halves([got1[0], got3[0], got2[0]], "pass_ffn2")))
    h2 = _merge_fwd(h1, z1, yss, gate, conv_ln_g, conv_ln_b, wp_f, wv_f, wg_f, wo_f, "merge_fwd")
    dh3, a2, b2, loss_part, d_final = _ffn_fwd_loss(
        h2, ffn2_norm, gw["ffn2_w1"], gw["ffn2_w3"], gw["ffn2_w2"], final_norm.reshape(1, D), tgt, "ffn2_fwd_loss")

    gbig = {}
    core = lax.axis_index("c").astype(jnp.int32).reshape(1)

    def pair_sums(group, tag):
        gl = [gbig[n] for n in group]
        sib = _pair_exchange(gl, "pair_exchange_" + tag)
        out = []
        for n, g_, s_ in zip(group, gl, sib):
            out.append(_add_pair(g_, s_, core, "pair_" + n))
        return out

    (dh2, da2, db2, s2, n2, d_ffn2_norm), _ = _ffn_bwd(
        h2, ffn2_norm, dh3, a2, b2, gw["ffn2_w1"], gw["ffn2_w3"], gw["ffn2_w2"], "ffn2_bwd")
    gbig["ffn2_w1"] = _wgrad(da2, n2, "ffn2_dw1")
    gbig["ffn2_w3"] = _wgrad(db2, n2, "ffn2_dw3")
    gbig["ffn2_w2"] = _wgrad(s2, dh3, "ffn2_dw2", 0.5)
    pair_c = pair_sums(grp_c, "ffn2")
    (m_b, dgate, dyc, z3, dz1, yg, dsv, dsg, dyss, d_b_gate, d_ln_g, d_ln_b) = _merge_bwd(
        dh2, z1, yss, gate, conv_ln_g, conv_ln_b, wp_f, wv_f, wg_f, wo_f, "merge_bwd")
    gbig["w_out"] = _wgrad(m_b, dh2, "dw_out").reshape(NSH, D // NSH, D)

    def shard_cols(gm):
        return gm.reshape(gm.shape[0], NSH, -1).transpose(1, 0, 2)

    gbig["conv_proj"] = shard_cols(_wgrad(z3, dyc, "dw_proj"))
    gbig["ssm_w_v"] = shard_cols(_wgrad(yg, dsv, "dw_v"))
    gbig["ssm_w_g"] = shard_cols(_wgrad(yg, dsg, "dw_g"))
    dv, dgl, ddw, d_dw_b = _conv_bwd(dz1, vg, dw_pad, "conv_bwd")
    (duf, dbre, dbim, dcre, dcim, dlam, d_ssm_d), recv_c = _ssm_bwd(
        uf, dyss, bre_bd, bim_bd, cre_bd, cim_bd, lamp, ssm_d, "ssm_bwd", _scatter_chips_behind(pair_c))
    dh1, u_b, dproj, d_mix_norm = _mix_in_bwd(h1, mix_norm, dh2, dv, dgl, duf, dgate, w_in_f, "mix_in_bwd")
    gbig["w_in"] = _wgrad(u_b, dproj, "dw_in")
    pair_b = pair_sums(grp_b, "mix")

    d_bbr = _diag_blocks(dbre.reshape(NQ, 8, H, 8, P)).transpose(0, 2, 1)
    d_bbi = _diag_blocks(dbim.reshape(NQ, 8, H, 8, P)).transpose(0, 2, 1)
    d_c_re = _diag_blocks(dcre.reshape(NQ, 8, P, 8, H)).transpose(0, 2, 1)
    d_c_im = _diag_blocks(dcim.reshape(NQ, 8, P, 8, H)).transpose(0, 2, 1)
    d_lbr = dlam[:, 0, :].reshape(G, P)
    d_lbi = dlam[:, 1, :].reshape(G, P)
    d_lam_re, d_lam_im, d_log_dt, d_b_re, d_b_im = disc_vjp((d_lbr, d_lbi, d_bbr, d_bbi))

    sg = {"mix_norm": d_mix_norm, "b_gate": d_b_gate, "conv_dw": ddw[:KW], "conv_dw_b": d_dw_b,
          "conv_ln_g": d_ln_g, "conv_ln_b": d_ln_b, "ssm_lam_re": d_lam_re, "ssm_lam_im": d_lam_im,
          "ssm_log_dt": d_log_dt, "ssm_b_re": d_b_re, "ssm_b_im": d_b_im, "ssm_c_re": d_c_re, "ssm_c_im": d_c_im,
          "ssm_d": d_ssm_d, "ffn2_norm": d_ffn2_norm, "final_norm": d_final}
    late = ["meta_tokens", "ffn1_norm"]
    early = [n for n in small if n not in late]

    (dh0, da1, db1, s1, n1, d_ffn1_norm), got = _ffn_bwd(
        h0, ffn1_norm, dh1, a1, b1, gw["ffn1_w1"], gw["ffn1_w3"], gw["ffn1_w2"], "ffn1_bwd",
        _join(_scatter_chips_behind(pair_b), _gather_all_behind(_pack([sg[n] for n in early]))))
    recv_b, early_all = got[:len(grp_b)], got[len(grp_b)]
    gbig["ffn1_w1"] = _wgrad(da1, n1, "ffn1_dw1")
    gbig["ffn1_w3"] = _wgrad(db1, n1, "ffn1_dw3")
    gbig["ffn1_w2"] = _wgrad(s1, dh1, "ffn1_dw2", 0.5)
    grad_x = dh0[FRONT:][None]
    sg["meta_tokens"] = dh0[FRONT - NMETA:FRONT]
    sg["ffn1_norm"] = d_ffn1_norm

    pair_a = pair_sums(grp_a, "ffn1")
    late_all = _gather_all(_pack([sg[n] for n in late]), "gather_late_grads")
    sa_send, sa_recv, pair_a, land_s, sa_token = _chips_start(
        pair_a, [jax.ShapeDtypeStruct(p.shape, p.dtype) for p in pair_a], False, "scatter_ffn1_start", [late_all])

    out_g, out_d, out_m, out_v = {}, {}, {}, {}

    def finish(group, recvs, tag, after=None):
        halves = [_sum_slots(r, "sum_" + n, after) for n, r in zip(group, recvs)]
        for n, f in zip(group, _swap_halves(halves, "swap_" + tag)):
            g3 = f.reshape(1, f.shape[0] * f.shape[1], f.shape[2])
            g3, d3, m3, v3 = _adamw(view(args[n], n), g3, view(args["m_" + n], n), view(args["v_" + n], n),
                                "adamw_" + n)
            out_g[n], out_d[n], out_m[n], out_v[n] = (view(t, n) for t in (g3, d3, m3, v3))
            done.append(d3)

    done = []
    finish(grp_b + grp_c, list(recv_b) + list(recv_c), "mix_ffn2", sa_token)

    sgr = dict(zip(early, _unpack(_sum_slots(early_all, "sum_early", sa_token), [sg[n].shape for n in early])))
    sgr.update(zip(late, _unpack(_sum_slots(late_all, "sum_late"), [sg[n].shape for n in late])))
    sgr["meta_tokens"] = lax.dynamic_slice_in_dim(sgr["meta_tokens"], chip * (D // NSH), D // NSH, axis=1)
    sgr["conv_dw"] = lax.dynamic_slice_in_dim(sgr["conv_dw"], chip * (DC // NSH), DC // NSH, axis=1)
    pshapes = [args[n].shape for n in small]
    _, d_s, m_s, v_s = _adamw(_pack([args[n] for n in small])[None], _pack([sgr[n] for n in small])[None],
                           _pack([args["m_" + n] for n in small])[None],
                           _pack([args["v_" + n] for n in small])[None], "adamw_small")
    for n, g_, d_, m_, v_ in zip(small, [sgr[n] for n in small], _unpack(d_s[0], pshapes),
                                 _unpack(m_s[0], pshapes), _unpack(v_s[0], pshapes)):
        out_g[n], out_d[n], out_m[n], out_v[n] = g_.reshape(args[n].shape), d_, m_, v_

    loss = lax.psum(loss_part[0, 0], ("x", "y", "c"))
    pair_a, recv_a = _chips_wait(sa_send, sa_recv, pair_a, land_s, [d_s, grad_x] + done, False,
                                 "scatter_ffn1_wait")
    finish(grp_a, _fill_own(pair_a, recv_a, "own_ffn1"), "ffn1")
    return (loss, grad_x, *[out_g[n] for n in names], *[out_d[n] for n in names],
            *[out_m[n] for n in names], *[out_v[n] for n in names])
```

```python
import math

import jax
import jax.numpy as jnp
from jax import lax
from jax.experimental import pallas as pl
from jax.experimental.pallas import tpu as pltpu

F32 = jnp.float32
BF16 = jnp.bfloat16

D = 1024
NSH = 4
F = 2816
FS = F // NSH
DC = 512
DS = 512
DIN = 2 * DC + DS + 2 * D
WS = DIN // NSH
KW = 31
KWP = 32
CONV_ROWS = 64
NMETA = 16
FRONT = 128
G, P, H = 32, 64, 16
NST = G * P
NQ = 4
QS = NST // NQ
QU = DS // NQ
NSEG = 32
NGRP = NSEG // 8
NCH = 8
SOFF = 8
EPS = 1e-6
LR, B1, B2, AEPS, WD, STEP = 1e-3, 0.9, 0.999, 1e-8, 0.01, 10
VMEM_LIMIT = 58 * 1024 * 1024
MESH = pl.DeviceIdType.MESH
ANY = pl.BlockSpec(memory_space=pl.ANY)


def _params(*sem):
    return pltpu.CompilerParams(dimension_semantics=sem, vmem_limit_bytes=VMEM_LIMIT)


def _res(shape):
    nd = len(shape)
    return pl.BlockSpec(shape, lambda *_: (0,) * nd, pipeline_mode=pl.Buffered(1))


def _tile(n, cap, mult=16):
    best = None
    for t in range(mult, min(n, cap) + 1, mult):
        if n % t == 0:
            best = t
    assert best is not None, (n, cap, mult)
    return best


def _dot(a, b):
    return jnp.dot(a, b, preferred_element_type=F32)


def _dot_nt(a, b):
    return lax.dot_general(a, b, (((1,), (1,)), ((), ())), preferred_element_type=F32)


def _dot_tn(a, b):
    return lax.dot_general(a, b, (((0,), (0,)), ((), ())), preferred_element_type=F32)


def _sigmoid(x):
    return 1.0 / (1.0 + jnp.exp(-x))


_GC = math.sqrt(2.0 / math.pi)
_GA = 0.044715


def _gelu(x):
    return 0.5 * x * (1.0 + jnp.tanh(_GC * (x + _GA * x * x * x)))


def _gelu_grad(x):
    t = jnp.tanh(_GC * (x + _GA * x * x * x))
    return 0.5 * (1.0 + t) + 0.5 * x * (1.0 - t * t) * _GC * (1.0 + 3.0 * _GA * x * x)


def _rms(hv, g):
    r = lax.rsqrt(jnp.mean(hv * hv, axis=-1, keepdims=True) + EPS)
    return hv * r * g, r


def _rms_bwd(dn, hv, r, g):
    xh = hv * r
    dxh = dn * g
    return r * (dxh - xh * jnp.mean(dxh * xh, axis=-1, keepdims=True)), xh


def _acc_rows(ref, part, first):
    @pl.when(first)
    def _():
        ref[...] = part

    @pl.when(jnp.logical_not(first))
    def _():
        ref[...] += part


def _coords():
    return lax.axis_index("x"), lax.axis_index("y"), lax.axis_index("c")


def _flip(v, d):
    return 1 - v if d else v


def _run(local, remote):
    for cp in local + remote:
        cp.start()
    for cp in remote:
        cp.wait()
    for cp in local:
        cp.wait()


def _via_vmem(src, dst, stage, sems, i):
    return (pltpu.make_async_copy(src, stage, sems.at[2 * i]), pltpu.make_async_copy(stage, dst, sems.at[2 * i + 1]))


def _run_staged(staged, remote):
    for load, _ in staged:
        load.start()
    for cp in remote:
        cp.start()
    for load, store in staged:
        load.wait()
        store.start()
    for cp in remote:
        cp.wait()
    for _, store in staged:
        store.wait()


_REL3 = ((1, 0), (0, 1), (1, 1))


class _Behind:
    def __init__(self, arrays, out_shapes, scratch, build, alias_pairs=()):
        self.arrays, self.out_shapes, self.scratch, self.build = list(arrays), list(out_shapes), list(scratch), build
        self.alias_pairs = list(alias_pairs)

    def aliases(self):
        return self.alias_pairs

    def start(self, ins, outs, scr):
        staged, remote = self.build(ins, outs, scr)
        for load, _ in staged:
            load.start()
        for cp in remote:
            cp.start()

    def finish(self, ins, outs, scr):
        staged, remote = self.build(ins, outs, scr)
        for load, store in staged:
            load.wait()
            store.start()
        for cp in remote:
            cp.wait()
        for _, store in staged:
            store.wait()


def _call(body, comm, *, name, grid, in_specs, out_specs, out_shape, scratch_shapes=(), params):
    in_specs, out_specs, out_shape = list(in_specs), list(out_specs), list(out_shape)
    scratch_shapes = list(scratch_shapes)
    if comm is None:
        f = pl.pallas_call(body, name=name, grid=grid, in_specs=in_specs, out_specs=out_specs,
                           out_shape=out_shape, scratch_shapes=scratch_shapes, compiler_params=params)
        return lambda *args: (f(*args), [])
    ni, no, ns = len(in_specs), len(out_specs), len(scratch_shapes)
    ci, co = len(comm.arrays), len(comm.out_shapes)

    def hosted(*refs):
        ins, cin = refs[:ni], refs[ni:ni + ci]
        outs, cout = refs[ni + ci:ni + ci + no], refs[ni + ci + no:ni + ci + no + co]
        scr, cscr = refs[ni + ci + no + co:ni + ci + no + co + ns], refs[ni + ci + no + co + ns:]
        first = last = None
        for axis, size in enumerate(grid):
            i = pl.program_id(axis)
            first = (i == 0) if first is None else jnp.logical_and(first, i == 0)
            last = (i == size - 1) if last is None else jnp.logical_and(last, i == size - 1)

        @pl.when(first)
        def _():
            comm.start(cin, cout, cscr)

        body(*ins, *outs, *scr)

        @pl.when(last)
        def _():
            comm.finish(cin, cout, cscr)

    f = pl.pallas_call(hosted, name=name, grid=grid, in_specs=in_specs + [ANY] * ci,
                       out_specs=out_specs + [ANY] * co, out_shape=out_shape + comm.out_shapes,
                       scratch_shapes=scratch_shapes + comm.scratch,
                       input_output_aliases={ni + a: no + b for a, b in comm.aliases()},
                       compiler_params=_params(*(("arbitrary",) * len(grid))))

    def run(*args):
        res = f(*args, *comm.arrays)
        return res[:no], res[no:]

    return run


def _gather_half_behind(shards):
    n = len(shards)

    def build(ins, outs, scr):
        send, recv, loc = scr[:3]
        stage = scr[3:]
        x, y, c = _coords()
        me = 2 * x + y
        staged = [_via_vmem(ins[t], outs[t].at[me], stage[t], loc, t) for t in range(n)]
        remote = []
        for t in range(n):
            half = shards[t].shape[0] // 2
            mine = pl.ds(c * half, half)
            for k, (dx, dy) in enumerate(_REL3):
                remote.append(pltpu.make_async_remote_copy(
                    src_ref=ins[t].at[mine], dst_ref=outs[t].at[me, mine],
                    send_sem=send.at[3 * t + k], recv_sem=recv.at[3 * t + k],
                    device_id=(_flip(x, dx), _flip(y, dy), c), device_id_type=MESH))
        return staged, remote

    return _Behind(shards, [jax.ShapeDtypeStruct((NSH,) + s.shape, s.dtype) for s in shards],
                   [pltpu.SemaphoreType.DMA((3 * n,)), pltpu.SemaphoreType.DMA((3 * n,)),
                    pltpu.SemaphoreType.DMA((2 * n,))] + [pltpu.VMEM(s.shape, s.dtype) for s in shards], build)


def _pass_halves(gathered, name, own=()):
    n, m = len(gathered), len(own)

    def body(*refs):
        shards, outs = refs[n:n + m], refs[n + m:2 * n + m]
        send, recv, loc = refs[2 * n + m:2 * n + m + 3]
        stage = refs[2 * n + m + 3:]
        x, y, c = _coords()
        staged = [_via_vmem(shards[t], outs[t].at[2 * x + y], stage[t], loc, t) for t in range(m)]
        remote = []
        for t in range(n):
            half = gathered[t].shape[1] // 2
            mine = pl.ds(c * half, half)
            for k, (dx, dy) in enumerate(_REL3):
                slot = 2 * _flip(x, dx) + _flip(y, dy)
                remote.append(pltpu.make_async_remote_copy(
                    src_ref=outs[t].at[slot, mine], dst_ref=outs[t].at[slot, mine],
                    send_sem=send.at[3 * t + k], recv_sem=recv.at[3 * t + k],
                    device_id=(x, y, 1 - c), device_id_type=MESH))
        _run_staged(staged, remote)

    return pl.pallas_call(
        body, name=name,
        out_shape=[jax.ShapeDtypeStruct(g.shape, g.dtype) for g in gathered],
        in_specs=[ANY] * (n + m), out_specs=[ANY] * n, input_output_aliases={t: t for t in range(n)},
        scratch_shapes=[pltpu.SemaphoreType.DMA((3 * n,)), pltpu.SemaphoreType.DMA((3 * n,)),
                        pltpu.SemaphoreType.DMA((max(2 * m, 1),))] + [pltpu.VMEM(s.shape, s.dtype) for s in own],
        compiler_params=pltpu.CompilerParams(vmem_limit_bytes=VMEM_LIMIT),
    )(*gathered, *own)


def _fill_own(sums, recvs, name):
    n = len(sums)

    def body(*refs):
        ins, outs = refs[:n], refs[2 * n:3 * n]
        loc = refs[3 * n]
        stage = refs[3 * n + 1:]
        x, y, _ = _coords()
        me = 2 * x + y
        _run_staged([_via_vmem(ins[t].at[me], outs[t].at[me], stage[t], loc, t) for t in range(n)], [])

    return pl.pallas_call(
        body, name=name,
        out_shape=[jax.ShapeDtypeStruct(r.shape, r.dtype) for r in recvs],
        in_specs=[ANY] * (2 * n), out_specs=[ANY] * n, input_output_aliases={n + t: t for t in range(n)},
        scratch_shapes=[pltpu.SemaphoreType.DMA((2 * n,))] + [pltpu.VMEM(s.shape[1:], s.dtype) for s in sums],
        compiler_params=pltpu.CompilerParams(vmem_limit_bytes=VMEM_LIMIT),
    )(*sums, *recvs)


def _scatter_chips_behind(sums):
    n = len(sums)

    def build(ins, outs, scr):
        send, recv, loc = scr[:3]
        stage = scr[3:]
        x, y, c = _coords()
        me = 2 * x + y
        staged = [_via_vmem(ins[t].at[me], outs[t].at[me], stage[t], loc, t) for t in range(n)]
        remote = []
        for t in range(n):
            for k, (dx, dy) in enumerate(_REL3):
                px, py = _flip(x, dx), _flip(y, dy)
                remote.append(pltpu.make_async_remote_copy(
                    src_ref=ins[t].at[2 * px + py], dst_ref=outs[t].at[me],
                    send_sem=send.at[3 * t + k], recv_sem=recv.at[3 * t + k],
                    device_id=(px, py, c), device_id_type=MESH))
        return staged, remote

    return _Behind(sums, [jax.ShapeDtypeStruct(s.shape, s.dtype) for s in sums],
                   [pltpu.SemaphoreType.DMA((3 * n,)), pltpu.SemaphoreType.DMA((3 * n,)),
                    pltpu.SemaphoreType.DMA((2 * n,))] + [pltpu.VMEM(s.shape[1:], s.dtype) for s in sums], build)


def _gather_all_behind(a):
    def build(ins, outs, scr):
        send, recv, loc, stage = scr
        x, y, c = _coords()
        me = 4 * x + 2 * y + c
        staged = [_via_vmem(ins[0], outs[0].at[me], stage, loc, 0)]
        remote = [pltpu.make_async_remote_copy(
            src_ref=ins[0], dst_ref=outs[0].at[me], send_sem=send.at[k], recv_sem=recv.at[k],
            device_id=(_flip(x, dx), _flip(y, dy), _flip(c, dc)), device_id_type=MESH)
            for k, (dx, dy, dc) in enumerate(_REL7)]
        return staged, remote

    return _Behind([a], [jax.ShapeDtypeStruct((8,) + a.shape, a.dtype)],
                   [pltpu.SemaphoreType.DMA((7,)), pltpu.SemaphoreType.DMA((7,)), pltpu.SemaphoreType.DMA((2,)),
                    pltpu.VMEM(a.shape, a.dtype)], build)


HBM = pl.BlockSpec(memory_space=pltpu.HBM)
SEM = pl.BlockSpec(memory_space=pltpu.SEMAPHORE)
EFFECT = pltpu.SideEffectType.DATAFLOW_SIDE_EFFECTING


def _chip_copies(srcs, lands, send, recv, gather):
    x, y, c = _coords()
    me = 2 * x + y
    cps = []
    for t in range(len(srcs)):
        for k, (dx, dy) in enumerate(_REL3):
            px, py = _flip(x, dx), _flip(y, dy)
            if gather:
                half = srcs[t].shape[0] // 2
                mine = pl.ds(c * half, half)
                src, dst = srcs[t].at[mine], lands[t].at[me, mine]
            else:
                src, dst = srcs[t].at[2 * px + py], lands[t].at[me]
            cps.append(pltpu.make_async_remote_copy(
                src_ref=src, dst_ref=dst, send_sem=send.at[3 * t + k], recv_sem=recv.at[3 * t + k],
                device_id=(px, py, c), device_id_type=MESH))
    return cps


def _chips_start(arrays, land_shapes, gather, name, after=()):
    n = len(arrays)

    def body(*refs):
        srcs, lands = refs[:n], refs[n:2 * n]
        send, recv = refs[2 * n + len(after)], refs[2 * n + len(after) + 1]
        token = refs[-1]
        for cp in _chip_copies(srcs, lands, send, recv, gather):
            cp.start()
        token[...] = jnp.zeros_like(token)

    lands = [lax.empty(s.shape, s.dtype) for s in land_shapes]
    thru = [pltpu.HBM(a.shape, a.dtype) for a in arrays] + [pltpu.HBM(s.shape, s.dtype) for s in land_shapes]
    res = pl.pallas_call(
        body, name=name,
        out_shape=(pltpu.SemaphoreType.DMA((3 * n,)), pltpu.SemaphoreType.DMA((3 * n,)), *thru,
                   jax.ShapeDtypeStruct((8, 128), F32)),
        in_specs=[HBM] * (2 * n) + [ANY] * len(after),
        out_specs=(SEM, SEM, *([HBM] * (2 * n)), pl.BlockSpec(memory_space=pltpu.VMEM)),
        input_output_aliases={t: 2 + t for t in range(2 * n)},
        compiler_params=pltpu.CompilerParams(has_side_effects=EFFECT),
    )(*[pltpu.with_memory_space_constraint(a, pltpu.HBM) for a in arrays],
      *[pltpu.with_memory_space_constraint(z, pltpu.HBM) for z in lands], *after)
    return res[0], res[1], list(res[2:2 + n]), list(res[2 + n:2 + 2 * n]), res[-1]


def _chips_wait(send, recv, arrays, lands, after, gather, name):
    n = len(arrays)

    def body(*refs):
        srcs, ls = refs[:n], refs[n:2 * n]
        sd, rv = refs[2 * n], refs[2 * n + 1]
        for cp in _chip_copies(srcs, ls, sd, rv, gather):
            cp.wait_send()
            cp.wait_recv()

    res = pl.pallas_call(
        body, name=name,
        out_shape=[pltpu.HBM(a.shape, a.dtype) for a in arrays] + [pltpu.HBM(z.shape, z.dtype) for z in lands],
        in_specs=[HBM] * (2 * n) + [SEM, SEM] + [ANY] * len(after), out_specs=[HBM] * (2 * n),
        input_output_aliases={t: t for t in range(2 * n)},
        compiler_params=pltpu.CompilerParams(has_side_effects=EFFECT),
    )(*arrays, *lands, send, recv, *after)
    return list(res[:n]), list(res[n:])


def _join(*parts):
    def cut(seq, key):
        res, o = [], 0
        for p in parts:
            k = len(getattr(p, key))
            res.append(seq[o:o + k])
            o += k
        return res

    def build(ins, outs, scr):
        staged, remote = [], []
        for p, i, o, s in zip(parts, cut(ins, "arrays"), cut(outs, "out_shapes"), cut(scr, "scratch")):
            st, rm = p.build(i, o, s)
            staged += st
            remote += rm
        return staged, remote

    pairs, ai, oi = [], 0, 0
    for p in parts:
        pairs += [(ai + a, oi + b) for a, b in p.alias_pairs]
        ai, oi = ai + len(p.arrays), oi + len(p.out_shapes)
    return _Behind(sum((p.arrays for p in parts), []), sum((p.out_shapes for p in parts), []),
                   sum((p.scratch for p in parts), []), build, pairs)


def _pass_halves_behind(gathered):
    n = len(gathered)

    def build(ins, outs, scr):
        send, recv = scr
        x, y, c = _coords()
        remote = []
        for t in range(n):
            half = gathered[t].shape[1] // 2
            mine = pl.ds(c * half, half)
            for k, (dx, dy) in enumerate(_REL3):
                slot = 2 * _flip(x, dx) + _flip(y, dy)
                remote.append(pltpu.make_async_remote_copy(
                    src_ref=outs[t].at[slot, mine], dst_ref=outs[t].at[slot, mine],
                    send_sem=send.at[3 * t + k], recv_sem=recv.at[3 * t + k],
                    device_id=(x, y, 1 - c), device_id_type=MESH))
        return [], remote

    return _Behind(gathered, [jax.ShapeDtypeStruct(g.shape, g.dtype) for g in gathered],
                   [pltpu.SemaphoreType.DMA((3 * n,)), pltpu.SemaphoreType.DMA((3 * n,))], build,
                   [(t, t) for t in range(n)])


def _gather_chips(shards, name):
    n = len(shards)

    def body(*refs):
        ins, outs = refs[:n], refs[n:2 * n]
        send, recv, fsend, frecv, loc = refs[2 * n:2 * n + 5]
        stage = refs[2 * n + 5:]
        x, y, c = _coords()
        me = 2 * x + y
        own = [_via_vmem(ins[t], outs[t].at[me], stage[t], loc, t) for t in range(n)]
        first, passed = [], []
        for t in range(n):
            half = shards[t].shape[0] // 2
            mine, theirs = pl.ds(c * half, half), pl.ds((1 - c) * half, half)
            for k, (dx, dy) in enumerate(_REL3):
                px, py = _flip(x, dx), _flip(y, dy)
                first.append(pltpu.make_async_remote_copy(
                    src_ref=ins[t].at[mine], dst_ref=outs[t].at[me, mine],
                    send_sem=send.at[3 * t + k], recv_sem=recv.at[3 * t + k],
                    device_id=(px, py, c), device_id_type=MESH))
                passed.append((
                    pltpu.make_async_remote_copy(
                        src_ref=outs[t].at[2 * px + py, mine], dst_ref=outs[t].at[2 * px + py, mine],
                        send_sem=fsend.at[3 * t + k], recv_sem=frecv.at[3 * t + k],
                        device_id=(x, y, 1 - c), device_id_type=MESH),
                    pltpu.make_async_remote_copy(
                        src_ref=outs[t].at[2 * px + py, theirs], dst_ref=outs[t].at[2 * px + py, theirs],
                        send_sem=fsend.at[3 * t + k], recv_sem=frecv.at[3 * t + k],
                        device_id=(x, y, 1 - c), device_id_type=MESH)))
        for load, _ in own:
            load.start()
        for cp in first:
            cp.start()
        for load, store in own:
            load.wait()
            store.start()
        for cp, (fwd, _) in zip(first, passed):
            cp.wait_recv()
            fwd.start()
        for cp, (fwd, back) in zip(first, passed):
            cp.wait_send()
            fwd.wait_send()
            back.wait_recv()
        for _, store in own:
            store.wait()

    return pl.pallas_call(
        body, name=name,
        out_shape=[jax.ShapeDtypeStruct((NSH,) + s.shape, s.dtype) for s in shards],
        in_specs=[ANY] * n, out_specs=[ANY] * n,
        scratch_shapes=[pltpu.SemaphoreType.DMA((3 * n,)) for _ in range(4)] + [pltpu.SemaphoreType.DMA((2 * n,))]
        + [pltpu.VMEM(s.shape, s.dtype) for s in shards],
        compiler_params=pltpu.CompilerParams(vmem_limit_bytes=VMEM_LIMIT),
    )(*shards)


_REL7 = tuple((dx, dy, dc) for dx in (0, 1) for dy in (0, 1) for dc in (0, 1))[1:]


def _gather_all(a, name):
    def body(a_ref, o_ref, send, recv, loc):
        x, y, c = _coords()
        me = 4 * x + 2 * y + c
        local = [pltpu.make_async_copy(a_ref, o_ref.at[me], loc.at[0])]
        remote = [pltpu.make_async_remote_copy(
            src_ref=a_ref, dst_ref=o_ref.at[me], send_sem=send.at[k], recv_sem=recv.at[k],
            device_id=(_flip(x, dx), _flip(y, dy), _flip(c, dc)), device_id_type=MESH)
            for k, (dx, dy, dc) in enumerate(_REL7)]
        _run(local, remote)

    return pl.pallas_call(
        body, name=name,
        out_shape=jax.ShapeDtypeStruct((8,) + a.shape, a.dtype),
        in_specs=[ANY], out_specs=ANY,
        scratch_shapes=[pltpu.SemaphoreType.DMA((7,)), pltpu.SemaphoreType.DMA((7,)),
                        pltpu.SemaphoreType.DMA((1,))],
    )(a)


def _pair_exchange(grads, name):
    n = len(grads)

    def body(*refs):
        ins, outs = refs[:n], refs[n:2 * n]
        send, recv = refs[2 * n:]
        x, y, c = _coords()
        remote = []
        for t in range(n):
            half = grads[t].shape[1] // 2
            remote.append(pltpu.make_async_remote_copy(
                src_ref=ins[t].at[:, pl.ds((1 - c) * half, half)], dst_ref=outs[t],
                send_sem=send.at[t], recv_sem=recv.at[t],
                device_id=(x, y, 1 - c), device_id_type=MESH))
        _run([], remote)

    return pl.pallas_call(
        body, name=name,
        out_shape=[jax.ShapeDtypeStruct((NSH, g.shape[1] // 2, g.shape[2]), g.dtype) for g in grads],
        in_specs=[ANY] * n, out_specs=[ANY] * n,
        scratch_shapes=[pltpu.SemaphoreType.DMA((n,)), pltpu.SemaphoreType.DMA((n,))],
    )(*grads)


def _scatter_chips(sums, name):
    n = len(sums)

    def body(*refs):
        ins, outs = refs[:n], refs[n:2 * n]
        send, recv, loc = refs[2 * n:2 * n + 3]
        stage = refs[2 * n + 3:]
        x, y, c = _coords()
        me = 2 * x + y
        local = [_via_vmem(ins[t].at[me], outs[t].at[me], stage[t], loc, t) for t in range(n)]
        remote = []
        for t in range(n):
            for k, (dx, dy) in enumerate(_REL3):
                px, py = _flip(x, dx), _flip(y, dy)
                remote.append(pltpu.make_async_remote_copy(
                    src_ref=ins[t].at[2 * px + py], dst_ref=outs[t].at[me],
                    send_sem=send.at[3 * t + k], recv_sem=recv.at[3 * t + k],
                    device_id=(px, py, c), device_id_type=MESH))
        _run_staged(local, remote)

    return pl.pallas_call(
        body, name=name,
        out_shape=[jax.ShapeDtypeStruct(s.shape, s.dtype) for s in sums],
        in_specs=[ANY] * n, out_specs=[ANY] * n,
        scratch_shapes=[pltpu.SemaphoreType.DMA((3 * n,)), pltpu.SemaphoreType.DMA((3 * n,)),
                        pltpu.SemaphoreType.DMA((2 * n,))]
        + [pltpu.VMEM(s.shape[1:], s.dtype) for s in sums],
        compiler_params=pltpu.CompilerParams(vmem_limit_bytes=VMEM_LIMIT),
    )(*sums)


def _swap_halves(halves, name):
    n = len(halves)

    def body(*refs):
        ins, outs = refs[:n], refs[n:2 * n]
        send, recv, loc = refs[2 * n:2 * n + 3]
        stage = refs[2 * n + 3:]
        x, y, c = _coords()
        local = [_via_vmem(ins[t], outs[t].at[c], stage[t], loc, t) for t in range(n)]
        remote = [pltpu.make_async_remote_copy(
            src_ref=ins[t], dst_ref=outs[t].at[c], send_sem=send.at[t], recv_sem=recv.at[t],
            device_id=(x, y, 1 - c), device_id_type=MESH) for t in range(n)]
        _run_staged(local, remote)

    return pl.pallas_call(
        body, name=name,
        out_shape=[jax.ShapeDtypeStruct((2,) + h.shape, h.dtype) for h in halves],
        in_specs=[ANY] * n, out_specs=[ANY] * n,
        scratch_shapes=[pltpu.SemaphoreType.DMA((n,)), pltpu.SemaphoreType.DMA((n,)),
                        pltpu.SemaphoreType.DMA((2 * n,))]
        + [pltpu.VMEM(h.shape, h.dtype) for h in halves],
        compiler_params=pltpu.CompilerParams(vmem_limit_bytes=VMEM_LIMIT),
    )(*halves)


def _sum_slots(r, name, after=None):
    K, R, C = r.shape
    tr = _tile(R, max(16, (1 << 22) // (K * C)), 8 * (4 // r.dtype.itemsize))

    def body(r_ref, *rest):
        o_ref = rest[-1]
        acc = r_ref[0].astype(F32)
        for k in range(1, K):
            acc = acc + r_ref[k].astype(F32)
        o_ref[...] = acc

    dep = [] if after is None else [after]
    return pl.pallas_call(
        body, name=name, grid=(R // tr,),
        out_shape=jax.ShapeDtypeStruct((R, C), F32),
        in_specs=[pl.BlockSpec((K, tr, C), lambda i: (0, i, 0))] + [ANY] * len(dep),
        out_specs=pl.BlockSpec((tr, C), lambda i: (i, 0)),
        compiler_params=_params("parallel"),
    )(r, *dep)


def _add_pair(g, s, core, name):
    _, half, C = s.shape
    tr = _tile(half, max(16, (1 << 19) // C))
    nb = half // tr

    def body(c_ref, g_ref, s_ref, o_ref):
        o_ref[...] = (g_ref[...].astype(F32) + s_ref[...].astype(F32)).astype(BF16)

    spec = pl.BlockSpec((1, tr, C), lambda j, i, c_ref: (j, i, 0))
    return pl.pallas_call(
        body, name=name,
        grid_spec=pltpu.PrefetchScalarGridSpec(
            num_scalar_prefetch=1, grid=(NSH, nb),
            in_specs=[pl.BlockSpec((1, tr, C), lambda j, i, c_ref: (j, c_ref[0] * nb + i, 0)), spec],
            out_specs=spec),
        out_shape=jax.ShapeDtypeStruct(s.shape, BF16),
        compiler_params=_params("parallel", "parallel"),
    )(core, g, s)


def _adamw(w, g, m, v, name):
    _, R, C = w.shape
    tr = _tile(R, max(8, (1 << 18) // C), 8)
    c1 = 1.0 / (1.0 - B1 ** STEP)
    c2 = 1.0 / (1.0 - B2 ** STEP)

    def body(w_ref, g_ref, m_ref, v_ref, go_ref, d_ref, nm_ref, nv_ref):
        gv = g_ref[...]
        go_ref[...] = gv
        nm = B1 * m_ref[...] + (1.0 - B1) * gv
        nv = B2 * v_ref[...] + (1.0 - B2) * gv * gv
        nm_ref[...] = nm
        nv_ref[...] = nv
        d_ref[...] = -LR * ((nm * c1) / (jnp.sqrt(nv * c2) + AEPS) + WD * w_ref[...])

    spec = pl.BlockSpec((1, tr, C), lambda i: (0, i, 0))
    return pl.pallas_call(
        body, name=name, grid=(R // tr,),
        out_shape=[jax.ShapeDtypeStruct((1, R, C), F32)] * 4,
        in_specs=[spec] * 4, out_specs=[spec] * 4,
        compiler_params=_params("parallel"),
    )(w, g, m, v)


def _ffn_fwd(h, g, w1, w3, w2, name, comm=None):
    L = h.shape[0]
    tm = _tile(L, 704)

    def body(h_ref, g_ref, w1_ref, w3_ref, w2_ref, o_ref, a_ref, b_ref, n_s, acc_s):
        j = pl.program_id(1)

        @pl.when(j == 0)
        def _():
            hv = h_ref[...]
            n, _ = _rms(hv, g_ref[...])
            n_s[...] = n.astype(BF16)
            acc_s[...] = hv

        n = n_s[...]
        a = _dot_nt(n, w1_ref[0])
        b = _dot_nt(n, w3_ref[0])
        a_ref[0] = a.astype(BF16)
        b_ref[0] = b.astype(BF16)
        s = (a * _sigmoid(a) * b).astype(BF16)
        acc_s[...] += 0.5 * _dot(s, w2_ref[0])

        @pl.when(j == NSH - 1)
        def _():
            o_ref[...] = acc_s[...]

    row = pl.BlockSpec((tm, D), lambda i, j: (i, 0))
    hid = pl.BlockSpec((1, tm, FS), lambda i, j: (j, i, 0))
    wsp = pl.BlockSpec((1, FS, D), lambda i, j: (j, 0, 0))
    return _call(
        body, comm, name=name, grid=(L // tm, NSH),
        out_shape=[jax.ShapeDtypeStruct((L, D), F32),
                   jax.ShapeDtypeStruct((NSH, L, FS), BF16), jax.ShapeDtypeStruct((NSH, L, FS), BF16)],
        in_specs=[row, _res((1, D)), wsp, wsp, wsp],
        out_specs=[row, hid, hid],
        scratch_shapes=[pltpu.VMEM((tm, D), BF16), pltpu.VMEM((tm, D), F32)],
        params=_params("arbitrary", "arbitrary"),
    )(h, g, w1, w3, w2)


def _loss_head(hv, gv, tv, row0):
    y, r = _rms(hv, gv)
    row = row0 + lax.broadcasted_iota(jnp.int32, (hv.shape[0], 1), 0)
    e = jnp.where(row >= FRONT, y - tv, 0.0)
    dy = e * (1.0 / D)
    part = 0.5 * jnp.sum(jnp.sum(e * dy, axis=1, keepdims=True), axis=0, keepdims=True)
    dx, xh = _rms_bwd(dy, hv, r, gv)
    return dx, part, jnp.sum(dy * xh, axis=0, keepdims=True)


def _ffn_fwd_loss(h, g, w1, w3, w2, gf, tgt, name):
    L = h.shape[0]
    tm = _tile(L, 704)

    def body(h_ref, g_ref, w1_ref, w3_ref, w2_ref, gf_ref, t_ref, o_ref, a_ref, b_ref, loss_ref, dgf_ref,
             n_s, acc_s):
        i, j = pl.program_id(0), pl.program_id(1)

        @pl.when(j == 0)
        def _():
            hv = h_ref[...]
            n, _ = _rms(hv, g_ref[...])
            n_s[...] = n.astype(BF16)
            acc_s[...] = hv

        n = n_s[...]
        a = _dot_nt(n, w1_ref[0])
        b = _dot_nt(n, w3_ref[0])
        a_ref[0] = a.astype(BF16)
        b_ref[0] = b.astype(BF16)
        s = (a * _sigmoid(a) * b).astype(BF16)
        acc_s[...] += 0.5 * _dot(s, w2_ref[0])

        @pl.when(j == NSH - 1)
        def _():
            dx, part, dgf = _loss_head(acc_s[...], gf_ref[...], t_ref[...], i * tm)
            o_ref[...] = dx
            _acc_rows(loss_ref, part, i == 0)
            _acc_rows(dgf_ref, dgf, i == 0)

    row = pl.BlockSpec((tm, D), lambda i, j: (i, 0))
    hid = pl.BlockSpec((1, tm, FS), lambda i, j: (j, i, 0))
    wsp = pl.BlockSpec((1, FS, D), lambda i, j: (j, 0, 0))
    return pl.pallas_call(
        body, name=name, grid=(L // tm, NSH),
        out_shape=[jax.ShapeDtypeStruct((L, D), F32),
                   jax.ShapeDtypeStruct((NSH, L, FS), BF16), jax.ShapeDtypeStruct((NSH, L, FS), BF16),
                   jax.ShapeDtypeStruct((1, 1), F32), jax.ShapeDtypeStruct((1, D), F32)],
        in_specs=[row, _res((1, D)), wsp, wsp, wsp, _res((1, D)), row],
        out_specs=[row, hid, hid, pl.BlockSpec((1, 1), lambda i, j: (0, 0)),
                   pl.BlockSpec((1, D), lambda i, j: (0, 0))],
        scratch_shapes=[pltpu.VMEM((tm, D), BF16), pltpu.VMEM((tm, D), F32)],
        compiler_params=_params("arbitrary", "arbitrary"),
    )(h, g, w1, w3, w2, gf, tgt)


def _ffn_bwd(h, g, dout, a, b, w1, w3, w2, name, comm=None):
    L = h.shape[0]
    tm = _tile(L, 528)

    def body(h_ref, g_ref, do_ref, a_ref, b_ref, w1_ref, w3_ref, w2_ref,
             dh_ref, da_ref, db_ref, s_ref, n_ref, dg_ref, dob_s, dn_s):
        i, j = pl.program_id(0), pl.program_id(1)

        @pl.when(j == 0)
        def _():
            n, _ = _rms(h_ref[...], g_ref[...])
            n_ref[...] = n.astype(BF16)
            dob_s[...] = (0.5 * do_ref[...]).astype(BF16)
            dn_s[...] = jnp.zeros_like(dn_s)

        av = a_ref[0].astype(F32)
        bv = b_ref[0].astype(F32)
        sig = _sigmoid(av)
        sa = av * sig
        ds = _dot_nt(dob_s[...], w2_ref[0])
        s_ref[0] = (sa * bv).astype(BF16)
        da = (ds * bv * (sig + sa * (1.0 - sig))).astype(BF16)
        db = (ds * sa).astype(BF16)
        da_ref[0] = da
        db_ref[0] = db
        dn_s[...] += _dot(da, w1_ref[0]) + _dot(db, w3_ref[0])

        @pl.when(j == NSH - 1)
        def _():
            hv = h_ref[...]
            gv = g_ref[...]
            r = lax.rsqrt(jnp.mean(hv * hv, axis=-1, keepdims=True) + EPS)
            dn = dn_s[...]
            dx, xh = _rms_bwd(dn, hv, r, gv)
            dh_ref[...] = do_ref[...] + dx
            _acc_rows(dg_ref, jnp.sum(dn * xh, axis=0, keepdims=True), i == 0)

    row = pl.BlockSpec((tm, D), lambda i, j: (i, 0))
    hid = pl.BlockSpec((1, tm, FS), lambda i, j: (j, i, 0))
    wsp = pl.BlockSpec((1, FS, D), lambda i, j: (j, 0, 0))
    return _call(
        body, comm, name=name, grid=(L // tm, NSH),
        out_shape=[jax.ShapeDtypeStruct((L, D), F32)]
        + [jax.ShapeDtypeStruct((NSH, L, FS), BF16)] * 3
        + [jax.ShapeDtypeStruct((L, D), BF16), jax.ShapeDtypeStruct((1, D), F32)],
        in_specs=[row, _res((1, D)), row, hid, hid,
                  wsp, wsp, wsp],
        out_specs=[row, hid, hid, hid, row, pl.BlockSpec((1, D), lambda i, j: (0, 0))],
        scratch_shapes=[pltpu.VMEM((tm, D), BF16), pltpu.VMEM((tm, D), F32)],
        params=_params("arbitrary", "arbitrary"),
    )(h, g, dout, a, b, w1, w3, w2)


def _wgrad(xm, ym, name, scale=1.0):
    xs, ys = xm.ndim == 3, ym.ndim == 3
    assert not (xs and ys)
    L = xm.shape[-2]
    K, N = xm.shape[-1], ym.shape[-1]
    tl = _tile(L, 2112)
    nl = L // tl
    if xs or ys:
        tn, grid_n = N, NSH
    else:
        tn = _tile(N, 1024, 128)
        grid_n = N // tn

    def body(x_ref, y_ref, o_ref, acc_s):
        l = pl.program_id(1)
        xv = x_ref[0] if xs else x_ref[...]
        yv = y_ref[0] if ys else y_ref[...]
        part = _dot_tn(xv.astype(BF16), yv.astype(BF16))
        _acc_rows(acc_s, part, l == 0)

        @pl.when(l == nl - 1)
        def _():
            res = (acc_s[...] * scale).astype(BF16)
            if xs or ys:
                o_ref[0] = res
            else:
                o_ref[...] = res

    if xs:
        x_spec = pl.BlockSpec((1, tl, K), lambda n, l: (n, l, 0))
        y_spec = pl.BlockSpec((tl, N), lambda n, l: (l, 0))
        o_spec = pl.BlockSpec((1, K, N), lambda n, l: (n, 0, 0))
        o_shape = (NSH, K, N)
    elif ys:
        x_spec = pl.BlockSpec((tl, K), lambda n, l: (l, 0))
        y_spec = pl.BlockSpec((1, tl, N), lambda n, l: (n, l, 0))
        o_spec = pl.BlockSpec((1, K, N), lambda n, l: (n, 0, 0))
        o_shape = (NSH, K, N)
    else:
        x_spec = pl.BlockSpec((tl, K), lambda n, l: (l, 0))
        y_spec = pl.BlockSpec((tl, tn), lambda n, l: (l, n))
        o_spec = pl.BlockSpec((K, tn), lambda n, l: (0, n))
        o_shape = (K, N)
    return pl.pallas_call(
        body, name=name, grid=(grid_n, nl),
        out_shape=jax.ShapeDtypeStruct(o_shape, BF16),
        in_specs=[x_spec, y_spec], out_specs=o_spec,
        scratch_shapes=[pltpu.VMEM((K, tn), F32)],
        compiler_params=_params("parallel", "arbitrary"),
    )(xm, ym)


def _mix_in_fwd(h, g, w_in, b_gate, name, comm=None):
    L = h.shape[0]
    tm = _tile(L, 528)

    def body(h_ref, g_ref, w_ref, bg_ref, vg_ref, uf_ref, gt_ref):
        u, _ = _rms(h_ref[...], g_ref[...])
        ub = u.astype(BF16)
        p = [_dot(ub, w_ref[j]) for j in range(NSH)]
        a0, a1 = 2 * DC - WS, 2 * DC + DS - WS
        vg_ref[:, 0:WS] = p[0].astype(BF16)
        vg_ref[:, WS:2 * DC] = p[1][:, 0:a0].astype(BF16)
        uf_ref[...] = p[1][:, a0:a1].astype(BF16)
        gin = jnp.concatenate([p[1][:, a1:], p[2], p[3]], axis=1)
        gt_ref[...] = _sigmoid(gin + bg_ref[...]).astype(BF16)

    def row(n):
        return pl.BlockSpec((tm, n), lambda i: (i, 0))

    return _call(
        body, comm, name=name, grid=(L // tm,),
        out_shape=[jax.ShapeDtypeStruct((L, 2 * DC), BF16), jax.ShapeDtypeStruct((L, DS), BF16),
                   jax.ShapeDtypeStruct((L, 2 * D), BF16)],
        in_specs=[row(D), _res((1, D)), _res((NSH, D, WS)), _res((1, 2 * D))],
        out_specs=[row(2 * DC), row(DS), row(2 * D)],
        params=_params("parallel"),
    )(h, g, w_in, b_gate)


def _mix_in_bwd(h, g, dres, dv, dgl, duf, dgate, w_in, name):
    L = h.shape[0]
    tm = _tile(L, 528)

    def body(h_ref, g_ref, dr_ref, dv_ref, dgl_ref, duf_ref, dgt_ref, w_ref, dh_ref, u_ref, dp_ref, dgm_ref):
        i = pl.program_id(0)
        hv = h_ref[...]
        gv = g_ref[...]
        u, r = _rms(hv, gv)
        u_ref[...] = u.astype(BF16)
        a0, a1 = 2 * DC - WS, 2 * DC + DS - WS
        b0 = WS - a1
        dp = [jnp.concatenate([dv_ref[...], dgl_ref[:, 0:WS - DC]], axis=1),
              jnp.concatenate([dgl_ref[:, WS - DC:], duf_ref[...], dgt_ref[:, 0:b0]], axis=1),
              dgt_ref[:, b0:b0 + WS], dgt_ref[:, b0 + WS:]]
        du = jnp.zeros((tm, D), F32)
        for j in range(NSH):
            dp_ref[j] = dp[j]
            du = du + _dot_nt(dp[j], w_ref[j])
        dx, xh = _rms_bwd(du, hv, r, gv)
        dh_ref[...] = dr_ref[...] + dx
        _acc_rows(dgm_ref, jnp.sum(du * xh, axis=0, keepdims=True), i == 0)

    def row(n):
        return pl.BlockSpec((tm, n), lambda i: (i, 0))

    return pl.pallas_call(
        body, name=name, grid=(L // tm,),
        out_shape=[jax.ShapeDtypeStruct((L, D), F32), jax.ShapeDtypeStruct((L, D), BF16),
                   jax.ShapeDtypeStruct((NSH, L, WS), BF16), jax.ShapeDtypeStruct((1, D), F32)],
        in_specs=[row(D), _res((1, D)), row(D), row(DC), row(DC), row(DS), row(2 * D), _res((NSH, D, WS))],
        out_specs=[row(D), row(D), pl.BlockSpec((NSH, tm, WS), lambda i: (0, i, 0)),
                   pl.BlockSpec((1, D), lambda i: (0, 0))],
        compiler_params=_params("arbitrary"),
    )(h, g, dres, dv, dgl, duf, dgate, w_in)


def _conv_fwd(vg, dw, dwb, name, comm=None):
    L = vg.shape[0]
    nc = DC // 128

    def body(v_ref, g_ref, dw_ref, dwb_ref, z_ref, zp_s):
        zp_s[0:KWP, :] = jnp.zeros((KWP, 128), F32)
        zp_s[KWP:, :] = v_ref[...].astype(F32) * _sigmoid(g_ref[...].astype(F32))
        for r0 in range(0, L, CONV_ROWS):
            acc = jnp.broadcast_to(dwb_ref[...], (CONV_ROWS, 128))
            for k in range(KW):
                acc = acc + dw_ref[k:k + 1, :] * zp_s[pl.ds(r0 + k + 2, CONV_ROWS), :]
            z_ref[pl.ds(r0, CONV_ROWS), :] = acc

    return _call(
        body, comm, name=name, grid=(nc,),
        out_shape=[jax.ShapeDtypeStruct((L, DC), F32)],
        in_specs=[pl.BlockSpec((L, 128), lambda c: (0, c)), pl.BlockSpec((L, 128), lambda c: (0, nc + c)),
                  pl.BlockSpec((KWP, 128), lambda c: (0, c)), pl.BlockSpec((1, 128), lambda c: (0, c))],
        out_specs=[pl.BlockSpec((L, 128), lambda c: (0, c))],
        scratch_shapes=[pltpu.VMEM((L + KWP, 128), F32)],
        params=_params("parallel"),
    )(vg, vg, dw, dwb)


def _conv_bwd(dz1, vg, dw, name):
    L = vg.shape[0]
    nc = DC // 128

    def body(dz_ref, v_ref, g_ref, dw_ref, dv_ref, dg_ref, ddw_ref, ddwb_ref, zp_s, dzp_s):
        vv = v_ref[...].astype(F32)
        sg = _sigmoid(g_ref[...].astype(F32))
        zp_s[0:KWP, :] = jnp.zeros((KWP, 128), F32)
        zp_s[KWP:, :] = vv * sg
        dz = dz_ref[...]
        dzp_s[0:L, :] = dz
        dzp_s[L:, :] = jnp.zeros((KWP, 128), F32)
        ddwb_ref[...] = jnp.sum(dz, axis=0, keepdims=True)
        part = [jnp.zeros((8, 128), F32) for _ in range(KW)]
        for r0 in range(0, L, CONV_ROWS):
            rows = pl.ds(r0, CONV_ROWS)
            dzc = dz_ref[rows, :]
            acc = jnp.zeros((CONV_ROWS, 128), F32)
            for k in range(KW):
                acc = acc + dw_ref[k:k + 1, :] * dzp_s[pl.ds(r0 + KW - 1 - k, CONV_ROWS), :]
                prod = dzc * zp_s[pl.ds(r0 + k + 2, CONV_ROWS), :]
                for q in range(CONV_ROWS // 8):
                    part[k] = part[k] + prod[8 * q:8 * (q + 1), :]
            vc = v_ref[rows, :].astype(F32)
            sc = _sigmoid(g_ref[rows, :].astype(F32))
            dv_ref[rows, :] = (acc * sc).astype(BF16)
            dg_ref[rows, :] = (acc * vc * sc * (1.0 - sc)).astype(BF16)
        for k in range(KW):
            ddw_ref[k:k + 1, :] = jnp.sum(part[k], axis=0, keepdims=True)
        ddw_ref[KW:KWP, :] = jnp.zeros((KWP - KW, 128), F32)

    col = pl.BlockSpec((L, 128), lambda c: (0, c))
    return pl.pallas_call(
        body, name=name, grid=(nc,),
        out_shape=[jax.ShapeDtypeStruct((L, DC), BF16), jax.ShapeDtypeStruct((L, DC), BF16),
                   jax.ShapeDtypeStruct((KWP, DC), F32), jax.ShapeDtypeStruct((1, DC), F32)],
        in_specs=[col, col, pl.BlockSpec((L, 128), lambda c: (0, nc + c)),
                  pl.BlockSpec((KWP, 128), lambda c: (0, c))],
        out_specs=[col, col, pl.BlockSpec((KWP, 128), lambda c: (0, c)), pl.BlockSpec((1, 128), lambda c: (0, c))],
        scratch_shapes=[pltpu.VMEM((L + KWP, 128), F32), pltpu.VMEM((L + KWP, 128), F32)],
        compiler_params=_params("parallel"),
    )(dz1, vg, vg, dw)


NLB = QS // 128


def _lb_store(ref, rows, val):
    for cb in range(NLB):
        ref[cb, rows, :] = val[:, cb * 128:(cb + 1) * 128]


def _lb_load(ref, rows):
    return jnp.concatenate([ref[cb, rows, :] for cb in range(NLB)], axis=1)


def _scan(xr_ref, xi_ref, base, T, ar, ai, atr, ati, reverse):
    W = ar.shape[1]
    ar, ai, atr, ati = (jnp.broadcast_to(v, (8, W)) for v in (ar, ai, atr, ati))
    zero = jnp.zeros((8, W), F32)

    def rows(t, g):
        tt = T - 1 - t if reverse else t
        return pl.ds(base + g * 8 * T + tt, 8, stride=T)

    def make_step(store):
        def step(t, carry):
            out = []
            for g in range(NGRP):
                sr, si = carry[2 * g], carry[2 * g + 1]
                idx = rows(t, g)
                nr = ar * sr - ai * si + _lb_load(xr_ref, idx)
                ni = ar * si + ai * sr + _lb_load(xi_ref, idx)
                if store:
                    _lb_store(xr_ref, idx, nr)
                    _lb_store(xi_ref, idx, ni)
                out += [nr, ni]
            return tuple(out)
        return step

    ends = lax.fori_loop(0, T, make_step(False), (zero,) * (2 * NGRP))
    sub = lax.broadcasted_iota(jnp.int32, (8, W), 0)
    edge = sub == (7 if reverse else 0)
    shift, last = (7, 0) if reverse else (1, 7)
    inr, ini = jnp.zeros((1, W), F32), jnp.zeros((1, W), F32)
    starts = [None] * (2 * NGRP)
    for g in (reversed(range(NGRP)) if reverse else range(NGRP)):
        er, ei = ends[2 * g], ends[2 * g + 1]
        cr, ci = jnp.where(edge, inr, 0.0), jnp.where(edge, ini, 0.0)
        for _ in range(7):
            nr = atr * cr - ati * ci + er
            ni = atr * ci + ati * cr + ei
            cr = jnp.where(edge, inr, pltpu.roll(nr, shift, 0))
            ci = jnp.where(edge, ini, pltpu.roll(ni, shift, 0))
        starts[2 * g], starts[2 * g + 1] = cr, ci
        inr = (atr * cr - ati * ci + er)[last:last + 1]
        ini = (atr * ci + ati * cr + ei)[last:last + 1]
    lax.fori_loop(0, T, make_step(True), tuple(starts))


def _ssm_fwd(uf, bre, bim, cre, cim, lamp, dsk, name, comm=None):
    L = uf.shape[0]
    T = L // NSEG
    tc = L // NCH

    def body(u_ref, bre_ref, bim_ref, cre_ref, cim_ref, lam_ref, d_ref, y_ref, sr_s, si_s):
        for k in range(NCH):
            sl = slice(k * tc, (k + 1) * tc)
            uk = u_ref[sl, :]
            _lb_store(sr_s, sl, _dot(uk, bre_ref[...]))
            _lb_store(si_s, sl, _dot(uk, bim_ref[...]))
        _scan(sr_s, si_s, 0, T, lam_ref[0:1, :], lam_ref[1:2, :], lam_ref[2:3, :], lam_ref[3:4, :], False)
        for k in range(NCH):
            sl = slice(k * tc, (k + 1) * tc)
            y_ref[sl, :] = (_dot(_lb_load(sr_s, sl).astype(BF16), cre_ref[...])
                            - _dot(_lb_load(si_s, sl).astype(BF16), cim_ref[...])
                            + d_ref[...] * u_ref[sl, :].astype(F32))

    return _call(
        body, comm, name=name, grid=(NQ,),
        out_shape=[jax.ShapeDtypeStruct((L, DS), F32)],
        in_specs=[pl.BlockSpec((L, QU), lambda q: (0, q)),
                  pl.BlockSpec((QU, QS), lambda q: (q, q)), pl.BlockSpec((QU, QS), lambda q: (q, q)),
                  pl.BlockSpec((QS, QU), lambda q: (q, q)), pl.BlockSpec((QS, QU), lambda q: (q, q)),
                  pl.BlockSpec((8, QS), lambda q: (0, q)), pl.BlockSpec((1, QU), lambda q: (0, q))],
        out_specs=[pl.BlockSpec((L, QU), lambda q: (0, q))],
        scratch_shapes=[pltpu.VMEM((NLB, L, 128), F32), pltpu.VMEM((NLB, L, 128), F32)],
        params=_params("parallel"),
    )(uf, bre, bim, cre, cim, lamp, dsk)


def _ssm_bwd(uf, dyss, bre, bim, cre, cim, lamp, dsk, name, comm=None):
    L = uf.shape[0]
    T = L // NSEG
    tc = L // NCH

    def body(u_ref, dy_ref, bre_ref, bim_ref, cre_ref, cim_ref, lam_ref, d_ref,
             du_ref, dbre_ref, dbim_ref, dcre_ref, dcim_ref, dlam_ref, dd_ref, sr_s, si_s, gr_s, gi_s):
        _lb_store(sr_s, slice(0, SOFF), jnp.zeros((SOFF, QS), F32))
        _lb_store(si_s, slice(0, SOFF), jnp.zeros((SOFF, QS), F32))
        for k in range(NCH):
            sl = slice(k * tc, (k + 1) * tc)
            ss = slice(SOFF + k * tc, SOFF + (k + 1) * tc)
            uk = u_ref[sl, :]
            dyk = dy_ref[sl, :].astype(BF16)
            _lb_store(sr_s, ss, _dot(uk, bre_ref[...]))
            _lb_store(si_s, ss, _dot(uk, bim_ref[...]))
            _lb_store(gr_s, sl, _dot_nt(dyk, cre_ref[...]))
            _lb_store(gi_s, sl, -_dot_nt(dyk, cim_ref[...]))
        ar, ai, atr, ati = lam_ref[0:1, :], lam_ref[1:2, :], lam_ref[2:3, :], lam_ref[3:4, :]
        _scan(sr_s, si_s, SOFF, T, ar, ai, atr, ati, False)
        _scan(gr_s, gi_s, 0, T, ar, -ai, atr, -ati, True)
        dbre = jnp.zeros((QU, QS), F32)
        dbim = jnp.zeros((QU, QS), F32)
        dcre = jnp.zeros((QS, QU), F32)
        dcim = jnp.zeros((QS, QU), F32)
        dd = jnp.zeros((1, QU), F32)
        qr = jnp.zeros((1, QS), F32)
        qi = jnp.zeros((1, QS), F32)
        for k in range(NCH):
            sl = slice(k * tc, (k + 1) * tc)
            ss = slice(SOFF + k * tc, SOFF + (k + 1) * tc)
            sp = slice(SOFF - 1 + k * tc, SOFF - 1 + (k + 1) * tc)
            uk = u_ref[sl, :]
            dyk = dy_ref[sl, :]
            dyb = dyk.astype(BF16)
            gr, gi = _lb_load(gr_s, sl), _lb_load(gi_s, sl)
            pr, pi = _lb_load(sr_s, sp), _lb_load(si_s, sp)
            qr = qr + jnp.sum(gr * pr + gi * pi, axis=0, keepdims=True)
            qi = qi + jnp.sum(gi * pr - gr * pi, axis=0, keepdims=True)
            grb, gib = gr.astype(BF16), gi.astype(BF16)
            du_ref[sl, :] = (_dot_nt(grb, bre_ref[...]) + _dot_nt(gib, bim_ref[...])
                             + dyk * d_ref[...]).astype(BF16)
            dbre = dbre + _dot_tn(uk, grb)
            dbim = dbim + _dot_tn(uk, gib)
            dcre = dcre + _dot_tn(_lb_load(sr_s, ss).astype(BF16), dyb)
            dcim = dcim - _dot_tn(_lb_load(si_s, ss).astype(BF16), dyb)
            dd = dd + jnp.sum(dyk * uk.astype(F32), axis=0, keepdims=True)
        dlam_ref[0] = jnp.concatenate([qr, qi, jnp.zeros((6, QS), F32)], axis=0)
        dbre_ref[0] = dbre
        dbim_ref[0] = dbim
        dcre_ref[0] = dcre
        dcim_ref[0] = dcim
        dd_ref[...] = dd

    col = pl.BlockSpec((L, QU), lambda q: (0, q))
    bsp = pl.BlockSpec((QU, QS), lambda q: (q, q))
    csp = pl.BlockSpec((QS, QU), lambda q: (q, q))
    return _call(
        body, comm, name=name, grid=(NQ,),
        out_shape=[jax.ShapeDtypeStruct((L, DS), BF16),
                   jax.ShapeDtypeStruct((NQ, QU, QS), F32), jax.ShapeDtypeStruct((NQ, QU, QS), F32),
                   jax.ShapeDtypeStruct((NQ, QS, QU), F32), jax.ShapeDtypeStruct((NQ, QS, QU), F32),
                   jax.ShapeDtypeStruct((NQ, 8, QS), F32), jax.ShapeDtypeStruct((1, DS), F32)],
        in_specs=[col, col, bsp, bsp, csp, csp,
                  pl.BlockSpec((8, QS), lambda q: (0, q)), pl.BlockSpec((1, QU), lambda q: (0, q))],
        out_specs=[col,
                   pl.BlockSpec((1, QU, QS), lambda q: (q, 0, 0)), pl.BlockSpec((1, QU, QS), lambda q: (q, 0, 0)),
                   pl.BlockSpec((1, QS, QU), lambda q: (q, 0, 0)), pl.BlockSpec((1, QS, QU), lambda q: (q, 0, 0)),
                   pl.BlockSpec((1, 8, QS), lambda q: (q, 0, 0)), pl.BlockSpec((1, QU), lambda q: (0, q))],
        scratch_shapes=[pltpu.VMEM((NLB, L + SOFF, 128), F32), pltpu.VMEM((NLB, L + SOFF, 128), F32),
                        pltpu.VMEM((NLB, L, 128), F32), pltpu.VMEM((NLB, L, 128), F32)],
        params=_params("parallel"),
    )(uf, dyss, bre, bim, cre, cim, lamp, dsk)


def _branches(z1_ref, yss_ref, gt_ref, lng_ref, lnb_ref, wp_ref, wv_ref, wg_ref):
    zf = z1_ref[...]
    mu = jnp.mean(zf, axis=-1, keepdims=True)
    zc = zf - mu
    rstd = lax.rsqrt(jnp.mean(zc * zc, axis=-1, keepdims=True) + EPS)
    zn = zc * rstd
    z2 = zn * lng_ref[...] + lnb_ref[...]
    sz = _sigmoid(z2)
    z3 = (z2 * sz).astype(BF16)
    y_conv = _dot(z3, wp_ref[...])
    yss = yss_ref[...]
    yg = _gelu(yss).astype(BF16)
    sv = _dot(yg, wv_ref[...])
    sig = _sigmoid(_dot(yg, wg_ref[...]))
    y_ssm = sv * sig
    gc = gt_ref[:, 0:D].astype(F32)
    gs = gt_ref[:, D:2 * D].astype(F32)
    m = gc * y_conv + gs * y_ssm
    return dict(rstd=rstd, zn=zn, z2=z2, sz=sz, z3=z3, y_conv=y_conv, yss=yss, yg=yg, sv=sv, sig=sig,
                y_ssm=y_ssm, gc=gc, gs=gs, m=m)


def _merge_fwd(h, z1, yss, gate, lng, lnb, wp, wv, wg, wo, name, comm=None):
    L = h.shape[0]
    tm = _tile(L, 528)

    def body(h_ref, z1_ref, yss_ref, gt_ref, lng_ref, lnb_ref, wp_ref, wv_ref, wg_ref, wo_ref, o_ref):
        f = _branches(z1_ref, yss_ref, gt_ref, lng_ref, lnb_ref, wp_ref, wv_ref, wg_ref)
        o_ref[...] = h_ref[...] + _dot(f["m"].astype(BF16), wo_ref[...])

    def row(n):
        return pl.BlockSpec((tm, n), lambda i: (i, 0))

    return _call(
        body, comm, name=name, grid=(L // tm,),
        out_shape=[jax.ShapeDtypeStruct((L, D), F32)],
        in_specs=[row(D), row(DC), row(DS), row(2 * D), _res((1, DC)), _res((1, DC)),
                  _res((DC, D)), _res((DS, D)), _res((DS, D)), _res((D, D))],
        out_specs=[row(D)],
        params=_params("parallel"),
    )(h, z1, yss, gate, lng, lnb, wp, wv, wg, wo)


def _merge_bwd(dh, z1, yss, gate, lng, lnb, wp, wv, wg, wo, name):
    L = dh.shape[0]
    tm = _tile(L, 352)

    def body(dh_ref, z1_ref, yss_ref, gt_ref, lng_ref, lnb_ref, wp_ref, wv_ref, wg_ref, wo_ref,
             m_ref, dgt_ref, dyc_ref, z3_ref, dz1_ref, yg_ref, dsv_ref, dsg_ref, dyss_ref,
             dbg_ref, dlng_ref, dlnb_ref):
        i = pl.program_id(0)
        f = _branches(z1_ref, yss_ref, gt_ref, lng_ref, lnb_ref, wp_ref, wv_ref, wg_ref)
        gc, gs, sig, sv = f["gc"], f["gs"], f["sig"], f["sv"]
        m_ref[...] = f["m"].astype(BF16)
        z3_ref[...] = f["z3"]
        yg_ref[...] = f["yg"]
        dm = _dot_nt(dh_ref[...].astype(BF16), wo_ref[...])
        dgc = (dm * f["y_conv"] * gc * (1.0 - gc)).astype(BF16)
        dgs = (dm * f["y_ssm"] * gs * (1.0 - gs)).astype(BF16)
        dgt_ref[:, 0:D] = dgc
        dgt_ref[:, D:2 * D] = dgs
        part = jnp.concatenate([jnp.sum(dgc.astype(F32), axis=0, keepdims=True),
                                jnp.sum(dgs.astype(F32), axis=0, keepdims=True)], axis=1)
        _acc_rows(dbg_ref, part, i == 0)
        dyc = (dm * gc).astype(BF16)
        dyc_ref[...] = dyc
        dys = dm * gs
        dsv = (dys * sig).astype(BF16)
        dsg = (dys * sv * sig * (1.0 - sig)).astype(BF16)
        dsv_ref[...] = dsv
        dsg_ref[...] = dsg
        dyg = _dot_nt(dsv, wv_ref[...]) + _dot_nt(dsg, wg_ref[...])
        dyss_ref[...] = dyg * _gelu_grad(f["yss"])
        dz3 = _dot_nt(dyc, wp_ref[...])
        z2, sz, zn = f["z2"], f["sz"], f["zn"]
        dz2 = dz3 * sz * (1.0 + z2 * (1.0 - sz))
        _acc_rows(dlng_ref, jnp.sum(dz2 * zn, axis=0, keepdims=True), i == 0)
        _acc_rows(dlnb_ref, jnp.sum(dz2, axis=0, keepdims=True), i == 0)
        dzn = dz2 * lng_ref[...]
        dz1_ref[...] = f["rstd"] * (dzn - jnp.mean(dzn, axis=-1, keepdims=True)
                                    - zn * jnp.mean(dzn * zn, axis=-1, keepdims=True))

    def row(n):
        return pl.BlockSpec((tm, n), lambda i: (i, 0))

    def tot(n):
        return pl.BlockSpec((1, n), lambda i: (0, 0))

    return pl.pallas_call(
        body, name=name, grid=(L // tm,),
        out_shape=[jax.ShapeDtypeStruct((L, D), BF16), jax.ShapeDtypeStruct((L, 2 * D), BF16),
                   jax.ShapeDtypeStruct((L, D), BF16), jax.ShapeDtypeStruct((L, DC), BF16),
                   jax.ShapeDtypeStruct((L, DC), F32), jax.ShapeDtypeStruct((L, DS), BF16),
                   jax.ShapeDtypeStruct((L, D), BF16), jax.ShapeDtypeStruct((L, D), BF16),
                   jax.ShapeDtypeStruct((L, DS), F32),
                   jax.ShapeDtypeStruct((1, 2 * D), F32), jax.ShapeDtypeStruct((1, DC), F32),
                   jax.ShapeDtypeStruct((1, DC), F32)],
        in_specs=[row(D), row(DC), row(DS), row(2 * D), _res((1, DC)), _res((1, DC)),
                  _res((DC, D)), _res((DS, D)), _res((DS, D)), _res((D, D))],
        out_specs=[row(D), row(2 * D), row(D), row(DC), row(DC), row(DS), row(D), row(D), row(DS),
                   tot(2 * D), tot(DC), tot(DC)],
        compiler_params=_params("arbitrary"),
    )(dh, z1, yss, gate, lng, lnb, wp, wv, wg, wo)


def _ssm_disc(lam_re, lam_im, log_dt, b_re, b_im):
    lam = lax.complex(lam_re, lam_im)
    dt = jnp.exp(log_dt)[:, None]
    lam_bar = jnp.exp(lam * dt)
    bbar = ((lam_bar - 1.0) / lam)[..., None] * lax.complex(b_re, b_im)
    return jnp.real(lam_bar), jnp.imag(lam_bar), jnp.real(bbar), jnp.imag(bbar)


def _bdiag_in(m):
    return jnp.einsum("gph,gk->ghkp", m, jnp.eye(G, dtype=m.dtype)).reshape(G * H, G * P)


def _bdiag_out(m):
    return jnp.einsum("ghp,gk->gpkh", m, jnp.eye(G, dtype=m.dtype)).reshape(G * P, G * H)


def _diag_blocks(m4):
    return jnp.einsum("qiaib->qiab", m4).reshape(G, m4.shape[2], m4.shape[4])


def _pack(parts, rows_mult=8):
    flat = jnp.concatenate([p.reshape(-1).astype(F32) for p in parts])
    n = flat.shape[0]
    tot = -(-n // (128 * rows_mult)) * (128 * rows_mult)
    return jnp.pad(flat, (0, tot - n)).reshape(tot // 128, 128)


def _unpack(buf, shapes):
    flat = buf.reshape(-1)
    out, o = [], 0
    for s in shapes:
        n = math.prod(s)
        out.append(flat[o:o + n].reshape(s))
        o += n
    return out


def kernel(x, meta_tokens, ffn1_norm, ffn1_w1, ffn1_w3, ffn1_w2, mix_norm, w_in, b_gate, conv_dw, conv_dw_b, conv_ln_g, conv_ln_b, conv_proj, ssm_lam_re, ssm_lam_im, ssm_log_dt, ssm_b_re, ssm_b_im, ssm_c_re, ssm_c_im, ssm_d, ssm_w_v, ssm_w_g, w_out, ffn2_norm, ffn2_w1, ffn2_w3, ffn2_w2, final_norm, loss_target, m_meta_tokens, m_ffn1_norm, m_ffn1_w1, m_ffn1_w3, m_ffn1_w2, m_mix_norm, m_w_in, m_b_gate, m_conv_dw, m_conv_dw_b, m_conv_ln_g, m_conv_ln_b, m_conv_proj, m_ssm_lam_re, m_ssm_lam_im, m_ssm_log_dt, m_ssm_b_re, m_ssm_b_im, m_ssm_c_re, m_ssm_c_im, m_ssm_d, m_ssm_w_v, m_ssm_w_g, m_w_out, m_ffn2_norm, m_ffn2_w1, m_ffn2_w3, m_ffn2_w2, m_final_norm, v_meta_tokens, v_ffn1_norm, v_ffn1_w1, v_ffn1_w3, v_ffn1_w2, v_mix_norm, v_w_in, v_b_gate, v_conv_dw, v_conv_dw_b, v_conv_ln_g, v_conv_ln_b, v_conv_proj, v_ssm_lam_re, v_ssm_lam_im, v_ssm_log_dt, v_ssm_b_re, v_ssm_b_im, v_ssm_c_re, v_ssm_c_im, v_ssm_d, v_ssm_w_v, v_ssm_w_g, v_w_out, v_ffn2_norm, v_ffn2_w1, v_ffn2_w3, v_ffn2_w2, v_final_norm):
    args = dict(locals())
    names = ["meta_tokens", "ffn1_norm", "ffn1_w1", "ffn1_w3", "ffn1_w2", "mix_norm", "w_in", "b_gate",
             "conv_dw", "conv_dw_b", "conv_ln_g", "conv_ln_b", "conv_proj", "ssm_lam_re", "ssm_lam_im",
             "ssm_log_dt", "ssm_b_re", "ssm_b_im", "ssm_c_re", "ssm_c_im", "ssm_d", "ssm_w_v", "ssm_w_g",
             "w_out", "ffn2_norm", "ffn2_w1", "ffn2_w3", "ffn2_w2", "final_norm"]
    big = ["ffn1_w1", "ffn1_w3", "ffn1_w2", "w_in", "conv_proj", "ssm_w_v", "ssm_w_g", "w_out",
           "ffn2_w1", "ffn2_w3", "ffn2_w2"]
    small = [n for n in names if n not in big]

    xs = x[0]
    S = xs.shape[0]
    L = FRONT + S
    T = L // NSEG
    jx, jy = lax.axis_index("x"), lax.axis_index("y")
    chip = 2 * jx + jy

    small_all = _gather_all(_pack([meta_tokens, conv_dw[0]]), "gather_small")
    sm = small_all[0::2].reshape(NSH, -1)
    nmt = NMETA * (D // NSH)
    ndw = KW * (DC // NSH)
    meta_full = sm[:, :nmt].reshape(NSH, NMETA, D // NSH).transpose(1, 0, 2).reshape(NMETA, D)
    dw_full = sm[:, nmt:nmt + ndw].reshape(NSH, KW, DC // NSH).transpose(1, 0, 2).reshape(KW, DC)
    dw_pad = jnp.pad(dw_full, ((0, KWP - KW), (0, 0)))
    tposed = ("ffn1_w1", "ffn1_w3", "ffn2_w1", "ffn2_w3")

    def view(a, n):
        return jnp.swapaxes(a, 1, 2) if n in tposed else a

    grp_a = ["ffn1_w1", "ffn1_w3", "ffn1_w2"]
    grp_b = ["w_in", "conv_proj", "ssm_w_v", "ssm_w_g", "w_out"]
    grp_c = ["ffn2_w1", "ffn2_w3", "ffn2_w2"]

    def shard(n):
        return view(args[n], n)[0].astype(BF16)

    sh_a = [shard(n) for n in grp_a]
    ga_send, ga_recv, sh_a, land_a, _ = _chips_start(
        sh_a, [jax.ShapeDtypeStruct((NSH,) + s.shape, s.dtype) for s in sh_a], True, "gather_ffn1_start",
        [small_all])

    def cols(w):
        return w.transpose(1, 0, 2).reshape(w.shape[1], -1)

    disc_in = (ssm_lam_re[0], ssm_lam_im[0], ssm_log_dt[0], ssm_b_re[0], ssm_b_im[0])
    (lbr, lbi, bbr, bbi), disc_vjp = jax.vjp(_ssm_disc, *disc_in)
    lam_t = jnp.exp(lax.complex(ssm_lam_re[0], ssm_lam_im[0]) * (jnp.exp(ssm_log_dt[0])[:, None] * T))
    lamp = jnp.concatenate([lbr.reshape(1, NST), lbi.reshape(1, NST), jnp.real(lam_t).reshape(1, NST),
                            jnp.imag(lam_t).reshape(1, NST), jnp.zeros((4, NST), F32)], axis=0)
    bre_bd, bim_bd = _bdiag_in(bbr).astype(BF16), _bdiag_in(bbi).astype(BF16)
    cre_bd, cim_bd = _bdiag_out(ssm_c_re[0]).astype(BF16), _bdiag_out(ssm_c_im[0]).astype(BF16)

    h0 = lax.dynamic_update_slice(jnp.pad(xs, ((FRONT, 0), (0, 0))), meta_full, (FRONT - NMETA, 0))
    sh_a, land_a = _chips_wait(ga_send, ga_recv, sh_a, land_a, [h0], True, "gather_ffn1_wait")
    gw = dict(zip(grp_a, _pass_halves(land_a, "pass_ffn1", sh_a)))
    tgt = jnp.pad(loss_target[0], ((FRONT, 0), (0, 0)))
    (h1, a1, b1), got = _ffn_fwd(h0, ffn1_norm, gw["ffn1_w1"], gw["ffn1_w3"], gw["ffn1_w2"], "ffn1_fwd",
                                 _gather_half_behind([shard(n) for n in grp_b]))
    w_in_f = _pass_halves(got[:1], "pass_w_in")[0]
    (vg, uf, gate), got1 = _mix_in_fwd(h1, mix_norm, w_in_f, b_gate, "mix_in_fwd",
                                       _join(_gather_half_behind([shard("ffn2_w1")]),
                                             _pass_halves_behind(list(got[1:]))))
    gw.update(zip(grp_b[1:], got1[1:]))
    wp_f, wv_f, wg_f = cols(gw["conv_proj"]), cols(gw["ssm_w_v"]), cols(gw["ssm_w_g"])
    wo_f = gw["w_out"].reshape(D, D)
    (z1,), got3 = _conv_fwd(vg, dw_pad, conv_dw_b, "conv_fwd", _gather_half_behind([shard("ffn2_w3")]))
    (yss,), got2 = _ssm_fwd(uf, bre_bd, bim_bd, cre_bd, cim_bd, lamp, ssm_d, "ssm_fwd",
                            _gather_half_behind([shard("ffn2_w2")]))
    (h2,), got_c = _merge_fwd(h1, z1, yss, gate, conv_ln_g, conv_ln_b, wp_f, wv_f, wg_f, wo_f, "merge_fwd",
                              _pass_halves_behind([got1[0], got3[0], got2[0]]))
    gw.update(zip(grp_c, got_c))
    dh3, a2, b2, loss_part, d_final = _ffn_fwd_loss(
        h2, ffn2_norm, gw["ffn2_w1"], gw["ffn2_w3"], gw["ffn2_w2"], final_norm.reshape(1, D), tgt, "ffn2_fwd_loss")

    gbig = {}
    core = lax.axis_index("c").astype(jnp.int32).reshape(1)

    def pair_sums(group, tag):
        gl = [gbig[n] for n in group]
        sib = _pair_exchange(gl, "pair_exchange_" + tag)
        out = []
        for n, g_, s_ in zip(group, gl, sib):
            out.append(_add_pair(g_, s_, core, "pair_" + n))
        return out

    (dh2, da2, db2, s2, n2, d_ffn2_norm), _ = _ffn_bwd(
        h2, ffn2_norm, dh3, a2, b2, gw["ffn2_w1"], gw["ffn2_w3"], gw["ffn2_w2"], "ffn2_bwd")
    gbig["ffn2_w1"] = _wgrad(da2, n2, "ffn2_dw1")
    gbig["ffn2_w3"] = _wgrad(db2, n2, "ffn2_dw3")
    gbig["ffn2_w2"] = _wgrad(s2, dh3, "ffn2_dw2", 0.5)
    pair_c = pair_sums(grp_c, "ffn2")
    (m_b, dgate, dyc, z3, dz1, yg, dsv, dsg, dyss, d_b_gate, d_ln_g, d_ln_b) = _merge_bwd(
        dh2, z1, yss, gate, conv_ln_g, conv_ln_b, wp_f, wv_f, wg_f, wo_f, "merge_bwd")
    gbig["w_out"] = _wgrad(m_b, dh2, "dw_out").reshape(NSH, D // NSH, D)

    def shard_cols(gm):
        return gm.reshape(gm.shape[0], NSH, -1).transpose(1, 0, 2)

    gbig["conv_proj"] = shard_cols(_wgrad(z3, dyc, "dw_proj"))
    gbig["ssm_w_v"] = shard_cols(_wgrad(yg, dsv, "dw_v"))
    gbig["ssm_w_g"] = shard_cols(_wgrad(yg, dsg, "dw_g"))
    dv, dgl, ddw, d_dw_b = _conv_bwd(dz1, vg, dw_pad, "conv_bwd")
    (duf, dbre, dbim, dcre, dcim, dlam, d_ssm_d), recv_c = _ssm_bwd(
        uf, dyss, bre_bd, bim_bd, cre_bd, cim_bd, lamp, ssm_d, "ssm_bwd", _scatter_chips_behind(pair_c))
    dh1, u_b, dproj, d_mix_norm = _mix_in_bwd(h1, mix_norm, dh2, dv, dgl, duf, dgate, w_in_f, "mix_in_bwd")
    gbig["w_in"] = _wgrad(u_b, dproj, "dw_in")
    pair_b = pair_sums(grp_b, "mix")

    d_bbr = _diag_blocks(dbre.reshape(NQ, 8, H, 8, P)).transpose(0, 2, 1)
    d_bbi = _diag_blocks(dbim.reshape(NQ, 8, H, 8, P)).transpose(0, 2, 1)
    d_c_re = _diag_blocks(dcre.reshape(NQ, 8, P, 8, H)).transpose(0, 2, 1)
    d_c_im = _diag_blocks(dcim.reshape(NQ, 8, P, 8, H)).transpose(0, 2, 1)
    d_lbr = dlam[:, 0, :].reshape(G, P)
    d_lbi = dlam[:, 1, :].reshape(G, P)
    d_lam_re, d_lam_im, d_log_dt, d_b_re, d_b_im = disc_vjp((d_lbr, d_lbi, d_bbr, d_bbi))

    sg = {"mix_norm": d_mix_norm, "b_gate": d_b_gate, "conv_dw": ddw[:KW], "conv_dw_b": d_dw_b,
          "conv_ln_g": d_ln_g, "conv_ln_b": d_ln_b, "ssm_lam_re": d_lam_re, "ssm_lam_im": d_lam_im,
          "ssm_log_dt": d_log_dt, "ssm_b_re": d_b_re, "ssm_b_im": d_b_im, "ssm_c_re": d_c_re, "ssm_c_im": d_c_im,
          "ssm_d": d_ssm_d, "ffn2_norm": d_ffn2_norm, "final_norm": d_final}
    late = ["meta_tokens", "ffn1_norm"]
    early = [n for n in small if n not in late]

    (dh0, da1, db1, s1, n1, d_ffn1_norm), got = _ffn_bwd(
        h0, ffn1_norm, dh1, a1, b1, gw["ffn1_w1"], gw["ffn1_w3"], gw["ffn1_w2"], "ffn1_bwd",
        _join(_scatter_chips_behind(pair_b), _gather_all_behind(_pack([sg[n] for n in early]))))
    recv_b, early_all = got[:len(grp_b)], got[len(grp_b)]
    gbig["ffn1_w1"] = _wgrad(da1, n1, "ffn1_dw1")
    gbig["ffn1_w3"] = _wgrad(db1, n1, "ffn1_dw3")
    gbig["ffn1_w2"] = _wgrad(s1, dh1, "ffn1_dw2", 0.5)
    grad_x = dh0[FRONT:][None]
    sg["meta_tokens"] = dh0[FRONT - NMETA:FRONT]
    sg["ffn1_norm"] = d_ffn1_norm

    pair_a = pair_sums(grp_a, "ffn1")
    late_all = _gather_all(_pack([sg[n] for n in late]), "gather_late_grads")
    sa_send, sa_recv, pair_a, land_s, sa_token = _chips_start(
        pair_a, [jax.ShapeDtypeStruct(p.shape, p.dtype) for p in pair_a], False, "scatter_ffn1_start", [late_all])

    out_g, out_d, out_m, out_v = {}, {}, {}, {}

    def finish(group, recvs, tag, after=None):
        halves = [_sum_slots(r, "sum_" + n, after) for n, r in zip(group, recvs)]
        for n, f in zip(group, _swap_halves(halves, "swap_" + tag)):
            g3 = f.reshape(1, f.shape[0] * f.shape[1], f.shape[2])
            g3, d3, m3, v3 = _adamw(view(args[n], n), g3, view(args["m_" + n], n), view(args["v_" + n], n),
                                "adamw_" + n)
            out_g[n], out_d[n], out_m[n], out_v[n] = (view(t, n) for t in (g3, d3, m3, v3))
            done.append(d3)

    done = []
    finish(grp_b + grp_c, list(recv_b) + list(recv_c), "mix_ffn2", sa_token)

    sgr = dict(zip(early, _unpack(_sum_slots(early_all, "sum_early", sa_token), [sg[n].shape for n in early])))
    sgr.update(zip(late, _unpack(_sum_slots(late_all, "sum_late"), [sg[n].shape for n in late])))
    sgr["meta_tokens"] = lax.dynamic_slice_in_dim(sgr["meta_tokens"], chip * (D // NSH), D // NSH, axis=1)
    sgr["conv_dw"] = lax.dynamic_slice_in_dim(sgr["conv_dw"], chip * (DC // NSH), DC // NSH, axis=1)
    pshapes = [args[n].shape for n in small]
    _, d_s, m_s, v_s = _adamw(_pack([args[n] for n in small])[None], _pack([sgr[n] for n in small])[None],
                           _pack([args["m_" + n] for n in small])[None],
                           _pack([args["v_" + n] for n in small])[None], "adamw_small")
    for n, g_, d_, m_, v_ in zip(small, [sgr[n] for n in small], _unpack(d_s[0], pshapes),
                                 _unpack(m_s[0], pshapes), _unpack(v_s[0], pshapes)):
        out_g[n], out_d[n], out_m[n], out_v[n] = g_.reshape(args[n].shape), d_, m_, v_

    loss = lax.psum(loss_part[0, 0], ("x", "y", "c"))
    pair_a, recv_a = _chips_wait(sa_send, sa_recv, pair_a, land_s, [d_s, grad_x] + done, False,
                                 "scatter_ffn1_wait")
    finish(grp_a, _fill_own(pair_a, recv_a, "own_ffn1"), "ffn1")
    return (loss, grad_x, *[out_g[n] for n in names], *[out_d[n] for n in names],
            *[out_m[n] for n in names], *[out_v[n] for n in names])
```

```python
import math

import jax
import jax.numpy as jnp
from jax import lax
from jax.experimental import pallas as pl
from jax.experimental.pallas import tpu as pltpu

F32 = jnp.float32
BF16 = jnp.bfloat16

D = 1024
NSH = 4
F = 2816
FS = F // NSH
DC = 512
DS = 512
DIN = 2 * DC + DS + 2 * D
WS = DIN // NSH
KW = 31
KWP = 32
CONV_ROWS = 64
NMETA = 16
FRONT = 128
G, P, H = 32, 64, 16
NST = G * P
NQ = 4
QS = NST // NQ
QU = DS // NQ
NSEG = 32
NGRP = NSEG // 8
NCH = 8
SOFF = 8
EPS = 1e-6
LR, B1, B2, AEPS, WD, STEP = 1e-3, 0.9, 0.999, 1e-8, 0.01, 10
VMEM_LIMIT = 58 * 1024 * 1024
MESH = pl.DeviceIdType.MESH
ANY = pl.BlockSpec(memory_space=pl.ANY)


def _params(*sem):
    return pltpu.CompilerParams(dimension_semantics=sem, vmem_limit_bytes=VMEM_LIMIT)


def _res(shape):
    nd = len(shape)
    return pl.BlockSpec(shape, lambda *_: (0,) * nd, pipeline_mode=pl.Buffered(1))


def _tile(n, cap, mult=16):
    best = None
    for t in range(mult, min(n, cap) + 1, mult):
        if n % t == 0:
            best = t
    assert best is not None, (n, cap, mult)
    return best


def _dot(a, b):
    return jnp.dot(a, b, preferred_element_type=F32)


def _dot_nt(a, b):
    return lax.dot_general(a, b, (((1,), (1,)), ((), ())), preferred_element_type=F32)


def _dot_tn(a, b):
    return lax.dot_general(a, b, (((0,), (0,)), ((), ())), preferred_element_type=F32)


def _sigmoid(x):
    return 1.0 / (1.0 + jnp.exp(-x))


_GC = math.sqrt(2.0 / math.pi)
_GA = 0.044715


def _gelu(x):
    return 0.5 * x * (1.0 + jnp.tanh(_GC * (x + _GA * x * x * x)))


def _gelu_grad(x):
    t = jnp.tanh(_GC * (x + _GA * x * x * x))
    return 0.5 * (1.0 + t) + 0.5 * x * (1.0 - t * t) * _GC * (1.0 + 3.0 * _GA * x * x)


def _rms(hv, g):
    r = lax.rsqrt(jnp.mean(hv * hv, axis=-1, keepdims=True) + EPS)
    return hv * r * g, r


def _rms_bwd(dn, hv, r, g):
    xh = hv * r
    dxh = dn * g
    return r * (dxh - xh * jnp.mean(dxh * xh, axis=-1, keepdims=True)), xh


def _acc_rows(ref, part, first):
    @pl.when(first)
    def _():
        ref[...] = part

    @pl.when(jnp.logical_not(first))
    def _():
        ref[...] += part


def _coords():
    return lax.axis_index("x"), lax.axis_index("y"), lax.axis_index("c")


def _flip(v, d):
    return 1 - v if d else v


def _run(local, remote):
    for cp in local + remote:
        cp.start()
    for cp in remote:
        cp.wait()
    for cp in local:
        cp.wait()


def _via_vmem(src, dst, stage, sems, i):
    return (pltpu.make_async_copy(src, stage, sems.at[2 * i]), pltpu.make_async_copy(stage, dst, sems.at[2 * i + 1]))


def _run_staged(staged, remote):
    for load, _ in staged:
        load.start()
    for cp in remote:
        cp.start()
    for load, store in staged:
        load.wait()
        store.start()
    for cp in remote:
        cp.wait()
    for _, store in staged:
        store.wait()


_REL3 = ((1, 0), (0, 1), (1, 1))


class _Behind:
    def __init__(self, arrays, out_shapes, scratch, build, alias_pairs=()):
        self.arrays, self.out_shapes, self.scratch, self.build = list(arrays), list(out_shapes), list(scratch), build
        self.alias_pairs = list(alias_pairs)

    def aliases(self):
        return self.alias_pairs

    def start(self, ins, outs, scr):
        staged, remote = self.build(ins, outs, scr)
        for load, _ in staged:
            load.start()
        for cp in remote:
            cp.start()

    def finish(self, ins, outs, scr):
        staged, remote = self.build(ins, outs, scr)
        for load, store in staged:
            load.wait()
            store.start()
        for cp in remote:
            cp.wait()
        for _, store in staged:
            store.wait()


def _call(body, comm, *, name, grid, in_specs, out_specs, out_shape, scratch_shapes=(), params):
    in_specs, out_specs, out_shape = list(in_specs), list(out_specs), list(out_shape)
    scratch_shapes = list(scratch_shapes)
    if comm is None:
        f = pl.pallas_call(body, name=name, grid=grid, in_specs=in_specs, out_specs=out_specs,
                           out_shape=out_shape, scratch_shapes=scratch_shapes, compiler_params=params)
        return lambda *args: (f(*args), [])
    ni, no, ns = len(in_specs), len(out_specs), len(scratch_shapes)
    ci, co = len(comm.arrays), len(comm.out_shapes)

    def hosted(*refs):
        ins, cin = refs[:ni], refs[ni:ni + ci]
        outs, cout = refs[ni + ci:ni + ci + no], refs[ni + ci + no:ni + ci + no + co]
        scr, cscr = refs[ni + ci + no + co:ni + ci + no + co + ns], refs[ni + ci + no + co + ns:]
        first = last = None
        for axis, size in enumerate(grid):
            i = pl.program_id(axis)
            first = (i == 0) if first is None else jnp.logical_and(first, i == 0)
            last = (i == size - 1) if last is None else jnp.logical_and(last, i == size - 1)

        @pl.when(first)
        def _():
            comm.start(cin, cout, cscr)

        body(*ins, *outs, *scr)

        @pl.when(last)
        def _():
            comm.finish(cin, cout, cscr)

    f = pl.pallas_call(hosted, name=name, grid=grid, in_specs=in_specs + [ANY] * ci,
                       out_specs=out_specs + [ANY] * co, out_shape=out_shape + comm.out_shapes,
                       scratch_shapes=scratch_shapes + comm.scratch,
                       input_output_aliases={ni + a: no + b for a, b in comm.aliases()},
                       compiler_params=_params(*(("arbitrary",) * len(grid))))

    def run(*args):
        res = f(*args, *comm.arrays)
        return res[:no], res[no:]

    return run


def _gather_half_behind(shards):
    n = len(shards)

    def build(ins, outs, scr):
        send, recv, loc = scr[:3]
        stage = scr[3:]
        x, y, c = _coords()
        me = 2 * x + y
        staged = [_via_vmem(ins[t], outs[t].at[me], stage[t], loc, t) for t in range(n)]
        remote = []
        for t in range(n):
            half = shards[t].shape[0] // 2
            mine = pl.ds(c * half, half)
            for k, (dx, dy) in enumerate(_REL3):
                remote.append(pltpu.make_async_remote_copy(
                    src_ref=ins[t].at[mine], dst_ref=outs[t].at[me, mine],
                    send_sem=send.at[3 * t + k], recv_sem=recv.at[3 * t + k],
                    device_id=(_flip(x, dx), _flip(y, dy), c), device_id_type=MESH))
        return staged, remote

    return _Behind(shards, [jax.ShapeDtypeStruct((NSH,) + s.shape, s.dtype) for s in shards],
                   [pltpu.SemaphoreType.DMA((3 * n,)), pltpu.SemaphoreType.DMA((3 * n,)),
                    pltpu.SemaphoreType.DMA((2 * n,))] + [pltpu.VMEM(s.shape, s.dtype) for s in shards], build)


def _pass_halves(gathered, name, own=()):
    n, m = len(gathered), len(own)

    def body(*refs):
        shards, outs = refs[n:n + m], refs[n + m:2 * n + m]
        send, recv, loc = refs[2 * n + m:2 * n + m + 3]
        stage = refs[2 * n + m + 3:]
        x, y, c = _coords()
        staged = [_via_vmem(shards[t], outs[t].at[2 * x + y], stage[t], loc, t) for t in range(m)]
        remote = []
        for t in range(n):
            half = gathered[t].shape[1] // 2
            mine = pl.ds(c * half, half)
            for k, (dx, dy) in enumerate(_REL3):
                slot = 2 * _flip(x, dx) + _flip(y, dy)
                remote.append(pltpu.make_async_remote_copy(
                    src_ref=outs[t].at[slot, mine], dst_ref=outs[t].at[slot, mine],
                    send_sem=send.at[3 * t + k], recv_sem=recv.at[3 * t + k],
                    device_id=(x, y, 1 - c), device_id_type=MESH))
        _run_staged(staged, remote)

    return pl.pallas_call(
        body, name=name,
        out_shape=[jax.ShapeDtypeStruct(g.shape, g.dtype) for g in gathered],
        in_specs=[ANY] * (n + m), out_specs=[ANY] * n, input_output_aliases={t: t for t in range(n)},
        scratch_shapes=[pltpu.SemaphoreType.DMA((3 * n,)), pltpu.SemaphoreType.DMA((3 * n,)),
                        pltpu.SemaphoreType.DMA((max(2 * m, 1),))] + [pltpu.VMEM(s.shape, s.dtype) for s in own],
        compiler_params=pltpu.CompilerParams(vmem_limit_bytes=VMEM_LIMIT),
    )(*gathered, *own)


def _fill_own(sums, recvs, name):
    n = len(sums)

    def body(*refs):
        ins, outs = refs[:n], refs[2 * n:3 * n]
        loc = refs[3 * n]
        stage = refs[3 * n + 1:]
        x, y, _ = _coords()
        me = 2 * x + y
        _run_staged([_via_vmem(ins[t].at[me], outs[t].at[me], stage[t], loc, t) for t in range(n)], [])

    return pl.pallas_call(
        body, name=name,
        out_shape=[jax.ShapeDtypeStruct(r.shape, r.dtype) for r in recvs],
        in_specs=[ANY] * (2 * n), out_specs=[ANY] * n, input_output_aliases={n + t: t for t in range(n)},
        scratch_shapes=[pltpu.SemaphoreType.DMA((2 * n,))] + [pltpu.VMEM(s.shape[1:], s.dtype) for s in sums],
        compiler_params=pltpu.CompilerParams(vmem_limit_bytes=VMEM_LIMIT),
    )(*sums, *recvs)


def _scatter_chips_behind(sums):
    n = len(sums)

    def build(ins, outs, scr):
        send, recv, loc = scr[:3]
        stage = scr[3:]
        x, y, c = _coords()
        me = 2 * x + y
        staged = [_via_vmem(ins[t].at[me], outs[t].at[me], stage[t], loc, t) for t in range(n)]
        remote = []
        for t in range(n):
            for k, (dx, dy) in enumerate(_REL3):
                px, py = _flip(x, dx), _flip(y, dy)
                remote.append(pltpu.make_async_remote_copy(
                    src_ref=ins[t].at[2 * px + py], dst_ref=outs[t].at[me],
                    send_sem=send.at[3 * t + k], recv_sem=recv.at[3 * t + k],
                    device_id=(px, py, c), device_id_type=MESH))
        return staged, remote

    return _Behind(sums, [jax.ShapeDtypeStruct(s.shape, s.dtype) for s in sums],
                   [pltpu.SemaphoreType.DMA((3 * n,)), pltpu.SemaphoreType.DMA((3 * n,)),
                    pltpu.SemaphoreType.DMA((2 * n,))] + [pltpu.VMEM(s.shape[1:], s.dtype) for s in sums], build)


def _gather_all_behind(a):
    def build(ins, outs, scr):
        send, recv, loc, stage = scr
        x, y, c = _coords()
        me = 4 * x + 2 * y + c
        staged = [_via_vmem(ins[0], outs[0].at[me], stage, loc, 0)]
        remote = [pltpu.make_async_remote_copy(
            src_ref=ins[0], dst_ref=outs[0].at[me], send_sem=send.at[k], recv_sem=recv.at[k],
            device_id=(_flip(x, dx), _flip(y, dy), _flip(c, dc)), device_id_type=MESH)
            for k, (dx, dy, dc) in enumerate(_REL7)]
        return staged, remote

    return _Behind([a], [jax.ShapeDtypeStruct((8,) + a.shape, a.dtype)],
                   [pltpu.SemaphoreType.DMA((7,)), pltpu.SemaphoreType.DMA((7,)), pltpu.SemaphoreType.DMA((2,)),
                    pltpu.VMEM(a.shape, a.dtype)], build)


HBM = pl.BlockSpec(memory_space=pltpu.HBM)
SEM = pl.BlockSpec(memory_space=pltpu.SEMAPHORE)
EFFECT = pltpu.SideEffectType.DATAFLOW_SIDE_EFFECTING


def _chip_copies(srcs, lands, send, recv, gather):
    x, y, c = _coords()
    me = 2 * x + y
    cps = []
    for t in range(len(srcs)):
        for k, (dx, dy) in enumerate(_REL3):
            px, py = _flip(x, dx), _flip(y, dy)
            if gather:
                half = srcs[t].shape[0] // 2
                mine = pl.ds(c * half, half)
                src, dst = srcs[t].at[mine], lands[t].at[me, mine]
            else:
                src, dst = srcs[t].at[2 * px + py], lands[t].at[me]
            cps.append(pltpu.make_async_remote_copy(
                src_ref=src, dst_ref=dst, send_sem=send.at[3 * t + k], recv_sem=recv.at[3 * t + k],
                device_id=(px, py, c), device_id_type=MESH))
    return cps


def _chips_start(arrays, land_shapes, gather, name, after=()):
    n = len(arrays)

    def body(*refs):
        srcs, lands = refs[:n], refs[n:2 * n]
        send, recv = refs[2 * n + len(after)], refs[2 * n + len(after) + 1]
        token = refs[-1]
        for cp in _chip_copies(srcs, lands, send, recv, gather):
            cp.start()
        token[...] = jnp.zeros_like(token)

    lands = [lax.empty(s.shape, s.dtype) for s in land_shapes]
    thru = [pltpu.HBM(a.shape, a.dtype) for a in arrays] + [pltpu.HBM(s.shape, s.dtype) for s in land_shapes]
    res = pl.pallas_call(
        body, name=name,
        out_shape=(pltpu.SemaphoreType.DMA((3 * n,)), pltpu.SemaphoreType.DMA((3 * n,)), *thru,
                   jax.ShapeDtypeStruct((8, 128), F32)),
        in_specs=[HBM] * (2 * n) + [ANY] * len(after),
        out_specs=(SEM, SEM, *([HBM] * (2 * n)), pl.BlockSpec(memory_space=pltpu.VMEM)),
        input_output_aliases={t: 2 + t for t in range(2 * n)},
        compiler_params=pltpu.CompilerParams(has_side_effects=EFFECT),
    )(*[pltpu.with_memory_space_constraint(a, pltpu.HBM) for a in arrays],
      *[pltpu.with_memory_space_constraint(z, pltpu.HBM) for z in lands], *after)
    return res[0], res[1], list(res[2:2 + n]), list(res[2 + n:2 + 2 * n]), res[-1]


def _chips_wait(send, recv, arrays, lands, after, gather, name):
    n = len(arrays)

    def body(*refs):
        srcs, ls = refs[:n], refs[n:2 * n]
        sd, rv = refs[2 * n], refs[2 * n + 1]
        for cp in _chip_copies(srcs, ls, sd, rv, gather):
            cp.wait_send()
            cp.wait_recv()

    res = pl.pallas_call(
        body, name=name,
        out_shape=[pltpu.HBM(a.shape, a.dtype) for a in arrays] + [pltpu.HBM(z.shape, z.dtype) for z in lands],
        in_specs=[HBM] * (2 * n) + [SEM, SEM] + [ANY] * len(after), out_specs=[HBM] * (2 * n),
        input_output_aliases={t: t for t in range(2 * n)},
        compiler_params=pltpu.CompilerParams(has_side_effects=EFFECT),
    )(*arrays, *lands, send, recv, *after)
    return list(res[:n]), list(res[n:])


def _join(*parts):
    def cut(seq, key):
        res, o = [], 0
        for p in parts:
            k = len(getattr(p, key))
            res.append(seq[o:o + k])
            o += k
        return res

    def build(ins, outs, scr):
        staged, remote = [], []
        for p, i, o, s in zip(parts, cut(ins, "arrays"), cut(outs, "out_shapes"), cut(scr, "scratch")):
            st, rm = p.build(i, o, s)
            staged += st
            remote += rm
        return staged, remote

    pairs, ai, oi = [], 0, 0
    for p in parts:
        pairs += [(ai + a, oi + b) for a, b in p.alias_pairs]
        ai, oi = ai + len(p.arrays), oi + len(p.out_shapes)
    return _Behind(sum((p.arrays for p in parts), []), sum((p.out_shapes for p in parts), []),
                   sum((p.scratch for p in parts), []), build, pairs)


def _pass_halves_behind(gathered):
    n = len(gathered)

    def build(ins, outs, scr):
        send, recv = scr
        x, y, c = _coords()
        remote = []
        for t in range(n):
            half = gathered[t].shape[1] // 2
            mine = pl.ds(c * half, half)
            for k, (dx, dy) in enumerate(_REL3):
                slot = 2 * _flip(x, dx) + _flip(y, dy)
                remote.append(pltpu.make_async_remote_copy(
                    src_ref=outs[t].at[slot, mine], dst_ref=outs[t].at[slot, mine],
                    send_sem=send.at[3 * t + k], recv_sem=recv.at[3 * t + k],
                    device_id=(x, y, 1 - c), device_id_type=MESH))
        return [], remote

    return _Behind(gathered, [jax.ShapeDtypeStruct(g.shape, g.dtype) for g in gathered],
                   [pltpu.SemaphoreType.DMA((3 * n,)), pltpu.SemaphoreType.DMA((3 * n,))], build,
                   [(t, t) for t in range(n)])


def _gather_chips(shards, name):
    n = len(shards)

    def body(*refs):
        ins, outs = refs[:n], refs[n:2 * n]
        send, recv, fsend, frecv, loc = refs[2 * n:2 * n + 5]
        stage = refs[2 * n + 5:]
        x, y, c = _coords()
        me = 2 * x + y
        own = [_via_vmem(ins[t], outs[t].at[me], stage[t], loc, t) for t in range(n)]
        first, passed = [], []
        for t in range(n):
            half = shards[t].shape[0] // 2
            mine, theirs = pl.ds(c * half, half), pl.ds((1 - c) * half, half)
            for k, (dx, dy) in enumerate(_REL3):
                px, py = _flip(x, dx), _flip(y, dy)
                first.append(pltpu.make_async_remote_copy(
                    src_ref=ins[t].at[mine], dst_ref=outs[t].at[me, mine],
                    send_sem=send.at[3 * t + k], recv_sem=recv.at[3 * t + k],
                    device_id=(px, py, c), device_id_type=MESH))
                passed.append((
                    pltpu.make_async_remote_copy(
                        src_ref=outs[t].at[2 * px + py, mine], dst_ref=outs[t].at[2 * px + py, mine],
                        send_sem=fsend.at[3 * t + k], recv_sem=frecv.at[3 * t + k],
                        device_id=(x, y, 1 - c), device_id_type=MESH),
                    pltpu.make_async_remote_copy(
                        src_ref=outs[t].at[2 * px + py, theirs], dst_ref=outs[t].at[2 * px + py, theirs],
                        send_sem=fsend.at[3 * t + k], recv_sem=frecv.at[3 * t + k],
                        device_id=(x, y, 1 - c), device_id_type=MESH)))
        for load, _ in own:
            load.start()
        for cp in first:
            cp.start()
        for load, store in own:
            load.wait()
            store.start()
        for cp, (fwd, _) in zip(first, passed):
            cp.wait_recv()
            fwd.start()
        for cp, (fwd, back) in zip(first, passed):
            cp.wait_send()
            fwd.wait_send()
            back.wait_recv()
        for _, store in own:
            store.wait()

    return pl.pallas_call(
        body, name=name,
        out_shape=[jax.ShapeDtypeStruct((NSH,) + s.shape, s.dtype) for s in shards],
        in_specs=[ANY] * n, out_specs=[ANY] * n,
        scratch_shapes=[pltpu.SemaphoreType.DMA((3 * n,)) for _ in range(4)] + [pltpu.SemaphoreType.DMA((2 * n,))]
        + [pltpu.VMEM(s.shape, s.dtype) for s in shards],
        compiler_params=pltpu.CompilerParams(vmem_limit_bytes=VMEM_LIMIT),
    )(*shards)


_REL7 = tuple((dx, dy, dc) for dx in (0, 1) for dy in (0, 1) for dc in (0, 1))[1:]


def _gather_all(a, name):
    def body(a_ref, o_ref, send, recv, loc):
        x, y, c = _coords()
        me = 4 * x + 2 * y + c
        local = [pltpu.make_async_copy(a_ref, o_ref.at[me], loc.at[0])]
        remote = [pltpu.make_async_remote_copy(
            src_ref=a_ref, dst_ref=o_ref.at[me], send_sem=send.at[k], recv_sem=recv.at[k],
            device_id=(_flip(x, dx), _flip(y, dy), _flip(c, dc)), device_id_type=MESH)
            for k, (dx, dy, dc) in enumerate(_REL7)]
        _run(local, remote)

    return pl.pallas_call(
        body, name=name,
        out_shape=jax.ShapeDtypeStruct((8,) + a.shape, a.dtype),
        in_specs=[ANY], out_specs=ANY,
        scratch_shapes=[pltpu.SemaphoreType.DMA((7,)), pltpu.SemaphoreType.DMA((7,)),
                        pltpu.SemaphoreType.DMA((1,))],
    )(a)


def _pair_exchange(grads, name):
    n = len(grads)

    def body(*refs):
        ins, outs = refs[:n], refs[n:2 * n]
        send, recv = refs[2 * n:]
        x, y, c = _coords()
        remote = []
        for t in range(n):
            half = grads[t].shape[1] // 2
            remote.append(pltpu.make_async_remote_copy(
                src_ref=ins[t].at[:, pl.ds((1 - c) * half, half)], dst_ref=outs[t],
                send_sem=send.at[t], recv_sem=recv.at[t],
                device_id=(x, y, 1 - c), device_id_type=MESH))
        _run([], remote)

    return pl.pallas_call(
        body, name=name,
        out_shape=[jax.ShapeDtypeStruct((NSH, g.shape[1] // 2, g.shape[2]), g.dtype) for g in grads],
        in_specs=[ANY] * n, out_specs=[ANY] * n,
        scratch_shapes=[pltpu.SemaphoreType.DMA((n,)), pltpu.SemaphoreType.DMA((n,))],
    )(*grads)


def _scatter_chips(sums, name):
    n = len(sums)

    def body(*refs):
        ins, outs = refs[:n], refs[n:2 * n]
        send, recv, loc = refs[2 * n:2 * n + 3]
        stage = refs[2 * n + 3:]
        x, y, c = _coords()
        me = 2 * x + y
        local = [_via_vmem(ins[t].at[me], outs[t].at[me], stage[t], loc, t) for t in range(n)]
        remote = []
        for t in range(n):
            for k, (dx, dy) in enumerate(_REL3):
                px, py = _flip(x, dx), _flip(y, dy)
                remote.append(pltpu.make_async_remote_copy(
                    src_ref=ins[t].at[2 * px + py], dst_ref=outs[t].at[me],
                    send_sem=send.at[3 * t + k], recv_sem=recv.at[3 * t + k],
                    device_id=(px, py, c), device_id_type=MESH))
        _run_staged(local, remote)

    return pl.pallas_call(
        body, name=name,
        out_shape=[jax.ShapeDtypeStruct(s.shape, s.dtype) for s in sums],
        in_specs=[ANY] * n, out_specs=[ANY] * n,
        scratch_shapes=[pltpu.SemaphoreType.DMA((3 * n,)), pltpu.SemaphoreType.DMA((3 * n,)),
                        pltpu.SemaphoreType.DMA((2 * n,))]
        + [pltpu.VMEM(s.shape[1:], s.dtype) for s in sums],
        compiler_params=pltpu.CompilerParams(vmem_limit_bytes=VMEM_LIMIT),
    )(*sums)


def _swap_halves(halves, name):
    n = len(halves)

    def body(*refs):
        ins, outs = refs[:n], refs[n:2 * n]
        send, recv, loc = refs[2 * n:2 * n + 3]
        stage = refs[2 * n + 3:]
        x, y, c = _coords()
        local = [_via_vmem(ins[t], outs[t].at[c], stage[t], loc, t) for t in range(n)]
        remote = [pltpu.make_async_remote_copy(
            src_ref=ins[t], dst_ref=outs[t].at[c], send_sem=send.at[t], recv_sem=recv.at[t],
            device_id=(x, y, 1 - c), device_id_type=MESH) for t in range(n)]
        _run_staged(local, remote)

    return pl.pallas_call(
        body, name=name,
        out_shape=[jax.ShapeDtypeStruct((2,) + h.shape, h.dtype) for h in halves],
        in_specs=[ANY] * n, out_specs=[ANY] * n,
        scratch_shapes=[pltpu.SemaphoreType.DMA((n,)), pltpu.SemaphoreType.DMA((n,)),
                        pltpu.SemaphoreType.DMA((2 * n,))]
        + [pltpu.VMEM(h.shape, h.dtype) for h in halves],
        compiler_params=pltpu.CompilerParams(vmem_limit_bytes=VMEM_LIMIT),
    )(*halves)


def _sum_slots(r, name, after=None):
    K, R, C = r.shape
    tr = _tile(R, max(16, (1 << 22) // (K * C)), 8 * (4 // r.dtype.itemsize))

    def body(r_ref, *rest):
        o_ref = rest[-1]
        acc = r_ref[0].astype(F32)
        for k in range(1, K):
            acc = acc + r_ref[k].astype(F32)
        o_ref[...] = acc

    dep = [] if after is None else [after]
    return pl.pallas_call(
        body, name=name, grid=(R // tr,),
        out_shape=jax.ShapeDtypeStruct((R, C), F32),
        in_specs=[pl.BlockSpec((K, tr, C), lambda i: (0, i, 0))] + [ANY] * len(dep),
        out_specs=pl.BlockSpec((tr, C), lambda i: (i, 0)),
        compiler_params=_params("parallel"),
    )(r, *dep)


def _add_pair(g, s, core, name):
    _, half, C = s.shape
    tr = _tile(half, max(16, (1 << 19) // C))
    nb = half // tr

    def body(c_ref, g_ref, s_ref, o_ref):
        o_ref[...] = (g_ref[...].astype(F32) + s_ref[...].astype(F32)).astype(BF16)

    spec = pl.BlockSpec((1, tr, C), lambda j, i, c_ref: (j, i, 0))
    return pl.pallas_call(
        body, name=name,
        grid_spec=pltpu.PrefetchScalarGridSpec(
            num_scalar_prefetch=1, grid=(NSH, nb),
            in_specs=[pl.BlockSpec((1, tr, C), lambda j, i, c_ref: (j, c_ref[0] * nb + i, 0)), spec],
            out_specs=spec),
        out_shape=jax.ShapeDtypeStruct(s.shape, BF16),
        compiler_params=_params("parallel", "parallel"),
    )(core, g, s)


def _adamw(w, g, m, v, name):
    _, R, C = w.shape
    tr = _tile(R, max(8, (1 << 18) // C), 8)
    c1 = 1.0 / (1.0 - B1 ** STEP)
    c2 = 1.0 / (1.0 - B2 ** STEP)

    def body(w_ref, g_ref, m_ref, v_ref, go_ref, d_ref, nm_ref, nv_ref):
        gv = g_ref[...]
        go_ref[...] = gv
        nm = B1 * m_ref[...] + (1.0 - B1) * gv
        nv = B2 * v_ref[...] + (1.0 - B2) * gv * gv
        nm_ref[...] = nm
        nv_ref[...] = nv
        d_ref[...] = -LR * ((nm * c1) / (jnp.sqrt(nv * c2) + AEPS) + WD * w_ref[...])

    spec = pl.BlockSpec((1, tr, C), lambda i: (0, i, 0))
    return pl.pallas_call(
        body, name=name, grid=(R // tr,),
        out_shape=[jax.ShapeDtypeStruct((1, R, C), F32)] * 4,
        in_specs=[spec] * 4, out_specs=[spec] * 4,
        compiler_params=_params("parallel"),
    )(w, g, m, v)


def _ffn_fwd(h, g, w1, w3, w2, name, comm=None):
    L = h.shape[0]
    tm = _tile(L, 704)

    def body(h_ref, g_ref, w1_ref, w3_ref, w2_ref, o_ref, a_ref, b_ref, n_s, acc_s):
        j = pl.program_id(1)

        @pl.when(j == 0)
        def _():
            hv = h_ref[...]
            n, _ = _rms(hv, g_ref[...])
            n_s[...] = n.astype(BF16)
            acc_s[...] = hv

        n = n_s[...]
        a = _dot_nt(n, w1_ref[0])
        b = _dot_nt(n, w3_ref[0])
        a_ref[0] = a.astype(BF16)
        b_ref[0] = b.astype(BF16)
        s = (a * _sigmoid(a) * b).astype(BF16)
        acc_s[...] += 0.5 * _dot(s, w2_ref[0])

        @pl.when(j == NSH - 1)
        def _():
            o_ref[...] = acc_s[...]

    row = pl.BlockSpec((tm, D), lambda i, j: (i, 0))
    hid = pl.BlockSpec((1, tm, FS), lambda i, j: (j, i, 0))
    wsp = pl.BlockSpec((1, FS, D), lambda i, j: (j, 0, 0))
    return _call(
        body, comm, name=name, grid=(L // tm, NSH),
        out_shape=[jax.ShapeDtypeStruct((L, D), F32),
                   jax.ShapeDtypeStruct((NSH, L, FS), BF16), jax.ShapeDtypeStruct((NSH, L, FS), BF16)],
        in_specs=[row, _res((1, D)), wsp, wsp, wsp],
        out_specs=[row, hid, hid],
        scratch_shapes=[pltpu.VMEM((tm, D), BF16), pltpu.VMEM((tm, D), F32)],
        params=_params("arbitrary", "arbitrary"),
    )(h, g, w1, w3, w2)


def _loss_head(hv, gv, tv, row0):
    y, r = _rms(hv, gv)
    row = row0 + lax.broadcasted_iota(jnp.int32, (hv.shape[0], 1), 0)
    e = jnp.where(row >= FRONT, y - tv, 0.0)
    dy = e * (1.0 / D)
    part = 0.5 * jnp.sum(jnp.sum(e * dy, axis=1, keepdims=True), axis=0, keepdims=True)
    dx, xh = _rms_bwd(dy, hv, r, gv)
    return dx, part, jnp.sum(dy * xh, axis=0, keepdims=True)


def _ffn_fwd_loss(h, g, w1, w3, w2, gf, tgt, name):
    L = h.shape[0]
    tm = _tile(L, 704)

    def body(h_ref, g_ref, w1_ref, w3_ref, w2_ref, gf_ref, t_ref, o_ref, a_ref, b_ref, loss_ref, dgf_ref,
             n_s, acc_s):
        i, j = pl.program_id(0), pl.program_id(1)

        @pl.when(j == 0)
        def _():
            hv = h_ref[...]
            n, _ = _rms(hv, g_ref[...])
            n_s[...] = n.astype(BF16)
            acc_s[...] = hv

        n = n_s[...]
        a = _dot_nt(n, w1_ref[0])
        b = _dot_nt(n, w3_ref[0])
        a_ref[0] = a.astype(BF16)
        b_ref[0] = b.astype(BF16)
        s = (a * _sigmoid(a) * b).astype(BF16)
        acc_s[...] += 0.5 * _dot(s, w2_ref[0])

        @pl.when(j == NSH - 1)
        def _():
            dx, part, dgf = _loss_head(acc_s[...], gf_ref[...], t_ref[...], i * tm)
            o_ref[...] = dx
            _acc_rows(loss_ref, part, i == 0)
            _acc_rows(dgf_ref, dgf, i == 0)

    row = pl.BlockSpec((tm, D), lambda i, j: (i, 0))
    hid = pl.BlockSpec((1, tm, FS), lambda i, j: (j, i, 0))
    wsp = pl.BlockSpec((1, FS, D), lambda i, j: (j, 0, 0))
    return pl.pallas_call(
        body, name=name, grid=(L // tm, NSH),
        out_shape=[jax.ShapeDtypeStruct((L, D), F32),
                   jax.ShapeDtypeStruct((NSH, L, FS), BF16), jax.ShapeDtypeStruct((NSH, L, FS), BF16),
                   jax.ShapeDtypeStruct((1, 1), F32), jax.ShapeDtypeStruct((1, D), F32)],
        in_specs=[row, _res((1, D)), wsp, wsp, wsp, _res((1, D)), row],
        out_specs=[row, hid, hid, pl.BlockSpec((1, 1), lambda i, j: (0, 0)),
                   pl.BlockSpec((1, D), lambda i, j: (0, 0))],
        scratch_shapes=[pltpu.VMEM((tm, D), BF16), pltpu.VMEM((tm, D), F32)],
        compiler_params=_params("arbitrary", "arbitrary"),
    )(h, g, w1, w3, w2, gf, tgt)


def _ffn_bwd(h, g, dout, a, b, w1, w3, w2, name, comm=None):
    L = h.shape[0]
    tm = _tile(L, 528)

    def body(h_ref, g_ref, do_ref, a_ref, b_ref, w1_ref, w3_ref, w2_ref,
             dh_ref, da_ref, db_ref, s_ref, n_ref, dg_ref, dob_s, dn_s):
        i, j = pl.program_id(0), pl.program_id(1)

        @pl.when(j == 0)
        def _():
            n, _ = _rms(h_ref[...], g_ref[...])
            n_ref[...] = n.astype(BF16)
            dob_s[...] = (0.5 * do_ref[...]).astype(BF16)
            dn_s[...] = jnp.zeros_like(dn_s)

        av = a_ref[0].astype(F32)
        bv = b_ref[0].astype(F32)
        sig = _sigmoid(av)
        sa = av * sig
        ds = _dot_nt(dob_s[...], w2_ref[0])
        s_ref[0] = (sa * bv).astype(BF16)
        da = (ds * bv * (sig + sa * (1.0 - sig))).astype(BF16)
        db = (ds * sa).astype(BF16)
        da_ref[0] = da
        db_ref[0] = db
        dn_s[...] += _dot(da, w1_ref[0]) + _dot(db, w3_ref[0])

        @pl.when(j == NSH - 1)
        def _():
            hv = h_ref[...]
            gv = g_ref[...]
            r = lax.rsqrt(jnp.mean(hv * hv, axis=-1, keepdims=True) + EPS)
            dn = dn_s[...]
            dx, xh = _rms_bwd(dn, hv, r, gv)
            dh_ref[...] = do_ref[...] + dx
            _acc_rows(dg_ref, jnp.sum(dn * xh, axis=0, keepdims=True), i == 0)

    row = pl.BlockSpec((tm, D), lambda i, j: (i, 0))
    hid = pl.BlockSpec((1, tm, FS), lambda i, j: (j, i, 0))
    wsp = pl.BlockSpec((1, FS, D), lambda i, j: (j, 0, 0))
    return _call(
        body, comm, name=name, grid=(L // tm, NSH),
        out_shape=[jax.ShapeDtypeStruct((L, D), F32)]
        + [jax.ShapeDtypeStruct((NSH, L, FS), BF16)] * 3
        + [jax.ShapeDtypeStruct((L, D), BF16), jax.ShapeDtypeStruct((1, D), F32)],
        in_specs=[row, _res((1, D)), row, hid, hid,
                  wsp, wsp, wsp],
        out_specs=[row, hid, hid, hid, row, pl.BlockSpec((1, D), lambda i, j: (0, 0))],
        scratch_shapes=[pltpu.VMEM((tm, D), BF16), pltpu.VMEM((tm, D), F32)],
        params=_params("arbitrary", "arbitrary"),
    )(h, g, dout, a, b, w1, w3, w2)


def _wgrad(xm, ym, name, scale=1.0):
    xs, ys = xm.ndim == 3, ym.ndim == 3
    assert not (xs and ys)
    L = xm.shape[-2]
    K, N = xm.shape[-1], ym.shape[-1]
    tl = _tile(L, 2112)
    nl = L // tl
    if xs or ys:
        tn, grid_n = N, NSH
    else:
        tn = _tile(N, 1024, 128)
        grid_n = N // tn

    def body(x_ref, y_ref, o_ref, acc_s):
        l = pl.program_id(1)
        xv = x_ref[0] if xs else x_ref[...]
        yv = y_ref[0] if ys else y_ref[...]
        part = _dot_tn(xv.astype(BF16), yv.astype(BF16))
        _acc_rows(acc_s, part, l == 0)

        @pl.when(l == nl - 1)
        def _():
            res = (acc_s[...] * scale).astype(BF16)
            if xs or ys:
                o_ref[0] = res
            else:
                o_ref[...] = res

    if xs:
        x_spec = pl.BlockSpec((1, tl, K), lambda n, l: (n, l, 0))
        y_spec = pl.BlockSpec((tl, N), lambda n, l: (l, 0))
        o_spec = pl.BlockSpec((1, K, N), lambda n, l: (n, 0, 0))
        o_shape = (NSH, K, N)
    elif ys:
        x_spec = pl.BlockSpec((tl, K), lambda n, l: (l, 0))
        y_spec = pl.BlockSpec((1, tl, N), lambda n, l: (n, l, 0))
        o_spec = pl.BlockSpec((1, K, N), lambda n, l: (n, 0, 0))
        o_shape = (NSH, K, N)
    else:
        x_spec = pl.BlockSpec((tl, K), lambda n, l: (l, 0))
        y_spec = pl.BlockSpec((tl, tn), lambda n, l: (l, n))
        o_spec = pl.BlockSpec((K, tn), lambda n, l: (0, n))
        o_shape = (K, N)
    return pl.pallas_call(
        body, name=name, grid=(grid_n, nl),
        out_shape=jax.ShapeDtypeStruct(o_shape, BF16),
        in_specs=[x_spec, y_spec], out_specs=o_spec,
        scratch_shapes=[pltpu.VMEM((K, tn), F32)],
        compiler_params=_params("parallel", "arbitrary"),
    )(xm, ym)


def _mix_in_fwd(h, g, w_in, b_gate, name, comm=None):
    L = h.shape[0]
    tm = _tile(L, 528)

    def body(h_ref, g_ref, w_ref, bg_ref, vg_ref, uf_ref, gt_ref):
        u, _ = _rms(h_ref[...], g_ref[...])
        ub = u.astype(BF16)
        p = [_dot(ub, w_ref[j]) for j in range(NSH)]
        a0, a1 = 2 * DC - WS, 2 * DC + DS - WS
        vg_ref[:, 0:WS] = p[0].astype(BF16)
        vg_ref[:, WS:2 * DC] = p[1][:, 0:a0].astype(BF16)
        uf_ref[...] = p[1][:, a0:a1].astype(BF16)
        gin = jnp.concatenate([p[1][:, a1:], p[2], p[3]], axis=1)
        gt_ref[...] = _sigmoid(gin + bg_ref[...]).astype(BF16)

    def row(n):
        return pl.BlockSpec((tm, n), lambda i: (i, 0))

    return _call(
        body, comm, name=name, grid=(L // tm,),
        out_shape=[jax.ShapeDtypeStruct((L, 2 * DC), BF16), jax.ShapeDtypeStruct((L, DS), BF16),
                   jax.ShapeDtypeStruct((L, 2 * D), BF16)],
        in_specs=[row(D), _res((1, D)), _res((NSH, D, WS)), _res((1, 2 * D))],
        out_specs=[row(2 * DC), row(DS), row(2 * D)],
        params=_params("parallel"),
    )(h, g, w_in, b_gate)


def _mix_in_bwd(h, g, dres, dv, dgl, duf, dgate, w_in, name):
    L = h.shape[0]
    tm = _tile(L, 528)

    def body(h_ref, g_ref, dr_ref, dv_ref, dgl_ref, duf_ref, dgt_ref, w_ref, dh_ref, u_ref, dp_ref, dgm_ref):
        i = pl.program_id(0)
        hv = h_ref[...]
        gv = g_ref[...]
        u, r = _rms(hv, gv)
        u_ref[...] = u.astype(BF16)
        a0, a1 = 2 * DC - WS, 2 * DC + DS - WS
        b0 = WS - a1
        dp = [jnp.concatenate([dv_ref[...], dgl_ref[:, 0:WS - DC]], axis=1),
              jnp.concatenate([dgl_ref[:, WS - DC:], duf_ref[...], dgt_ref[:, 0:b0]], axis=1),
              dgt_ref[:, b0:b0 + WS], dgt_ref[:, b0 + WS:]]
        du = jnp.zeros((tm, D), F32)
        for j in range(NSH):
            dp_ref[j] = dp[j]
            du = du + _dot_nt(dp[j], w_ref[j])
        dx, xh = _rms_bwd(du, hv, r, gv)
        dh_ref[...] = dr_ref[...] + dx
        _acc_rows(dgm_ref, jnp.sum(du * xh, axis=0, keepdims=True), i == 0)

    def row(n):
        return pl.BlockSpec((tm, n), lambda i: (i, 0))

    return pl.pallas_call(
        body, name=name, grid=(L // tm,),
        out_shape=[jax.ShapeDtypeStruct((L, D), F32), jax.ShapeDtypeStruct((L, D), BF16),
                   jax.ShapeDtypeStruct((NSH, L, WS), BF16), jax.ShapeDtypeStruct((1, D), F32)],
        in_specs=[row(D), _res((1, D)), row(D), row(DC), row(DC), row(DS), row(2 * D), _res((NSH, D, WS))],
        out_specs=[row(D), row(D), pl.BlockSpec((NSH, tm, WS), lambda i: (0, i, 0)),
                   pl.BlockSpec((1, D), lambda i: (0, 0))],
        compiler_params=_params("arbitrary"),
    )(h, g, dres, dv, dgl, duf, dgate, w_in)


def _conv_fwd(vg, dw, dwb, name, comm=None):
    L = vg.shape[0]
    nc = DC // 128

    def body(v_ref, g_ref, dw_ref, dwb_ref, z_ref, zp_s):
        zp_s[0:KWP, :] = jnp.zeros((KWP, 128), F32)
        zp_s[KWP:, :] = v_ref[...].astype(F32) * _sigmoid(g_ref[...].astype(F32))
        for r0 in range(0, L, CONV_ROWS):
            acc = jnp.broadcast_to(dwb_ref[...], (CONV_ROWS, 128))
            for k in range(KW):
                acc = acc + dw_ref[k:k + 1, :] * zp_s[pl.ds(r0 + k + 2, CONV_ROWS), :]
            z_ref[pl.ds(r0, CONV_ROWS), :] = acc

    return _call(
        body, comm, name=name, grid=(nc,),
        out_shape=[jax.ShapeDtypeStruct((L, DC), F32)],
        in_specs=[pl.BlockSpec((L, 128), lambda c: (0, c)), pl.BlockSpec((L, 128), lambda c: (0, nc + c)),
                  pl.BlockSpec((KWP, 128), lambda c: (0, c)), pl.BlockSpec((1, 128), lambda c: (0, c))],
        out_specs=[pl.BlockSpec((L, 128), lambda c: (0, c))],
        scratch_shapes=[pltpu.VMEM((L + KWP, 128), F32)],
        params=_params("parallel"),
    )(vg, vg, dw, dwb)


def _conv_bwd(dz1, vg, dw, name):
    L = vg.shape[0]
    nc = DC // 128

    def body(dz_ref, v_ref, g_ref, dw_ref, dv_ref, dg_ref, ddw_ref, ddwb_ref, zp_s, dzp_s):
        vv = v_ref[...].astype(F32)
        sg = _sigmoid(g_ref[...].astype(F32))
        zp_s[0:KWP, :] = jnp.zeros((KWP, 128), F32)
        zp_s[KWP:, :] = vv * sg
        dz = dz_ref[...]
        dzp_s[0:L, :] = dz
        dzp_s[L:, :] = jnp.zeros((KWP, 128), F32)
        ddwb_ref[...] = jnp.sum(dz, axis=0, keepdims=True)
        part = [jnp.zeros((8, 128), F32) for _ in range(KW)]
        for r0 in range(0, L, CONV_ROWS):
            rows = pl.ds(r0, CONV_ROWS)
            dzc = dz_ref[rows, :]
            acc = jnp.zeros((CONV_ROWS, 128), F32)
            for k in range(KW):
                acc = acc + dw_ref[k:k + 1, :] * dzp_s[pl.ds(r0 + KW - 1 - k, CONV_ROWS), :]
                prod = dzc * zp_s[pl.ds(r0 + k + 2, CONV_ROWS), :]
                for q in range(CONV_ROWS // 8):
                    part[k] = part[k] + prod[8 * q:8 * (q + 1), :]
            vc = v_ref[rows, :].astype(F32)
            sc = _sigmoid(g_ref[rows, :].astype(F32))
            dv_ref[rows, :] = (acc * sc).astype(BF16)
            dg_ref[rows, :] = (acc * vc * sc * (1.0 - sc)).astype(BF16)
        for k in range(KW):
            ddw_ref[k:k + 1, :] = jnp.sum(part[k], axis=0, keepdims=True)
        ddw_ref[KW:KWP, :] = jnp.zeros((KWP - KW, 128), F32)

    col = pl.BlockSpec((L, 128), lambda c: (0, c))
    return pl.pallas_call(
        body, name=name, grid=(nc,),
        out_shape=[jax.ShapeDtypeStruct((L, DC), BF16), jax.ShapeDtypeStruct((L, DC), BF16),
                   jax.ShapeDtypeStruct((KWP, DC), F32), jax.ShapeDtypeStruct((1, DC), F32)],
        in_specs=[col, col, pl.BlockSpec((L, 128), lambda c: (0, nc + c)),
                  pl.BlockSpec((KWP, 128), lambda c: (0, c))],
        out_specs=[col, col, pl.BlockSpec((KWP, 128), lambda c: (0, c)), pl.BlockSpec((1, 128), lambda c: (0, c))],
        scratch_shapes=[pltpu.VMEM((L + KWP, 128), F32), pltpu.VMEM((L + KWP, 128), F32)],
        compiler_params=_params("parallel"),
    )(dz1, vg, vg, dw)


NLB = QS // 128


def _lb_store(ref, rows, val):
    for cb in range(NLB):
        ref[cb, rows, :] = val[:, cb * 128:(cb + 1) * 128]


def _lb_load(ref, rows):
    return jnp.concatenate([ref[cb, rows, :] for cb in range(NLB)], axis=1)


def _scan(xr_ref, xi_ref, base, T, ar, ai, atr, ati, reverse):
    W = ar.shape[1]
    ar, ai, atr, ati = (jnp.broadcast_to(v, (8, W)) for v in (ar, ai, atr, ati))
    zero = jnp.zeros((8, W), F32)

    def rows(t, g):
        tt = T - 1 - t if reverse else t
        return pl.ds(base + g * 8 * T + tt, 8, stride=T)

    def make_step(store):
        def step(t, carry):
            out = []
            for g in range(NGRP):
                sr, si = carry[2 * g], carry[2 * g + 1]
                idx = rows(t, g)
                nr = ar * sr - ai * si + _lb_load(xr_ref, idx)
                ni = ar * si + ai * sr + _lb_load(xi_ref, idx)
                if store:
                    _lb_store(xr_ref, idx, nr)
                    _lb_store(xi_ref, idx, ni)
                out += [nr, ni]
            return tuple(out)
        return step

    ends = lax.fori_loop(0, T, make_step(False), (zero,) * (2 * NGRP))
    sub = lax.broadcasted_iota(jnp.int32, (8, W), 0)
    edge = sub == (7 if reverse else 0)
    shift, last = (7, 0) if reverse else (1, 7)
    inr, ini = jnp.zeros((1, W), F32), jnp.zeros((1, W), F32)
    starts = [None] * (2 * NGRP)
    for g in (reversed(range(NGRP)) if reverse else range(NGRP)):
        er, ei = ends[2 * g], ends[2 * g + 1]
        cr, ci = jnp.where(edge, inr, 0.0), jnp.where(edge, ini, 0.0)
        for _ in range(7):
            nr = atr * cr - ati * ci + er
            ni = atr * ci + ati * cr + ei
            cr = jnp.where(edge, inr, pltpu.roll(nr, shift, 0))
            ci = jnp.where(edge, ini, pltpu.roll(ni, shift, 0))
        starts[2 * g], starts[2 * g + 1] = cr, ci
        inr = (atr * cr - ati * ci + er)[last:last + 1]
        ini = (atr * ci + ati * cr + ei)[last:last + 1]
    lax.fori_loop(0, T, make_step(True), tuple(starts))


def _ssm_fwd(uf, bre, bim, cre, cim, lamp, dsk, name, comm=None):
    L = uf.shape[0]
    T = L // NSEG
    tc = L // NCH

    def body(u_ref, bre_ref, bim_ref, cre_ref, cim_ref, lam_ref, d_ref, y_ref, sr_s, si_s):
        for k in range(NCH):
            sl = slice(k * tc, (k + 1) * tc)
            uk = u_ref[sl, :]
            _lb_store(sr_s, sl, _dot(uk, bre_ref[0]))
            _lb_store(si_s, sl, _dot(uk, bim_ref[0]))
        _scan(sr_s, si_s, 0, T, lam_ref[0:1, :], lam_ref[1:2, :], lam_ref[2:3, :], lam_ref[3:4, :], False)
        for k in range(NCH):
            sl = slice(k * tc, (k + 1) * tc)
            y_ref[sl, :] = (_dot(_lb_load(sr_s, sl).astype(BF16), cre_ref[0])
                            - _dot(_lb_load(si_s, sl).astype(BF16), cim_ref[0])
                            + d_ref[...] * u_ref[sl, :].astype(F32))

    return _call(
        body, comm, name=name, grid=(NQ,),
        out_shape=[jax.ShapeDtypeStruct((L, DS), F32)],
        in_specs=[pl.BlockSpec((L, QU), lambda q: (0, q)),
                  pl.BlockSpec((1, QU, QS), lambda q: (q, 0, 0)), pl.BlockSpec((1, QU, QS), lambda q: (q, 0, 0)),
                  pl.BlockSpec((1, QS, QU), lambda q: (q, 0, 0)), pl.BlockSpec((1, QS, QU), lambda q: (q, 0, 0)),
                  pl.BlockSpec((8, QS), lambda q: (0, q)), pl.BlockSpec((1, QU), lambda q: (0, q))],
        out_specs=[pl.BlockSpec((L, QU), lambda q: (0, q))],
        scratch_shapes=[pltpu.VMEM((NLB, L, 128), F32), pltpu.VMEM((NLB, L, 128), F32)],
        params=_params("parallel"),
    )(uf, bre, bim, cre, cim, lamp, dsk)


def _ssm_bwd(uf, dyss, bre, bim, cre, cim, lamp, dsk, name, comm=None):
    L = uf.shape[0]
    T = L // NSEG
    tc = L // NCH

    def body(u_ref, dy_ref, bre_ref, bim_ref, cre_ref, cim_ref, lam_ref, d_ref,
             du_ref, dbre_ref, dbim_ref, dcre_ref, dcim_ref, dlam_ref, dd_ref, sr_s, si_s, gr_s, gi_s):
        _lb_store(sr_s, slice(0, SOFF), jnp.zeros((SOFF, QS), F32))
        _lb_store(si_s, slice(0, SOFF), jnp.zeros((SOFF, QS), F32))
        for k in range(NCH):
            sl = slice(k * tc, (k + 1) * tc)
            ss = slice(SOFF + k * tc, SOFF + (k + 1) * tc)
            uk = u_ref[sl, :]
            dyk = dy_ref[sl, :].astype(BF16)
            _lb_store(sr_s, ss, _dot(uk, bre_ref[0]))
            _lb_store(si_s, ss, _dot(uk, bim_ref[0]))
            _lb_store(gr_s, sl, _dot_nt(dyk, cre_ref[0]))
            _lb_store(gi_s, sl, -_dot_nt(dyk, cim_ref[0]))
        ar, ai, atr, ati = lam_ref[0:1, :], lam_ref[1:2, :], lam_ref[2:3, :], lam_ref[3:4, :]
        _scan(sr_s, si_s, SOFF, T, ar, ai, atr, ati, False)
        _scan(gr_s, gi_s, 0, T, ar, -ai, atr, -ati, True)
        dbre = jnp.zeros((QU, QS), F32)
        dbim = jnp.zeros((QU, QS), F32)
        dcre = jnp.zeros((QS, QU), F32)
        dcim = jnp.zeros((QS, QU), F32)
        dd = jnp.zeros((1, QU), F32)
        qr = jnp.zeros((1, QS), F32)
        qi = jnp.zeros((1, QS), F32)
        for k in range(NCH):
            sl = slice(k * tc, (k + 1) * tc)
            ss = slice(SOFF + k * tc, SOFF + (k + 1) * tc)
            sp = slice(SOFF - 1 + k * tc, SOFF - 1 + (k + 1) * tc)
            uk = u_ref[sl, :]
            dyk = dy_ref[sl, :]
            dyb = dyk.astype(BF16)
            gr, gi = _lb_load(gr_s, sl), _lb_load(gi_s, sl)
            pr, pi = _lb_load(sr_s, sp), _lb_load(si_s, sp)
            qr = qr + jnp.sum(gr * pr + gi * pi, axis=0, keepdims=True)
            qi = qi + jnp.sum(gi * pr - gr * pi, axis=0, keepdims=True)
            grb, gib = gr.astype(BF16), gi.astype(BF16)
            du_ref[sl, :] = (_dot_nt(grb, bre_ref[0]) + _dot_nt(gib, bim_ref[0])
                             + dyk * d_ref[...]).astype(BF16)
            dbre = dbre + _dot_tn(uk, grb)
            dbim = dbim + _dot_tn(uk, gib)
            dcre = dcre + _dot_tn(_lb_load(sr_s, ss).astype(BF16), dyb)
            dcim = dcim - _dot_tn(_lb_load(si_s, ss).astype(BF16), dyb)
            dd = dd + jnp.sum(dyk * uk.astype(F32), axis=0, keepdims=True)
        dlam_ref[0] = jnp.concatenate([qr, qi, jnp.zeros((6, QS), F32)], axis=0)
        dbre_ref[0] = dbre
        dbim_ref[0] = dbim
        dcre_ref[0] = dcre
        dcim_ref[0] = dcim
        dd_ref[...] = dd

    col = pl.BlockSpec((L, QU), lambda q: (0, q))
    bsp = pl.BlockSpec((1, QU, QS), lambda q: (q, 0, 0))
    csp = pl.BlockSpec((1, QS, QU), lambda q: (q, 0, 0))
    return _call(
        body, comm, name=name, grid=(NQ,),
        out_shape=[jax.ShapeDtypeStruct((L, DS), BF16),
                   jax.ShapeDtypeStruct((NQ, QU, QS), F32), jax.ShapeDtypeStruct((NQ, QU, QS), F32),
                   jax.ShapeDtypeStruct((NQ, QS, QU), F32), jax.ShapeDtypeStruct((NQ, QS, QU), F32),
                   jax.ShapeDtypeStruct((NQ, 8, QS), F32), jax.ShapeDtypeStruct((1, DS), F32)],
        in_specs=[col, col, bsp, bsp, csp, csp,
                  pl.BlockSpec((8, QS), lambda q: (0, q)), pl.BlockSpec((1, QU), lambda q: (0, q))],
        out_specs=[col,
                   pl.BlockSpec((1, QU, QS), lambda q: (q, 0, 0)), pl.BlockSpec((1, QU, QS), lambda q: (q, 0, 0)),
                   pl.BlockSpec((1, QS, QU), lambda q: (q, 0, 0)), pl.BlockSpec((1, QS, QU), lambda q: (q, 0, 0)),
                   pl.BlockSpec((1, 8, QS), lambda q: (q, 0, 0)), pl.BlockSpec((1, QU), lambda q: (0, q))],
        scratch_shapes=[pltpu.VMEM((NLB, L + SOFF, 128), F32), pltpu.VMEM((NLB, L + SOFF, 128), F32),
                        pltpu.VMEM((NLB, L, 128), F32), pltpu.VMEM((NLB, L, 128), F32)],
        params=_params("parallel"),
    )(uf, dyss, bre, bim, cre, cim, lamp, dsk)


def _branches(z1_ref, yss_ref, gt_ref, lng_ref, lnb_ref, wp_ref, wv_ref, wg_ref):
    zf = z1_ref[...]
    mu = jnp.mean(zf, axis=-1, keepdims=True)
    zc = zf - mu
    rstd = lax.rsqrt(jnp.mean(zc * zc, axis=-1, keepdims=True) + EPS)
    zn = zc * rstd
    z2 = zn * lng_ref[...] + lnb_ref[...]
    sz = _sigmoid(z2)
    z3 = (z2 * sz).astype(BF16)
    y_conv = _dot(z3, wp_ref[...])
    yss = yss_ref[...]
    yg = _gelu(yss).astype(BF16)
    sv = _dot(yg, wv_ref[...])
    sig = _sigmoid(_dot(yg, wg_ref[...]))
    y_ssm = sv * sig
    gc = gt_ref[:, 0:D].astype(F32)
    gs = gt_ref[:, D:2 * D].astype(F32)
    m = gc * y_conv + gs * y_ssm
    return dict(rstd=rstd, zn=zn, z2=z2, sz=sz, z3=z3, y_conv=y_conv, yss=yss, yg=yg, sv=sv, sig=sig,
                y_ssm=y_ssm, gc=gc, gs=gs, m=m)


def _merge_fwd(h, z1, yss, gate, lng, lnb, wp, wv, wg, wo, name, comm=None):
    L = h.shape[0]
    tm = _tile(L, 528)

    def body(h_ref, z1_ref, yss_ref, gt_ref, lng_ref, lnb_ref, wp_ref, wv_ref, wg_ref, wo_ref, o_ref):
        f = _branches(z1_ref, yss_ref, gt_ref, lng_ref, lnb_ref, wp_ref, wv_ref, wg_ref)
        o_ref[...] = h_ref[...] + _dot(f["m"].astype(BF16), wo_ref[...])

    def row(n):
        return pl.BlockSpec((tm, n), lambda i: (i, 0))

    return _call(
        body, comm, name=name, grid=(L // tm,),
        out_shape=[jax.ShapeDtypeStruct((L, D), F32)],
        in_specs=[row(D), row(DC), row(DS), row(2 * D), _res((1, DC)), _res((1, DC)),
                  _res((DC, D)), _res((DS, D)), _res((DS, D)), _res((D, D))],
        out_specs=[row(D)],
        params=_params("parallel"),
    )(h, z1, yss, gate, lng, lnb, wp, wv, wg, wo)


def _merge_bwd(dh, z1, yss, gate, lng, lnb, wp, wv, wg, wo, name):
    L = dh.shape[0]
    tm = _tile(L, 352)

    def body(dh_ref, z1_ref, yss_ref, gt_ref, lng_ref, lnb_ref, wp_ref, wv_ref, wg_ref, wo_ref,
             m_ref, dgt_ref, dyc_ref, z3_ref, dz1_ref, yg_ref, dsv_ref, dsg_ref, dyss_ref,
             dbg_ref, dlng_ref, dlnb_ref):
        i = pl.program_id(0)
        f = _branches(z1_ref, yss_ref, gt_ref, lng_ref, lnb_ref, wp_ref, wv_ref, wg_ref)
        gc, gs, sig, sv = f["gc"], f["gs"], f["sig"], f["sv"]
        m_ref[...] = f["m"].astype(BF16)
        z3_ref[...] = f["z3"]
        yg_ref[...] = f["yg"]
        dm = _dot_nt(dh_ref[...].astype(BF16), wo_ref[...])
        dgc = (dm * f["y_conv"] * gc * (1.0 - gc)).astype(BF16)
        dgs = (dm * f["y_ssm"] * gs * (1.0 - gs)).astype(BF16)
        dgt_ref[:, 0:D] = dgc
        dgt_ref[:, D:2 * D] = dgs
        part = jnp.concatenate([jnp.sum(dgc.astype(F32), axis=0, keepdims=True),
                                jnp.sum(dgs.astype(F32), axis=0, keepdims=True)], axis=1)
        _acc_rows(dbg_ref, part, i == 0)
        dyc = (dm * gc).astype(BF16)
        dyc_ref[...] = dyc
        dys = dm * gs
        dsv = (dys * sig).astype(BF16)
        dsg = (dys * sv * sig * (1.0 - sig)).astype(BF16)
        dsv_ref[...] = dsv
        dsg_ref[...] = dsg
        dyg = _dot_nt(dsv, wv_ref[...]) + _dot_nt(dsg, wg_ref[...])
        dyss_ref[...] = dyg * _gelu_grad(f["yss"])
        dz3 = _dot_nt(dyc, wp_ref[...])
        z2, sz, zn = f["z2"], f["sz"], f["zn"]
        dz2 = dz3 * sz * (1.0 + z2 * (1.0 - sz))
        _acc_rows(dlng_ref, jnp.sum(dz2 * zn, axis=0, keepdims=True), i == 0)
        _acc_rows(dlnb_ref, jnp.sum(dz2, axis=0, keepdims=True), i == 0)
        dzn = dz2 * lng_ref[...]
        dz1_ref[...] = f["rstd"] * (dzn - jnp.mean(dzn, axis=-1, keepdims=True)
                                    - zn * jnp.mean(dzn * zn, axis=-1, keepdims=True))

    def row(n):
        return pl.BlockSpec((tm, n), lambda i: (i, 0))

    def tot(n):
        return pl.BlockSpec((1, n), lambda i: (0, 0))

    return pl.pallas_call(
        body, name=name, grid=(L // tm,),
        out_shape=[jax.ShapeDtypeStruct((L, D), BF16), jax.ShapeDtypeStruct((L, 2 * D), BF16),
                   jax.ShapeDtypeStruct((L, D), BF16), jax.ShapeDtypeStruct((L, DC), BF16),
                   jax.ShapeDtypeStruct((L, DC), F32), jax.ShapeDtypeStruct((L, DS), BF16),
                   jax.ShapeDtypeStruct((L, D), BF16), jax.ShapeDtypeStruct((L, D), BF16),
                   jax.ShapeDtypeStruct((L, DS), F32),
                   jax.ShapeDtypeStruct((1, 2 * D), F32), jax.ShapeDtypeStruct((1, DC), F32),
                   jax.ShapeDtypeStruct((1, DC), F32)],
        in_specs=[row(D), row(DC), row(DS), row(2 * D), _res((1, DC)), _res((1, DC)),
                  _res((DC, D)), _res((DS, D)), _res((DS, D)), _res((D, D))],
        out_specs=[row(D), row(2 * D), row(D), row(DC), row(DC), row(DS), row(D), row(D), row(DS),
                   tot(2 * D), tot(DC), tot(DC)],
        compiler_params=_params("arbitrary"),
    )(dh, z1, yss, gate, lng, lnb, wp, wv, wg, wo)


def _ssm_disc(lam_re, lam_im, log_dt, b_re, b_im):
    lam = lax.complex(lam_re, lam_im)
    dt = jnp.exp(log_dt)[:, None]
    lam_bar = jnp.exp(lam * dt)
    bbar = ((lam_bar - 1.0) / lam)[..., None] * lax.complex(b_re, b_im)
    return jnp.real(lam_bar), jnp.imag(lam_bar), jnp.real(bbar), jnp.imag(bbar)


def _bdiag_in(m):
    m4 = m.reshape(NQ, G // NQ, P, H)
    return jnp.einsum("qgph,gk->qghkp", m4, jnp.eye(G // NQ, dtype=m.dtype)).reshape(NQ, QU, QS)


def _bdiag_out(m):
    m4 = m.reshape(NQ, G // NQ, H, P)
    return jnp.einsum("qghp,gk->qgpkh", m4, jnp.eye(G // NQ, dtype=m.dtype)).reshape(NQ, QS, QU)


def _diag_blocks(m4):
    return jnp.einsum("qiaib->qiab", m4).reshape(G, m4.shape[2], m4.shape[4])


def _pack(parts, rows_mult=8):
    flat = jnp.concatenate([p.reshape(-1).astype(F32) for p in parts])
    n = flat.shape[0]
    tot = -(-n // (128 * rows_mult)) * (128 * rows_mult)
    return jnp.pad(flat, (0, tot - n)).reshape(tot // 128, 128)


def _unpack(buf, shapes):
    flat = buf.reshape(-1)
    out, o = [], 0
    for s in shapes:
        n = math.prod(s)
        out.append(flat[o:o + n].reshape(s))
        o += n
    return out


def kernel(x, meta_tokens, ffn1_norm, ffn1_w1, ffn1_w3, ffn1_w2, mix_norm, w_in, b_gate, conv_dw, conv_dw_b, conv_ln_g, conv_ln_b, conv_proj, ssm_lam_re, ssm_lam_im, ssm_log_dt, ssm_b_re, ssm_b_im, ssm_c_re, ssm_c_im, ssm_d, ssm_w_v, ssm_w_g, w_out, ffn2_norm, ffn2_w1, ffn2_w3, ffn2_w2, final_norm, loss_target, m_meta_tokens, m_ffn1_norm, m_ffn1_w1, m_ffn1_w3, m_ffn1_w2, m_mix_norm, m_w_in, m_b_gate, m_conv_dw, m_conv_dw_b, m_conv_ln_g, m_conv_ln_b, m_conv_proj, m_ssm_lam_re, m_ssm_lam_im, m_ssm_log_dt, m_ssm_b_re, m_ssm_b_im, m_ssm_c_re, m_ssm_c_im, m_ssm_d, m_ssm_w_v, m_ssm_w_g, m_w_out, m_ffn2_norm, m_ffn2_w1, m_ffn2_w3, m_ffn2_w2, m_final_norm, v_meta_tokens, v_ffn1_norm, v_ffn1_w1, v_ffn1_w3, v_ffn1_w2, v_mix_norm, v_w_in, v_b_gate, v_conv_dw, v_conv_dw_b, v_conv_ln_g, v_conv_ln_b, v_conv_proj, v_ssm_lam_re, v_ssm_lam_im, v_ssm_log_dt, v_ssm_b_re, v_ssm_b_im, v_ssm_c_re, v_ssm_c_im, v_ssm_d, v_ssm_w_v, v_ssm_w_g, v_w_out, v_ffn2_norm, v_ffn2_w1, v_ffn2_w3, v_ffn2_w2, v_final_norm):
    args = dict(locals())
    names = ["meta_tokens", "ffn1_norm", "ffn1_w1", "ffn1_w3", "ffn1_w2", "mix_norm", "w_in", "b_gate",
             "conv_dw", "conv_dw_b", "conv_ln_g", "conv_ln_b", "conv_proj", "ssm_lam_re", "ssm_lam_im",
             "ssm_log_dt", "ssm_b_re", "ssm_b_im", "ssm_c_re", "ssm_c_im", "ssm_d", "ssm_w_v", "ssm_w_g",
             "w_out", "ffn2_norm", "ffn2_w1", "ffn2_w3", "ffn2_w2", "final_norm"]
    big = ["ffn1_w1", "ffn1_w3", "ffn1_w2", "w_in", "conv_proj", "ssm_w_v", "ssm_w_g", "w_out",
           "ffn2_w1", "ffn2_w3", "ffn2_w2"]
    small = [n for n in names if n not in big]

    xs = x[0]
    S = xs.shape[0]
    L = FRONT + S
    T = L // NSEG
    jx, jy = lax.axis_index("x"), lax.axis_index("y")
    chip = 2 * jx + jy

    small_all = _gather_all(_pack([meta_tokens, conv_dw[0]]), "gather_small")
    sm = small_all[0::2].reshape(NSH, -1)
    nmt = NMETA * (D // NSH)
    ndw = KW * (DC // NSH)
    meta_full = sm[:, :nmt].reshape(NSH, NMETA, D // NSH).transpose(1, 0, 2).reshape(NMETA, D)
    dw_full = sm[:, nmt:nmt + ndw].reshape(NSH, KW, DC // NSH).transpose(1, 0, 2).reshape(KW, DC)
    dw_pad = jnp.pad(dw_full, ((0, KWP - KW), (0, 0)))
    tposed = ("ffn1_w1", "ffn1_w3", "ffn2_w1", "ffn2_w3")

    def view(a, n):
        return jnp.swapaxes(a, 1, 2) if n in tposed else a

    grp_a = ["ffn1_w1", "ffn1_w3", "ffn1_w2"]
    grp_b = ["w_in", "conv_proj", "ssm_w_v", "ssm_w_g", "w_out"]
    grp_c = ["ffn2_w1", "ffn2_w3", "ffn2_w2"]

    def shard(n):
        return view(args[n], n)[0].astype(BF16)

    sh_a = [shard(n) for n in grp_a]
    ga_send, ga_recv, sh_a, land_a, _ = _chips_start(
        sh_a, [jax.ShapeDtypeStruct((NSH,) + s.shape, s.dtype) for s in sh_a], True, "gather_ffn1_start",
        [small_all])

    def cols(w):
        return w.transpose(1, 0, 2).reshape(w.shape[1], -1)

    disc_in = (ssm_lam_re[0], ssm_lam_im[0], ssm_log_dt[0], ssm_b_re[0], ssm_b_im[0])
    (lbr, lbi, bbr, bbi), disc_vjp = jax.vjp(_ssm_disc, *disc_in)
    lam_t = jnp.exp(lax.complex(ssm_lam_re[0], ssm_lam_im[0]) * (jnp.exp(ssm_log_dt[0])[:, None] * T))
    lamp = jnp.concatenate([lbr.reshape(1, NST), lbi.reshape(1, NST), jnp.real(lam_t).reshape(1, NST),
                            jnp.imag(lam_t).reshape(1, NST), jnp.zeros((4, NST), F32)], axis=0)
    bre_bd, bim_bd = _bdiag_in(bbr).astype(BF16), _bdiag_in(bbi).astype(BF16)
    cre_bd, cim_bd = _bdiag_out(ssm_c_re[0]).astype(BF16), _bdiag_out(ssm_c_im[0]).astype(BF16)

    h0 = lax.dynamic_update_slice(jnp.pad(xs, ((FRONT, 0), (0, 0))), meta_full, (FRONT - NMETA, 0))
    sh_a, land_a = _chips_wait(ga_send, ga_recv, sh_a, land_a, [h0], True, "gather_ffn1_wait")
    gw = dict(zip(grp_a, _pass_halves(land_a, "pass_ffn1", sh_a)))
    tgt = jnp.pad(loss_target[0], ((FRONT, 0), (0, 0)))
    (h1, a1, b1), got = _ffn_fwd(h0, ffn1_norm, gw["ffn1_w1"], gw["ffn1_w3"], gw["ffn1_w2"], "ffn1_fwd",
                                 _gather_half_behind([shard(n) for n in grp_b]))
    w_in_f = _pass_halves(got[:1], "pass_w_in")[0]
    (vg, uf, gate), got1 = _mix_in_fwd(h1, mix_norm, w_in_f, b_gate, "mix_in_fwd",
                                       _join(_gather_half_behind([shard("ffn2_w1")]),
                                             _pass_halves_behind(list(got[1:]))))
    gw.update(zip(grp_b[1:], got1[1:]))
    wp_f, wv_f, wg_f = cols(gw["conv_proj"]), cols(gw["ssm_w_v"]), cols(gw["ssm_w_g"])
    wo_f = gw["w_out"].reshape(D, D)
    (z1,), got3 = _conv_fwd(vg, dw_pad, conv_dw_b, "conv_fwd", _gather_half_behind([shard("ffn2_w3")]))
    (yss,), got2 = _ssm_fwd(uf, bre_bd, bim_bd, cre_bd, cim_bd, lamp, ssm_d, "ssm_fwd",
                            _gather_half_behind([shard("ffn2_w2")]))
    (h2,), got_c = _merge_fwd(h1, z1, yss, gate, conv_ln_g, conv_ln_b, wp_f, wv_f, wg_f, wo_f, "merge_fwd",
                              _pass_halves_behind([got1[0], got3[0], got2[0]]))
    gw.update(zip(grp_c, got_c))
    dh3, a2, b2, loss_part, d_final = _ffn_fwd_loss(
        h2, ffn2_norm, gw["ffn2_w1"], gw["ffn2_w3"], gw["ffn2_w2"], final_norm.reshape(1, D), tgt, "ffn2_fwd_loss")

    gbig = {}
    core = lax.axis_index("c").astype(jnp.int32).reshape(1)

    def pair_sums(group, tag):
        gl = [gbig[n] for n in group]
        sib = _pair_exchange(gl, "pair_exchange_" + tag)
        out = []
        for n, g_, s_ in zip(group, gl, sib):
            out.append(_add_pair(g_, s_, core, "pair_" + n))
        return out

    (dh2, da2, db2, s2, n2, d_ffn2_norm), _ = _ffn_bwd(
        h2, ffn2_norm, dh3, a2, b2, gw["ffn2_w1"], gw["ffn2_w3"], gw["ffn2_w2"], "ffn2_bwd")
    gbig["ffn2_w1"] = _wgrad(da2, n2, "ffn2_dw1")
    gbig["ffn2_w3"] = _wgrad(db2, n2, "ffn2_dw3")
    gbig["ffn2_w2"] = _wgrad(s2, dh3, "ffn2_dw2", 0.5)
    pair_c = pair_sums(grp_c, "ffn2")
    (m_b, dgate, dyc, z3, dz1, yg, dsv, dsg, dyss, d_b_gate, d_ln_g, d_ln_b) = _merge_bwd(
        dh2, z1, yss, gate, conv_ln_g, conv_ln_b, wp_f, wv_f, wg_f, wo_f, "merge_bwd")
    gbig["w_out"] = _wgrad(m_b, dh2, "dw_out").reshape(NSH, D // NSH, D)

    def shard_cols(gm):
        return gm.reshape(gm.shape[0], NSH, -1).transpose(1, 0, 2)

    gbig["conv_proj"] = shard_cols(_wgrad(z3, dyc, "dw_proj"))
    gbig["ssm_w_v"] = shard_cols(_wgrad(yg, dsv, "dw_v"))
    gbig["ssm_w_g"] = shard_cols(_wgrad(yg, dsg, "dw_g"))
    dv, dgl, ddw, d_dw_b = _conv_bwd(dz1, vg, dw_pad, "conv_bwd")
    (duf, dbre, dbim, dcre, dcim, dlam, d_ssm_d), recv_c = _ssm_bwd(
        uf, dyss, bre_bd, bim_bd, cre_bd, cim_bd, lamp, ssm_d, "ssm_bwd", _scatter_chips_behind(pair_c))
    dh1, u_b, dproj, d_mix_norm = _mix_in_bwd(h1, mix_norm, dh2, dv, dgl, duf, dgate, w_in_f, "mix_in_bwd")
    gbig["w_in"] = _wgrad(u_b, dproj, "dw_in")
    pair_b = pair_sums(grp_b, "mix")

    d_bbr = _diag_blocks(dbre.reshape(NQ, 8, H, 8, P)).transpose(0, 2, 1)
    d_bbi = _diag_blocks(dbim.reshape(NQ, 8, H, 8, P)).transpose(0, 2, 1)
    d_c_re = _diag_blocks(dcre.reshape(NQ, 8, P, 8, H)).transpose(0, 2, 1)
    d_c_im = _diag_blocks(dcim.reshape(NQ, 8, P, 8, H)).transpose(0, 2, 1)
    d_lbr = dlam[:, 0, :].reshape(G, P)
    d_lbi = dlam[:, 1, :].reshape(G, P)
    d_lam_re, d_lam_im, d_log_dt, d_b_re, d_b_im = disc_vjp((d_lbr, d_lbi, d_bbr, d_bbi))

    sg = {"mix_norm": d_mix_norm, "b_gate": d_b_gate, "conv_dw": ddw[:KW], "conv_dw_b": d_dw_b,
          "conv_ln_g": d_ln_g, "conv_ln_b": d_ln_b, "ssm_lam_re": d_lam_re, "ssm_lam_im": d_lam_im,
          "ssm_log_dt": d_log_dt, "ssm_b_re": d_b_re, "ssm_b_im": d_b_im, "ssm_c_re": d_c_re, "ssm_c_im": d_c_im,
          "ssm_d": d_ssm_d, "ffn2_norm": d_ffn2_norm, "final_norm": d_final}
    late = ["meta_tokens", "ffn1_norm"]
    early = [n for n in small if n not in late]

    (dh0, da1, db1, s1, n1, d_ffn1_norm), got = _ffn_bwd(
        h0, ffn1_norm, dh1, a1, b1, gw["ffn1_w1"], gw["ffn1_w3"], gw["ffn1_w2"], "ffn1_bwd",
        _join(_scatter_chips_behind(pair_b), _gather_all_behind(_pack([sg[n] for n in early]))))
    recv_b, early_all = got[:len(grp_b)], got[len(grp_b)]
    gbig["ffn1_w1"] = _wgrad(da1, n1, "ffn1_dw1")
    gbig["ffn1_w3"] = _wgrad(db1, n1, "ffn1_dw3")
    gbig["ffn1_w2"] = _wgrad(s1, dh1, "ffn1_dw2", 0.5)
    grad_x = dh0[FRONT:][None]
    sg["meta_tokens"] = dh0[FRONT - NMETA:FRONT]
    sg["ffn1_norm"] = d_ffn1_norm

    pair_a = pair_sums(grp_a, "ffn1")
    late_all = _gather_all(_pack([sg[n] for n in late]), "gather_late_grads")
    sa_send, sa_recv, pair_a, land_s, sa_token = _chips_start(
        pair_a, [jax.ShapeDtypeStruct(p.shape, p.dtype) for p in pair_a], False, "scatter_ffn1_start", [late_all])

    out_g, out_d, out_m, out_v = {}, {}, {}, {}

    def finish(group, recvs, tag, after=None):
        halves = [_sum_slots(r, "sum_" + n, after) for n, r in zip(group, recvs)]
        for n, f in zip(group, _swap_halves(halves, "swap_" + tag)):
            g3 = f.reshape(1, f.shape[0] * f.shape[1], f.shape[2])
            g3, d3, m3, v3 = _adamw(view(args[n], n), g3, view(args["m_" + n], n), view(args["v_" + n], n),
                                "adamw_" + n)
            out_g[n], out_d[n], out_m[n], out_v[n] = (view(t, n) for t in (g3, d3, m3, v3))
            done.append(d3)

    done = []
    finish(grp_b + grp_c, list(recv_b) + list(recv_c), "mix_ffn2", sa_token)

    sgr = dict(zip(early, _unpack(_sum_slots(early_all, "sum_early", sa_token), [sg[n].shape for n in early])))
    sgr.update(zip(late, _unpack(_sum_slots(late_all, "sum_late"), [sg[n].shape for n in late])))
    sgr["meta_tokens"] = lax.dynamic_slice_in_dim(sgr["meta_tokens"], chip * (D // NSH), D // NSH, axis=1)
    sgr["conv_dw"] = lax.dynamic_slice_in_dim(sgr["conv_dw"], chip * (DC // NSH), DC // NSH, axis=1)
    pshapes = [args[n].shape for n in small]
    _, d_s, m_s, v_s = _adamw(_pack([args[n] for n in small])[None], _pack([sgr[n] for n in small])[None],
                           _pack([args["m_" + n] for n in small])[None],
                           _pack([args["v_" + n] for n in small])[None], "adamw_small")
    for n, g_, d_, m_, v_ in zip(small, [sgr[n] for n in small], _unpack(d_s[0], pshapes),
                                 _unpack(m_s[0], pshapes), _unpack(v_s[0], pshapes)):
        out_g[n], out_d[n], out_m[n], out_v[n] = g_.reshape(args[n].shape), d_, m_, v_

    loss = lax.psum(loss_part[0, 0], ("x", "y", "c"))
    pair_a, recv_a = _chips_wait(sa_send, sa_recv, pair_a, land_s, [d_s, grad_x] + done, False,
                                 "scatter_ffn1_wait")
    finish(grp_a, _fill_own(pair_a, recv_a, "own_ffn1"), "ffn1")
    return (loss, grad_x, *[out_g[n] for n in names], *[out_d[n] for n in names],
            *[out_m[n] for n in names], *[out_v[n] for n in names])
```

```python
import math

import jax
import jax.numpy as jnp
from jax import lax
from jax.experimental import pallas as pl
from jax.experimental.pallas import tpu as pltpu

F32 = jnp.float32
BF16 = jnp.bfloat16

D = 1024
NSH = 4
F = 2816
FS = F // NSH
DC = 512
DS = 512
DIN = 2 * DC + DS + 2 * D
WS = DIN // NSH
KW = 31
KWP = 32
CONV_ROWS = 64
NMETA = 16
FRONT = 128
G, P, H = 32, 64, 16
NST = G * P
NQ = 4
QS = NST // NQ
QU = DS // NQ
NSEG = 32
NGRP = NSEG // 8
NCH = 8
SOFF = 8
EPS = 1e-6
LR, B1, B2, AEPS, WD, STEP = 1e-3, 0.9, 0.999, 1e-8, 0.01, 10
VMEM_LIMIT = 58 * 1024 * 1024
MESH = pl.DeviceIdType.MESH
ANY = pl.BlockSpec(memory_space=pl.ANY)


def _params(*sem):
    return pltpu.CompilerParams(dimension_semantics=sem, vmem_limit_bytes=VMEM_LIMIT)


def _res(shape):
    nd = len(shape)
    return pl.BlockSpec(shape, lambda *_: (0,) * nd, pipeline_mode=pl.Buffered(1))


def _tile(n, cap, mult=16):
    best = None
    for t in range(mult, min(n, cap) + 1, mult):
        if n % t == 0:
            best = t
    assert best is not None, (n, cap, mult)
    return best


def _dot(a, b):
    return jnp.dot(a, b, preferred_element_type=F32)


def _dot_nt(a, b):
    return lax.dot_general(a, b, (((1,), (1,)), ((), ())), preferred_element_type=F32)


def _dot_tn(a, b):
    return lax.dot_general(a, b, (((0,), (0,)), ((), ())), preferred_element_type=F32)


def _sigmoid(x):
    return 1.0 / (1.0 + jnp.exp(-x))


_GC = math.sqrt(2.0 / math.pi)
_GA = 0.044715


def _gelu(x):
    return 0.5 * x * (1.0 + jnp.tanh(_GC * (x + _GA * x * x * x)))


def _gelu_grad(x):
    t = jnp.tanh(_GC * (x + _GA * x * x * x))
    return 0.5 * (1.0 + t) + 0.5 * x * (1.0 - t * t) * _GC * (1.0 + 3.0 * _GA * x * x)


def _rms(hv, g):
    r = lax.rsqrt(jnp.mean(hv * hv, axis=-1, keepdims=True) + EPS)
    return hv * r * g, r


def _rms_bwd(dn, hv, r, g):
    xh = hv * r
    dxh = dn * g
    return r * (dxh - xh * jnp.mean(dxh * xh, axis=-1, keepdims=True)), xh


def _acc_rows(ref, part, first):
    @pl.when(first)
    def _():
        ref[...] = part

    @pl.when(jnp.logical_not(first))
    def _():
        ref[...] += part


def _coords():
    return lax.axis_index("x"), lax.axis_index("y"), lax.axis_index("c")


def _flip(v, d):
    return 1 - v if d else v


def _run(local, remote):
    for cp in local + remote:
        cp.start()
    for cp in remote:
        cp.wait()
    for cp in local:
        cp.wait()


def _via_vmem(src, dst, stage, sems, i):
    return (pltpu.make_async_copy(src, stage, sems.at[2 * i]), pltpu.make_async_copy(stage, dst, sems.at[2 * i + 1]))


def _run_staged(staged, remote):
    for load, _ in staged:
        load.start()
    for cp in remote:
        cp.start()
    for load, store in staged:
        load.wait()
        store.start()
    for cp in remote:
        cp.wait()
    for _, store in staged:
        store.wait()


_REL3 = ((1, 0), (0, 1), (1, 1))


class _Behind:
    def __init__(self, arrays, out_shapes, scratch, build, alias_pairs=()):
        self.arrays, self.out_shapes, self.scratch, self.build = list(arrays), list(out_shapes), list(scratch), build
        self.alias_pairs = list(alias_pairs)

    def aliases(self):
        return self.alias_pairs

    def start(self, ins, outs, scr):
        staged, remote = self.build(ins, outs, scr)
        for load, _ in staged:
            load.start()
        for cp in remote:
            cp.start()

    def finish(self, ins, outs, scr):
        staged, remote = self.build(ins, outs, scr)
        for load, store in staged:
            load.wait()
            store.start()
        for cp in remote:
            cp.wait()
        for _, store in staged:
            store.wait()


def _call(body, comm, *, name, grid, in_specs, out_specs, out_shape, scratch_shapes=(), params):
    in_specs, out_specs, out_shape = list(in_specs), list(out_specs), list(out_shape)
    scratch_shapes = list(scratch_shapes)
    if comm is None:
        f = pl.pallas_call(body, name=name, grid=grid, in_specs=in_specs, out_specs=out_specs,
                           out_shape=out_shape, scratch_shapes=scratch_shapes, compiler_params=params)
        return lambda *args: (f(*args), [])
    ni, no, ns = len(in_specs), len(out_specs), len(scratch_shapes)
    ci, co = len(comm.arrays), len(comm.out_shapes)

    def hosted(*refs):
        ins, cin = refs[:ni], refs[ni:ni + ci]
        outs, cout = refs[ni + ci:ni + ci + no], refs[ni + ci + no:ni + ci + no + co]
        scr, cscr = refs[ni + ci + no + co:ni + ci + no + co + ns], refs[ni + ci + no + co + ns:]
        first = last = None
        for axis, size in enumerate(grid):
            i = pl.program_id(axis)
            first = (i == 0) if first is None else jnp.logical_and(first, i == 0)
            last = (i == size - 1) if last is None else jnp.logical_and(last, i == size - 1)

        @pl.when(first)
        def _():
            comm.start(cin, cout, cscr)

        body(*ins, *outs, *scr)

        @pl.when(last)
        def _():
            comm.finish(cin, cout, cscr)

    f = pl.pallas_call(hosted, name=name, grid=grid, in_specs=in_specs + [ANY] * ci,
                       out_specs=out_specs + [ANY] * co, out_shape=out_shape + comm.out_shapes,
                       scratch_shapes=scratch_shapes + comm.scratch,
                       input_output_aliases={ni + a: no + b for a, b in comm.aliases()},
                       compiler_params=_params(*(("arbitrary",) * len(grid))))

    def run(*args):
        res = f(*args, *comm.arrays)
        return res[:no], res[no:]

    return run


def _gather_half_behind(shards):
    n = len(shards)

    def build(ins, outs, scr):
        send, recv, loc = scr[:3]
        stage = scr[3:]
        x, y, c = _coords()
        me = 2 * x + y
        staged = [_via_vmem(ins[t], outs[t].at[me], stage[t], loc, t) for t in range(n)]
        remote = []
        for t in range(n):
            half = shards[t].shape[0] // 2
            mine = pl.ds(c * half, half)
            for k, (dx, dy) in enumerate(_REL3):
                remote.append(pltpu.make_async_remote_copy(
                    src_ref=ins[t].at[mine], dst_ref=outs[t].at[me, mine],
                    send_sem=send.at[3 * t + k], recv_sem=recv.at[3 * t + k],
                    device_id=(_flip(x, dx), _flip(y, dy), c), device_id_type=MESH))
        return staged, remote

    return _Behind(shards, [jax.ShapeDtypeStruct((NSH,) + s.shape, s.dtype) for s in shards],
                   [pltpu.SemaphoreType.DMA((3 * n,)), pltpu.SemaphoreType.DMA((3 * n,)),
                    pltpu.SemaphoreType.DMA((2 * n,))] + [pltpu.VMEM(s.shape, s.dtype) for s in shards], build)


def _pass_halves(gathered, name, own=()):
    n, m = len(gathered), len(own)

    def body(*refs):
        shards, outs = refs[n:n + m], refs[n + m:2 * n + m]
        send, recv, loc = refs[2 * n + m:2 * n + m + 3]
        stage = refs[2 * n + m + 3:]
        x, y, c = _coords()
        staged = [_via_vmem(shards[t], outs[t].at[2 * x + y], stage[t], loc, t) for t in range(m)]
        remote = []
        for t in range(n):
            half = gathered[t].shape[1] // 2
            mine = pl.ds(c * half, half)
            for k, (dx, dy) in enumerate(_REL3):
                slot = 2 * _flip(x, dx) + _flip(y, dy)
                remote.append(pltpu.make_async_remote_copy(
                    src_ref=outs[t].at[slot, mine], dst_ref=outs[t].at[slot, mine],
                    send_sem=send.at[3 * t + k], recv_sem=recv.at[3 * t + k],
                    device_id=(x, y, 1 - c), device_id_type=MESH))
        _run_staged(staged, remote)

    return pl.pallas_call(
        body, name=name,
        out_shape=[jax.ShapeDtypeStruct(g.shape, g.dtype) for g in gathered],
        in_specs=[ANY] * (n + m), out_specs=[ANY] * n, input_output_aliases={t: t for t in range(n)},
        scratch_shapes=[pltpu.SemaphoreType.DMA((3 * n,)), pltpu.SemaphoreType.DMA((3 * n,)),
                        pltpu.SemaphoreType.DMA((max(2 * m, 1),))] + [pltpu.VMEM(s.shape, s.dtype) for s in own],
        compiler_params=pltpu.CompilerParams(vmem_limit_bytes=VMEM_LIMIT),
    )(*gathered, *own)


def _fill_own(sums, recvs, name):
    n = len(sums)

    def body(*refs):
        ins, outs = refs[:n], refs[2 * n:3 * n]
        loc = refs[3 * n]
        stage = refs[3 * n + 1:]
        x, y, _ = _coords()
        me = 2 * x + y
        _run_staged([_via_vmem(ins[t].at[me], outs[t].at[me], stage[t], loc, t) for t in range(n)], [])

    return pl.pallas_call(
        body, name=name,
        out_shape=[jax.ShapeDtypeStruct(r.shape, r.dtype) for r in recvs],
        in_specs=[ANY] * (2 * n), out_specs=[ANY] * n, input_output_aliases={n + t: t for t in range(n)},
        scratch_shapes=[pltpu.SemaphoreType.DMA((2 * n,))] + [pltpu.VMEM(s.shape[1:], s.dtype) for s in sums],
        compiler_params=pltpu.CompilerParams(vmem_limit_bytes=VMEM_LIMIT),
    )(*sums, *recvs)


def _scatter_chips_behind(sums):
    n = len(sums)

    def build(ins, outs, scr):
        send, recv, loc = scr[:3]
        stage = scr[3:]
        x, y, c = _coords()
        me = 2 * x + y
        staged = [_via_vmem(ins[t].at[me], outs[t].at[me], stage[t], loc, t) for t in range(n)]
        remote = []
        for t in range(n):
            for k, (dx, dy) in enumerate(_REL3):
                px, py = _flip(x, dx), _flip(y, dy)
                remote.append(pltpu.make_async_remote_copy(
                    src_ref=ins[t].at[2 * px + py], dst_ref=outs[t].at[me],
                    send_sem=send.at[3 * t + k], recv_sem=recv.at[3 * t + k],
                    device_id=(px, py, c), device_id_type=MESH))
        return staged, remote

    return _Behind(sums, [jax.ShapeDtypeStruct(s.shape, s.dtype) for s in sums],
                   [pltpu.SemaphoreType.DMA((3 * n,)), pltpu.SemaphoreType.DMA((3 * n,)),
                    pltpu.SemaphoreType.DMA((2 * n,))] + [pltpu.VMEM(s.shape[1:], s.dtype) for s in sums], build)


def _gather_all_behind(a):
    def build(ins, outs, scr):
        send, recv, loc, stage = scr
        x, y, c = _coords()
        me = 4 * x + 2 * y + c
        staged = [_via_vmem(ins[0], outs[0].at[me], stage, loc, 0)]
        remote = [pltpu.make_async_remote_copy(
            src_ref=ins[0], dst_ref=outs[0].at[me], send_sem=send.at[k], recv_sem=recv.at[k],
            device_id=(_flip(x, dx), _flip(y, dy), _flip(c, dc)), device_id_type=MESH)
            for k, (dx, dy, dc) in enumerate(_REL7)]
        return staged, remote

    return _Behind([a], [jax.ShapeDtypeStruct((8,) + a.shape, a.dtype)],
                   [pltpu.SemaphoreType.DMA((7,)), pltpu.SemaphoreType.DMA((7,)), pltpu.SemaphoreType.DMA((2,)),
                    pltpu.VMEM(a.shape, a.dtype)], build)


HBM = pl.BlockSpec(memory_space=pltpu.HBM)
SEM = pl.BlockSpec(memory_space=pltpu.SEMAPHORE)
EFFECT = pltpu.SideEffectType.DATAFLOW_SIDE_EFFECTING


def _chip_copies(srcs, lands, send, recv, gather):
    x, y, c = _coords()
    me = 2 * x + y
    cps = []
    for t in range(len(srcs)):
        for k, (dx, dy) in enumerate(_REL3):
            px, py = _flip(x, dx), _flip(y, dy)
            if gather:
                half = srcs[t].shape[0] // 2
                mine = pl.ds(c * half, half)
                src, dst = srcs[t].at[mine], lands[t].at[me, mine]
            else:
                src, dst = srcs[t].at[2 * px + py], lands[t].at[me]
            cps.append(pltpu.make_async_remote_copy(
                src_ref=src, dst_ref=dst, send_sem=send.at[3 * t + k], recv_sem=recv.at[3 * t + k],
                device_id=(px, py, c), device_id_type=MESH))
    return cps


def _chips_start(arrays, land_shapes, gather, name, after=()):
    n = len(arrays)

    def body(*refs):
        srcs, lands = refs[:n], refs[n:2 * n]
        send, recv = refs[2 * n + len(after)], refs[2 * n + len(after) + 1]
        token = refs[-1]
        for cp in _chip_copies(srcs, lands, send, recv, gather):
            cp.start()
        token[...] = jnp.zeros_like(token)

    lands = [lax.empty(s.shape, s.dtype) for s in land_shapes]
    thru = [pltpu.HBM(a.shape, a.dtype) for a in arrays] + [pltpu.HBM(s.shape, s.dtype) for s in land_shapes]
    res = pl.pallas_call(
        body, name=name,
        out_shape=(pltpu.SemaphoreType.DMA((3 * n,)), pltpu.SemaphoreType.DMA((3 * n,)), *thru,
                   jax.ShapeDtypeStruct((8, 128), F32)),
        in_specs=[HBM] * (2 * n) + [ANY] * len(after),
        out_specs=(SEM, SEM, *([HBM] * (2 * n)), pl.BlockSpec(memory_space=pltpu.VMEM)),
        input_output_aliases={t: 2 + t for t in range(2 * n)},
        compiler_params=pltpu.CompilerParams(has_side_effects=EFFECT),
    )(*[pltpu.with_memory_space_constraint(a, pltpu.HBM) for a in arrays],
      *[pltpu.with_memory_space_constraint(z, pltpu.HBM) for z in lands], *after)
    return res[0], res[1], list(res[2:2 + n]), list(res[2 + n:2 + 2 * n]), res[-1]


def _chips_wait(send, recv, arrays, lands, after, gather, name):
    n = len(arrays)

    def body(*refs):
        srcs, ls = refs[:n], refs[n:2 * n]
        sd, rv = refs[2 * n], refs[2 * n + 1]
        for cp in _chip_copies(srcs, ls, sd, rv, gather):
            cp.wait_send()
            cp.wait_recv()

    res = pl.pallas_call(
        body, name=name,
        out_shape=[pltpu.HBM(a.shape, a.dtype) for a in arrays] + [pltpu.HBM(z.shape, z.dtype) for z in lands],
        in_specs=[HBM] * (2 * n) + [SEM, SEM] + [ANY] * len(after), out_specs=[HBM] * (2 * n),
        input_output_aliases={t: t for t in range(2 * n)},
        compiler_params=pltpu.CompilerParams(has_side_effects=EFFECT),
    )(*arrays, *lands, send, recv, *after)
    return list(res[:n]), list(res[n:])


def _join(*parts):
    def cut(seq, key):
        res, o = [], 0
        for p in parts:
            k = len(getattr(p, key))
            res.append(seq[o:o + k])
            o += k
        return res

    def build(ins, outs, scr):
        staged, remote = [], []
        for p, i, o, s in zip(parts, cut(ins, "arrays"), cut(outs, "out_shapes"), cut(scr, "scratch")):
            st, rm = p.build(i, o, s)
            staged += st
            remote += rm
        return staged, remote

    pairs, ai, oi = [], 0, 0
    for p in parts:
        pairs += [(ai + a, oi + b) for a, b in p.alias_pairs]
        ai, oi = ai + len(p.arrays), oi + len(p.out_shapes)
    return _Behind(sum((p.arrays for p in parts), []), sum((p.out_shapes for p in parts), []),
                   sum((p.scratch for p in parts), []), build, pairs)


def _pass_halves_behind(gathered):
    n = len(gathered)

    def build(ins, outs, scr):
        send, recv = scr
        x, y, c = _coords()
        remote = []
        for t in range(n):
            half = gathered[t].shape[1] // 2
            mine = pl.ds(c * half, half)
            for k, (dx, dy) in enumerate(_REL3):
                slot = 2 * _flip(x, dx) + _flip(y, dy)
                remote.append(pltpu.make_async_remote_copy(
                    src_ref=outs[t].at[slot, mine], dst_ref=outs[t].at[slot, mine],
                    send_sem=send.at[3 * t + k], recv_sem=recv.at[3 * t + k],
                    device_id=(x, y, 1 - c), device_id_type=MESH))
        return [], remote

    return _Behind(gathered, [jax.ShapeDtypeStruct(g.shape, g.dtype) for g in gathered],
                   [pltpu.SemaphoreType.DMA((3 * n,)), pltpu.SemaphoreType.DMA((3 * n,))], build,
                   [(t, t) for t in range(n)])


_REL7 = tuple((dx, dy, dc) for dx in (0, 1) for dy in (0, 1) for dc in (0, 1))[1:]


def _gather_all(a, name):
    def body(a_ref, o_ref, send, recv, loc):
        x, y, c = _coords()
        me = 4 * x + 2 * y + c
        local = [pltpu.make_async_copy(a_ref, o_ref.at[me], loc.at[0])]
        remote = [pltpu.make_async_remote_copy(
            src_ref=a_ref, dst_ref=o_ref.at[me], send_sem=send.at[k], recv_sem=recv.at[k],
            device_id=(_flip(x, dx), _flip(y, dy), _flip(c, dc)), device_id_type=MESH)
            for k, (dx, dy, dc) in enumerate(_REL7)]
        _run(local, remote)

    return pl.pallas_call(
        body, name=name,
        out_shape=jax.ShapeDtypeStruct((8,) + a.shape, a.dtype),
        in_specs=[ANY], out_specs=ANY,
        scratch_shapes=[pltpu.SemaphoreType.DMA((7,)), pltpu.SemaphoreType.DMA((7,)),
                        pltpu.SemaphoreType.DMA((1,))],
    )(a)


def _pair_exchange(grads, name):
    n = len(grads)

    def body(*refs):
        ins, outs = refs[:n], refs[n:2 * n]
        send, recv = refs[2 * n:]
        x, y, c = _coords()
        remote = []
        for t in range(n):
            half = grads[t].shape[1] // 2
            remote.append(pltpu.make_async_remote_copy(
                src_ref=ins[t].at[:, pl.ds((1 - c) * half, half)], dst_ref=outs[t],
                send_sem=send.at[t], recv_sem=recv.at[t],
                device_id=(x, y, 1 - c), device_id_type=MESH))
        _run([], remote)

    return pl.pallas_call(
        body, name=name,
        out_shape=[jax.ShapeDtypeStruct((NSH, g.shape[1] // 2, g.shape[2]), g.dtype) for g in grads],
        in_specs=[ANY] * n, out_specs=[ANY] * n,
        scratch_shapes=[pltpu.SemaphoreType.DMA((n,)), pltpu.SemaphoreType.DMA((n,))],
    )(*grads)


def _swap_halves(halves, name):
    n = len(halves)

    def body(*refs):
        ins, outs = refs[:n], refs[n:2 * n]
        send, recv, loc = refs[2 * n:2 * n + 3]
        stage = refs[2 * n + 3:]
        x, y, c = _coords()
        local = [_via_vmem(ins[t], outs[t].at[c], stage[t], loc, t) for t in range(n)]
        remote = [pltpu.make_async_remote_copy(
            src_ref=ins[t], dst_ref=outs[t].at[c], send_sem=send.at[t], recv_sem=recv.at[t],
            device_id=(x, y, 1 - c), device_id_type=MESH) for t in range(n)]
        _run_staged(local, remote)

    return pl.pallas_call(
        body, name=name,
        out_shape=[jax.ShapeDtypeStruct((2,) + h.shape, h.dtype) for h in halves],
        in_specs=[ANY] * n, out_specs=[ANY] * n,
        scratch_shapes=[pltpu.SemaphoreType.DMA((n,)), pltpu.SemaphoreType.DMA((n,)),
                        pltpu.SemaphoreType.DMA((2 * n,))]
        + [pltpu.VMEM(h.shape, h.dtype) for h in halves],
        compiler_params=pltpu.CompilerParams(vmem_limit_bytes=VMEM_LIMIT),
    )(*halves)


def _sum_slots(r, name, after=None):
    K, R, C = r.shape
    tr = _tile(R, max(16, (1 << 22) // (K * C)), 8 * (4 // r.dtype.itemsize))

    def body(r_ref, *rest):
        o_ref = rest[-1]
        acc = r_ref[0].astype(F32)
        for k in range(1, K):
            acc = acc + r_ref[k].astype(F32)
        o_ref[...] = acc

    dep = [] if after is None else [after]
    return pl.pallas_call(
        body, name=name, grid=(R // tr,),
        out_shape=jax.ShapeDtypeStruct((R, C), F32),
        in_specs=[pl.BlockSpec((K, tr, C), lambda i: (0, i, 0))] + [ANY] * len(dep),
        out_specs=pl.BlockSpec((tr, C), lambda i: (i, 0)),
        compiler_params=_params("parallel"),
    )(r, *dep)


def _add_pair(g, s, core, name):
    _, half, C = s.shape
    tr = _tile(half, max(16, (1 << 19) // C))
    nb = half // tr

    def body(c_ref, g_ref, s_ref, o_ref):
        o_ref[...] = (g_ref[...].astype(F32) + s_ref[...].astype(F32)).astype(BF16)

    spec = pl.BlockSpec((1, tr, C), lambda j, i, c_ref: (j, i, 0))
    return pl.pallas_call(
        body, name=name,
        grid_spec=pltpu.PrefetchScalarGridSpec(
            num_scalar_prefetch=1, grid=(NSH, nb),
            in_specs=[pl.BlockSpec((1, tr, C), lambda j, i, c_ref: (j, c_ref[0] * nb + i, 0)), spec],
            out_specs=spec),
        out_shape=jax.ShapeDtypeStruct(s.shape, BF16),
        compiler_params=_params("parallel", "parallel"),
    )(core, g, s)


def _adamw(w, g, m, v, name):
    _, R, C = w.shape
    tr = _tile(R, max(8, (1 << 18) // C), 8)
    c1 = 1.0 / (1.0 - B1 ** STEP)
    c2 = 1.0 / (1.0 - B2 ** STEP)

    def body(w_ref, g_ref, m_ref, v_ref, go_ref, d_ref, nm_ref, nv_ref):
        gv = g_ref[...]
        go_ref[...] = gv
        nm = B1 * m_ref[...] + (1.0 - B1) * gv
        nv = B2 * v_ref[...] + (1.0 - B2) * gv * gv
        nm_ref[...] = nm
        nv_ref[...] = nv
        d_ref[...] = -LR * ((nm * c1) / (jnp.sqrt(nv * c2) + AEPS) + WD * w_ref[...])

    spec = pl.BlockSpec((1, tr, C), lambda i: (0, i, 0))
    return pl.pallas_call(
        body, name=name, grid=(R // tr,),
        out_shape=[jax.ShapeDtypeStruct((1, R, C), F32)] * 4,
        in_specs=[spec] * 4, out_specs=[spec] * 4,
        compiler_params=_params("parallel"),
    )(w, g, m, v)


def _ffn_fwd(h, g, w1, w3, w2, name, comm=None):
    L = h.shape[0]
    tm = _tile(L, 704)

    def body(h_ref, g_ref, w1_ref, w3_ref, w2_ref, o_ref, a_ref, b_ref, n_s, acc_s):
        j = pl.program_id(1)

        @pl.when(j == 0)
        def _():
            hv = h_ref[...]
            n, _ = _rms(hv, g_ref[...])
            n_s[...] = n.astype(BF16)
            acc_s[...] = hv

        n = n_s[...]
        a = _dot_nt(n, w1_ref[0])
        b = _dot_nt(n, w3_ref[0])
        a_ref[0] = a.astype(BF16)
        b_ref[0] = b.astype(BF16)
        s = (a * _sigmoid(a) * b).astype(BF16)
        acc_s[...] += 0.5 * _dot(s, w2_ref[0])

        @pl.when(j == NSH - 1)
        def _():
            o_ref[...] = acc_s[...]

    row = pl.BlockSpec((tm, D), lambda i, j: (i, 0))
    hid = pl.BlockSpec((1, tm, FS), lambda i, j: (j, i, 0))
    wsp = pl.BlockSpec((1, FS, D), lambda i, j: (j, 0, 0))
    return _call(
        body, comm, name=name, grid=(L // tm, NSH),
        out_shape=[jax.ShapeDtypeStruct((L, D), F32),
                   jax.ShapeDtypeStruct((NSH, L, FS), BF16), jax.ShapeDtypeStruct((NSH, L, FS), BF16)],
        in_specs=[row, _res((1, D)), wsp, wsp, wsp],
        out_specs=[row, hid, hid],
        scratch_shapes=[pltpu.VMEM((tm, D), BF16), pltpu.VMEM((tm, D), F32)],
        params=_params("arbitrary", "arbitrary"),
    )(h, g, w1, w3, w2)


def _loss_head(hv, gv, tv, row0):
    y, r = _rms(hv, gv)
    row = row0 + lax.broadcasted_iota(jnp.int32, (hv.shape[0], 1), 0)
    e = jnp.where(row >= FRONT, y - tv, 0.0)
    dy = e * (1.0 / D)
    part = 0.5 * jnp.sum(jnp.sum(e * dy, axis=1, keepdims=True), axis=0, keepdims=True)
    dx, xh = _rms_bwd(dy, hv, r, gv)
    return dx, part, jnp.sum(dy * xh, axis=0, keepdims=True)


def _ffn_fwd_loss(h, g, w1, w3, w2, gf, tgt, name):
    L = h.shape[0]
    tm = _tile(L, 704)

    def body(h_ref, g_ref, w1_ref, w3_ref, w2_ref, gf_ref, t_ref, o_ref, a_ref, b_ref, loss_ref, dgf_ref,
             n_s, acc_s):
        i, j = pl.program_id(0), pl.program_id(1)

        @pl.when(j == 0)
        def _():
            hv = h_ref[...]
            n, _ = _rms(hv, g_ref[...])
            n_s[...] = n.astype(BF16)
            acc_s[...] = hv

        n = n_s[...]
        a = _dot_nt(n, w1_ref[0])
        b = _dot_nt(n, w3_ref[0])
        a_ref[0] = a.astype(BF16)
        b_ref[0] = b.astype(BF16)
        s = (a * _sigmoid(a) * b).astype(BF16)
        acc_s[...] += 0.5 * _dot(s, w2_ref[0])

        @pl.when(j == NSH - 1)
        def _():
            dx, part, dgf = _loss_head(acc_s[...], gf_ref[...], t_ref[...], i * tm)
            o_ref[...] = dx
            _acc_rows(loss_ref, part, i == 0)
            _acc_rows(dgf_ref, dgf, i == 0)

    row = pl.BlockSpec((tm, D), lambda i, j: (i, 0))
    hid = pl.BlockSpec((1, tm, FS), lambda i, j: (j, i, 0))
    wsp = pl.BlockSpec((1, FS, D), lambda i, j: (j, 0, 0))
    return pl.pallas_call(
        body, name=name, grid=(L // tm, NSH),
        out_shape=[jax.ShapeDtypeStruct((L, D), F32),
                   jax.ShapeDtypeStruct((NSH, L, FS), BF16), jax.ShapeDtypeStruct((NSH, L, FS), BF16),
                   jax.ShapeDtypeStruct((1, 1), F32), jax.ShapeDtypeStruct((1, D), F32)],
        in_specs=[row, _res((1, D)), wsp, wsp, wsp, _res((1, D)), row],
        out_specs=[row, hid, hid, pl.BlockSpec((1, 1), lambda i, j: (0, 0)),
                   pl.BlockSpec((1, D), lambda i, j: (0, 0))],
        scratch_shapes=[pltpu.VMEM((tm, D), BF16), pltpu.VMEM((tm, D), F32)],
        compiler_params=_params("arbitrary", "arbitrary"),
    )(h, g, w1, w3, w2, gf, tgt)


def _ffn_bwd(h, g, dout, a, b, w1, w3, w2, name, comm=None):
    L = h.shape[0]
    tm = _tile(L, 528)

    def body(h_ref, g_ref, do_ref, a_ref, b_ref, w1_ref, w3_ref, w2_ref,
             dh_ref, da_ref, db_ref, s_ref, n_ref, dg_ref, dob_s, dn_s):
        i, j = pl.program_id(0), pl.program_id(1)

        @pl.when(j == 0)
        def _():
            n, _ = _rms(h_ref[...], g_ref[...])
            n_ref[...] = n.astype(BF16)
            dob_s[...] = (0.5 * do_ref[...]).astype(BF16)
            dn_s[...] = jnp.zeros_like(dn_s)

        av = a_ref[0].astype(F32)
        bv = b_ref[0].astype(F32)
        sig = _sigmoid(av)
        sa = av * sig
        ds = _dot_nt(dob_s[...], w2_ref[0])
        s_ref[0] = (sa * bv).astype(BF16)
        da = (ds * bv * (sig + sa * (1.0 - sig))).astype(BF16)
        db = (ds * sa).astype(BF16)
        da_ref[0] = da
        db_ref[0] = db
        dn_s[...] += _dot(da, w1_ref[0]) + _dot(db, w3_ref[0])

        @pl.when(j == NSH - 1)
        def _():
            hv = h_ref[...]
            gv = g_ref[...]
            r = lax.rsqrt(jnp.mean(hv * hv, axis=-1, keepdims=True) + EPS)
            dn = dn_s[...]
            dx, xh = _rms_bwd(dn, hv, r, gv)
            dh_ref[...] = do_ref[...] + dx
            _acc_rows(dg_ref, jnp.sum(dn * xh, axis=0, keepdims=True), i == 0)

    row = pl.BlockSpec((tm, D), lambda i, j: (i, 0))
    hid = pl.BlockSpec((1, tm, FS), lambda i, j: (j, i, 0))
    wsp = pl.BlockSpec((1, FS, D), lambda i, j: (j, 0, 0))
    return _call(
        body, comm, name=name, grid=(L // tm, NSH),
        out_shape=[jax.ShapeDtypeStruct((L, D), F32)]
        + [jax.ShapeDtypeStruct((NSH, L, FS), BF16)] * 3
        + [jax.ShapeDtypeStruct((L, D), BF16), jax.ShapeDtypeStruct((1, D), F32)],
        in_specs=[row, _res((1, D)), row, hid, hid,
                  wsp, wsp, wsp],
        out_specs=[row, hid, hid, hid, row, pl.BlockSpec((1, D), lambda i, j: (0, 0))],
        scratch_shapes=[pltpu.VMEM((tm, D), BF16), pltpu.VMEM((tm, D), F32)],
        params=_params("arbitrary", "arbitrary"),
    )(h, g, dout, a, b, w1, w3, w2)


def _wgrad(xm, ym, name, scale=1.0):
    xs, ys = xm.ndim == 3, ym.ndim == 3
    assert not (xs and ys)
    L = xm.shape[-2]
    K, N = xm.shape[-1], ym.shape[-1]
    tl = _tile(L, 2112)
    nl = L // tl
    if xs or ys:
        tn, grid_n = N, NSH
    else:
        tn = _tile(N, 1024, 128)
        grid_n = N // tn

    def body(x_ref, y_ref, o_ref, acc_s):
        l = pl.program_id(1)
        xv = x_ref[0] if xs else x_ref[...]
        yv = y_ref[0] if ys else y_ref[...]
        part = _dot_tn(xv.astype(BF16), yv.astype(BF16))
        _acc_rows(acc_s, part, l == 0)

        @pl.when(l == nl - 1)
        def _():
            res = (acc_s[...] * scale).astype(BF16)
            if xs or ys:
                o_ref[0] = res
            else:
                o_ref[...] = res

    if xs:
        x_spec = pl.BlockSpec((1, tl, K), lambda n, l: (n, l, 0))
        y_spec = pl.BlockSpec((tl, N), lambda n, l: (l, 0))
        o_spec = pl.BlockSpec((1, K, N), lambda n, l: (n, 0, 0))
        o_shape = (NSH, K, N)
    elif ys:
        x_spec = pl.BlockSpec((tl, K), lambda n, l: (l, 0))
        y_spec = pl.BlockSpec((1, tl, N), lambda n, l: (n, l, 0))
        o_spec = pl.BlockSpec((1, K, N), lambda n, l: (n, 0, 0))
        o_shape = (NSH, K, N)
    else:
        x_spec = pl.BlockSpec((tl, K), lambda n, l: (l, 0))
        y_spec = pl.BlockSpec((tl, tn), lambda n, l: (l, n))
        o_spec = pl.BlockSpec((K, tn), lambda n, l: (0, n))
        o_shape = (K, N)
    return pl.pallas_call(
        body, name=name, grid=(grid_n, nl),
        out_shape=jax.ShapeDtypeStruct(o_shape, BF16),
        in_specs=[x_spec, y_spec], out_specs=o_spec,
        scratch_shapes=[pltpu.VMEM((K, tn), F32)],
        compiler_params=_params("parallel", "arbitrary"),
    )(xm, ym)


def _mix_in_fwd(h, g, w_in, b_gate, name, comm=None):
    L = h.shape[0]
    tm = _tile(L, 528)

    def body(h_ref, g_ref, w_ref, bg_ref, vg_ref, uf_ref, gt_ref):
        u, _ = _rms(h_ref[...], g_ref[...])
        ub = u.astype(BF16)
        p = [_dot(ub, w_ref[j]) for j in range(NSH)]
        a0, a1 = 2 * DC - WS, 2 * DC + DS - WS
        vg_ref[:, 0:WS] = p[0].astype(BF16)
        vg_ref[:, WS:2 * DC] = p[1][:, 0:a0].astype(BF16)
        uf_ref[...] = p[1][:, a0:a1].astype(BF16)
        gin = jnp.concatenate([p[1][:, a1:], p[2], p[3]], axis=1)
        gt_ref[...] = _sigmoid(gin + bg_ref[...]).astype(BF16)

    def row(n):
        return pl.BlockSpec((tm, n), lambda i: (i, 0))

    return _call(
        body, comm, name=name, grid=(L // tm,),
        out_shape=[jax.ShapeDtypeStruct((L, 2 * DC), BF16), jax.ShapeDtypeStruct((L, DS), BF16),
                   jax.ShapeDtypeStruct((L, 2 * D), BF16)],
        in_specs=[row(D), _res((1, D)), _res((NSH, D, WS)), _res((1, 2 * D))],
        out_specs=[row(2 * DC), row(DS), row(2 * D)],
        params=_params("parallel"),
    )(h, g, w_in, b_gate)


def _mix_in_bwd(h, g, dres, dv, dgl, duf, dgate, w_in, name):
    L = h.shape[0]
    tm = _tile(L, 528)

    def body(h_ref, g_ref, dr_ref, dv_ref, dgl_ref, duf_ref, dgt_ref, w_ref, dh_ref, u_ref, dp_ref, dgm_ref):
        i = pl.program_id(0)
        hv = h_ref[...]
        gv = g_ref[...]
        u, r = _rms(hv, gv)
        u_ref[...] = u.astype(BF16)
        a0, a1 = 2 * DC - WS, 2 * DC + DS - WS
        b0 = WS - a1
        dp = [jnp.concatenate([dv_ref[...], dgl_ref[:, 0:WS - DC]], axis=1),
              jnp.concatenate([dgl_ref[:, WS - DC:], duf_ref[...], dgt_ref[:, 0:b0]], axis=1),
              dgt_ref[:, b0:b0 + WS], dgt_ref[:, b0 + WS:]]
        du = jnp.zeros((tm, D), F32)
        for j in range(NSH):
            dp_ref[j] = dp[j]
            du = du + _dot_nt(dp[j], w_ref[j])
        dx, xh = _rms_bwd(du, hv, r, gv)
        dh_ref[...] = dr_ref[...] + dx
        _acc_rows(dgm_ref, jnp.sum(du * xh, axis=0, keepdims=True), i == 0)

    def row(n):
        return pl.BlockSpec((tm, n), lambda i: (i, 0))

    return pl.pallas_call(
        body, name=name, grid=(L // tm,),
        out_shape=[jax.ShapeDtypeStruct((L, D), F32), jax.ShapeDtypeStruct((L, D), BF16),
                   jax.ShapeDtypeStruct((NSH, L, WS), BF16), jax.ShapeDtypeStruct((1, D), F32)],
        in_specs=[row(D), _res((1, D)), row(D), row(DC), row(DC), row(DS), row(2 * D), _res((NSH, D, WS))],
        out_specs=[row(D), row(D), pl.BlockSpec((NSH, tm, WS), lambda i: (0, i, 0)),
                   pl.BlockSpec((1, D), lambda i: (0, 0))],
        compiler_params=_params("arbitrary"),
    )(h, g, dres, dv, dgl, duf, dgate, w_in)


def _conv_fwd(vg, dw, dwb, name, comm=None):
    L = vg.shape[0]
    nc = DC // 128

    def body(v_ref, g_ref, dw_ref, dwb_ref, z_ref, zp_s):
        zp_s[0:KWP, :] = jnp.zeros((KWP, 128), F32)
        zp_s[KWP:, :] = v_ref[...].astype(F32) * _sigmoid(g_ref[...].astype(F32))
        for r0 in range(0, L, CONV_ROWS):
            acc = jnp.broadcast_to(dwb_ref[...], (CONV_ROWS, 128))
            for k in range(KW):
                acc = acc + dw_ref[k:k + 1, :] * zp_s[pl.ds(r0 + k + 2, CONV_ROWS), :]
            z_ref[pl.ds(r0, CONV_ROWS), :] = acc

    return _call(
        body, comm, name=name, grid=(nc,),
        out_shape=[jax.ShapeDtypeStruct((L, DC), F32)],
        in_specs=[pl.BlockSpec((L, 128), lambda c: (0, c)), pl.BlockSpec((L, 128), lambda c: (0, nc + c)),
                  pl.BlockSpec((KWP, 128), lambda c: (0, c)), pl.BlockSpec((1, 128), lambda c: (0, c))],
        out_specs=[pl.BlockSpec((L, 128), lambda c: (0, c))],
        scratch_shapes=[pltpu.VMEM((L + KWP, 128), F32)],
        params=_params("parallel"),
    )(vg, vg, dw, dwb)


def _conv_bwd(dz1, vg, dw, name):
    L = vg.shape[0]
    nc = DC // 128

    def body(dz_ref, v_ref, g_ref, dw_ref, dv_ref, dg_ref, ddw_ref, ddwb_ref, zp_s, dzp_s):
        vv = v_ref[...].astype(F32)
        sg = _sigmoid(g_ref[...].astype(F32))
        zp_s[0:KWP, :] = jnp.zeros((KWP, 128), F32)
        zp_s[KWP:, :] = vv * sg
        dz = dz_ref[...]
        dzp_s[0:L, :] = dz
        dzp_s[L:, :] = jnp.zeros((KWP, 128), F32)
        ddwb_ref[...] = jnp.sum(dz, axis=0, keepdims=True)
        part = [jnp.zeros((8, 128), F32) for _ in range(KW)]
        for r0 in range(0, L, CONV_ROWS):
            rows = pl.ds(r0, CONV_ROWS)
            dzc = dz_ref[rows, :]
            acc = jnp.zeros((CONV_ROWS, 128), F32)
            for k in range(KW):
                acc = acc + dw_ref[k:k + 1, :] * dzp_s[pl.ds(r0 + KW - 1 - k, CONV_ROWS), :]
                prod = dzc * zp_s[pl.ds(r0 + k + 2, CONV_ROWS), :]
                for q in range(CONV_ROWS // 8):
                    part[k] = part[k] + prod[8 * q:8 * (q + 1), :]
            vc = v_ref[rows, :].astype(F32)
            sc = _sigmoid(g_ref[rows, :].astype(F32))
            dv_ref[rows, :] = (acc * sc).astype(BF16)
            dg_ref[rows, :] = (acc * vc * sc * (1.0 - sc)).astype(BF16)
        for k in range(KW):
            ddw_ref[k:k + 1, :] = jnp.sum(part[k], axis=0, keepdims=True)
        ddw_ref[KW:KWP, :] = jnp.zeros((KWP - KW, 128), F32)

    col = pl.BlockSpec((L, 128), lambda c: (0, c))
    return pl.pallas_call(
        body, name=name, grid=(nc,),
        out_shape=[jax.ShapeDtypeStruct((L, DC), BF16), jax.ShapeDtypeStruct((L, DC), BF16),
                   jax.ShapeDtypeStruct((KWP, DC), F32), jax.ShapeDtypeStruct((1, DC), F32)],
        in_specs=[col, col, pl.BlockSpec((L, 128), lambda c: (0, nc + c)),
                  pl.BlockSpec((KWP, 128), lambda c: (0, c))],
        out_specs=[col, col, pl.BlockSpec((KWP, 128), lambda c: (0, c)), pl.BlockSpec((1, 128), lambda c: (0, c))],
        scratch_shapes=[pltpu.VMEM((L + KWP, 128), F32), pltpu.VMEM((L + KWP, 128), F32)],
        compiler_params=_params("parallel"),
    )(dz1, vg, vg, dw)


NLB = QS // 128


def _lb_store(ref, rows, val):
    for cb in range(NLB):
        ref[cb, rows, :] = val[:, cb * 128:(cb + 1) * 128]


def _lb_load(ref, rows):
    return jnp.concatenate([ref[cb, rows, :] for cb in range(NLB)], axis=1)


def _scan(xr_ref, xi_ref, base, T, ar, ai, atr, ati, reverse):
    W = ar.shape[1]
    ar, ai, atr, ati = (jnp.broadcast_to(v, (8, W)) for v in (ar, ai, atr, ati))
    zero = jnp.zeros((8, W), F32)

    def rows(t, g):
        tt = T - 1 - t if reverse else t
        return pl.ds(base + g * 8 * T + tt, 8, stride=T)

    def make_step(store):
        def step(t, carry):
            out = []
            for g in range(NGRP):
                sr, si = carry[2 * g], carry[2 * g + 1]
                idx = rows(t, g)
                nr = ar * sr - ai * si + _lb_load(xr_ref, idx)
                ni = ar * si + ai * sr + _lb_load(xi_ref, idx)
                if store:
                    _lb_store(xr_ref, idx, nr)
                    _lb_store(xi_ref, idx, ni)
                out += [nr, ni]
            return tuple(out)
        return step

    ends = lax.fori_loop(0, T, make_step(False), (zero,) * (2 * NGRP))
    sub = lax.broadcasted_iota(jnp.int32, (8, W), 0)
    edge = sub == (7 if reverse else 0)
    shift, last = (7, 0) if reverse else (1, 7)
    inr, ini = jnp.zeros((1, W), F32), jnp.zeros((1, W), F32)
    starts = [None] * (2 * NGRP)
    for g in (reversed(range(NGRP)) if reverse else range(NGRP)):
        er, ei = ends[2 * g], ends[2 * g + 1]
        cr, ci = jnp.where(edge, inr, 0.0), jnp.where(edge, ini, 0.0)
        for _ in range(7):
            nr = atr * cr - ati * ci + er
            ni = atr * ci + ati * cr + ei
            cr = jnp.where(edge, inr, pltpu.roll(nr, shift, 0))
            ci = jnp.where(edge, ini, pltpu.roll(ni, shift, 0))
        starts[2 * g], starts[2 * g + 1] = cr, ci
        inr = (atr * cr - ati * ci + er)[last:last + 1]
        ini = (atr * ci + ati * cr + ei)[last:last + 1]
    lax.fori_loop(0, T, make_step(True), tuple(starts))


def _ssm_fwd(uf, bre, bim, cre, cim, lamp, dsk, name, comm=None):
    L = uf.shape[0]
    T = L // NSEG
    tc = L // NCH

    def body(u_ref, bre_ref, bim_ref, cre_ref, cim_ref, lam_ref, d_ref, y_ref, sr_s, si_s):
        for k in range(NCH):
            sl = slice(k * tc, (k + 1) * tc)
            uk = u_ref[sl, :]
            _lb_store(sr_s, sl, _dot(uk, bre_ref[0]))
            _lb_store(si_s, sl, _dot(uk, bim_ref[0]))
        _scan(sr_s, si_s, 0, T, lam_ref[0:1, :], lam_ref[1:2, :], lam_ref[2:3, :], lam_ref[3:4, :], False)
        for k in range(NCH):
            sl = slice(k * tc, (k + 1) * tc)
            y_ref[sl, :] = (_dot(_lb_load(sr_s, sl).astype(BF16), cre_ref[0])
                            - _dot(_lb_load(si_s, sl).astype(BF16), cim_ref[0])
                            + d_ref[...] * u_ref[sl, :].astype(F32))

    return _call(
        body, comm, name=name, grid=(NQ,),
        out_shape=[jax.ShapeDtypeStruct((L, DS), F32)],
        in_specs=[pl.BlockSpec((L, QU), lambda q: (0, q)),
                  pl.BlockSpec((1, QU, QS), lambda q: (q, 0, 0)), pl.BlockSpec((1, QU, QS), lambda q: (q, 0, 0)),
                  pl.BlockSpec((1, QS, QU), lambda q: (q, 0, 0)), pl.BlockSpec((1, QS, QU), lambda q: (q, 0, 0)),
                  pl.BlockSpec((8, QS), lambda q: (0, q)), pl.BlockSpec((1, QU), lambda q: (0, q))],
        out_specs=[pl.BlockSpec((L, QU), lambda q: (0, q))],
        scratch_shapes=[pltpu.VMEM((NLB, L, 128), F32), pltpu.VMEM((NLB, L, 128), F32)],
        params=_params("parallel"),
    )(uf, bre, bim, cre, cim, lamp, dsk)


def _ssm_bwd(uf, dyss, bre, bim, cre, cim, lamp, dsk, name, comm=None):
    L = uf.shape[0]
    T = L // NSEG
    tc = L // NCH

    def body(u_ref, dy_ref, bre_ref, bim_ref, cre_ref, cim_ref, lam_ref, d_ref,
             du_ref, dbre_ref, dbim_ref, dcre_ref, dcim_ref, dlam_ref, dd_ref, sr_s, si_s, gr_s, gi_s):
        _lb_store(sr_s, slice(0, SOFF), jnp.zeros((SOFF, QS), F32))
        _lb_store(si_s, slice(0, SOFF), jnp.zeros((SOFF, QS), F32))
        for k in range(NCH):
            sl = slice(k * tc, (k + 1) * tc)
            ss = slice(SOFF + k * tc, SOFF + (k + 1) * tc)
            uk = u_ref[sl, :]
            dyk = dy_ref[sl, :].astype(BF16)
            _lb_store(sr_s, ss, _dot(uk, bre_ref[0]))
            _lb_store(si_s, ss, _dot(uk, bim_ref[0]))
            _lb_store(gr_s, sl, _dot_nt(dyk, cre_ref[0]))
            _lb_store(gi_s, sl, -_dot_nt(dyk, cim_ref[0]))
        ar, ai, atr, ati = lam_ref[0:1, :], lam_ref[1:2, :], lam_ref[2:3, :], lam_ref[3:4, :]
        _scan(sr_s, si_s, SOFF, T, ar, ai, atr, ati, False)
        _scan(gr_s, gi_s, 0, T, ar, -ai, atr, -ati, True)
        dbre = jnp.zeros((QU, QS), F32)
        dbim = jnp.zeros((QU, QS), F32)
        dcre = jnp.zeros((QS, QU), F32)
        dcim = jnp.zeros((QS, QU), F32)
        dd = jnp.zeros((1, QU), F32)
        qr = jnp.zeros((1, QS), F32)
        qi = jnp.zeros((1, QS), F32)
        for k in range(NCH):
            sl = slice(k * tc, (k + 1) * tc)
            ss = slice(SOFF + k * tc, SOFF + (k + 1) * tc)
            sp = slice(SOFF - 1 + k * tc, SOFF - 1 + (k + 1) * tc)
            uk = u_ref[sl, :]
            dyk = dy_ref[sl, :]
            dyb = dyk.astype(BF16)
            gr, gi = _lb_load(gr_s, sl), _lb_load(gi_s, sl)
            pr, pi = _lb_load(sr_s, sp), _lb_load(si_s, sp)
            qr = qr + jnp.sum(gr * pr + gi * pi, axis=0, keepdims=True)
            qi = qi + jnp.sum(gi * pr - gr * pi, axis=0, keepdims=True)
            grb, gib = gr.astype(BF16), gi.astype(BF16)
            du_ref[sl, :] = (_dot_nt(grb, bre_ref[0]) + _dot_nt(gib, bim_ref[0])
                             + dyk * d_ref[...]).astype(BF16)
            dbre = dbre + _dot_tn(uk, grb)
            dbim = dbim + _dot_tn(uk, gib)
            dcre = dcre + _dot_tn(_lb_load(sr_s, ss).astype(BF16), dyb)
            dcim = dcim - _dot_tn(_lb_load(si_s, ss).astype(BF16), dyb)
            dd = dd + jnp.sum(dyk * uk.astype(F32), axis=0, keepdims=True)
        dlam_ref[0] = jnp.concatenate([qr, qi, jnp.zeros((6, QS), F32)], axis=0)
        dbre_ref[0] = dbre
        dbim_ref[0] = dbim
        dcre_ref[0] = dcre
        dcim_ref[0] = dcim
        dd_ref[...] = dd

    col = pl.BlockSpec((L, QU), lambda q: (0, q))
    bsp = pl.BlockSpec((1, QU, QS), lambda q: (q, 0, 0))
    csp = pl.BlockSpec((1, QS, QU), lambda q: (q, 0, 0))
    return _call(
        body, comm, name=name, grid=(NQ,),
        out_shape=[jax.ShapeDtypeStruct((L, DS), BF16),
                   jax.ShapeDtypeStruct((NQ, QU, QS), F32), jax.ShapeDtypeStruct((NQ, QU, QS), F32),
                   jax.ShapeDtypeStruct((NQ, QS, QU), F32), jax.ShapeDtypeStruct((NQ, QS, QU), F32),
                   jax.ShapeDtypeStruct((NQ, 8, QS), F32), jax.ShapeDtypeStruct((1, DS), F32)],
        in_specs=[col, col, bsp, bsp, csp, csp,
                  pl.BlockSpec((8, QS), lambda q: (0, q)), pl.BlockSpec((1, QU), lambda q: (0, q))],
        out_specs=[col,
                   pl.BlockSpec((1, QU, QS), lambda q: (q, 0, 0)), pl.BlockSpec((1, QU, QS), lambda q: (q, 0, 0)),
                   pl.BlockSpec((1, QS, QU), lambda q: (q, 0, 0)), pl.BlockSpec((1, QS, QU), lambda q: (q, 0, 0)),
                   pl.BlockSpec((1, 8, QS), lambda q: (q, 0, 0)), pl.BlockSpec((1, QU), lambda q: (0, q))],
        scratch_shapes=[pltpu.VMEM((NLB, L + SOFF, 128), F32), pltpu.VMEM((NLB, L + SOFF, 128), F32),
                        pltpu.VMEM((NLB, L, 128), F32), pltpu.VMEM((NLB, L, 128), F32)],
        params=_params("parallel"),
    )(uf, dyss, bre, bim, cre, cim, lamp, dsk)


def _branches(z1_ref, yss_ref, gt_ref, lng_ref, lnb_ref, wp_ref, wv_ref, wg_ref):
    zf = z1_ref[...]
    mu = jnp.mean(zf, axis=-1, keepdims=True)
    zc = zf - mu
    rstd = lax.rsqrt(jnp.mean(zc * zc, axis=-1, keepdims=True) + EPS)
    zn = zc * rstd
    z2 = zn * lng_ref[...] + lnb_ref[...]
    sz = _sigmoid(z2)
    z3 = (z2 * sz).astype(BF16)
    y_conv = _dot(z3, wp_ref[...])
    yss = yss_ref[...]
    yg = _gelu(yss).astype(BF16)
    sv = _dot(yg, wv_ref[...])
    sig = _sigmoid(_dot(yg, wg_ref[...]))
    y_ssm = sv * sig
    gc = gt_ref[:, 0:D].astype(F32)
    gs = gt_ref[:, D:2 * D].astype(F32)
    m = gc * y_conv + gs * y_ssm
    return dict(rstd=rstd, zn=zn, z2=z2, sz=sz, z3=z3, y_conv=y_conv, yss=yss, yg=yg, sv=sv, sig=sig,
                y_ssm=y_ssm, gc=gc, gs=gs, m=m)


def _merge_fwd(h, z1, yss, gate, lng, lnb, wp, wv, wg, wo, name, comm=None):
    L = h.shape[0]
    tm = _tile(L, 528)

    def body(h_ref, z1_ref, yss_ref, gt_ref, lng_ref, lnb_ref, wp_ref, wv_ref, wg_ref, wo_ref, o_ref):
        f = _branches(z1_ref, yss_ref, gt_ref, lng_ref, lnb_ref, wp_ref, wv_ref, wg_ref)
        o_ref[...] = h_ref[...] + _dot(f["m"].astype(BF16), wo_ref[...])

    def row(n):
        return pl.BlockSpec((tm, n), lambda i: (i, 0))

    return _call(
        body, comm, name=name, grid=(L // tm,),
        out_shape=[jax.ShapeDtypeStruct((L, D), F32)],
        in_specs=[row(D), row(DC), row(DS), row(2 * D), _res((1, DC)), _res((1, DC)),
                  _res((DC, D)), _res((DS, D)), _res((DS, D)), _res((D, D))],
        out_specs=[row(D)],
        params=_params("parallel"),
    )(h, z1, yss, gate, lng, lnb, wp, wv, wg, wo)


def _merge_bwd(dh, z1, yss, gate, lng, lnb, wp, wv, wg, wo, name):
    L = dh.shape[0]
    tm = _tile(L, 352)

    def body(dh_ref, z1_ref, yss_ref, gt_ref, lng_ref, lnb_ref, wp_ref, wv_ref, wg_ref, wo_ref,
             m_ref, dgt_ref, dyc_ref, z3_ref, dz1_ref, yg_ref, dsv_ref, dsg_ref, dyss_ref,
             dbg_ref, dlng_ref, dlnb_ref):
        i = pl.program_id(0)
        f = _branches(z1_ref, yss_ref, gt_ref, lng_ref, lnb_ref, wp_ref, wv_ref, wg_ref)
        gc, gs, sig, sv = f["gc"], f["gs"], f["sig"], f["sv"]
        m_ref[...] = f["m"].astype(BF16)
        z3_ref[...] = f["z3"]
        yg_ref[...] = f["yg"]
        dm = _dot_nt(dh_ref[...].astype(BF16), wo_ref[...])
        dgc = (dm * f["y_conv"] * gc * (1.0 - gc)).astype(BF16)
        dgs = (dm * f["y_ssm"] * gs * (1.0 - gs)).astype(BF16)
        dgt_ref[:, 0:D] = dgc
        dgt_ref[:, D:2 * D] = dgs
        part = jnp.concatenate([jnp.sum(dgc.astype(F32), axis=0, keepdims=True),
                                jnp.sum(dgs.astype(F32), axis=0, keepdims=True)], axis=1)
        _acc_rows(dbg_ref, part, i == 0)
        dyc = (dm * gc).astype(BF16)
        dyc_ref[...] = dyc
        dys = dm * gs
        dsv = (dys * sig).astype(BF16)
        dsg = (dys * sv * sig * (1.0 - sig)).astype(BF16)
        dsv_ref[...] = dsv
        dsg_ref[...] = dsg
        dyg = _dot_nt(dsv, wv_ref[...]) + _dot_nt(dsg, wg_ref[...])
        dyss_ref[...] = dyg * _gelu_grad(f["yss"])
        dz3 = _dot_nt(dyc, wp_ref[...])
        z2, sz, zn = f["z2"], f["sz"], f["zn"]
        dz2 = dz3 * sz * (1.0 + z2 * (1.0 - sz))
        _acc_rows(dlng_ref, jnp.sum(dz2 * zn, axis=0, keepdims=True), i == 0)
        _acc_rows(dlnb_ref, jnp.sum(dz2, axis=0, keepdims=True), i == 0)
        dzn = dz2 * lng_ref[...]
        dz1_ref[...] = f["rstd"] * (dzn - jnp.mean(dzn, axis=-1, keepdims=True)
                                    - zn * jnp.mean(dzn * zn, axis=-1, keepdims=True))

    def row(n):
        return pl.BlockSpec((tm, n), lambda i: (i, 0))

    def tot(n):
        return pl.BlockSpec((1, n), lambda i: (0, 0))

    return pl.pallas_call(
        body, name=name, grid=(L // tm,),
        out_shape=[jax.ShapeDtypeStruct((L, D), BF16), jax.ShapeDtypeStruct((L, 2 * D), BF16),
                   jax.ShapeDtypeStruct((L, D), BF16), jax.ShapeDtypeStruct((L, DC), BF16),
                   jax.ShapeDtypeStruct((L, DC), F32), jax.ShapeDtypeStruct((L, DS), BF16),
                   jax.ShapeDtypeStruct((L, D), BF16), jax.ShapeDtypeStruct((L, D), BF16),
                   jax.ShapeDtypeStruct((L, DS), F32),
                   jax.ShapeDtypeStruct((1, 2 * D), F32), jax.ShapeDtypeStruct((1, DC), F32),
                   jax.ShapeDtypeStruct((1, DC), F32)],
        in_specs=[row(D), row(DC), row(DS), row(2 * D), _res((1, DC)), _res((1, DC)),
                  _res((DC, D)), _res((DS, D)), _res((DS, D)), _res((D, D))],
        out_specs=[row(D), row(2 * D), row(D), row(DC), row(DC), row(DS), row(D), row(D), row(DS),
                   tot(2 * D), tot(DC), tot(DC)],
        compiler_params=_params("arbitrary"),
    )(dh, z1, yss, gate, lng, lnb, wp, wv, wg, wo)


def _ssm_disc(lam_re, lam_im, log_dt, b_re, b_im):
    lam = lax.complex(lam_re, lam_im)
    dt = jnp.exp(log_dt)[:, None]
    lam_bar = jnp.exp(lam * dt)
    bbar = ((lam_bar - 1.0) / lam)[..., None] * lax.complex(b_re, b_im)
    return jnp.real(lam_bar), jnp.imag(lam_bar), jnp.real(bbar), jnp.imag(bbar)


def _bdiag_in(m):
    m4 = m.reshape(NQ, G // NQ, P, H)
    return jnp.einsum("qgph,gk->qghkp", m4, jnp.eye(G // NQ, dtype=m.dtype)).reshape(NQ, QU, QS)


def _bdiag_out(m):
    m4 = m.reshape(NQ, G // NQ, H, P)
    return jnp.einsum("qghp,gk->qgpkh", m4, jnp.eye(G // NQ, dtype=m.dtype)).reshape(NQ, QS, QU)


def _diag_blocks(m4):
    return jnp.einsum("qiaib->qiab", m4).reshape(G, m4.shape[2], m4.shape[4])


def _pack(parts, rows_mult=8):
    flat = jnp.concatenate([p.reshape(-1).astype(F32) for p in parts])
    n = flat.shape[0]
    tot = -(-n // (128 * rows_mult)) * (128 * rows_mult)
    return jnp.pad(flat, (0, tot - n)).reshape(tot // 128, 128)


def _unpack(buf, shapes):
    flat = buf.reshape(-1)
    out, o = [], 0
    for s in shapes:
        n = math.prod(s)
        out.append(flat[o:o + n].reshape(s))
        o += n
    return out


def kernel(x, meta_tokens, ffn1_norm, ffn1_w1, ffn1_w3, ffn1_w2, mix_norm, w_in, b_gate, conv_dw, conv_dw_b, conv_ln_g, conv_ln_b, conv_proj, ssm_lam_re, ssm_lam_im, ssm_log_dt, ssm_b_re, ssm_b_im, ssm_c_re, ssm_c_im, ssm_d, ssm_w_v, ssm_w_g, w_out, ffn2_norm, ffn2_w1, ffn2_w3, ffn2_w2, final_norm, loss_target, m_meta_tokens, m_ffn1_norm, m_ffn1_w1, m_ffn1_w3, m_ffn1_w2, m_mix_norm, m_w_in, m_b_gate, m_conv_dw, m_conv_dw_b, m_conv_ln_g, m_conv_ln_b, m_conv_proj, m_ssm_lam_re, m_ssm_lam_im, m_ssm_log_dt, m_ssm_b_re, m_ssm_b_im, m_ssm_c_re, m_ssm_c_im, m_ssm_d, m_ssm_w_v, m_ssm_w_g, m_w_out, m_ffn2_norm, m_ffn2_w1, m_ffn2_w3, m_ffn2_w2, m_final_norm, v_meta_tokens, v_ffn1_norm, v_ffn1_w1, v_ffn1_w3, v_ffn1_w2, v_mix_norm, v_w_in, v_b_gate, v_conv_dw, v_conv_dw_b, v_conv_ln_g, v_conv_ln_b, v_conv_proj, v_ssm_lam_re, v_ssm_lam_im, v_ssm_log_dt, v_ssm_b_re, v_ssm_b_im, v_ssm_c_re, v_ssm_c_im, v_ssm_d, v_ssm_w_v, v_ssm_w_g, v_w_out, v_ffn2_norm, v_ffn2_w1, v_ffn2_w3, v_ffn2_w2, v_final_norm):
    args = dict(locals())
    names = ["meta_tokens", "ffn1_norm", "ffn1_w1", "ffn1_w3", "ffn1_w2", "mix_norm", "w_in", "b_gate",
             "conv_dw", "conv_dw_b", "conv_ln_g", "conv_ln_b", "conv_proj", "ssm_lam_re", "ssm_lam_im",
             "ssm_log_dt", "ssm_b_re", "ssm_b_im", "ssm_c_re", "ssm_c_im", "ssm_d", "ssm_w_v", "ssm_w_g",
             "w_out", "ffn2_norm", "ffn2_w1", "ffn2_w3", "ffn2_w2", "final_norm"]
    big = ["ffn1_w1", "ffn1_w3", "ffn1_w2", "w_in", "conv_proj", "ssm_w_v", "ssm_w_g", "w_out",
           "ffn2_w1", "ffn2_w3", "ffn2_w2"]
    small = [n for n in names if n not in big]

    xs = x[0]
    S = xs.shape[0]
    L = FRONT + S
    T = L // NSEG
    jx, jy = lax.axis_index("x"), lax.axis_index("y")
    chip = 2 * jx + jy

    small_all = _gather_all(_pack([meta_tokens, conv_dw[0]]), "gather_small")
    sm = small_all[0::2].reshape(NSH, -1)
    nmt = NMETA * (D // NSH)
    ndw = KW * (DC // NSH)
    meta_full = sm[:, :nmt].reshape(NSH, NMETA, D // NSH).transpose(1, 0, 2).reshape(NMETA, D)
    dw_full = sm[:, nmt:nmt + ndw].reshape(NSH, KW, DC // NSH).transpose(1, 0, 2).reshape(KW, DC)
    dw_pad = jnp.pad(dw_full, ((0, KWP - KW), (0, 0)))
    tposed = ("ffn1_w1", "ffn1_w3", "ffn2_w1", "ffn2_w3")

    def view(a, n):
        return jnp.swapaxes(a, 1, 2) if n in tposed else a

    grp_a = ["ffn1_w1", "ffn1_w3", "ffn1_w2"]
    grp_b = ["w_in", "conv_proj", "ssm_w_v", "ssm_w_g", "w_out"]
    grp_c = ["ffn2_w1", "ffn2_w3", "ffn2_w2"]

    shards = {n: view(args[n], n)[0].astype(BF16) for n in big}

    def shard(n):
        return shards[n]

    sh_a = [shard(n) for n in grp_a]
    ga_send, ga_recv, sh_a, land_a, _ = _chips_start(
        sh_a, [jax.ShapeDtypeStruct((NSH,) + s.shape, s.dtype) for s in sh_a], True, "gather_ffn1_start",
        [small_all])

    def cols(w):
        return w.transpose(1, 0, 2).reshape(w.shape[1], -1)

    disc_in = (ssm_lam_re[0], ssm_lam_im[0], ssm_log_dt[0], ssm_b_re[0], ssm_b_im[0])
    (lbr, lbi, bbr, bbi), disc_vjp = jax.vjp(_ssm_disc, *disc_in)
    lam_t = jnp.exp(lax.complex(ssm_lam_re[0], ssm_lam_im[0]) * (jnp.exp(ssm_log_dt[0])[:, None] * T))
    lamp = jnp.concatenate([lbr.reshape(1, NST), lbi.reshape(1, NST), jnp.real(lam_t).reshape(1, NST),
                            jnp.imag(lam_t).reshape(1, NST), jnp.zeros((4, NST), F32)], axis=0)
    bre_bd, bim_bd = _bdiag_in(bbr).astype(BF16), _bdiag_in(bbi).astype(BF16)
    cre_bd, cim_bd = _bdiag_out(ssm_c_re[0]).astype(BF16), _bdiag_out(ssm_c_im[0]).astype(BF16)

    h0 = lax.dynamic_update_slice(jnp.pad(xs, ((FRONT, 0), (0, 0))), meta_full, (FRONT - NMETA, 0))
    tgt = jnp.pad(loss_target[0], ((FRONT, 0), (0, 0)))
    early_work = [h0, tgt, bre_bd, bim_bd, cre_bd, cim_bd] + [shards[n] for n in grp_b + grp_c]
    sh_a, land_a = _chips_wait(ga_send, ga_recv, sh_a, land_a, early_work, True, "gather_ffn1_wait")
    gw = dict(zip(grp_a, _pass_halves(land_a, "pass_ffn1", sh_a)))
    (h1, a1, b1), got = _ffn_fwd(h0, ffn1_norm, gw["ffn1_w1"], gw["ffn1_w3"], gw["ffn1_w2"], "ffn1_fwd",
                                 _gather_half_behind([shard(n) for n in grp_b]))
    w_in_f = _pass_halves(got[:1], "pass_w_in")[0]
    (vg, uf, gate), got1 = _mix_in_fwd(h1, mix_norm, w_in_f, b_gate, "mix_in_fwd",
                                       _join(_gather_half_behind([shard("ffn2_w1")]),
                                             _pass_halves_behind(list(got[1:]))))
    gw.update(zip(grp_b[1:], got1[1:]))
    wp_f, wv_f, wg_f = cols(gw["conv_proj"]), cols(gw["ssm_w_v"]), cols(gw["ssm_w_g"])
    wo_f = gw["w_out"].reshape(D, D)
    (z1,), got3 = _conv_fwd(vg, dw_pad, conv_dw_b, "conv_fwd", _gather_half_behind([shard("ffn2_w3")]))
    (yss,), got2 = _ssm_fwd(uf, bre_bd, bim_bd, cre_bd, cim_bd, lamp, ssm_d, "ssm_fwd",
                            _gather_half_behind([shard("ffn2_w2")]))
    (h2,), got_c = _merge_fwd(h1, z1, yss, gate, conv_ln_g, conv_ln_b, wp_f, wv_f, wg_f, wo_f, "merge_fwd",
                              _pass_halves_behind([got1[0], got3[0], got2[0]]))
    gw.update(zip(grp_c, got_c))
    dh3, a2, b2, loss_part, d_final = _ffn_fwd_loss(
        h2, ffn2_norm, gw["ffn2_w1"], gw["ffn2_w3"], gw["ffn2_w2"], final_norm.reshape(1, D), tgt, "ffn2_fwd_loss")

    gbig = {}
    core = lax.axis_index("c").astype(jnp.int32).reshape(1)

    def pair_sums(group, tag):
        gl = [gbig[n] for n in group]
        sib = _pair_exchange(gl, "pair_exchange_" + tag)
        out = []
        for n, g_, s_ in zip(group, gl, sib):
            out.append(_add_pair(g_, s_, core, "pair_" + n))
        return out

    (dh2, da2, db2, s2, n2, d_ffn2_norm), _ = _ffn_bwd(
        h2, ffn2_norm, dh3, a2, b2, gw["ffn2_w1"], gw["ffn2_w3"], gw["ffn2_w2"], "ffn2_bwd")
    gbig["ffn2_w1"] = _wgrad(da2, n2, "ffn2_dw1")
    gbig["ffn2_w3"] = _wgrad(db2, n2, "ffn2_dw3")
    gbig["ffn2_w2"] = _wgrad(s2, dh3, "ffn2_dw2", 0.5)
    pair_c = pair_sums(grp_c, "ffn2")
    (m_b, dgate, dyc, z3, dz1, yg, dsv, dsg, dyss, d_b_gate, d_ln_g, d_ln_b) = _merge_bwd(
        dh2, z1, yss, gate, conv_ln_g, conv_ln_b, wp_f, wv_f, wg_f, wo_f, "merge_bwd")
    gbig["w_out"] = _wgrad(m_b, dh2, "dw_out").reshape(NSH, D // NSH, D)

    def shard_cols(gm):
        return gm.reshape(gm.shape[0], NSH, -1).transpose(1, 0, 2)

    gbig["conv_proj"] = shard_cols(_wgrad(z3, dyc, "dw_proj"))
    gbig["ssm_w_v"] = shard_cols(_wgrad(yg, dsv, "dw_v"))
    gbig["ssm_w_g"] = shard_cols(_wgrad(yg, dsg, "dw_g"))
    dv, dgl, ddw, d_dw_b = _conv_bwd(dz1, vg, dw_pad, "conv_bwd")
    (duf, dbre, dbim, dcre, dcim, dlam, d_ssm_d), recv_c = _ssm_bwd(
        uf, dyss, bre_bd, bim_bd, cre_bd, cim_bd, lamp, ssm_d, "ssm_bwd", _scatter_chips_behind(pair_c))
    dh1, u_b, dproj, d_mix_norm = _mix_in_bwd(h1, mix_norm, dh2, dv, dgl, duf, dgate, w_in_f, "mix_in_bwd")
    gbig["w_in"] = _wgrad(u_b, dproj, "dw_in")
    pair_b = pair_sums(grp_b, "mix")

    d_bbr = _diag_blocks(dbre.reshape(NQ, 8, H, 8, P)).transpose(0, 2, 1)
    d_bbi = _diag_blocks(dbim.reshape(NQ, 8, H, 8, P)).transpose(0, 2, 1)
    d_c_re = _diag_blocks(dcre.reshape(NQ, 8, P, 8, H)).transpose(0, 2, 1)
    d_c_im = _diag_blocks(dcim.reshape(NQ, 8, P, 8, H)).transpose(0, 2, 1)
    d_lbr = dlam[:, 0, :].reshape(G, P)
    d_lbi = dlam[:, 1, :].reshape(G, P)
    d_lam_re, d_lam_im, d_log_dt, d_b_re, d_b_im = disc_vjp((d_lbr, d_lbi, d_bbr, d_bbi))

    sg = {"mix_norm": d_mix_norm, "b_gate": d_b_gate, "conv_dw": ddw[:KW], "conv_dw_b": d_dw_b,
          "conv_ln_g": d_ln_g, "conv_ln_b": d_ln_b, "ssm_lam_re": d_lam_re, "ssm_lam_im": d_lam_im,
          "ssm_log_dt": d_log_dt, "ssm_b_re": d_b_re, "ssm_b_im": d_b_im, "ssm_c_re": d_c_re, "ssm_c_im": d_c_im,
          "ssm_d": d_ssm_d, "ffn2_norm": d_ffn2_norm, "final_norm": d_final}
    late = ["meta_tokens", "ffn1_norm"]
    early = [n for n in small if n not in late]

    (dh0, da1, db1, s1, n1, d_ffn1_norm), got = _ffn_bwd(
        h0, ffn1_norm, dh1, a1, b1, gw["ffn1_w1"], gw["ffn1_w3"], gw["ffn1_w2"], "ffn1_bwd",
        _join(_scatter_chips_behind(pair_b), _gather_all_behind(_pack([sg[n] for n in early]))))
    recv_b, early_all = got[:len(grp_b)], got[len(grp_b)]
    gbig["ffn1_w1"] = _wgrad(da1, n1, "ffn1_dw1")
    gbig["ffn1_w3"] = _wgrad(db1, n1, "ffn1_dw3")
    gbig["ffn1_w2"] = _wgrad(s1, dh1, "ffn1_dw2", 0.5)
    grad_x = dh0[FRONT:][None]
    sg["meta_tokens"] = dh0[FRONT - NMETA:FRONT]
    sg["ffn1_norm"] = d_ffn1_norm

    pair_a = pair_sums(grp_a, "ffn1")
    late_all = _gather_all(_pack([sg[n] for n in late]), "gather_late_grads")
    sa_send, sa_recv, pair_a, land_s, sa_token = _chips_start(
        pair_a, [jax.ShapeDtypeStruct(p.shape, p.dtype) for p in pair_a], False, "scatter_ffn1_start", [late_all])

    out_g, out_d, out_m, out_v = {}, {}, {}, {}

    def finish(group, recvs, tag, after=None):
        halves = [_sum_slots(r, "sum_" + n, after) for n, r in zip(group, recvs)]
        for n, f in zip(group, _swap_halves(halves, "swap_" + tag)):
            g3 = f.reshape(1, f.shape[0] * f.shape[1], f.shape[2])
            g3, d3, m3, v3 = _adamw(view(args[n], n), g3, view(args["m_" + n], n), view(args["v_" + n], n),
                                "adamw_" + n)
            out_g[n], out_d[n], out_m[n], out_v[n] = (view(t, n) for t in (g3, d3, m3, v3))
            done.append(d3)

    done = []
    finish(grp_b + grp_c, list(recv_b) + list(recv_c), "mix_ffn2", sa_token)

    sgr = dict(zip(early, _unpack(_sum_slots(early_all, "sum_early", sa_token), [sg[n].shape for n in early])))
    sgr.update(zip(late, _unpack(_sum_slots(late_all, "sum_late"), [sg[n].shape for n in late])))
    sgr["meta_tokens"] = lax.dynamic_slice_in_dim(sgr["meta_tokens"], chip * (D // NSH), D // NSH, axis=1)
    sgr["conv_dw"] = lax.dynamic_slice_in_dim(sgr["conv_dw"], chip * (DC // NSH), DC // NSH, axis=1)
    pshapes = [args[n].shape for n in small]
    _, d_s, m_s, v_s = _adamw(_pack([args[n] for n in small])[None], _pack([sgr[n] for n in small])[None],
                           _pack([args["m_" + n] for n in small])[None],
                           _pack([args["v_" + n] for n in small])[None], "adamw_small")
    for n, g_, d_, m_, v_ in zip(small, [sgr[n] for n in small], _unpack(d_s[0], pshapes),
                                 _unpack(m_s[0], pshapes), _unpack(v_s[0], pshapes)):
        out_g[n], out_d[n], out_m[n], out_v[n] = g_.reshape(args[n].shape), d_, m_, v_

    loss = lax.psum(loss_part[0, 0], ("x", "y", "c"))
    pair_a, recv_a = _chips_wait(sa_send, sa_recv, pair_a, land_s, [d_s, grad_x] + done, False,
                                 "scatter_ffn1_wait")
    finish(grp_a, _fill_own(pair_a, recv_a, "own_ffn1"), "ffn1")
    return (loss, grad_x, *[out_g[n] for n in names], *[out_d[n] for n in names],
            *[out_m[n] for n in names], *[out_v[n] for n in names])
```

```python
import math

import jax
import jax.numpy as jnp
from jax import lax
from jax.experimental import pallas as pl
from jax.experimental.pallas import tpu as pltpu

F32 = jnp.float32
BF16 = jnp.bfloat16

D = 1024
NSH = 4
F = 2816
FS = F // NSH
DC = 512
DS = 512
DIN = 2 * DC + DS + 2 * D
WS = DIN // NSH
KW = 31
KWP = 32
CONV_ROWS = 64
NMETA = 16
FRONT = 128
G, P, H = 32, 64, 16
NST = G * P
NQ = 4
QS = NST // NQ
QU = DS // NQ
NSEG = 32
NGRP = NSEG // 8
NCH = 8
SOFF = 8
EPS = 1e-6
LR, B1, B2, AEPS, WD, STEP = 1e-3, 0.9, 0.999, 1e-8, 0.01, 10
VMEM_LIMIT = 58 * 1024 * 1024
MESH = pl.DeviceIdType.MESH
ANY = pl.BlockSpec(memory_space=pl.ANY)


def _params(*sem):
    return pltpu.CompilerParams(dimension_semantics=sem, vmem_limit_bytes=VMEM_LIMIT)


def _res(shape):
    nd = len(shape)
    return pl.BlockSpec(shape, lambda *_: (0,) * nd, pipeline_mode=pl.Buffered(1))


def _tile(n, cap, mult=16):
    best = None
    for t in range(mult, min(n, cap) + 1, mult):
        if n % t == 0:
            best = t
    assert best is not None, (n, cap, mult)
    return best


def _dot(a, b):
    return jnp.dot(a, b, preferred_element_type=F32)


def _dot_nt(a, b):
    return lax.dot_general(a, b, (((1,), (1,)), ((), ())), preferred_element_type=F32)


def _dot_tn(a, b):
    return lax.dot_general(a, b, (((0,), (0,)), ((), ())), preferred_element_type=F32)


def _sigmoid(x):
    return 1.0 / (1.0 + jnp.exp(-x))


_GC = math.sqrt(2.0 / math.pi)
_GA = 0.044715


def _gelu(x):
    return 0.5 * x * (1.0 + jnp.tanh(_GC * (x + _GA * x * x * x)))


def _gelu_grad(x):
    t = jnp.tanh(_GC * (x + _GA * x * x * x))
    return 0.5 * (1.0 + t) + 0.5 * x * (1.0 - t * t) * _GC * (1.0 + 3.0 * _GA * x * x)


def _rms(hv, g):
    r = lax.rsqrt(jnp.mean(hv * hv, axis=-1, keepdims=True) + EPS)
    return hv * r * g, r


def _rms_bwd(dn, hv, r, g):
    xh = hv * r
    dxh = dn * g
    return r * (dxh - xh * jnp.mean(dxh * xh, axis=-1, keepdims=True)), xh


def _acc_rows(ref, part, first):
    @pl.when(first)
    def _():
        ref[...] = part

    @pl.when(jnp.logical_not(first))
    def _():
        ref[...] += part


def _coords():
    return lax.axis_index("x"), lax.axis_index("y"), lax.axis_index("c")


def _flip(v, d):
    return 1 - v if d else v


def _run(local, remote):
    for cp in local + remote:
        cp.start()
    for cp in remote:
        cp.wait()
    for cp in local:
        cp.wait()


def _via_vmem(src, dst, stage, sems, i):
    return (pltpu.make_async_copy(src, stage, sems.at[2 * i]), pltpu.make_async_copy(stage, dst, sems.at[2 * i + 1]))


def _run_staged(staged, remote):
    for load, _ in staged:
        load.start()
    for cp in remote:
        cp.start()
    for load, store in staged:
        load.wait()
        store.start()
    for cp in remote:
        cp.wait()
    for _, store in staged:
        store.wait()


_REL3 = ((1, 0), (0, 1), (1, 1))


class _Behind:
    def __init__(self, arrays, out_shapes, scratch, build, alias_pairs=()):
        self.arrays, self.out_shapes, self.scratch, self.build = list(arrays), list(out_shapes), list(scratch), build
        self.alias_pairs = list(alias_pairs)

    def aliases(self):
        return self.alias_pairs

    def start(self, ins, outs, scr):
        staged, remote = self.build(ins, outs, scr)
        for load, _ in staged:
            load.start()
        for cp in remote:
            cp.start()

    def finish(self, ins, outs, scr):
        staged, remote = self.build(ins, outs, scr)
        for load, store in staged:
            load.wait()
            store.start()
        for cp in remote:
            cp.wait()
        for _, store in staged:
            store.wait()


def _call(body, comm, *, name, grid, in_specs, out_specs, out_shape, scratch_shapes=(), params):
    in_specs, out_specs, out_shape = list(in_specs), list(out_specs), list(out_shape)
    scratch_shapes = list(scratch_shapes)
    if comm is None:
        f = pl.pallas_call(body, name=name, grid=grid, in_specs=in_specs, out_specs=out_specs,
                           out_shape=out_shape, scratch_shapes=scratch_shapes, compiler_params=params)
        return lambda *args: (f(*args), [])
    ni, no, ns = len(in_specs), len(out_specs), len(scratch_shapes)
    ci, co = len(comm.arrays), len(comm.out_shapes)

    def hosted(*refs):
        ins, cin = refs[:ni], refs[ni:ni + ci]
        outs, cout = refs[ni + ci:ni + ci + no], refs[ni + ci + no:ni + ci + no + co]
        scr, cscr = refs[ni + ci + no + co:ni + ci + no + co + ns], refs[ni + ci + no + co + ns:]
        first = last = None
        for axis, size in enumerate(grid):
            i = pl.program_id(axis)
            first = (i == 0) if first is None else jnp.logical_and(first, i == 0)
            last = (i == size - 1) if last is None else jnp.logical_and(last, i == size - 1)

        @pl.when(first)
        def _():
            comm.start(cin, cout, cscr)

        body(*ins, *outs, *scr)

        @pl.when(last)
        def _():
            comm.finish(cin, cout, cscr)

    f = pl.pallas_call(hosted, name=name, grid=grid, in_specs=in_specs + [ANY] * ci,
                       out_specs=out_specs + [ANY] * co, out_shape=out_shape + comm.out_shapes,
                       scratch_shapes=scratch_shapes + comm.scratch,
                       input_output_aliases={ni + a: no + b for a, b in comm.aliases()},
                       compiler_params=_params(*(("arbitrary",) * len(grid))))

    def run(*args):
        res = f(*args, *comm.arrays)
        return res[:no], res[no:]

    return run


def _gather_half_behind(shards):
    n = len(shards)

    def build(ins, outs, scr):
        send, recv, loc = scr[:3]
        stage = scr[3:]
        x, y, c = _coords()
        me = 2 * x + y
        staged = [_via_vmem(ins[t], outs[t].at[me], stage[t], loc, t) for t in range(n)]
        remote = []
        for t in range(n):
            half = shards[t].shape[0] // 2
            mine = pl.ds(c * half, half)
            for k, (dx, dy) in enumerate(_REL3):
                remote.append(pltpu.make_async_remote_copy(
                    src_ref=ins[t].at[mine], dst_ref=outs[t].at[me, mine],
                    send_sem=send.at[3 * t + k], recv_sem=recv.at[3 * t + k],
                    device_id=(_flip(x, dx), _flip(y, dy), c), device_id_type=MESH))
        return staged, remote

    return _Behind(shards, [jax.ShapeDtypeStruct((NSH,) + s.shape, s.dtype) for s in shards],
                   [pltpu.SemaphoreType.DMA((3 * n,)), pltpu.SemaphoreType.DMA((3 * n,)),
                    pltpu.SemaphoreType.DMA((2 * n,))] + [pltpu.VMEM(s.shape, s.dtype) for s in shards], build)


def _pass_halves(gathered, name, own=()):
    n, m = len(gathered), len(own)

    def body(*refs):
        shards, outs = refs[n:n + m], refs[n + m:2 * n + m]
        send, recv, loc = refs[2 * n + m:2 * n + m + 3]
        stage = refs[2 * n + m + 3:]
        x, y, c = _coords()
        staged = [_via_vmem(shards[t], outs[t].at[2 * x + y], stage[t], loc, t) for t in range(m)]
        remote = []
        for t in range(n):
            half = gathered[t].shape[1] // 2
            mine = pl.ds(c * half, half)
            for k, (dx, dy) in enumerate(_REL3):
                slot = 2 * _flip(x, dx) + _flip(y, dy)
                remote.append(pltpu.make_async_remote_copy(
                    src_ref=outs[t].at[slot, mine], dst_ref=outs[t].at[slot, mine],
                    send_sem=send.at[3 * t + k], recv_sem=recv.at[3 * t + k],
                    device_id=(x, y, 1 - c), device_id_type=MESH))
        _run_staged(staged, remote)

    return pl.pallas_call(
        body, name=name,
        out_shape=[jax.ShapeDtypeStruct(g.shape, g.dtype) for g in gathered],
        in_specs=[ANY] * (n + m), out_specs=[ANY] * n, input_output_aliases={t: t for t in range(n)},
        scratch_shapes=[pltpu.SemaphoreType.DMA((3 * n,)), pltpu.SemaphoreType.DMA((3 * n,)),
                        pltpu.SemaphoreType.DMA((max(2 * m, 1),))] + [pltpu.VMEM(s.shape, s.dtype) for s in own],
        compiler_params=pltpu.CompilerParams(vmem_limit_bytes=VMEM_LIMIT),
    )(*gathered, *own)


def _fill_own(sums, recvs, name):
    n = len(sums)

    def body(*refs):
        ins, outs = refs[:n], refs[2 * n:3 * n]
        loc = refs[3 * n]
        stage = refs[3 * n + 1:]
        x, y, _ = _coords()
        me = 2 * x + y
        _run_staged([_via_vmem(ins[t].at[me], outs[t].at[me], stage[t], loc, t) for t in range(n)], [])

    return pl.pallas_call(
        body, name=name,
        out_shape=[jax.ShapeDtypeStruct(r.shape, r.dtype) for r in recvs],
        in_specs=[ANY] * (2 * n), out_specs=[ANY] * n, input_output_aliases={n + t: t for t in range(n)},
        scratch_shapes=[pltpu.SemaphoreType.DMA((2 * n,))] + [pltpu.VMEM(s.shape[1:], s.dtype) for s in sums],
        compiler_params=pltpu.CompilerParams(vmem_limit_bytes=VMEM_LIMIT),
    )(*sums, *recvs)


def _scatter_chips_behind(sums):
    n = len(sums)

    def build(ins, outs, scr):
        send, recv, loc = scr[:3]
        stage = scr[3:]
        x, y, c = _coords()
        me = 2 * x + y
        staged = [_via_vmem(ins[t].at[me], outs[t].at[me], stage[t], loc, t) for t in range(n)]
        remote = []
        for t in range(n):
            for k, (dx, dy) in enumerate(_REL3):
                px, py = _flip(x, dx), _flip(y, dy)
                remote.append(pltpu.make_async_remote_copy(
                    src_ref=ins[t].at[2 * px + py], dst_ref=outs[t].at[me],
                    send_sem=send.at[3 * t + k], recv_sem=recv.at[3 * t + k],
                    device_id=(px, py, c), device_id_type=MESH))
        return staged, remote

    return _Behind(sums, [jax.ShapeDtypeStruct(s.shape, s.dtype) for s in sums],
                   [pltpu.SemaphoreType.DMA((3 * n,)), pltpu.SemaphoreType.DMA((3 * n,)),
                    pltpu.SemaphoreType.DMA((2 * n,))] + [pltpu.VMEM(s.shape[1:], s.dtype) for s in sums], build)


def _gather_all_behind(a):
    def build(ins, outs, scr):
        send, recv, loc, stage = scr
        x, y, c = _coords()
        me = 4 * x + 2 * y + c
        staged = [_via_vmem(ins[0], outs[0].at[me], stage, loc, 0)]
        remote = [pltpu.make_async_remote_copy(
            src_ref=ins[0], dst_ref=outs[0].at[me], send_sem=send.at[k], recv_sem=recv.at[k],
            device_id=(_flip(x, dx), _flip(y, dy), _flip(c, dc)), device_id_type=MESH)
            for k, (dx, dy, dc) in enumerate(_REL7)]
        return staged, remote

    return _Behind([a], [jax.ShapeDtypeStruct((8,) + a.shape, a.dtype)],
                   [pltpu.SemaphoreType.DMA((7,)), pltpu.SemaphoreType.DMA((7,)), pltpu.SemaphoreType.DMA((2,)),
                    pltpu.VMEM(a.shape, a.dtype)], build)


HBM = pl.BlockSpec(memory_space=pltpu.HBM)
SEM = pl.BlockSpec(memory_space=pltpu.SEMAPHORE)
EFFECT = pltpu.SideEffectType.DATAFLOW_SIDE_EFFECTING


def _chip_copies(srcs, lands, send, recv, gather):
    x, y, c = _coords()
    me = 2 * x + y
    cps = []
    for t in range(len(srcs)):
        for k, (dx, dy) in enumerate(_REL3):
            px, py = _flip(x, dx), _flip(y, dy)
            if gather:
                half = srcs[t].shape[0] // 2
                mine = pl.ds(c * half, half)
                src, dst = srcs[t].at[mine], lands[t].at[me, mine]
            else:
                src, dst = srcs[t].at[2 * px + py], lands[t].at[me]
            cps.append(pltpu.make_async_remote_copy(
                src_ref=src, dst_ref=dst, send_sem=send.at[3 * t + k], recv_sem=recv.at[3 * t + k],
                device_id=(px, py, c), device_id_type=MESH))
    return cps


def _chips_start(arrays, land_shapes, gather, name, after=()):
    n = len(arrays)

    def body(*refs):
        srcs, lands = refs[:n], refs[n:2 * n]
        send, recv = refs[2 * n + len(after)], refs[2 * n + len(after) + 1]
        token = refs[-1]
        for cp in _chip_copies(srcs, lands, send, recv, gather):
            cp.start()
        token[...] = jnp.zeros_like(token)

    lands = [lax.empty(s.shape, s.dtype) for s in land_shapes]
    thru = [pltpu.HBM(a.shape, a.dtype) for a in arrays] + [pltpu.HBM(s.shape, s.dtype) for s in land_shapes]
    res = pl.pallas_call(
        body, name=name,
        out_shape=(pltpu.SemaphoreType.DMA((3 * n,)), pltpu.SemaphoreType.DMA((3 * n,)), *thru,
                   jax.ShapeDtypeStruct((8, 128), F32)),
        in_specs=[HBM] * (2 * n) + [ANY] * len(after),
        out_specs=(SEM, SEM, *([HBM] * (2 * n)), pl.BlockSpec(memory_space=pltpu.VMEM)),
        input_output_aliases={t: 2 + t for t in range(2 * n)},
        compiler_params=pltpu.CompilerParams(has_side_effects=EFFECT),
    )(*[pltpu.with_memory_space_constraint(a, pltpu.HBM) for a in arrays],
      *[pltpu.with_memory_space_constraint(z, pltpu.HBM) for z in lands], *after)
    return res[0], res[1], list(res[2:2 + n]), list(res[2 + n:2 + 2 * n]), res[-1]


def _chips_wait(send, recv, arrays, lands, after, gather, name):
    n = len(arrays)

    def body(*refs):
        srcs, ls = refs[:n], refs[n:2 * n]
        sd, rv = refs[2 * n], refs[2 * n + 1]
        for cp in _chip_copies(srcs, ls, sd, rv, gather):
            cp.wait_send()
            cp.wait_recv()

    res = pl.pallas_call(
        body, name=name,
        out_shape=[pltpu.HBM(a.shape, a.dtype) for a in arrays] + [pltpu.HBM(z.shape, z.dtype) for z in lands],
        in_specs=[HBM] * (2 * n) + [SEM, SEM] + [ANY] * len(after), out_specs=[HBM] * (2 * n),
        input_output_aliases={t: t for t in range(2 * n)},
        compiler_params=pltpu.CompilerParams(has_side_effects=EFFECT),
    )(*arrays, *lands, send, recv, *after)
    return list(res[:n]), list(res[n:])


def _join(*parts):
    def cut(seq, key):
        res, o = [], 0
        for p in parts:
            k = len(getattr(p, key))
            res.append(seq[o:o + k])
            o += k
        return res

    def build(ins, outs, scr):
        staged, remote = [], []
        for p, i, o, s in zip(parts, cut(ins, "arrays"), cut(outs, "out_shapes"), cut(scr, "scratch")):
            st, rm = p.build(i, o, s)
            staged += st
            remote += rm
        return staged, remote

    pairs, ai, oi = [], 0, 0
    for p in parts:
        pairs += [(ai + a, oi + b) for a, b in p.alias_pairs]
        ai, oi = ai + len(p.arrays), oi + len(p.out_shapes)
    return _Behind(sum((p.arrays for p in parts), []), sum((p.out_shapes for p in parts), []),
                   sum((p.scratch for p in parts), []), build, pairs)


def _pass_halves_behind(gathered):
    n = len(gathered)

    def build(ins, outs, scr):
        send, recv = scr
        x, y, c = _coords()
        remote = []
        for t in range(n):
            half = gathered[t].shape[1] // 2
            mine = pl.ds(c * half, half)
            for k, (dx, dy) in enumerate(_REL3):
                slot = 2 * _flip(x, dx) + _flip(y, dy)
                remote.append(pltpu.make_async_remote_copy(
                    src_ref=outs[t].at[slot, mine], dst_ref=outs[t].at[slot, mine],
                    send_sem=send.at[3 * t + k], recv_sem=recv.at[3 * t + k],
                    device_id=(x, y, 1 - c), device_id_type=MESH))
        return [], remote

    return _Behind(gathered, [jax.ShapeDtypeStruct(g.shape, g.dtype) for g in gathered],
                   [pltpu.SemaphoreType.DMA((3 * n,)), pltpu.SemaphoreType.DMA((3 * n,))], build,
                   [(t, t) for t in range(n)])


_REL7 = tuple((dx, dy, dc) for dx in (0, 1) for dy in (0, 1) for dc in (0, 1))[1:]


def _gather_all(a, name):
    def body(a_ref, o_ref, send, recv, loc):
        x, y, c = _coords()
        me = 4 * x + 2 * y + c
        local = [pltpu.make_async_copy(a_ref, o_ref.at[me], loc.at[0])]
        remote = [pltpu.make_async_remote_copy(
            src_ref=a_ref, dst_ref=o_ref.at[me], send_sem=send.at[k], recv_sem=recv.at[k],
            device_id=(_flip(x, dx), _flip(y, dy), _flip(c, dc)), device_id_type=MESH)
            for k, (dx, dy, dc) in enumerate(_REL7)]
        _run(local, remote)

    return pl.pallas_call(
        body, name=name,
        out_shape=jax.ShapeDtypeStruct((8,) + a.shape, a.dtype),
        in_specs=[ANY], out_specs=ANY,
        scratch_shapes=[pltpu.SemaphoreType.DMA((7,)), pltpu.SemaphoreType.DMA((7,)),
                        pltpu.SemaphoreType.DMA((1,))],
    )(a)


def _pair_exchange(grads, name):
    n = len(grads)

    def body(*refs):
        ins, outs = refs[:n], refs[n:2 * n]
        send, recv = refs[2 * n:]
        x, y, c = _coords()
        remote = []
        for t in range(n):
            half = grads[t].shape[1] // 2
            remote.append(pltpu.make_async_remote_copy(
                src_ref=ins[t].at[:, pl.ds((1 - c) * half, half)], dst_ref=outs[t],
                send_sem=send.at[t], recv_sem=recv.at[t],
                device_id=(x, y, 1 - c), device_id_type=MESH))
        _run([], remote)

    return pl.pallas_call(
        body, name=name,
        out_shape=[jax.ShapeDtypeStruct((NSH, g.shape[1] // 2, g.shape[2]), g.dtype) for g in grads],
        in_specs=[ANY] * n, out_specs=[ANY] * n,
        scratch_shapes=[pltpu.SemaphoreType.DMA((n,)), pltpu.SemaphoreType.DMA((n,))],
    )(*grads)


def _swap_halves(halves, name):
    n = len(halves)

    def body(*refs):
        ins, outs = refs[:n], refs[n:2 * n]
        send, recv, loc = refs[2 * n:2 * n + 3]
        stage = refs[2 * n + 3:]
        x, y, c = _coords()
        local = [_via_vmem(ins[t], outs[t].at[c], stage[t], loc, t) for t in range(n)]
        remote = [pltpu.make_async_remote_copy(
            src_ref=ins[t], dst_ref=outs[t].at[c], send_sem=send.at[t], recv_sem=recv.at[t],
            device_id=(x, y, 1 - c), device_id_type=MESH) for t in range(n)]
        _run_staged(local, remote)

    return pl.pallas_call(
        body, name=name,
        out_shape=[jax.ShapeDtypeStruct((2,) + h.shape, h.dtype) for h in halves],
        in_specs=[ANY] * n, out_specs=[ANY] * n,
        scratch_shapes=[pltpu.SemaphoreType.DMA((n,)), pltpu.SemaphoreType.DMA((n,)),
                        pltpu.SemaphoreType.DMA((2 * n,))]
        + [pltpu.VMEM(h.shape, h.dtype) for h in halves],
        compiler_params=pltpu.CompilerParams(vmem_limit_bytes=VMEM_LIMIT),
    )(*halves)


def _sum_slots(r, name, after=None):
    K, R, C = r.shape
    tr = _tile(R, max(16, (1 << 22) // (K * C)), 8 * (4 // r.dtype.itemsize))

    def body(r_ref, *rest):
        o_ref = rest[-1]
        acc = r_ref[0].astype(F32)
        for k in range(1, K):
            acc = acc + r_ref[k].astype(F32)
        o_ref[...] = acc

    dep = [] if after is None else [after]
    return pl.pallas_call(
        body, name=name, grid=(R // tr,),
        out_shape=jax.ShapeDtypeStruct((R, C), F32),
        in_specs=[pl.BlockSpec((K, tr, C), lambda i: (0, i, 0))] + [ANY] * len(dep),
        out_specs=pl.BlockSpec((tr, C), lambda i: (i, 0)),
        compiler_params=_params("parallel"),
    )(r, *dep)


def _add_pair(g, s, core, name):
    _, half, C = s.shape
    tr = _tile(half, max(16, (1 << 19) // C))
    nb = half // tr

    def body(c_ref, g_ref, s_ref, o_ref):
        o_ref[...] = (g_ref[...].astype(F32) + s_ref[...].astype(F32)).astype(BF16)

    spec = pl.BlockSpec((1, tr, C), lambda j, i, c_ref: (j, i, 0))
    return pl.pallas_call(
        body, name=name,
        grid_spec=pltpu.PrefetchScalarGridSpec(
            num_scalar_prefetch=1, grid=(NSH, nb),
            in_specs=[pl.BlockSpec((1, tr, C), lambda j, i, c_ref: (j, c_ref[0] * nb + i, 0)), spec],
            out_specs=spec),
        out_shape=jax.ShapeDtypeStruct(s.shape, BF16),
        compiler_params=_params("parallel", "parallel"),
    )(core, g, s)


def _adamw(w, g, m, v, name):
    _, R, C = w.shape
    tr = _tile(R, max(8, (1 << 18) // C), 8)
    c1 = 1.0 / (1.0 - B1 ** STEP)
    c2 = 1.0 / (1.0 - B2 ** STEP)

    def body(w_ref, g_ref, m_ref, v_ref, go_ref, d_ref, nm_ref, nv_ref):
        gv = g_ref[...]
        go_ref[...] = gv
        nm = B1 * m_ref[...] + (1.0 - B1) * gv
        nv = B2 * v_ref[...] + (1.0 - B2) * gv * gv
        nm_ref[...] = nm
        nv_ref[...] = nv
        d_ref[...] = -LR * ((nm * c1) / (jnp.sqrt(nv * c2) + AEPS) + WD * w_ref[...])

    spec = pl.BlockSpec((1, tr, C), lambda i: (0, i, 0))
    return pl.pallas_call(
        body, name=name, grid=(R // tr,),
        out_shape=[jax.ShapeDtypeStruct((1, R, C), F32)] * 4,
        in_specs=[spec] * 4, out_specs=[spec] * 4,
        compiler_params=_params("parallel"),
    )(w, g, m, v)


def _ffn_fwd(h, g, w1, w3, w2, name, comm=None):
    L = h.shape[0]
    tm = _tile(L, 704)

    def body(h_ref, g_ref, w1_ref, w3_ref, w2_ref, o_ref, a_ref, b_ref, n_s, acc_s):
        j = pl.program_id(1)

        @pl.when(j == 0)
        def _():
            hv = h_ref[...]
            n, _ = _rms(hv, g_ref[...])
            n_s[...] = n.astype(BF16)
            acc_s[...] = hv

        n = n_s[...]
        a = _dot_nt(n, w1_ref[0])
        b = _dot_nt(n, w3_ref[0])
        a_ref[0] = a.astype(BF16)
        b_ref[0] = b.astype(BF16)
        s = (a * _sigmoid(a) * b).astype(BF16)
        acc_s[...] += 0.5 * _dot(s, w2_ref[0])

        @pl.when(j == NSH - 1)
        def _():
            o_ref[...] = acc_s[...]

    row = pl.BlockSpec((tm, D), lambda i, j: (i, 0))
    hid = pl.BlockSpec((1, tm, FS), lambda i, j: (j, i, 0))
    wsp = pl.BlockSpec((1, FS, D), lambda i, j: (j, 0, 0))
    return _call(
        body, comm, name=name, grid=(L // tm, NSH),
        out_shape=[jax.ShapeDtypeStruct((L, D), F32),
                   jax.ShapeDtypeStruct((NSH, L, FS), BF16), jax.ShapeDtypeStruct((NSH, L, FS), BF16)],
        in_specs=[row, _res((1, D)), wsp, wsp, wsp],
        out_specs=[row, hid, hid],
        scratch_shapes=[pltpu.VMEM((tm, D), BF16), pltpu.VMEM((tm, D), F32)],
        params=_params("arbitrary", "arbitrary"),
    )(h, g, w1, w3, w2)


def _loss_head(hv, gv, tv, row0):
    y, r = _rms(hv, gv)
    row = row0 + lax.broadcasted_iota(jnp.int32, (hv.shape[0], 1), 0)
    e = jnp.where(row >= FRONT, y - tv, 0.0)
    dy = e * (1.0 / D)
    part = 0.5 * jnp.sum(jnp.sum(e * dy, axis=1, keepdims=True), axis=0, keepdims=True)
    dx, xh = _rms_bwd(dy, hv, r, gv)
    return dx, part, jnp.sum(dy * xh, axis=0, keepdims=True)


def _ffn_fwd_loss(h, g, w1, w3, w2, gf, tgt, name):
    L = h.shape[0]
    tm = _tile(L, 704)

    def body(h_ref, g_ref, w1_ref, w3_ref, w2_ref, gf_ref, t_ref, o_ref, a_ref, b_ref, loss_ref, dgf_ref,
             n_s, acc_s):
        i, j = pl.program_id(0), pl.program_id(1)

        @pl.when(j == 0)
        def _():
            hv = h_ref[...]
            n, _ = _rms(hv, g_ref[...])
            n_s[...] = n.astype(BF16)
            acc_s[...] = hv

        n = n_s[...]
        a = _dot_nt(n, w1_ref[0])
        b = _dot_nt(n, w3_ref[0])
        a_ref[0] = a.astype(BF16)
        b_ref[0] = b.astype(BF16)
        s = (a * _sigmoid(a) * b).astype(BF16)
        acc_s[...] += 0.5 * _dot(s, w2_ref[0])

        @pl.when(j == NSH - 1)
        def _():
            dx, part, dgf = _loss_head(acc_s[...], gf_ref[...], t_ref[...], i * tm)
            o_ref[...] = dx
            _acc_rows(loss_ref, part, i == 0)
            _acc_rows(dgf_ref, dgf, i == 0)

    row = pl.BlockSpec((tm, D), lambda i, j: (i, 0))
    hid = pl.BlockSpec((1, tm, FS), lambda i, j: (j, i, 0))
    wsp = pl.BlockSpec((1, FS, D), lambda i, j: (j, 0, 0))
    return pl.pallas_call(
        body, name=name, grid=(L // tm, NSH),
        out_shape=[jax.ShapeDtypeStruct((L, D), F32),
                   jax.ShapeDtypeStruct((NSH, L, FS), BF16), jax.ShapeDtypeStruct((NSH, L, FS), BF16),
                   jax.ShapeDtypeStruct((1, 1), F32), jax.ShapeDtypeStruct((1, D), F32)],
        in_specs=[row, _res((1, D)), wsp, wsp, wsp, _res((1, D)), row],
        out_specs=[row, hid, hid, pl.BlockSpec((1, 1), lambda i, j: (0, 0)),
                   pl.BlockSpec((1, D), lambda i, j: (0, 0))],
        scratch_shapes=[pltpu.VMEM((tm, D), BF16), pltpu.VMEM((tm, D), F32)],
        compiler_params=_params("arbitrary", "arbitrary"),
    )(h, g, w1, w3, w2, gf, tgt)


def _ffn_bwd(h, g, dout, a, b, w1, w3, w2, name, comm=None):
    L = h.shape[0]
    tm = _tile(L, 528)

    def body(h_ref, g_ref, do_ref, a_ref, b_ref, w1_ref, w3_ref, w2_ref,
             dh_ref, da_ref, db_ref, s_ref, n_ref, dg_ref, dob_s, dn_s):
        i, j = pl.program_id(0), pl.program_id(1)

        @pl.when(j == 0)
        def _():
            n, _ = _rms(h_ref[...], g_ref[...])
            n_ref[...] = n.astype(BF16)
            dob_s[...] = (0.5 * do_ref[...]).astype(BF16)
            dn_s[...] = jnp.zeros_like(dn_s)

        av = a_ref[0].astype(F32)
        bv = b_ref[0].astype(F32)
        sig = _sigmoid(av)
        sa = av * sig
        ds = _dot_nt(dob_s[...], w2_ref[0])
        s_ref[0] = (sa * bv).astype(BF16)
        da = (ds * bv * (sig + sa * (1.0 - sig))).astype(BF16)
        db = (ds * sa).astype(BF16)
        da_ref[0] = da
        db_ref[0] = db
        dn_s[...] += _dot(da, w1_ref[0]) + _dot(db, w3_ref[0])

        @pl.when(j == NSH - 1)
        def _():
            hv = h_ref[...]
            gv = g_ref[...]
            r = lax.rsqrt(jnp.mean(hv * hv, axis=-1, keepdims=True) + EPS)
            dn = dn_s[...]
            dx, xh = _rms_bwd(dn, hv, r, gv)
            dh_ref[...] = do_ref[...] + dx
            _acc_rows(dg_ref, jnp.sum(dn * xh, axis=0, keepdims=True), i == 0)

    row = pl.BlockSpec((tm, D), lambda i, j: (i, 0))
    hid = pl.BlockSpec((1, tm, FS), lambda i, j: (j, i, 0))
    wsp = pl.BlockSpec((1, FS, D), lambda i, j: (j, 0, 0))
    return _call(
        body, comm, name=name, grid=(L // tm, NSH),
        out_shape=[jax.ShapeDtypeStruct((L, D), F32)]
        + [jax.ShapeDtypeStruct((NSH, L, FS), BF16)] * 3
        + [jax.ShapeDtypeStruct((L, D), BF16), jax.ShapeDtypeStruct((1, D), F32)],
        in_specs=[row, _res((1, D)), row, hid, hid,
                  wsp, wsp, wsp],
        out_specs=[row, hid, hid, hid, row, pl.BlockSpec((1, D), lambda i, j: (0, 0))],
        scratch_shapes=[pltpu.VMEM((tm, D), BF16), pltpu.VMEM((tm, D), F32)],
        params=_params("arbitrary", "arbitrary"),
    )(h, g, dout, a, b, w1, w3, w2)


def _wgrad(xm, ym, name, scale=1.0):
    xs, ys = xm.ndim == 3, ym.ndim == 3
    assert not (xs and ys)
    L = xm.shape[-2]
    K, N = xm.shape[-1], ym.shape[-1]
    tl = _tile(L, 2112)
    nl = L // tl
    if xs or ys:
        tn, grid_n = N, NSH
    else:
        tn = _tile(N, 1024, 128)
        grid_n = N // tn

    def body(x_ref, y_ref, o_ref, acc_s):
        l = pl.program_id(1)
        xv = x_ref[0] if xs else x_ref[...]
        yv = y_ref[0] if ys else y_ref[...]
        part = _dot_tn(xv.astype(BF16), yv.astype(BF16))
        _acc_rows(acc_s, part, l == 0)

        @pl.when(l == nl - 1)
        def _():
            res = (acc_s[...] * scale).astype(BF16)
            if xs or ys:
                o_ref[0] = res
            else:
                o_ref[...] = res

    if xs:
        x_spec = pl.BlockSpec((1, tl, K), lambda n, l: (n, l, 0))
        y_spec = pl.BlockSpec((tl, N), lambda n, l: (l, 0))
        o_spec = pl.BlockSpec((1, K, N), lambda n, l: (n, 0, 0))
        o_shape = (NSH, K, N)
    elif ys:
        x_spec = pl.BlockSpec((tl, K), lambda n, l: (l, 0))
        y_spec = pl.BlockSpec((1, tl, N), lambda n, l: (n, l, 0))
        o_spec = pl.BlockSpec((1, K, N), lambda n, l: (n, 0, 0))
        o_shape = (NSH, K, N)
    else:
        x_spec = pl.BlockSpec((tl, K), lambda n, l: (l, 0))
        y_spec = pl.BlockSpec((tl, tn), lambda n, l: (l, n))
        o_spec = pl.BlockSpec((K, tn), lambda n, l: (0, n))
        o_shape = (K, N)
    return pl.pallas_call(
        body, name=name, grid=(grid_n, nl),
        out_shape=jax.ShapeDtypeStruct(o_shape, BF16),
        in_specs=[x_spec, y_spec], out_specs=o_spec,
        scratch_shapes=[pltpu.VMEM((K, tn), F32)],
        compiler_params=_params("parallel", "arbitrary"),
    )(xm, ym)


def _mix_in_fwd(h, g, w_in, b_gate, name, comm=None):
    L = h.shape[0]
    tm = _tile(L, 528)

    def body(h_ref, g_ref, w_ref, bg_ref, vg_ref, uf_ref, gt_ref):
        u, _ = _rms(h_ref[...], g_ref[...])
        ub = u.astype(BF16)
        p = [_dot(ub, w_ref[j]) for j in range(NSH)]
        a0, a1 = 2 * DC - WS, 2 * DC + DS - WS
        vg_ref[:, 0:WS] = p[0].astype(BF16)
        vg_ref[:, WS:2 * DC] = p[1][:, 0:a0].astype(BF16)
        uf_ref[...] = p[1][:, a0:a1].astype(BF16)
        gin = jnp.concatenate([p[1][:, a1:], p[2], p[3]], axis=1)
        gt_ref[...] = _sigmoid(gin + bg_ref[...]).astype(BF16)

    def row(n):
        return pl.BlockSpec((tm, n), lambda i: (i, 0))

    return _call(
        body, comm, name=name, grid=(L // tm,),
        out_shape=[jax.ShapeDtypeStruct((L, 2 * DC), BF16), jax.ShapeDtypeStruct((L, DS), BF16),
                   jax.ShapeDtypeStruct((L, 2 * D), BF16)],
        in_specs=[row(D), _res((1, D)), _res((NSH, D, WS)), _res((1, 2 * D))],
        out_specs=[row(2 * DC), row(DS), row(2 * D)],
        params=_params("parallel"),
    )(h, g, w_in, b_gate)


def _mix_in_bwd(h, g, dres, dv, dgl, duf, dgate, w_in, name):
    L = h.shape[0]
    tm = _tile(L, 528)

    def body(h_ref, g_ref, dr_ref, dv_ref, dgl_ref, duf_ref, dgt_ref, w_ref, dh_ref, u_ref, dp_ref, dgm_ref):
        i = pl.program_id(0)
        hv = h_ref[...]
        gv = g_ref[...]
        u, r = _rms(hv, gv)
        u_ref[...] = u.astype(BF16)
        a0, a1 = 2 * DC - WS, 2 * DC + DS - WS
        b0 = WS - a1
        dp = [jnp.concatenate([dv_ref[...], dgl_ref[:, 0:WS - DC]], axis=1),
              jnp.concatenate([dgl_ref[:, WS - DC:], duf_ref[...], dgt_ref[:, 0:b0]], axis=1),
              dgt_ref[:, b0:b0 + WS], dgt_ref[:, b0 + WS:]]
        du = jnp.zeros((tm, D), F32)
        for j in range(NSH):
            dp_ref[j] = dp[j]
            du = du + _dot_nt(dp[j], w_ref[j])
        dx, xh = _rms_bwd(du, hv, r, gv)
        dh_ref[...] = dr_ref[...] + dx
        _acc_rows(dgm_ref, jnp.sum(du * xh, axis=0, keepdims=True), i == 0)

    def row(n):
        return pl.BlockSpec((tm, n), lambda i: (i, 0))

    return pl.pallas_call(
        body, name=name, grid=(L // tm,),
        out_shape=[jax.ShapeDtypeStruct((L, D), F32), jax.ShapeDtypeStruct((L, D), BF16),
                   jax.ShapeDtypeStruct((NSH, L, WS), BF16), jax.ShapeDtypeStruct((1, D), F32)],
        in_specs=[row(D), _res((1, D)), row(D), row(DC), row(DC), row(DS), row(2 * D), _res((NSH, D, WS))],
        out_specs=[row(D), row(D), pl.BlockSpec((NSH, tm, WS), lambda i: (0, i, 0)),
                   pl.BlockSpec((1, D), lambda i: (0, 0))],
        compiler_params=_params("arbitrary"),
    )(h, g, dres, dv, dgl, duf, dgate, w_in)


def _conv_fwd(vg, dw, dwb, name, comm=None):
    L = vg.shape[0]
    nc = DC // 128

    def body(v_ref, g_ref, dw_ref, dwb_ref, z_ref, zp_s):
        zp_s[0:KWP, :] = jnp.zeros((KWP, 128), F32)
        zp_s[KWP:, :] = v_ref[...].astype(F32) * _sigmoid(g_ref[...].astype(F32))
        for r0 in range(0, L, CONV_ROWS):
            acc = jnp.broadcast_to(dwb_ref[...], (CONV_ROWS, 128))
            for k in range(KW):
                acc = acc + dw_ref[k:k + 1, :] * zp_s[pl.ds(r0 + k + 2, CONV_ROWS), :]
            z_ref[pl.ds(r0, CONV_ROWS), :] = acc

    return _call(
        body, comm, name=name, grid=(nc,),
        out_shape=[jax.ShapeDtypeStruct((L, DC), F32)],
        in_specs=[pl.BlockSpec((L, 128), lambda c: (0, c)), pl.BlockSpec((L, 128), lambda c: (0, nc + c)),
                  pl.BlockSpec((KWP, 128), lambda c: (0, c)), pl.BlockSpec((1, 128), lambda c: (0, c))],
        out_specs=[pl.BlockSpec((L, 128), lambda c: (0, c))],
        scratch_shapes=[pltpu.VMEM((L + KWP, 128), F32)],
        params=_params("parallel"),
    )(vg, vg, dw, dwb)


def _conv_bwd(dz1, vg, dw, name):
    L = vg.shape[0]
    nc = DC // 128

    def body(dz_ref, v_ref, g_ref, dw_ref, dv_ref, dg_ref, ddw_ref, ddwb_ref, zp_s, dzp_s):
        vv = v_ref[...].astype(F32)
        sg = _sigmoid(g_ref[...].astype(F32))
        zp_s[0:KWP, :] = jnp.zeros((KWP, 128), F32)
        zp_s[KWP:, :] = vv * sg
        dz = dz_ref[...]
        dzp_s[0:L, :] = dz
        dzp_s[L:, :] = jnp.zeros((KWP, 128), F32)
        ddwb_ref[...] = jnp.sum(dz, axis=0, keepdims=True)
        part = [jnp.zeros((8, 128), F32) for _ in range(KW)]
        for r0 in range(0, L, CONV_ROWS):
            rows = pl.ds(r0, CONV_ROWS)
            dzc = dz_ref[rows, :]
            acc = jnp.zeros((CONV_ROWS, 128), F32)
            for k in range(KW):
                acc = acc + dw_ref[k:k + 1, :] * dzp_s[pl.ds(r0 + KW - 1 - k, CONV_ROWS), :]
                prod = dzc * zp_s[pl.ds(r0 + k + 2, CONV_ROWS), :]
                for q in range(CONV_ROWS // 8):
                    part[k] = part[k] + prod[8 * q:8 * (q + 1), :]
            vc = v_ref[rows, :].astype(F32)
            sc = _sigmoid(g_ref[rows, :].astype(F32))
            dv_ref[rows, :] = (acc * sc).astype(BF16)
            dg_ref[rows, :] = (acc * vc * sc * (1.0 - sc)).astype(BF16)
        for k in range(KW):
            ddw_ref[k:k + 1, :] = jnp.sum(part[k], axis=0, keepdims=True)
        ddw_ref[KW:KWP, :] = jnp.zeros((KWP - KW, 128), F32)

    col = pl.BlockSpec((L, 128), lambda c: (0, c))
    return pl.pallas_call(
        body, name=name, grid=(nc,),
        out_shape=[jax.ShapeDtypeStruct((L, DC), BF16), jax.ShapeDtypeStruct((L, DC), BF16),
                   jax.ShapeDtypeStruct((KWP, DC), F32), jax.ShapeDtypeStruct((1, DC), F32)],
        in_specs=[col, col, pl.BlockSpec((L, 128), lambda c: (0, nc + c)),
                  pl.BlockSpec((KWP, 128), lambda c: (0, c))],
        out_specs=[col, col, pl.BlockSpec((KWP, 128), lambda c: (0, c)), pl.BlockSpec((1, 128), lambda c: (0, c))],
        scratch_shapes=[pltpu.VMEM((L + KWP, 128), F32), pltpu.VMEM((L + KWP, 128), F32)],
        compiler_params=_params("parallel"),
    )(dz1, vg, vg, dw)


NLB = QS // 128


def _lb_store(ref, rows, val):
    for cb in range(NLB):
        ref[cb, rows, :] = val[:, cb * 128:(cb + 1) * 128]


def _lb_load(ref, rows):
    return jnp.concatenate([ref[cb, rows, :] for cb in range(NLB)], axis=1)


def _scan(xr_ref, xi_ref, base, T, ar, ai, atr, ati, reverse):
    W = ar.shape[1]
    ar, ai, atr, ati = (jnp.broadcast_to(v, (8, W)) for v in (ar, ai, atr, ati))
    zero = jnp.zeros((8, W), F32)

    def rows(t, g):
        tt = T - 1 - t if reverse else t
        return pl.ds(base + g * 8 * T + tt, 8, stride=T)

    def make_step(store):
        def step(t, carry):
            out = []
            for g in range(NGRP):
                sr, si = carry[2 * g], carry[2 * g + 1]
                idx = rows(t, g)
                nr = ar * sr - ai * si + _lb_load(xr_ref, idx)
                ni = ar * si + ai * sr + _lb_load(xi_ref, idx)
                if store:
                    _lb_store(xr_ref, idx, nr)
                    _lb_store(xi_ref, idx, ni)
                out += [nr, ni]
            return tuple(out)
        return step

    ends = lax.fori_loop(0, T, make_step(False), (zero,) * (2 * NGRP))
    sub = lax.broadcasted_iota(jnp.int32, (8, W), 0)
    edge = sub == (7 if reverse else 0)
    shift, last = (7, 0) if reverse else (1, 7)
    inr, ini = jnp.zeros((1, W), F32), jnp.zeros((1, W), F32)
    starts = [None] * (2 * NGRP)
    for g in (reversed(range(NGRP)) if reverse else range(NGRP)):
        er, ei = ends[2 * g], ends[2 * g + 1]
        cr, ci = jnp.where(edge, inr, 0.0), jnp.where(edge, ini, 0.0)
        for _ in range(7):
            nr = atr * cr - ati * ci + er
            ni = atr * ci + ati * cr + ei
            cr = jnp.where(edge, inr, pltpu.roll(nr, shift, 0))
            ci = jnp.where(edge, ini, pltpu.roll(ni, shift, 0))
        starts[2 * g], starts[2 * g + 1] = cr, ci
        inr = (atr * cr - ati * ci + er)[last:last + 1]
        ini = (atr * ci + ati * cr + ei)[last:last + 1]
    lax.fori_loop(0, T, make_step(True), tuple(starts))


def _ssm_fwd(uf, bre, bim, cre, cim, lamp, dsk, name, comm=None):
    L = uf.shape[0]
    T = L // NSEG
    tc = L // NCH

    def body(u_ref, bre_ref, bim_ref, cre_ref, cim_ref, lam_ref, d_ref, y_ref, sr_s, si_s):
        for k in range(NCH):
            sl = slice(k * tc, (k + 1) * tc)
            uk = u_ref[sl, :]
            _lb_store(sr_s, sl, _dot(uk, bre_ref[0]))
            _lb_store(si_s, sl, _dot(uk, bim_ref[0]))
        _scan(sr_s, si_s, 0, T, lam_ref[0:1, :], lam_ref[1:2, :], lam_ref[2:3, :], lam_ref[3:4, :], False)
        for k in range(NCH):
            sl = slice(k * tc, (k + 1) * tc)
            y_ref[sl, :] = (_dot(_lb_load(sr_s, sl).astype(BF16), cre_ref[0])
                            - _dot(_lb_load(si_s, sl).astype(BF16), cim_ref[0])
                            + d_ref[...] * u_ref[sl, :].astype(F32))

    return _call(
        body, comm, name=name, grid=(NQ,),
        out_shape=[jax.ShapeDtypeStruct((L, DS), F32)],
        in_specs=[pl.BlockSpec((L, QU), lambda q: (0, q)),
                  pl.BlockSpec((1, QU, QS), lambda q: (q, 0, 0)), pl.BlockSpec((1, QU, QS), lambda q: (q, 0, 0)),
                  pl.BlockSpec((1, QS, QU), lambda q: (q, 0, 0)), pl.BlockSpec((1, QS, QU), lambda q: (q, 0, 0)),
                  pl.BlockSpec((8, QS), lambda q: (0, q)), pl.BlockSpec((1, QU), lambda q: (0, q))],
        out_specs=[pl.BlockSpec((L, QU), lambda q: (0, q))],
        scratch_shapes=[pltpu.VMEM((NLB, L, 128), F32), pltpu.VMEM((NLB, L, 128), F32)],
        params=_params("parallel"),
    )(uf, bre, bim, cre, cim, lamp, dsk)


def _ssm_bwd(uf, dyss, bre, bim, cre, cim, lamp, dsk, name, comm=None):
    L = uf.shape[0]
    T = L // NSEG
    tc = L // NCH

    def body(u_ref, dy_ref, bre_ref, bim_ref, cre_ref, cim_ref, lam_ref, d_ref,
             du_ref, dbre_ref, dbim_ref, dcre_ref, dcim_ref, dlam_ref, dd_ref, sr_s, si_s, gr_s, gi_s):
        _lb_store(sr_s, slice(0, SOFF), jnp.zeros((SOFF, QS), F32))
        _lb_store(si_s, slice(0, SOFF), jnp.zeros((SOFF, QS), F32))
        for k in range(NCH):
            sl = slice(k * tc, (k + 1) * tc)
            ss = slice(SOFF + k * tc, SOFF + (k + 1) * tc)
            uk = u_ref[sl, :]
            dyk = dy_ref[sl, :].astype(BF16)
            _lb_store(sr_s, ss, _dot(uk, bre_ref[0]))
            _lb_store(si_s, ss, _dot(uk, bim_ref[0]))
            _lb_store(gr_s, sl, _dot_nt(dyk, cre_ref[0]))
            _lb_store(gi_s, sl, -_dot_nt(dyk, cim_ref[0]))
        ar, ai, atr, ati = lam_ref[0:1, :], lam_ref[1:2, :], lam_ref[2:3, :], lam_ref[3:4, :]
        _scan(sr_s, si_s, SOFF, T, ar, ai, atr, ati, False)
        _scan(gr_s, gi_s, 0, T, ar, -ai, atr, -ati, True)
        dbre = jnp.zeros((QU, QS), F32)
        dbim = jnp.zeros((QU, QS), F32)
        dcre = jnp.zeros((QS, QU), F32)
        dcim = jnp.zeros((QS, QU), F32)
        dd = jnp.zeros((1, QU), F32)
        qr = jnp.zeros((1, QS), F32)
        qi = jnp.zeros((1, QS), F32)
        for k in range(NCH):
            sl = slice(k * tc, (k + 1) * tc)
            ss = slice(SOFF + k * tc, SOFF + (k + 1) * tc)
            sp = slice(SOFF - 1 + k * tc, SOFF - 1 + (k + 1) * tc)
            uk = u_ref[sl, :]
            dyk = dy_ref[sl, :]
            dyb = dyk.astype(BF16)
            gr, gi = _lb_load(gr_s, sl), _lb_load(gi_s, sl)
            pr, pi = _lb_load(sr_s, sp), _lb_load(si_s, sp)
            qr = qr + jnp.sum(gr * pr + gi * pi, axis=0, keepdims=True)
            qi = qi + jnp.sum(gi * pr - gr * pi, axis=0, keepdims=True)
            grb, gib = gr.astype(BF16), gi.astype(BF16)
            du_ref[sl, :] = (_dot_nt(grb, bre_ref[0]) + _dot_nt(gib, bim_ref[0])
                             + dyk * d_ref[...]).astype(BF16)
            dbre = dbre + _dot_tn(uk, grb)
            dbim = dbim + _dot_tn(uk, gib)
            dcre = dcre + _dot_tn(_lb_load(sr_s, ss).astype(BF16), dyb)
            dcim = dcim - _dot_tn(_lb_load(si_s, ss).astype(BF16), dyb)
            dd = dd + jnp.sum(dyk * uk.astype(F32), axis=0, keepdims=True)
        dlam_ref[0] = jnp.concatenate([qr, qi, jnp.zeros((6, QS), F32)], axis=0)
        dbre_ref[0] = dbre
        dbim_ref[0] = dbim
        dcre_ref[0] = dcre
        dcim_ref[0] = dcim
        dd_ref[...] = dd

    col = pl.BlockSpec((L, QU), lambda q: (0, q))
    bsp = pl.BlockSpec((1, QU, QS), lambda q: (q, 0, 0))
    csp = pl.BlockSpec((1, QS, QU), lambda q: (q, 0, 0))
    return _call(
        body, comm, name=name, grid=(NQ,),
        out_shape=[jax.ShapeDtypeStruct((L, DS), BF16),
                   jax.ShapeDtypeStruct((NQ, QU, QS), F32), jax.ShapeDtypeStruct((NQ, QU, QS), F32),
                   jax.ShapeDtypeStruct((NQ, QS, QU), F32), jax.ShapeDtypeStruct((NQ, QS, QU), F32),
                   jax.ShapeDtypeStruct((NQ, 8, QS), F32), jax.ShapeDtypeStruct((1, DS), F32)],
        in_specs=[col, col, bsp, bsp, csp, csp,
                  pl.BlockSpec((8, QS), lambda q: (0, q)), pl.BlockSpec((1, QU), lambda q: (0, q))],
        out_specs=[col,
                   pl.BlockSpec((1, QU, QS), lambda q: (q, 0, 0)), pl.BlockSpec((1, QU, QS), lambda q: (q, 0, 0)),
                   pl.BlockSpec((1, QS, QU), lambda q: (q, 0, 0)), pl.BlockSpec((1, QS, QU), lambda q: (q, 0, 0)),
                   pl.BlockSpec((1, 8, QS), lambda q: (q, 0, 0)), pl.BlockSpec((1, QU), lambda q: (0, q))],
        scratch_shapes=[pltpu.VMEM((NLB, L + SOFF, 128), F32), pltpu.VMEM((NLB, L + SOFF, 128), F32),
                        pltpu.VMEM((NLB, L, 128), F32), pltpu.VMEM((NLB, L, 128), F32)],
        params=_params("parallel"),
    )(uf, dyss, bre, bim, cre, cim, lamp, dsk)


def _branches(z1_ref, yss_ref, gt_ref, lng_ref, lnb_ref, wp_ref, wv_ref, wg_ref):
    zf = z1_ref[...]
    mu = jnp.mean(zf, axis=-1, keepdims=True)
    zc = zf - mu
    rstd = lax.rsqrt(jnp.mean(zc * zc, axis=-1, keepdims=True) + EPS)
    zn = zc * rstd
    z2 = zn * lng_ref[...] + lnb_ref[...]
    sz = _sigmoid(z2)
    z3 = (z2 * sz).astype(BF16)
    y_conv = _dot(z3, wp_ref[...])
    yss = yss_ref[...]
    yg = _gelu(yss).astype(BF16)
    sv = _dot(yg, wv_ref[...])
    sig = _sigmoid(_dot(yg, wg_ref[...]))
    y_ssm = sv * sig
    gc = gt_ref[:, 0:D].astype(F32)
    gs = gt_ref[:, D:2 * D].astype(F32)
    m = gc * y_conv + gs * y_ssm
    return dict(rstd=rstd, zn=zn, z2=z2, sz=sz, z3=z3, y_conv=y_conv, yss=yss, yg=yg, sv=sv, sig=sig,
                y_ssm=y_ssm, gc=gc, gs=gs, m=m)


def _merge_fwd(h, z1, yss, gate, lng, lnb, wp, wv, wg, wo, name, comm=None):
    L = h.shape[0]
    tm = _tile(L, 528)

    def body(h_ref, z1_ref, yss_ref, gt_ref, lng_ref, lnb_ref, wp_ref, wv_ref, wg_ref, wo_ref, o_ref):
        f = _branches(z1_ref, yss_ref, gt_ref, lng_ref, lnb_ref, wp_ref, wv_ref, wg_ref)
        o_ref[...] = h_ref[...] + _dot(f["m"].astype(BF16), wo_ref[...])

    def row(n):
        return pl.BlockSpec((tm, n), lambda i: (i, 0))

    return _call(
        body, comm, name=name, grid=(L // tm,),
        out_shape=[jax.ShapeDtypeStruct((L, D), F32)],
        in_specs=[row(D), row(DC), row(DS), row(2 * D), _res((1, DC)), _res((1, DC)),
                  _res((DC, D)), _res((DS, D)), _res((DS, D)), _res((D, D))],
        out_specs=[row(D)],
        params=_params("parallel"),
    )(h, z1, yss, gate, lng, lnb, wp, wv, wg, wo)


def _merge_bwd(dh, z1, yss, gate, lng, lnb, wp, wv, wg, wo, name):
    L = dh.shape[0]
    tm = _tile(L, 352)

    def body(dh_ref, z1_ref, yss_ref, gt_ref, lng_ref, lnb_ref, wp_ref, wv_ref, wg_ref, wo_ref,
             m_ref, dgt_ref, dyc_ref, z3_ref, dz1_ref, yg_ref, dsv_ref, dsg_ref, dyss_ref,
             dbg_ref, dlng_ref, dlnb_ref):
        i = pl.program_id(0)
        f = _branches(z1_ref, yss_ref, gt_ref, lng_ref, lnb_ref, wp_ref, wv_ref, wg_ref)
        gc, gs, sig, sv = f["gc"], f["gs"], f["sig"], f["sv"]
        m_ref[...] = f["m"].astype(BF16)
        z3_ref[...] = f["z3"]
        yg_ref[...] = f["yg"]
        dm = _dot_nt(dh_ref[...].astype(BF16), wo_ref[...])
        dgc = (dm * f["y_conv"] * gc * (1.0 - gc)).astype(BF16)
        dgs = (dm * f["y_ssm"] * gs * (1.0 - gs)).astype(BF16)
        dgt_ref[:, 0:D] = dgc
        dgt_ref[:, D:2 * D] = dgs
        part = jnp.concatenate([jnp.sum(dgc.astype(F32), axis=0, keepdims=True),
                                jnp.sum(dgs.astype(F32), axis=0, keepdims=True)], axis=1)
        _acc_rows(dbg_ref, part, i == 0)
        dyc = (dm * gc).astype(BF16)
        dyc_ref[...] = dyc
        dys = dm * gs
        dsv = (dys * sig).astype(BF16)
        dsg = (dys * sv * sig * (1.0 - sig)).astype(BF16)
        dsv_ref[...] = dsv
        dsg_ref[...] = dsg
        dyg = _dot_nt(dsv, wv_ref[...]) + _dot_nt(dsg, wg_ref[...])
        dyss_ref[...] = dyg * _gelu_grad(f["yss"])
        dz3 = _dot_nt(dyc, wp_ref[...])
        z2, sz, zn = f["z2"], f["sz"], f["zn"]
        dz2 = dz3 * sz * (1.0 + z2 * (1.0 - sz))
        _acc_rows(dlng_ref, jnp.sum(dz2 * zn, axis=0, keepdims=True), i == 0)
        _acc_rows(dlnb_ref, jnp.sum(dz2, axis=0, keepdims=True), i == 0)
        dzn = dz2 * lng_ref[...]
        dz1_ref[...] = f["rstd"] * (dzn - jnp.mean(dzn, axis=-1, keepdims=True)
                                    - zn * jnp.mean(dzn * zn, axis=-1, keepdims=True))

    def row(n):
        return pl.BlockSpec((tm, n), lambda i: (i, 0))

    def tot(n):
        return pl.BlockSpec((1, n), lambda i: (0, 0))

    return pl.pallas_call(
        body, name=name, grid=(L // tm,),
        out_shape=[jax.ShapeDtypeStruct((L, D), BF16), jax.ShapeDtypeStruct((L, 2 * D), BF16),
                   jax.ShapeDtypeStruct((L, D), BF16), jax.ShapeDtypeStruct((L, DC), BF16),
                   jax.ShapeDtypeStruct((L, DC), F32), jax.ShapeDtypeStruct((L, DS), BF16),
                   jax.ShapeDtypeStruct((L, D), BF16), jax.ShapeDtypeStruct((L, D), BF16),
                   jax.ShapeDtypeStruct((L, DS), F32),
                   jax.ShapeDtypeStruct((1, 2 * D), F32), jax.ShapeDtypeStruct((1, DC), F32),
                   jax.ShapeDtypeStruct((1, DC), F32)],
        in_specs=[row(D), row(DC), row(DS), row(2 * D), _res((1, DC)), _res((1, DC)),
                  _res((DC, D)), _res((DS, D)), _res((DS, D)), _res((D, D))],
        out_specs=[row(D), row(2 * D), row(D), row(DC), row(DC), row(DS), row(D), row(D), row(DS),
                   tot(2 * D), tot(DC), tot(DC)],
        compiler_params=_params("arbitrary"),
    )(dh, z1, yss, gate, lng, lnb, wp, wv, wg, wo)


def _ssm_disc(lam_re, lam_im, log_dt, b_re, b_im):
    lam = lax.complex(lam_re, lam_im)
    dt = jnp.exp(log_dt)[:, None]
    lam_bar = jnp.exp(lam * dt)
    bbar = ((lam_bar - 1.0) / lam)[..., None] * lax.complex(b_re, b_im)
    return jnp.real(lam_bar), jnp.imag(lam_bar), jnp.real(bbar), jnp.imag(bbar)


def _bdiag_in(m):
    m4 = m.reshape(NQ, G // NQ, P, H)
    return jnp.einsum("qgph,gk->qghkp", m4, jnp.eye(G // NQ, dtype=m.dtype)).reshape(NQ, QU, QS)


def _bdiag_out(m):
    m4 = m.reshape(NQ, G // NQ, H, P)
    return jnp.einsum("qghp,gk->qgpkh", m4, jnp.eye(G // NQ, dtype=m.dtype)).reshape(NQ, QS, QU)


def _diag_blocks(m4):
    return jnp.einsum("qiaib->qiab", m4).reshape(G, m4.shape[2], m4.shape[4])


def _pack(parts, rows_mult=8):
    flat = jnp.concatenate([p.reshape(-1).astype(F32) for p in parts])
    n = flat.shape[0]
    tot = -(-n // (128 * rows_mult)) * (128 * rows_mult)
    return jnp.pad(flat, (0, tot - n)).reshape(tot // 128, 128)


def _unpack(buf, shapes):
    flat = buf.reshape(-1)
    out, o = [], 0
    for s in shapes:
        n = math.prod(s)
        out.append(flat[o:o + n].reshape(s))
        o += n
    return out


def kernel(x, meta_tokens, ffn1_norm, ffn1_w1, ffn1_w3, ffn1_w2, mix_norm, w_in, b_gate, conv_dw, conv_dw_b, conv_ln_g, conv_ln_b, conv_proj, ssm_lam_re, ssm_lam_im, ssm_log_dt, ssm_b_re, ssm_b_im, ssm_c_re, ssm_c_im, ssm_d, ssm_w_v, ssm_w_g, w_out, ffn2_norm, ffn2_w1, ffn2_w3, ffn2_w2, final_norm, loss_target, m_meta_tokens, m_ffn1_norm, m_ffn1_w1, m_ffn1_w3, m_ffn1_w2, m_mix_norm, m_w_in, m_b_gate, m_conv_dw, m_conv_dw_b, m_conv_ln_g, m_conv_ln_b, m_conv_proj, m_ssm_lam_re, m_ssm_lam_im, m_ssm_log_dt, m_ssm_b_re, m_ssm_b_im, m_ssm_c_re, m_ssm_c_im, m_ssm_d, m_ssm_w_v, m_ssm_w_g, m_w_out, m_ffn2_norm, m_ffn2_w1, m_ffn2_w3, m_ffn2_w2, m_final_norm, v_meta_tokens, v_ffn1_norm, v_ffn1_w1, v_ffn1_w3, v_ffn1_w2, v_mix_norm, v_w_in, v_b_gate, v_conv_dw, v_conv_dw_b, v_conv_ln_g, v_conv_ln_b, v_conv_proj, v_ssm_lam_re, v_ssm_lam_im, v_ssm_log_dt, v_ssm_b_re, v_ssm_b_im, v_ssm_c_re, v_ssm_c_im, v_ssm_d, v_ssm_w_v, v_ssm_w_g, v_w_out, v_ffn2_norm, v_ffn2_w1, v_ffn2_w3, v_ffn2_w2, v_final_norm):
    args = dict(locals())
    names = ["meta_tokens", "ffn1_norm", "ffn1_w1", "ffn1_w3", "ffn1_w2", "mix_norm", "w_in", "b_gate",
             "conv_dw", "conv_dw_b", "conv_ln_g", "conv_ln_b", "conv_proj", "ssm_lam_re", "ssm_lam_im",
             "ssm_log_dt", "ssm_b_re", "ssm_b_im", "ssm_c_re", "ssm_c_im", "ssm_d", "ssm_w_v", "ssm_w_g",
             "w_out", "ffn2_norm", "ffn2_w1", "ffn2_w3", "ffn2_w2", "final_norm"]
    big = ["ffn1_w1", "ffn1_w3", "ffn1_w2", "w_in", "conv_proj", "ssm_w_v", "ssm_w_g", "w_out",
           "ffn2_w1", "ffn2_w3", "ffn2_w2"]
    small = [n for n in names if n not in big]

    xs = x[0]
    S = xs.shape[0]
    L = FRONT + S
    T = L // NSEG
    jx, jy = lax.axis_index("x"), lax.axis_index("y")
    chip = 2 * jx + jy

    small_all = _gather_all(_pack([meta_tokens, conv_dw[0]]), "gather_small")
    sm = small_all[0::2].reshape(NSH, -1)
    nmt = NMETA * (D // NSH)
    ndw = KW * (DC // NSH)
    meta_full = sm[:, :nmt].reshape(NSH, NMETA, D // NSH).transpose(1, 0, 2).reshape(NMETA, D)
    dw_full = sm[:, nmt:nmt + ndw].reshape(NSH, KW, DC // NSH).transpose(1, 0, 2).reshape(KW, DC)
    dw_pad = jnp.pad(dw_full, ((0, KWP - KW), (0, 0)))
    tposed = ("ffn1_w1", "ffn1_w3", "ffn2_w1", "ffn2_w3")

    def view(a, n):
        return jnp.swapaxes(a, 1, 2) if n in tposed else a

    grp_a = ["ffn1_w1", "ffn1_w3", "ffn1_w2"]
    grp_b = ["w_in", "conv_proj", "ssm_w_v", "ssm_w_g", "w_out"]
    grp_c = ["ffn2_w1", "ffn2_w3", "ffn2_w2"]

    shards = {n: view(args[n], n)[0].astype(BF16) for n in big}

    def shard(n):
        return shards[n]

    sh_a = [shard(n) for n in grp_a]
    ga_send, ga_recv, sh_a, land_a, _ = _chips_start(
        sh_a, [jax.ShapeDtypeStruct((NSH,) + s.shape, s.dtype) for s in sh_a], True, "gather_ffn1_start",
        [small_all])

    def cols(w):
        return w.transpose(1, 0, 2).reshape(w.shape[1], -1)

    disc_in = (ssm_lam_re[0], ssm_lam_im[0], ssm_log_dt[0], ssm_b_re[0], ssm_b_im[0])
    (lbr, lbi, bbr, bbi), disc_vjp = jax.vjp(_ssm_disc, *disc_in)
    lam_t = jnp.exp(lax.complex(ssm_lam_re[0], ssm_lam_im[0]) * (jnp.exp(ssm_log_dt[0])[:, None] * T))
    lamp = jnp.concatenate([lbr.reshape(1, NST), lbi.reshape(1, NST), jnp.real(lam_t).reshape(1, NST),
                            jnp.imag(lam_t).reshape(1, NST), jnp.zeros((4, NST), F32)], axis=0)
    bre_bd, bim_bd = _bdiag_in(bbr).astype(BF16), _bdiag_in(bbi).astype(BF16)
    cre_bd, cim_bd = _bdiag_out(ssm_c_re[0]).astype(BF16), _bdiag_out(ssm_c_im[0]).astype(BF16)

    h0 = lax.dynamic_update_slice(jnp.pad(xs, ((FRONT, 0), (0, 0))), meta_full, (FRONT - NMETA, 0))
    tgt = jnp.pad(loss_target[0], ((FRONT, 0), (0, 0)))
    small_wmv = [_pack([args[p + n] for n in small])[None] for p in ("", "m_", "v_")]
    early_work = [h0, tgt, bre_bd, bim_bd, cre_bd, cim_bd] + [shards[n] for n in grp_b + grp_c] + small_wmv
    sh_a, land_a = _chips_wait(ga_send, ga_recv, sh_a, land_a, early_work, True, "gather_ffn1_wait")
    gw = dict(zip(grp_a, _pass_halves(land_a, "pass_ffn1", sh_a)))
    (h1, a1, b1), got = _ffn_fwd(h0, ffn1_norm, gw["ffn1_w1"], gw["ffn1_w3"], gw["ffn1_w2"], "ffn1_fwd",
                                 _gather_half_behind([shard(n) for n in grp_b]))
    w_in_f = _pass_halves(got[:1], "pass_w_in")[0]
    (vg, uf, gate), got1 = _mix_in_fwd(h1, mix_norm, w_in_f, b_gate, "mix_in_fwd",
                                       _join(_gather_half_behind([shard("ffn2_w1")]),
                                             _pass_halves_behind(list(got[1:]))))
    gw.update(zip(grp_b[1:], got1[1:]))
    wp_f, wv_f, wg_f = cols(gw["conv_proj"]), cols(gw["ssm_w_v"]), cols(gw["ssm_w_g"])
    wo_f = gw["w_out"].reshape(D, D)
    (z1,), got3 = _conv_fwd(vg, dw_pad, conv_dw_b, "conv_fwd", _gather_half_behind([shard("ffn2_w3")]))
    (yss,), got2 = _ssm_fwd(uf, bre_bd, bim_bd, cre_bd, cim_bd, lamp, ssm_d, "ssm_fwd",
                            _gather_half_behind([shard("ffn2_w2")]))
    (h2,), got_c = _merge_fwd(h1, z1, yss, gate, conv_ln_g, conv_ln_b, wp_f, wv_f, wg_f, wo_f, "merge_fwd",
                              _pass_halves_behind([got1[0], got3[0], got2[0]]))
    gw.update(zip(grp_c, got_c))
    dh3, a2, b2, loss_part, d_final = _ffn_fwd_loss(
        h2, ffn2_norm, gw["ffn2_w1"], gw["ffn2_w3"], gw["ffn2_w2"], final_norm.reshape(1, D), tgt, "ffn2_fwd_loss")

    gbig = {}
    core = lax.axis_index("c").astype(jnp.int32).reshape(1)

    def pair_sums(group, tag):
        gl = [gbig[n] for n in group]
        sib = _pair_exchange(gl, "pair_exchange_" + tag)
        out = []
        for n, g_, s_ in zip(group, gl, sib):
            out.append(_add_pair(g_, s_, core, "pair_" + n))
        return out

    (dh2, da2, db2, s2, n2, d_ffn2_norm), _ = _ffn_bwd(
        h2, ffn2_norm, dh3, a2, b2, gw["ffn2_w1"], gw["ffn2_w3"], gw["ffn2_w2"], "ffn2_bwd")
    gbig["ffn2_w1"] = _wgrad(da2, n2, "ffn2_dw1")
    gbig["ffn2_w3"] = _wgrad(db2, n2, "ffn2_dw3")
    gbig["ffn2_w2"] = _wgrad(s2, dh3, "ffn2_dw2", 0.5)
    pair_c = pair_sums(grp_c, "ffn2")
    (m_b, dgate, dyc, z3, dz1, yg, dsv, dsg, dyss, d_b_gate, d_ln_g, d_ln_b) = _merge_bwd(
        dh2, z1, yss, gate, conv_ln_g, conv_ln_b, wp_f, wv_f, wg_f, wo_f, "merge_bwd")
    gbig["w_out"] = _wgrad(m_b, dh2, "dw_out").reshape(NSH, D // NSH, D)

    def shard_cols(gm):
        return gm.reshape(gm.shape[0], NSH, -1).transpose(1, 0, 2)

    gbig["conv_proj"] = shard_cols(_wgrad(z3, dyc, "dw_proj"))
    gbig["ssm_w_v"] = shard_cols(_wgrad(yg, dsv, "dw_v"))
    gbig["ssm_w_g"] = shard_cols(_wgrad(yg, dsg, "dw_g"))
    dv, dgl, ddw, d_dw_b = _conv_bwd(dz1, vg, dw_pad, "conv_bwd")
    (duf, dbre, dbim, dcre, dcim, dlam, d_ssm_d), recv_c = _ssm_bwd(
        uf, dyss, bre_bd, bim_bd, cre_bd, cim_bd, lamp, ssm_d, "ssm_bwd", _scatter_chips_behind(pair_c))
    dh1, u_b, dproj, d_mix_norm = _mix_in_bwd(h1, mix_norm, dh2, dv, dgl, duf, dgate, w_in_f, "mix_in_bwd")
    gbig["w_in"] = _wgrad(u_b, dproj, "dw_in")
    pair_b = pair_sums(grp_b, "mix")

    d_bbr = _diag_blocks(dbre.reshape(NQ, 8, H, 8, P)).transpose(0, 2, 1)
    d_bbi = _diag_blocks(dbim.reshape(NQ, 8, H, 8, P)).transpose(0, 2, 1)
    d_c_re = _diag_blocks(dcre.reshape(NQ, 8, P, 8, H)).transpose(0, 2, 1)
    d_c_im = _diag_blocks(dcim.reshape(NQ, 8, P, 8, H)).transpose(0, 2, 1)
    d_lbr = dlam[:, 0, :].reshape(G, P)
    d_lbi = dlam[:, 1, :].reshape(G, P)
    d_lam_re, d_lam_im, d_log_dt, d_b_re, d_b_im = disc_vjp((d_lbr, d_lbi, d_bbr, d_bbi))

    sg = {"mix_norm": d_mix_norm, "b_gate": d_b_gate, "conv_dw": ddw[:KW], "conv_dw_b": d_dw_b,
          "conv_ln_g": d_ln_g, "conv_ln_b": d_ln_b, "ssm_lam_re": d_lam_re, "ssm_lam_im": d_lam_im,
          "ssm_log_dt": d_log_dt, "ssm_b_re": d_b_re, "ssm_b_im": d_b_im, "ssm_c_re": d_c_re, "ssm_c_im": d_c_im,
          "ssm_d": d_ssm_d, "ffn2_norm": d_ffn2_norm, "final_norm": d_final}
    late = ["meta_tokens", "ffn1_norm"]
    early = [n for n in small if n not in late]

    (dh0, da1, db1, s1, n1, d_ffn1_norm), got = _ffn_bwd(
        h0, ffn1_norm, dh1, a1, b1, gw["ffn1_w1"], gw["ffn1_w3"], gw["ffn1_w2"], "ffn1_bwd",
        _join(_scatter_chips_behind(pair_b), _gather_all_behind(_pack([sg[n] for n in early]))))
    recv_b, early_all = got[:len(grp_b)], got[len(grp_b)]
    gbig["ffn1_w1"] = _wgrad(da1, n1, "ffn1_dw1")
    gbig["ffn1_w3"] = _wgrad(db1, n1, "ffn1_dw3")
    gbig["ffn1_w2"] = _wgrad(s1, dh1, "ffn1_dw2", 0.5)
    grad_x = dh0[FRONT:][None]
    sg["meta_tokens"] = dh0[FRONT - NMETA:FRONT]
    sg["ffn1_norm"] = d_ffn1_norm

    pair_a = pair_sums(grp_a, "ffn1")
    late_all = _gather_all(_pack([sg[n] for n in late]), "gather_late_grads")
    sa_send, sa_recv, pair_a, land_s, sa_token = _chips_start(
        pair_a, [jax.ShapeDtypeStruct(p.shape, p.dtype) for p in pair_a], False, "scatter_ffn1_start", [late_all])

    out_g, out_d, out_m, out_v = {}, {}, {}, {}

    def finish(group, recvs, tag, after=None):
        halves = [_sum_slots(r, "sum_" + n, after) for n, r in zip(group, recvs)]
        for n, f in zip(group, _swap_halves(halves, "swap_" + tag)):
            g3 = f.reshape(1, f.shape[0] * f.shape[1], f.shape[2])
            g3, d3, m3, v3 = _adamw(view(args[n], n), g3, view(args["m_" + n], n), view(args["v_" + n], n),
                                "adamw_" + n)
            out_g[n], out_d[n], out_m[n], out_v[n] = (view(t, n) for t in (g3, d3, m3, v3))
            done.append(d3)

    done = []
    finish(grp_b + grp_c, list(recv_b) + list(recv_c), "mix_ffn2", sa_token)

    sgr = dict(zip(early, _unpack(_sum_slots(early_all, "sum_early", sa_token), [sg[n].shape for n in early])))
    sgr.update(zip(late, _unpack(_sum_slots(late_all, "sum_late"), [sg[n].shape for n in late])))
    sgr["meta_tokens"] = lax.dynamic_slice_in_dim(sgr["meta_tokens"], chip * (D // NSH), D // NSH, axis=1)
    sgr["conv_dw"] = lax.dynamic_slice_in_dim(sgr["conv_dw"], chip * (DC // NSH), DC // NSH, axis=1)
    pshapes = [args[n].shape for n in small]
    _, d_s, m_s, v_s = _adamw(small_wmv[0], _pack([sgr[n] for n in small])[None], small_wmv[1], small_wmv[2],
                              "adamw_small")
    for n, g_, d_, m_, v_ in zip(small, [sgr[n] for n in small], _unpack(d_s[0], pshapes),
                                 _unpack(m_s[0], pshapes), _unpack(v_s[0], pshapes)):
        out_g[n], out_d[n], out_m[n], out_v[n] = g_.reshape(args[n].shape), d_, m_, v_

    loss = lax.psum(loss_part[0, 0], ("x", "y", "c"))
    pair_a, recv_a = _chips_wait(sa_send, sa_recv, pair_a, land_s, [d_s, grad_x] + done, False,
                                 "scatter_ffn1_wait")
    finish(grp_a, _fill_own(pair_a, recv_a, "own_ffn1"), "ffn1")
    return (loss, grad_x, *[out_g[n] for n in names], *[out_d[n] for n in names],
            *[out_m[n] for n in names], *[out_v[n] for n in names])
```

```python
import math

import jax
import jax.numpy as jnp
from jax import lax
from jax.experimental import pallas as pl
from jax.experimental.pallas import tpu as pltpu

F32 = jnp.float32
BF16 = jnp.bfloat16

D = 1024
NSH = 4
F = 2816
FS = F // NSH
DC = 512
DS = 512
DIN = 2 * DC + DS + 2 * D
WS = DIN // NSH
KW = 31
KWP = 32
CONV_ROWS = 64
NMETA = 16
FRONT = 128
G, P, H = 32, 64, 16
NST = G * P
NQ = 4
QS = NST // NQ
QU = DS // NQ
NSEG = 32
NGRP = NSEG // 8
NCH = 8
SOFF = 8
EPS = 1e-6
LR, B1, B2, AEPS, WD, STEP = 1e-3, 0.9, 0.999, 1e-8, 0.01, 10
VMEM_LIMIT = 58 * 1024 * 1024
MESH = pl.DeviceIdType.MESH
ANY = pl.BlockSpec(memory_space=pl.ANY)


def _params(*sem):
    return pltpu.CompilerParams(dimension_semantics=sem, vmem_limit_bytes=VMEM_LIMIT)


def _res(shape):
    nd = len(shape)
    return pl.BlockSpec(shape, lambda *_: (0,) * nd, pipeline_mode=pl.Buffered(1))


def _tile(n, cap, mult=16):
    best = None
    for t in range(mult, min(n, cap) + 1, mult):
        if n % t == 0:
            best = t
    assert best is not None, (n, cap, mult)
    return best


def _dot(a, b):
    return jnp.dot(a, b, preferred_element_type=F32)


def _dot_nt(a, b):
    return lax.dot_general(a, b, (((1,), (1,)), ((), ())), preferred_element_type=F32)


def _dot_tn(a, b):
    return lax.dot_general(a, b, (((0,), (0,)), ((), ())), preferred_element_type=F32)


def _sigmoid(x):
    return 1.0 / (1.0 + jnp.exp(-x))


_GC = math.sqrt(2.0 / math.pi)
_GA = 0.044715


def _gelu(x):
    return 0.5 * x * (1.0 + jnp.tanh(_GC * (x + _GA * x * x * x)))


def _gelu_grad(x):
    t = jnp.tanh(_GC * (x + _GA * x * x * x))
    return 0.5 * (1.0 + t) + 0.5 * x * (1.0 - t * t) * _GC * (1.0 + 3.0 * _GA * x * x)


def _rms(hv, g):
    r = lax.rsqrt(jnp.mean(hv * hv, axis=-1, keepdims=True) + EPS)
    return hv * r * g, r


def _rms_bwd(dn, hv, r, g):
    xh = hv * r
    dxh = dn * g
    return r * (dxh - xh * jnp.mean(dxh * xh, axis=-1, keepdims=True)), xh


def _acc_rows(ref, part, first):
    @pl.when(first)
    def _():
        ref[...] = part

    @pl.when(jnp.logical_not(first))
    def _():
        ref[...] += part


def _coords():
    return lax.axis_index("x"), lax.axis_index("y"), lax.axis_index("c")


def _flip(v, d):
    return 1 - v if d else v


def _run(local, remote):
    for cp in local + remote:
        cp.start()
    for cp in remote:
        cp.wait()
    for cp in local:
        cp.wait()


def _via_vmem(src, dst, stage, sems, i):
    return (pltpu.make_async_copy(src, stage, sems.at[2 * i]), pltpu.make_async_copy(stage, dst, sems.at[2 * i + 1]))


def _run_staged(staged, remote):
    for load, _ in staged:
        load.start()
    for cp in remote:
        cp.start()
    for load, store in staged:
        load.wait()
        store.start()
    for cp in remote:
        cp.wait()
    for _, store in staged:
        store.wait()


_REL3 = ((1, 0), (0, 1), (1, 1))


class _Behind:
    def __init__(self, arrays, out_shapes, scratch, build, alias_pairs=()):
        self.arrays, self.out_shapes, self.scratch, self.build = list(arrays), list(out_shapes), list(scratch), build
        self.alias_pairs = list(alias_pairs)

    def aliases(self):
        return self.alias_pairs

    def start(self, ins, outs, scr):
        staged, remote = self.build(ins, outs, scr)
        for load, _ in staged:
            load.start()
        for cp in remote:
            cp.start()

    def finish(self, ins, outs, scr):
        staged, remote = self.build(ins, outs, scr)
        for load, store in staged:
            load.wait()
            store.start()
        for cp in remote:
            cp.wait()
        for _, store in staged:
            store.wait()


def _call(body, comm, *, name, grid, in_specs, out_specs, out_shape, scratch_shapes=(), params):
    in_specs, out_specs, out_shape = list(in_specs), list(out_specs), list(out_shape)
    scratch_shapes = list(scratch_shapes)
    if comm is None:
        f = pl.pallas_call(body, name=name, grid=grid, in_specs=in_specs, out_specs=out_specs,
                           out_shape=out_shape, scratch_shapes=scratch_shapes, compiler_params=params)
        return lambda *args: (f(*args), [])
    ni, no, ns = len(in_specs), len(out_specs), len(scratch_shapes)
    ci, co = len(comm.arrays), len(comm.out_shapes)

    def hosted(*refs):
        ins, cin = refs[:ni], refs[ni:ni + ci]
        outs, cout = refs[ni + ci:ni + ci + no], refs[ni + ci + no:ni + ci + no + co]
        scr, cscr = refs[ni + ci + no + co:ni + ci + no + co + ns], refs[ni + ci + no + co + ns:]
        first = last = None
        for axis, size in enumerate(grid):
            i = pl.program_id(axis)
            first = (i == 0) if first is None else jnp.logical_and(first, i == 0)
            last = (i == size - 1) if last is None else jnp.logical_and(last, i == size - 1)

        @pl.when(first)
        def _():
            comm.start(cin, cout, cscr)

        body(*ins, *outs, *scr)

        @pl.when(last)
        def _():
            comm.finish(cin, cout, cscr)

    f = pl.pallas_call(hosted, name=name, grid=grid, in_specs=in_specs + [ANY] * ci,
                       out_specs=out_specs + [ANY] * co, out_shape=out_shape + comm.out_shapes,
                       scratch_shapes=scratch_shapes + comm.scratch,
                       input_output_aliases={ni + a: no + b for a, b in comm.aliases()},
                       compiler_params=_params(*(("arbitrary",) * len(grid))))

    def run(*args):
        res = f(*args, *comm.arrays)
        return res[:no], res[no:]

    return run


def _gather_half_behind(shards):
    n = len(shards)

    def build(ins, outs, scr):
        send, recv, loc = scr[:3]
        stage = scr[3:]
        x, y, c = _coords()
        me = 2 * x + y
        staged = [_via_vmem(ins[t], outs[t].at[me], stage[t], loc, t) for t in range(n)]
        remote = []
        for t in range(n):
            half = shards[t].shape[0] // 2
            mine = pl.ds(c * half, half)
            for k, (dx, dy) in enumerate(_REL3):
                remote.append(pltpu.make_async_remote_copy(
                    src_ref=ins[t].at[mine], dst_ref=outs[t].at[me, mine],
                    send_sem=send.at[3 * t + k], recv_sem=recv.at[3 * t + k],
                    device_id=(_flip(x, dx), _flip(y, dy), c), device_id_type=MESH))
        return staged, remote

    return _Behind(shards, [jax.ShapeDtypeStruct((NSH,) + s.shape, s.dtype) for s in shards],
                   [pltpu.SemaphoreType.DMA((3 * n,)), pltpu.SemaphoreType.DMA((3 * n,)),
                    pltpu.SemaphoreType.DMA((2 * n,))] + [pltpu.VMEM(s.shape, s.dtype) for s in shards], build)


def _pass_halves(gathered, name, own=()):
    n, m = len(gathered), len(own)

    def body(*refs):
        shards, outs = refs[n:n + m], refs[n + m:2 * n + m]
        send, recv, loc = refs[2 * n + m:2 * n + m + 3]
        stage = refs[2 * n + m + 3:]
        x, y, c = _coords()
        staged = [_via_vmem(shards[t], outs[t].at[2 * x + y], stage[t], loc, t) for t in range(m)]
        remote = []
        for t in range(n):
            half = gathered[t].shape[1] // 2
            mine = pl.ds(c * half, half)
            for k, (dx, dy) in enumerate(_REL3):
                slot = 2 * _flip(x, dx) + _flip(y, dy)
                remote.append(pltpu.make_async_remote_copy(
                    src_ref=outs[t].at[slot, mine], dst_ref=outs[t].at[slot, mine],
                    send_sem=send.at[3 * t + k], recv_sem=recv.at[3 * t + k],
                    device_id=(x, y, 1 - c), device_id_type=MESH))
        _run_staged(staged, remote)

    return pl.pallas_call(
        body, name=name,
        out_shape=[jax.ShapeDtypeStruct(g.shape, g.dtype) for g in gathered],
        in_specs=[ANY] * (n + m), out_specs=[ANY] * n, input_output_aliases={t: t for t in range(n)},
        scratch_shapes=[pltpu.SemaphoreType.DMA((3 * n,)), pltpu.SemaphoreType.DMA((3 * n,)),
                        pltpu.SemaphoreType.DMA((max(2 * m, 1),))] + [pltpu.VMEM(s.shape, s.dtype) for s in own],
        compiler_params=pltpu.CompilerParams(vmem_limit_bytes=VMEM_LIMIT),
    )(*gathered, *own)


def _fill_own(sums, recvs, name):
    n = len(sums)

    def body(*refs):
        ins, outs = refs[:n], refs[2 * n:3 * n]
        loc = refs[3 * n]
        stage = refs[3 * n + 1:]
        x, y, _ = _coords()
        me = 2 * x + y
        _run_staged([_via_vmem(ins[t].at[me], outs[t].at[me], stage[t], loc, t) for t in range(n)], [])

    return pl.pallas_call(
        body, name=name,
        out_shape=[jax.ShapeDtypeStruct(r.shape, r.dtype) for r in recvs],
        in_specs=[ANY] * (2 * n), out_specs=[ANY] * n, input_output_aliases={n + t: t for t in range(n)},
        scratch_shapes=[pltpu.SemaphoreType.DMA((2 * n,))] + [pltpu.VMEM(s.shape[1:], s.dtype) for s in sums],
        compiler_params=pltpu.CompilerParams(vmem_limit_bytes=VMEM_LIMIT),
    )(*sums, *recvs)


def _scatter_chips_behind(sums):
    n = len(sums)

    def build(ins, outs, scr):
        send, recv, loc = scr[:3]
        stage = scr[3:]
        x, y, c = _coords()
        me = 2 * x + y
        staged = [_via_vmem(ins[t].at[me], outs[t].at[me], stage[t], loc, t) for t in range(n)]
        remote = []
        for t in range(n):
            for k, (dx, dy) in enumerate(_REL3):
                px, py = _flip(x, dx), _flip(y, dy)
                remote.append(pltpu.make_async_remote_copy(
                    src_ref=ins[t].at[2 * px + py], dst_ref=outs[t].at[me],
                    send_sem=send.at[3 * t + k], recv_sem=recv.at[3 * t + k],
                    device_id=(px, py, c), device_id_type=MESH))
        return staged, remote

    return _Behind(sums, [jax.ShapeDtypeStruct(s.shape, s.dtype) for s in sums],
                   [pltpu.SemaphoreType.DMA((3 * n,)), pltpu.SemaphoreType.DMA((3 * n,)),
                    pltpu.SemaphoreType.DMA((2 * n,))] + [pltpu.VMEM(s.shape[1:], s.dtype) for s in sums], build)


def _gather_all_behind(a):
    def build(ins, outs, scr):
        send, recv, loc, stage = scr
        x, y, c = _coords()
        me = 4 * x + 2 * y + c
        staged = [_via_vmem(ins[0], outs[0].at[me], stage, loc, 0)]
        remote = [pltpu.make_async_remote_copy(
            src_ref=ins[0], dst_ref=outs[0].at[me], send_sem=send.at[k], recv_sem=recv.at[k],
            device_id=(_flip(x, dx), _flip(y, dy), _flip(c, dc)), device_id_type=MESH)
            for k, (dx, dy, dc) in enumerate(_REL7)]
        return staged, remote

    return _Behind([a], [jax.ShapeDtypeStruct((8,) + a.shape, a.dtype)],
                   [pltpu.SemaphoreType.DMA((7,)), pltpu.SemaphoreType.DMA((7,)), pltpu.SemaphoreType.DMA((2,)),
                    pltpu.VMEM(a.shape, a.dtype)], build)


HBM = pl.BlockSpec(memory_space=pltpu.HBM)
SEM = pl.BlockSpec(memory_space=pltpu.SEMAPHORE)
EFFECT = pltpu.SideEffectType.DATAFLOW_SIDE_EFFECTING


def _chip_copies(srcs, lands, send, recv, gather):
    x, y, c = _coords()
    me = 2 * x + y
    cps = []
    for t in range(len(srcs)):
        for k, (dx, dy) in enumerate(_REL3):
            px, py = _flip(x, dx), _flip(y, dy)
            if gather:
                half = srcs[t].shape[0] // 2
                mine = pl.ds(c * half, half)
                src, dst = srcs[t].at[mine], lands[t].at[me, mine]
            else:
                src, dst = srcs[t].at[2 * px + py], lands[t].at[me]
            cps.append(pltpu.make_async_remote_copy(
                src_ref=src, dst_ref=dst, send_sem=send.at[3 * t + k], recv_sem=recv.at[3 * t + k],
                device_id=(px, py, c), device_id_type=MESH))
    return cps


def _chips_start(arrays, land_shapes, gather, name, after=()):
    n = len(arrays)

    def body(*refs):
        srcs, lands = refs[:n], refs[n:2 * n]
        send, recv = refs[2 * n + len(after)], refs[2 * n + len(after) + 1]
        token = refs[-1]
        for cp in _chip_copies(srcs, lands, send, recv, gather):
            cp.start()
        token[...] = jnp.zeros_like(token)

    lands = [lax.empty(s.shape, s.dtype) for s in land_shapes]
    thru = [pltpu.HBM(a.shape, a.dtype) for a in arrays] + [pltpu.HBM(s.shape, s.dtype) for s in land_shapes]
    res = pl.pallas_call(
        body, name=name,
        out_shape=(pltpu.SemaphoreType.DMA((3 * n,)), pltpu.SemaphoreType.DMA((3 * n,)), *thru,
                   jax.ShapeDtypeStruct((8, 128), F32)),
        in_specs=[HBM] * (2 * n) + [ANY] * len(after),
        out_specs=(SEM, SEM, *([HBM] * (2 * n)), pl.BlockSpec(memory_space=pltpu.VMEM)),
        input_output_aliases={t: 2 + t for t in range(2 * n)},
        compiler_params=pltpu.CompilerParams(has_side_effects=EFFECT),
    )(*[pltpu.with_memory_space_constraint(a, pltpu.HBM) for a in arrays],
      *[pltpu.with_memory_space_constraint(z, pltpu.HBM) for z in lands], *after)
    return res[0], res[1], list(res[2:2 + n]), list(res[2 + n:2 + 2 * n]), res[-1]


def _chips_wait(send, recv, arrays, lands, after, gather, name):
    n = len(arrays)

    def body(*refs):
        srcs, ls = refs[:n], refs[n:2 * n]
        sd, rv = refs[2 * n], refs[2 * n + 1]
        for cp in _chip_copies(srcs, ls, sd, rv, gather):
            cp.wait_send()
            cp.wait_recv()

    res = pl.pallas_call(
        body, name=name,
        out_shape=[pltpu.HBM(a.shape, a.dtype) for a in arrays] + [pltpu.HBM(z.shape, z.dtype) for z in lands],
        in_specs=[HBM] * (2 * n) + [SEM, SEM] + [ANY] * len(after), out_specs=[HBM] * (2 * n),
        input_output_aliases={t: t for t in range(2 * n)},
        compiler_params=pltpu.CompilerParams(has_side_effects=EFFECT),
    )(*arrays, *lands, send, recv, *after)
    return list(res[:n]), list(res[n:])


def _join(*parts):
    def cut(seq, key):
        res, o = [], 0
        for p in parts:
            k = len(getattr(p, key))
            res.append(seq[o:o + k])
            o += k
        return res

    def build(ins, outs, scr):
        staged, remote = [], []
        for p, i, o, s in zip(parts, cut(ins, "arrays"), cut(outs, "out_shapes"), cut(scr, "scratch")):
            st, rm = p.build(i, o, s)
            staged += st
            remote += rm
        return staged, remote

    pairs, ai, oi = [], 0, 0
    for p in parts:
        pairs += [(ai + a, oi + b) for a, b in p.alias_pairs]
        ai, oi = ai + len(p.arrays), oi + len(p.out_shapes)
    return _Behind(sum((p.arrays for p in parts), []), sum((p.out_shapes for p in parts), []),
                   sum((p.scratch for p in parts), []), build, pairs)


def _pass_halves_behind(gathered):
    n = len(gathered)

    def build(ins, outs, scr):
        send, recv = scr
        x, y, c = _coords()
        remote = []
        for t in range(n):
            half = gathered[t].shape[1] // 2
            mine = pl.ds(c * half, half)
            for k, (dx, dy) in enumerate(_REL3):
                slot = 2 * _flip(x, dx) + _flip(y, dy)
                remote.append(pltpu.make_async_remote_copy(
                    src_ref=outs[t].at[slot, mine], dst_ref=outs[t].at[slot, mine],
                    send_sem=send.at[3 * t + k], recv_sem=recv.at[3 * t + k],
                    device_id=(x, y, 1 - c), device_id_type=MESH))
        return [], remote

    return _Behind(gathered, [jax.ShapeDtypeStruct(g.shape, g.dtype) for g in gathered],
                   [pltpu.SemaphoreType.DMA((3 * n,)), pltpu.SemaphoreType.DMA((3 * n,))], build,
                   [(t, t) for t in range(n)])


_REL7 = tuple((dx, dy, dc) for dx in (0, 1) for dy in (0, 1) for dc in (0, 1))[1:]


def _gather_all(a, name):
    def body(a_ref, o_ref, send, recv, loc):
        x, y, c = _coords()
        me = 4 * x + 2 * y + c
        local = [pltpu.make_async_copy(a_ref, o_ref.at[me], loc.at[0])]
        remote = [pltpu.make_async_remote_copy(
            src_ref=a_ref, dst_ref=o_ref.at[me], send_sem=send.at[k], recv_sem=recv.at[k],
            device_id=(_flip(x, dx), _flip(y, dy), _flip(c, dc)), device_id_type=MESH)
            for k, (dx, dy, dc) in enumerate(_REL7)]
        _run(local, remote)

    return pl.pallas_call(
        body, name=name,
        out_shape=jax.ShapeDtypeStruct((8,) + a.shape, a.dtype),
        in_specs=[ANY], out_specs=ANY,
        scratch_shapes=[pltpu.SemaphoreType.DMA((7,)), pltpu.SemaphoreType.DMA((7,)),
                        pltpu.SemaphoreType.DMA((1,))],
    )(a)


def _pair_exchange(grads, name):
    n = len(grads)

    def body(*refs):
        ins, outs = refs[:n], refs[n:2 * n]
        send, recv = refs[2 * n:]
        x, y, c = _coords()
        remote = []
        for t in range(n):
            half = grads[t].shape[1] // 2
            remote.append(pltpu.make_async_remote_copy(
                src_ref=ins[t].at[:, pl.ds((1 - c) * half, half)], dst_ref=outs[t],
                send_sem=send.at[t], recv_sem=recv.at[t],
                device_id=(x, y, 1 - c), device_id_type=MESH))
        _run([], remote)

    return pl.pallas_call(
        body, name=name,
        out_shape=[jax.ShapeDtypeStruct((NSH, g.shape[1] // 2, g.shape[2]), g.dtype) for g in grads],
        in_specs=[ANY] * n, out_specs=[ANY] * n,
        scratch_shapes=[pltpu.SemaphoreType.DMA((n,)), pltpu.SemaphoreType.DMA((n,))],
    )(*grads)


def _swap_halves(halves, name):
    n = len(halves)

    def body(*refs):
        ins, outs = refs[:n], refs[n:2 * n]
        send, recv, loc = refs[2 * n:2 * n + 3]
        stage = refs[2 * n + 3:]
        x, y, c = _coords()
        local = [_via_vmem(ins[t], outs[t].at[c], stage[t], loc, t) for t in range(n)]
        remote = [pltpu.make_async_remote_copy(
            src_ref=ins[t], dst_ref=outs[t].at[c], send_sem=send.at[t], recv_sem=recv.at[t],
            device_id=(x, y, 1 - c), device_id_type=MESH) for t in range(n)]
        _run_staged(local, remote)

    return pl.pallas_call(
        body, name=name,
        out_shape=[jax.ShapeDtypeStruct((2,) + h.shape, h.dtype) for h in halves],
        in_specs=[ANY] * n, out_specs=[ANY] * n,
        scratch_shapes=[pltpu.SemaphoreType.DMA((n,)), pltpu.SemaphoreType.DMA((n,)),
                        pltpu.SemaphoreType.DMA((2 * n,))]
        + [pltpu.VMEM(h.shape, h.dtype) for h in halves],
        compiler_params=pltpu.CompilerParams(vmem_limit_bytes=VMEM_LIMIT),
    )(*halves)


def _by_shape(arrays):
    groups = {}
    for i, a in enumerate(arrays):
        groups.setdefault((a.shape, a.dtype), []).append(i)
    return list(groups.values())


def _sum_slots(rs, name, after=None):
    n = len(rs)
    K, R, C = rs[0].shape
    tr = _tile(R, max(16, (1 << 22) // (n * K * C)), 8 * (4 // rs[0].dtype.itemsize))

    def body(*refs):
        for r_ref, o_ref in zip(refs[:n], refs[len(refs) - n:]):
            acc = r_ref[0].astype(F32)
            for k in range(1, K):
                acc = acc + r_ref[k].astype(F32)
            o_ref[...] = acc

    dep = [] if after is None else [after]
    return pl.pallas_call(
        body, name=name, grid=(R // tr,),
        out_shape=[jax.ShapeDtypeStruct((R, C), F32)] * n,
        in_specs=[pl.BlockSpec((K, tr, C), lambda i: (0, i, 0))] * n + [ANY] * len(dep),
        out_specs=[pl.BlockSpec((tr, C), lambda i: (i, 0))] * n,
        compiler_params=_params("parallel"),
    )(*rs, *dep)


def _add_pair(gs, ss, core, name):
    n = len(gs)
    _, half, C = ss[0].shape
    tr = _tile(half, max(16, (1 << 21) // (n * C)))
    nb = half // tr

    def body(c_ref, *refs):
        for g_ref, s_ref, o_ref in zip(refs[:n], refs[n:2 * n], refs[2 * n:]):
            o_ref[...] = (g_ref[...].astype(F32) + s_ref[...].astype(F32)).astype(BF16)

    spec = pl.BlockSpec((1, tr, C), lambda j, i, c_ref: (j, i, 0))
    return pl.pallas_call(
        body, name=name,
        grid_spec=pltpu.PrefetchScalarGridSpec(
            num_scalar_prefetch=1, grid=(NSH, nb),
            in_specs=[pl.BlockSpec((1, tr, C), lambda j, i, c_ref: (j, c_ref[0] * nb + i, 0))] * n + [spec] * n,
            out_specs=[spec] * n),
        out_shape=[jax.ShapeDtypeStruct(ss[0].shape, BF16)] * n,
        compiler_params=_params("parallel", "parallel"),
    )(core, *gs, *ss)


def _adamw(w, g, m, v, name):
    _, R, C = w.shape
    tr = _tile(R, max(8, (1 << 18) // C), 8)
    c1 = 1.0 / (1.0 - B1 ** STEP)
    c2 = 1.0 / (1.0 - B2 ** STEP)

    def body(w_ref, g_ref, m_ref, v_ref, go_ref, d_ref, nm_ref, nv_ref):
        gv = g_ref[...]
        go_ref[...] = gv
        nm = B1 * m_ref[...] + (1.0 - B1) * gv
        nv = B2 * v_ref[...] + (1.0 - B2) * gv * gv
        nm_ref[...] = nm
        nv_ref[...] = nv
        d_ref[...] = -LR * ((nm * c1) / (jnp.sqrt(nv * c2) + AEPS) + WD * w_ref[...])

    spec = pl.BlockSpec((1, tr, C), lambda i: (0, i, 0))
    return pl.pallas_call(
        body, name=name, grid=(R // tr,),
        out_shape=[jax.ShapeDtypeStruct((1, R, C), F32)] * 4,
        in_specs=[spec] * 4, out_specs=[spec] * 4,
        compiler_params=_params("parallel"),
    )(w, g, m, v)


def _ffn_fwd(h, g, w1, w3, w2, name, comm=None):
    L = h.shape[0]
    tm = _tile(L, 704)

    def body(h_ref, g_ref, w1_ref, w3_ref, w2_ref, o_ref, a_ref, b_ref, n_s, acc_s):
        j = pl.program_id(1)

        @pl.when(j == 0)
        def _():
            hv = h_ref[...]
            n, _ = _rms(hv, g_ref[...])
            n_s[...] = n.astype(BF16)
            acc_s[...] = hv

        n = n_s[...]
        a = _dot_nt(n, w1_ref[0])
        b = _dot_nt(n, w3_ref[0])
        a_ref[0] = a.astype(BF16)
        b_ref[0] = b.astype(BF16)
        s = (a * _sigmoid(a) * b).astype(BF16)
        acc_s[...] += 0.5 * _dot(s, w2_ref[0])

        @pl.when(j == NSH - 1)
        def _():
            o_ref[...] = acc_s[...]

    row = pl.BlockSpec((tm, D), lambda i, j: (i, 0))
    hid = pl.BlockSpec((1, tm, FS), lambda i, j: (j, i, 0))
    wsp = pl.BlockSpec((1, FS, D), lambda i, j: (j, 0, 0))
    return _call(
        body, comm, name=name, grid=(L // tm, NSH),
        out_shape=[jax.ShapeDtypeStruct((L, D), F32),
                   jax.ShapeDtypeStruct((NSH, L, FS), BF16), jax.ShapeDtypeStruct((NSH, L, FS), BF16)],
        in_specs=[row, _res((1, D)), wsp, wsp, wsp],
        out_specs=[row, hid, hid],
        scratch_shapes=[pltpu.VMEM((tm, D), BF16), pltpu.VMEM((tm, D), F32)],
        params=_params("arbitrary", "arbitrary"),
    )(h, g, w1, w3, w2)


def _loss_head(hv, gv, tv, row0):
    y, r = _rms(hv, gv)
    row = row0 + lax.broadcasted_iota(jnp.int32, (hv.shape[0], 1), 0)
    e = jnp.where(row >= FRONT, y - tv, 0.0)
    dy = e * (1.0 / D)
    part = 0.5 * jnp.sum(jnp.sum(e * dy, axis=1, keepdims=True), axis=0, keepdims=True)
    dx, xh = _rms_bwd(dy, hv, r, gv)
    return dx, part, jnp.sum(dy * xh, axis=0, keepdims=True)


def _ffn_fwd_loss(h, g, w1, w3, w2, gf, tgt, name):
    L = h.shape[0]
    tm = _tile(L, 704)

    def body(h_ref, g_ref, w1_ref, w3_ref, w2_ref, gf_ref, t_ref, o_ref, a_ref, b_ref, loss_ref, dgf_ref,
             n_s, acc_s):
        i, j = pl.program_id(0), pl.program_id(1)

        @pl.when(j == 0)
        def _():
            hv = h_ref[...]
            n, _ = _rms(hv, g_ref[...])
            n_s[...] = n.astype(BF16)
            acc_s[...] = hv

        n = n_s[...]
        a = _dot_nt(n, w1_ref[0])
        b = _dot_nt(n, w3_ref[0])
        a_ref[0] = a.astype(BF16)
        b_ref[0] = b.astype(BF16)
        s = (a * _sigmoid(a) * b).astype(BF16)
        acc_s[...] += 0.5 * _dot(s, w2_ref[0])

        @pl.when(j == NSH - 1)
        def _():
            dx, part, dgf = _loss_head(acc_s[...], gf_ref[...], t_ref[...], i * tm)
            o_ref[...] = dx
            _acc_rows(loss_ref, part, i == 0)
            _acc_rows(dgf_ref, dgf, i == 0)

    row = pl.BlockSpec((tm, D), lambda i, j: (i, 0))
    hid = pl.BlockSpec((1, tm, FS), lambda i, j: (j, i, 0))
    wsp = pl.BlockSpec((1, FS, D), lambda i, j: (j, 0, 0))
    return pl.pallas_call(
        body, name=name, grid=(L // tm, NSH),
        out_shape=[jax.ShapeDtypeStruct((L, D), F32),
                   jax.ShapeDtypeStruct((NSH, L, FS), BF16), jax.ShapeDtypeStruct((NSH, L, FS), BF16),
                   jax.ShapeDtypeStruct((1, 1), F32), jax.ShapeDtypeStruct((1, D), F32)],
        in_specs=[row, _res((1, D)), wsp, wsp, wsp, _res((1, D)), row],
        out_specs=[row, hid, hid, pl.BlockSpec((1, 1), lambda i, j: (0, 0)),
                   pl.BlockSpec((1, D), lambda i, j: (0, 0))],
        scratch_shapes=[pltpu.VMEM((tm, D), BF16), pltpu.VMEM((tm, D), F32)],
        compiler_params=_params("arbitrary", "arbitrary"),
    )(h, g, w1, w3, w2, gf, tgt)


def _ffn_bwd(h, g, dout, a, b, w1, w3, w2, name, comm=None):
    L = h.shape[0]
    tm = _tile(L, 528)

    def body(h_ref, g_ref, do_ref, a_ref, b_ref, w1_ref, w3_ref, w2_ref,
             dh_ref, da_ref, db_ref, s_ref, n_ref, dg_ref, dob_s, dn_s):
        i, j = pl.program_id(0), pl.program_id(1)

        @pl.when(j == 0)
        def _():
            n, _ = _rms(h_ref[...], g_ref[...])
            n_ref[...] = n.astype(BF16)
            dob_s[...] = (0.5 * do_ref[...]).astype(BF16)
            dn_s[...] = jnp.zeros_like(dn_s)

        av = a_ref[0].astype(F32)
        bv = b_ref[0].astype(F32)
        sig = _sigmoid(av)
        sa = av * sig
        ds = _dot_nt(dob_s[...], w2_ref[0])
        s_ref[0] = (sa * bv).astype(BF16)
        da = (ds * bv * (sig + sa * (1.0 - sig))).astype(BF16)
        db = (ds * sa).astype(BF16)
        da_ref[0] = da
        db_ref[0] = db
        dn_s[...] += _dot(da, w1_ref[0]) + _dot(db, w3_ref[0])

        @pl.when(j == NSH - 1)
        def _():
            hv = h_ref[...]
            gv = g_ref[...]
            r = lax.rsqrt(jnp.mean(hv * hv, axis=-1, keepdims=True) + EPS)
            dn = dn_s[...]
            dx, xh = _rms_bwd(dn, hv, r, gv)
            dh_ref[...] = do_ref[...] + dx
            _acc_rows(dg_ref, jnp.sum(dn * xh, axis=0, keepdims=True), i == 0)

    row = pl.BlockSpec((tm, D), lambda i, j: (i, 0))
    hid = pl.BlockSpec((1, tm, FS), lambda i, j: (j, i, 0))
    wsp = pl.BlockSpec((1, FS, D), lambda i, j: (j, 0, 0))
    return _call(
        body, comm, name=name, grid=(L // tm, NSH),
        out_shape=[jax.ShapeDtypeStruct((L, D), F32)]
        + [jax.ShapeDtypeStruct((NSH, L, FS), BF16)] * 3
        + [jax.ShapeDtypeStruct((L, D), BF16), jax.ShapeDtypeStruct((1, D), F32)],
        in_specs=[row, _res((1, D)), row, hid, hid,
                  wsp, wsp, wsp],
        out_specs=[row, hid, hid, hid, row, pl.BlockSpec((1, D), lambda i, j: (0, 0))],
        scratch_shapes=[pltpu.VMEM((tm, D), BF16), pltpu.VMEM((tm, D), F32)],
        params=_params("arbitrary", "arbitrary"),
    )(h, g, dout, a, b, w1, w3, w2)


def _wgrad(xm, ym, name, scale=1.0):
    xs, ys = xm.ndim == 3, ym.ndim == 3
    assert not (xs and ys)
    L = xm.shape[-2]
    K, N = xm.shape[-1], ym.shape[-1]
    tl = _tile(L, 2112)
    nl = L // tl
    if xs or ys:
        tn, grid_n = N, NSH
    else:
        tn = _tile(N, 1024, 128)
        grid_n = N // tn

    def body(x_ref, y_ref, o_ref, acc_s):
        l = pl.program_id(1)
        xv = x_ref[0] if xs else x_ref[...]
        yv = y_ref[0] if ys else y_ref[...]
        part = _dot_tn(xv.astype(BF16), yv.astype(BF16))
        _acc_rows(acc_s, part, l == 0)

        @pl.when(l == nl - 1)
        def _():
            res = (acc_s[...] * scale).astype(BF16)
            if xs or ys:
                o_ref[0] = res
            else:
                o_ref[...] = res

    if xs:
        x_spec = pl.BlockSpec((1, tl, K), lambda n, l: (n, l, 0))
        y_spec = pl.BlockSpec((tl, N), lambda n, l: (l, 0))
        o_spec = pl.BlockSpec((1, K, N), lambda n, l: (n, 0, 0))
        o_shape = (NSH, K, N)
    elif ys:
        x_spec = pl.BlockSpec((tl, K), lambda n, l: (l, 0))
        y_spec = pl.BlockSpec((1, tl, N), lambda n, l: (n, l, 0))
        o_spec = pl.BlockSpec((1, K, N), lambda n, l: (n, 0, 0))
        o_shape = (NSH, K, N)
    else:
        x_spec = pl.BlockSpec((tl, K), lambda n, l: (l, 0))
        y_spec = pl.BlockSpec((tl, tn), lambda n, l: (l, n))
        o_spec = pl.BlockSpec((K, tn), lambda n, l: (0, n))
        o_shape = (K, N)
    return pl.pallas_call(
        body, name=name, grid=(grid_n, nl),
        out_shape=jax.ShapeDtypeStruct(o_shape, BF16),
        in_specs=[x_spec, y_spec], out_specs=o_spec,
        scratch_shapes=[pltpu.VMEM((K, tn), F32)],
        compiler_params=_params("parallel", "arbitrary"),
    )(xm, ym)


def _mix_in_fwd(h, g, w_in, b_gate, name, comm=None):
    L = h.shape[0]
    tm = _tile(L, 528)

    def body(h_ref, g_ref, w_ref, bg_ref, vg_ref, uf_ref, gt_ref):
        u, _ = _rms(h_ref[...], g_ref[...])
        ub = u.astype(BF16)
        p = [_dot(ub, w_ref[j]) for j in range(NSH)]
        a0, a1 = 2 * DC - WS, 2 * DC + DS - WS
        vg_ref[:, 0:WS] = p[0].astype(BF16)
        vg_ref[:, WS:2 * DC] = p[1][:, 0:a0].astype(BF16)
        uf_ref[...] = p[1][:, a0:a1].astype(BF16)
        gin = jnp.concatenate([p[1][:, a1:], p[2], p[3]], axis=1)
        gt_ref[...] = _sigmoid(gin + bg_ref[...]).astype(BF16)

    def row(n):
        return pl.BlockSpec((tm, n), lambda i: (i, 0))

    return _call(
        body, comm, name=name, grid=(L // tm,),
        out_shape=[jax.ShapeDtypeStruct((L, 2 * DC), BF16), jax.ShapeDtypeStruct((L, DS), BF16),
                   jax.ShapeDtypeStruct((L, 2 * D), BF16)],
        in_specs=[row(D), _res((1, D)), _res((NSH, D, WS)), _res((1, 2 * D))],
        out_specs=[row(2 * DC), row(DS), row(2 * D)],
        params=_params("parallel"),
    )(h, g, w_in, b_gate)


def _mix_in_bwd(h, g, dres, dv, dgl, duf, dgate, w_in, name):
    L = h.shape[0]
    tm = _tile(L, 528)

    def body(h_ref, g_ref, dr_ref, dv_ref, dgl_ref, duf_ref, dgt_ref, w_ref, dh_ref, u_ref, dp_ref, dgm_ref):
        i = pl.program_id(0)
        hv = h_ref[...]
        gv = g_ref[...]
        u, r = _rms(hv, gv)
        u_ref[...] = u.astype(BF16)
        a0, a1 = 2 * DC - WS, 2 * DC + DS - WS
        b0 = WS - a1
        dp = [jnp.concatenate([dv_ref[...], dgl_ref[:, 0:WS - DC]], axis=1),
              jnp.concatenate([dgl_ref[:, WS - DC:], duf_ref[...], dgt_ref[:, 0:b0]], axis=1),
              dgt_ref[:, b0:b0 + WS], dgt_ref[:, b0 + WS:]]
        du = jnp.zeros((tm, D), F32)
        for j in range(NSH):
            dp_ref[j] = dp[j]
            du = du + _dot_nt(dp[j], w_ref[j])
        dx, xh = _rms_bwd(du, hv, r, gv)
        dh_ref[...] = dr_ref[...] + dx
        _acc_rows(dgm_ref, jnp.sum(du * xh, axis=0, keepdims=True), i == 0)

    def row(n):
        return pl.BlockSpec((tm, n), lambda i: (i, 0))

    return pl.pallas_call(
        body, name=name, grid=(L // tm,),
        out_shape=[jax.ShapeDtypeStruct((L, D), F32), jax.ShapeDtypeStruct((L, D), BF16),
                   jax.ShapeDtypeStruct((NSH, L, WS), BF16), jax.ShapeDtypeStruct((1, D), F32)],
        in_specs=[row(D), _res((1, D)), row(D), row(DC), row(DC), row(DS), row(2 * D), _res((NSH, D, WS))],
        out_specs=[row(D), row(D), pl.BlockSpec((NSH, tm, WS), lambda i: (0, i, 0)),
                   pl.BlockSpec((1, D), lambda i: (0, 0))],
        compiler_params=_params("arbitrary"),
    )(h, g, dres, dv, dgl, duf, dgate, w_in)


def _conv_fwd(vg, dw, dwb, name, comm=None):
    L = vg.shape[0]
    nc = DC // 128

    def body(v_ref, g_ref, dw_ref, dwb_ref, z_ref, zp_s):
        zp_s[0:KWP, :] = jnp.zeros((KWP, 128), F32)
        zp_s[KWP:, :] = v_ref[...].astype(F32) * _sigmoid(g_ref[...].astype(F32))
        for r0 in range(0, L, CONV_ROWS):
            acc = jnp.broadcast_to(dwb_ref[...], (CONV_ROWS, 128))
            for k in range(KW):
                acc = acc + dw_ref[k:k + 1, :] * zp_s[pl.ds(r0 + k + 2, CONV_ROWS), :]
            z_ref[pl.ds(r0, CONV_ROWS), :] = acc

    return _call(
        body, comm, name=name, grid=(nc,),
        out_shape=[jax.ShapeDtypeStruct((L, DC), F32)],
        in_specs=[pl.BlockSpec((L, 128), lambda c: (0, c)), pl.BlockSpec((L, 128), lambda c: (0, nc + c)),
                  pl.BlockSpec((KWP, 128), lambda c: (0, c)), pl.BlockSpec((1, 128), lambda c: (0, c))],
        out_specs=[pl.BlockSpec((L, 128), lambda c: (0, c))],
        scratch_shapes=[pltpu.VMEM((L + KWP, 128), F32)],
        params=_params("parallel"),
    )(vg, vg, dw, dwb)


def _conv_bwd(dz1, vg, dw, name):
    L = vg.shape[0]
    nc = DC // 128

    def body(dz_ref, v_ref, g_ref, dw_ref, dv_ref, dg_ref, ddw_ref, ddwb_ref, zp_s, dzp_s):
        vv = v_ref[...].astype(F32)
        sg = _sigmoid(g_ref[...].astype(F32))
        zp_s[0:KWP, :] = jnp.zeros((KWP, 128), F32)
        zp_s[KWP:, :] = vv * sg
        dz = dz_ref[...]
        dzp_s[0:L, :] = dz
        dzp_s[L:, :] = jnp.zeros((KWP, 128), F32)
        ddwb_ref[...] = jnp.sum(dz, axis=0, keepdims=True)
        part = [jnp.zeros((8, 128), F32) for _ in range(KW)]
        for r0 in range(0, L, CONV_ROWS):
            rows = pl.ds(r0, CONV_ROWS)
            dzc = dz_ref[rows, :]
            acc = jnp.zeros((CONV_ROWS, 128), F32)
            for k in range(KW):
                acc = acc + dw_ref[k:k + 1, :] * dzp_s[pl.ds(r0 + KW - 1 - k, CONV_ROWS), :]
                prod = dzc * zp_s[pl.ds(r0 + k + 2, CONV_ROWS), :]
                for q in range(CONV_ROWS // 8):
                    part[k] = part[k] + prod[8 * q:8 * (q + 1), :]
            vc = v_ref[rows, :].astype(F32)
            sc = _sigmoid(g_ref[rows, :].astype(F32))
            dv_ref[rows, :] = (acc * sc).astype(BF16)
            dg_ref[rows, :] = (acc * vc * sc * (1.0 - sc)).astype(BF16)
        for k in range(KW):
            ddw_ref[k:k + 1, :] = jnp.sum(part[k], axis=0, keepdims=True)
        ddw_ref[KW:KWP, :] = jnp.zeros((KWP - KW, 128), F32)

    col = pl.BlockSpec((L, 128), lambda c: (0, c))
    return pl.pallas_call(
        body, name=name, grid=(nc,),
        out_shape=[jax.ShapeDtypeStruct((L, DC), BF16), jax.ShapeDtypeStruct((L, DC), BF16),
                   jax.ShapeDtypeStruct((KWP, DC), F32), jax.ShapeDtypeStruct((1, DC), F32)],
        in_specs=[col, col, pl.BlockSpec((L, 128), lambda c: (0, nc + c)),
                  pl.BlockSpec((KWP, 128), lambda c: (0, c))],
        out_specs=[col, col, pl.BlockSpec((KWP, 128), lambda c: (0, c)), pl.BlockSpec((1, 128), lambda c: (0, c))],
        scratch_shapes=[pltpu.VMEM((L + KWP, 128), F32), pltpu.VMEM((L + KWP, 128), F32)],
        compiler_params=_params("parallel"),
    )(dz1, vg, vg, dw)


NLB = QS // 128


def _lb_store(ref, rows, val):
    for cb in range(NLB):
        ref[cb, rows, :] = val[:, cb * 128:(cb + 1) * 128]


def _lb_load(ref, rows):
    return jnp.concatenate([ref[cb, rows, :] for cb in range(NLB)], axis=1)


def _scan(xr_ref, xi_ref, base, T, ar, ai, atr, ati, reverse):
    W = ar.shape[1]
    ar, ai, atr, ati = (jnp.broadcast_to(v, (8, W)) for v in (ar, ai, atr, ati))
    zero = jnp.zeros((8, W), F32)

    def rows(t, g):
        tt = T - 1 - t if reverse else t
        return pl.ds(base + g * 8 * T + tt, 8, stride=T)

    def make_step(store):
        def step(t, carry):
            out = []
            for g in range(NGRP):
                sr, si = carry[2 * g], carry[2 * g + 1]
                idx = rows(t, g)
                nr = ar * sr - ai * si + _lb_load(xr_ref, idx)
                ni = ar * si + ai * sr + _lb_load(xi_ref, idx)
                if store:
                    _lb_store(xr_ref, idx, nr)
                    _lb_store(xi_ref, idx, ni)
                out += [nr, ni]
            return tuple(out)
        return step

    ends = lax.fori_loop(0, T, make_step(False), (zero,) * (2 * NGRP))
    sub = lax.broadcasted_iota(jnp.int32, (8, W), 0)
    edge = sub == (7 if reverse else 0)
    shift, last = (7, 0) if reverse else (1, 7)
    inr, ini = jnp.zeros((1, W), F32), jnp.zeros((1, W), F32)
    starts = [None] * (2 * NGRP)
    for g in (reversed(range(NGRP)) if reverse else range(NGRP)):
        er, ei = ends[2 * g], ends[2 * g + 1]
        cr, ci = jnp.where(edge, inr, 0.0), jnp.where(edge, ini, 0.0)
        for _ in range(7):
            nr = atr * cr - ati * ci + er
            ni = atr * ci + ati * cr + ei
            cr = jnp.where(edge, inr, pltpu.roll(nr, shift, 0))
            ci = jnp.where(edge, ini, pltpu.roll(ni, shift, 0))
        starts[2 * g], starts[2 * g + 1] = cr, ci
        inr = (atr * cr - ati * ci + er)[last:last + 1]
        ini = (atr * ci + ati * cr + ei)[last:last + 1]
    lax.fori_loop(0, T, make_step(True), tuple(starts))


def _ssm_fwd(uf, bre, bim, cre, cim, lamp, dsk, name, comm=None):
    L = uf.shape[0]
    T = L // NSEG
    tc = L // NCH

    def body(u_ref, bre_ref, bim_ref, cre_ref, cim_ref, lam_ref, d_ref, y_ref, sr_s, si_s):
        for k in range(NCH):
            sl = slice(k * tc, (k + 1) * tc)
            uk = u_ref[sl, :]
            _lb_store(sr_s, sl, _dot(uk, bre_ref[0]))
            _lb_store(si_s, sl, _dot(uk, bim_ref[0]))
        _scan(sr_s, si_s, 0, T, lam_ref[0:1, :], lam_ref[1:2, :], lam_ref[2:3, :], lam_ref[3:4, :], False)
        for k in range(NCH):
            sl = slice(k * tc, (k + 1) * tc)
            y_ref[sl, :] = (_dot(_lb_load(sr_s, sl).astype(BF16), cre_ref[0])
                            - _dot(_lb_load(si_s, sl).astype(BF16), cim_ref[0])
                            + d_ref[...] * u_ref[sl, :].astype(F32))

    return _call(
        body, comm, name=name, grid=(NQ,),
        out_shape=[jax.ShapeDtypeStruct((L, DS), F32)],
        in_specs=[pl.BlockSpec((L, QU), lambda q: (0, q)),
                  pl.BlockSpec((1, QU, QS), lambda q: (q, 0, 0)), pl.BlockSpec((1, QU, QS), lambda q: (q, 0, 0)),
                  pl.BlockSpec((1, QS, QU), lambda q: (q, 0, 0)), pl.BlockSpec((1, QS, QU), lambda q: (q, 0, 0)),
                  pl.BlockSpec((8, QS), lambda q: (0, q)), pl.BlockSpec((1, QU), lambda q: (0, q))],
        out_specs=[pl.BlockSpec((L, QU), lambda q: (0, q))],
        scratch_shapes=[pltpu.VMEM((NLB, L, 128), F32), pltpu.VMEM((NLB, L, 128), F32)],
        params=_params("parallel"),
    )(uf, bre, bim, cre, cim, lamp, dsk)


def _ssm_bwd(uf, dyss, bre, bim, cre, cim, lamp, dsk, name, comm=None):
    L = uf.shape[0]
    T = L // NSEG
    tc = L // NCH

    def body(u_ref, dy_ref, bre_ref, bim_ref, cre_ref, cim_ref, lam_ref, d_ref,
             du_ref, dbre_ref, dbim_ref, dcre_ref, dcim_ref, dlam_ref, dd_ref, sr_s, si_s, gr_s, gi_s):
        _lb_store(sr_s, slice(0, SOFF), jnp.zeros((SOFF, QS), F32))
        _lb_store(si_s, slice(0, SOFF), jnp.zeros((SOFF, QS), F32))
        for k in range(NCH):
            sl = slice(k * tc, (k + 1) * tc)
            ss = slice(SOFF + k * tc, SOFF + (k + 1) * tc)
            uk = u_ref[sl, :]
            dyk = dy_ref[sl, :].astype(BF16)
            _lb_store(sr_s, ss, _dot(uk, bre_ref[0]))
            _lb_store(si_s, ss, _dot(uk, bim_ref[0]))
            _lb_store(gr_s, sl, _dot_nt(dyk, cre_ref[0]))
            _lb_store(gi_s, sl, -_dot_nt(dyk, cim_ref[0]))
        ar, ai, atr, ati = lam_ref[0:1, :], lam_ref[1:2, :], lam_ref[2:3, :], lam_ref[3:4, :]
        _scan(sr_s, si_s, SOFF, T, ar, ai, atr, ati, False)
        _scan(gr_s, gi_s, 0, T, ar, -ai, atr, -ati, True)
        dbre = jnp.zeros((QU, QS), F32)
        dbim = jnp.zeros((QU, QS), F32)
        dcre = jnp.zeros((QS, QU), F32)
        dcim = jnp.zeros((QS, QU), F32)
        dd = jnp.zeros((1, QU), F32)
        qr = jnp.zeros((1, QS), F32)
        qi = jnp.zeros((1, QS), F32)
        for k in range(NCH):
            sl = slice(k * tc, (k + 1) * tc)
            ss = slice(SOFF + k * tc, SOFF + (k + 1) * tc)
            sp = slice(SOFF - 1 + k * tc, SOFF - 1 + (k + 1) * tc)
            uk = u_ref[sl, :]
            dyk = dy_ref[sl, :]
            dyb = dyk.astype(BF16)
            gr, gi = _lb_load(gr_s, sl), _lb_load(gi_s, sl)
            pr, pi = _lb_load(sr_s, sp), _lb_load(si_s, sp)
            qr = qr + jnp.sum(gr * pr + gi * pi, axis=0, keepdims=True)
            qi = qi + jnp.sum(gi * pr - gr * pi, axis=0, keepdims=True)
            grb, gib = gr.astype(BF16), gi.astype(BF16)
            du_ref[sl, :] = (_dot_nt(grb, bre_ref[0]) + _dot_nt(gib, bim_ref[0])
                             + dyk * d_ref[...]).astype(BF16)
            dbre = dbre + _dot_tn(uk, grb)
            dbim = dbim + _dot_tn(uk, gib)
            dcre = dcre + _dot_tn(_lb_load(sr_s, ss).astype(BF16), dyb)
            dcim = dcim - _dot_tn(_lb_load(si_s, ss).astype(BF16), dyb)
            dd = dd + jnp.sum(dyk * uk.astype(F32), axis=0, keepdims=True)
        dlam_ref[0] = jnp.concatenate([qr, qi, jnp.zeros((6, QS), F32)], axis=0)
        dbre_ref[0] = dbre
        dbim_ref[0] = dbim
        dcre_ref[0] = dcre
        dcim_ref[0] = dcim
        dd_ref[...] = dd

    col = pl.BlockSpec((L, QU), lambda q: (0, q))
    bsp = pl.BlockSpec((1, QU, QS), lambda q: (q, 0, 0))
    csp = pl.BlockSpec((1, QS, QU), lambda q: (q, 0, 0))
    return _call(
        body, comm, name=name, grid=(NQ,),
        out_shape=[jax.ShapeDtypeStruct((L, DS), BF16),
                   jax.ShapeDtypeStruct((NQ, QU, QS), F32), jax.ShapeDtypeStruct((NQ, QU, QS), F32),
                   jax.ShapeDtypeStruct((NQ, QS, QU), F32), jax.ShapeDtypeStruct((NQ, QS, QU), F32),
                   jax.ShapeDtypeStruct((NQ, 8, QS), F32), jax.ShapeDtypeStruct((1, DS), F32)],
        in_specs=[col, col, bsp, bsp, csp, csp,
                  pl.BlockSpec((8, QS), lambda q: (0, q)), pl.BlockSpec((1, QU), lambda q: (0, q))],
        out_specs=[col,
                   pl.BlockSpec((1, QU, QS), lambda q: (q, 0, 0)), pl.BlockSpec((1, QU, QS), lambda q: (q, 0, 0)),
                   pl.BlockSpec((1, QS, QU), lambda q: (q, 0, 0)), pl.BlockSpec((1, QS, QU), lambda q: (q, 0, 0)),
                   pl.BlockSpec((1, 8, QS), lambda q: (q, 0, 0)), pl.BlockSpec((1, QU), lambda q: (0, q))],
        scratch_shapes=[pltpu.VMEM((NLB, L + SOFF, 128), F32), pltpu.VMEM((NLB, L + SOFF, 128), F32),
                        pltpu.VMEM((NLB, L, 128), F32), pltpu.VMEM((NLB, L, 128), F32)],
        params=_params("parallel"),
    )(uf, dyss, bre, bim, cre, cim, lamp, dsk)


def _branches(z1_ref, yss_ref, gt_ref, lng_ref, lnb_ref, wp_ref, wv_ref, wg_ref):
    zf = z1_ref[...]
    mu = jnp.mean(zf, axis=-1, keepdims=True)
    zc = zf - mu
    rstd = lax.rsqrt(jnp.mean(zc * zc, axis=-1, keepdims=True) + EPS)
    zn = zc * rstd
    z2 = zn * lng_ref[...] + lnb_ref[...]
    sz = _sigmoid(z2)
    z3 = (z2 * sz).astype(BF16)
    y_conv = _dot(z3, wp_ref[...])
    yss = yss_ref[...]
    yg = _gelu(yss).astype(BF16)
    sv = _dot(yg, wv_ref[...])
    sig = _sigmoid(_dot(yg, wg_ref[...]))
    y_ssm = sv * sig
    gc = gt_ref[:, 0:D].astype(F32)
    gs = gt_ref[:, D:2 * D].astype(F32)
    m = gc * y_conv + gs * y_ssm
    return dict(rstd=rstd, zn=zn, z2=z2, sz=sz, z3=z3, y_conv=y_conv, yss=yss, yg=yg, sv=sv, sig=sig,
                y_ssm=y_ssm, gc=gc, gs=gs, m=m)


def _merge_fwd(h, z1, yss, gate, lng, lnb, wp, wv, wg, wo, name, comm=None):
    L = h.shape[0]
    tm = _tile(L, 528)

    def body(h_ref, z1_ref, yss_ref, gt_ref, lng_ref, lnb_ref, wp_ref, wv_ref, wg_ref, wo_ref, o_ref):
        f = _branches(z1_ref, yss_ref, gt_ref, lng_ref, lnb_ref, wp_ref, wv_ref, wg_ref)
        o_ref[...] = h_ref[...] + _dot(f["m"].astype(BF16), wo_ref[...])

    def row(n):
        return pl.BlockSpec((tm, n), lambda i: (i, 0))

    return _call(
        body, comm, name=name, grid=(L // tm,),
        out_shape=[jax.ShapeDtypeStruct((L, D), F32)],
        in_specs=[row(D), row(DC), row(DS), row(2 * D), _res((1, DC)), _res((1, DC)),
                  _res((DC, D)), _res((DS, D)), _res((DS, D)), _res((D, D))],
        out_specs=[row(D)],
        params=_params("parallel"),
    )(h, z1, yss, gate, lng, lnb, wp, wv, wg, wo)


def _merge_bwd(dh, z1, yss, gate, lng, lnb, wp, wv, wg, wo, name):
    L = dh.shape[0]
    tm = _tile(L, 352)

    def body(dh_ref, z1_ref, yss_ref, gt_ref, lng_ref, lnb_ref, wp_ref, wv_ref, wg_ref, wo_ref,
             m_ref, dgt_ref, dyc_ref, z3_ref, dz1_ref, yg_ref, dsv_ref, dsg_ref, dyss_ref,
             dbg_ref, dlng_ref, dlnb_ref):
        i = pl.program_id(0)
        f = _branches(z1_ref, yss_ref, gt_ref, lng_ref, lnb_ref, wp_ref, wv_ref, wg_ref)
        gc, gs, sig, sv = f["gc"], f["gs"], f["sig"], f["sv"]
        m_ref[...] = f["m"].astype(BF16)
        z3_ref[...] = f["z3"]
        yg_ref[...] = f["yg"]
        dm = _dot_nt(dh_ref[...].astype(BF16), wo_ref[...])
        dgc = (dm * f["y_conv"] * gc * (1.0 - gc)).astype(BF16)
        dgs = (dm * f["y_ssm"] * gs * (1.0 - gs)).astype(BF16)
        dgt_ref[:, 0:D] = dgc
        dgt_ref[:, D:2 * D] = dgs
        part = jnp.concatenate([jnp.sum(dgc.astype(F32), axis=0, keepdims=True),
                                jnp.sum(dgs.astype(F32), axis=0, keepdims=True)], axis=1)
        _acc_rows(dbg_ref, part, i == 0)
        dyc = (dm * gc).astype(BF16)
        dyc_ref[...] = dyc
        dys = dm * gs
        dsv = (dys * sig).astype(BF16)
        dsg = (dys * sv * sig * (1.0 - sig)).astype(BF16)
        dsv_ref[...] = dsv
        dsg_ref[...] = dsg
        dyg = _dot_nt(dsv, wv_ref[...]) + _dot_nt(dsg, wg_ref[...])
        dyss_ref[...] = dyg * _gelu_grad(f["yss"])
        dz3 = _dot_nt(dyc, wp_ref[...])
        z2, sz, zn = f["z2"], f["sz"], f["zn"]
        dz2 = dz3 * sz * (1.0 + z2 * (1.0 - sz))
        _acc_rows(dlng_ref, jnp.sum(dz2 * zn, axis=0, keepdims=True), i == 0)
        _acc_rows(dlnb_ref, jnp.sum(dz2, axis=0, keepdims=True), i == 0)
        dzn = dz2 * lng_ref[...]
        dz1_ref[...] = f["rstd"] * (dzn - jnp.mean(dzn, axis=-1, keepdims=True)
                                    - zn * jnp.mean(dzn * zn, axis=-1, keepdims=True))

    def row(n):
        return pl.BlockSpec((tm, n), lambda i: (i, 0))

    def tot(n):
        return pl.BlockSpec((1, n), lambda i: (0, 0))

    return pl.pallas_call(
        body, name=name, grid=(L // tm,),
        out_shape=[jax.ShapeDtypeStruct((L, D), BF16), jax.ShapeDtypeStruct((L, 2 * D), BF16),
                   jax.ShapeDtypeStruct((L, D), BF16), jax.ShapeDtypeStruct((L, DC), BF16),
                   jax.ShapeDtypeStruct((L, DC), F32), jax.ShapeDtypeStruct((L, DS), BF16),
                   jax.ShapeDtypeStruct((L, D), BF16), jax.ShapeDtypeStruct((L, D), BF16),
                   jax.ShapeDtypeStruct((L, DS), F32),
                   jax.ShapeDtypeStruct((1, 2 * D), F32), jax.ShapeDtypeStruct((1, DC), F32),
                   jax.ShapeDtypeStruct((1, DC), F32)],
        in_specs=[row(D), row(DC), row(DS), row(2 * D), _res((1, DC)), _res((1, DC)),
                  _res((DC, D)), _res((DS, D)), _res((DS, D)), _res((D, D))],
        out_specs=[row(D), row(2 * D), row(D), row(DC), row(DC), row(DS), row(D), row(D), row(DS),
                   tot(2 * D), tot(DC), tot(DC)],
        compiler_params=_params("arbitrary"),
    )(dh, z1, yss, gate, lng, lnb, wp, wv, wg, wo)


def _ssm_disc(lam_re, lam_im, log_dt, b_re, b_im):
    lam = lax.complex(lam_re, lam_im)
    dt = jnp.exp(log_dt)[:, None]
    lam_bar = jnp.exp(lam * dt)
    bbar = ((lam_bar - 1.0) / lam)[..., None] * lax.complex(b_re, b_im)
    return jnp.real(lam_bar), jnp.imag(lam_bar), jnp.real(bbar), jnp.imag(bbar)


def _bdiag_in(m):
    m4 = m.reshape(NQ, G // NQ, P, H)
    return jnp.einsum("qgph,gk->qghkp", m4, jnp.eye(G // NQ, dtype=m.dtype)).reshape(NQ, QU, QS)


def _bdiag_out(m):
    m4 = m.reshape(NQ, G // NQ, H, P)
    return jnp.einsum("qghp,gk->qgpkh", m4, jnp.eye(G // NQ, dtype=m.dtype)).reshape(NQ, QS, QU)


def _diag_blocks(m4):
    return jnp.einsum("qiaib->qiab", m4).reshape(G, m4.shape[2], m4.shape[4])


def _pack(parts, rows_mult=8):
    flat = jnp.concatenate([p.reshape(-1).astype(F32) for p in parts])
    n = flat.shape[0]
    tot = -(-n // (128 * rows_mult)) * (128 * rows_mult)
    return jnp.pad(flat, (0, tot - n)).reshape(tot // 128, 128)


def _unpack(buf, shapes):
    flat = buf.reshape(-1)
    out, o = [], 0
    for s in shapes:
        n = math.prod(s)
        out.append(flat[o:o + n].reshape(s))
        o += n
    return out


def kernel(x, meta_tokens, ffn1_norm, ffn1_w1, ffn1_w3, ffn1_w2, mix_norm, w_in, b_gate, conv_dw, conv_dw_b, conv_ln_g, conv_ln_b, conv_proj, ssm_lam_re, ssm_lam_im, ssm_log_dt, ssm_b_re, ssm_b_im, ssm_c_re, ssm_c_im, ssm_d, ssm_w_v, ssm_w_g, w_out, ffn2_norm, ffn2_w1, ffn2_w3, ffn2_w2, final_norm, loss_target, m_meta_tokens, m_ffn1_norm, m_ffn1_w1, m_ffn1_w3, m_ffn1_w2, m_mix_norm, m_w_in, m_b_gate, m_conv_dw, m_conv_dw_b, m_conv_ln_g, m_conv_ln_b, m_conv_proj, m_ssm_lam_re, m_ssm_lam_im, m_ssm_log_dt, m_ssm_b_re, m_ssm_b_im, m_ssm_c_re, m_ssm_c_im, m_ssm_d, m_ssm_w_v, m_ssm_w_g, m_w_out, m_ffn2_norm, m_ffn2_w1, m_ffn2_w3, m_ffn2_w2, m_final_norm, v_meta_tokens, v_ffn1_norm, v_ffn1_w1, v_ffn1_w3, v_ffn1_w2, v_mix_norm, v_w_in, v_b_gate, v_conv_dw, v_conv_dw_b, v_conv_ln_g, v_conv_ln_b, v_conv_proj, v_ssm_lam_re, v_ssm_lam_im, v_ssm_log_dt, v_ssm_b_re, v_ssm_b_im, v_ssm_c_re, v_ssm_c_im, v_ssm_d, v_ssm_w_v, v_ssm_w_g, v_w_out, v_ffn2_norm, v_ffn2_w1, v_ffn2_w3, v_ffn2_w2, v_final_norm):
    args = dict(locals())
    names = ["meta_tokens", "ffn1_norm", "ffn1_w1", "ffn1_w3", "ffn1_w2", "mix_norm", "w_in", "b_gate",
             "conv_dw", "conv_dw_b", "conv_ln_g", "conv_ln_b", "conv_proj", "ssm_lam_re", "ssm_lam_im",
             "ssm_log_dt", "ssm_b_re", "ssm_b_im", "ssm_c_re", "ssm_c_im", "ssm_d", "ssm_w_v", "ssm_w_g",
             "w_out", "ffn2_norm", "ffn2_w1", "ffn2_w3", "ffn2_w2", "final_norm"]
    big = ["ffn1_w1", "ffn1_w3", "ffn1_w2", "w_in", "conv_proj", "ssm_w_v", "ssm_w_g", "w_out",
           "ffn2_w1", "ffn2_w3", "ffn2_w2"]
    small = [n for n in names if n not in big]

    xs = x[0]
    S = xs.shape[0]
    L = FRONT + S
    T = L // NSEG
    jx, jy = lax.axis_index("x"), lax.axis_index("y")
    chip = 2 * jx + jy

    small_all = _gather_all(_pack([meta_tokens, conv_dw[0]]), "gather_small")
    sm = small_all[0::2].reshape(NSH, -1)
    nmt = NMETA * (D // NSH)
    ndw = KW * (DC // NSH)
    meta_full = sm[:, :nmt].reshape(NSH, NMETA, D // NSH).transpose(1, 0, 2).reshape(NMETA, D)
    dw_full = sm[:, nmt:nmt + ndw].reshape(NSH, KW, DC // NSH).transpose(1, 0, 2).reshape(KW, DC)
    dw_pad = jnp.pad(dw_full, ((0, KWP - KW), (0, 0)))
    tposed = ("ffn1_w1", "ffn1_w3", "ffn2_w1", "ffn2_w3")

    def view(a, n):
        return jnp.swapaxes(a, 1, 2) if n in tposed else a

    grp_a = ["ffn1_w1", "ffn1_w3", "ffn1_w2"]
    grp_b = ["w_in", "conv_proj", "ssm_w_v", "ssm_w_g", "w_out"]
    grp_c = ["ffn2_w1", "ffn2_w3", "ffn2_w2"]

    shards = {n: view(args[n], n)[0].astype(BF16) for n in big}

    def shard(n):
        return shards[n]

    sh_a = [shard(n) for n in grp_a]
    ga_send, ga_recv, sh_a, land_a, _ = _chips_start(
        sh_a, [jax.ShapeDtypeStruct((NSH,) + s.shape, s.dtype) for s in sh_a], True, "gather_ffn1_start",
        [small_all])

    def cols(w):
        return w.transpose(1, 0, 2).reshape(w.shape[1], -1)

    disc_in = (ssm_lam_re[0], ssm_lam_im[0], ssm_log_dt[0], ssm_b_re[0], ssm_b_im[0])
    (lbr, lbi, bbr, bbi), disc_vjp = jax.vjp(_ssm_disc, *disc_in)
    lam_t = jnp.exp(lax.complex(ssm_lam_re[0], ssm_lam_im[0]) * (jnp.exp(ssm_log_dt[0])[:, None] * T))
    lamp = jnp.concatenate([lbr.reshape(1, NST), lbi.reshape(1, NST), jnp.real(lam_t).reshape(1, NST),
                            jnp.imag(lam_t).reshape(1, NST), jnp.zeros((4, NST), F32)], axis=0)
    bre_bd, bim_bd = _bdiag_in(bbr).astype(BF16), _bdiag_in(bbi).astype(BF16)
    cre_bd, cim_bd = _bdiag_out(ssm_c_re[0]).astype(BF16), _bdiag_out(ssm_c_im[0]).astype(BF16)

    h0 = lax.dynamic_update_slice(jnp.pad(xs, ((FRONT, 0), (0, 0))), meta_full, (FRONT - NMETA, 0))
    tgt = jnp.pad(loss_target[0], ((FRONT, 0), (0, 0)))
    small_wmv = [_pack([args[p + n] for n in small])[None] for p in ("", "m_", "v_")]
    early_work = [h0, tgt, bre_bd, bim_bd, cre_bd, cim_bd] + [shards[n] for n in grp_b + grp_c] + small_wmv
    sh_a, land_a = _chips_wait(ga_send, ga_recv, sh_a, land_a, early_work, True, "gather_ffn1_wait")
    gw = dict(zip(grp_a, _pass_halves(land_a, "pass_ffn1", sh_a)))
    (h1, a1, b1), got = _ffn_fwd(h0, ffn1_norm, gw["ffn1_w1"], gw["ffn1_w3"], gw["ffn1_w2"], "ffn1_fwd",
                                 _gather_half_behind([shard(n) for n in grp_b]))
    w_in_f = _pass_halves(got[:1], "pass_w_in")[0]
    (vg, uf, gate), got1 = _mix_in_fwd(h1, mix_norm, w_in_f, b_gate, "mix_in_fwd",
                                       _join(_gather_half_behind([shard("ffn2_w1")]),
                                             _pass_halves_behind(list(got[1:]))))
    gw.update(zip(grp_b[1:], got1[1:]))
    wp_f, wv_f, wg_f = cols(gw["conv_proj"]), cols(gw["ssm_w_v"]), cols(gw["ssm_w_g"])
    wo_f = gw["w_out"].reshape(D, D)
    (z1,), got3 = _conv_fwd(vg, dw_pad, conv_dw_b, "conv_fwd", _gather_half_behind([shard("ffn2_w3")]))
    (yss,), got2 = _ssm_fwd(uf, bre_bd, bim_bd, cre_bd, cim_bd, lamp, ssm_d, "ssm_fwd",
                            _gather_half_behind([shard("ffn2_w2")]))
    (h2,), got_c = _merge_fwd(h1, z1, yss, gate, conv_ln_g, conv_ln_b, wp_f, wv_f, wg_f, wo_f, "merge_fwd",
                              _pass_halves_behind([got1[0], got3[0], got2[0]]))
    gw.update(zip(grp_c, got_c))
    dh3, a2, b2, loss_part, d_final = _ffn_fwd_loss(
        h2, ffn2_norm, gw["ffn2_w1"], gw["ffn2_w3"], gw["ffn2_w2"], final_norm.reshape(1, D), tgt, "ffn2_fwd_loss")

    gbig = {}
    core = lax.axis_index("c").astype(jnp.int32).reshape(1)

    def pair_sums(group, tag):
        gl = [gbig[n] for n in group]
        sib = _pair_exchange(gl, "pair_exchange_" + tag)
        out = [None] * len(group)
        for idx in _by_shape(gl):
            res = _add_pair([gl[i] for i in idx], [sib[i] for i in idx], core, "pair_" + group[idx[0]])
            for i, r in zip(idx, res):
                out[i] = r
        return out

    (dh2, da2, db2, s2, n2, d_ffn2_norm), _ = _ffn_bwd(
        h2, ffn2_norm, dh3, a2, b2, gw["ffn2_w1"], gw["ffn2_w3"], gw["ffn2_w2"], "ffn2_bwd")
    gbig["ffn2_w1"] = _wgrad(da2, n2, "ffn2_dw1")
    gbig["ffn2_w3"] = _wgrad(db2, n2, "ffn2_dw3")
    gbig["ffn2_w2"] = _wgrad(s2, dh3, "ffn2_dw2", 0.5)
    pair_c = pair_sums(grp_c, "ffn2")
    (m_b, dgate, dyc, z3, dz1, yg, dsv, dsg, dyss, d_b_gate, d_ln_g, d_ln_b) = _merge_bwd(
        dh2, z1, yss, gate, conv_ln_g, conv_ln_b, wp_f, wv_f, wg_f, wo_f, "merge_bwd")
    gbig["w_out"] = _wgrad(m_b, dh2, "dw_out").reshape(NSH, D // NSH, D)

    def shard_cols(gm):
        return gm.reshape(gm.shape[0], NSH, -1).transpose(1, 0, 2)

    gbig["conv_proj"] = shard_cols(_wgrad(z3, dyc, "dw_proj"))
    gbig["ssm_w_v"] = shard_cols(_wgrad(yg, dsv, "dw_v"))
    gbig["ssm_w_g"] = shard_cols(_wgrad(yg, dsg, "dw_g"))
    dv, dgl, ddw, d_dw_b = _conv_bwd(dz1, vg, dw_pad, "conv_bwd")
    (duf, dbre, dbim, dcre, dcim, dlam, d_ssm_d), recv_c = _ssm_bwd(
        uf, dyss, bre_bd, bim_bd, cre_bd, cim_bd, lamp, ssm_d, "ssm_bwd", _scatter_chips_behind(pair_c))
    dh1, u_b, dproj, d_mix_norm = _mix_in_bwd(h1, mix_norm, dh2, dv, dgl, duf, dgate, w_in_f, "mix_in_bwd")
    gbig["w_in"] = _wgrad(u_b, dproj, "dw_in")
    pair_b = pair_sums(grp_b, "mix")

    d_bbr = _diag_blocks(dbre.reshape(NQ, 8, H, 8, P)).transpose(0, 2, 1)
    d_bbi = _diag_blocks(dbim.reshape(NQ, 8, H, 8, P)).transpose(0, 2, 1)
    d_c_re = _diag_blocks(dcre.reshape(NQ, 8, P, 8, H)).transpose(0, 2, 1)
    d_c_im = _diag_blocks(dcim.reshape(NQ, 8, P, 8, H)).transpose(0, 2, 1)
    d_lbr = dlam[:, 0, :].reshape(G, P)
    d_lbi = dlam[:, 1, :].reshape(G, P)
    d_lam_re, d_lam_im, d_log_dt, d_b_re, d_b_im = disc_vjp((d_lbr, d_lbi, d_bbr, d_bbi))

    sg = {"mix_norm": d_mix_norm, "b_gate": d_b_gate, "conv_dw": ddw[:KW], "conv_dw_b": d_dw_b,
          "conv_ln_g": d_ln_g, "conv_ln_b": d_ln_b, "ssm_lam_re": d_lam_re, "ssm_lam_im": d_lam_im,
          "ssm_log_dt": d_log_dt, "ssm_b_re": d_b_re, "ssm_b_im": d_b_im, "ssm_c_re": d_c_re, "ssm_c_im": d_c_im,
          "ssm_d": d_ssm_d, "ffn2_norm": d_ffn2_norm, "final_norm": d_final}
    late = ["meta_tokens", "ffn1_norm"]
    early = [n for n in small if n not in late]

    (dh0, da1, db1, s1, n1, d_ffn1_norm), got = _ffn_bwd(
        h0, ffn1_norm, dh1, a1, b1, gw["ffn1_w1"], gw["ffn1_w3"], gw["ffn1_w2"], "ffn1_bwd",
        _join(_scatter_chips_behind(pair_b), _gather_all_behind(_pack([sg[n] for n in early]))))
    recv_b, early_all = got[:len(grp_b)], got[len(grp_b)]
    gbig["ffn1_w1"] = _wgrad(da1, n1, "ffn1_dw1")
    gbig["ffn1_w3"] = _wgrad(db1, n1, "ffn1_dw3")
    gbig["ffn1_w2"] = _wgrad(s1, dh1, "ffn1_dw2", 0.5)
    grad_x = dh0[FRONT:][None]
    sg["meta_tokens"] = dh0[FRONT - NMETA:FRONT]
    sg["ffn1_norm"] = d_ffn1_norm

    pair_a = pair_sums(grp_a, "ffn1")
    late_all = _gather_all(_pack([sg[n] for n in late]), "gather_late_grads")
    sa_send, sa_recv, pair_a, land_s, sa_token = _chips_start(
        pair_a, [jax.ShapeDtypeStruct(p.shape, p.dtype) for p in pair_a], False, "scatter_ffn1_start", [late_all])

    out_g, out_d, out_m, out_v = {}, {}, {}, {}

    def finish(group, recvs, tag, after=None):
        halves = [None] * len(group)
        for idx in _by_shape(recvs):
            res = _sum_slots([recvs[i] for i in idx], "sum_" + group[idx[0]], after)
            for i, r in zip(idx, res):
                halves[i] = r
        for n, f in zip(group, _swap_halves(halves, "swap_" + tag)):
            g3 = f.reshape(1, f.shape[0] * f.shape[1], f.shape[2])
            g3, d3, m3, v3 = _adamw(view(args[n], n), g3, view(args["m_" + n], n), view(args["v_" + n], n),
                                "adamw_" + n)
            out_g[n], out_d[n], out_m[n], out_v[n] = (view(t, n) for t in (g3, d3, m3, v3))
            done.append(d3)

    done = []
    finish(grp_b + grp_c, list(recv_b) + list(recv_c), "mix_ffn2", sa_token)

    sgr = dict(zip(early, _unpack(_sum_slots([early_all], "sum_early", sa_token)[0], [sg[n].shape for n in early])))
    sgr.update(zip(late, _unpack(_sum_slots([late_all], "sum_late")[0], [sg[n].shape for n in late])))
    sgr["meta_tokens"] = lax.dynamic_slice_in_dim(sgr["meta_tokens"], chip * (D // NSH), D // NSH, axis=1)
    sgr["conv_dw"] = lax.dynamic_slice_in_dim(sgr["conv_dw"], chip * (DC // NSH), DC // NSH, axis=1)
    pshapes = [args[n].shape for n in small]
    _, d_s, m_s, v_s = _adamw(small_wmv[0], _pack([sgr[n] for n in small])[None], small_wmv[1], small_wmv[2],
                              "adamw_small")
    for n, g_, d_, m_, v_ in zip(small, [sgr[n] for n in small], _unpack(d_s[0], pshapes),
                                 _unpack(m_s[0], pshapes), _unpack(v_s[0], pshapes)):
        out_g[n], out_d[n], out_m[n], out_v[n] = g_.reshape(args[n].shape), d_, m_, v_

    loss = lax.psum(loss_part[0, 0], ("x", "y", "c"))
    pair_a, recv_a = _chips_wait(sa_send, sa_recv, pair_a, land_s, [d_s, grad_x] + done, False,
                                 "scatter_ffn1_wait")
    finish(grp_a, _fill_own(pair_a, recv_a, "own_ffn1"), "ffn1")
    return (loss, grad_x, *[out_g[n] for n in names], *[out_d[n] for n in names],
            *[out_m[n] for n in names], *[out_v[n] for n in names])
```

```python
import math

import jax
import jax.numpy as jnp
from jax import lax
from jax.experimental import pallas as pl
from jax.experimental.pallas import tpu as pltpu

F32 = jnp.float32
BF16 = jnp.bfloat16

D = 1024
NSH = 4
F = 2816
FS = F // NSH
DC = 512
DS = 512
DIN = 2 * DC + DS + 2 * D
WS = DIN // NSH
KW = 31
KWP = 32
CONV_ROWS = 64
NMETA = 16
FRONT = 128
G, P, H = 32, 64, 16
NST = G * P
NQ = 4
QS = NST // NQ
QU = DS // NQ
NSEG = 32
NGRP = NSEG // 8
NCH = 8
SOFF = 8
EPS = 1e-6
LR, B1, B2, AEPS, WD, STEP = 1e-3, 0.9, 0.999, 1e-8, 0.01, 10
VMEM_LIMIT = 58 * 1024 * 1024
MESH = pl.DeviceIdType.MESH
ANY = pl.BlockSpec(memory_space=pl.ANY)


def _params(*sem):
    return pltpu.CompilerParams(dimension_semantics=sem, vmem_limit_bytes=VMEM_LIMIT)


def _res(shape):
    nd = len(shape)
    return pl.BlockSpec(shape, lambda *_: (0,) * nd, pipeline_mode=pl.Buffered(1))


def _tile(n, cap, mult=16):
    best = None
    for t in range(mult, min(n, cap) + 1, mult):
        if n % t == 0:
            best = t
    assert best is not None, (n, cap, mult)
    return best


def _dot(a, b):
    return jnp.dot(a, b, preferred_element_type=F32)


def _dot_nt(a, b):
    return lax.dot_general(a, b, (((1,), (1,)), ((), ())), preferred_element_type=F32)


def _dot_tn(a, b):
    return lax.dot_general(a, b, (((0,), (0,)), ((), ())), preferred_element_type=F32)


def _sigmoid(x):
    return 1.0 / (1.0 + jnp.exp(-x))


_GC = math.sqrt(2.0 / math.pi)
_GA = 0.044715


def _gelu(x):
    return 0.5 * x * (1.0 + jnp.tanh(_GC * (x + _GA * x * x * x)))


def _gelu_grad(x):
    t = jnp.tanh(_GC * (x + _GA * x * x * x))
    return 0.5 * (1.0 + t) + 0.5 * x * (1.0 - t * t) * _GC * (1.0 + 3.0 * _GA * x * x)


def _rms(hv, g):
    r = lax.rsqrt(jnp.mean(hv * hv, axis=-1, keepdims=True) + EPS)
    return hv * r * g, r


def _rms_bwd(dn, hv, r, g):
    xh = hv * r
    dxh = dn * g
    return r * (dxh - xh * jnp.mean(dxh * xh, axis=-1, keepdims=True)), xh


def _acc_rows(ref, part, first):
    @pl.when(first)
    def _():
        ref[...] = part

    @pl.when(jnp.logical_not(first))
    def _():
        ref[...] += part


def _coords():
    return lax.axis_index("x"), lax.axis_index("y"), lax.axis_index("c")


def _flip(v, d):
    return 1 - v if d else v


def _run(local, remote):
    for cp in local + remote:
        cp.start()
    for cp in remote:
        cp.wait()
    for cp in local:
        cp.wait()


def _via_vmem(src, dst, stage, sems, i):
    return (pltpu.make_async_copy(src, stage, sems.at[2 * i]), pltpu.make_async_copy(stage, dst, sems.at[2 * i + 1]))


def _run_staged(staged, remote):
    for load, _ in staged:
        load.start()
    for cp in remote:
        cp.start()
    for load, store in staged:
        load.wait()
        store.start()
    for cp in remote:
        cp.wait()
    for _, store in staged:
        store.wait()


_REL3 = ((1, 0), (0, 1), (1, 1))


class _Behind:
    def __init__(self, arrays, out_shapes, scratch, build, alias_pairs=()):
        self.arrays, self.out_shapes, self.scratch, self.build = list(arrays), list(out_shapes), list(scratch), build
        self.alias_pairs = list(alias_pairs)

    def aliases(self):
        return self.alias_pairs

    def start(self, ins, outs, scr):
        staged, remote = self.build(ins, outs, scr)
        for load, _ in staged:
            load.start()
        for cp in remote:
            cp.start()

    def finish(self, ins, outs, scr):
        staged, remote = self.build(ins, outs, scr)
        for load, store in staged:
            load.wait()
            store.start()
        for cp in remote:
            cp.wait()
        for _, store in staged:
            store.wait()


def _call(body, comm, *, name, grid, in_specs, out_specs, out_shape, scratch_shapes=(), params):
    in_specs, out_specs, out_shape = list(in_specs), list(out_specs), list(out_shape)
    scratch_shapes = list(scratch_shapes)
    if comm is None:
        f = pl.pallas_call(body, name=name, grid=grid, in_specs=in_specs, out_specs=out_specs,
                           out_shape=out_shape, scratch_shapes=scratch_shapes, compiler_params=params)
        return lambda *args: (f(*args), [])
    ni, no, ns = len(in_specs), len(out_specs), len(scratch_shapes)
    ci, co = len(comm.arrays), len(comm.out_shapes)

    def hosted(*refs):
        ins, cin = refs[:ni], refs[ni:ni + ci]
        outs, cout = refs[ni + ci:ni + ci + no], refs[ni + ci + no:ni + ci + no + co]
        scr, cscr = refs[ni + ci + no + co:ni + ci + no + co + ns], refs[ni + ci + no + co + ns:]
        first = last = None
        for axis, size in enumerate(grid):
            i = pl.program_id(axis)
            first = (i == 0) if first is None else jnp.logical_and(first, i == 0)
            last = (i == size - 1) if last is None else jnp.logical_and(last, i == size - 1)

        @pl.when(first)
        def _():
            comm.start(cin, cout, cscr)

        body(*ins, *outs, *scr)

        @pl.when(last)
        def _():
            comm.finish(cin, cout, cscr)

    f = pl.pallas_call(hosted, name=name, grid=grid, in_specs=in_specs + [ANY] * ci,
                       out_specs=out_specs + [ANY] * co, out_shape=out_shape + comm.out_shapes,
                       scratch_shapes=scratch_shapes + comm.scratch,
                       input_output_aliases={ni + a: no + b for a, b in comm.aliases()},
                       compiler_params=_params(*(("arbitrary",) * len(grid))))

    def run(*args):
        res = f(*args, *comm.arrays)
        return res[:no], res[no:]

    return run


def _gather_half_behind(shards):
    n = len(shards)

    def build(ins, outs, scr):
        send, recv, loc = scr[:3]
        stage = scr[3:]
        x, y, c = _coords()
        me = 2 * x + y
        staged = [_via_vmem(ins[t], outs[t].at[me], stage[t], loc, t) for t in range(n)]
        remote = []
        for t in range(n):
            half = shards[t].shape[0] // 2
            mine = pl.ds(c * half, half)
            for k, (dx, dy) in enumerate(_REL3):
                remote.append(pltpu.make_async_remote_copy(
                    src_ref=ins[t].at[mine], dst_ref=outs[t].at[me, mine],
                    send_sem=send.at[3 * t + k], recv_sem=recv.at[3 * t + k],
                    device_id=(_flip(x, dx), _flip(y, dy), c), device_id_type=MESH))
        return staged, remote

    return _Behind(shards, [jax.ShapeDtypeStruct((NSH,) + s.shape, s.dtype) for s in shards],
                   [pltpu.SemaphoreType.DMA((3 * n,)), pltpu.SemaphoreType.DMA((3 * n,)),
                    pltpu.SemaphoreType.DMA((2 * n,))] + [pltpu.VMEM(s.shape, s.dtype) for s in shards], build)


def _pass_halves(gathered, name, own=()):
    n, m = len(gathered), len(own)

    def body(*refs):
        shards, outs = refs[n:n + m], refs[n + m:2 * n + m]
        send, recv, loc = refs[2 * n + m:2 * n + m + 3]
        stage = refs[2 * n + m + 3:]
        x, y, c = _coords()
        staged = [_via_vmem(shards[t], outs[t].at[2 * x + y], stage[t], loc, t) for t in range(m)]
        remote = []
        for t in range(n):
            half = gathered[t].shape[1] // 2
            mine = pl.ds(c * half, half)
            for k, (dx, dy) in enumerate(_REL3):
                slot = 2 * _flip(x, dx) + _flip(y, dy)
                remote.append(pltpu.make_async_remote_copy(
                    src_ref=outs[t].at[slot, mine], dst_ref=outs[t].at[slot, mine],
                    send_sem=send.at[3 * t + k], recv_sem=recv.at[3 * t + k],
                    device_id=(x, y, 1 - c), device_id_type=MESH))
        _run_staged(staged, remote)

    return pl.pallas_call(
        body, name=name,
        out_shape=[jax.ShapeDtypeStruct(g.shape, g.dtype) for g in gathered],
        in_specs=[ANY] * (n + m), out_specs=[ANY] * n, input_output_aliases={t: t for t in range(n)},
        scratch_shapes=[pltpu.SemaphoreType.DMA((3 * n,)), pltpu.SemaphoreType.DMA((3 * n,)),
                        pltpu.SemaphoreType.DMA((max(2 * m, 1),))] + [pltpu.VMEM(s.shape, s.dtype) for s in own],
        compiler_params=pltpu.CompilerParams(vmem_limit_bytes=VMEM_LIMIT),
    )(*gathered, *own)


def _fill_own(sums, recvs, name):
    n = len(sums)

    def body(*refs):
        ins, outs = refs[:n], refs[2 * n:3 * n]
        loc = refs[3 * n]
        stage = refs[3 * n + 1:]
        x, y, _ = _coords()
        me = 2 * x + y
        _run_staged([_via_vmem(ins[t].at[me], outs[t].at[me], stage[t], loc, t) for t in range(n)], [])

    return pl.pallas_call(
        body, name=name,
        out_shape=[jax.ShapeDtypeStruct(r.shape, r.dtype) for r in recvs],
        in_specs=[ANY] * (2 * n), out_specs=[ANY] * n, input_output_aliases={n + t: t for t in range(n)},
        scratch_shapes=[pltpu.SemaphoreType.DMA((2 * n,))] + [pltpu.VMEM(s.shape[1:], s.dtype) for s in sums],
        compiler_params=pltpu.CompilerParams(vmem_limit_bytes=VMEM_LIMIT),
    )(*sums, *recvs)


def _scatter_chips_behind(sums):
    n = len(sums)

    def build(ins, outs, scr):
        send, recv, loc = scr[:3]
        stage = scr[3:]
        x, y, c = _coords()
        me = 2 * x + y
        staged = [_via_vmem(ins[t].at[me], outs[t].at[me], stage[t], loc, t) for t in range(n)]
        remote = []
        for t in range(n):
            for k, (dx, dy) in enumerate(_REL3):
                px, py = _flip(x, dx), _flip(y, dy)
                remote.append(pltpu.make_async_remote_copy(
                    src_ref=ins[t].at[2 * px + py], dst_ref=outs[t].at[me],
                    send_sem=send.at[3 * t + k], recv_sem=recv.at[3 * t + k],
                    device_id=(px, py, c), device_id_type=MESH))
        return staged, remote

    return _Behind(sums, [jax.ShapeDtypeStruct(s.shape, s.dtype) for s in sums],
                   [pltpu.SemaphoreType.DMA((3 * n,)), pltpu.SemaphoreType.DMA((3 * n,)),
                    pltpu.SemaphoreType.DMA((2 * n,))] + [pltpu.VMEM(s.shape[1:], s.dtype) for s in sums], build)


def _gather_all_behind(a):
    def build(ins, outs, scr):
        send, recv, loc, stage = scr
        x, y, c = _coords()
        me = 4 * x + 2 * y + c
        staged = [_via_vmem(ins[0], outs[0].at[me], stage, loc, 0)]
        remote = [pltpu.make_async_remote_copy(
            src_ref=ins[0], dst_ref=outs[0].at[me], send_sem=send.at[k], recv_sem=recv.at[k],
            device_id=(_flip(x, dx), _flip(y, dy), _flip(c, dc)), device_id_type=MESH)
            for k, (dx, dy, dc) in enumerate(_REL7)]
        return staged, remote

    return _Behind([a], [jax.ShapeDtypeStruct((8,) + a.shape, a.dtype)],
                   [pltpu.SemaphoreType.DMA((7,)), pltpu.SemaphoreType.DMA((7,)), pltpu.SemaphoreType.DMA((2,)),
                    pltpu.VMEM(a.shape, a.dtype)], build)


HBM = pl.BlockSpec(memory_space=pltpu.HBM)
SEM = pl.BlockSpec(memory_space=pltpu.SEMAPHORE)
EFFECT = pltpu.SideEffectType.DATAFLOW_SIDE_EFFECTING


def _chip_copies(srcs, lands, send, recv, gather):
    x, y, c = _coords()
    me = 2 * x + y
    cps = []
    for t in range(len(srcs)):
        for k, (dx, dy) in enumerate(_REL3):
            px, py = _flip(x, dx), _flip(y, dy)
            if gather:
                half = srcs[t].shape[0] // 2
                mine = pl.ds(c * half, half)
                src, dst = srcs[t].at[mine], lands[t].at[me, mine]
            else:
                src, dst = srcs[t].at[2 * px + py], lands[t].at[me]
            cps.append(pltpu.make_async_remote_copy(
                src_ref=src, dst_ref=dst, send_sem=send.at[3 * t + k], recv_sem=recv.at[3 * t + k],
                device_id=(px, py, c), device_id_type=MESH))
    return cps


def _chips_start(arrays, land_shapes, gather, name, after=()):
    n = len(arrays)

    def body(*refs):
        srcs, lands = refs[:n], refs[n:2 * n]
        send, recv = refs[2 * n + len(after)], refs[2 * n + len(after) + 1]
        token = refs[-1]
        for cp in _chip_copies(srcs, lands, send, recv, gather):
            cp.start()
        token[...] = jnp.zeros_like(token)

    lands = [lax.empty(s.shape, s.dtype) for s in land_shapes]
    thru = [pltpu.HBM(a.shape, a.dtype) for a in arrays] + [pltpu.HBM(s.shape, s.dtype) for s in land_shapes]
    res = pl.pallas_call(
        body, name=name,
        out_shape=(pltpu.SemaphoreType.DMA((3 * n,)), pltpu.SemaphoreType.DMA((3 * n,)), *thru,
                   jax.ShapeDtypeStruct((8, 128), F32)),
        in_specs=[HBM] * (2 * n) + [ANY] * len(after),
        out_specs=(SEM, SEM, *([HBM] * (2 * n)), pl.BlockSpec(memory_space=pltpu.VMEM)),
        input_output_aliases={t: 2 + t for t in range(2 * n)},
        compiler_params=pltpu.CompilerParams(has_side_effects=EFFECT),
    )(*[pltpu.with_memory_space_constraint(a, pltpu.HBM) for a in arrays],
      *[pltpu.with_memory_space_constraint(z, pltpu.HBM) for z in lands], *after)
    return res[0], res[1], list(res[2:2 + n]), list(res[2 + n:2 + 2 * n]), res[-1]


def _chips_wait(send, recv, arrays, lands, after, gather, name):
    n = len(arrays)

    def body(*refs):
        srcs, ls = refs[:n], refs[n:2 * n]
        sd, rv = refs[2 * n], refs[2 * n + 1]
        for cp in _chip_copies(srcs, ls, sd, rv, gather):
            cp.wait_send()
            cp.wait_recv()

    res = pl.pallas_call(
        body, name=name,
        out_shape=[pltpu.HBM(a.shape, a.dtype) for a in arrays] + [pltpu.HBM(z.shape, z.dtype) for z in lands],
        in_specs=[HBM] * (2 * n) + [SEM, SEM] + [ANY] * len(after), out_specs=[HBM] * (2 * n),
        input_output_aliases={t: t for t in range(2 * n)},
        compiler_params=pltpu.CompilerParams(has_side_effects=EFFECT),
    )(*arrays, *lands, send, recv, *after)
    return list(res[:n]), list(res[n:])


def _join(*parts):
    def cut(seq, key):
        res, o = [], 0
        for p in parts:
            k = len(getattr(p, key))
            res.append(seq[o:o + k])
            o += k
        return res

    def build(ins, outs, scr):
        staged, remote = [], []
        for p, i, o, s in zip(parts, cut(ins, "arrays"), cut(outs, "out_shapes"), cut(scr, "scratch")):
            st, rm = p.build(i, o, s)
            staged += st
            remote += rm
        return staged, remote

    pairs, ai, oi = [], 0, 0
    for p in parts:
        pairs += [(ai + a, oi + b) for a, b in p.alias_pairs]
        ai, oi = ai + len(p.arrays), oi + len(p.out_shapes)
    return _Behind(sum((p.arrays for p in parts), []), sum((p.out_shapes for p in parts), []),
                   sum((p.scratch for p in parts), []), build, pairs)


def _pass_halves_behind(gathered):
    n = len(gathered)

    def build(ins, outs, scr):
        send, recv = scr
        x, y, c = _coords()
        remote = []
        for t in range(n):
            half = gathered[t].shape[1] // 2
            mine = pl.ds(c * half, half)
            for k, (dx, dy) in enumerate(_REL3):
                slot = 2 * _flip(x, dx) + _flip(y, dy)
                remote.append(pltpu.make_async_remote_copy(
                    src_ref=outs[t].at[slot, mine], dst_ref=outs[t].at[slot, mine],
                    send_sem=send.at[3 * t + k], recv_sem=recv.at[3 * t + k],
                    device_id=(x, y, 1 - c), device_id_type=MESH))
        return [], remote

    return _Behind(gathered, [jax.ShapeDtypeStruct(g.shape, g.dtype) for g in gathered],
                   [pltpu.SemaphoreType.DMA((3 * n,)), pltpu.SemaphoreType.DMA((3 * n,))], build,
                   [(t, t) for t in range(n)])


_REL7 = tuple((dx, dy, dc) for dx in (0, 1) for dy in (0, 1) for dc in (0, 1))[1:]


def _gather_all(a, name):
    def body(a_ref, o_ref, send, recv, loc):
        x, y, c = _coords()
        me = 4 * x + 2 * y + c
        local = [pltpu.make_async_copy(a_ref, o_ref.at[me], loc.at[0])]
        remote = [pltpu.make_async_remote_copy(
            src_ref=a_ref, dst_ref=o_ref.at[me], send_sem=send.at[k], recv_sem=recv.at[k],
            device_id=(_flip(x, dx), _flip(y, dy), _flip(c, dc)), device_id_type=MESH)
            for k, (dx, dy, dc) in enumerate(_REL7)]
        _run(local, remote)

    return pl.pallas_call(
        body, name=name,
        out_shape=jax.ShapeDtypeStruct((8,) + a.shape, a.dtype),
        in_specs=[ANY], out_specs=ANY,
        scratch_shapes=[pltpu.SemaphoreType.DMA((7,)), pltpu.SemaphoreType.DMA((7,)),
                        pltpu.SemaphoreType.DMA((1,))],
    )(a)


def _pair_exchange(grads, name):
    n = len(grads)

    def body(*refs):
        ins, outs = refs[:n], refs[n:2 * n]
        send, recv = refs[2 * n:]
        x, y, c = _coords()
        remote = []
        for t in range(n):
            half = grads[t].shape[1] // 2
            remote.append(pltpu.make_async_remote_copy(
                src_ref=ins[t].at[:, pl.ds((1 - c) * half, half)], dst_ref=outs[t],
                send_sem=send.at[t], recv_sem=recv.at[t],
                device_id=(x, y, 1 - c), device_id_type=MESH))
        _run([], remote)

    return pl.pallas_call(
        body, name=name,
        out_shape=[jax.ShapeDtypeStruct((NSH, g.shape[1] // 2, g.shape[2]), g.dtype) for g in grads],
        in_specs=[ANY] * n, out_specs=[ANY] * n,
        scratch_shapes=[pltpu.SemaphoreType.DMA((n,)), pltpu.SemaphoreType.DMA((n,))],
    )(*grads)


def _swap_halves(halves, name):
    n = len(halves)

    def body(*refs):
        ins, outs = refs[:n], refs[n:2 * n]
        send, recv, loc = refs[2 * n:2 * n + 3]
        stage = refs[2 * n + 3:]
        x, y, c = _coords()
        local = [_via_vmem(ins[t], outs[t].at[c], stage[t], loc, t) for t in range(n)]
        remote = [pltpu.make_async_remote_copy(
            src_ref=ins[t], dst_ref=outs[t].at[c], send_sem=send.at[t], recv_sem=recv.at[t],
            device_id=(x, y, 1 - c), device_id_type=MESH) for t in range(n)]
        _run_staged(local, remote)

    return pl.pallas_call(
        body, name=name,
        out_shape=[jax.ShapeDtypeStruct((2,) + h.shape, h.dtype) for h in halves],
        in_specs=[ANY] * n, out_specs=[ANY] * n,
        scratch_shapes=[pltpu.SemaphoreType.DMA((n,)), pltpu.SemaphoreType.DMA((n,)),
                        pltpu.SemaphoreType.DMA((2 * n,))]
        + [pltpu.VMEM(h.shape, h.dtype) for h in halves],
        compiler_params=pltpu.CompilerParams(vmem_limit_bytes=VMEM_LIMIT),
    )(*halves)


def _by_shape(arrays):
    groups = {}
    for i, a in enumerate(arrays):
        groups.setdefault((a.shape, a.dtype), []).append(i)
    return list(groups.values())


def _sum_slots(rs, name, after=None):
    n = len(rs)
    K, R, C = rs[0].shape
    tr = _tile(R, max(16, (1 << 22) // (n * K * C)), 8 * (4 // rs[0].dtype.itemsize))

    def body(*refs):
        for r_ref, o_ref in zip(refs[:n], refs[len(refs) - n:]):
            acc = r_ref[0].astype(F32)
            for k in range(1, K):
                acc = acc + r_ref[k].astype(F32)
            o_ref[...] = acc

    dep = [] if after is None else [after]
    return pl.pallas_call(
        body, name=name, grid=(R // tr,),
        out_shape=[jax.ShapeDtypeStruct((R, C), F32)] * n,
        in_specs=[pl.BlockSpec((K, tr, C), lambda i: (0, i, 0))] * n + [ANY] * len(dep),
        out_specs=[pl.BlockSpec((tr, C), lambda i: (i, 0))] * n,
        compiler_params=_params("parallel"),
    )(*rs, *dep)


def _add_pair(gs, ss, core, name):
    n = len(gs)
    _, half, C = ss[0].shape
    tr = _tile(half, max(16, (1 << 21) // (n * C)))
    nb = half // tr

    def body(c_ref, *refs):
        for g_ref, s_ref, o_ref in zip(refs[:n], refs[n:2 * n], refs[2 * n:]):
            o_ref[...] = (g_ref[...].astype(F32) + s_ref[...].astype(F32)).astype(BF16)

    spec = pl.BlockSpec((1, tr, C), lambda j, i, c_ref: (j, i, 0))
    return pl.pallas_call(
        body, name=name,
        grid_spec=pltpu.PrefetchScalarGridSpec(
            num_scalar_prefetch=1, grid=(NSH, nb),
            in_specs=[pl.BlockSpec((1, tr, C), lambda j, i, c_ref: (j, c_ref[0] * nb + i, 0))] * n + [spec] * n,
            out_specs=[spec] * n),
        out_shape=[jax.ShapeDtypeStruct(ss[0].shape, BF16)] * n,
        compiler_params=_params("parallel", "parallel"),
    )(core, *gs, *ss)


def _adamw(wgmv, name):
    n = len(wgmv)
    _, R, C = wgmv[0][0].shape
    tr = _tile(R, max(8, (1 << 18) // (n * C)), 8)
    c1 = 1.0 / (1.0 - B1 ** STEP)
    c2 = 1.0 / (1.0 - B2 ** STEP)

    def body(*refs):
        for t in range(n):
            w_ref, g_ref, m_ref, v_ref = refs[4 * t:4 * t + 4]
            go_ref, d_ref, nm_ref, nv_ref = refs[4 * n + 4 * t:4 * n + 4 * t + 4]
            gv = g_ref[...]
            go_ref[...] = gv
            nm = B1 * m_ref[...] + (1.0 - B1) * gv
            nv = B2 * v_ref[...] + (1.0 - B2) * gv * gv
            nm_ref[...] = nm
            nv_ref[...] = nv
            d_ref[...] = -LR * ((nm * c1) / (jnp.sqrt(nv * c2) + AEPS) + WD * w_ref[...])

    spec = pl.BlockSpec((1, tr, C), lambda i: (0, i, 0))
    res = pl.pallas_call(
        body, name=name, grid=(R // tr,),
        out_shape=[jax.ShapeDtypeStruct((1, R, C), F32)] * (4 * n),
        in_specs=[spec] * (4 * n), out_specs=[spec] * (4 * n),
        compiler_params=_params("parallel"),
    )(*[a for four in wgmv for a in four])
    return [tuple(res[4 * t:4 * t + 4]) for t in range(n)]


def _ffn_fwd(h, g, w1, w3, w2, name, comm=None):
    L = h.shape[0]
    tm = _tile(L, 704)

    def body(h_ref, g_ref, w1_ref, w3_ref, w2_ref, o_ref, a_ref, b_ref, n_s, acc_s):
        j = pl.program_id(1)

        @pl.when(j == 0)
        def _():
            hv = h_ref[...]
            n, _ = _rms(hv, g_ref[...])
            n_s[...] = n.astype(BF16)
            acc_s[...] = hv

        n = n_s[...]
        a = _dot_nt(n, w1_ref[0])
        b = _dot_nt(n, w3_ref[0])
        a_ref[0] = a.astype(BF16)
        b_ref[0] = b.astype(BF16)
        s = (a * _sigmoid(a) * b).astype(BF16)
        acc_s[...] += 0.5 * _dot(s, w2_ref[0])

        @pl.when(j == NSH - 1)
        def _():
            o_ref[...] = acc_s[...]

    row = pl.BlockSpec((tm, D), lambda i, j: (i, 0))
    hid = pl.BlockSpec((1, tm, FS), lambda i, j: (j, i, 0))
    wsp = pl.BlockSpec((1, FS, D), lambda i, j: (j, 0, 0))
    return _call(
        body, comm, name=name, grid=(L // tm, NSH),
        out_shape=[jax.ShapeDtypeStruct((L, D), F32),
                   jax.ShapeDtypeStruct((NSH, L, FS), BF16), jax.ShapeDtypeStruct((NSH, L, FS), BF16)],
        in_specs=[row, _res((1, D)), wsp, wsp, wsp],
        out_specs=[row, hid, hid],
        scratch_shapes=[pltpu.VMEM((tm, D), BF16), pltpu.VMEM((tm, D), F32)],
        params=_params("arbitrary", "arbitrary"),
    )(h, g, w1, w3, w2)


def _loss_head(hv, gv, tv, row0):
    y, r = _rms(hv, gv)
    row = row0 + lax.broadcasted_iota(jnp.int32, (hv.shape[0], 1), 0)
    e = jnp.where(row >= FRONT, y - tv, 0.0)
    dy = e * (1.0 / D)
    part = 0.5 * jnp.sum(jnp.sum(e * dy, axis=1, keepdims=True), axis=0, keepdims=True)
    dx, xh = _rms_bwd(dy, hv, r, gv)
    return dx, part, jnp.sum(dy * xh, axis=0, keepdims=True)


def _ffn_fwd_loss(h, g, w1, w3, w2, gf, tgt, name):
    L = h.shape[0]
    tm = _tile(L, 704)

    def body(h_ref, g_ref, w1_ref, w3_ref, w2_ref, gf_ref, t_ref, o_ref, a_ref, b_ref, loss_ref, dgf_ref,
             n_s, acc_s):
        i, j = pl.program_id(0), pl.program_id(1)

        @pl.when(j == 0)
        def _():
            hv = h_ref[...]
            n, _ = _rms(hv, g_ref[...])
            n_s[...] = n.astype(BF16)
            acc_s[...] = hv

        n = n_s[...]
        a = _dot_nt(n, w1_ref[0])
        b = _dot_nt(n, w3_ref[0])
        a_ref[0] = a.astype(BF16)
        b_ref[0] = b.astype(BF16)
        s = (a * _sigmoid(a) * b).astype(BF16)
        acc_s[...] += 0.5 * _dot(s, w2_ref[0])

        @pl.when(j == NSH - 1)
        def _():
            dx, part, dgf = _loss_head(acc_s[...], gf_ref[...], t_ref[...], i * tm)
            o_ref[...] = dx
            _acc_rows(loss_ref, part, i == 0)
            _acc_rows(dgf_ref, dgf, i == 0)

    row = pl.BlockSpec((tm, D), lambda i, j: (i, 0))
    hid = pl.BlockSpec((1, tm, FS), lambda i, j: (j, i, 0))
    wsp = pl.BlockSpec((1, FS, D), lambda i, j: (j, 0, 0))
    return pl.pallas_call(
        body, name=name, grid=(L // tm, NSH),
        out_shape=[jax.ShapeDtypeStruct((L, D), F32),
                   jax.ShapeDtypeStruct((NSH, L, FS), BF16), jax.ShapeDtypeStruct((NSH, L, FS), BF16),
                   jax.ShapeDtypeStruct((1, 1), F32), jax.ShapeDtypeStruct((1, D), F32)],
        in_specs=[row, _res((1, D)), wsp, wsp, wsp, _res((1, D)), row],
        out_specs=[row, hid, hid, pl.BlockSpec((1, 1), lambda i, j: (0, 0)),
                   pl.BlockSpec((1, D), lambda i, j: (0, 0))],
        scratch_shapes=[pltpu.VMEM((tm, D), BF16), pltpu.VMEM((tm, D), F32)],
        compiler_params=_params("arbitrary", "arbitrary"),
    )(h, g, w1, w3, w2, gf, tgt)


def _ffn_bwd(h, g, dout, a, b, w1, w3, w2, name, comm=None):
    L = h.shape[0]
    tm = _tile(L, 528)

    def body(h_ref, g_ref, do_ref, a_ref, b_ref, w1_ref, w3_ref, w2_ref,
             dh_ref, da_ref, db_ref, s_ref, n_ref, dg_ref, dob_s, dn_s):
        i, j = pl.program_id(0), pl.program_id(1)

        @pl.when(j == 0)
        def _():
            n, _ = _rms(h_ref[...], g_ref[...])
            n_ref[...] = n.astype(BF16)
            dob_s[...] = (0.5 * do_ref[...]).astype(BF16)
            dn_s[...] = jnp.zeros_like(dn_s)

        av = a_ref[0].astype(F32)
        bv = b_ref[0].astype(F32)
        sig = _sigmoid(av)
        sa = av * sig
        ds = _dot_nt(dob_s[...], w2_ref[0])
        s_ref[0] = (sa * bv).astype(BF16)
        da = (ds * bv * (sig + sa * (1.0 - sig))).astype(BF16)
        db = (ds * sa).astype(BF16)
        da_ref[0] = da
        db_ref[0] = db
        dn_s[...] += _dot(da, w1_ref[0]) + _dot(db, w3_ref[0])

        @pl.when(j == NSH - 1)
        def _():
            hv = h_ref[...]
            gv = g_ref[...]
            r = lax.rsqrt(jnp.mean(hv * hv, axis=-1, keepdims=True) + EPS)
            dn = dn_s[...]
            dx, xh = _rms_bwd(dn, hv, r, gv)
            dh_ref[...] = do_ref[...] + dx
            _acc_rows(dg_ref, jnp.sum(dn * xh, axis=0, keepdims=True), i == 0)

    row = pl.BlockSpec((tm, D), lambda i, j: (i, 0))
    hid = pl.BlockSpec((1, tm, FS), lambda i, j: (j, i, 0))
    wsp = pl.BlockSpec((1, FS, D), lambda i, j: (j, 0, 0))
    return _call(
        body, comm, name=name, grid=(L // tm, NSH),
        out_shape=[jax.ShapeDtypeStruct((L, D), F32)]
        + [jax.ShapeDtypeStruct((NSH, L, FS), BF16)] * 3
        + [jax.ShapeDtypeStruct((L, D), BF16), jax.ShapeDtypeStruct((1, D), F32)],
        in_specs=[row, _res((1, D)), row, hid, hid,
                  wsp, wsp, wsp],
        out_specs=[row, hid, hid, hid, row, pl.BlockSpec((1, D), lambda i, j: (0, 0))],
        scratch_shapes=[pltpu.VMEM((tm, D), BF16), pltpu.VMEM((tm, D), F32)],
        params=_params("arbitrary", "arbitrary"),
    )(h, g, dout, a, b, w1, w3, w2)


def _wgrad(xm, ym, name, scale=1.0):
    xs, ys = xm.ndim == 3, ym.ndim == 3
    assert not (xs and ys)
    L = xm.shape[-2]
    K, N = xm.shape[-1], ym.shape[-1]
    tl = _tile(L, 2112)
    nl = L // tl
    if xs or ys:
        tn, grid_n = N, NSH
    else:
        tn = _tile(N, 1024, 128)
        grid_n = N // tn

    def body(x_ref, y_ref, o_ref, acc_s):
        l = pl.program_id(1)
        xv = x_ref[0] if xs else x_ref[...]
        yv = y_ref[0] if ys else y_ref[...]
        part = _dot_tn(xv.astype(BF16), yv.astype(BF16))
        _acc_rows(acc_s, part, l == 0)

        @pl.when(l == nl - 1)
        def _():
            res = (acc_s[...] * scale).astype(BF16)
            if xs or ys:
                o_ref[0] = res
            else:
                o_ref[...] = res

    if xs:
        x_spec = pl.BlockSpec((1, tl, K), lambda n, l: (n, l, 0))
        y_spec = pl.BlockSpec((tl, N), lambda n, l: (l, 0))
        o_spec = pl.BlockSpec((1, K, N), lambda n, l: (n, 0, 0))
        o_shape = (NSH, K, N)
    elif ys:
        x_spec = pl.BlockSpec((tl, K), lambda n, l: (l, 0))
        y_spec = pl.BlockSpec((1, tl, N), lambda n, l: (n, l, 0))
        o_spec = pl.BlockSpec((1, K, N), lambda n, l: (n, 0, 0))
        o_shape = (NSH, K, N)
    else:
        x_spec = pl.BlockSpec((tl, K), lambda n, l: (l, 0))
        y_spec = pl.BlockSpec((tl, tn), lambda n, l: (l, n))
        o_spec = pl.BlockSpec((K, tn), lambda n, l: (0, n))
        o_shape = (K, N)
    return pl.pallas_call(
        body, name=name, grid=(grid_n, nl),
        out_shape=jax.ShapeDtypeStruct(o_shape, BF16),
        in_specs=[x_spec, y_spec], out_specs=o_spec,
        scratch_shapes=[pltpu.VMEM((K, tn), F32)],
        compiler_params=_params("parallel", "arbitrary"),
    )(xm, ym)


def _mix_in_fwd(h, g, w_in, b_gate, name, comm=None):
    L = h.shape[0]
    tm = _tile(L, 528)

    def body(h_ref, g_ref, w_ref, bg_ref, vg_ref, uf_ref, gt_ref):
        u, _ = _rms(h_ref[...], g_ref[...])
        ub = u.astype(BF16)
        p = [_dot(ub, w_ref[j]) for j in range(NSH)]
        a0, a1 = 2 * DC - WS, 2 * DC + DS - WS
        vg_ref[:, 0:WS] = p[0].astype(BF16)
        vg_ref[:, WS:2 * DC] = p[1][:, 0:a0].astype(BF16)
        uf_ref[...] = p[1][:, a0:a1].astype(BF16)
        gin = jnp.concatenate([p[1][:, a1:], p[2], p[3]], axis=1)
        gt_ref[...] = _sigmoid(gin + bg_ref[...]).astype(BF16)

    def row(n):
        return pl.BlockSpec((tm, n), lambda i: (i, 0))

    return _call(
        body, comm, name=name, grid=(L // tm,),
        out_shape=[jax.ShapeDtypeStruct((L, 2 * DC), BF16), jax.ShapeDtypeStruct((L, DS), BF16),
                   jax.ShapeDtypeStruct((L, 2 * D), BF16)],
        in_specs=[row(D), _res((1, D)), _res((NSH, D, WS)), _res((1, 2 * D))],
        out_specs=[row(2 * DC), row(DS), row(2 * D)],
        params=_params("parallel"),
    )(h, g, w_in, b_gate)


def _mix_in_bwd(h, g, dres, dv, dgl, duf, dgate, w_in, name):
    L = h.shape[0]
    tm = _tile(L, 528)

    def body(h_ref, g_ref, dr_ref, dv_ref, dgl_ref, duf_ref, dgt_ref, w_ref, dh_ref, u_ref, dp_ref, dgm_ref):
        i = pl.program_id(0)
        hv = h_ref[...]
        gv = g_ref[...]
        u, r = _rms(hv, gv)
        u_ref[...] = u.astype(BF16)
        a0, a1 = 2 * DC - WS, 2 * DC + DS - WS
        b0 = WS - a1
        dp = [jnp.concatenate([dv_ref[...], dgl_ref[:, 0:WS - DC]], axis=1),
              jnp.concatenate([dgl_ref[:, WS - DC:], duf_ref[...], dgt_ref[:, 0:b0]], axis=1),
              dgt_ref[:, b0:b0 + WS], dgt_ref[:, b0 + WS:]]
        du = jnp.zeros((tm, D), F32)
        for j in range(NSH):
            dp_ref[j] = dp[j]
            du = du + _dot_nt(dp[j], w_ref[j])
        dx, xh = _rms_bwd(du, hv, r, gv)
        dh_ref[...] = dr_ref[...] + dx
        _acc_rows(dgm_ref, jnp.sum(du * xh, axis=0, keepdims=True), i == 0)

    def row(n):
        return pl.BlockSpec((tm, n), lambda i: (i, 0))

    return pl.pallas_call(
        body, name=name, grid=(L // tm,),
        out_shape=[jax.ShapeDtypeStruct((L, D), F32), jax.ShapeDtypeStruct((L, D), BF16),
                   jax.ShapeDtypeStruct((NSH, L, WS), BF16), jax.ShapeDtypeStruct((1, D), F32)],
        in_specs=[row(D), _res((1, D)), row(D), row(DC), row(DC), row(DS), row(2 * D), _res((NSH, D, WS))],
        out_specs=[row(D), row(D), pl.BlockSpec((NSH, tm, WS), lambda i: (0, i, 0)),
                   pl.BlockSpec((1, D), lambda i: (0, 0))],
        compiler_params=_params("arbitrary"),
    )(h, g, dres, dv, dgl, duf, dgate, w_in)


def _conv_fwd(vg, dw, dwb, name, comm=None):
    L = vg.shape[0]
    nc = DC // 128

    def body(v_ref, g_ref, dw_ref, dwb_ref, z_ref, zp_s):
        zp_s[0:KWP, :] = jnp.zeros((KWP, 128), F32)
        zp_s[KWP:, :] = v_ref[...].astype(F32) * _sigmoid(g_ref[...].astype(F32))
        for r0 in range(0, L, CONV_ROWS):
            acc = jnp.broadcast_to(dwb_ref[...], (CONV_ROWS, 128))
            for k in range(KW):
                acc = acc + dw_ref[k:k + 1, :] * zp_s[pl.ds(r0 + k + 2, CONV_ROWS), :]
            z_ref[pl.ds(r0, CONV_ROWS), :] = acc

    return _call(
        body, comm, name=name, grid=(nc,),
        out_shape=[jax.ShapeDtypeStruct((L, DC), F32)],
        in_specs=[pl.BlockSpec((L, 128), lambda c: (0, c)), pl.BlockSpec((L, 128), lambda c: (0, nc + c)),
                  pl.BlockSpec((KWP, 128), lambda c: (0, c)), pl.BlockSpec((1, 128), lambda c: (0, c))],
        out_specs=[pl.BlockSpec((L, 128), lambda c: (0, c))],
        scratch_shapes=[pltpu.VMEM((L + KWP, 128), F32)],
        params=_params("parallel"),
    )(vg, vg, dw, dwb)


def _conv_bwd(dz1, vg, dw, name):
    L = vg.shape[0]
    nc = DC // 128

    def body(dz_ref, v_ref, g_ref, dw_ref, dv_ref, dg_ref, ddw_ref, ddwb_ref, zp_s, dzp_s):
        vv = v_ref[...].astype(F32)
        sg = _sigmoid(g_ref[...].astype(F32))
        zp_s[0:KWP, :] = jnp.zeros((KWP, 128), F32)
        zp_s[KWP:, :] = vv * sg
        dz = dz_ref[...]
        dzp_s[0:L, :] = dz
        dzp_s[L:, :] = jnp.zeros((KWP, 128), F32)
        ddwb_ref[...] = jnp.sum(dz, axis=0, keepdims=True)
        part = [jnp.zeros((8, 128), F32) for _ in range(KW)]
        for r0 in range(0, L, CONV_ROWS):
            rows = pl.ds(r0, CONV_ROWS)
            dzc = dz_ref[rows, :]
            acc = jnp.zeros((CONV_ROWS, 128), F32)
            for k in range(KW):
                acc = acc + dw_ref[k:k + 1, :] * dzp_s[pl.ds(r0 + KW - 1 - k, CONV_ROWS), :]
                prod = dzc * zp_s[pl.ds(r0 + k + 2, CONV_ROWS), :]
                for q in range(CONV_ROWS // 8):
                    part[k] = part[k] + prod[8 * q:8 * (q + 1), :]
            vc = v_ref[rows, :].astype(F32)
            sc = _sigmoid(g_ref[rows, :].astype(F32))
            dv_ref[rows, :] = (acc * sc).astype(BF16)
            dg_ref[rows, :] = (acc * vc * sc * (1.0 - sc)).astype(BF16)
        for k in range(KW):
            ddw_ref[k:k + 1, :] = jnp.sum(part[k], axis=0, keepdims=True)
        ddw_ref[KW:KWP, :] = jnp.zeros((KWP - KW, 128), F32)

    col = pl.BlockSpec((L, 128), lambda c: (0, c))
    return pl.pallas_call(
        body, name=name, grid=(nc,),
        out_shape=[jax.ShapeDtypeStruct((L, DC), BF16), jax.ShapeDtypeStruct((L, DC), BF16),
                   jax.ShapeDtypeStruct((KWP, DC), F32), jax.ShapeDtypeStruct((1, DC), F32)],
        in_specs=[col, col, pl.BlockSpec((L, 128), lambda c: (0, nc + c)),
                  pl.BlockSpec((KWP, 128), lambda c: (0, c))],
        out_specs=[col, col, pl.BlockSpec((KWP, 128), lambda c: (0, c)), pl.BlockSpec((1, 128), lambda c: (0, c))],
        scratch_shapes=[pltpu.VMEM((L + KWP, 128), F32), pltpu.VMEM((L + KWP, 128), F32)],
        compiler_params=_params("parallel"),
    )(dz1, vg, vg, dw)


NLB = QS // 128


def _lb_store(ref, rows, val):
    for cb in range(NLB):
        ref[cb, rows, :] = val[:, cb * 128:(cb + 1) * 128]


def _lb_load(ref, rows):
    return jnp.concatenate([ref[cb, rows, :] for cb in range(NLB)], axis=1)


def _scan(xr_ref, xi_ref, base, T, ar, ai, atr, ati, reverse):
    W = ar.shape[1]
    ar, ai, atr, ati = (jnp.broadcast_to(v, (8, W)) for v in (ar, ai, atr, ati))
    zero = jnp.zeros((8, W), F32)

    def rows(t, g):
        tt = T - 1 - t if reverse else t
        return pl.ds(base + g * 8 * T + tt, 8, stride=T)

    def make_step(store):
        def step(t, carry):
            out = []
            for g in range(NGRP):
                sr, si = carry[2 * g], carry[2 * g + 1]
                idx = rows(t, g)
                nr = ar * sr - ai * si + _lb_load(xr_ref, idx)
                ni = ar * si + ai * sr + _lb_load(xi_ref, idx)
                if store:
                    _lb_store(xr_ref, idx, nr)
                    _lb_store(xi_ref, idx, ni)
                out += [nr, ni]
            return tuple(out)
        return step

    ends = lax.fori_loop(0, T, make_step(False), (zero,) * (2 * NGRP))
    sub = lax.broadcasted_iota(jnp.int32, (8, W), 0)
    edge = sub == (7 if reverse else 0)
    shift, last = (7, 0) if reverse else (1, 7)
    inr, ini = jnp.zeros((1, W), F32), jnp.zeros((1, W), F32)
    starts = [None] * (2 * NGRP)
    for g in (reversed(range(NGRP)) if reverse else range(NGRP)):
        er, ei = ends[2 * g], ends[2 * g + 1]
        cr, ci = jnp.where(edge, inr, 0.0), jnp.where(edge, ini, 0.0)
        for _ in range(7):
            nr = atr * cr - ati * ci + er
            ni = atr * ci + ati * cr + ei
            cr = jnp.where(edge, inr, pltpu.roll(nr, shift, 0))
            ci = jnp.where(edge, ini, pltpu.roll(ni, shift, 0))
        starts[2 * g], starts[2 * g + 1] = cr, ci
        inr = (atr * cr - ati * ci + er)[last:last + 1]
        ini = (atr * ci + ati * cr + ei)[last:last + 1]
    lax.fori_loop(0, T, make_step(True), tuple(starts))


def _ssm_fwd(uf, bre, bim, cre, cim, lamp, dsk, name, comm=None):
    L = uf.shape[0]
    T = L // NSEG
    tc = L // NCH

    def body(u_ref, bre_ref, bim_ref, cre_ref, cim_ref, lam_ref, d_ref, y_ref, sr_s, si_s):
        for k in range(NCH):
            sl = slice(k * tc, (k + 1) * tc)
            uk = u_ref[sl, :]
            _lb_store(sr_s, sl, _dot(uk, bre_ref[0]))
            _lb_store(si_s, sl, _dot(uk, bim_ref[0]))
        _scan(sr_s, si_s, 0, T, lam_ref[0:1, :], lam_ref[1:2, :], lam_ref[2:3, :], lam_ref[3:4, :], False)
        for k in range(NCH):
            sl = slice(k * tc, (k + 1) * tc)
            y_ref[sl, :] = (_dot(_lb_load(sr_s, sl).astype(BF16), cre_ref[0])
                            - _dot(_lb_load(si_s, sl).astype(BF16), cim_ref[0])
                            + d_ref[...] * u_ref[sl, :].astype(F32))

    return _call(
        body, comm, name=name, grid=(NQ,),
        out_shape=[jax.ShapeDtypeStruct((L, DS), F32)],
        in_specs=[pl.BlockSpec((L, QU), lambda q: (0, q)),
                  pl.BlockSpec((1, QU, QS), lambda q: (q, 0, 0)), pl.BlockSpec((1, QU, QS), lambda q: (q, 0, 0)),
                  pl.BlockSpec((1, QS, QU), lambda q: (q, 0, 0)), pl.BlockSpec((1, QS, QU), lambda q: (q, 0, 0)),
                  pl.BlockSpec((8, QS), lambda q: (0, q)), pl.BlockSpec((1, QU), lambda q: (0, q))],
        out_specs=[pl.BlockSpec((L, QU), lambda q: (0, q))],
        scratch_shapes=[pltpu.VMEM((NLB, L, 128), F32), pltpu.VMEM((NLB, L, 128), F32)],
        params=_params("parallel"),
    )(uf, bre, bim, cre, cim, lamp, dsk)


def _ssm_bwd(uf, dyss, bre, bim, cre, cim, lamp, dsk, name, comm=None):
    L = uf.shape[0]
    T = L // NSEG
    tc = L // NCH

    def body(u_ref, dy_ref, bre_ref, bim_ref, cre_ref, cim_ref, lam_ref, d_ref,
             du_ref, dbre_ref, dbim_ref, dcre_ref, dcim_ref, dlam_ref, dd_ref, sr_s, si_s, gr_s, gi_s):
        _lb_store(sr_s, slice(0, SOFF), jnp.zeros((SOFF, QS), F32))
        _lb_store(si_s, slice(0, SOFF), jnp.zeros((SOFF, QS), F32))
        for k in range(NCH):
            sl = slice(k * tc, (k + 1) * tc)
            ss = slice(SOFF + k * tc, SOFF + (k + 1) * tc)
            uk = u_ref[sl, :]
            dyk = dy_ref[sl, :].astype(BF16)
            _lb_store(sr_s, ss, _dot(uk, bre_ref[0]))
            _lb_store(si_s, ss, _dot(uk, bim_ref[0]))
            _lb_store(gr_s, sl, _dot_nt(dyk, cre_ref[0]))
            _lb_store(gi_s, sl, -_dot_nt(dyk, cim_ref[0]))
        ar, ai, atr, ati = lam_ref[0:1, :], lam_ref[1:2, :], lam_ref[2:3, :], lam_ref[3:4, :]
        _scan(sr_s, si_s, SOFF, T, ar, ai, atr, ati, False)
        _scan(gr_s, gi_s, 0, T, ar, -ai, atr, -ati, True)
        dbre = jnp.zeros((QU, QS), F32)
        dbim = jnp.zeros((QU, QS), F32)
        dcre = jnp.zeros((QS, QU), F32)
        dcim = jnp.zeros((QS, QU), F32)
        dd = jnp.zeros((1, QU), F32)
        qr = jnp.zeros((1, QS), F32)
        qi = jnp.zeros((1, QS), F32)
        for k in range(NCH):
            sl = slice(k * tc, (k + 1) * tc)
            ss = slice(SOFF + k * tc, SOFF + (k + 1) * tc)
            sp = slice(SOFF - 1 + k * tc, SOFF - 1 + (k + 1) * tc)
            uk = u_ref[sl, :]
            dyk = dy_ref[sl, :]
            dyb = dyk.astype(BF16)
            gr, gi = _lb_load(gr_s, sl), _lb_load(gi_s, sl)
            pr, pi = _lb_load(sr_s, sp), _lb_load(si_s, sp)
            qr = qr + jnp.sum(gr * pr + gi * pi, axis=0, keepdims=True)
            qi = qi + jnp.sum(gi * pr - gr * pi, axis=0, keepdims=True)
            grb, gib = gr.astype(BF16), gi.astype(BF16)
            du_ref[sl, :] = (_dot_nt(grb, bre_ref[0]) + _dot_nt(gib, bim_ref[0])
                             + dyk * d_ref[...]).astype(BF16)
            dbre = dbre + _dot_tn(uk, grb)
            dbim = dbim + _dot_tn(uk, gib)
            dcre = dcre + _dot_tn(_lb_load(sr_s, ss).astype(BF16), dyb)
            dcim = dcim - _dot_tn(_lb_load(si_s, ss).astype(BF16), dyb)
            dd = dd + jnp.sum(dyk * uk.astype(F32), axis=0, keepdims=True)
        dlam_ref[0] = jnp.concatenate([qr, qi, jnp.zeros((6, QS), F32)], axis=0)
        dbre_ref[0] = dbre
        dbim_ref[0] = dbim
        dcre_ref[0] = dcre
        dcim_ref[0] = dcim
        dd_ref[...] = dd

    col = pl.BlockSpec((L, QU), lambda q: (0, q))
    bsp = pl.BlockSpec((1, QU, QS), lambda q: (q, 0, 0))
    csp = pl.BlockSpec((1, QS, QU), lambda q: (q, 0, 0))
    return _call(
        body, comm, name=name, grid=(NQ,),
        out_shape=[jax.ShapeDtypeStruct((L, DS), BF16),
                   jax.ShapeDtypeStruct((NQ, QU, QS), F32), jax.ShapeDtypeStruct((NQ, QU, QS), F32),
                   jax.ShapeDtypeStruct((NQ, QS, QU), F32), jax.ShapeDtypeStruct((NQ, QS, QU), F32),
                   jax.ShapeDtypeStruct((NQ, 8, QS), F32), jax.ShapeDtypeStruct((1, DS), F32)],
        in_specs=[col, col, bsp, bsp, csp, csp,
                  pl.BlockSpec((8, QS), lambda q: (0, q)), pl.BlockSpec((1, QU), lambda q: (0, q))],
        out_specs=[col,
                   pl.BlockSpec((1, QU, QS), lambda q: (q, 0, 0)), pl.BlockSpec((1, QU, QS), lambda q: (q, 0, 0)),
                   pl.BlockSpec((1, QS, QU), lambda q: (q, 0, 0)), pl.BlockSpec((1, QS, QU), lambda q: (q, 0, 0)),
                   pl.BlockSpec((1, 8, QS), lambda q: (q, 0, 0)), pl.BlockSpec((1, QU), lambda q: (0, q))],
        scratch_shapes=[pltpu.VMEM((NLB, L + SOFF, 128), F32), pltpu.VMEM((NLB, L + SOFF, 128), F32),
                        pltpu.VMEM((NLB, L, 128), F32), pltpu.VMEM((NLB, L, 128), F32)],
        params=_params("parallel"),
    )(uf, dyss, bre, bim, cre, cim, lamp, dsk)


def _branches(z1_ref, yss_ref, gt_ref, lng_ref, lnb_ref, wp_ref, wv_ref, wg_ref):
    zf = z1_ref[...]
    mu = jnp.mean(zf, axis=-1, keepdims=True)
    zc = zf - mu
    rstd = lax.rsqrt(jnp.mean(zc * zc, axis=-1, keepdims=True) + EPS)
    zn = zc * rstd
    z2 = zn * lng_ref[...] + lnb_ref[...]
    sz = _sigmoid(z2)
    z3 = (z2 * sz).astype(BF16)
    y_conv = _dot(z3, wp_ref[...])
    yss = yss_ref[...]
    yg = _gelu(yss).astype(BF16)
    sv = _dot(yg, wv_ref[...])
    sig = _sigmoid(_dot(yg, wg_ref[...]))
    y_ssm = sv * sig
    gc = gt_ref[:, 0:D].astype(F32)
    gs = gt_ref[:, D:2 * D].astype(F32)
    m = gc * y_conv + gs * y_ssm
    return dict(rstd=rstd, zn=zn, z2=z2, sz=sz, z3=z3, y_conv=y_conv, yss=yss, yg=yg, sv=sv, sig=sig,
                y_ssm=y_ssm, gc=gc, gs=gs, m=m)


def _merge_fwd(h, z1, yss, gate, lng, lnb, wp, wv, wg, wo, name, comm=None):
    L = h.shape[0]
    tm = _tile(L, 528)

    def body(h_ref, z1_ref, yss_ref, gt_ref, lng_ref, lnb_ref, wp_ref, wv_ref, wg_ref, wo_ref, o_ref):
        f = _branches(z1_ref, yss_ref, gt_ref, lng_ref, lnb_ref, wp_ref, wv_ref, wg_ref)
        o_ref[...] = h_ref[...] + _dot(f["m"].astype(BF16), wo_ref[...])

    def row(n):
        return pl.BlockSpec((tm, n), lambda i: (i, 0))

    return _call(
        body, comm, name=name, grid=(L // tm,),
        out_shape=[jax.ShapeDtypeStruct((L, D), F32)],
        in_specs=[row(D), row(DC), row(DS), row(2 * D), _res((1, DC)), _res((1, DC)),
                  _res((DC, D)), _res((DS, D)), _res((DS, D)), _res((D, D))],
        out_specs=[row(D)],
        params=_params("parallel"),
    )(h, z1, yss, gate, lng, lnb, wp, wv, wg, wo)


def _merge_bwd(dh, z1, yss, gate, lng, lnb, wp, wv, wg, wo, name):
    L = dh.shape[0]
    tm = _tile(L, 352)

    def body(dh_ref, z1_ref, yss_ref, gt_ref, lng_ref, lnb_ref, wp_ref, wv_ref, wg_ref, wo_ref,
             m_ref, dgt_ref, dyc_ref, z3_ref, dz1_ref, yg_ref, dsv_ref, dsg_ref, dyss_ref,
             dbg_ref, dlng_ref, dlnb_ref):
        i = pl.program_id(0)
        f = _branches(z1_ref, yss_ref, gt_ref, lng_ref, lnb_ref, wp_ref, wv_ref, wg_ref)
        gc, gs, sig, sv = f["gc"], f["gs"], f["sig"], f["sv"]
        m_ref[...] = f["m"].astype(BF16)
        z3_ref[...] = f["z3"]
        yg_ref[...] = f["yg"]
        dm = _dot_nt(dh_ref[...].astype(BF16), wo_ref[...])
        dgc = (dm * f["y_conv"] * gc * (1.0 - gc)).astype(BF16)
        dgs = (dm * f["y_ssm"] * gs * (1.0 - gs)).astype(BF16)
        dgt_ref[:, 0:D] = dgc
        dgt_ref[:, D:2 * D] = dgs
        part = jnp.concatenate([jnp.sum(dgc.astype(F32), axis=0, keepdims=True),
                                jnp.sum(dgs.astype(F32), axis=0, keepdims=True)], axis=1)
        _acc_rows(dbg_ref, part, i == 0)
        dyc = (dm * gc).astype(BF16)
        dyc_ref[...] = dyc
        dys = dm * gs
        dsv = (dys * sig).astype(BF16)
        dsg = (dys * sv * sig * (1.0 - sig)).astype(BF16)
        dsv_ref[...] = dsv
        dsg_ref[...] = dsg
        dyg = _dot_nt(dsv, wv_ref[...]) + _dot_nt(dsg, wg_ref[...])
        dyss_ref[...] = dyg * _gelu_grad(f["yss"])
        dz3 = _dot_nt(dyc, wp_ref[...])
        z2, sz, zn = f["z2"], f["sz"], f["zn"]
        dz2 = dz3 * sz * (1.0 + z2 * (1.0 - sz))
        _acc_rows(dlng_ref, jnp.sum(dz2 * zn, axis=0, keepdims=True), i == 0)
        _acc_rows(dlnb_ref, jnp.sum(dz2, axis=0, keepdims=True), i == 0)
        dzn = dz2 * lng_ref[...]
        dz1_ref[...] = f["rstd"] * (dzn - jnp.mean(dzn, axis=-1, keepdims=True)
                                    - zn * jnp.mean(dzn * zn, axis=-1, keepdims=True))

    def row(n):
        return pl.BlockSpec((tm, n), lambda i: (i, 0))

    def tot(n):
        return pl.BlockSpec((1, n), lambda i: (0, 0))

    return pl.pallas_call(
        body, name=name, grid=(L // tm,),
        out_shape=[jax.ShapeDtypeStruct((L, D), BF16), jax.ShapeDtypeStruct((L, 2 * D), BF16),
                   jax.ShapeDtypeStruct((L, D), BF16), jax.ShapeDtypeStruct((L, DC), BF16),
                   jax.ShapeDtypeStruct((L, DC), F32), jax.ShapeDtypeStruct((L, DS), BF16),
                   jax.ShapeDtypeStruct((L, D), BF16), jax.ShapeDtypeStruct((L, D), BF16),
                   jax.ShapeDtypeStruct((L, DS), F32),
                   jax.ShapeDtypeStruct((1, 2 * D), F32), jax.ShapeDtypeStruct((1, DC), F32),
                   jax.ShapeDtypeStruct((1, DC), F32)],
        in_specs=[row(D), row(DC), row(DS), row(2 * D), _res((1, DC)), _res((1, DC)),
                  _res((DC, D)), _res((DS, D)), _res((DS, D)), _res((D, D))],
        out_specs=[row(D), row(2 * D), row(D), row(DC), row(DC), row(DS), row(D), row(D), row(DS),
                   tot(2 * D), tot(DC), tot(DC)],
        compiler_params=_params("arbitrary"),
    )(dh, z1, yss, gate, lng, lnb, wp, wv, wg, wo)


def _ssm_disc(lam_re, lam_im, log_dt, b_re, b_im):
    lam = lax.complex(lam_re, lam_im)
    dt = jnp.exp(log_dt)[:, None]
    lam_bar = jnp.exp(lam * dt)
    bbar = ((lam_bar - 1.0) / lam)[..., None] * lax.complex(b_re, b_im)
    return jnp.real(lam_bar), jnp.imag(lam_bar), jnp.real(bbar), jnp.imag(bbar)


def _bdiag_in(m):
    m4 = m.reshape(NQ, G // NQ, P, H)
    return jnp.einsum("qgph,gk->qghkp", m4, jnp.eye(G // NQ, dtype=m.dtype)).reshape(NQ, QU, QS)


def _bdiag_out(m):
    m4 = m.reshape(NQ, G // NQ, H, P)
    return jnp.einsum("qghp,gk->qgpkh", m4, jnp.eye(G // NQ, dtype=m.dtype)).reshape(NQ, QS, QU)


def _diag_blocks(m4):
    return jnp.einsum("qiaib->qiab", m4).reshape(G, m4.shape[2], m4.shape[4])


def _pack(parts, rows_mult=8):
    flat = jnp.concatenate([p.reshape(-1).astype(F32) for p in parts])
    n = flat.shape[0]
    tot = -(-n // (128 * rows_mult)) * (128 * rows_mult)
    return jnp.pad(flat, (0, tot - n)).reshape(tot // 128, 128)


def _unpack(buf, shapes):
    flat = buf.reshape(-1)
    out, o = [], 0
    for s in shapes:
        n = math.prod(s)
        out.append(flat[o:o + n].reshape(s))
        o += n
    return out


def kernel(x, meta_tokens, ffn1_norm, ffn1_w1, ffn1_w3, ffn1_w2, mix_norm, w_in, b_gate, conv_dw, conv_dw_b, conv_ln_g, conv_ln_b, conv_proj, ssm_lam_re, ssm_lam_im, ssm_log_dt, ssm_b_re, ssm_b_im, ssm_c_re, ssm_c_im, ssm_d, ssm_w_v, ssm_w_g, w_out, ffn2_norm, ffn2_w1, ffn2_w3, ffn2_w2, final_norm, loss_target, m_meta_tokens, m_ffn1_norm, m_ffn1_w1, m_ffn1_w3, m_ffn1_w2, m_mix_norm, m_w_in, m_b_gate, m_conv_dw, m_conv_dw_b, m_conv_ln_g, m_conv_ln_b, m_conv_proj, m_ssm_lam_re, m_ssm_lam_im, m_ssm_log_dt, m_ssm_b_re, m_ssm_b_im, m_ssm_c_re, m_ssm_c_im, m_ssm_d, m_ssm_w_v, m_ssm_w_g, m_w_out, m_ffn2_norm, m_ffn2_w1, m_ffn2_w3, m_ffn2_w2, m_final_norm, v_meta_tokens, v_ffn1_norm, v_ffn1_w1, v_ffn1_w3, v_ffn1_w2, v_mix_norm, v_w_in, v_b_gate, v_conv_dw, v_conv_dw_b, v_conv_ln_g, v_conv_ln_b, v_conv_proj, v_ssm_lam_re, v_ssm_lam_im, v_ssm_log_dt, v_ssm_b_re, v_ssm_b_im, v_ssm_c_re, v_ssm_c_im, v_ssm_d, v_ssm_w_v, v_ssm_w_g, v_w_out, v_ffn2_norm, v_ffn2_w1, v_ffn2_w3, v_ffn2_w2, v_final_norm):
    args = dict(locals())
    names = ["meta_tokens", "ffn1_norm", "ffn1_w1", "ffn1_w3", "ffn1_w2", "mix_norm", "w_in", "b_gate",
             "conv_dw", "conv_dw_b", "conv_ln_g", "conv_ln_b", "conv_proj", "ssm_lam_re", "ssm_lam_im",
             "ssm_log_dt", "ssm_b_re", "ssm_b_im", "ssm_c_re", "ssm_c_im", "ssm_d", "ssm_w_v", "ssm_w_g",
             "w_out", "ffn2_norm", "ffn2_w1", "ffn2_w3", "ffn2_w2", "final_norm"]
    big = ["ffn1_w1", "ffn1_w3", "ffn1_w2", "w_in", "conv_proj", "ssm_w_v", "ssm_w_g", "w_out",
           "ffn2_w1", "ffn2_w3", "ffn2_w2"]
    small = [n for n in names if n not in big]

    xs = x[0]
    S = xs.shape[0]
    L = FRONT + S
    T = L // NSEG
    jx, jy = lax.axis_index("x"), lax.axis_index("y")
    chip = 2 * jx + jy

    small_all = _gather_all(_pack([meta_tokens, conv_dw[0]]), "gather_small")
    sm = small_all[0::2].reshape(NSH, -1)
    nmt = NMETA * (D // NSH)
    ndw = KW * (DC // NSH)
    meta_full = sm[:, :nmt].reshape(NSH, NMETA, D // NSH).transpose(1, 0, 2).reshape(NMETA, D)
    dw_full = sm[:, nmt:nmt + ndw].reshape(NSH, KW, DC // NSH).transpose(1, 0, 2).reshape(KW, DC)
    dw_pad = jnp.pad(dw_full, ((0, KWP - KW), (0, 0)))
    tposed = ("ffn1_w1", "ffn1_w3", "ffn2_w1", "ffn2_w3")

    def view(a, n):
        return jnp.swapaxes(a, 1, 2) if n in tposed else a

    grp_a = ["ffn1_w1", "ffn1_w3", "ffn1_w2"]
    grp_b = ["w_in", "conv_proj", "ssm_w_v", "ssm_w_g", "w_out"]
    grp_c = ["ffn2_w1", "ffn2_w3", "ffn2_w2"]

    shards = {n: view(args[n], n)[0].astype(BF16) for n in big}

    def shard(n):
        return shards[n]

    sh_a = [shard(n) for n in grp_a]
    ga_send, ga_recv, sh_a, land_a, _ = _chips_start(
        sh_a, [jax.ShapeDtypeStruct((NSH,) + s.shape, s.dtype) for s in sh_a], True, "gather_ffn1_start",
        [small_all])

    def cols(w):
        return w.transpose(1, 0, 2).reshape(w.shape[1], -1)

    disc_in = (ssm_lam_re[0], ssm_lam_im[0], ssm_log_dt[0], ssm_b_re[0], ssm_b_im[0])
    (lbr, lbi, bbr, bbi), disc_vjp = jax.vjp(_ssm_disc, *disc_in)
    lam_t = jnp.exp(lax.complex(ssm_lam_re[0], ssm_lam_im[0]) * (jnp.exp(ssm_log_dt[0])[:, None] * T))
    lamp = jnp.concatenate([lbr.reshape(1, NST), lbi.reshape(1, NST), jnp.real(lam_t).reshape(1, NST),
                            jnp.imag(lam_t).reshape(1, NST), jnp.zeros((4, NST), F32)], axis=0)
    bre_bd, bim_bd = _bdiag_in(bbr).astype(BF16), _bdiag_in(bbi).astype(BF16)
    cre_bd, cim_bd = _bdiag_out(ssm_c_re[0]).astype(BF16), _bdiag_out(ssm_c_im[0]).astype(BF16)

    h0 = lax.dynamic_update_slice(jnp.pad(xs, ((FRONT, 0), (0, 0))), meta_full, (FRONT - NMETA, 0))
    tgt = jnp.pad(loss_target[0], ((FRONT, 0), (0, 0)))
    small_wmv = [_pack([args[p + n] for n in small])[None] for p in ("", "m_", "v_")]
    early_work = [h0, tgt, bre_bd, bim_bd, cre_bd, cim_bd] + [shards[n] for n in grp_b + grp_c] + small_wmv
    sh_a, land_a = _chips_wait(ga_send, ga_recv, sh_a, land_a, early_work, True, "gather_ffn1_wait")
    gw = dict(zip(grp_a, _pass_halves(land_a, "pass_ffn1", sh_a)))
    (h1, a1, b1), got = _ffn_fwd(h0, ffn1_norm, gw["ffn1_w1"], gw["ffn1_w3"], gw["ffn1_w2"], "ffn1_fwd",
                                 _gather_half_behind([shard(n) for n in grp_b]))
    w_in_f = _pass_halves(got[:1], "pass_w_in")[0]
    (vg, uf, gate), got1 = _mix_in_fwd(h1, mix_norm, w_in_f, b_gate, "mix_in_fwd",
                                       _join(_gather_half_behind([shard("ffn2_w1")]),
                                             _pass_halves_behind(list(got[1:]))))
    gw.update(zip(grp_b[1:], got1[1:]))
    wp_f, wv_f, wg_f = cols(gw["conv_proj"]), cols(gw["ssm_w_v"]), cols(gw["ssm_w_g"])
    wo_f = gw["w_out"].reshape(D, D)
    (z1,), got3 = _conv_fwd(vg, dw_pad, conv_dw_b, "conv_fwd", _gather_half_behind([shard("ffn2_w3")]))
    (yss,), got2 = _ssm_fwd(uf, bre_bd, bim_bd, cre_bd, cim_bd, lamp, ssm_d, "ssm_fwd",
                            _gather_half_behind([shard("ffn2_w2")]))
    (h2,), got_c = _merge_fwd(h1, z1, yss, gate, conv_ln_g, conv_ln_b, wp_f, wv_f, wg_f, wo_f, "merge_fwd",
                              _pass_halves_behind([got1[0], got3[0], got2[0]]))
    gw.update(zip(grp_c, got_c))
    dh3, a2, b2, loss_part, d_final = _ffn_fwd_loss(
        h2, ffn2_norm, gw["ffn2_w1"], gw["ffn2_w3"], gw["ffn2_w2"], final_norm.reshape(1, D), tgt, "ffn2_fwd_loss")

    gbig = {}
    core = lax.axis_index("c").astype(jnp.int32).reshape(1)

    def pair_sums(group, tag):
        gl = [gbig[n] for n in group]
        sib = _pair_exchange(gl, "pair_exchange_" + tag)
        out = [None] * len(group)
        for idx in _by_shape(gl):
            res = _add_pair([gl[i] for i in idx], [sib[i] for i in idx], core, "pair_" + group[idx[0]])
            for i, r in zip(idx, res):
                out[i] = r
        return out

    (dh2, da2, db2, s2, n2, d_ffn2_norm), _ = _ffn_bwd(
        h2, ffn2_norm, dh3, a2, b2, gw["ffn2_w1"], gw["ffn2_w3"], gw["ffn2_w2"], "ffn2_bwd")
    gbig["ffn2_w1"] = _wgrad(da2, n2, "ffn2_dw1")
    gbig["ffn2_w3"] = _wgrad(db2, n2, "ffn2_dw3")
    gbig["ffn2_w2"] = _wgrad(s2, dh3, "ffn2_dw2", 0.5)
    pair_c = pair_sums(grp_c, "ffn2")
    (m_b, dgate, dyc, z3, dz1, yg, dsv, dsg, dyss, d_b_gate, d_ln_g, d_ln_b) = _merge_bwd(
        dh2, z1, yss, gate, conv_ln_g, conv_ln_b, wp_f, wv_f, wg_f, wo_f, "merge_bwd")
    gbig["w_out"] = _wgrad(m_b, dh2, "dw_out").reshape(NSH, D // NSH, D)

    def shard_cols(gm):
        return gm.reshape(gm.shape[0], NSH, -1).transpose(1, 0, 2)

    gbig["conv_proj"] = shard_cols(_wgrad(z3, dyc, "dw_proj"))
    gbig["ssm_w_v"] = shard_cols(_wgrad(yg, dsv, "dw_v"))
    gbig["ssm_w_g"] = shard_cols(_wgrad(yg, dsg, "dw_g"))
    dv, dgl, ddw, d_dw_b = _conv_bwd(dz1, vg, dw_pad, "conv_bwd")
    (duf, dbre, dbim, dcre, dcim, dlam, d_ssm_d), recv_c = _ssm_bwd(
        uf, dyss, bre_bd, bim_bd, cre_bd, cim_bd, lamp, ssm_d, "ssm_bwd", _scatter_chips_behind(pair_c))
    dh1, u_b, dproj, d_mix_norm = _mix_in_bwd(h1, mix_norm, dh2, dv, dgl, duf, dgate, w_in_f, "mix_in_bwd")
    gbig["w_in"] = _wgrad(u_b, dproj, "dw_in")
    pair_b = pair_sums(grp_b, "mix")

    d_bbr = _diag_blocks(dbre.reshape(NQ, 8, H, 8, P)).transpose(0, 2, 1)
    d_bbi = _diag_blocks(dbim.reshape(NQ, 8, H, 8, P)).transpose(0, 2, 1)
    d_c_re = _diag_blocks(dcre.reshape(NQ, 8, P, 8, H)).transpose(0, 2, 1)
    d_c_im = _diag_blocks(dcim.reshape(NQ, 8, P, 8, H)).transpose(0, 2, 1)
    d_lbr = dlam[:, 0, :].reshape(G, P)
    d_lbi = dlam[:, 1, :].reshape(G, P)
    d_lam_re, d_lam_im, d_log_dt, d_b_re, d_b_im = disc_vjp((d_lbr, d_lbi, d_bbr, d_bbi))

    sg = {"mix_norm": d_mix_norm, "b_gate": d_b_gate, "conv_dw": ddw[:KW], "conv_dw_b": d_dw_b,
          "conv_ln_g": d_ln_g, "conv_ln_b": d_ln_b, "ssm_lam_re": d_lam_re, "ssm_lam_im": d_lam_im,
          "ssm_log_dt": d_log_dt, "ssm_b_re": d_b_re, "ssm_b_im": d_b_im, "ssm_c_re": d_c_re, "ssm_c_im": d_c_im,
          "ssm_d": d_ssm_d, "ffn2_norm": d_ffn2_norm, "final_norm": d_final}
    late = ["meta_tokens", "ffn1_norm"]
    early = [n for n in small if n not in late]

    (dh0, da1, db1, s1, n1, d_ffn1_norm), got = _ffn_bwd(
        h0, ffn1_norm, dh1, a1, b1, gw["ffn1_w1"], gw["ffn1_w3"], gw["ffn1_w2"], "ffn1_bwd",
        _join(_scatter_chips_behind(pair_b), _gather_all_behind(_pack([sg[n] for n in early]))))
    recv_b, early_all = got[:len(grp_b)], got[len(grp_b)]
    gbig["ffn1_w1"] = _wgrad(da1, n1, "ffn1_dw1")
    gbig["ffn1_w3"] = _wgrad(db1, n1, "ffn1_dw3")
    gbig["ffn1_w2"] = _wgrad(s1, dh1, "ffn1_dw2", 0.5)
    grad_x = dh0[FRONT:][None]
    sg["meta_tokens"] = dh0[FRONT - NMETA:FRONT]
    sg["ffn1_norm"] = d_ffn1_norm

    pair_a = pair_sums(grp_a, "ffn1")
    late_all = _gather_all(_pack([sg[n] for n in late]), "gather_late_grads")
    sa_send, sa_recv, pair_a, land_s, sa_token = _chips_start(
        pair_a, [jax.ShapeDtypeStruct(p.shape, p.dtype) for p in pair_a], False, "scatter_ffn1_start", [late_all])

    out_g, out_d, out_m, out_v = {}, {}, {}, {}

    def finish(group, recvs, tag, after=None):
        halves = [None] * len(group)
        for idx in _by_shape(recvs):
            res = _sum_slots([recvs[i] for i in idx], "sum_" + group[idx[0]], after)
            for i, r in zip(idx, res):
                halves[i] = r
        fours = [(view(args[n], n), f.reshape(1, f.shape[0] * f.shape[1], f.shape[2]), view(args["m_" + n], n),
                  view(args["v_" + n], n)) for n, f in zip(group, _swap_halves(halves, "swap_" + tag))]
        for idx in _by_shape([four[0] for four in fours]):
            for i, (g3, d3, m3, v3) in zip(idx, _adamw([fours[i] for i in idx], "adamw_" + group[idx[0]])):
                n = group[i]
                out_g[n], out_d[n], out_m[n], out_v[n] = (view(t, n) for t in (g3, d3, m3, v3))
                done.append(d3)

    done = []
    finish(grp_b + grp_c, list(recv_b) + list(recv_c), "mix_ffn2", sa_token)

    sgr = dict(zip(early, _unpack(_sum_slots([early_all], "sum_early", sa_token)[0], [sg[n].shape for n in early])))
    sgr.update(zip(late, _unpack(_sum_slots([late_all], "sum_late")[0], [sg[n].shape for n in late])))
    sgr["meta_tokens"] = lax.dynamic_slice_in_dim(sgr["meta_tokens"], chip * (D // NSH), D // NSH, axis=1)
    sgr["conv_dw"] = lax.dynamic_slice_in_dim(sgr["conv_dw"], chip * (DC // NSH), DC // NSH, axis=1)
    pshapes = [args[n].shape for n in small]
    _, d_s, m_s, v_s = _adamw([(small_wmv[0], _pack([sgr[n] for n in small])[None], small_wmv[1], small_wmv[2])],
                              "adamw_small")[0]
    for n, g_, d_, m_, v_ in zip(small, [sgr[n] for n in small], _unpack(d_s[0], pshapes),
                                 _unpack(m_s[0], pshapes), _unpack(v_s[0], pshapes)):
        out_g[n], out_d[n], out_m[n], out_v[n] = g_.reshape(args[n].shape), d_, m_, v_

    loss = lax.psum(loss_part[0, 0], ("x", "y", "c"))
    pair_a, recv_a = _chips_wait(sa_send, sa_recv, pair_a, land_s, [d_s, grad_x] + done, False,
                                 "scatter_ffn1_wait")
    finish(grp_a, _fill_own(pair_a, recv_a, "own_ffn1"), "ffn1")
    return (loss, grad_x, *[out_g[n] for n in names], *[out_d[n] for n in names],
            *[out_m[n] for n in names], *[out_v[n] for n in names])
```

```python
import math

import jax
import jax.numpy as jnp
from jax import lax
from jax.experimental import pallas as pl
from jax.experimental.pallas import tpu as pltpu

F32 = jnp.float32
BF16 = jnp.bfloat16

D = 1024
NSH = 4
F = 2816
FS = F // NSH
DC = 512
DS = 512
DIN = 2 * DC + DS + 2 * D
WS = DIN // NSH
KW = 31
KWP = 32
CONV_ROWS = 64
NMETA = 16
FRONT = 128
G, P, H = 32, 64, 16
NST = G * P
NQ = 4
QS = NST // NQ
QU = DS // NQ
NSEG = 32
NGRP = NSEG // 8
NCH = 8
SOFF = 8
EPS = 1e-6
LR, B1, B2, AEPS, WD, STEP = 1e-3, 0.9, 0.999, 1e-8, 0.01, 10
VMEM_LIMIT = 58 * 1024 * 1024
MESH = pl.DeviceIdType.MESH
ANY = pl.BlockSpec(memory_space=pl.ANY)


def _params(*sem):
    return pltpu.CompilerParams(dimension_semantics=sem, vmem_limit_bytes=VMEM_LIMIT)


def _res(shape):
    nd = len(shape)
    return pl.BlockSpec(shape, lambda *_: (0,) * nd, pipeline_mode=pl.Buffered(1))


def _tile(n, cap, mult=16):
    best = None
    for t in range(mult, min(n, cap) + 1, mult):
        if n % t == 0:
            best = t
    assert best is not None, (n, cap, mult)
    return best


def _dot(a, b):
    return jnp.dot(a, b, preferred_element_type=F32)


def _dot_nt(a, b):
    return lax.dot_general(a, b, (((1,), (1,)), ((), ())), preferred_element_type=F32)


def _dot_tn(a, b):
    return lax.dot_general(a, b, (((0,), (0,)), ((), ())), preferred_element_type=F32)


def _sigmoid(x):
    return 1.0 / (1.0 + jnp.exp(-x))


_GC = math.sqrt(2.0 / math.pi)
_GA = 0.044715


def _gelu(x):
    return 0.5 * x * (1.0 + jnp.tanh(_GC * (x + _GA * x * x * x)))


def _gelu_grad(x):
    t = jnp.tanh(_GC * (x + _GA * x * x * x))
    return 0.5 * (1.0 + t) + 0.5 * x * (1.0 - t * t) * _GC * (1.0 + 3.0 * _GA * x * x)


def _rms(hv, g):
    r = lax.rsqrt(jnp.mean(hv * hv, axis=-1, keepdims=True) + EPS)
    return hv * r * g, r


def _rms_bwd(dn, hv, r, g):
    xh = hv * r
    dxh = dn * g
    return r * (dxh - xh * jnp.mean(dxh * xh, axis=-1, keepdims=True)), xh


def _acc_rows(ref, part, first):
    @pl.when(first)
    def _():
        ref[...] = part

    @pl.when(jnp.logical_not(first))
    def _():
        ref[...] += part


def _coords():
    return lax.axis_index("x"), lax.axis_index("y"), lax.axis_index("c")


def _flip(v, d):
    return 1 - v if d else v


def _run(local, remote):
    for cp in local + remote:
        cp.start()
    for cp in remote:
        cp.wait()
    for cp in local:
        cp.wait()


def _via_vmem(src, dst, stage, sems, i):
    return (pltpu.make_async_copy(src, stage, sems.at[2 * i]), pltpu.make_async_copy(stage, dst, sems.at[2 * i + 1]))


def _run_staged(staged, remote):
    for load, _ in staged:
        load.start()
    for cp in remote:
        cp.start()
    for load, store in staged:
        load.wait()
        store.start()
    for cp in remote:
        cp.wait()
    for _, store in staged:
        store.wait()


_REL3 = ((1, 0), (0, 1), (1, 1))


class _Behind:
    def __init__(self, arrays, out_shapes, scratch, build, alias_pairs=()):
        self.arrays, self.out_shapes, self.scratch, self.build = list(arrays), list(out_shapes), list(scratch), build
        self.alias_pairs = list(alias_pairs)

    def aliases(self):
        return self.alias_pairs

    def start(self, ins, outs, scr):
        staged, remote = self.build(ins, outs, scr)
        for load, _ in staged:
            load.start()
        for cp in remote:
            cp.start()

    def finish(self, ins, outs, scr):
        staged, remote = self.build(ins, outs, scr)
        for load, store in staged:
            load.wait()
            store.start()
        for cp in remote:
            cp.wait()
        for _, store in staged:
            store.wait()


def _call(body, comm, *, name, grid, in_specs, out_specs, out_shape, scratch_shapes=(), params):
    in_specs, out_specs, out_shape = list(in_specs), list(out_specs), list(out_shape)
    scratch_shapes = list(scratch_shapes)
    if comm is None:
        f = pl.pallas_call(body, name=name, grid=grid, in_specs=in_specs, out_specs=out_specs,
                           out_shape=out_shape, scratch_shapes=scratch_shapes, compiler_params=params)
        return lambda *args: (f(*args), [])
    ni, no, ns = len(in_specs), len(out_specs), len(scratch_shapes)
    ci, co = len(comm.arrays), len(comm.out_shapes)

    def hosted(*refs):
        ins, cin = refs[:ni], refs[ni:ni + ci]
        outs, cout = refs[ni + ci:ni + ci + no], refs[ni + ci + no:ni + ci + no + co]
        scr, cscr = refs[ni + ci + no + co:ni + ci + no + co + ns], refs[ni + ci + no + co + ns:]
        first = last = None
        for axis, size in enumerate(grid):
            i = pl.program_id(axis)
            first = (i == 0) if first is None else jnp.logical_and(first, i == 0)
            last = (i == size - 1) if last is None else jnp.logical_and(last, i == size - 1)

        @pl.when(first)
        def _():
            comm.start(cin, cout, cscr)

        body(*ins, *outs, *scr)

        @pl.when(last)
        def _():
            comm.finish(cin, cout, cscr)

    f = pl.pallas_call(hosted, name=name, grid=grid, in_specs=in_specs + [ANY] * ci,
                       out_specs=out_specs + [ANY] * co, out_shape=out_shape + comm.out_shapes,
                       scratch_shapes=scratch_shapes + comm.scratch,
                       input_output_aliases={ni + a: no + b for a, b in comm.aliases()},
                       compiler_params=_params(*(("arbitrary",) * len(grid))))

    def run(*args):
        res = f(*args, *comm.arrays)
        return res[:no], res[no:]

    return run


def _gather_half_behind(shards):
    n = len(shards)

    def build(ins, outs, scr):
        send, recv, loc = scr[:3]
        stage = scr[3:]
        x, y, c = _coords()
        me = 2 * x + y
        staged = [_via_vmem(ins[t], outs[t].at[me], stage[t], loc, t) for t in range(n)]
        remote = []
        for t in range(n):
            half = shards[t].shape[0] // 2
            mine = pl.ds(c * half, half)
            for k, (dx, dy) in enumerate(_REL3):
                remote.append(pltpu.make_async_remote_copy(
                    src_ref=ins[t].at[mine], dst_ref=outs[t].at[me, mine],
                    send_sem=send.at[3 * t + k], recv_sem=recv.at[3 * t + k],
                    device_id=(_flip(x, dx), _flip(y, dy), c), device_id_type=MESH))
        return staged, remote

    return _Behind(shards, [jax.ShapeDtypeStruct((NSH,) + s.shape, s.dtype) for s in shards],
                   [pltpu.SemaphoreType.DMA((3 * n,)), pltpu.SemaphoreType.DMA((3 * n,)),
                    pltpu.SemaphoreType.DMA((2 * n,))] + [pltpu.VMEM(s.shape, s.dtype) for s in shards], build)


def _pass_halves(gathered, name, own=()):
    n, m = len(gathered), len(own)

    def body(*refs):
        shards, outs = refs[n:n + m], refs[n + m:2 * n + m]
        send, recv, loc = refs[2 * n + m:2 * n + m + 3]
        stage = refs[2 * n + m + 3:]
        x, y, c = _coords()
        staged = [_via_vmem(shards[t], outs[t].at[2 * x + y], stage[t], loc, t) for t in range(m)]
        remote = []
        for t in range(n):
            half = gathered[t].shape[1] // 2
            mine = pl.ds(c * half, half)
            for k, (dx, dy) in enumerate(_REL3):
                slot = 2 * _flip(x, dx) + _flip(y, dy)
                remote.append(pltpu.make_async_remote_copy(
                    src_ref=outs[t].at[slot, mine], dst_ref=outs[t].at[slot, mine],
                    send_sem=send.at[3 * t + k], recv_sem=recv.at[3 * t + k],
                    device_id=(x, y, 1 - c), device_id_type=MESH))
        _run_staged(staged, remote)

    return pl.pallas_call(
        body, name=name,
        out_shape=[jax.ShapeDtypeStruct(g.shape, g.dtype) for g in gathered],
        in_specs=[ANY] * (n + m), out_specs=[ANY] * n, input_output_aliases={t: t for t in range(n)},
        scratch_shapes=[pltpu.SemaphoreType.DMA((3 * n,)), pltpu.SemaphoreType.DMA((3 * n,)),
                        pltpu.SemaphoreType.DMA((max(2 * m, 1),))] + [pltpu.VMEM(s.shape, s.dtype) for s in own],
        compiler_params=pltpu.CompilerParams(vmem_limit_bytes=VMEM_LIMIT),
    )(*gathered, *own)


def _fill_own(sums, recvs, name):
    n = len(sums)

    def body(*refs):
        ins, outs = refs[:n], refs[2 * n:3 * n]
        loc = refs[3 * n]
        stage = refs[3 * n + 1:]
        x, y, _ = _coords()
        me = 2 * x + y
        _run_staged([_via_vmem(ins[t].at[me], outs[t].at[me], stage[t], loc, t) for t in range(n)], [])

    return pl.pallas_call(
        body, name=name,
        out_shape=[jax.ShapeDtypeStruct(r.shape, r.dtype) for r in recvs],
        in_specs=[ANY] * (2 * n), out_specs=[ANY] * n, input_output_aliases={n + t: t for t in range(n)},
        scratch_shapes=[pltpu.SemaphoreType.DMA((2 * n,))] + [pltpu.VMEM(s.shape[1:], s.dtype) for s in sums],
        compiler_params=pltpu.CompilerParams(vmem_limit_bytes=VMEM_LIMIT),
    )(*sums, *recvs)


def _scatter_chips_behind(sums):
    n = len(sums)

    def build(ins, outs, scr):
        send, recv, loc = scr[:3]
        stage = scr[3:]
        x, y, c = _coords()
        me = 2 * x + y
        staged = [_via_vmem(ins[t].at[me], outs[t].at[me], stage[t], loc, t) for t in range(n)]
        remote = []
        for t in range(n):
            for k, (dx, dy) in enumerate(_REL3):
                px, py = _flip(x, dx), _flip(y, dy)
                remote.append(pltpu.make_async_remote_copy(
                    src_ref=ins[t].at[2 * px + py], dst_ref=outs[t].at[me],
                    send_sem=send.at[3 * t + k], recv_sem=recv.at[3 * t + k],
                    device_id=(px, py, c), device_id_type=MESH))
        return staged, remote

    return _Behind(sums, [jax.ShapeDtypeStruct(s.shape, s.dtype) for s in sums],
                   [pltpu.SemaphoreType.DMA((3 * n,)), pltpu.SemaphoreType.DMA((3 * n,)),
                    pltpu.SemaphoreType.DMA((2 * n,))] + [pltpu.VMEM(s.shape[1:], s.dtype) for s in sums], build)


def _gather_all_behind(a):
    def build(ins, outs, scr):
        send, recv, loc, stage = scr
        x, y, c = _coords()
        me = 4 * x + 2 * y + c
        staged = [_via_vmem(ins[0], outs[0].at[me], stage, loc, 0)]
        remote = [pltpu.make_async_remote_copy(
            src_ref=ins[0], dst_ref=outs[0].at[me], send_sem=send.at[k], recv_sem=recv.at[k],
            device_id=(_flip(x, dx), _flip(y, dy), _flip(c, dc)), device_id_type=MESH)
            for k, (dx, dy, dc) in enumerate(_REL7)]
        return staged, remote

    return _Behind([a], [jax.ShapeDtypeStruct((8,) + a.shape, a.dtype)],
                   [pltpu.SemaphoreType.DMA((7,)), pltpu.SemaphoreType.DMA((7,)), pltpu.SemaphoreType.DMA((2,)),
                    pltpu.VMEM(a.shape, a.dtype)], build)


HBM = pl.BlockSpec(memory_space=pltpu.HBM)
SEM = pl.BlockSpec(memory_space=pltpu.SEMAPHORE)
EFFECT = pltpu.SideEffectType.DATAFLOW_SIDE_EFFECTING


def _chip_copies(srcs, lands, send, recv, gather):
    x, y, c = _coords()
    me = 2 * x + y
    cps = []
    for t in range(len(srcs)):
        for k, (dx, dy) in enumerate(_REL3):
            px, py = _flip(x, dx), _flip(y, dy)
            if gather:
                half = srcs[t].shape[0] // 2
                mine = pl.ds(c * half, half)
                src, dst = srcs[t].at[mine], lands[t].at[me, mine]
            else:
                src, dst = srcs[t].at[2 * px + py], lands[t].at[me]
            cps.append(pltpu.make_async_remote_copy(
                src_ref=src, dst_ref=dst, send_sem=send.at[3 * t + k], recv_sem=recv.at[3 * t + k],
                device_id=(px, py, c), device_id_type=MESH))
    return cps


def _chips_start(arrays, land_shapes, gather, name, after=()):
    n = len(arrays)

    def body(*refs):
        srcs, lands = refs[:n], refs[n:2 * n]
        send, recv = refs[2 * n + len(after)], refs[2 * n + len(after) + 1]
        token = refs[-1]
        for cp in _chip_copies(srcs, lands, send, recv, gather):
            cp.start()
        token[...] = jnp.zeros_like(token)

    lands = [lax.empty(s.shape, s.dtype) for s in land_shapes]
    thru = [pltpu.HBM(a.shape, a.dtype) for a in arrays] + [pltpu.HBM(s.shape, s.dtype) for s in land_shapes]
    res = pl.pallas_call(
        body, name=name,
        out_shape=(pltpu.SemaphoreType.DMA((3 * n,)), pltpu.SemaphoreType.DMA((3 * n,)), *thru,
                   jax.ShapeDtypeStruct((8, 128), F32)),
        in_specs=[HBM] * (2 * n) + [ANY] * len(after),
        out_specs=(SEM, SEM, *([HBM] * (2 * n)), pl.BlockSpec(memory_space=pltpu.VMEM)),
        input_output_aliases={t: 2 + t for t in range(2 * n)},
        compiler_params=pltpu.CompilerParams(has_side_effects=EFFECT),
    )(*[pltpu.with_memory_space_constraint(a, pltpu.HBM) for a in arrays],
      *[pltpu.with_memory_space_constraint(z, pltpu.HBM) for z in lands], *after)
    return res[0], res[1], list(res[2:2 + n]), list(res[2 + n:2 + 2 * n]), res[-1]


def _chips_wait(send, recv, arrays, lands, after, gather, name):
    n = len(arrays)

    def body(*refs):
        srcs, ls = refs[:n], refs[n:2 * n]
        sd, rv = refs[2 * n], refs[2 * n + 1]
        for cp in _chip_copies(srcs, ls, sd, rv, gather):
            cp.wait_send()
            cp.wait_recv()

    res = pl.pallas_call(
        body, name=name,
        out_shape=[pltpu.HBM(a.shape, a.dtype) for a in arrays] + [pltpu.HBM(z.shape, z.dtype) for z in lands],
        in_specs=[HBM] * (2 * n) + [SEM, SEM] + [ANY] * len(after), out_specs=[HBM] * (2 * n),
        input_output_aliases={t: t for t in range(2 * n)},
        compiler_params=pltpu.CompilerParams(has_side_effects=EFFECT),
    )(*arrays, *lands, send, recv, *after)
    return list(res[:n]), list(res[n:])


def _join(*parts):
    def cut(seq, key):
        res, o = [], 0
        for p in parts:
            k = len(getattr(p, key))
            res.append(seq[o:o + k])
            o += k
        return res

    def build(ins, outs, scr):
        staged, remote = [], []
        for p, i, o, s in zip(parts, cut(ins, "arrays"), cut(outs, "out_shapes"), cut(scr, "scratch")):
            st, rm = p.build(i, o, s)
            staged += st
            remote += rm
        return staged, remote

    pairs, ai, oi = [], 0, 0
    for p in parts:
        pairs += [(ai + a, oi + b) for a, b in p.alias_pairs]
        ai, oi = ai + len(p.arrays), oi + len(p.out_shapes)
    return _Behind(sum((p.arrays for p in parts), []), sum((p.out_shapes for p in parts), []),
                   sum((p.scratch for p in parts), []), build, pairs)


def _pass_halves_behind(gathered):
    n = len(gathered)

    def build(ins, outs, scr):
        send, recv = scr
        x, y, c = _coords()
        remote = []
        for t in range(n):
            half = gathered[t].shape[1] // 2
            mine = pl.ds(c * half, half)
            for k, (dx, dy) in enumerate(_REL3):
                slot = 2 * _flip(x, dx) + _flip(y, dy)
                remote.append(pltpu.make_async_remote_copy(
                    src_ref=outs[t].at[slot, mine], dst_ref=outs[t].at[slot, mine],
                    send_sem=send.at[3 * t + k], recv_sem=recv.at[3 * t + k],
                    device_id=(x, y, 1 - c), device_id_type=MESH))
        return [], remote

    return _Behind(gathered, [jax.ShapeDtypeStruct(g.shape, g.dtype) for g in gathered],
                   [pltpu.SemaphoreType.DMA((3 * n,)), pltpu.SemaphoreType.DMA((3 * n,))], build,
                   [(t, t) for t in range(n)])


_REL7 = tuple((dx, dy, dc) for dx in (0, 1) for dy in (0, 1) for dc in (0, 1))[1:]


def _gather_all(a, name):
    def body(a_ref, o_ref, send, recv, loc, stage):
        x, y, c = _coords()
        me = 4 * x + 2 * y + c
        local = [_via_vmem(a_ref, o_ref.at[me], stage, loc, 0)]
        remote = [pltpu.make_async_remote_copy(
            src_ref=a_ref, dst_ref=o_ref.at[me], send_sem=send.at[k], recv_sem=recv.at[k],
            device_id=(_flip(x, dx), _flip(y, dy), _flip(c, dc)), device_id_type=MESH)
            for k, (dx, dy, dc) in enumerate(_REL7)]
        _run_staged(local, remote)

    return pl.pallas_call(
        body, name=name,
        out_shape=jax.ShapeDtypeStruct((8,) + a.shape, a.dtype),
        in_specs=[ANY], out_specs=ANY,
        scratch_shapes=[pltpu.SemaphoreType.DMA((7,)), pltpu.SemaphoreType.DMA((7,)),
                        pltpu.SemaphoreType.DMA((2,)), pltpu.VMEM(a.shape, a.dtype)],
    )(a)


def _pair_exchange(grads, name):
    n = len(grads)

    def body(*refs):
        ins, outs = refs[:n], refs[n:2 * n]
        send, recv = refs[2 * n:]
        x, y, c = _coords()
        remote = []
        for t in range(n):
            half = grads[t].shape[1] // 2
            remote.append(pltpu.make_async_remote_copy(
                src_ref=ins[t].at[:, pl.ds((1 - c) * half, half)], dst_ref=outs[t],
                send_sem=send.at[t], recv_sem=recv.at[t],
                device_id=(x, y, 1 - c), device_id_type=MESH))
        _run([], remote)

    return pl.pallas_call(
        body, name=name,
        out_shape=[jax.ShapeDtypeStruct((NSH, g.shape[1] // 2, g.shape[2]), g.dtype) for g in grads],
        in_specs=[ANY] * n, out_specs=[ANY] * n,
        scratch_shapes=[pltpu.SemaphoreType.DMA((n,)), pltpu.SemaphoreType.DMA((n,))],
    )(*grads)


def _swap_halves(halves, name):
    n = len(halves)

    def body(*refs):
        ins, outs = refs[:n], refs[n:2 * n]
        send, recv, loc = refs[2 * n:2 * n + 3]
        stage = refs[2 * n + 3:]
        x, y, c = _coords()
        local = [_via_vmem(ins[t], outs[t].at[c], stage[t], loc, t) for t in range(n)]
        remote = [pltpu.make_async_remote_copy(
            src_ref=ins[t], dst_ref=outs[t].at[c], send_sem=send.at[t], recv_sem=recv.at[t],
            device_id=(x, y, 1 - c), device_id_type=MESH) for t in range(n)]
        _run_staged(local, remote)

    return pl.pallas_call(
        body, name=name,
        out_shape=[jax.ShapeDtypeStruct((2,) + h.shape, h.dtype) for h in halves],
        in_specs=[ANY] * n, out_specs=[ANY] * n,
        scratch_shapes=[pltpu.SemaphoreType.DMA((n,)), pltpu.SemaphoreType.DMA((n,)),
                        pltpu.SemaphoreType.DMA((2 * n,))]
        + [pltpu.VMEM(h.shape, h.dtype) for h in halves],
        compiler_params=pltpu.CompilerParams(vmem_limit_bytes=VMEM_LIMIT),
    )(*halves)


def _by_shape(arrays):
    groups = {}
    for i, a in enumerate(arrays):
        groups.setdefault((a.shape, a.dtype), []).append(i)
    return list(groups.values())


def _sum_slots(rs, name, after=None):
    n = len(rs)
    K, R, C = rs[0].shape
    tr = _tile(R, max(16, (1 << 22) // (n * K * C)), 8 * (4 // rs[0].dtype.itemsize))

    def body(*refs):
        for r_ref, o_ref in zip(refs[:n], refs[len(refs) - n:]):
            acc = r_ref[0].astype(F32)
            for k in range(1, K):
                acc = acc + r_ref[k].astype(F32)
            o_ref[...] = acc

    dep = [] if after is None else [after]
    return pl.pallas_call(
        body, name=name, grid=(R // tr,),
        out_shape=[jax.ShapeDtypeStruct((R, C), F32)] * n,
        in_specs=[pl.BlockSpec((K, tr, C), lambda i: (0, i, 0))] * n + [ANY] * len(dep),
        out_specs=[pl.BlockSpec((tr, C), lambda i: (i, 0))] * n,
        compiler_params=_params("parallel"),
    )(*rs, *dep)


def _add_pair(gs, ss, core, name):
    n = len(gs)
    _, half, C = ss[0].shape
    tr = _tile(half, max(16, (1 << 21) // (n * C)))
    nb = half // tr

    def body(c_ref, *refs):
        for g_ref, s_ref, o_ref in zip(refs[:n], refs[n:2 * n], refs[2 * n:]):
            o_ref[...] = (g_ref[...].astype(F32) + s_ref[...].astype(F32)).astype(BF16)

    spec = pl.BlockSpec((1, tr, C), lambda j, i, c_ref: (j, i, 0))
    return pl.pallas_call(
        body, name=name,
        grid_spec=pltpu.PrefetchScalarGridSpec(
            num_scalar_prefetch=1, grid=(NSH, nb),
            in_specs=[pl.BlockSpec((1, tr, C), lambda j, i, c_ref: (j, c_ref[0] * nb + i, 0))] * n + [spec] * n,
            out_specs=[spec] * n),
        out_shape=[jax.ShapeDtypeStruct(ss[0].shape, BF16)] * n,
        compiler_params=_params("parallel", "parallel"),
    )(core, *gs, *ss)


def _adamw(wgmv, name):
    n = len(wgmv)
    _, R, C = wgmv[0][0].shape
    tr = _tile(R, max(8, (1 << 18) // (n * C)), 8)
    c1 = 1.0 / (1.0 - B1 ** STEP)
    c2 = 1.0 / (1.0 - B2 ** STEP)

    def body(*refs):
        for t in range(n):
            w_ref, g_ref, m_ref, v_ref = refs[4 * t:4 * t + 4]
            go_ref, d_ref, nm_ref, nv_ref = refs[4 * n + 4 * t:4 * n + 4 * t + 4]
            gv = g_ref[...]
            go_ref[...] = gv
            nm = B1 * m_ref[...] + (1.0 - B1) * gv
            nv = B2 * v_ref[...] + (1.0 - B2) * gv * gv
            nm_ref[...] = nm
            nv_ref[...] = nv
            d_ref[...] = -LR * ((nm * c1) / (jnp.sqrt(nv * c2) + AEPS) + WD * w_ref[...])

    spec = pl.BlockSpec((1, tr, C), lambda i: (0, i, 0))
    res = pl.pallas_call(
        body, name=name, grid=(R // tr,),
        out_shape=[jax.ShapeDtypeStruct((1, R, C), F32)] * (4 * n),
        in_specs=[spec] * (4 * n), out_specs=[spec] * (4 * n),
        compiler_params=_params("parallel"),
    )(*[a for four in wgmv for a in four])
    return [tuple(res[4 * t:4 * t + 4]) for t in range(n)]


def _ffn_fwd(h, g, w1, w3, w2, name, comm=None):
    L = h.shape[0]
    tm = _tile(L, 704)

    def body(h_ref, g_ref, w1_ref, w3_ref, w2_ref, o_ref, a_ref, b_ref, n_s, acc_s):
        j = pl.program_id(1)

        @pl.when(j == 0)
        def _():
            hv = h_ref[...]
            n, _ = _rms(hv, g_ref[...])
            n_s[...] = n.astype(BF16)
            acc_s[...] = hv

        n = n_s[...]
        a = _dot_nt(n, w1_ref[0])
        b = _dot_nt(n, w3_ref[0])
        a_ref[0] = a.astype(BF16)
        b_ref[0] = b.astype(BF16)
        s = (a * _sigmoid(a) * b).astype(BF16)
        acc_s[...] += 0.5 * _dot(s, w2_ref[0])

        @pl.when(j == NSH - 1)
        def _():
            o_ref[...] = acc_s[...]

    row = pl.BlockSpec((tm, D), lambda i, j: (i, 0))
    hid = pl.BlockSpec((1, tm, FS), lambda i, j: (j, i, 0))
    wsp = pl.BlockSpec((1, FS, D), lambda i, j: (j, 0, 0))
    return _call(
        body, comm, name=name, grid=(L // tm, NSH),
        out_shape=[jax.ShapeDtypeStruct((L, D), F32),
                   jax.ShapeDtypeStruct((NSH, L, FS), BF16), jax.ShapeDtypeStruct((NSH, L, FS), BF16)],
        in_specs=[row, _res((1, D)), wsp, wsp, wsp],
        out_specs=[row, hid, hid],
        scratch_shapes=[pltpu.VMEM((tm, D), BF16), pltpu.VMEM((tm, D), F32)],
        params=_params("arbitrary", "arbitrary"),
    )(h, g, w1, w3, w2)


def _loss_head(hv, gv, tv, row0):
    y, r = _rms(hv, gv)
    row = row0 + lax.broadcasted_iota(jnp.int32, (hv.shape[0], 1), 0)
    e = jnp.where(row >= FRONT, y - tv, 0.0)
    dy = e * (1.0 / D)
    part = 0.5 * jnp.sum(jnp.sum(e * dy, axis=1, keepdims=True), axis=0, keepdims=True)
    dx, xh = _rms_bwd(dy, hv, r, gv)
    return dx, part, jnp.sum(dy * xh, axis=0, keepdims=True)


def _ffn_fwd_loss(h, g, w1, w3, w2, gf, tgt, name):
    L = h.shape[0]
    tm = _tile(L, 704)

    def body(h_ref, g_ref, w1_ref, w3_ref, w2_ref, gf_ref, t_ref, o_ref, a_ref, b_ref, loss_ref, dgf_ref,
             n_s, acc_s):
        i, j = pl.program_id(0), pl.program_id(1)

        @pl.when(j == 0)
        def _():
            hv = h_ref[...]
            n, _ = _rms(hv, g_ref[...])
            n_s[...] = n.astype(BF16)
            acc_s[...] = hv

        n = n_s[...]
        a = _dot_nt(n, w1_ref[0])
        b = _dot_nt(n, w3_ref[0])
        a_ref[0] = a.astype(BF16)
        b_ref[0] = b.astype(BF16)
        s = (a * _sigmoid(a) * b).astype(BF16)
        acc_s[...] += 0.5 * _dot(s, w2_ref[0])

        @pl.when(j == NSH - 1)
        def _():
            dx, part, dgf = _loss_head(acc_s[...], gf_ref[...], t_ref[...], i * tm)
            o_ref[...] = dx
            _acc_rows(loss_ref, part, i == 0)
            _acc_rows(dgf_ref, dgf, i == 0)

    row = pl.BlockSpec((tm, D), lambda i, j: (i, 0))
    hid = pl.BlockSpec((1, tm, FS), lambda i, j: (j, i, 0))
    wsp = pl.BlockSpec((1, FS, D), lambda i, j: (j, 0, 0))
    return pl.pallas_call(
        body, name=name, grid=(L // tm, NSH),
        out_shape=[jax.ShapeDtypeStruct((L, D), F32),
                   jax.ShapeDtypeStruct((NSH, L, FS), BF16), jax.ShapeDtypeStruct((NSH, L, FS), BF16),
                   jax.ShapeDtypeStruct((1, 1), F32), jax.ShapeDtypeStruct((1, D), F32)],
        in_specs=[row, _res((1, D)), wsp, wsp, wsp, _res((1, D)), row],
        out_specs=[row, hid, hid, pl.BlockSpec((1, 1), lambda i, j: (0, 0)),
                   pl.BlockSpec((1, D), lambda i, j: (0, 0))],
        scratch_shapes=[pltpu.VMEM((tm, D), BF16), pltpu.VMEM((tm, D), F32)],
        compiler_params=_params("arbitrary", "arbitrary"),
    )(h, g, w1, w3, w2, gf, tgt)


def _ffn_bwd(h, g, dout, a, b, w1, w3, w2, name, comm=None):
    L = h.shape[0]
    tm = _tile(L, 528)

    def body(h_ref, g_ref, do_ref, a_ref, b_ref, w1_ref, w3_ref, w2_ref,
             dh_ref, da_ref, db_ref, s_ref, n_ref, dg_ref, dob_s, dn_s):
        i, j = pl.program_id(0), pl.program_id(1)

        @pl.when(j == 0)
        def _():
            n, _ = _rms(h_ref[...], g_ref[...])
            n_ref[...] = n.astype(BF16)
            dob_s[...] = (0.5 * do_ref[...]).astype(BF16)
            dn_s[...] = jnp.zeros_like(dn_s)

        av = a_ref[0].astype(F32)
        bv = b_ref[0].astype(F32)
        sig = _sigmoid(av)
        sa = av * sig
        ds = _dot_nt(dob_s[...], w2_ref[0])
        s_ref[0] = (sa * bv).astype(BF16)
        da = (ds * bv * (sig + sa * (1.0 - sig))).astype(BF16)
        db = (ds * sa).astype(BF16)
        da_ref[0] = da
        db_ref[0] = db
        dn_s[...] += _dot(da, w1_ref[0]) + _dot(db, w3_ref[0])

        @pl.when(j == NSH - 1)
        def _():
            hv = h_ref[...]
            gv = g_ref[...]
            r = lax.rsqrt(jnp.mean(hv * hv, axis=-1, keepdims=True) + EPS)
            dn = dn_s[...]
            dx, xh = _rms_bwd(dn, hv, r, gv)
            dh_ref[...] = do_ref[...] + dx
            _acc_rows(dg_ref, jnp.sum(dn * xh, axis=0, keepdims=True), i == 0)

    row = pl.BlockSpec((tm, D), lambda i, j: (i, 0))
    hid = pl.BlockSpec((1, tm, FS), lambda i, j: (j, i, 0))
    wsp = pl.BlockSpec((1, FS, D), lambda i, j: (j, 0, 0))
    return _call(
        body, comm, name=name, grid=(L // tm, NSH),
        out_shape=[jax.ShapeDtypeStruct((L, D), F32)]
        + [jax.ShapeDtypeStruct((NSH, L, FS), BF16)] * 3
        + [jax.ShapeDtypeStruct((L, D), BF16), jax.ShapeDtypeStruct((1, D), F32)],
        in_specs=[row, _res((1, D)), row, hid, hid,
                  wsp, wsp, wsp],
        out_specs=[row, hid, hid, hid, row, pl.BlockSpec((1, D), lambda i, j: (0, 0))],
        scratch_shapes=[pltpu.VMEM((tm, D), BF16), pltpu.VMEM((tm, D), F32)],
        params=_params("arbitrary", "arbitrary"),
    )(h, g, dout, a, b, w1, w3, w2)


def _wgrad(xm, ym, name, scale=1.0):
    xs, ys = xm.ndim == 3, ym.ndim == 3
    assert not (xs and ys)
    L = xm.shape[-2]
    K, N = xm.shape[-1], ym.shape[-1]
    tl = _tile(L, 2112)
    nl = L // tl
    if xs or ys:
        tn, grid_n = N, NSH
    else:
        tn = _tile(N, 1024, 128)
        grid_n = N // tn

    def body(x_ref, y_ref, o_ref, acc_s):
        l = pl.program_id(1)
        xv = x_ref[0] if xs else x_ref[...]
        yv = y_ref[0] if ys else y_ref[...]
        part = _dot_tn(xv.astype(BF16), yv.astype(BF16))
        _acc_rows(acc_s, part, l == 0)

        @pl.when(l == nl - 1)
        def _():
            res = (acc_s[...] * scale).astype(BF16)
            if xs or ys:
                o_ref[0] = res
            else:
                o_ref[...] = res

    if xs:
        x_spec = pl.BlockSpec((1, tl, K), lambda n, l: (n, l, 0))
        y_spec = pl.BlockSpec((tl, N), lambda n, l: (l, 0))
        o_spec = pl.BlockSpec((1, K, N), lambda n, l: (n, 0, 0))
        o_shape = (NSH, K, N)
    elif ys:
        x_spec = pl.BlockSpec((tl, K), lambda n, l: (l, 0))
        y_spec = pl.BlockSpec((1, tl, N), lambda n, l: (n, l, 0))
        o_spec = pl.BlockSpec((1, K, N), lambda n, l: (n, 0, 0))
        o_shape = (NSH, K, N)
    else:
        x_spec = pl.BlockSpec((tl, K), lambda n, l: (l, 0))
        y_spec = pl.BlockSpec((tl, tn), lambda n, l: (l, n))
        o_spec = pl.BlockSpec((K, tn), lambda n, l: (0, n))
        o_shape = (K, N)
    return pl.pallas_call(
        body, name=name, grid=(grid_n, nl),
        out_shape=jax.ShapeDtypeStruct(o_shape, BF16),
        in_specs=[x_spec, y_spec], out_specs=o_spec,
        scratch_shapes=[pltpu.VMEM((K, tn), F32)],
        compiler_params=_params("parallel", "arbitrary"),
    )(xm, ym)


def _mix_in_fwd(h, g, w_in, b_gate, name, comm=None):
    L = h.shape[0]
    tm = _tile(L, 528)

    def body(h_ref, g_ref, w_ref, bg_ref, vg_ref, uf_ref, gt_ref):
        u, _ = _rms(h_ref[...], g_ref[...])
        ub = u.astype(BF16)
        p = [_dot(ub, w_ref[j]) for j in range(NSH)]
        a0, a1 = 2 * DC - WS, 2 * DC + DS - WS
        vg_ref[:, 0:WS] = p[0].astype(BF16)
        vg_ref[:, WS:2 * DC] = p[1][:, 0:a0].astype(BF16)
        uf_ref[...] = p[1][:, a0:a1].astype(BF16)
        gin = jnp.concatenate([p[1][:, a1:], p[2], p[3]], axis=1)
        gt_ref[...] = _sigmoid(gin + bg_ref[...]).astype(BF16)

    def row(n):
        return pl.BlockSpec((tm, n), lambda i: (i, 0))

    return _call(
        body, comm, name=name, grid=(L // tm,),
        out_shape=[jax.ShapeDtypeStruct((L, 2 * DC), BF16), jax.ShapeDtypeStruct((L, DS), BF16),
                   jax.ShapeDtypeStruct((L, 2 * D), BF16)],
        in_specs=[row(D), _res((1, D)), _res((NSH, D, WS)), _res((1, 2 * D))],
        out_specs=[row(2 * DC), row(DS), row(2 * D)],
        params=_params("parallel"),
    )(h, g, w_in, b_gate)


def _mix_in_bwd(h, g, dres, dv, dgl, duf, dgate, w_in, name):
    L = h.shape[0]
    tm = _tile(L, 528)

    def body(h_ref, g_ref, dr_ref, dv_ref, dgl_ref, duf_ref, dgt_ref, w_ref, dh_ref, u_ref, dp_ref, dgm_ref):
        i = pl.program_id(0)
        hv = h_ref[...]
        gv = g_ref[...]
        u, r = _rms(hv, gv)
        u_ref[...] = u.astype(BF16)
        a0, a1 = 2 * DC - WS, 2 * DC + DS - WS
        b0 = WS - a1
        dp = [jnp.concatenate([dv_ref[...], dgl_ref[:, 0:WS - DC]], axis=1),
              jnp.concatenate([dgl_ref[:, WS - DC:], duf_ref[...], dgt_ref[:, 0:b0]], axis=1),
              dgt_ref[:, b0:b0 + WS], dgt_ref[:, b0 + WS:]]
        du = jnp.zeros((tm, D), F32)
        for j in range(NSH):
            dp_ref[j] = dp[j]
            du = du + _dot_nt(dp[j], w_ref[j])
        dx, xh = _rms_bwd(du, hv, r, gv)
        dh_ref[...] = dr_ref[...] + dx
        _acc_rows(dgm_ref, jnp.sum(du * xh, axis=0, keepdims=True), i == 0)

    def row(n):
        return pl.BlockSpec((tm, n), lambda i: (i, 0))

    return pl.pallas_call(
        body, name=name, grid=(L // tm,),
        out_shape=[jax.ShapeDtypeStruct((L, D), F32), jax.ShapeDtypeStruct((L, D), BF16),
                   jax.ShapeDtypeStruct((NSH, L, WS), BF16), jax.ShapeDtypeStruct((1, D), F32)],
        in_specs=[row(D), _res((1, D)), row(D), row(DC), row(DC), row(DS), row(2 * D), _res((NSH, D, WS))],
        out_specs=[row(D), row(D), pl.BlockSpec((NSH, tm, WS), lambda i: (0, i, 0)),
                   pl.BlockSpec((1, D), lambda i: (0, 0))],
        compiler_params=_params("arbitrary"),
    )(h, g, dres, dv, dgl, duf, dgate, w_in)


def _conv_fwd(vg, dw, dwb, name, comm=None):
    L = vg.shape[0]
    nc = DC // 128

    def body(v_ref, g_ref, dw_ref, dwb_ref, z_ref, zp_s):
        zp_s[0:KWP, :] = jnp.zeros((KWP, 128), F32)
        zp_s[KWP:, :] = v_ref[...].astype(F32) * _sigmoid(g_ref[...].astype(F32))
        for r0 in range(0, L, CONV_ROWS):
            acc = jnp.broadcast_to(dwb_ref[...], (CONV_ROWS, 128))
            for k in range(KW):
                acc = acc + dw_ref[k:k + 1, :] * zp_s[pl.ds(r0 + k + 2, CONV_ROWS), :]
            z_ref[pl.ds(r0, CONV_ROWS), :] = acc

    return _call(
        body, comm, name=name, grid=(nc,),
        out_shape=[jax.ShapeDtypeStruct((L, DC), F32)],
        in_specs=[pl.BlockSpec((L, 128), lambda c: (0, c)), pl.BlockSpec((L, 128), lambda c: (0, nc + c)),
                  pl.BlockSpec((KWP, 128), lambda c: (0, c)), pl.BlockSpec((1, 128), lambda c: (0, c))],
        out_specs=[pl.BlockSpec((L, 128), lambda c: (0, c))],
        scratch_shapes=[pltpu.VMEM((L + KWP, 128), F32)],
        params=_params("parallel"),
    )(vg, vg, dw, dwb)


def _conv_bwd(dz1, vg, dw, name):
    L = vg.shape[0]
    nc = DC // 128

    def body(dz_ref, v_ref, g_ref, dw_ref, dv_ref, dg_ref, ddw_ref, ddwb_ref, zp_s, dzp_s):
        vv = v_ref[...].astype(F32)
        sg = _sigmoid(g_ref[...].astype(F32))
        zp_s[0:KWP, :] = jnp.zeros((KWP, 128), F32)
        zp_s[KWP:, :] = vv * sg
        dz = dz_ref[...]
        dzp_s[0:L, :] = dz
        dzp_s[L:, :] = jnp.zeros((KWP, 128), F32)
        ddwb_ref[...] = jnp.sum(dz, axis=0, keepdims=True)
        part = [jnp.zeros((8, 128), F32) for _ in range(KW)]
        for r0 in range(0, L, CONV_ROWS):
            rows = pl.ds(r0, CONV_ROWS)
            dzc = dz_ref[rows, :]
            acc = jnp.zeros((CONV_ROWS, 128), F32)
            for k in range(KW):
                acc = acc + dw_ref[k:k + 1, :] * dzp_s[pl.ds(r0 + KW - 1 - k, CONV_ROWS), :]
                prod = dzc * zp_s[pl.ds(r0 + k + 2, CONV_ROWS), :]
                for q in range(CONV_ROWS // 8):
                    part[k] = part[k] + prod[8 * q:8 * (q + 1), :]
            vc = v_ref[rows, :].astype(F32)
            sc = _sigmoid(g_ref[rows, :].astype(F32))
            dv_ref[rows, :] = (acc * sc).astype(BF16)
            dg_ref[rows, :] = (acc * vc * sc * (1.0 - sc)).astype(BF16)
        for k in range(KW):
            ddw_ref[k:k + 1, :] = jnp.sum(part[k], axis=0, keepdims=True)
        ddw_ref[KW:KWP, :] = jnp.zeros((KWP - KW, 128), F32)

    col = pl.BlockSpec((L, 128), lambda c: (0, c))
    return pl.pallas_call(
        body, name=name, grid=(nc,),
        out_shape=[jax.ShapeDtypeStruct((L, DC), BF16), jax.ShapeDtypeStruct((L, DC), BF16),
                   jax.ShapeDtypeStruct((KWP, DC), F32), jax.ShapeDtypeStruct((1, DC), F32)],
        in_specs=[col, col, pl.BlockSpec((L, 128), lambda c: (0, nc + c)),
                  pl.BlockSpec((KWP, 128), lambda c: (0, c))],
        out_specs=[col, col, pl.BlockSpec((KWP, 128), lambda c: (0, c)), pl.BlockSpec((1, 128), lambda c: (0, c))],
        scratch_shapes=[pltpu.VMEM((L + KWP, 128), F32), pltpu.VMEM((L + KWP, 128), F32)],
        compiler_params=_params("parallel"),
    )(dz1, vg, vg, dw)


NLB = QS // 128


def _lb_store(ref, rows, val):
    for cb in range(NLB):
        ref[cb, rows, :] = val[:, cb * 128:(cb + 1) * 128]


def _lb_load(ref, rows):
    return jnp.concatenate([ref[cb, rows, :] for cb in range(NLB)], axis=1)


def _scan(xr_ref, xi_ref, base, T, ar, ai, atr, ati, reverse):
    W = ar.shape[1]
    ar, ai, atr, ati = (jnp.broadcast_to(v, (8, W)) for v in (ar, ai, atr, ati))
    zero = jnp.zeros((8, W), F32)

    def rows(t, g):
        tt = T - 1 - t if reverse else t
        return pl.ds(base + g * 8 * T + tt, 8, stride=T)

    def make_step(store):
        def step(t, carry):
            out = []
            for g in range(NGRP):
                sr, si = carry[2 * g], carry[2 * g + 1]
                idx = rows(t, g)
                nr = ar * sr - ai * si + _lb_load(xr_ref, idx)
                ni = ar * si + ai * sr + _lb_load(xi_ref, idx)
                if store:
                    _lb_store(xr_ref, idx, nr)
                    _lb_store(xi_ref, idx, ni)
                out += [nr, ni]
            return tuple(out)
        return step

    ends = lax.fori_loop(0, T, make_step(False), (zero,) * (2 * NGRP))
    sub = lax.broadcasted_iota(jnp.int32, (8, W), 0)
    edge = sub == (7 if reverse else 0)
    shift, last = (7, 0) if reverse else (1, 7)
    inr, ini = jnp.zeros((1, W), F32), jnp.zeros((1, W), F32)
    starts = [None] * (2 * NGRP)
    for g in (reversed(range(NGRP)) if reverse else range(NGRP)):
        er, ei = ends[2 * g], ends[2 * g + 1]
        cr, ci = jnp.where(edge, inr, 0.0), jnp.where(edge, ini, 0.0)
        for _ in range(7):
            nr = atr * cr - ati * ci + er
            ni = atr * ci + ati * cr + ei
            cr = jnp.where(edge, inr, pltpu.roll(nr, shift, 0))
            ci = jnp.where(edge, ini, pltpu.roll(ni, shift, 0))
        starts[2 * g], starts[2 * g + 1] = cr, ci
        inr = (atr * cr - ati * ci + er)[last:last + 1]
        ini = (atr * ci + ati * cr + ei)[last:last + 1]
    lax.fori_loop(0, T, make_step(True), tuple(starts))


def _ssm_fwd(uf, bre, bim, cre, cim, lamp, dsk, name, comm=None):
    L = uf.shape[0]
    T = L // NSEG
    tc = L // NCH

    def body(u_ref, bre_ref, bim_ref, cre_ref, cim_ref, lam_ref, d_ref, y_ref, sr_s, si_s):
        for k in range(NCH):
            sl = slice(k * tc, (k + 1) * tc)
            uk = u_ref[sl, :]
            _lb_store(sr_s, sl, _dot(uk, bre_ref[0]))
            _lb_store(si_s, sl, _dot(uk, bim_ref[0]))
        _scan(sr_s, si_s, 0, T, lam_ref[0:1, :], lam_ref[1:2, :], lam_ref[2:3, :], lam_ref[3:4, :], False)
        for k in range(NCH):
            sl = slice(k * tc, (k + 1) * tc)
            y_ref[sl, :] = (_dot(_lb_load(sr_s, sl).astype(BF16), cre_ref[0])
                            - _dot(_lb_load(si_s, sl).astype(BF16), cim_ref[0])
                            + d_ref[...] * u_ref[sl, :].astype(F32))

    return _call(
        body, comm, name=name, grid=(NQ,),
        out_shape=[jax.ShapeDtypeStruct((L, DS), F32)],
        in_specs=[pl.BlockSpec((L, QU), lambda q: (0, q)),
                  pl.BlockSpec((1, QU, QS), lambda q: (q, 0, 0)), pl.BlockSpec((1, QU, QS), lambda q: (q, 0, 0)),
                  pl.BlockSpec((1, QS, QU), lambda q: (q, 0, 0)), pl.BlockSpec((1, QS, QU), lambda q: (q, 0, 0)),
                  pl.BlockSpec((8, QS), lambda q: (0, q)), pl.BlockSpec((1, QU), lambda q: (0, q))],
        out_specs=[pl.BlockSpec((L, QU), lambda q: (0, q))],
        scratch_shapes=[pltpu.VMEM((NLB, L, 128), F32), pltpu.VMEM((NLB, L, 128), F32)],
        params=_params("parallel"),
    )(uf, bre, bim, cre, cim, lamp, dsk)


def _ssm_bwd(uf, dyss, bre, bim, cre, cim, lamp, dsk, name, comm=None):
    L = uf.shape[0]
    T = L // NSEG
    tc = L // NCH

    def body(u_ref, dy_ref, bre_ref, bim_ref, cre_ref, cim_ref, lam_ref, d_ref,
             du_ref, dbre_ref, dbim_ref, dcre_ref, dcim_ref, dlam_ref, dd_ref, sr_s, si_s, gr_s, gi_s):
        _lb_store(sr_s, slice(0, SOFF), jnp.zeros((SOFF, QS), F32))
        _lb_store(si_s, slice(0, SOFF), jnp.zeros((SOFF, QS), F32))
        for k in range(NCH):
            sl = slice(k * tc, (k + 1) * tc)
            ss = slice(SOFF + k * tc, SOFF + (k + 1) * tc)
            uk = u_ref[sl, :]
            dyk = dy_ref[sl, :].astype(BF16)
            _lb_store(sr_s, ss, _dot(uk, bre_ref[0]))
            _lb_store(si_s, ss, _dot(uk, bim_ref[0]))
            _lb_store(gr_s, sl, _dot_nt(dyk, cre_ref[0]))
            _lb_store(gi_s, sl, -_dot_nt(dyk, cim_ref[0]))
        ar, ai, atr, ati = lam_ref[0:1, :], lam_ref[1:2, :], lam_ref[2:3, :], lam_ref[3:4, :]
        _scan(sr_s, si_s, SOFF, T, ar, ai, atr, ati, False)
        _scan(gr_s, gi_s, 0, T, ar, -ai, atr, -ati, True)
        dbre = jnp.zeros((QU, QS), F32)
        dbim = jnp.zeros((QU, QS), F32)
        dcre = jnp.zeros((QS, QU), F32)
        dcim = jnp.zeros((QS, QU), F32)
        dd = jnp.zeros((1, QU), F32)
        qr = jnp.zeros((1, QS), F32)
        qi = jnp.zeros((1, QS), F32)
        for k in range(NCH):
            sl = slice(k * tc, (k + 1) * tc)
            ss = slice(SOFF + k * tc, SOFF + (k + 1) * tc)
            sp = slice(SOFF - 1 + k * tc, SOFF - 1 + (k + 1) * tc)
            uk = u_ref[sl, :]
            dyk = dy_ref[sl, :]
            dyb = dyk.astype(BF16)
            gr, gi = _lb_load(gr_s, sl), _lb_load(gi_s, sl)
            pr, pi = _lb_load(sr_s, sp), _lb_load(si_s, sp)
            qr = qr + jnp.sum(gr * pr + gi * pi, axis=0, keepdims=True)
            qi = qi + jnp.sum(gi * pr - gr * pi, axis=0, keepdims=True)
            grb, gib = gr.astype(BF16), gi.astype(BF16)
            du_ref[sl, :] = (_dot_nt(grb, bre_ref[0]) + _dot_nt(gib, bim_ref[0])
                             + dyk * d_ref[...]).astype(BF16)
            dbre = dbre + _dot_tn(uk, grb)
            dbim = dbim + _dot_tn(uk, gib)
            dcre = dcre + _dot_tn(_lb_load(sr_s, ss).astype(BF16), dyb)
            dcim = dcim - _dot_tn(_lb_load(si_s, ss).astype(BF16), dyb)
            dd = dd + jnp.sum(dyk * uk.astype(F32), axis=0, keepdims=True)
        dlam_ref[0] = jnp.concatenate([qr, qi, jnp.zeros((6, QS), F32)], axis=0)
        dbre_ref[0] = dbre
        dbim_ref[0] = dbim
        dcre_ref[0] = dcre
        dcim_ref[0] = dcim
        dd_ref[...] = dd

    col = pl.BlockSpec((L, QU), lambda q: (0, q))
    bsp = pl.BlockSpec((1, QU, QS), lambda q: (q, 0, 0))
    csp = pl.BlockSpec((1, QS, QU), lambda q: (q, 0, 0))
    return _call(
        body, comm, name=name, grid=(NQ,),
        out_shape=[jax.ShapeDtypeStruct((L, DS), BF16),
                   jax.ShapeDtypeStruct((NQ, QU, QS), F32), jax.ShapeDtypeStruct((NQ, QU, QS), F32),
                   jax.ShapeDtypeStruct((NQ, QS, QU), F32), jax.ShapeDtypeStruct((NQ, QS, QU), F32),
                   jax.ShapeDtypeStruct((NQ, 8, QS), F32), jax.ShapeDtypeStruct((1, DS), F32)],
        in_specs=[col, col, bsp, bsp, csp, csp,
                  pl.BlockSpec((8, QS), lambda q: (0, q)), pl.BlockSpec((1, QU), lambda q: (0, q))],
        out_specs=[col,
                   pl.BlockSpec((1, QU, QS), lambda q: (q, 0, 0)), pl.BlockSpec((1, QU, QS), lambda q: (q, 0, 0)),
                   pl.BlockSpec((1, QS, QU), lambda q: (q, 0, 0)), pl.BlockSpec((1, QS, QU), lambda q: (q, 0, 0)),
                   pl.BlockSpec((1, 8, QS), lambda q: (q, 0, 0)), pl.BlockSpec((1, QU), lambda q: (0, q))],
        scratch_shapes=[pltpu.VMEM((NLB, L + SOFF, 128), F32), pltpu.VMEM((NLB, L + SOFF, 128), F32),
                        pltpu.VMEM((NLB, L, 128), F32), pltpu.VMEM((NLB, L, 128), F32)],
        params=_params("parallel"),
    )(uf, dyss, bre, bim, cre, cim, lamp, dsk)


def _branches(z1_ref, yss_ref, gt_ref, lng_ref, lnb_ref, wp_ref, wv_ref, wg_ref):
    zf = z1_ref[...]
    mu = jnp.mean(zf, axis=-1, keepdims=True)
    zc = zf - mu
    rstd = lax.rsqrt(jnp.mean(zc * zc, axis=-1, keepdims=True) + EPS)
    zn = zc * rstd
    z2 = zn * lng_ref[...] + lnb_ref[...]
    sz = _sigmoid(z2)
    z3 = (z2 * sz).astype(BF16)
    y_conv = _dot(z3, wp_ref[...])
    yss = yss_ref[...]
    yg = _gelu(yss).astype(BF16)
    sv = _dot(yg, wv_ref[...])
    sig = _sigmoid(_dot(yg, wg_ref[...]))
    y_ssm = sv * sig
    gc = gt_ref[:, 0:D].astype(F32)
    gs = gt_ref[:, D:2 * D].astype(F32)
    m = gc * y_conv + gs * y_ssm
    return dict(rstd=rstd, zn=zn, z2=z2, sz=sz, z3=z3, y_conv=y_conv, yss=yss, yg=yg, sv=sv, sig=sig,
                y_ssm=y_ssm, gc=gc, gs=gs, m=m)


def _merge_fwd(h, z1, yss, gate, lng, lnb, wp, wv, wg, wo, name, comm=None):
    L = h.shape[0]
    tm = _tile(L, 528)

    def body(h_ref, z1_ref, yss_ref, gt_ref, lng_ref, lnb_ref, wp_ref, wv_ref, wg_ref, wo_ref, o_ref):
        f = _branches(z1_ref, yss_ref, gt_ref, lng_ref, lnb_ref, wp_ref, wv_ref, wg_ref)
        o_ref[...] = h_ref[...] + _dot(f["m"].astype(BF16), wo_ref[...])

    def row(n):
        return pl.BlockSpec((tm, n), lambda i: (i, 0))

    return _call(
        body, comm, name=name, grid=(L // tm,),
        out_shape=[jax.ShapeDtypeStruct((L, D), F32)],
        in_specs=[row(D), row(DC), row(DS), row(2 * D), _res((1, DC)), _res((1, DC)),
                  _res((DC, D)), _res((DS, D)), _res((DS, D)), _res((D, D))],
        out_specs=[row(D)],
        params=_params("parallel"),
    )(h, z1, yss, gate, lng, lnb, wp, wv, wg, wo)


def _merge_bwd(dh, z1, yss, gate, lng, lnb, wp, wv, wg, wo, name):
    L = dh.shape[0]
    tm = _tile(L, 352)

    def body(dh_ref, z1_ref, yss_ref, gt_ref, lng_ref, lnb_ref, wp_ref, wv_ref, wg_ref, wo_ref,
             m_ref, dgt_ref, dyc_ref, z3_ref, dz1_ref, yg_ref, dsv_ref, dsg_ref, dyss_ref,
             dbg_ref, dlng_ref, dlnb_ref):
        i = pl.program_id(0)
        f = _branches(z1_ref, yss_ref, gt_ref, lng_ref, lnb_ref, wp_ref, wv_ref, wg_ref)
        gc, gs, sig, sv = f["gc"], f["gs"], f["sig"], f["sv"]
        m_ref[...] = f["m"].astype(BF16)
        z3_ref[...] = f["z3"]
        yg_ref[...] = f["yg"]
        dm = _dot_nt(dh_ref[...].astype(BF16), wo_ref[...])
        dgc = (dm * f["y_conv"] * gc * (1.0 - gc)).astype(BF16)
        dgs = (dm * f["y_ssm"] * gs * (1.0 - gs)).astype(BF16)
        dgt_ref[:, 0:D] = dgc
        dgt_ref[:, D:2 * D] = dgs
        part = jnp.concatenate([jnp.sum(dgc.astype(F32), axis=0, keepdims=True),
                                jnp.sum(dgs.astype(F32), axis=0, keepdims=True)], axis=1)
        _acc_rows(dbg_ref, part, i == 0)
        dyc = (dm * gc).astype(BF16)
        dyc_ref[...] = dyc
        dys = dm * gs
        dsv = (dys * sig).astype(BF16)
        dsg = (dys * sv * sig * (1.0 - sig)).astype(BF16)
        dsv_ref[...] = dsv
        dsg_ref[...] = dsg
        dyg = _dot_nt(dsv, wv_ref[...]) + _dot_nt(dsg, wg_ref[...])
        dyss_ref[...] = dyg * _gelu_grad(f["yss"])
        dz3 = _dot_nt(dyc, wp_ref[...])
        z2, sz, zn = f["z2"], f["sz"], f["zn"]
        dz2 = dz3 * sz * (1.0 + z2 * (1.0 - sz))
        _acc_rows(dlng_ref, jnp.sum(dz2 * zn, axis=0, keepdims=True), i == 0)
        _acc_rows(dlnb_ref, jnp.sum(dz2, axis=0, keepdims=True), i == 0)
        dzn = dz2 * lng_ref[...]
        dz1_ref[...] = f["rstd"] * (dzn - jnp.mean(dzn, axis=-1, keepdims=True)
                                    - zn * jnp.mean(dzn * zn, axis=-1, keepdims=True))

    def row(n):
        return pl.BlockSpec((tm, n), lambda i: (i, 0))

    def tot(n):
        return pl.BlockSpec((1, n), lambda i: (0, 0))

    return pl.pallas_call(
        body, name=name, grid=(L // tm,),
        out_shape=[jax.ShapeDtypeStruct((L, D), BF16), jax.ShapeDtypeStruct((L, 2 * D), BF16),
                   jax.ShapeDtypeStruct((L, D), BF16), jax.ShapeDtypeStruct((L, DC), BF16),
                   jax.ShapeDtypeStruct((L, DC), F32), jax.ShapeDtypeStruct((L, DS), BF16),
                   jax.ShapeDtypeStruct((L, D), BF16), jax.ShapeDtypeStruct((L, D), BF16),
                   jax.ShapeDtypeStruct((L, DS), F32),
                   jax.ShapeDtypeStruct((1, 2 * D), F32), jax.ShapeDtypeStruct((1, DC), F32),
                   jax.ShapeDtypeStruct((1, DC), F32)],
        in_specs=[row(D), row(DC), row(DS), row(2 * D), _res((1, DC)), _res((1, DC)),
                  _res((DC, D)), _res((DS, D)), _res((DS, D)), _res((D, D))],
        out_specs=[row(D), row(2 * D), row(D), row(DC), row(DC), row(DS), row(D), row(D), row(DS),
                   tot(2 * D), tot(DC), tot(DC)],
        compiler_params=_params("arbitrary"),
    )(dh, z1, yss, gate, lng, lnb, wp, wv, wg, wo)


def _ssm_disc(lam_re, lam_im, log_dt, b_re, b_im):
    lam = lax.complex(lam_re, lam_im)
    dt = jnp.exp(log_dt)[:, None]
    lam_bar = jnp.exp(lam * dt)
    bbar = ((lam_bar - 1.0) / lam)[..., None] * lax.complex(b_re, b_im)
    return jnp.real(lam_bar), jnp.imag(lam_bar), jnp.real(bbar), jnp.imag(bbar)


def _bdiag_in(m):
    m4 = m.reshape(NQ, G // NQ, P, H)
    return jnp.einsum("qgph,gk->qghkp", m4, jnp.eye(G // NQ, dtype=m.dtype)).reshape(NQ, QU, QS)


def _bdiag_out(m):
    m4 = m.reshape(NQ, G // NQ, H, P)
    return jnp.einsum("qghp,gk->qgpkh", m4, jnp.eye(G // NQ, dtype=m.dtype)).reshape(NQ, QS, QU)


def _diag_blocks(m4):
    return jnp.einsum("qiaib->qiab", m4).reshape(G, m4.shape[2], m4.shape[4])


def _pack(parts, rows_mult=8):
    flat = jnp.concatenate([p.reshape(-1).astype(F32) for p in parts])
    n = flat.shape[0]
    tot = -(-n // (128 * rows_mult)) * (128 * rows_mult)
    return jnp.pad(flat, (0, tot - n)).reshape(tot // 128, 128)


def _unpack(buf, shapes):
    flat = buf.reshape(-1)
    out, o = [], 0
    for s in shapes:
        n = math.prod(s)
        out.append(flat[o:o + n].reshape(s))
        o += n
    return out


def kernel(x, meta_tokens, ffn1_norm, ffn1_w1, ffn1_w3, ffn1_w2, mix_norm, w_in, b_gate, conv_dw, conv_dw_b, conv_ln_g, conv_ln_b, conv_proj, ssm_lam_re, ssm_lam_im, ssm_log_dt, ssm_b_re, ssm_b_im, ssm_c_re, ssm_c_im, ssm_d, ssm_w_v, ssm_w_g, w_out, ffn2_norm, ffn2_w1, ffn2_w3, ffn2_w2, final_norm, loss_target, m_meta_tokens, m_ffn1_norm, m_ffn1_w1, m_ffn1_w3, m_ffn1_w2, m_mix_norm, m_w_in, m_b_gate, m_conv_dw, m_conv_dw_b, m_conv_ln_g, m_conv_ln_b, m_conv_proj, m_ssm_lam_re, m_ssm_lam_im, m_ssm_log_dt, m_ssm_b_re, m_ssm_b_im, m_ssm_c_re, m_ssm_c_im, m_ssm_d, m_ssm_w_v, m_ssm_w_g, m_w_out, m_ffn2_norm, m_ffn2_w1, m_ffn2_w3, m_ffn2_w2, m_final_norm, v_meta_tokens, v_ffn1_norm, v_ffn1_w1, v_ffn1_w3, v_ffn1_w2, v_mix_norm, v_w_in, v_b_gate, v_conv_dw, v_conv_dw_b, v_conv_ln_g, v_conv_ln_b, v_conv_proj, v_ssm_lam_re, v_ssm_lam_im, v_ssm_log_dt, v_ssm_b_re, v_ssm_b_im, v_ssm_c_re, v_ssm_c_im, v_ssm_d, v_ssm_w_v, v_ssm_w_g, v_w_out, v_ffn2_norm, v_ffn2_w1, v_ffn2_w3, v_ffn2_w2, v_final_norm):
    args = dict(locals())
    names = ["meta_tokens", "ffn1_norm", "ffn1_w1", "ffn1_w3", "ffn1_w2", "mix_norm", "w_in", "b_gate",
             "conv_dw", "conv_dw_b", "conv_ln_g", "conv_ln_b", "conv_proj", "ssm_lam_re", "ssm_lam_im",
             "ssm_log_dt", "ssm_b_re", "ssm_b_im", "ssm_c_re", "ssm_c_im", "ssm_d", "ssm_w_v", "ssm_w_g",
             "w_out", "ffn2_norm", "ffn2_w1", "ffn2_w3", "ffn2_w2", "final_norm"]
    big = ["ffn1_w1", "ffn1_w3", "ffn1_w2", "w_in", "conv_proj", "ssm_w_v", "ssm_w_g", "w_out",
           "ffn2_w1", "ffn2_w3", "ffn2_w2"]
    small = [n for n in names if n not in big]

    xs = x[0]
    S = xs.shape[0]
    L = FRONT + S
    T = L // NSEG
    jx, jy = lax.axis_index("x"), lax.axis_index("y")
    chip = 2 * jx + jy

    small_all = _gather_all(_pack([meta_tokens, conv_dw[0]]), "gather_small")
    sm = small_all[0::2].reshape(NSH, -1)
    nmt = NMETA * (D // NSH)
    ndw = KW * (DC // NSH)
    meta_full = sm[:, :nmt].reshape(NSH, NMETA, D // NSH).transpose(1, 0, 2).reshape(NMETA, D)
    dw_full = sm[:, nmt:nmt + ndw].reshape(NSH, KW, DC // NSH).transpose(1, 0, 2).reshape(KW, DC)
    dw_pad = jnp.pad(dw_full, ((0, KWP - KW), (0, 0)))
    tposed = ("ffn1_w1", "ffn1_w3", "ffn2_w1", "ffn2_w3")

    def view(a, n):
        return jnp.swapaxes(a, 1, 2) if n in tposed else a

    grp_a = ["ffn1_w1", "ffn1_w3", "ffn1_w2"]
    grp_b = ["w_in", "conv_proj", "ssm_w_v", "ssm_w_g", "w_out"]
    grp_c = ["ffn2_w1", "ffn2_w3", "ffn2_w2"]

    shards = {n: view(args[n], n)[0].astype(BF16) for n in big}

    def shard(n):
        return shards[n]

    sh_a = [shard(n) for n in grp_a]
    ga_send, ga_recv, sh_a, land_a, _ = _chips_start(
        sh_a, [jax.ShapeDtypeStruct((NSH,) + s.shape, s.dtype) for s in sh_a], True, "gather_ffn1_start",
        [small_all])

    def cols(w):
        return w.transpose(1, 0, 2).reshape(w.shape[1], -1)

    disc_in = (ssm_lam_re[0], ssm_lam_im[0], ssm_log_dt[0], ssm_b_re[0], ssm_b_im[0])
    (lbr, lbi, bbr, bbi), disc_vjp = jax.vjp(_ssm_disc, *disc_in)
    lam_t = jnp.exp(lax.complex(ssm_lam_re[0], ssm_lam_im[0]) * (jnp.exp(ssm_log_dt[0])[:, None] * T))
    lamp = jnp.concatenate([lbr.reshape(1, NST), lbi.reshape(1, NST), jnp.real(lam_t).reshape(1, NST),
                            jnp.imag(lam_t).reshape(1, NST), jnp.zeros((4, NST), F32)], axis=0)
    bre_bd, bim_bd = _bdiag_in(bbr).astype(BF16), _bdiag_in(bbi).astype(BF16)
    cre_bd, cim_bd = _bdiag_out(ssm_c_re[0]).astype(BF16), _bdiag_out(ssm_c_im[0]).astype(BF16)

    h0 = lax.dynamic_update_slice(jnp.pad(xs, ((FRONT, 0), (0, 0))), meta_full, (FRONT - NMETA, 0))
    tgt = jnp.pad(loss_target[0], ((FRONT, 0), (0, 0)))
    small_wmv = [_pack([args[p + n] for n in small])[None] for p in ("", "m_", "v_")]
    early_work = [h0, tgt, bre_bd, bim_bd, cre_bd, cim_bd] + [shards[n] for n in grp_b + grp_c] + small_wmv
    sh_a, land_a = _chips_wait(ga_send, ga_recv, sh_a, land_a, early_work, True, "gather_ffn1_wait")
    gw = dict(zip(grp_a, _pass_halves(land_a, "pass_ffn1", sh_a)))
    (h1, a1, b1), got = _ffn_fwd(h0, ffn1_norm, gw["ffn1_w1"], gw["ffn1_w3"], gw["ffn1_w2"], "ffn1_fwd",
                                 _gather_half_behind([shard(n) for n in grp_b]))
    w_in_f = _pass_halves(got[:1], "pass_w_in")[0]
    (vg, uf, gate), got1 = _mix_in_fwd(h1, mix_norm, w_in_f, b_gate, "mix_in_fwd",
                                       _join(_gather_half_behind([shard("ffn2_w1")]),
                                             _pass_halves_behind(list(got[1:]))))
    gw.update(zip(grp_b[1:], got1[1:]))
    wp_f, wv_f, wg_f = cols(gw["conv_proj"]), cols(gw["ssm_w_v"]), cols(gw["ssm_w_g"])
    wo_f = gw["w_out"].reshape(D, D)
    (z1,), got3 = _conv_fwd(vg, dw_pad, conv_dw_b, "conv_fwd", _gather_half_behind([shard("ffn2_w3")]))
    (yss,), got2 = _ssm_fwd(uf, bre_bd, bim_bd, cre_bd, cim_bd, lamp, ssm_d, "ssm_fwd",
                            _gather_half_behind([shard("ffn2_w2")]))
    (h2,), got_c = _merge_fwd(h1, z1, yss, gate, conv_ln_g, conv_ln_b, wp_f, wv_f, wg_f, wo_f, "merge_fwd",
                              _pass_halves_behind([got1[0], got3[0], got2[0]]))
    gw.update(zip(grp_c, got_c))
    dh3, a2, b2, loss_part, d_final = _ffn_fwd_loss(
        h2, ffn2_norm, gw["ffn2_w1"], gw["ffn2_w3"], gw["ffn2_w2"], final_norm.reshape(1, D), tgt, "ffn2_fwd_loss")

    gbig = {}
    core = lax.axis_index("c").astype(jnp.int32).reshape(1)

    def pair_sums(group, tag):
        gl = [gbig[n] for n in group]
        sib = _pair_exchange(gl, "pair_exchange_" + tag)
        out = [None] * len(group)
        for idx in _by_shape(gl):
            res = _add_pair([gl[i] for i in idx], [sib[i] for i in idx], core, "pair_" + group[idx[0]])
            for i, r in zip(idx, res):
                out[i] = r
        return out

    (dh2, da2, db2, s2, n2, d_ffn2_norm), _ = _ffn_bwd(
        h2, ffn2_norm, dh3, a2, b2, gw["ffn2_w1"], gw["ffn2_w3"], gw["ffn2_w2"], "ffn2_bwd")
    gbig["ffn2_w1"] = _wgrad(da2, n2, "ffn2_dw1")
    gbig["ffn2_w3"] = _wgrad(db2, n2, "ffn2_dw3")
    gbig["ffn2_w2"] = _wgrad(s2, dh3, "ffn2_dw2", 0.5)
    pair_c = pair_sums(grp_c, "ffn2")
    (m_b, dgate, dyc, z3, dz1, yg, dsv, dsg, dyss, d_b_gate, d_ln_g, d_ln_b) = _merge_bwd(
        dh2, z1, yss, gate, conv_ln_g, conv_ln_b, wp_f, wv_f, wg_f, wo_f, "merge_bwd")
    gbig["w_out"] = _wgrad(m_b, dh2, "dw_out").reshape(NSH, D // NSH, D)

    def shard_cols(gm):
        return gm.reshape(gm.shape[0], NSH, -1).transpose(1, 0, 2)

    gbig["conv_proj"] = shard_cols(_wgrad(z3, dyc, "dw_proj"))
    gbig["ssm_w_v"] = shard_cols(_wgrad(yg, dsv, "dw_v"))
    gbig["ssm_w_g"] = shard_cols(_wgrad(yg, dsg, "dw_g"))
    dv, dgl, ddw, d_dw_b = _conv_bwd(dz1, vg, dw_pad, "conv_bwd")
    (duf, dbre, dbim, dcre, dcim, dlam, d_ssm_d), recv_c = _ssm_bwd(
        uf, dyss, bre_bd, bim_bd, cre_bd, cim_bd, lamp, ssm_d, "ssm_bwd", _scatter_chips_behind(pair_c))
    dh1, u_b, dproj, d_mix_norm = _mix_in_bwd(h1, mix_norm, dh2, dv, dgl, duf, dgate, w_in_f, "mix_in_bwd")
    gbig["w_in"] = _wgrad(u_b, dproj, "dw_in")
    pair_b = pair_sums(grp_b, "mix")

    d_bbr = _diag_blocks(dbre.reshape(NQ, 8, H, 8, P)).transpose(0, 2, 1)
    d_bbi = _diag_blocks(dbim.reshape(NQ, 8, H, 8, P)).transpose(0, 2, 1)
    d_c_re = _diag_blocks(dcre.reshape(NQ, 8, P, 8, H)).transpose(0, 2, 1)
    d_c_im = _diag_blocks(dcim.reshape(NQ, 8, P, 8, H)).transpose(0, 2, 1)
    d_lbr = dlam[:, 0, :].reshape(G, P)
    d_lbi = dlam[:, 1, :].reshape(G, P)
    d_lam_re, d_lam_im, d_log_dt, d_b_re, d_b_im = disc_vjp((d_lbr, d_lbi, d_bbr, d_bbi))

    sg = {"mix_norm": d_mix_norm, "b_gate": d_b_gate, "conv_dw": ddw[:KW], "conv_dw_b": d_dw_b,
          "conv_ln_g": d_ln_g, "conv_ln_b": d_ln_b, "ssm_lam_re": d_lam_re, "ssm_lam_im": d_lam_im,
          "ssm_log_dt": d_log_dt, "ssm_b_re": d_b_re, "ssm_b_im": d_b_im, "ssm_c_re": d_c_re, "ssm_c_im": d_c_im,
          "ssm_d": d_ssm_d, "ffn2_norm": d_ffn2_norm, "final_norm": d_final}
    late = ["meta_tokens", "ffn1_norm"]
    early = [n for n in small if n not in late]

    (dh0, da1, db1, s1, n1, d_ffn1_norm), got = _ffn_bwd(
        h0, ffn1_norm, dh1, a1, b1, gw["ffn1_w1"], gw["ffn1_w3"], gw["ffn1_w2"], "ffn1_bwd",
        _join(_scatter_chips_behind(pair_b), _gather_all_behind(_pack([sg[n] for n in early]))))
    recv_b, early_all = got[:len(grp_b)], got[len(grp_b)]
    gbig["ffn1_w1"] = _wgrad(da1, n1, "ffn1_dw1")
    gbig["ffn1_w3"] = _wgrad(db1, n1, "ffn1_dw3")
    gbig["ffn1_w2"] = _wgrad(s1, dh1, "ffn1_dw2", 0.5)
    grad_x = dh0[FRONT:][None]
    sg["meta_tokens"] = dh0[FRONT - NMETA:FRONT]
    sg["ffn1_norm"] = d_ffn1_norm

    pair_a = pair_sums(grp_a, "ffn1")
    late_all = _gather_all(_pack([sg[n] for n in late]), "gather_late_grads")
    sa_send, sa_recv, pair_a, land_s, sa_token = _chips_start(
        pair_a, [jax.ShapeDtypeStruct(p.shape, p.dtype) for p in pair_a], False, "scatter_ffn1_start", [late_all])

    out_g, out_d, out_m, out_v = {}, {}, {}, {}

    def finish(group, recvs, tag, after=None):
        halves = [None] * len(group)
        for idx in _by_shape(recvs):
            res = _sum_slots([recvs[i] for i in idx], "sum_" + group[idx[0]], after)
            for i, r in zip(idx, res):
                halves[i] = r
        fours = [(view(args[n], n), f.reshape(1, f.shape[0] * f.shape[1], f.shape[2]), view(args["m_" + n], n),
                  view(args["v_" + n], n)) for n, f in zip(group, _swap_halves(halves, "swap_" + tag))]
        for idx in _by_shape([four[0] for four in fours]):
            for i, (g3, d3, m3, v3) in zip(idx, _adamw([fours[i] for i in idx], "adamw_" + group[idx[0]])):
                n = group[i]
                out_g[n], out_d[n], out_m[n], out_v[n] = (view(t, n) for t in (g3, d3, m3, v3))
                done.append(d3)

    done = []
    finish(grp_b + grp_c, list(recv_b) + list(recv_c), "mix_ffn2", sa_token)

    sgr = dict(zip(early, _unpack(_sum_slots([early_all], "sum_early", sa_token)[0], [sg[n].shape for n in early])))
    sgr.update(zip(late, _unpack(_sum_slots([late_all], "sum_late")[0], [sg[n].shape for n in late])))
    sgr["meta_tokens"] = lax.dynamic_slice_in_dim(sgr["meta_tokens"], chip * (D // NSH), D // NSH, axis=1)
    sgr["conv_dw"] = lax.dynamic_slice_in_dim(sgr["conv_dw"], chip * (DC // NSH), DC // NSH, axis=1)
    pshapes = [args[n].shape for n in small]
    _, d_s, m_s, v_s = _adamw([(small_wmv[0], _pack([sgr[n] for n in small])[None], small_wmv[1], small_wmv[2])],
                              "adamw_small")[0]
    for n, g_, d_, m_, v_ in zip(small, [sgr[n] for n in small], _unpack(d_s[0], pshapes),
                                 _unpack(m_s[0], pshapes), _unpack(v_s[0], pshapes)):
        out_g[n], out_d[n], out_m[n], out_v[n] = g_.reshape(args[n].shape), d_, m_, v_

    loss = lax.psum(loss_part[0, 0], ("x", "y", "c"))
    pair_a, recv_a = _chips_wait(sa_send, sa_recv, pair_a, land_s, [d_s, grad_x] + done, False,
                                 "scatter_ffn1_wait")
    finish(grp_a, _fill_own(pair_a, recv_a, "own_ffn1"), "ffn1")
    return (loss, grad_x, *[out_g[n] for n in names], *[out_d[n] for n in names],
            *[out_m[n] for n in names], *[out_v[n] for n in names])
```

```python
import math

import jax
import jax.numpy as jnp
from jax import lax
from jax.experimental import pallas as pl
from jax.experimental.pallas import tpu as pltpu

F32 = jnp.float32
BF16 = jnp.bfloat16

D = 1024
NSH = 4
F = 2816
FS = F // NSH
DC = 512
DS = 512
DIN = 2 * DC + DS + 2 * D
WS = DIN // NSH
KW = 31
KWP = 32
CONV_ROWS = 64
NMETA = 16
FRONT = 128
G, P, H = 32, 64, 16
NST = G * P
NQ = 4
QS = NST // NQ
QU = DS // NQ
NSEG = 32
NGRP = NSEG // 8
NCH = 8
SOFF = 8
EPS = 1e-6
LR, B1, B2, AEPS, WD, STEP = 1e-3, 0.9, 0.999, 1e-8, 0.01, 10
VMEM_LIMIT = 58 * 1024 * 1024
MESH = pl.DeviceIdType.MESH
ANY = pl.BlockSpec(memory_space=pl.ANY)


def _params(*sem):
    return pltpu.CompilerParams(dimension_semantics=sem, vmem_limit_bytes=VMEM_LIMIT)


def _res(shape):
    nd = len(shape)
    return pl.BlockSpec(shape, lambda *_: (0,) * nd, pipeline_mode=pl.Buffered(1))


def _tile(n, cap, mult=16):
    best = None
    for t in range(mult, min(n, cap) + 1, mult):
        if n % t == 0:
            best = t
    assert best is not None, (n, cap, mult)
    return best


def _dot(a, b):
    return jnp.dot(a, b, preferred_element_type=F32)


def _dot_nt(a, b):
    return lax.dot_general(a, b, (((1,), (1,)), ((), ())), preferred_element_type=F32)


def _dot_tn(a, b):
    return lax.dot_general(a, b, (((0,), (0,)), ((), ())), preferred_element_type=F32)


def _sigmoid(x):
    return 1.0 / (1.0 + jnp.exp(-x))


_GC = math.sqrt(2.0 / math.pi)
_GA = 0.044715


def _gelu(x):
    return 0.5 * x * (1.0 + jnp.tanh(_GC * (x + _GA * x * x * x)))


def _gelu_grad(x):
    t = jnp.tanh(_GC * (x + _GA * x * x * x))
    return 0.5 * (1.0 + t) + 0.5 * x * (1.0 - t * t) * _GC * (1.0 + 3.0 * _GA * x * x)


def _rms(hv, g):
    r = lax.rsqrt(jnp.mean(hv * hv, axis=-1, keepdims=True) + EPS)
    return hv * r * g, r


def _rms_bwd(dn, hv, r, g):
    xh = hv * r
    dxh = dn * g
    return r * (dxh - xh * jnp.mean(dxh * xh, axis=-1, keepdims=True)), xh


def _acc_rows(ref, part, first):
    @pl.when(first)
    def _():
        ref[...] = part

    @pl.when(jnp.logical_not(first))
    def _():
        ref[...] += part


def _coords():
    return lax.axis_index("x"), lax.axis_index("y"), lax.axis_index("c")


def _flip(v, d):
    return 1 - v if d else v


def _run(local, remote):
    for cp in local + remote:
        cp.start()
    for cp in remote:
        cp.wait()
    for cp in local:
        cp.wait()


def _via_vmem(src, dst, stage, sems, i):
    return (pltpu.make_async_copy(src, stage, sems.at[2 * i]), pltpu.make_async_copy(stage, dst, sems.at[2 * i + 1]))


def _run_staged(staged, remote):
    for load, _ in staged:
        load.start()
    for cp in remote:
        cp.start()
    for load, store in staged:
        load.wait()
        store.start()
    for cp in remote:
        cp.wait()
    for _, store in staged:
        store.wait()


_REL3 = ((1, 0), (0, 1), (1, 1))


class _Behind:
    def __init__(self, arrays, out_shapes, scratch, build, alias_pairs=()):
        self.arrays, self.out_shapes, self.scratch, self.build = list(arrays), list(out_shapes), list(scratch), build
        self.alias_pairs = list(alias_pairs)

    def aliases(self):
        return self.alias_pairs

    def start(self, ins, outs, scr):
        staged, remote = self.build(ins, outs, scr)
        for load, _ in staged:
            load.start()
        for cp in remote:
            cp.start()

    def finish(self, ins, outs, scr):
        staged, remote = self.build(ins, outs, scr)
        for load, store in staged:
            load.wait()
            store.start()
        for cp in remote:
            cp.wait()
        for _, store in staged:
            store.wait()


def _call(body, comm, *, name, grid, in_specs, out_specs, out_shape, scratch_shapes=(), params):
    in_specs, out_specs, out_shape = list(in_specs), list(out_specs), list(out_shape)
    scratch_shapes = list(scratch_shapes)
    if comm is None:
        f = pl.pallas_call(body, name=name, grid=grid, in_specs=in_specs, out_specs=out_specs,
                           out_shape=out_shape, scratch_shapes=scratch_shapes, compiler_params=params)
        return lambda *args: (f(*args), [])
    ni, no, ns = len(in_specs), len(out_specs), len(scratch_shapes)
    ci, co = len(comm.arrays), len(comm.out_shapes)

    def hosted(*refs):
        ins, cin = refs[:ni], refs[ni:ni + ci]
        outs, cout = refs[ni + ci:ni + ci + no], refs[ni + ci + no:ni + ci + no + co]
        scr, cscr = refs[ni + ci + no + co:ni + ci + no + co + ns], refs[ni + ci + no + co + ns:]
        first = last = None
        for axis, size in enumerate(grid):
            i = pl.program_id(axis)
            first = (i == 0) if first is None else jnp.logical_and(first, i == 0)
            last = (i == size - 1) if last is None else jnp.logical_and(last, i == size - 1)

        @pl.when(first)
        def _():
            comm.start(cin, cout, cscr)

        body(*ins, *outs, *scr)

        @pl.when(last)
        def _():
            comm.finish(cin, cout, cscr)

    f = pl.pallas_call(hosted, name=name, grid=grid, in_specs=in_specs + [ANY] * ci,
                       out_specs=out_specs + [ANY] * co, out_shape=out_shape + comm.out_shapes,
                       scratch_shapes=scratch_shapes + comm.scratch,
                       input_output_aliases={ni + a: no + b for a, b in comm.aliases()},
                       compiler_params=_params(*(("arbitrary",) * len(grid))))

    def run(*args):
        res = f(*args, *comm.arrays)
        return res[:no], res[no:]

    return run


def _gather_half_behind(shards):
    n = len(shards)

    def build(ins, outs, scr):
        send, recv, loc = scr[:3]
        stage = scr[3:]
        x, y, c = _coords()
        me = 2 * x + y
        staged = [_via_vmem(ins[t], outs[t].at[me], stage[t], loc, t) for t in range(n)]
        remote = []
        for t in range(n):
            half = shards[t].shape[0] // 2
            mine = pl.ds(c * half, half)
            for k, (dx, dy) in enumerate(_REL3):
                remote.append(pltpu.make_async_remote_copy(
                    src_ref=ins[t].at[mine], dst_ref=outs[t].at[me, mine],
                    send_sem=send.at[3 * t + k], recv_sem=recv.at[3 * t + k],
                    device_id=(_flip(x, dx), _flip(y, dy), c), device_id_type=MESH))
        return staged, remote

    return _Behind(shards, [jax.ShapeDtypeStruct((NSH,) + s.shape, s.dtype) for s in shards],
                   [pltpu.SemaphoreType.DMA((3 * n,)), pltpu.SemaphoreType.DMA((3 * n,)),
                    pltpu.SemaphoreType.DMA((2 * n,))] + [pltpu.VMEM(s.shape, s.dtype) for s in shards], build)


def _pass_halves(gathered, name, own=()):
    n, m = len(gathered), len(own)

    def body(*refs):
        shards, outs = refs[n:n + m], refs[n + m:2 * n + m]
        send, recv, loc = refs[2 * n + m:2 * n + m + 3]
        stage = refs[2 * n + m + 3:]
        x, y, c = _coords()
        staged = [_via_vmem(shards[t], outs[t].at[2 * x + y], stage[t], loc, t) for t in range(m)]
        remote = []
        for t in range(n):
            half = gathered[t].shape[1] // 2
            mine = pl.ds(c * half, half)
            for k, (dx, dy) in enumerate(_REL3):
                slot = 2 * _flip(x, dx) + _flip(y, dy)
                remote.append(pltpu.make_async_remote_copy(
                    src_ref=outs[t].at[slot, mine], dst_ref=outs[t].at[slot, mine],
                    send_sem=send.at[3 * t + k], recv_sem=recv.at[3 * t + k],
                    device_id=(x, y, 1 - c), device_id_type=MESH))
        _run_staged(staged, remote)

    return pl.pallas_call(
        body, name=name,
        out_shape=[jax.ShapeDtypeStruct(g.shape, g.dtype) for g in gathered],
        in_specs=[ANY] * (n + m), out_specs=[ANY] * n, input_output_aliases={t: t for t in range(n)},
        scratch_shapes=[pltpu.SemaphoreType.DMA((3 * n,)), pltpu.SemaphoreType.DMA((3 * n,)),
                        pltpu.SemaphoreType.DMA((max(2 * m, 1),))] + [pltpu.VMEM(s.shape, s.dtype) for s in own],
        compiler_params=pltpu.CompilerParams(vmem_limit_bytes=VMEM_LIMIT),
    )(*gathered, *own)


def _fill_own(sums, recvs, name):
    n = len(sums)

    def body(*refs):
        ins, outs = refs[:n], refs[2 * n:3 * n]
        loc = refs[3 * n]
        stage = refs[3 * n + 1:]
        x, y, _ = _coords()
        me = 2 * x + y
        _run_staged([_via_vmem(ins[t].at[me], outs[t].at[me], stage[t], loc, t) for t in range(n)], [])

    return pl.pallas_call(
        body, name=name,
        out_shape=[jax.ShapeDtypeStruct(r.shape, r.dtype) for r in recvs],
        in_specs=[ANY] * (2 * n), out_specs=[ANY] * n, input_output_aliases={n + t: t for t in range(n)},
        scratch_shapes=[pltpu.SemaphoreType.DMA((2 * n,))] + [pltpu.VMEM(s.shape[1:], s.dtype) for s in sums],
        compiler_params=pltpu.CompilerParams(vmem_limit_bytes=VMEM_LIMIT),
    )(*sums, *recvs)


def _scatter_chips_behind(sums):
    n = len(sums)

    def build(ins, outs, scr):
        send, recv, loc = scr[:3]
        stage = scr[3:]
        x, y, c = _coords()
        me = 2 * x + y
        staged = [_via_vmem(ins[t].at[me], outs[t].at[me], stage[t], loc, t) for t in range(n)]
        remote = []
        for t in range(n):
            for k, (dx, dy) in enumerate(_REL3):
                px, py = _flip(x, dx), _flip(y, dy)
                remote.append(pltpu.make_async_remote_copy(
                    src_ref=ins[t].at[2 * px + py], dst_ref=outs[t].at[me],
                    send_sem=send.at[3 * t + k], recv_sem=recv.at[3 * t + k],
                    device_id=(px, py, c), device_id_type=MESH))
        return staged, remote

    return _Behind(sums, [jax.ShapeDtypeStruct(s.shape, s.dtype) for s in sums],
                   [pltpu.SemaphoreType.DMA((3 * n,)), pltpu.SemaphoreType.DMA((3 * n,)),
                    pltpu.SemaphoreType.DMA((2 * n,))] + [pltpu.VMEM(s.shape[1:], s.dtype) for s in sums], build)


def _gather_all_behind(a):
    def build(ins, outs, scr):
        send, recv, loc, stage = scr
        x, y, c = _coords()
        me = 4 * x + 2 * y + c
        staged = [_via_vmem(ins[0], outs[0].at[me], stage, loc, 0)]
        remote = [pltpu.make_async_remote_copy(
            src_ref=ins[0], dst_ref=outs[0].at[me], send_sem=send.at[k], recv_sem=recv.at[k],
            device_id=(_flip(x, dx), _flip(y, dy), _flip(c, dc)), device_id_type=MESH)
            for k, (dx, dy, dc) in enumerate(_REL7)]
        return staged, remote

    return _Behind([a], [jax.ShapeDtypeStruct((8,) + a.shape, a.dtype)],
                   [pltpu.SemaphoreType.DMA((7,)), pltpu.SemaphoreType.DMA((7,)), pltpu.SemaphoreType.DMA((2,)),
                    pltpu.VMEM(a.shape, a.dtype)], build)


HBM = pl.BlockSpec(memory_space=pltpu.HBM)
SEM = pl.BlockSpec(memory_space=pltpu.SEMAPHORE)
EFFECT = pltpu.SideEffectType.DATAFLOW_SIDE_EFFECTING


def _chip_copies(srcs, lands, send, recv, gather):
    x, y, c = _coords()
    me = 2 * x + y
    cps = []
    for t in range(len(srcs)):
        for k, (dx, dy) in enumerate(_REL3):
            px, py = _flip(x, dx), _flip(y, dy)
            if gather:
                half = srcs[t].shape[0] // 2
                mine = pl.ds(c * half, half)
                src, dst = srcs[t].at[mine], lands[t].at[me, mine]
            else:
                src, dst = srcs[t].at[2 * px + py], lands[t].at[me]
            cps.append(pltpu.make_async_remote_copy(
                src_ref=src, dst_ref=dst, send_sem=send.at[3 * t + k], recv_sem=recv.at[3 * t + k],
                device_id=(px, py, c), device_id_type=MESH))
    return cps


def _chips_start(arrays, land_shapes, gather, name, after=()):
    n = len(arrays)

    def body(*refs):
        srcs, lands = refs[:n], refs[n:2 * n]
        send, recv = refs[2 * n + len(after)], refs[2 * n + len(after) + 1]
        token = refs[-1]
        for cp in _chip_copies(srcs, lands, send, recv, gather):
            cp.start()
        token[...] = jnp.zeros_like(token)

    lands = [lax.empty(s.shape, s.dtype) for s in land_shapes]
    thru = [pltpu.HBM(a.shape, a.dtype) for a in arrays] + [pltpu.HBM(s.shape, s.dtype) for s in land_shapes]
    res = pl.pallas_call(
        body, name=name,
        out_shape=(pltpu.SemaphoreType.DMA((3 * n,)), pltpu.SemaphoreType.DMA((3 * n,)), *thru,
                   jax.ShapeDtypeStruct((8, 128), F32)),
        in_specs=[HBM] * (2 * n) + [ANY] * len(after),
        out_specs=(SEM, SEM, *([HBM] * (2 * n)), pl.BlockSpec(memory_space=pltpu.VMEM)),
        input_output_aliases={t: 2 + t for t in range(2 * n)},
        compiler_params=pltpu.CompilerParams(has_side_effects=EFFECT),
    )(*[pltpu.with_memory_space_constraint(a, pltpu.HBM) for a in arrays],
      *[pltpu.with_memory_space_constraint(z, pltpu.HBM) for z in lands], *after)
    return res[0], res[1], list(res[2:2 + n]), list(res[2 + n:2 + 2 * n]), res[-1]


def _chips_wait(send, recv, arrays, lands, after, gather, name):
    n = len(arrays)

    def body(*refs):
        srcs, ls = refs[:n], refs[n:2 * n]
        sd, rv = refs[2 * n], refs[2 * n + 1]
        for cp in _chip_copies(srcs, ls, sd, rv, gather):
            cp.wait_send()
            cp.wait_recv()

    res = pl.pallas_call(
        body, name=name,
        out_shape=[pltpu.HBM(a.shape, a.dtype) for a in arrays] + [pltpu.HBM(z.shape, z.dtype) for z in lands],
        in_specs=[HBM] * (2 * n) + [SEM, SEM] + [ANY] * len(after), out_specs=[HBM] * (2 * n),
        input_output_aliases={t: t for t in range(2 * n)},
        compiler_params=pltpu.CompilerParams(has_side_effects=EFFECT),
    )(*arrays, *lands, send, recv, *after)
    return list(res[:n]), list(res[n:])


def _join(*parts):
    def cut(seq, key):
        res, o = [], 0
        for p in parts:
            k = len(getattr(p, key))
            res.append(seq[o:o + k])
            o += k
        return res

    def build(ins, outs, scr):
        staged, remote = [], []
        for p, i, o, s in zip(parts, cut(ins, "arrays"), cut(outs, "out_shapes"), cut(scr, "scratch")):
            st, rm = p.build(i, o, s)
            staged += st
            remote += rm
        return staged, remote

    pairs, ai, oi = [], 0, 0
    for p in parts:
        pairs += [(ai + a, oi + b) for a, b in p.alias_pairs]
        ai, oi = ai + len(p.arrays), oi + len(p.out_shapes)
    return _Behind(sum((p.arrays for p in parts), []), sum((p.out_shapes for p in parts), []),
                   sum((p.scratch for p in parts), []), build, pairs)


def _pass_halves_behind(gathered):
    n = len(gathered)

    def build(ins, outs, scr):
        send, recv = scr
        x, y, c = _coords()
        remote = []
        for t in range(n):
            half = gathered[t].shape[1] // 2
            mine = pl.ds(c * half, half)
            for k, (dx, dy) in enumerate(_REL3):
                slot = 2 * _flip(x, dx) + _flip(y, dy)
                remote.append(pltpu.make_async_remote_copy(
                    src_ref=outs[t].at[slot, mine], dst_ref=outs[t].at[slot, mine],
                    send_sem=send.at[3 * t + k], recv_sem=recv.at[3 * t + k],
                    device_id=(x, y, 1 - c), device_id_type=MESH))
        return [], remote

    return _Behind(gathered, [jax.ShapeDtypeStruct(g.shape, g.dtype) for g in gathered],
                   [pltpu.SemaphoreType.DMA((3 * n,)), pltpu.SemaphoreType.DMA((3 * n,))], build,
                   [(t, t) for t in range(n)])


_REL7 = tuple((dx, dy, dc) for dx in (0, 1) for dy in (0, 1) for dc in (0, 1))[1:]


def _gather_all(a, name):
    def body(a_ref, o_ref, send, recv, loc):
        x, y, c = _coords()
        me = 4 * x + 2 * y + c
        local = [pltpu.make_async_copy(a_ref, o_ref.at[me], loc.at[0])]
        remote = [pltpu.make_async_remote_copy(
            src_ref=a_ref, dst_ref=o_ref.at[me], send_sem=send.at[k], recv_sem=recv.at[k],
            device_id=(_flip(x, dx), _flip(y, dy), _flip(c, dc)), device_id_type=MESH)
            for k, (dx, dy, dc) in enumerate(_REL7)]
        _run(local, remote)

    return pl.pallas_call(
        body, name=name,
        out_shape=jax.ShapeDtypeStruct((8,) + a.shape, a.dtype),
        in_specs=[ANY], out_specs=ANY,
        scratch_shapes=[pltpu.SemaphoreType.DMA((7,)), pltpu.SemaphoreType.DMA((7,)),
                        pltpu.SemaphoreType.DMA((1,))],
    )(a)


def _pair_exchange(grads, name):
    n = len(grads)

    def body(*refs):
        ins, outs = refs[:n], refs[n:2 * n]
        send, recv = refs[2 * n:]
        x, y, c = _coords()
        remote = []
        for t in range(n):
            half = grads[t].shape[1] // 2
            remote.append(pltpu.make_async_remote_copy(
                src_ref=ins[t].at[:, pl.ds((1 - c) * half, half)], dst_ref=outs[t],
                send_sem=send.at[t], recv_sem=recv.at[t],
                device_id=(x, y, 1 - c), device_id_type=MESH))
        _run([], remote)

    return pl.pallas_call(
        body, name=name,
        out_shape=[jax.ShapeDtypeStruct((NSH, g.shape[1] // 2, g.shape[2]), g.dtype) for g in grads],
        in_specs=[ANY] * n, out_specs=[ANY] * n,
        scratch_shapes=[pltpu.SemaphoreType.DMA((n,)), pltpu.SemaphoreType.DMA((n,))],
    )(*grads)


def _swap_halves(halves, name):
    n = len(halves)

    def body(*refs):
        ins, outs = refs[:n], refs[n:2 * n]
        send, recv, loc = refs[2 * n:2 * n + 3]
        stage = refs[2 * n + 3:]
        x, y, c = _coords()
        local = [_via_vmem(ins[t], outs[t].at[c], stage[t], loc, t) for t in range(n)]
        remote = [pltpu.make_async_remote_copy(
            src_ref=ins[t], dst_ref=outs[t].at[c], send_sem=send.at[t], recv_sem=recv.at[t],
            device_id=(x, y, 1 - c), device_id_type=MESH) for t in range(n)]
        _run_staged(local, remote)

    return pl.pallas_call(
        body, name=name,
        out_shape=[jax.ShapeDtypeStruct((2,) + h.shape, h.dtype) for h in halves],
        in_specs=[ANY] * n, out_specs=[ANY] * n,
        scratch_shapes=[pltpu.SemaphoreType.DMA((n,)), pltpu.SemaphoreType.DMA((n,)),
                        pltpu.SemaphoreType.DMA((2 * n,))]
        + [pltpu.VMEM(h.shape, h.dtype) for h in halves],
        compiler_params=pltpu.CompilerParams(vmem_limit_bytes=VMEM_LIMIT),
    )(*halves)


def _by_shape(arrays):
    groups = {}
    for i, a in enumerate(arrays):
        groups.setdefault((a.shape, a.dtype), []).append(i)
    return list(groups.values())


def _sum_slots(rs, name, after=None):
    n = len(rs)
    K, R, C = rs[0].shape
    tr = _tile(R, max(16, (1 << 22) // (n * K * C)), 8 * (4 // rs[0].dtype.itemsize))

    def body(*refs):
        for r_ref, o_ref in zip(refs[:n], refs[len(refs) - n:]):
            acc = r_ref[0].astype(F32)
            for k in range(1, K):
                acc = acc + r_ref[k].astype(F32)
            o_ref[...] = acc

    dep = [] if after is None else [after]
    return pl.pallas_call(
        body, name=name, grid=(R // tr,),
        out_shape=[jax.ShapeDtypeStruct((R, C), F32)] * n,
        in_specs=[pl.BlockSpec((K, tr, C), lambda i: (0, i, 0))] * n + [ANY] * len(dep),
        out_specs=[pl.BlockSpec((tr, C), lambda i: (i, 0))] * n,
        compiler_params=_params("parallel"),
    )(*rs, *dep)


def _add_pair(gs, ss, core, name):
    n = len(gs)
    _, half, C = ss[0].shape
    tr = _tile(half, max(16, (1 << 21) // (n * C)))
    nb = half // tr

    def body(c_ref, *refs):
        for g_ref, s_ref, o_ref in zip(refs[:n], refs[n:2 * n], refs[2 * n:]):
            o_ref[...] = (g_ref[...].astype(F32) + s_ref[...].astype(F32)).astype(BF16)

    spec = pl.BlockSpec((1, tr, C), lambda j, i, c_ref: (j, i, 0))
    return pl.pallas_call(
        body, name=name,
        grid_spec=pltpu.PrefetchScalarGridSpec(
            num_scalar_prefetch=1, grid=(NSH, nb),
            in_specs=[pl.BlockSpec((1, tr, C), lambda j, i, c_ref: (j, c_ref[0] * nb + i, 0))] * n + [spec] * n,
            out_specs=[spec] * n),
        out_shape=[jax.ShapeDtypeStruct(ss[0].shape, BF16)] * n,
        compiler_params=_params("parallel", "parallel"),
    )(core, *gs, *ss)


def _adamw(wgmv, name):
    n = len(wgmv)
    _, R, C = wgmv[0][0].shape
    tr = _tile(R, max(8, (1 << 18) // (n * C)), 8)
    c1 = 1.0 / (1.0 - B1 ** STEP)
    c2 = 1.0 / (1.0 - B2 ** STEP)

    def body(*refs):
        for t in range(n):
            w_ref, g_ref, m_ref, v_ref = refs[4 * t:4 * t + 4]
            go_ref, d_ref, nm_ref, nv_ref = refs[4 * n + 4 * t:4 * n + 4 * t + 4]
            gv = g_ref[...]
            go_ref[...] = gv
            nm = B1 * m_ref[...] + (1.0 - B1) * gv
            nv = B2 * v_ref[...] + (1.0 - B2) * gv * gv
            nm_ref[...] = nm
            nv_ref[...] = nv
            d_ref[...] = -LR * ((nm * c1) / (jnp.sqrt(nv * c2) + AEPS) + WD * w_ref[...])

    spec = pl.BlockSpec((1, tr, C), lambda i: (0, i, 0))
    res = pl.pallas_call(
        body, name=name, grid=(R // tr,),
        out_shape=[jax.ShapeDtypeStruct((1, R, C), F32)] * (4 * n),
        in_specs=[spec] * (4 * n), out_specs=[spec] * (4 * n),
        compiler_params=_params("parallel"),
    )(*[a for four in wgmv for a in four])
    return [tuple(res[4 * t:4 * t + 4]) for t in range(n)]


def _ffn_fwd(h, g, w1, w3, w2, name, comm=None):
    L = h.shape[0]
    tm = _tile(L, 704)

    def body(h_ref, g_ref, w1_ref, w3_ref, w2_ref, o_ref, a_ref, b_ref, n_s, acc_s):
        j = pl.program_id(1)

        @pl.when(j == 0)
        def _():
            hv = h_ref[...]
            n, _ = _rms(hv, g_ref[...])
            n_s[...] = n.astype(BF16)
            acc_s[...] = hv

        n = n_s[...]
        a = _dot_nt(n, w1_ref[0])
        b = _dot_nt(n, w3_ref[0])
        a_ref[0] = a.astype(BF16)
        b_ref[0] = b.astype(BF16)
        s = (a * _sigmoid(a) * b).astype(BF16)
        acc_s[...] += 0.5 * _dot(s, w2_ref[0])

        @pl.when(j == NSH - 1)
        def _():
            o_ref[...] = acc_s[...]

    row = pl.BlockSpec((tm, D), lambda i, j: (i, 0))
    hid = pl.BlockSpec((1, tm, FS), lambda i, j: (j, i, 0))
    wsp = pl.BlockSpec((1, FS, D), lambda i, j: (j, 0, 0))
    return _call(
        body, comm, name=name, grid=(L // tm, NSH),
        out_shape=[jax.ShapeDtypeStruct((L, D), F32),
                   jax.ShapeDtypeStruct((NSH, L, FS), BF16), jax.ShapeDtypeStruct((NSH, L, FS), BF16)],
        in_specs=[row, _res((1, D)), wsp, wsp, wsp],
        out_specs=[row, hid, hid],
        scratch_shapes=[pltpu.VMEM((tm, D), BF16), pltpu.VMEM((tm, D), F32)],
        params=_params("arbitrary", "arbitrary"),
    )(h, g, w1, w3, w2)


def _loss_head(hv, gv, tv, row0):
    y, r = _rms(hv, gv)
    row = row0 + lax.broadcasted_iota(jnp.int32, (hv.shape[0], 1), 0)
    e = jnp.where(row >= FRONT, y - tv, 0.0)
    dy = e * (1.0 / D)
    part = 0.5 * jnp.sum(jnp.sum(e * dy, axis=1, keepdims=True), axis=0, keepdims=True)
    dx, xh = _rms_bwd(dy, hv, r, gv)
    return dx, part, jnp.sum(dy * xh, axis=0, keepdims=True)


def _ffn_fwd_loss(h, g, w1, w3, w2, gf, tgt, name):
    L = h.shape[0]
    tm = _tile(L, 704)

    def body(h_ref, g_ref, w1_ref, w3_ref, w2_ref, gf_ref, t_ref, o_ref, a_ref, b_ref, loss_ref, dgf_ref,
             n_s, acc_s):
        i, j = pl.program_id(0), pl.program_id(1)

        @pl.when(j == 0)
        def _():
            hv = h_ref[...]
            n, _ = _rms(hv, g_ref[...])
            n_s[...] = n.astype(BF16)
            acc_s[...] = hv

        n = n_s[...]
        a = _dot_nt(n, w1_ref[0])
        b = _dot_nt(n, w3_ref[0])
        a_ref[0] = a.astype(BF16)
        b_ref[0] = b.astype(BF16)
        s = (a * _sigmoid(a) * b).astype(BF16)
        acc_s[...] += 0.5 * _dot(s, w2_ref[0])

        @pl.when(j == NSH - 1)
        def _():
            dx, part, dgf = _loss_head(acc_s[...], gf_ref[...], t_ref[...], i * tm)
            o_ref[...] = dx
            _acc_rows(loss_ref, part, i == 0)
            _acc_rows(dgf_ref, dgf, i == 0)

    row = pl.BlockSpec((tm, D), lambda i, j: (i, 0))
    hid = pl.BlockSpec((1, tm, FS), lambda i, j: (j, i, 0))
    wsp = pl.BlockSpec((1, FS, D), lambda i, j: (j, 0, 0))
    return pl.pallas_call(
        body, name=name, grid=(L // tm, NSH),
        out_shape=[jax.ShapeDtypeStruct((L, D), F32),
                   jax.ShapeDtypeStruct((NSH, L, FS), BF16), jax.ShapeDtypeStruct((NSH, L, FS), BF16),
                   jax.ShapeDtypeStruct((1, 1), F32), jax.ShapeDtypeStruct((1, D), F32)],
        in_specs=[row, _res((1, D)), wsp, wsp, wsp, _res((1, D)), row],
        out_specs=[row, hid, hid, pl.BlockSpec((1, 1), lambda i, j: (0, 0)),
                   pl.BlockSpec((1, D), lambda i, j: (0, 0))],
        scratch_shapes=[pltpu.VMEM((tm, D), BF16), pltpu.VMEM((tm, D), F32)],
        compiler_params=_params("arbitrary", "arbitrary"),
    )(h, g, w1, w3, w2, gf, tgt)


def _ffn_bwd(h, g, dout, a, b, w1, w3, w2, name, comm=None):
    L = h.shape[0]
    tm = _tile(L, 528)

    def body(h_ref, g_ref, do_ref, a_ref, b_ref, w1_ref, w3_ref, w2_ref,
             dh_ref, da_ref, db_ref, s_ref, n_ref, dg_ref, dob_s, dn_s):
        i, j = pl.program_id(0), pl.program_id(1)

        @pl.when(j == 0)
        def _():
            n, _ = _rms(h_ref[...], g_ref[...])
            n_ref[...] = n.astype(BF16)
            dob_s[...] = (0.5 * do_ref[...]).astype(BF16)
            dn_s[...] = jnp.zeros_like(dn_s)

        av = a_ref[0].astype(F32)
        bv = b_ref[0].astype(F32)
        sig = _sigmoid(av)
        sa = av * sig
        ds = _dot_nt(dob_s[...], w2_ref[0])
        s_ref[0] = (sa * bv).astype(BF16)
        da = (ds * bv * (sig + sa * (1.0 - sig))).astype(BF16)
        db = (ds * sa).astype(BF16)
        da_ref[0] = da
        db_ref[0] = db
        dn_s[...] += _dot(da, w1_ref[0]) + _dot(db, w3_ref[0])

        @pl.when(j == NSH - 1)
        def _():
            hv = h_ref[...]
            gv = g_ref[...]
            r = lax.rsqrt(jnp.mean(hv * hv, axis=-1, keepdims=True) + EPS)
            dn = dn_s[...]
            dx, xh = _rms_bwd(dn, hv, r, gv)
            dh_ref[...] = do_ref[...] + dx
            _acc_rows(dg_ref, jnp.sum(dn * xh, axis=0, keepdims=True), i == 0)

    row = pl.BlockSpec((tm, D), lambda i, j: (i, 0))
    hid = pl.BlockSpec((1, tm, FS), lambda i, j: (j, i, 0))
    wsp = pl.BlockSpec((1, FS, D), lambda i, j: (j, 0, 0))
    return _call(
        body, comm, name=name, grid=(L // tm, NSH),
        out_shape=[jax.ShapeDtypeStruct((L, D), F32)]
        + [jax.ShapeDtypeStruct((NSH, L, FS), BF16)] * 3
        + [jax.ShapeDtypeStruct((L, D), BF16), jax.ShapeDtypeStruct((1, D), F32)],
        in_specs=[row, _res((1, D)), row, hid, hid,
                  wsp, wsp, wsp],
        out_specs=[row, hid, hid, hid, row, pl.BlockSpec((1, D), lambda i, j: (0, 0))],
        scratch_shapes=[pltpu.VMEM((tm, D), BF16), pltpu.VMEM((tm, D), F32)],
        params=_params("arbitrary", "arbitrary"),
    )(h, g, dout, a, b, w1, w3, w2)


def _wgrad(xm, ym, name, scale=1.0):
    xs, ys = xm.ndim == 3, ym.ndim == 3
    assert not (xs and ys)
    L = xm.shape[-2]
    K, N = xm.shape[-1], ym.shape[-1]
    tl = _tile(L, 2112)
    nl = L // tl
    if xs or ys:
        tn, grid_n = N, NSH
    else:
        tn = _tile(N, 1024, 128)
        grid_n = N // tn

    def body(x_ref, y_ref, o_ref, acc_s):
        l = pl.program_id(1)
        xv = x_ref[0] if xs else x_ref[...]
        yv = y_ref[0] if ys else y_ref[...]
        part = _dot_tn(xv.astype(BF16), yv.astype(BF16))
        _acc_rows(acc_s, part, l == 0)

        @pl.when(l == nl - 1)
        def _():
            res = (acc_s[...] * scale).astype(BF16)
            if xs or ys:
                o_ref[0] = res
            else:
                o_ref[...] = res

    if xs:
        x_spec = pl.BlockSpec((1, tl, K), lambda n, l: (n, l, 0))
        y_spec = pl.BlockSpec((tl, N), lambda n, l: (l, 0))
        o_spec = pl.BlockSpec((1, K, N), lambda n, l: (n, 0, 0))
        o_shape = (NSH, K, N)
    elif ys:
        x_spec = pl.BlockSpec((tl, K), lambda n, l: (l, 0))
        y_spec = pl.BlockSpec((1, tl, N), lambda n, l: (n, l, 0))
        o_spec = pl.BlockSpec((1, K, N), lambda n, l: (n, 0, 0))
        o_shape = (NSH, K, N)
    else:
        x_spec = pl.BlockSpec((tl, K), lambda n, l: (l, 0))
        y_spec = pl.BlockSpec((tl, tn), lambda n, l: (l, n))
        o_spec = pl.BlockSpec((K, tn), lambda n, l: (0, n))
        o_shape = (K, N)
    return pl.pallas_call(
        body, name=name, grid=(grid_n, nl),
        out_shape=jax.ShapeDtypeStruct(o_shape, BF16),
        in_specs=[x_spec, y_spec], out_specs=o_spec,
        scratch_shapes=[pltpu.VMEM((K, tn), F32)],
        compiler_params=_params("parallel", "arbitrary"),
    )(xm, ym)


def _mix_in_fwd(h, g, w_in, b_gate, name, comm=None):
    L = h.shape[0]
    tm = _tile(L, 528)

    def body(h_ref, g_ref, w_ref, bg_ref, vg_ref, uf_ref, gt_ref):
        u, _ = _rms(h_ref[...], g_ref[...])
        ub = u.astype(BF16)
        p = [_dot(ub, w_ref[j]) for j in range(NSH)]
        a0, a1 = 2 * DC - WS, 2 * DC + DS - WS
        vg_ref[:, 0:WS] = p[0].astype(BF16)
        vg_ref[:, WS:2 * DC] = p[1][:, 0:a0].astype(BF16)
        uf_ref[...] = p[1][:, a0:a1].astype(BF16)
        gin = jnp.concatenate([p[1][:, a1:], p[2], p[3]], axis=1)
        gt_ref[...] = _sigmoid(gin + bg_ref[...]).astype(BF16)

    def row(n):
        return pl.BlockSpec((tm, n), lambda i: (i, 0))

    return _call(
        body, comm, name=name, grid=(L // tm,),
        out_shape=[jax.ShapeDtypeStruct((L, 2 * DC), BF16), jax.ShapeDtypeStruct((L, DS), BF16),
                   jax.ShapeDtypeStruct((L, 2 * D), BF16)],
        in_specs=[row(D), _res((1, D)), _res((NSH, D, WS)), _res((1, 2 * D))],
        out_specs=[row(2 * DC), row(DS), row(2 * D)],
        params=_params("parallel"),
    )(h, g, w_in, b_gate)


def _mix_in_bwd(h, g, dres, dv, dgl, duf, dgate, w_in, name):
    L = h.shape[0]
    tm = _tile(L, 528)

    def body(h_ref, g_ref, dr_ref, dv_ref, dgl_ref, duf_ref, dgt_ref, w_ref, dh_ref, u_ref, dp_ref, dgm_ref):
        i = pl.program_id(0)
        hv = h_ref[...]
        gv = g_ref[...]
        u, r = _rms(hv, gv)
        u_ref[...] = u.astype(BF16)
        a0, a1 = 2 * DC - WS, 2 * DC + DS - WS
        b0 = WS - a1
        dp = [jnp.concatenate([dv_ref[...], dgl_ref[:, 0:WS - DC]], axis=1),
              jnp.concatenate([dgl_ref[:, WS - DC:], duf_ref[...], dgt_ref[:, 0:b0]], axis=1),
              dgt_ref[:, b0:b0 + WS], dgt_ref[:, b0 + WS:]]
        du = jnp.zeros((tm, D), F32)
        for j in range(NSH):
            dp_ref[j] = dp[j]
            du = du + _dot_nt(dp[j], w_ref[j])
        dx, xh = _rms_bwd(du, hv, r, gv)
        dh_ref[...] = dr_ref[...] + dx
        _acc_rows(dgm_ref, jnp.sum(du * xh, axis=0, keepdims=True), i == 0)

    def row(n):
        return pl.BlockSpec((tm, n), lambda i: (i, 0))

    return pl.pallas_call(
        body, name=name, grid=(L // tm,),
        out_shape=[jax.ShapeDtypeStruct((L, D), F32), jax.ShapeDtypeStruct((L, D), BF16),
                   jax.ShapeDtypeStruct((NSH, L, WS), BF16), jax.ShapeDtypeStruct((1, D), F32)],
        in_specs=[row(D), _res((1, D)), row(D), row(DC), row(DC), row(DS), row(2 * D), _res((NSH, D, WS))],
        out_specs=[row(D), row(D), pl.BlockSpec((NSH, tm, WS), lambda i: (0, i, 0)),
                   pl.BlockSpec((1, D), lambda i: (0, 0))],
        compiler_params=_params("arbitrary"),
    )(h, g, dres, dv, dgl, duf, dgate, w_in)


def _conv_fwd(vg, dw, dwb, name, comm=None):
    L = vg.shape[0]
    nc = DC // 128

    def body(v_ref, g_ref, dw_ref, dwb_ref, z_ref, zp_s):
        zp_s[0:KWP, :] = jnp.zeros((KWP, 128), F32)
        zp_s[KWP:, :] = v_ref[...].astype(F32) * _sigmoid(g_ref[...].astype(F32))
        for r0 in range(0, L, CONV_ROWS):
            acc = jnp.broadcast_to(dwb_ref[...], (CONV_ROWS, 128))
            for k in range(KW):
                acc = acc + dw_ref[k:k + 1, :] * zp_s[pl.ds(r0 + k + 2, CONV_ROWS), :]
            z_ref[pl.ds(r0, CONV_ROWS), :] = acc

    return _call(
        body, comm, name=name, grid=(nc,),
        out_shape=[jax.ShapeDtypeStruct((L, DC), F32)],
        in_specs=[pl.BlockSpec((L, 128), lambda c: (0, c)), pl.BlockSpec((L, 128), lambda c: (0, nc + c)),
                  pl.BlockSpec((KWP, 128), lambda c: (0, c)), pl.BlockSpec((1, 128), lambda c: (0, c))],
        out_specs=[pl.BlockSpec((L, 128), lambda c: (0, c))],
        scratch_shapes=[pltpu.VMEM((L + KWP, 128), F32)],
        params=_params("parallel"),
    )(vg, vg, dw, dwb)


def _conv_bwd(dz1, vg, dw, name):
    L = vg.shape[0]
    nc = DC // 128

    def body(dz_ref, v_ref, g_ref, dw_ref, dv_ref, dg_ref, ddw_ref, ddwb_ref, zp_s, dzp_s):
        vv = v_ref[...].astype(F32)
        sg = _sigmoid(g_ref[...].astype(F32))
        zp_s[0:KWP, :] = jnp.zeros((KWP, 128), F32)
        zp_s[KWP:, :] = vv * sg
        dz = dz_ref[...]
        dzp_s[0:L, :] = dz
        dzp_s[L:, :] = jnp.zeros((KWP, 128), F32)
        ddwb_ref[...] = jnp.sum(dz, axis=0, keepdims=True)
        part = [jnp.zeros((8, 128), F32) for _ in range(KW)]
        for r0 in range(0, L, CONV_ROWS):
            rows = pl.ds(r0, CONV_ROWS)
            dzc = dz_ref[rows, :]
            acc = jnp.zeros((CONV_ROWS, 128), F32)
            for k in range(KW):
                acc = acc + dw_ref[k:k + 1, :] * dzp_s[pl.ds(r0 + KW - 1 - k, CONV_ROWS), :]
                prod = dzc * zp_s[pl.ds(r0 + k + 2, CONV_ROWS), :]
                for q in range(CONV_ROWS // 8):
                    part[k] = part[k] + prod[8 * q:8 * (q + 1), :]
            vc = v_ref[rows, :].astype(F32)
            sc = _sigmoid(g_ref[rows, :].astype(F32))
            dv_ref[rows, :] = (acc * sc).astype(BF16)
            dg_ref[rows, :] = (acc * vc * sc * (1.0 - sc)).astype(BF16)
        for k in range(KW):
            ddw_ref[k:k + 1, :] = jnp.sum(part[k], axis=0, keepdims=True)
        ddw_ref[KW:KWP, :] = jnp.zeros((KWP - KW, 128), F32)

    col = pl.BlockSpec((L, 128), lambda c: (0, c))
    return pl.pallas_call(
        body, name=name, grid=(nc,),
        out_shape=[jax.ShapeDtypeStruct((L, DC), BF16), jax.ShapeDtypeStruct((L, DC), BF16),
                   jax.ShapeDtypeStruct((KWP, DC), F32), jax.ShapeDtypeStruct((1, DC), F32)],
        in_specs=[col, col, pl.BlockSpec((L, 128), lambda c: (0, nc + c)),
                  pl.BlockSpec((KWP, 128), lambda c: (0, c))],
        out_specs=[col, col, pl.BlockSpec((KWP, 128), lambda c: (0, c)), pl.BlockSpec((1, 128), lambda c: (0, c))],
        scratch_shapes=[pltpu.VMEM((L + KWP, 128), F32), pltpu.VMEM((L + KWP, 128), F32)],
        compiler_params=_params("parallel"),
    )(dz1, vg, vg, dw)


NLB = QS // 128


def _lb_store(ref, rows, val):
    for cb in range(NLB):
        ref[cb, rows, :] = val[:, cb * 128:(cb + 1) * 128]


def _lb_load(ref, rows):
    return jnp.concatenate([ref[cb, rows, :] for cb in range(NLB)], axis=1)


def _scan(xr_ref, xi_ref, base, T, ar, ai, atr, ati, reverse):
    W = ar.shape[1]
    ar, ai, atr, ati = (jnp.broadcast_to(v, (8, W)) for v in (ar, ai, atr, ati))
    zero = jnp.zeros((8, W), F32)

    def rows(t, g):
        tt = T - 1 - t if reverse else t
        return pl.ds(base + g * 8 * T + tt, 8, stride=T)

    def make_step(store):
        def step(t, carry):
            out = []
            for g in range(NGRP):
                sr, si = carry[2 * g], carry[2 * g + 1]
                idx = rows(t, g)
                nr = ar * sr - ai * si + _lb_load(xr_ref, idx)
                ni = ar * si + ai * sr + _lb_load(xi_ref, idx)
                if store:
                    _lb_store(xr_ref, idx, nr)
                    _lb_store(xi_ref, idx, ni)
                out += [nr, ni]
            return tuple(out)
        return step

    ends = lax.fori_loop(0, T, make_step(False), (zero,) * (2 * NGRP))
    sub = lax.broadcasted_iota(jnp.int32, (8, W), 0)
    edge = sub == (7 if reverse else 0)
    shift, last = (7, 0) if reverse else (1, 7)
    inr, ini = jnp.zeros((1, W), F32), jnp.zeros((1, W), F32)
    starts = [None] * (2 * NGRP)
    for g in (reversed(range(NGRP)) if reverse else range(NGRP)):
        er, ei = ends[2 * g], ends[2 * g + 1]
        cr, ci = jnp.where(edge, inr, 0.0), jnp.where(edge, ini, 0.0)
        for _ in range(7):
            nr = atr * cr - ati * ci + er
            ni = atr * ci + ati * cr + ei
            cr = jnp.where(edge, inr, pltpu.roll(nr, shift, 0))
            ci = jnp.where(edge, ini, pltpu.roll(ni, shift, 0))
        starts[2 * g], starts[2 * g + 1] = cr, ci
        inr = (atr * cr - ati * ci + er)[last:last + 1]
        ini = (atr * ci + ati * cr + ei)[last:last + 1]
    lax.fori_loop(0, T, make_step(True), tuple(starts))


def _ssm_fwd(uf, bre, bim, cre, cim, lamp, dsk, name, comm=None):
    L = uf.shape[0]
    T = L // NSEG
    tc = L // NCH

    def body(u_ref, bre_ref, bim_ref, cre_ref, cim_ref, lam_ref, d_ref, y_ref, sr_s, si_s):
        for k in range(NCH):
            sl = slice(k * tc, (k + 1) * tc)
            uk = u_ref[sl, :]
            _lb_store(sr_s, sl, _dot(uk, bre_ref[0]))
            _lb_store(si_s, sl, _dot(uk, bim_ref[0]))
        _scan(sr_s, si_s, 0, T, lam_ref[0:1, :], lam_ref[1:2, :], lam_ref[2:3, :], lam_ref[3:4, :], False)
        for k in range(NCH):
            sl = slice(k * tc, (k + 1) * tc)
            y_ref[sl, :] = (_dot(_lb_load(sr_s, sl).astype(BF16), cre_ref[0])
                            - _dot(_lb_load(si_s, sl).astype(BF16), cim_ref[0])
                            + d_ref[...] * u_ref[sl, :].astype(F32))

    return _call(
        body, comm, name=name, grid=(NQ,),
        out_shape=[jax.ShapeDtypeStruct((L, DS), F32)],
        in_specs=[pl.BlockSpec((L, QU), lambda q: (0, q)),
                  pl.BlockSpec((1, QU, QS), lambda q: (q, 0, 0)), pl.BlockSpec((1, QU, QS), lambda q: (q, 0, 0)),
                  pl.BlockSpec((1, QS, QU), lambda q: (q, 0, 0)), pl.BlockSpec((1, QS, QU), lambda q: (q, 0, 0)),
                  pl.BlockSpec((8, QS), lambda q: (0, q)), pl.BlockSpec((1, QU), lambda q: (0, q))],
        out_specs=[pl.BlockSpec((L, QU), lambda q: (0, q))],
        scratch_shapes=[pltpu.VMEM((NLB, L, 128), F32), pltpu.VMEM((NLB, L, 128), F32)],
        params=_params("parallel"),
    )(uf, bre, bim, cre, cim, lamp, dsk)


def _ssm_bwd(uf, dyss, bre, bim, cre, cim, lamp, dsk, name, comm=None):
    L = uf.shape[0]
    T = L // NSEG
    tc = L // NCH

    def body(u_ref, dy_ref, bre_ref, bim_ref, cre_ref, cim_ref, lam_ref, d_ref,
             du_ref, dbre_ref, dbim_ref, dcre_ref, dcim_ref, dlam_ref, dd_ref, sr_s, si_s, gr_s, gi_s):
        _lb_store(sr_s, slice(0, SOFF), jnp.zeros((SOFF, QS), F32))
        _lb_store(si_s, slice(0, SOFF), jnp.zeros((SOFF, QS), F32))
        for k in range(NCH):
            sl = slice(k * tc, (k + 1) * tc)
            ss = slice(SOFF + k * tc, SOFF + (k + 1) * tc)
            uk = u_ref[sl, :]
            dyk = dy_ref[sl, :].astype(BF16)
            _lb_store(sr_s, ss, _dot(uk, bre_ref[0]))
            _lb_store(si_s, ss, _dot(uk, bim_ref[0]))
            _lb_store(gr_s, sl, _dot_nt(dyk, cre_ref[0]))
            _lb_store(gi_s, sl, -_dot_nt(dyk, cim_ref[0]))
        ar, ai, atr, ati = lam_ref[0:1, :], lam_ref[1:2, :], lam_ref[2:3, :], lam_ref[3:4, :]
        _scan(sr_s, si_s, SOFF, T, ar, ai, atr, ati, False)
        _scan(gr_s, gi_s, 0, T, ar, -ai, atr, -ati, True)
        dbre = jnp.zeros((QU, QS), F32)
        dbim = jnp.zeros((QU, QS), F32)
        dcre = jnp.zeros((QS, QU), F32)
        dcim = jnp.zeros((QS, QU), F32)
        dd = jnp.zeros((1, QU), F32)
        qr = jnp.zeros((1, QS), F32)
        qi = jnp.zeros((1, QS), F32)
        for k in range(NCH):
            sl = slice(k * tc, (k + 1) * tc)
            ss = slice(SOFF + k * tc, SOFF + (k + 1) * tc)
            sp = slice(SOFF - 1 + k * tc, SOFF - 1 + (k + 1) * tc)
            uk = u_ref[sl, :]
            dyk = dy_ref[sl, :]
            dyb = dyk.astype(BF16)
            gr, gi = _lb_load(gr_s, sl), _lb_load(gi_s, sl)
            pr, pi = _lb_load(sr_s, sp), _lb_load(si_s, sp)
            qr = qr + jnp.sum(gr * pr + gi * pi, axis=0, keepdims=True)
            qi = qi + jnp.sum(gi * pr - gr * pi, axis=0, keepdims=True)
            grb, gib = gr.astype(BF16), gi.astype(BF16)
            du_ref[sl, :] = (_dot_nt(grb, bre_ref[0]) + _dot_nt(gib, bim_ref[0])
                             + dyk * d_ref[...]).astype(BF16)
            dbre = dbre + _dot_tn(uk, grb)
            dbim = dbim + _dot_tn(uk, gib)
            dcre = dcre + _dot_tn(_lb_load(sr_s, ss).astype(BF16), dyb)
            dcim = dcim - _dot_tn(_lb_load(si_s, ss).astype(BF16), dyb)
            dd = dd + jnp.sum(dyk * uk.astype(F32), axis=0, keepdims=True)
        dlam_ref[0] = jnp.concatenate([qr, qi, jnp.zeros((6, QS), F32)], axis=0)
        dbre_ref[0] = dbre
        dbim_ref[0] = dbim
        dcre_ref[0] = dcre
        dcim_ref[0] = dcim
        dd_ref[...] = dd

    col = pl.BlockSpec((L, QU), lambda q: (0, q))
    bsp = pl.BlockSpec((1, QU, QS), lambda q: (q, 0, 0))
    csp = pl.BlockSpec((1, QS, QU), lambda q: (q, 0, 0))
    return _call(
        body, comm, name=name, grid=(NQ,),
        out_shape=[jax.ShapeDtypeStruct((L, DS), BF16),
                   jax.ShapeDtypeStruct((NQ, QU, QS), F32), jax.ShapeDtypeStruct((NQ, QU, QS), F32),
                   jax.ShapeDtypeStruct((NQ, QS, QU), F32), jax.ShapeDtypeStruct((NQ, QS, QU), F32),
                   jax.ShapeDtypeStruct((NQ, 8, QS), F32), jax.ShapeDtypeStruct((1, DS), F32)],
        in_specs=[col, col, bsp, bsp, csp, csp,
                  pl.BlockSpec((8, QS), lambda q: (0, q)), pl.BlockSpec((1, QU), lambda q: (0, q))],
        out_specs=[col,
                   pl.BlockSpec((1, QU, QS), lambda q: (q, 0, 0)), pl.BlockSpec((1, QU, QS), lambda q: (q, 0, 0)),
                   pl.BlockSpec((1, QS, QU), lambda q: (q, 0, 0)), pl.BlockSpec((1, QS, QU), lambda q: (q, 0, 0)),
                   pl.BlockSpec((1, 8, QS), lambda q: (q, 0, 0)), pl.BlockSpec((1, QU), lambda q: (0, q))],
        scratch_shapes=[pltpu.VMEM((NLB, L + SOFF, 128), F32), pltpu.VMEM((NLB, L + SOFF, 128), F32),
                        pltpu.VMEM((NLB, L, 128), F32), pltpu.VMEM((NLB, L, 128), F32)],
        params=_params("parallel"),
    )(uf, dyss, bre, bim, cre, cim, lamp, dsk)


def _branches(z1_ref, yss_ref, gt_ref, lng_ref, lnb_ref, wp_ref, wv_ref, wg_ref):
    zf = z1_ref[...]
    mu = jnp.mean(zf, axis=-1, keepdims=True)
    zc = zf - mu
    rstd = lax.rsqrt(jnp.mean(zc * zc, axis=-1, keepdims=True) + EPS)
    zn = zc * rstd
    z2 = zn * lng_ref[...] + lnb_ref[...]
    sz = _sigmoid(z2)
    z3 = (z2 * sz).astype(BF16)
    y_conv = _dot(z3, wp_ref[...])
    yss = yss_ref[...]
    yg = _gelu(yss).astype(BF16)
    sv = _dot(yg, wv_ref[...])
    sig = _sigmoid(_dot(yg, wg_ref[...]))
    y_ssm = sv * sig
    gc = gt_ref[:, 0:D].astype(F32)
    gs = gt_ref[:, D:2 * D].astype(F32)
    m = gc * y_conv + gs * y_ssm
    return dict(rstd=rstd, zn=zn, z2=z2, sz=sz, z3=z3, y_conv=y_conv, yss=yss, yg=yg, sv=sv, sig=sig,
                y_ssm=y_ssm, gc=gc, gs=gs, m=m)


def _merge_fwd(h, z1, yss, gate, lng, lnb, wp, wv, wg, wo, name, comm=None):
    L = h.shape[0]
    tm = _tile(L, 528)

    def body(h_ref, z1_ref, yss_ref, gt_ref, lng_ref, lnb_ref, wp_ref, wv_ref, wg_ref, wo_ref, o_ref):
        f = _branches(z1_ref, yss_ref, gt_ref, lng_ref, lnb_ref, wp_ref, wv_ref, wg_ref)
        o_ref[...] = h_ref[...] + _dot(f["m"].astype(BF16), wo_ref[...])

    def row(n):
        return pl.BlockSpec((tm, n), lambda i: (i, 0))

    return _call(
        body, comm, name=name, grid=(L // tm,),
        out_shape=[jax.ShapeDtypeStruct((L, D), F32)],
        in_specs=[row(D), row(DC), row(DS), row(2 * D), _res((1, DC)), _res((1, DC)),
                  _res((DC, D)), _res((DS, D)), _res((DS, D)), _res((D, D))],
        out_specs=[row(D)],
        params=_params("parallel"),
    )(h, z1, yss, gate, lng, lnb, wp, wv, wg, wo)


def _merge_bwd(dh, z1, yss, gate, lng, lnb, wp, wv, wg, wo, name):
    L = dh.shape[0]
    tm = _tile(L, 352)

    def body(dh_ref, z1_ref, yss_ref, gt_ref, lng_ref, lnb_ref, wp_ref, wv_ref, wg_ref, wo_ref,
             m_ref, dgt_ref, dyc_ref, z3_ref, dz1_ref, yg_ref, dsv_ref, dsg_ref, dyss_ref,
             dbg_ref, dlng_ref, dlnb_ref):
        i = pl.program_id(0)
        f = _branches(z1_ref, yss_ref, gt_ref, lng_ref, lnb_ref, wp_ref, wv_ref, wg_ref)
        gc, gs, sig, sv = f["gc"], f["gs"], f["sig"], f["sv"]
        m_ref[...] = f["m"].astype(BF16)
        z3_ref[...] = f["z3"]
        yg_ref[...] = f["yg"]
        dm = _dot_nt(dh_ref[...].astype(BF16), wo_ref[...])
        dgc = (dm * f["y_conv"] * gc * (1.0 - gc)).astype(BF16)
        dgs = (dm * f["y_ssm"] * gs * (1.0 - gs)).astype(BF16)
        dgt_ref[:, 0:D] = dgc
        dgt_ref[:, D:2 * D] = dgs
        part = jnp.concatenate([jnp.sum(dgc.astype(F32), axis=0, keepdims=True),
                                jnp.sum(dgs.astype(F32), axis=0, keepdims=True)], axis=1)
        _acc_rows(dbg_ref, part, i == 0)
        dyc = (dm * gc).astype(BF16)
        dyc_ref[...] = dyc
        dys = dm * gs
        dsv = (dys * sig).astype(BF16)
        dsg = (dys * sv * sig * (1.0 - sig)).astype(BF16)
        dsv_ref[...] = dsv
        dsg_ref[...] = dsg
        dyg = _dot_nt(dsv, wv_ref[...]) + _dot_nt(dsg, wg_ref[...])
        dyss_ref[...] = dyg * _gelu_grad(f["yss"])
        dz3 = _dot_nt(dyc, wp_ref[...])
        z2, sz, zn = f["z2"], f["sz"], f["zn"]
        dz2 = dz3 * sz * (1.0 + z2 * (1.0 - sz))
        _acc_rows(dlng_ref, jnp.sum(dz2 * zn, axis=0, keepdims=True), i == 0)
        _acc_rows(dlnb_ref, jnp.sum(dz2, axis=0, keepdims=True), i == 0)
        dzn = dz2 * lng_ref[...]
        dz1_ref[...] = f["rstd"] * (dzn - jnp.mean(dzn, axis=-1, keepdims=True)
                                    - zn * jnp.mean(dzn * zn, axis=-1, keepdims=True))

    def row(n):
        return pl.BlockSpec((tm, n), lambda i: (i, 0))

    def tot(n):
        return pl.BlockSpec((1, n), lambda i: (0, 0))

    return pl.pallas_call(
        body, name=name, grid=(L // tm,),
        out_shape=[jax.ShapeDtypeStruct((L, D), BF16), jax.ShapeDtypeStruct((L, 2 * D), BF16),
                   jax.ShapeDtypeStruct((L, D), BF16), jax.ShapeDtypeStruct((L, DC), BF16),
                   jax.ShapeDtypeStruct((L, DC), F32), jax.ShapeDtypeStruct((L, DS), BF16),
                   jax.ShapeDtypeStruct((L, D), BF16), jax.ShapeDtypeStruct((L, D), BF16),
                   jax.ShapeDtypeStruct((L, DS), F32),
                   jax.ShapeDtypeStruct((1, 2 * D), F32), jax.ShapeDtypeStruct((1, DC), F32),
                   jax.ShapeDtypeStruct((1, DC), F32)],
        in_specs=[row(D), row(DC), row(DS), row(2 * D), _res((1, DC)), _res((1, DC)),
                  _res((DC, D)), _res((DS, D)), _res((DS, D)), _res((D, D))],
        out_specs=[row(D), row(2 * D), row(D), row(DC), row(DC), row(DS), row(D), row(D), row(DS),
                   tot(2 * D), tot(DC), tot(DC)],
        compiler_params=_params("arbitrary"),
    )(dh, z1, yss, gate, lng, lnb, wp, wv, wg, wo)


def _ssm_disc(lam_re, lam_im, log_dt, b_re, b_im):
    lam = lax.complex(lam_re, lam_im)
    dt = jnp.exp(log_dt)[:, None]
    lam_bar = jnp.exp(lam * dt)
    bbar = ((lam_bar - 1.0) / lam)[..., None] * lax.complex(b_re, b_im)
    return jnp.real(lam_bar), jnp.imag(lam_bar), jnp.real(bbar), jnp.imag(bbar)


def _bdiag_in(m):
    m4 = m.reshape(NQ, G // NQ, P, H)
    return jnp.einsum("qgph,gk->qghkp", m4, jnp.eye(G // NQ, dtype=m.dtype)).reshape(NQ, QU, QS)


def _bdiag_out(m):
    m4 = m.reshape(NQ, G // NQ, H, P)
    return jnp.einsum("qghp,gk->qgpkh", m4, jnp.eye(G // NQ, dtype=m.dtype)).reshape(NQ, QS, QU)


def _diag_blocks(m4):
    return jnp.einsum("qiaib->qiab", m4).reshape(G, m4.shape[2], m4.shape[4])


def _pack(parts, rows_mult=8):
    flat = jnp.concatenate([p.reshape(-1).astype(F32) for p in parts])
    n = flat.shape[0]
    tot = -(-n // (128 * rows_mult)) * (128 * rows_mult)
    return jnp.pad(flat, (0, tot - n)).reshape(tot // 128, 128)


def _unpack(buf, shapes):
    flat = buf.reshape(-1)
    out, o = [], 0
    for s in shapes:
        n = math.prod(s)
        out.append(flat[o:o + n].reshape(s))
        o += n
    return out


def kernel(x, meta_tokens, ffn1_norm, ffn1_w1, ffn1_w3, ffn1_w2, mix_norm, w_in, b_gate, conv_dw, conv_dw_b, conv_ln_g, conv_ln_b, conv_proj, ssm_lam_re, ssm_lam_im, ssm_log_dt, ssm_b_re, ssm_b_im, ssm_c_re, ssm_c_im, ssm_d, ssm_w_v, ssm_w_g, w_out, ffn2_norm, ffn2_w1, ffn2_w3, ffn2_w2, final_norm, loss_target, m_meta_tokens, m_ffn1_norm, m_ffn1_w1, m_ffn1_w3, m_ffn1_w2, m_mix_norm, m_w_in, m_b_gate, m_conv_dw, m_conv_dw_b, m_conv_ln_g, m_conv_ln_b, m_conv_proj, m_ssm_lam_re, m_ssm_lam_im, m_ssm_log_dt, m_ssm_b_re, m_ssm_b_im, m_ssm_c_re, m_ssm_c_im, m_ssm_d, m_ssm_w_v, m_ssm_w_g, m_w_out, m_ffn2_norm, m_ffn2_w1, m_ffn2_w3, m_ffn2_w2, m_final_norm, v_meta_tokens, v_ffn1_norm, v_ffn1_w1, v_ffn1_w3, v_ffn1_w2, v_mix_norm, v_w_in, v_b_gate, v_conv_dw, v_conv_dw_b, v_conv_ln_g, v_conv_ln_b, v_conv_proj, v_ssm_lam_re, v_ssm_lam_im, v_ssm_log_dt, v_ssm_b_re, v_ssm_b_im, v_ssm_c_re, v_ssm_c_im, v_ssm_d, v_ssm_w_v, v_ssm_w_g, v_w_out, v_ffn2_norm, v_ffn2_w1, v_ffn2_w3, v_ffn2_w2, v_final_norm):
    args = dict(locals())
    names = ["meta_tokens", "ffn1_norm", "ffn1_w1", "ffn1_w3", "ffn1_w2", "mix_norm", "w_in", "b_gate",
             "conv_dw", "conv_dw_b", "conv_ln_g", "conv_ln_b", "conv_proj", "ssm_lam_re", "ssm_lam_im",
             "ssm_log_dt", "ssm_b_re", "ssm_b_im", "ssm_c_re", "ssm_c_im", "ssm_d", "ssm_w_v", "ssm_w_g",
             "w_out", "ffn2_norm", "ffn2_w1", "ffn2_w3", "ffn2_w2", "final_norm"]
    big = ["ffn1_w1", "ffn1_w3", "ffn1_w2", "w_in", "conv_proj", "ssm_w_v", "ssm_w_g", "w_out",
           "ffn2_w1", "ffn2_w3", "ffn2_w2"]
    small = [n for n in names if n not in big]

    xs = x[0]
    S = xs.shape[0]
    L = FRONT + S
    T = L // NSEG
    jx, jy = lax.axis_index("x"), lax.axis_index("y")
    chip = 2 * jx + jy

    small_all = _gather_all(_pack([meta_tokens, conv_dw[0]]), "gather_small")
    sm = small_all[0::2].reshape(NSH, -1)
    nmt = NMETA * (D // NSH)
    ndw = KW * (DC // NSH)
    meta_full = sm[:, :nmt].reshape(NSH, NMETA, D // NSH).transpose(1, 0, 2).reshape(NMETA, D)
    dw_full = sm[:, nmt:nmt + ndw].reshape(NSH, KW, DC // NSH).transpose(1, 0, 2).reshape(KW, DC)
    dw_pad = jnp.pad(dw_full, ((0, KWP - KW), (0, 0)))
    tposed = ("ffn1_w1", "ffn1_w3", "ffn2_w1", "ffn2_w3")

    def view(a, n):
        return jnp.swapaxes(a, 1, 2) if n in tposed else a

    grp_a = ["ffn1_w1", "ffn1_w3", "ffn1_w2"]
    grp_b = ["w_in", "conv_proj", "ssm_w_v", "ssm_w_g", "w_out"]
    grp_c = ["ffn2_w1", "ffn2_w3", "ffn2_w2"]

    shards = {n: view(args[n], n)[0].astype(BF16) for n in big}

    def shard(n):
        return shards[n]

    sh_a = [shard(n) for n in grp_a]
    ga_send, ga_recv, sh_a, land_a, _ = _chips_start(
        sh_a, [jax.ShapeDtypeStruct((NSH,) + s.shape, s.dtype) for s in sh_a], True, "gather_ffn1_start",
        [small_all])

    def cols(w):
        return w.transpose(1, 0, 2).reshape(w.shape[1], -1)

    disc_in = (ssm_lam_re[0], ssm_lam_im[0], ssm_log_dt[0], ssm_b_re[0], ssm_b_im[0])
    (lbr, lbi, bbr, bbi), disc_vjp = jax.vjp(_ssm_disc, *disc_in)
    lam_t = jnp.exp(lax.complex(ssm_lam_re[0], ssm_lam_im[0]) * (jnp.exp(ssm_log_dt[0])[:, None] * T))
    lamp = jnp.concatenate([lbr.reshape(1, NST), lbi.reshape(1, NST), jnp.real(lam_t).reshape(1, NST),
                            jnp.imag(lam_t).reshape(1, NST), jnp.zeros((4, NST), F32)], axis=0)
    bre_bd, bim_bd = _bdiag_in(bbr).astype(BF16), _bdiag_in(bbi).astype(BF16)
    cre_bd, cim_bd = _bdiag_out(ssm_c_re[0]).astype(BF16), _bdiag_out(ssm_c_im[0]).astype(BF16)

    h0 = lax.dynamic_update_slice(jnp.pad(xs, ((FRONT, 0), (0, 0))), meta_full, (FRONT - NMETA, 0))
    tgt = jnp.pad(loss_target[0], ((FRONT, 0), (0, 0)))
    small_wmv = [_pack([args[p + n] for n in small])[None] for p in ("", "m_", "v_")]
    early_work = [h0, tgt, bre_bd, bim_bd, cre_bd, cim_bd] + [shards[n] for n in grp_b + grp_c] + small_wmv
    sh_a, land_a = _chips_wait(ga_send, ga_recv, sh_a, land_a, early_work, True, "gather_ffn1_wait")
    gw = dict(zip(grp_a, _pass_halves(land_a, "pass_ffn1", sh_a)))
    (h1, a1, b1), got = _ffn_fwd(h0, ffn1_norm, gw["ffn1_w1"], gw["ffn1_w3"], gw["ffn1_w2"], "ffn1_fwd",
                                 _gather_half_behind([shard(n) for n in grp_b]))
    w_in_f = _pass_halves(got[:1], "pass_w_in")[0]
    (vg, uf, gate), got1 = _mix_in_fwd(h1, mix_norm, w_in_f, b_gate, "mix_in_fwd",
                                       _join(_gather_half_behind([shard("ffn2_w1")]),
                                             _pass_halves_behind(list(got[1:]))))
    gw.update(zip(grp_b[1:], got1[1:]))
    wp_f, wv_f, wg_f = cols(gw["conv_proj"]), cols(gw["ssm_w_v"]), cols(gw["ssm_w_g"])
    wo_f = gw["w_out"].reshape(D, D)
    (z1,), got3 = _conv_fwd(vg, dw_pad, conv_dw_b, "conv_fwd", _gather_half_behind([shard("ffn2_w3")]))
    (yss,), got2 = _ssm_fwd(uf, bre_bd, bim_bd, cre_bd, cim_bd, lamp, ssm_d, "ssm_fwd",
                            _gather_half_behind([shard("ffn2_w2")]))
    (h2,), got_c = _merge_fwd(h1, z1, yss, gate, conv_ln_g, conv_ln_b, wp_f, wv_f, wg_f, wo_f, "merge_fwd",
                              _pass_halves_behind([got1[0], got3[0], got2[0]]))
    gw.update(zip(grp_c, got_c))
    dh3, a2, b2, loss_part, d_final = _ffn_fwd_loss(
        h2, ffn2_norm, gw["ffn2_w1"], gw["ffn2_w3"], gw["ffn2_w2"], final_norm.reshape(1, D), tgt, "ffn2_fwd_loss")

    gbig = {}
    core = lax.axis_index("c").astype(jnp.int32).reshape(1)

    def pair_sums(group, tag):
        gl = [gbig[n] for n in group]
        sib = _pair_exchange(gl, "pair_exchange_" + tag)
        out = [None] * len(group)
        for idx in _by_shape(gl):
            res = _add_pair([gl[i] for i in idx], [sib[i] for i in idx], core, "pair_" + group[idx[0]])
            for i, r in zip(idx, res):
                out[i] = r
        return out

    (dh2, da2, db2, s2, n2, d_ffn2_norm), _ = _ffn_bwd(
        h2, ffn2_norm, dh3, a2, b2, gw["ffn2_w1"], gw["ffn2_w3"], gw["ffn2_w2"], "ffn2_bwd")
    gbig["ffn2_w1"] = _wgrad(da2, n2, "ffn2_dw1")
    gbig["ffn2_w3"] = _wgrad(db2, n2, "ffn2_dw3")
    gbig["ffn2_w2"] = _wgrad(s2, dh3, "ffn2_dw2", 0.5)
    pair_c = pair_sums(grp_c, "ffn2")
    (m_b, dgate, dyc, z3, dz1, yg, dsv, dsg, dyss, d_b_gate, d_ln_g, d_ln_b) = _merge_bwd(
        dh2, z1, yss, gate, conv_ln_g, conv_ln_b, wp_f, wv_f, wg_f, wo_f, "merge_bwd")
    gbig["w_out"] = _wgrad(m_b, dh2, "dw_out").reshape(NSH, D // NSH, D)

    def shard_cols(gm):
        return gm.reshape(gm.shape[0], NSH, -1).transpose(1, 0, 2)

    gbig["conv_proj"] = shard_cols(_wgrad(z3, dyc, "dw_proj"))
    gbig["ssm_w_v"] = shard_cols(_wgrad(yg, dsv, "dw_v"))
    gbig["ssm_w_g"] = shard_cols(_wgrad(yg, dsg, "dw_g"))
    dv, dgl, ddw, d_dw_b = _conv_bwd(dz1, vg, dw_pad, "conv_bwd")
    (duf, dbre, dbim, dcre, dcim, dlam, d_ssm_d), recv_c = _ssm_bwd(
        uf, dyss, bre_bd, bim_bd, cre_bd, cim_bd, lamp, ssm_d, "ssm_bwd", _scatter_chips_behind(pair_c))
    dh1, u_b, dproj, d_mix_norm = _mix_in_bwd(h1, mix_norm, dh2, dv, dgl, duf, dgate, w_in_f, "mix_in_bwd")
    gbig["w_in"] = _wgrad(u_b, dproj, "dw_in")
    pair_b = pair_sums(grp_b, "mix")

    d_bbr = _diag_blocks(dbre.reshape(NQ, 8, H, 8, P)).transpose(0, 2, 1)
    d_bbi = _diag_blocks(dbim.reshape(NQ, 8, H, 8, P)).transpose(0, 2, 1)
    d_c_re = _diag_blocks(dcre.reshape(NQ, 8, P, 8, H)).transpose(0, 2, 1)
    d_c_im = _diag_blocks(dcim.reshape(NQ, 8, P, 8, H)).transpose(0, 2, 1)
    d_lbr = dlam[:, 0, :].reshape(G, P)
    d_lbi = dlam[:, 1, :].reshape(G, P)
    d_lam_re, d_lam_im, d_log_dt, d_b_re, d_b_im = disc_vjp((d_lbr, d_lbi, d_bbr, d_bbi))

    sg = {"mix_norm": d_mix_norm, "b_gate": d_b_gate, "conv_dw": ddw[:KW], "conv_dw_b": d_dw_b,
          "conv_ln_g": d_ln_g, "conv_ln_b": d_ln_b, "ssm_lam_re": d_lam_re, "ssm_lam_im": d_lam_im,
          "ssm_log_dt": d_log_dt, "ssm_b_re": d_b_re, "ssm_b_im": d_b_im, "ssm_c_re": d_c_re, "ssm_c_im": d_c_im,
          "ssm_d": d_ssm_d, "ffn2_norm": d_ffn2_norm, "final_norm": d_final}
    late = ["meta_tokens", "ffn1_norm"]
    early = [n for n in small if n not in late]

    (dh0, da1, db1, s1, n1, d_ffn1_norm), got = _ffn_bwd(
        h0, ffn1_norm, dh1, a1, b1, gw["ffn1_w1"], gw["ffn1_w3"], gw["ffn1_w2"], "ffn1_bwd",
        _join(_scatter_chips_behind(pair_b), _gather_all_behind(_pack([sg[n] for n in early]))))
    recv_b, early_all = got[:len(grp_b)], got[len(grp_b)]
    gbig["ffn1_w1"] = _wgrad(da1, n1, "ffn1_dw1")
    gbig["ffn1_w3"] = _wgrad(db1, n1, "ffn1_dw3")
    gbig["ffn1_w2"] = _wgrad(s1, dh1, "ffn1_dw2", 0.5)
    grad_x, loss = lax.optimization_barrier((dh0[FRONT:][None], lax.psum(loss_part[0, 0], ("x", "y", "c"))))
    sg["meta_tokens"] = dh0[FRONT - NMETA:FRONT]
    sg["ffn1_norm"] = d_ffn1_norm

    pair_a = pair_sums(grp_a, "ffn1")
    late_all = _gather_all(_pack([sg[n] for n in late]), "gather_late_grads")
    sa_send, sa_recv, pair_a, land_s, sa_token = _chips_start(
        pair_a, [jax.ShapeDtypeStruct(p.shape, p.dtype) for p in pair_a], False, "scatter_ffn1_start", [late_all])

    out_g, out_d, out_m, out_v = {}, {}, {}, {}

    def finish(group, recvs, tag, after=None):
        halves = [None] * len(group)
        for idx in _by_shape(recvs):
            res = _sum_slots([recvs[i] for i in idx], "sum_" + group[idx[0]], after)
            for i, r in zip(idx, res):
                halves[i] = r
        fours = [(view(args[n], n), f.reshape(1, f.shape[0] * f.shape[1], f.shape[2]), view(args["m_" + n], n),
                  view(args["v_" + n], n)) for n, f in zip(group, _swap_halves(halves, "swap_" + tag))]
        for idx in _by_shape([four[0] for four in fours]):
            for i, (g3, d3, m3, v3) in zip(idx, _adamw([fours[i] for i in idx], "adamw_" + group[idx[0]])):
                n = group[i]
                out_g[n], out_d[n], out_m[n], out_v[n] = (view(t, n) for t in (g3, d3, m3, v3))
                done.append(d3)

    done = []
    finish(grp_b + grp_c, list(recv_b) + list(recv_c), "mix_ffn2", sa_token)

    sgr = dict(zip(early, _unpack(_sum_slots([early_all], "sum_early", sa_token)[0], [sg[n].shape for n in early])))
    sgr.update(zip(late, _unpack(_sum_slots([late_all], "sum_late")[0], [sg[n].shape for n in late])))
    sgr["meta_tokens"] = lax.dynamic_slice_in_dim(sgr["meta_tokens"], chip * (D // NSH), D // NSH, axis=1)
    sgr["conv_dw"] = lax.dynamic_slice_in_dim(sgr["conv_dw"], chip * (DC // NSH), DC // NSH, axis=1)
    pshapes = [args[n].shape for n in small]
    _, d_s, m_s, v_s = _adamw([(small_wmv[0], _pack([sgr[n] for n in small])[None], small_wmv[1], small_wmv[2])],
                              "adamw_small")[0]
    for n, g_, d_, m_, v_ in zip(small, [sgr[n] for n in small], _unpack(d_s[0], pshapes),
                                 _unpack(m_s[0], pshapes), _unpack(v_s[0], pshapes)):
        out_g[n], out_d[n], out_m[n], out_v[n] = g_.reshape(args[n].shape), d_, m_, v_

    pair_a, recv_a = _chips_wait(sa_send, sa_recv, pair_a, land_s, [d_s, grad_x] + done, False,
                                 "scatter_ffn1_wait")
    finish(grp_a, _fill_own(pair_a, recv_a, "own_ffn1"), "ffn1")
    return (loss, grad_x, *[out_g[n] for n in names], *[out_d[n] for n in names],
            *[out_m[n] for n in names], *[out_v[n] for n in names])
```

```python
import math

import jax
import jax.numpy as jnp
from jax import lax
from jax.experimental import pallas as pl
from jax.experimental.pallas import tpu as pltpu

F32 = jnp.float32
BF16 = jnp.bfloat16

D = 1024
NSH = 4
F = 2816
FS = F // NSH
DC = 512
DS = 512
DIN = 2 * DC + DS + 2 * D
WS = DIN // NSH
KW = 31
KWP = 32
CONV_ROWS = 64
NMETA = 16
FRONT = 128
G, P, H = 32, 64, 16
NST = G * P
NQ = 4
QS = NST // NQ
QU = DS // NQ
NSEG = 32
NGRP = NSEG // 8
NCH = 8
SOFF = 8
EPS = 1e-6
LR, B1, B2, AEPS, WD, STEP = 1e-3, 0.9, 0.999, 1e-8, 0.01, 10
VMEM_LIMIT = 58 * 1024 * 1024
MESH = pl.DeviceIdType.MESH
ANY = pl.BlockSpec(memory_space=pl.ANY)


def _params(*sem):
    return pltpu.CompilerParams(dimension_semantics=sem, vmem_limit_bytes=VMEM_LIMIT)


def _res(shape):
    nd = len(shape)
    return pl.BlockSpec(shape, lambda *_: (0,) * nd, pipeline_mode=pl.Buffered(1))


def _tile(n, cap, mult=16):
    best = None
    for t in range(mult, min(n, cap) + 1, mult):
        if n % t == 0:
            best = t
    assert best is not None, (n, cap, mult)
    return best


def _dot(a, b):
    return jnp.dot(a, b, preferred_element_type=F32)


def _dot_nt(a, b):
    return lax.dot_general(a, b, (((1,), (1,)), ((), ())), preferred_element_type=F32)


def _dot_tn(a, b):
    return lax.dot_general(a, b, (((0,), (0,)), ((), ())), preferred_element_type=F32)


def _sigmoid(x):
    return 1.0 / (1.0 + jnp.exp(-x))


_GC = math.sqrt(2.0 / math.pi)
_GA = 0.044715


def _gelu(x):
    return 0.5 * x * (1.0 + jnp.tanh(_GC * (x + _GA * x * x * x)))


def _gelu_grad(x):
    t = jnp.tanh(_GC * (x + _GA * x * x * x))
    return 0.5 * (1.0 + t) + 0.5 * x * (1.0 - t * t) * _GC * (1.0 + 3.0 * _GA * x * x)


def _rms(hv, g):
    r = lax.rsqrt(jnp.mean(hv * hv, axis=-1, keepdims=True) + EPS)
    return hv * r * g, r


def _rms_bwd(dn, hv, r, g):
    xh = hv * r
    dxh = dn * g
    return r * (dxh - xh * jnp.mean(dxh * xh, axis=-1, keepdims=True)), xh


def _acc_rows(ref, part, first):
    @pl.when(first)
    def _():
        ref[...] = part

    @pl.when(jnp.logical_not(first))
    def _():
        ref[...] += part


def _coords():
    return lax.axis_index("x"), lax.axis_index("y"), lax.axis_index("c")


def _flip(v, d):
    return 1 - v if d else v


def _run(local, remote):
    for cp in local + remote:
        cp.start()
    for cp in remote:
        cp.wait()
    for cp in local:
        cp.wait()


def _via_vmem(src, dst, stage, sems, i):
    return (pltpu.make_async_copy(src, stage, sems.at[2 * i]), pltpu.make_async_copy(stage, dst, sems.at[2 * i + 1]))


def _run_staged(staged, remote):
    for load, _ in staged:
        load.start()
    for cp in remote:
        cp.start()
    for load, store in staged:
        load.wait()
        store.start()
    for cp in remote:
        cp.wait()
    for _, store in staged:
        store.wait()


_REL3 = ((1, 0), (0, 1), (1, 1))


class _Behind:
    def __init__(self, arrays, out_shapes, scratch, build, alias_pairs=()):
        self.arrays, self.out_shapes, self.scratch, self.build = list(arrays), list(out_shapes), list(scratch), build
        self.alias_pairs = list(alias_pairs)

    def aliases(self):
        return self.alias_pairs

    def start(self, ins, outs, scr):
        staged, remote = self.build(ins, outs, scr)
        for load, _ in staged:
            load.start()
        for cp in remote:
            cp.start()

    def finish(self, ins, outs, scr):
        staged, remote = self.build(ins, outs, scr)
        for load, store in staged:
            load.wait()
            store.start()
        for cp in remote:
            cp.wait()
        for _, store in staged:
            store.wait()


def _call(body, comm, *, name, grid, in_specs, out_specs, out_shape, scratch_shapes=(), params):
    in_specs, out_specs, out_shape = list(in_specs), list(out_specs), list(out_shape)
    scratch_shapes = list(scratch_shapes)
    if comm is None:
        f = pl.pallas_call(body, name=name, grid=grid, in_specs=in_specs, out_specs=out_specs,
                           out_shape=out_shape, scratch_shapes=scratch_shapes, compiler_params=params)
        return lambda *args: (f(*args), [])
    ni, no, ns = len(in_specs), len(out_specs), len(scratch_shapes)
    ci, co = len(comm.arrays), len(comm.out_shapes)

    def hosted(*refs):
        ins, cin = refs[:ni], refs[ni:ni + ci]
        outs, cout = refs[ni + ci:ni + ci + no], refs[ni + ci + no:ni + ci + no + co]
        scr, cscr = refs[ni + ci + no + co:ni + ci + no + co + ns], refs[ni + ci + no + co + ns:]
        first = last = None
        for axis, size in enumerate(grid):
            i = pl.program_id(axis)
            first = (i == 0) if first is None else jnp.logical_and(first, i == 0)
            last = (i == size - 1) if last is None else jnp.logical_and(last, i == size - 1)

        @pl.when(first)
        def _():
            comm.start(cin, cout, cscr)

        body(*ins, *outs, *scr)

        @pl.when(last)
        def _():
            comm.finish(cin, cout, cscr)

    f = pl.pallas_call(hosted, name=name, grid=grid, in_specs=in_specs + [ANY] * ci,
                       out_specs=out_specs + [ANY] * co, out_shape=out_shape + comm.out_shapes,
                       scratch_shapes=scratch_shapes + comm.scratch,
                       input_output_aliases={ni + a: no + b for a, b in comm.aliases()},
                       compiler_params=_params(*(("arbitrary",) * len(grid))))

    def run(*args):
        res = f(*args, *comm.arrays)
        return res[:no], res[no:]

    return run


def _gather_half_behind(shards):
    n = len(shards)

    def build(ins, outs, scr):
        send, recv, loc = scr[:3]
        stage = scr[3:]
        x, y, c = _coords()
        me = 2 * x + y
        staged = [_via_vmem(ins[t], outs[t].at[me], stage[t], loc, t) for t in range(n)]
        remote = []
        for t in range(n):
            half = shards[t].shape[0] // 2
            mine = pl.ds(c * half, half)
            for k, (dx, dy) in enumerate(_REL3):
                remote.append(pltpu.make_async_remote_copy(
                    src_ref=ins[t].at[mine], dst_ref=outs[t].at[me, mine],
                    send_sem=send.at[3 * t + k], recv_sem=recv.at[3 * t + k],
                    device_id=(_flip(x, dx), _flip(y, dy), c), device_id_type=MESH))
        return staged, remote

    return _Behind(shards, [jax.ShapeDtypeStruct((NSH,) + s.shape, s.dtype) for s in shards],
                   [pltpu.SemaphoreType.DMA((3 * n,)), pltpu.SemaphoreType.DMA((3 * n,)),
                    pltpu.SemaphoreType.DMA((2 * n,))] + [pltpu.VMEM(s.shape, s.dtype) for s in shards], build)


def _pass_halves(gathered, name, own=()):
    n, m = len(gathered), len(own)

    def body(*refs):
        shards, outs = refs[n:n + m], refs[n + m:2 * n + m]
        send, recv, loc = refs[2 * n + m:2 * n + m + 3]
        stage = refs[2 * n + m + 3:]
        x, y, c = _coords()
        staged = [_via_vmem(shards[t], outs[t].at[2 * x + y], stage[t], loc, t) for t in range(m)]
        remote = []
        for t in range(n):
            half = gathered[t].shape[1] // 2
            mine = pl.ds(c * half, half)
            for k, (dx, dy) in enumerate(_REL3):
                slot = 2 * _flip(x, dx) + _flip(y, dy)
                remote.append(pltpu.make_async_remote_copy(
                    src_ref=outs[t].at[slot, mine], dst_ref=outs[t].at[slot, mine],
                    send_sem=send.at[3 * t + k], recv_sem=recv.at[3 * t + k],
                    device_id=(x, y, 1 - c), device_id_type=MESH))
        _run_staged(staged, remote)

    return pl.pallas_call(
        body, name=name,
        out_shape=[jax.ShapeDtypeStruct(g.shape, g.dtype) for g in gathered],
        in_specs=[ANY] * (n + m), out_specs=[ANY] * n, input_output_aliases={t: t for t in range(n)},
        scratch_shapes=[pltpu.SemaphoreType.DMA((3 * n,)), pltpu.SemaphoreType.DMA((3 * n,)),
                        pltpu.SemaphoreType.DMA((max(2 * m, 1),))] + [pltpu.VMEM(s.shape, s.dtype) for s in own],
        compiler_params=pltpu.CompilerParams(vmem_limit_bytes=VMEM_LIMIT),
    )(*gathered, *own)


def _fill_own(sums, recvs, name):
    n = len(sums)

    def body(*refs):
        ins, outs = refs[:n], refs[2 * n:3 * n]
        loc = refs[3 * n]
        stage = refs[3 * n + 1:]
        x, y, _ = _coords()
        me = 2 * x + y
        _run_staged([_via_vmem(ins[t].at[me], outs[t].at[me], stage[t], loc, t) for t in range(n)], [])

    return pl.pallas_call(
        body, name=name,
        out_shape=[jax.ShapeDtypeStruct(r.shape, r.dtype) for r in recvs],
        in_specs=[ANY] * (2 * n), out_specs=[ANY] * n, input_output_aliases={n + t: t for t in range(n)},
        scratch_shapes=[pltpu.SemaphoreType.DMA((2 * n,))] + [pltpu.VMEM(s.shape[1:], s.dtype) for s in sums],
        compiler_params=pltpu.CompilerParams(vmem_limit_bytes=VMEM_LIMIT),
    )(*sums, *recvs)


def _scatter_chips_behind(sums):
    n = len(sums)

    def build(ins, outs, scr):
        send, recv, loc = scr[:3]
        stage = scr[3:]
        x, y, c = _coords()
        me = 2 * x + y
        staged = [_via_vmem(ins[t].at[me], outs[t].at[me], stage[t], loc, t) for t in range(n)]
        remote = []
        for t in range(n):
            for k, (dx, dy) in enumerate(_REL3):
                px, py = _flip(x, dx), _flip(y, dy)
                remote.append(pltpu.make_async_remote_copy(
                    src_ref=ins[t].at[2 * px + py], dst_ref=outs[t].at[me],
                    send_sem=send.at[3 * t + k], recv_sem=recv.at[3 * t + k],
                    device_id=(px, py, c), device_id_type=MESH))
        return staged, remote

    return _Behind(sums, [jax.ShapeDtypeStruct(s.shape, s.dtype) for s in sums],
                   [pltpu.SemaphoreType.DMA((3 * n,)), pltpu.SemaphoreType.DMA((3 * n,)),
                    pltpu.SemaphoreType.DMA((2 * n,))] + [pltpu.VMEM(s.shape[1:], s.dtype) for s in sums], build)


def _gather_all_behind(a):
    def build(ins, outs, scr):
        send, recv, loc, stage = scr
        x, y, c = _coords()
        me = 4 * x + 2 * y + c
        staged = [_via_vmem(ins[0], outs[0].at[me], stage, loc, 0)]
        remote = [pltpu.make_async_remote_copy(
            src_ref=ins[0], dst_ref=outs[0].at[me], send_sem=send.at[k], recv_sem=recv.at[k],
            device_id=(_flip(x, dx), _flip(y, dy), _flip(c, dc)), device_id_type=MESH)
            for k, (dx, dy, dc) in enumerate(_REL7)]
        return staged, remote

    return _Behind([a], [jax.ShapeDtypeStruct((8,) + a.shape, a.dtype)],
                   [pltpu.SemaphoreType.DMA((7,)), pltpu.SemaphoreType.DMA((7,)), pltpu.SemaphoreType.DMA((2,)),
                    pltpu.VMEM(a.shape, a.dtype)], build)


HBM = pl.BlockSpec(memory_space=pltpu.HBM)
SEM = pl.BlockSpec(memory_space=pltpu.SEMAPHORE)
EFFECT = pltpu.SideEffectType.DATAFLOW_SIDE_EFFECTING


def _chip_copies(srcs, lands, send, recv, gather):
    x, y, c = _coords()
    me = 2 * x + y
    cps = []
    for t in range(len(srcs)):
        for k, (dx, dy) in enumerate(_REL3):
            px, py = _flip(x, dx), _flip(y, dy)
            if gather:
                half = srcs[t].shape[0] // 2
                mine = pl.ds(c * half, half)
                src, dst = srcs[t].at[mine], lands[t].at[me, mine]
            else:
                src, dst = srcs[t].at[2 * px + py], lands[t].at[me]
            cps.append(pltpu.make_async_remote_copy(
                src_ref=src, dst_ref=dst, send_sem=send.at[3 * t + k], recv_sem=recv.at[3 * t + k],
                device_id=(px, py, c), device_id_type=MESH))
    return cps


def _chips_start(arrays, land_shapes, gather, name, after=()):
    n = len(arrays)

    def body(*refs):
        srcs, lands = refs[:n], refs[n:2 * n]
        send, recv = refs[2 * n + len(after)], refs[2 * n + len(after) + 1]
        token = refs[-1]
        for cp in _chip_copies(srcs, lands, send, recv, gather):
            cp.start()
        token[...] = jnp.zeros_like(token)

    lands = [lax.empty(s.shape, s.dtype) for s in land_shapes]
    thru = [pltpu.HBM(a.shape, a.dtype) for a in arrays] + [pltpu.HBM(s.shape, s.dtype) for s in land_shapes]
    res = pl.pallas_call(
        body, name=name,
        out_shape=(pltpu.SemaphoreType.DMA((3 * n,)), pltpu.SemaphoreType.DMA((3 * n,)), *thru,
                   jax.ShapeDtypeStruct((8, 128), F32)),
        in_specs=[HBM] * (2 * n) + [ANY] * len(after),
        out_specs=(SEM, SEM, *([HBM] * (2 * n)), pl.BlockSpec(memory_space=pltpu.VMEM)),
        input_output_aliases={t: 2 + t for t in range(2 * n)},
        compiler_params=pltpu.CompilerParams(has_side_effects=EFFECT),
    )(*[pltpu.with_memory_space_constraint(a, pltpu.HBM) for a in arrays],
      *[pltpu.with_memory_space_constraint(z, pltpu.HBM) for z in lands], *after)
    return res[0], res[1], list(res[2:2 + n]), list(res[2 + n:2 + 2 * n]), res[-1]


def _chips_wait(send, recv, arrays, lands, after, gather, name):
    n = len(arrays)

    def body(*refs):
        srcs, ls = refs[:n], refs[n:2 * n]
        sd, rv = refs[2 * n], refs[2 * n + 1]
        for cp in _chip_copies(srcs, ls, sd, rv, gather):
            cp.wait_send()
            cp.wait_recv()

    res = pl.pallas_call(
        body, name=name,
        out_shape=[pltpu.HBM(a.shape, a.dtype) for a in arrays] + [pltpu.HBM(z.shape, z.dtype) for z in lands],
        in_specs=[HBM] * (2 * n) + [SEM, SEM] + [ANY] * len(after), out_specs=[HBM] * (2 * n),
        input_output_aliases={t: t for t in range(2 * n)},
        compiler_params=pltpu.CompilerParams(has_side_effects=EFFECT),
    )(*arrays, *lands, send, recv, *after)
    return list(res[:n]), list(res[n:])


def _join(*parts):
    def cut(seq, key):
        res, o = [], 0
        for p in parts:
            k = len(getattr(p, key))
            res.append(seq[o:o + k])
            o += k
        return res

    def build(ins, outs, scr):
        staged, remote = [], []
        for p, i, o, s in zip(parts, cut(ins, "arrays"), cut(outs, "out_shapes"), cut(scr, "scratch")):
            st, rm = p.build(i, o, s)
            staged += st
            remote += rm
        return staged, remote

    pairs, ai, oi = [], 0, 0
    for p in parts:
        pairs += [(ai + a, oi + b) for a, b in p.alias_pairs]
        ai, oi = ai + len(p.arrays), oi + len(p.out_shapes)
    return _Behind(sum((p.arrays for p in parts), []), sum((p.out_shapes for p in parts), []),
                   sum((p.scratch for p in parts), []), build, pairs)


def _pass_halves_behind(gathered):
    n = len(gathered)

    def build(ins, outs, scr):
        send, recv = scr
        x, y, c = _coords()
        remote = []
        for t in range(n):
            half = gathered[t].shape[1] // 2
            mine = pl.ds(c * half, half)
            for k, (dx, dy) in enumerate(_REL3):
                slot = 2 * _flip(x, dx) + _flip(y, dy)
                remote.append(pltpu.make_async_remote_copy(
                    src_ref=outs[t].at[slot, mine], dst_ref=outs[t].at[slot, mine],
                    send_sem=send.at[3 * t + k], recv_sem=recv.at[3 * t + k],
                    device_id=(x, y, 1 - c), device_id_type=MESH))
        return [], remote

    return _Behind(gathered, [jax.ShapeDtypeStruct(g.shape, g.dtype) for g in gathered],
                   [pltpu.SemaphoreType.DMA((3 * n,)), pltpu.SemaphoreType.DMA((3 * n,))], build,
                   [(t, t) for t in range(n)])


_REL7 = tuple((dx, dy, dc) for dx in (0, 1) for dy in (0, 1) for dc in (0, 1))[1:]


def _gather_all(a, name):
    def body(a_ref, o_ref, send, recv, loc):
        x, y, c = _coords()
        me = 4 * x + 2 * y + c
        local = [pltpu.make_async_copy(a_ref, o_ref.at[me], loc.at[0])]
        remote = [pltpu.make_async_remote_copy(
            src_ref=a_ref, dst_ref=o_ref.at[me], send_sem=send.at[k], recv_sem=recv.at[k],
            device_id=(_flip(x, dx), _flip(y, dy), _flip(c, dc)), device_id_type=MESH)
            for k, (dx, dy, dc) in enumerate(_REL7)]
        _run(local, remote)

    return pl.pallas_call(
        body, name=name,
        out_shape=jax.ShapeDtypeStruct((8,) + a.shape, a.dtype),
        in_specs=[ANY], out_specs=ANY,
        scratch_shapes=[pltpu.SemaphoreType.DMA((7,)), pltpu.SemaphoreType.DMA((7,)),
                        pltpu.SemaphoreType.DMA((1,))],
    )(a)


def _pair_exchange(grads, name):
    n = len(grads)

    def body(*refs):
        ins, outs = refs[:n], refs[n:2 * n]
        send, recv = refs[2 * n:]
        x, y, c = _coords()
        remote = []
        for t in range(n):
            half = grads[t].shape[1] // 2
            remote.append(pltpu.make_async_remote_copy(
                src_ref=ins[t].at[:, pl.ds((1 - c) * half, half)], dst_ref=outs[t],
                send_sem=send.at[t], recv_sem=recv.at[t],
                device_id=(x, y, 1 - c), device_id_type=MESH))
        _run([], remote)

    return pl.pallas_call(
        body, name=name,
        out_shape=[jax.ShapeDtypeStruct((NSH, g.shape[1] // 2, g.shape[2]), g.dtype) for g in grads],
        in_specs=[ANY] * n, out_specs=[ANY] * n,
        scratch_shapes=[pltpu.SemaphoreType.DMA((n,)), pltpu.SemaphoreType.DMA((n,))],
    )(*grads)


def _swap_halves(halves, name):
    n = len(halves)

    def body(*refs):
        ins, outs = refs[:n], refs[n:2 * n]
        send, recv, loc = refs[2 * n:2 * n + 3]
        stage = refs[2 * n + 3:]
        x, y, c = _coords()
        local = [_via_vmem(ins[t], outs[t].at[c], stage[t], loc, t) for t in range(n)]
        remote = [pltpu.make_async_remote_copy(
            src_ref=ins[t], dst_ref=outs[t].at[c], send_sem=send.at[t], recv_sem=recv.at[t],
            device_id=(x, y, 1 - c), device_id_type=MESH) for t in range(n)]
        _run_staged(local, remote)

    return pl.pallas_call(
        body, name=name,
        out_shape=[jax.ShapeDtypeStruct((2,) + h.shape, h.dtype) for h in halves],
        in_specs=[ANY] * n, out_specs=[ANY] * n,
        scratch_shapes=[pltpu.SemaphoreType.DMA((n,)), pltpu.SemaphoreType.DMA((n,)),
                        pltpu.SemaphoreType.DMA((2 * n,))]
        + [pltpu.VMEM(h.shape, h.dtype) for h in halves],
        compiler_params=pltpu.CompilerParams(vmem_limit_bytes=VMEM_LIMIT),
    )(*halves)


def _by_shape(arrays):
    groups = {}
    for i, a in enumerate(arrays):
        groups.setdefault((a.shape, a.dtype), []).append(i)
    return list(groups.values())


def _sum_slots(rs, name, after=None):
    n = len(rs)
    K, R, C = rs[0].shape
    tr = _tile(R, max(16, (1 << 22) // (n * K * C)), 8 * (4 // rs[0].dtype.itemsize))

    def body(*refs):
        for r_ref, o_ref in zip(refs[:n], refs[len(refs) - n:]):
            acc = r_ref[0].astype(F32)
            for k in range(1, K):
                acc = acc + r_ref[k].astype(F32)
            o_ref[...] = acc

    dep = [] if after is None else [after]
    return pl.pallas_call(
        body, name=name, grid=(R // tr,),
        out_shape=[jax.ShapeDtypeStruct((R, C), F32)] * n,
        in_specs=[pl.BlockSpec((K, tr, C), lambda i: (0, i, 0))] * n + [ANY] * len(dep),
        out_specs=[pl.BlockSpec((tr, C), lambda i: (i, 0))] * n,
        compiler_params=_params("parallel"),
    )(*rs, *dep)


def _add_pair(gs, ss, core, name):
    n = len(gs)
    _, half, C = ss[0].shape
    tr = _tile(half, max(16, (1 << 21) // (n * C)))
    nb = half // tr

    def body(c_ref, *refs):
        for g_ref, s_ref, o_ref in zip(refs[:n], refs[n:2 * n], refs[2 * n:]):
            o_ref[...] = (g_ref[...].astype(F32) + s_ref[...].astype(F32)).astype(BF16)

    spec = pl.BlockSpec((1, tr, C), lambda j, i, c_ref: (j, i, 0))
    return pl.pallas_call(
        body, name=name,
        grid_spec=pltpu.PrefetchScalarGridSpec(
            num_scalar_prefetch=1, grid=(NSH, nb),
            in_specs=[pl.BlockSpec((1, tr, C), lambda j, i, c_ref: (j, c_ref[0] * nb + i, 0))] * n + [spec] * n,
            out_specs=[spec] * n),
        out_shape=[jax.ShapeDtypeStruct(ss[0].shape, BF16)] * n,
        compiler_params=_params("parallel", "parallel"),
    )(core, *gs, *ss)


def _adamw(wgmv, name):
    n = len(wgmv)
    _, R, C = wgmv[0][0].shape
    tr = _tile(R, max(8, (1 << 18) // (n * C)), 8)
    c1 = 1.0 / (1.0 - B1 ** STEP)
    c2 = 1.0 / (1.0 - B2 ** STEP)

    def body(*refs):
        for t in range(n):
            w_ref, g_ref, m_ref, v_ref = refs[4 * t:4 * t + 4]
            go_ref, d_ref, nm_ref, nv_ref = refs[4 * n + 4 * t:4 * n + 4 * t + 4]
            gv = g_ref[...]
            go_ref[...] = gv
            nm = B1 * m_ref[...] + (1.0 - B1) * gv
            nv = B2 * v_ref[...] + (1.0 - B2) * gv * gv
            nm_ref[...] = nm
            nv_ref[...] = nv
            d_ref[...] = -LR * ((nm * c1) / (jnp.sqrt(nv * c2) + AEPS) + WD * w_ref[...])

    spec = pl.BlockSpec((1, tr, C), lambda i: (0, i, 0))
    res = pl.pallas_call(
        body, name=name, grid=(R // tr,),
        out_shape=[jax.ShapeDtypeStruct((1, R, C), F32)] * (4 * n),
        in_specs=[spec] * (4 * n), out_specs=[spec] * (4 * n),
        compiler_params=_params("parallel"),
    )(*[a for four in wgmv for a in four])
    return [tuple(res[4 * t:4 * t + 4]) for t in range(n)]


def _ffn_fwd(h, g, w1, w3, w2, name, comm=None):
    L = h.shape[0]
    tm = _tile(L, 704)

    def body(h_ref, g_ref, w1_ref, w3_ref, w2_ref, o_ref, a_ref, b_ref, n_s, acc_s):
        j = pl.program_id(1)

        @pl.when(j == 0)
        def _():
            hv = h_ref[...]
            n, _ = _rms(hv, g_ref[...])
            n_s[...] = n.astype(BF16)
            acc_s[...] = hv

        n = n_s[...]
        a = _dot_nt(n, w1_ref[0])
        b = _dot_nt(n, w3_ref[0])
        a_ref[0] = a.astype(BF16)
        b_ref[0] = b.astype(BF16)
        s = (a * _sigmoid(a) * b).astype(BF16)
        acc_s[...] += 0.5 * _dot(s, w2_ref[0])

        @pl.when(j == NSH - 1)
        def _():
            o_ref[...] = acc_s[...]

    row = pl.BlockSpec((tm, D), lambda i, j: (i, 0))
    hid = pl.BlockSpec((1, tm, FS), lambda i, j: (j, i, 0))
    wsp = pl.BlockSpec((1, FS, D), lambda i, j: (j, 0, 0))
    return _call(
        body, comm, name=name, grid=(L // tm, NSH),
        out_shape=[jax.ShapeDtypeStruct((L, D), F32),
                   jax.ShapeDtypeStruct((NSH, L, FS), BF16), jax.ShapeDtypeStruct((NSH, L, FS), BF16)],
        in_specs=[row, _res((1, D)), wsp, wsp, wsp],
        out_specs=[row, hid, hid],
        scratch_shapes=[pltpu.VMEM((tm, D), BF16), pltpu.VMEM((tm, D), F32)],
        params=_params("arbitrary", "arbitrary"),
    )(h, g, w1, w3, w2)


def _loss_head(hv, gv, tv, row0):
    y, r = _rms(hv, gv)
    row = row0 + lax.broadcasted_iota(jnp.int32, (hv.shape[0], 1), 0)
    e = jnp.where(row >= FRONT, y - tv, 0.0)
    dy = e * (1.0 / D)
    part = 0.5 * jnp.sum(jnp.sum(e * dy, axis=1, keepdims=True), axis=0, keepdims=True)
    dx, xh = _rms_bwd(dy, hv, r, gv)
    return dx, part, jnp.sum(dy * xh, axis=0, keepdims=True)


def _ffn_fwd_loss(h, g, w1, w3, w2, gf, tgt, name):
    L = h.shape[0]
    tm = _tile(L, 704)

    def body(h_ref, g_ref, w1_ref, w3_ref, w2_ref, gf_ref, t_ref, o_ref, a_ref, b_ref, loss_ref, dgf_ref,
             n_s, acc_s):
        i, j = pl.program_id(0), pl.program_id(1)

        @pl.when(j == 0)
        def _():
            hv = h_ref[...]
            n, _ = _rms(hv, g_ref[...])
            n_s[...] = n.astype(BF16)
            acc_s[...] = hv

        n = n_s[...]
        a = _dot_nt(n, w1_ref[0])
        b = _dot_nt(n, w3_ref[0])
        a_ref[0] = a.astype(BF16)
        b_ref[0] = b.astype(BF16)
        s = (a * _sigmoid(a) * b).astype(BF16)
        acc_s[...] += 0.5 * _dot(s, w2_ref[0])

        @pl.when(j == NSH - 1)
        def _():
            dx, part, dgf = _loss_head(acc_s[...], gf_ref[...], t_ref[...], i * tm)
            o_ref[...] = dx
            _acc_rows(loss_ref, part, i == 0)
            _acc_rows(dgf_ref, dgf, i == 0)

    row = pl.BlockSpec((tm, D), lambda i, j: (i, 0))
    hid = pl.BlockSpec((1, tm, FS), lambda i, j: (j, i, 0))
    wsp = pl.BlockSpec((1, FS, D), lambda i, j: (j, 0, 0))
    return pl.pallas_call(
        body, name=name, grid=(L // tm, NSH),
        out_shape=[jax.ShapeDtypeStruct((L, D), F32),
                   jax.ShapeDtypeStruct((NSH, L, FS), BF16), jax.ShapeDtypeStruct((NSH, L, FS), BF16),
                   jax.ShapeDtypeStruct((1, 1), F32), jax.ShapeDtypeStruct((1, D), F32)],
        in_specs=[row, _res((1, D)), wsp, wsp, wsp, _res((1, D)), row],
        out_specs=[row, hid, hid, pl.BlockSpec((1, 1), lambda i, j: (0, 0)),
                   pl.BlockSpec((1, D), lambda i, j: (0, 0))],
        scratch_shapes=[pltpu.VMEM((tm, D), BF16), pltpu.VMEM((tm, D), F32)],
        compiler_params=_params("arbitrary", "arbitrary"),
    )(h, g, w1, w3, w2, gf, tgt)


def _ffn_bwd(h, g, dout, a, b, w1, w3, w2, name, comm=None):
    L = h.shape[0]
    tm = _tile(L, 528)

    def body(h_ref, g_ref, do_ref, a_ref, b_ref, w1_ref, w3_ref, w2_ref,
             dh_ref, da_ref, db_ref, s_ref, n_ref, dg_ref, dob_s, dn_s):
        i, j = pl.program_id(0), pl.program_id(1)

        @pl.when(j == 0)
        def _():
            n, _ = _rms(h_ref[...], g_ref[...])
            n_ref[...] = n.astype(BF16)
            dob_s[...] = (0.5 * do_ref[...]).astype(BF16)
            dn_s[...] = jnp.zeros_like(dn_s)

        ds_all = _dot_nt(dob_s[...], w2_ref[0])
        rc = tm // 3
        for r0 in range(0, tm, rc):
            rows = pl.ds(r0, rc)
            av = a_ref[0, rows, :].astype(F32)
            bv = b_ref[0, rows, :].astype(F32)
            ds = ds_all[r0:r0 + rc, :]
            sig = _sigmoid(av)
            sa = av * sig
            s_ref[0, rows, :] = (sa * bv).astype(BF16)
            da_ref[0, rows, :] = (ds * bv * (sig + sa * (1.0 - sig))).astype(BF16)
            db_ref[0, rows, :] = (ds * sa).astype(BF16)
        dn_s[...] += _dot(da_ref[0], w1_ref[0]) + _dot(db_ref[0], w3_ref[0])

        @pl.when(j == NSH - 1)
        def _():
            hv = h_ref[...]
            gv = g_ref[...]
            r = lax.rsqrt(jnp.mean(hv * hv, axis=-1, keepdims=True) + EPS)
            dn = dn_s[...]
            dx, xh = _rms_bwd(dn, hv, r, gv)
            dh_ref[...] = do_ref[...] + dx
            _acc_rows(dg_ref, jnp.sum(dn * xh, axis=0, keepdims=True), i == 0)

    row = pl.BlockSpec((tm, D), lambda i, j: (i, 0))
    hid = pl.BlockSpec((1, tm, FS), lambda i, j: (j, i, 0))
    wsp = pl.BlockSpec((1, FS, D), lambda i, j: (j, 0, 0))
    return _call(
        body, comm, name=name, grid=(L // tm, NSH),
        out_shape=[jax.ShapeDtypeStruct((L, D), F32)]
        + [jax.ShapeDtypeStruct((NSH, L, FS), BF16)] * 3
        + [jax.ShapeDtypeStruct((L, D), BF16), jax.ShapeDtypeStruct((1, D), F32)],
        in_specs=[row, _res((1, D)), row, hid, hid,
                  wsp, wsp, wsp],
        out_specs=[row, hid, hid, hid, row, pl.BlockSpec((1, D), lambda i, j: (0, 0))],
        scratch_shapes=[pltpu.VMEM((tm, D), BF16), pltpu.VMEM((tm, D), F32)],
        params=_params("arbitrary", "arbitrary"),
    )(h, g, dout, a, b, w1, w3, w2)


def _wgrad(xm, ym, name, scale=1.0):
    xs, ys = xm.ndim == 3, ym.ndim == 3
    assert not (xs and ys)
    L = xm.shape[-2]
    K, N = xm.shape[-1], ym.shape[-1]
    tl = _tile(L, 2112)
    nl = L // tl
    if xs or ys:
        tn, grid_n = N, NSH
    else:
        tn = _tile(N, 1024, 128)
        grid_n = N // tn

    def body(x_ref, y_ref, o_ref, acc_s):
        l = pl.program_id(1)
        xv = x_ref[0] if xs else x_ref[...]
        yv = y_ref[0] if ys else y_ref[...]
        part = _dot_tn(xv.astype(BF16), yv.astype(BF16))
        _acc_rows(acc_s, part, l == 0)

        @pl.when(l == nl - 1)
        def _():
            res = (acc_s[...] * scale).astype(BF16)
            if xs or ys:
                o_ref[0] = res
            else:
                o_ref[...] = res

    if xs:
        x_spec = pl.BlockSpec((1, tl, K), lambda n, l: (n, l, 0))
        y_spec = pl.BlockSpec((tl, N), lambda n, l: (l, 0))
        o_spec = pl.BlockSpec((1, K, N), lambda n, l: (n, 0, 0))
        o_shape = (NSH, K, N)
    elif ys:
        x_spec = pl.BlockSpec((tl, K), lambda n, l: (l, 0))
        y_spec = pl.BlockSpec((1, tl, N), lambda n, l: (n, l, 0))
        o_spec = pl.BlockSpec((1, K, N), lambda n, l: (n, 0, 0))
        o_shape = (NSH, K, N)
    else:
        x_spec = pl.BlockSpec((tl, K), lambda n, l: (l, 0))
        y_spec = pl.BlockSpec((tl, tn), lambda n, l: (l, n))
        o_spec = pl.BlockSpec((K, tn), lambda n, l: (0, n))
        o_shape = (K, N)
    return pl.pallas_call(
        body, name=name, grid=(grid_n, nl),
        out_shape=jax.ShapeDtypeStruct(o_shape, BF16),
        in_specs=[x_spec, y_spec], out_specs=o_spec,
        scratch_shapes=[pltpu.VMEM((K, tn), F32)],
        compiler_params=_params("parallel", "arbitrary"),
    )(xm, ym)


def _mix_in_fwd(h, g, w_in, b_gate, name, comm=None):
    L = h.shape[0]
    tm = _tile(L, 528)

    def body(h_ref, g_ref, w_ref, bg_ref, vg_ref, uf_ref, gt_ref):
        u, _ = _rms(h_ref[...], g_ref[...])
        ub = u.astype(BF16)
        p = [_dot(ub, w_ref[j]) for j in range(NSH)]
        a0, a1 = 2 * DC - WS, 2 * DC + DS - WS
        vg_ref[:, 0:WS] = p[0].astype(BF16)
        vg_ref[:, WS:2 * DC] = p[1][:, 0:a0].astype(BF16)
        uf_ref[...] = p[1][:, a0:a1].astype(BF16)
        gin = jnp.concatenate([p[1][:, a1:], p[2], p[3]], axis=1)
        gt_ref[...] = _sigmoid(gin + bg_ref[...]).astype(BF16)

    def row(n):
        return pl.BlockSpec((tm, n), lambda i: (i, 0))

    return _call(
        body, comm, name=name, grid=(L // tm,),
        out_shape=[jax.ShapeDtypeStruct((L, 2 * DC), BF16), jax.ShapeDtypeStruct((L, DS), BF16),
                   jax.ShapeDtypeStruct((L, 2 * D), BF16)],
        in_specs=[row(D), _res((1, D)), _res((NSH, D, WS)), _res((1, 2 * D))],
        out_specs=[row(2 * DC), row(DS), row(2 * D)],
        params=_params("parallel"),
    )(h, g, w_in, b_gate)


def _mix_in_bwd(h, g, dres, dv, dgl, duf, dgate, w_in, name):
    L = h.shape[0]
    tm = _tile(L, 528)

    def body(h_ref, g_ref, dr_ref, dv_ref, dgl_ref, duf_ref, dgt_ref, w_ref, dh_ref, u_ref, dp_ref, dgm_ref):
        i = pl.program_id(0)
        hv = h_ref[...]
        gv = g_ref[...]
        u, r = _rms(hv, gv)
        u_ref[...] = u.astype(BF16)
        a0, a1 = 2 * DC - WS, 2 * DC + DS - WS
        b0 = WS - a1
        dp = [jnp.concatenate([dv_ref[...], dgl_ref[:, 0:WS - DC]], axis=1),
              jnp.concatenate([dgl_ref[:, WS - DC:], duf_ref[...], dgt_ref[:, 0:b0]], axis=1),
              dgt_ref[:, b0:b0 + WS], dgt_ref[:, b0 + WS:]]
        du = jnp.zeros((tm, D), F32)
        for j in range(NSH):
            dp_ref[j] = dp[j]
            du = du + _dot_nt(dp[j], w_ref[j])
        dx, xh = _rms_bwd(du, hv, r, gv)
        dh_ref[...] = dr_ref[...] + dx
        _acc_rows(dgm_ref, jnp.sum(du * xh, axis=0, keepdims=True), i == 0)

    def row(n):
        return pl.BlockSpec((tm, n), lambda i: (i, 0))

    return pl.pallas_call(
        body, name=name, grid=(L // tm,),
        out_shape=[jax.ShapeDtypeStruct((L, D), F32), jax.ShapeDtypeStruct((L, D), BF16),
                   jax.ShapeDtypeStruct((NSH, L, WS), BF16), jax.ShapeDtypeStruct((1, D), F32)],
        in_specs=[row(D), _res((1, D)), row(D), row(DC), row(DC), row(DS), row(2 * D), _res((NSH, D, WS))],
        out_specs=[row(D), row(D), pl.BlockSpec((NSH, tm, WS), lambda i: (0, i, 0)),
                   pl.BlockSpec((1, D), lambda i: (0, 0))],
        compiler_params=_params("arbitrary"),
    )(h, g, dres, dv, dgl, duf, dgate, w_in)


def _conv_fwd(vg, dw, dwb, name, comm=None):
    L = vg.shape[0]
    nc = DC // 128

    def body(v_ref, g_ref, dw_ref, dwb_ref, z_ref, zp_s):
        zp_s[0:KWP, :] = jnp.zeros((KWP, 128), F32)
        zp_s[KWP:, :] = v_ref[...].astype(F32) * _sigmoid(g_ref[...].astype(F32))
        for r0 in range(0, L, CONV_ROWS):
            acc = jnp.broadcast_to(dwb_ref[...], (CONV_ROWS, 128))
            for k in range(KW):
                acc = acc + dw_ref[k:k + 1, :] * zp_s[pl.ds(r0 + k + 2, CONV_ROWS), :]
            z_ref[pl.ds(r0, CONV_ROWS), :] = acc

    return _call(
        body, comm, name=name, grid=(nc,),
        out_shape=[jax.ShapeDtypeStruct((L, DC), F32)],
        in_specs=[pl.BlockSpec((L, 128), lambda c: (0, c)), pl.BlockSpec((L, 128), lambda c: (0, nc + c)),
                  pl.BlockSpec((KWP, 128), lambda c: (0, c)), pl.BlockSpec((1, 128), lambda c: (0, c))],
        out_specs=[pl.BlockSpec((L, 128), lambda c: (0, c))],
        scratch_shapes=[pltpu.VMEM((L + KWP, 128), F32)],
        params=_params("parallel"),
    )(vg, vg, dw, dwb)


def _conv_bwd(dz1, vg, dw, name):
    L = vg.shape[0]
    nc = DC // 128

    def body(dz_ref, v_ref, g_ref, dw_ref, dv_ref, dg_ref, ddw_ref, ddwb_ref, zp_s, dzp_s):
        vv = v_ref[...].astype(F32)
        sg = _sigmoid(g_ref[...].astype(F32))
        zp_s[0:KWP, :] = jnp.zeros((KWP, 128), F32)
        zp_s[KWP:, :] = vv * sg
        dz = dz_ref[...]
        dzp_s[0:L, :] = dz
        dzp_s[L:, :] = jnp.zeros((KWP, 128), F32)
        ddwb_ref[...] = jnp.sum(dz, axis=0, keepdims=True)
        part = [jnp.zeros((8, 128), F32) for _ in range(KW)]
        for r0 in range(0, L, CONV_ROWS):
            rows = pl.ds(r0, CONV_ROWS)
            dzc = dz_ref[rows, :]
            acc = jnp.zeros((CONV_ROWS, 128), F32)
            for k in range(KW):
                acc = acc + dw_ref[k:k + 1, :] * dzp_s[pl.ds(r0 + KW - 1 - k, CONV_ROWS), :]
                prod = dzc * zp_s[pl.ds(r0 + k + 2, CONV_ROWS), :]
                for q in range(CONV_ROWS // 8):
                    part[k] = part[k] + prod[8 * q:8 * (q + 1), :]
            vc = v_ref[rows, :].astype(F32)
            sc = _sigmoid(g_ref[rows, :].astype(F32))
            dv_ref[rows, :] = (acc * sc).astype(BF16)
            dg_ref[rows, :] = (acc * vc * sc * (1.0 - sc)).astype(BF16)
        for k in range(KW):
            ddw_ref[k:k + 1, :] = jnp.sum(part[k], axis=0, keepdims=True)
        ddw_ref[KW:KWP, :] = jnp.zeros((KWP - KW, 128), F32)

    col = pl.BlockSpec((L, 128), lambda c: (0, c))
    return pl.pallas_call(
        body, name=name, grid=(nc,),
        out_shape=[jax.ShapeDtypeStruct((L, DC), BF16), jax.ShapeDtypeStruct((L, DC), BF16),
                   jax.ShapeDtypeStruct((KWP, DC), F32), jax.ShapeDtypeStruct((1, DC), F32)],
        in_specs=[col, col, pl.BlockSpec((L, 128), lambda c: (0, nc + c)),
                  pl.BlockSpec((KWP, 128), lambda c: (0, c))],
        out_specs=[col, col, pl.BlockSpec((KWP, 128), lambda c: (0, c)), pl.BlockSpec((1, 128), lambda c: (0, c))],
        scratch_shapes=[pltpu.VMEM((L + KWP, 128), F32), pltpu.VMEM((L + KWP, 128), F32)],
        compiler_params=_params("parallel"),
    )(dz1, vg, vg, dw)


NLB = QS // 128


def _lb_store(ref, rows, val):
    for cb in range(NLB):
        ref[cb, rows, :] = val[:, cb * 128:(cb + 1) * 128]


def _lb_load(ref, rows):
    return jnp.concatenate([ref[cb, rows, :] for cb in range(NLB)], axis=1)


def _scan(xr_ref, xi_ref, base, T, ar, ai, atr, ati, reverse):
    W = ar.shape[1]
    ar, ai, atr, ati = (jnp.broadcast_to(v, (8, W)) for v in (ar, ai, atr, ati))
    zero = jnp.zeros((8, W), F32)

    def rows(t, g):
        tt = T - 1 - t if reverse else t
        return pl.ds(base + g * 8 * T + tt, 8, stride=T)

    def make_step(store):
        def step(t, carry):
            out = []
            for g in range(NGRP):
                sr, si = carry[2 * g], carry[2 * g + 1]
                idx = rows(t, g)
                nr = ar * sr - ai * si + _lb_load(xr_ref, idx)
                ni = ar * si + ai * sr + _lb_load(xi_ref, idx)
                if store:
                    _lb_store(xr_ref, idx, nr)
                    _lb_store(xi_ref, idx, ni)
                out += [nr, ni]
            return tuple(out)
        return step

    ends = lax.fori_loop(0, T, make_step(False), (zero,) * (2 * NGRP))
    sub = lax.broadcasted_iota(jnp.int32, (8, W), 0)
    edge = sub == (7 if reverse else 0)
    shift, last = (7, 0) if reverse else (1, 7)
    inr, ini = jnp.zeros((1, W), F32), jnp.zeros((1, W), F32)
    starts = [None] * (2 * NGRP)
    for g in (reversed(range(NGRP)) if reverse else range(NGRP)):
        er, ei = ends[2 * g], ends[2 * g + 1]
        cr, ci = jnp.where(edge, inr, 0.0), jnp.where(edge, ini, 0.0)
        for _ in range(7):
            nr = atr * cr - ati * ci + er
            ni = atr * ci + ati * cr + ei
            cr = jnp.where(edge, inr, pltpu.roll(nr, shift, 0))
            ci = jnp.where(edge, ini, pltpu.roll(ni, shift, 0))
        starts[2 * g], starts[2 * g + 1] = cr, ci
        inr = (atr * cr - ati * ci + er)[last:last + 1]
        ini = (atr * ci + ati * cr + ei)[last:last + 1]
    lax.fori_loop(0, T, make_step(True), tuple(starts))


def _ssm_fwd(uf, bre, bim, cre, cim, lamp, dsk, name, comm=None):
    L = uf.shape[0]
    T = L // NSEG
    tc = L // NCH

    def body(u_ref, bre_ref, bim_ref, cre_ref, cim_ref, lam_ref, d_ref, y_ref, sr_s, si_s):
        for k in range(NCH):
            sl = slice(k * tc, (k + 1) * tc)
            uk = u_ref[sl, :]
            _lb_store(sr_s, sl, _dot(uk, bre_ref[0]))
            _lb_store(si_s, sl, _dot(uk, bim_ref[0]))
        _scan(sr_s, si_s, 0, T, lam_ref[0:1, :], lam_ref[1:2, :], lam_ref[2:3, :], lam_ref[3:4, :], False)
        for k in range(NCH):
            sl = slice(k * tc, (k + 1) * tc)
            y_ref[sl, :] = (_dot(_lb_load(sr_s, sl).astype(BF16), cre_ref[0])
                            - _dot(_lb_load(si_s, sl).astype(BF16), cim_ref[0])
                            + d_ref[...] * u_ref[sl, :].astype(F32))

    return _call(
        body, comm, name=name, grid=(NQ,),
        out_shape=[jax.ShapeDtypeStruct((L, DS), F32)],
        in_specs=[pl.BlockSpec((L, QU), lambda q: (0, q)),
                  pl.BlockSpec((1, QU, QS), lambda q: (q, 0, 0)), pl.BlockSpec((1, QU, QS), lambda q: (q, 0, 0)),
                  pl.BlockSpec((1, QS, QU), lambda q: (q, 0, 0)), pl.BlockSpec((1, QS, QU), lambda q: (q, 0, 0)),
                  pl.BlockSpec((8, QS), lambda q: (0, q)), pl.BlockSpec((1, QU), lambda q: (0, q))],
        out_specs=[pl.BlockSpec((L, QU), lambda q: (0, q))],
        scratch_shapes=[pltpu.VMEM((NLB, L, 128), F32), pltpu.VMEM((NLB, L, 128), F32)],
        params=_params("parallel"),
    )(uf, bre, bim, cre, cim, lamp, dsk)


def _ssm_bwd(uf, dyss, bre, bim, cre, cim, lamp, dsk, name, comm=None):
    L = uf.shape[0]
    T = L // NSEG
    tc = L // NCH

    def body(u_ref, dy_ref, bre_ref, bim_ref, cre_ref, cim_ref, lam_ref, d_ref,
             du_ref, dbre_ref, dbim_ref, dcre_ref, dcim_ref, dlam_ref, dd_ref, sr_s, si_s, gr_s, gi_s):
        _lb_store(sr_s, slice(0, SOFF), jnp.zeros((SOFF, QS), F32))
        _lb_store(si_s, slice(0, SOFF), jnp.zeros((SOFF, QS), F32))
        for k in range(NCH):
            sl = slice(k * tc, (k + 1) * tc)
            ss = slice(SOFF + k * tc, SOFF + (k + 1) * tc)
            uk = u_ref[sl, :]
            dyk = dy_ref[sl, :].astype(BF16)
            _lb_store(sr_s, ss, _dot(uk, bre_ref[0]))
            _lb_store(si_s, ss, _dot(uk, bim_ref[0]))
            _lb_store(gr_s, sl, _dot_nt(dyk, cre_ref[0]))
            _lb_store(gi_s, sl, -_dot_nt(dyk, cim_ref[0]))
        ar, ai, atr, ati = lam_ref[0:1, :], lam_ref[1:2, :], lam_ref[2:3, :], lam_ref[3:4, :]
        _scan(sr_s, si_s, SOFF, T, ar, ai, atr, ati, False)
        _scan(gr_s, gi_s, 0, T, ar, -ai, atr, -ati, True)
        dbre = jnp.zeros((QU, QS), F32)
        dbim = jnp.zeros((QU, QS), F32)
        dcre = jnp.zeros((QS, QU), F32)
        dcim = jnp.zeros((QS, QU), F32)
        dd = jnp.zeros((1, QU), F32)
        qr = jnp.zeros((1, QS), F32)
        qi = jnp.zeros((1, QS), F32)
        for k in range(NCH):
            sl = slice(k * tc, (k + 1) * tc)
            ss = slice(SOFF + k * tc, SOFF + (k + 1) * tc)
            sp = slice(SOFF - 1 + k * tc, SOFF - 1 + (k + 1) * tc)
            uk = u_ref[sl, :]
            dyk = dy_ref[sl, :]
            dyb = dyk.astype(BF16)
            gr, gi = _lb_load(gr_s, sl), _lb_load(gi_s, sl)
            pr, pi = _lb_load(sr_s, sp), _lb_load(si_s, sp)
            qr = qr + jnp.sum(gr * pr + gi * pi, axis=0, keepdims=True)
            qi = qi + jnp.sum(gi * pr - gr * pi, axis=0, keepdims=True)
            grb, gib = gr.astype(BF16), gi.astype(BF16)
            du_ref[sl, :] = (_dot_nt(grb, bre_ref[0]) + _dot_nt(gib, bim_ref[0])
                             + dyk * d_ref[...]).astype(BF16)
            dbre = dbre + _dot_tn(uk, grb)
            dbim = dbim + _dot_tn(uk, gib)
            dcre = dcre + _dot_tn(_lb_load(sr_s, ss).astype(BF16), dyb)
            dcim = dcim - _dot_tn(_lb_load(si_s, ss).astype(BF16), dyb)
            dd = dd + jnp.sum(dyk * uk.astype(F32), axis=0, keepdims=True)
        dlam_ref[0] = jnp.concatenate([qr, qi, jnp.zeros((6, QS), F32)], axis=0)
        dbre_ref[0] = dbre
        dbim_ref[0] = dbim
        dcre_ref[0] = dcre
        dcim_ref[0] = dcim
        dd_ref[...] = dd

    col = pl.BlockSpec((L, QU), lambda q: (0, q))
    bsp = pl.BlockSpec((1, QU, QS), lambda q: (q, 0, 0))
    csp = pl.BlockSpec((1, QS, QU), lambda q: (q, 0, 0))
    return _call(
        body, comm, name=name, grid=(NQ,),
        out_shape=[jax.ShapeDtypeStruct((L, DS), BF16),
                   jax.ShapeDtypeStruct((NQ, QU, QS), F32), jax.ShapeDtypeStruct((NQ, QU, QS), F32),
                   jax.ShapeDtypeStruct((NQ, QS, QU), F32), jax.ShapeDtypeStruct((NQ, QS, QU), F32),
                   jax.ShapeDtypeStruct((NQ, 8, QS), F32), jax.ShapeDtypeStruct((1, DS), F32)],
        in_specs=[col, col, bsp, bsp, csp, csp,
                  pl.BlockSpec((8, QS), lambda q: (0, q)), pl.BlockSpec((1, QU), lambda q: (0, q))],
        out_specs=[col,
                   pl.BlockSpec((1, QU, QS), lambda q: (q, 0, 0)), pl.BlockSpec((1, QU, QS), lambda q: (q, 0, 0)),
                   pl.BlockSpec((1, QS, QU), lambda q: (q, 0, 0)), pl.BlockSpec((1, QS, QU), lambda q: (q, 0, 0)),
                   pl.BlockSpec((1, 8, QS), lambda q: (q, 0, 0)), pl.BlockSpec((1, QU), lambda q: (0, q))],
        scratch_shapes=[pltpu.VMEM((NLB, L + SOFF, 128), F32), pltpu.VMEM((NLB, L + SOFF, 128), F32),
                        pltpu.VMEM((NLB, L, 128), F32), pltpu.VMEM((NLB, L, 128), F32)],
        params=_params("parallel"),
    )(uf, dyss, bre, bim, cre, cim, lamp, dsk)


def _branches(z1_ref, yss_ref, gt_ref, lng_ref, lnb_ref, wp_ref, wv_ref, wg_ref):
    zf = z1_ref[...]
    mu = jnp.mean(zf, axis=-1, keepdims=True)
    zc = zf - mu
    rstd = lax.rsqrt(jnp.mean(zc * zc, axis=-1, keepdims=True) + EPS)
    zn = zc * rstd
    z2 = zn * lng_ref[...] + lnb_ref[...]
    sz = _sigmoid(z2)
    z3 = (z2 * sz).astype(BF16)
    y_conv = _dot(z3, wp_ref[...])
    yss = yss_ref[...]
    yg = _gelu(yss).astype(BF16)
    sv = _dot(yg, wv_ref[...])
    sig = _sigmoid(_dot(yg, wg_ref[...]))
    y_ssm = sv * sig
    gc = gt_ref[:, 0:D].astype(F32)
    gs = gt_ref[:, D:2 * D].astype(F32)
    m = gc * y_conv + gs * y_ssm
    return dict(rstd=rstd, zn=zn, z2=z2, sz=sz, z3=z3, y_conv=y_conv, yss=yss, yg=yg, sv=sv, sig=sig,
                y_ssm=y_ssm, gc=gc, gs=gs, m=m)


def _merge_fwd(h, z1, yss, gate, lng, lnb, wp, wv, wg, wo, name, comm=None):
    L = h.shape[0]
    tm = _tile(L, 528)

    def body(h_ref, z1_ref, yss_ref, gt_ref, lng_ref, lnb_ref, wp_ref, wv_ref, wg_ref, wo_ref, o_ref):
        f = _branches(z1_ref, yss_ref, gt_ref, lng_ref, lnb_ref, wp_ref, wv_ref, wg_ref)
        o_ref[...] = h_ref[...] + _dot(f["m"].astype(BF16), wo_ref[...])

    def row(n):
        return pl.BlockSpec((tm, n), lambda i: (i, 0))

    return _call(
        body, comm, name=name, grid=(L // tm,),
        out_shape=[jax.ShapeDtypeStruct((L, D), F32)],
        in_specs=[row(D), row(DC), row(DS), row(2 * D), _res((1, DC)), _res((1, DC)),
                  _res((DC, D)), _res((DS, D)), _res((DS, D)), _res((D, D))],
        out_specs=[row(D)],
        params=_params("parallel"),
    )(h, z1, yss, gate, lng, lnb, wp, wv, wg, wo)


def _merge_bwd(dh, z1, yss, gate, lng, lnb, wp, wv, wg, wo, name):
    L = dh.shape[0]
    tm = _tile(L, 352)

    def body(dh_ref, z1_ref, yss_ref, gt_ref, lng_ref, lnb_ref, wp_ref, wv_ref, wg_ref, wo_ref,
             m_ref, dgt_ref, dyc_ref, z3_ref, dz1_ref, yg_ref, dsv_ref, dsg_ref, dyss_ref,
             dbg_ref, dlng_ref, dlnb_ref):
        i = pl.program_id(0)
        f = _branches(z1_ref, yss_ref, gt_ref, lng_ref, lnb_ref, wp_ref, wv_ref, wg_ref)
        gc, gs, sig, sv = f["gc"], f["gs"], f["sig"], f["sv"]
        m_ref[...] = f["m"].astype(BF16)
        z3_ref[...] = f["z3"]
        yg_ref[...] = f["yg"]
        dm = _dot_nt(dh_ref[...].astype(BF16), wo_ref[...])
        dgc = (dm * f["y_conv"] * gc * (1.0 - gc)).astype(BF16)
        dgs = (dm * f["y_ssm"] * gs * (1.0 - gs)).astype(BF16)
        dgt_ref[:, 0:D] = dgc
        dgt_ref[:, D:2 * D] = dgs
        part = jnp.concatenate([jnp.sum(dgc.astype(F32), axis=0, keepdims=True),
                                jnp.sum(dgs.astype(F32), axis=0, keepdims=True)], axis=1)
        _acc_rows(dbg_ref, part, i == 0)
        dyc = (dm * gc).astype(BF16)
        dyc_ref[...] = dyc
        dys = dm * gs
        dsv = (dys * sig).astype(BF16)
        dsg = (dys * sv * sig * (1.0 - sig)).astype(BF16)
        dsv_ref[...] = dsv
        dsg_ref[...] = dsg
        dyg = _dot_nt(dsv, wv_ref[...]) + _dot_nt(dsg, wg_ref[...])
        dyss_ref[...] = dyg * _gelu_grad(f["yss"])
        dz3 = _dot_nt(dyc, wp_ref[...])
        z2, sz, zn = f["z2"], f["sz"], f["zn"]
        dz2 = dz3 * sz * (1.0 + z2 * (1.0 - sz))
        _acc_rows(dlng_ref, jnp.sum(dz2 * zn, axis=0, keepdims=True), i == 0)
        _acc_rows(dlnb_ref, jnp.sum(dz2, axis=0, keepdims=True), i == 0)
        dzn = dz2 * lng_ref[...]
        dz1_ref[...] = f["rstd"] * (dzn - jnp.mean(dzn, axis=-1, keepdims=True)
                                    - zn * jnp.mean(dzn * zn, axis=-1, keepdims=True))

    def row(n):
        return pl.BlockSpec((tm, n), lambda i: (i, 0))

    def tot(n):
        return pl.BlockSpec((1, n), lambda i: (0, 0))

    return pl.pallas_call(
        body, name=name, grid=(L // tm,),
        out_shape=[jax.ShapeDtypeStruct((L, D), BF16), jax.ShapeDtypeStruct((L, 2 * D), BF16),
                   jax.ShapeDtypeStruct((L, D), BF16), jax.ShapeDtypeStruct((L, DC), BF16),
                   jax.ShapeDtypeStruct((L, DC), F32), jax.ShapeDtypeStruct((L, DS), BF16),
                   jax.ShapeDtypeStruct((L, D), BF16), jax.ShapeDtypeStruct((L, D), BF16),
                   jax.ShapeDtypeStruct((L, DS), F32),
                   jax.ShapeDtypeStruct((1, 2 * D), F32), jax.ShapeDtypeStruct((1, DC), F32),
                   jax.ShapeDtypeStruct((1, DC), F32)],
        in_specs=[row(D), row(DC), row(DS), row(2 * D), _res((1, DC)), _res((1, DC)),
                  _res((DC, D)), _res((DS, D)), _res((DS, D)), _res((D, D))],
        out_specs=[row(D), row(2 * D), row(D), row(DC), row(DC), row(DS), row(D), row(D), row(DS),
                   tot(2 * D), tot(DC), tot(DC)],
        compiler_params=_params("arbitrary"),
    )(dh, z1, yss, gate, lng, lnb, wp, wv, wg, wo)


def _ssm_disc(lam_re, lam_im, log_dt, b_re, b_im):
    lam = lax.complex(lam_re, lam_im)
    dt = jnp.exp(log_dt)[:, None]
    lam_bar = jnp.exp(lam * dt)
    bbar = ((lam_bar - 1.0) / lam)[..., None] * lax.complex(b_re, b_im)
    return jnp.real(lam_bar), jnp.imag(lam_bar), jnp.real(bbar), jnp.imag(bbar)


def _bdiag_in(m):
    m4 = m.reshape(NQ, G // NQ, P, H)
    return jnp.einsum("qgph,gk->qghkp", m4, jnp.eye(G // NQ, dtype=m.dtype)).reshape(NQ, QU, QS)


def _bdiag_out(m):
    m4 = m.reshape(NQ, G // NQ, H, P)
    return jnp.einsum("qghp,gk->qgpkh", m4, jnp.eye(G // NQ, dtype=m.dtype)).reshape(NQ, QS, QU)


def _diag_blocks(m4):
    return jnp.einsum("qiaib->qiab", m4).reshape(G, m4.shape[2], m4.shape[4])


def _pack(parts, rows_mult=8):
    flat = jnp.concatenate([p.reshape(-1).astype(F32) for p in parts])
    n = flat.shape[0]
    tot = -(-n // (128 * rows_mult)) * (128 * rows_mult)
    return jnp.pad(flat, (0, tot - n)).reshape(tot // 128, 128)


def _unpack(buf, shapes):
    flat = buf.reshape(-1)
    out, o = [], 0
    for s in shapes:
        n = math.prod(s)
        out.append(flat[o:o + n].reshape(s))
        o += n
    return out


def kernel(x, meta_tokens, ffn1_norm, ffn1_w1, ffn1_w3, ffn1_w2, mix_norm, w_in, b_gate, conv_dw, conv_dw_b, conv_ln_g, conv_ln_b, conv_proj, ssm_lam_re, ssm_lam_im, ssm_log_dt, ssm_b_re, ssm_b_im, ssm_c_re, ssm_c_im, ssm_d, ssm_w_v, ssm_w_g, w_out, ffn2_norm, ffn2_w1, ffn2_w3, ffn2_w2, final_norm, loss_target, m_meta_tokens, m_ffn1_norm, m_ffn1_w1, m_ffn1_w3, m_ffn1_w2, m_mix_norm, m_w_in, m_b_gate, m_conv_dw, m_conv_dw_b, m_conv_ln_g, m_conv_ln_b, m_conv_proj, m_ssm_lam_re, m_ssm_lam_im, m_ssm_log_dt, m_ssm_b_re, m_ssm_b_im, m_ssm_c_re, m_ssm_c_im, m_ssm_d, m_ssm_w_v, m_ssm_w_g, m_w_out, m_ffn2_norm, m_ffn2_w1, m_ffn2_w3, m_ffn2_w2, m_final_norm, v_meta_tokens, v_ffn1_norm, v_ffn1_w1, v_ffn1_w3, v_ffn1_w2, v_mix_norm, v_w_in, v_b_gate, v_conv_dw, v_conv_dw_b, v_conv_ln_g, v_conv_ln_b, v_conv_proj, v_ssm_lam_re, v_ssm_lam_im, v_ssm_log_dt, v_ssm_b_re, v_ssm_b_im, v_ssm_c_re, v_ssm_c_im, v_ssm_d, v_ssm_w_v, v_ssm_w_g, v_w_out, v_ffn2_norm, v_ffn2_w1, v_ffn2_w3, v_ffn2_w2, v_final_norm):
    args = dict(locals())
    names = ["meta_tokens", "ffn1_norm", "ffn1_w1", "ffn1_w3", "ffn1_w2", "mix_norm", "w_in", "b_gate",
             "conv_dw", "conv_dw_b", "conv_ln_g", "conv_ln_b", "conv_proj", "ssm_lam_re", "ssm_lam_im",
             "ssm_log_dt", "ssm_b_re", "ssm_b_im", "ssm_c_re", "ssm_c_im", "ssm_d", "ssm_w_v", "ssm_w_g",
             "w_out", "ffn2_norm", "ffn2_w1", "ffn2_w3", "ffn2_w2", "final_norm"]
    big = ["ffn1_w1", "ffn1_w3", "ffn1_w2", "w_in", "conv_proj", "ssm_w_v", "ssm_w_g", "w_out",
           "ffn2_w1", "ffn2_w3", "ffn2_w2"]
    small = [n for n in names if n not in big]

    xs = x[0]
    S = xs.shape[0]
    L = FRONT + S
    T = L // NSEG
    jx, jy = lax.axis_index("x"), lax.axis_index("y")
    chip = 2 * jx + jy

    small_all = _gather_all(_pack([meta_tokens, conv_dw[0]]), "gather_small")
    sm = small_all[0::2].reshape(NSH, -1)
    nmt = NMETA * (D // NSH)
    ndw = KW * (DC // NSH)
    meta_full = sm[:, :nmt].reshape(NSH, NMETA, D // NSH).transpose(1, 0, 2).reshape(NMETA, D)
    dw_full = sm[:, nmt:nmt + ndw].reshape(NSH, KW, DC // NSH).transpose(1, 0, 2).reshape(KW, DC)
    dw_pad = jnp.pad(dw_full, ((0, KWP - KW), (0, 0)))
    tposed = ("ffn1_w1", "ffn1_w3", "ffn2_w1", "ffn2_w3")

    def view(a, n):
        return jnp.swapaxes(a, 1, 2) if n in tposed else a

    grp_a = ["ffn1_w1", "ffn1_w3", "ffn1_w2"]
    grp_b = ["w_in", "conv_proj", "ssm_w_v", "ssm_w_g", "w_out"]
    grp_c = ["ffn2_w1", "ffn2_w3", "ffn2_w2"]

    shards = {n: view(args[n], n)[0].astype(BF16) for n in big}

    def shard(n):
        return shards[n]

    sh_a = [shard(n) for n in grp_a]
    ga_send, ga_recv, sh_a, land_a, _ = _chips_start(
        sh_a, [jax.ShapeDtypeStruct((NSH,) + s.shape, s.dtype) for s in sh_a], True, "gather_ffn1_start",
        [small_all])

    def cols(w):
        return w.transpose(1, 0, 2).reshape(w.shape[1], -1)

    disc_in = (ssm_lam_re[0], ssm_lam_im[0], ssm_log_dt[0], ssm_b_re[0], ssm_b_im[0])
    (lbr, lbi, bbr, bbi), disc_vjp = jax.vjp(_ssm_disc, *disc_in)
    lam_t = jnp.exp(lax.complex(ssm_lam_re[0], ssm_lam_im[0]) * (jnp.exp(ssm_log_dt[0])[:, None] * T))
    lamp = jnp.concatenate([lbr.reshape(1, NST), lbi.reshape(1, NST), jnp.real(lam_t).reshape(1, NST),
                            jnp.imag(lam_t).reshape(1, NST), jnp.zeros((4, NST), F32)], axis=0)
    bre_bd, bim_bd = _bdiag_in(bbr).astype(BF16), _bdiag_in(bbi).astype(BF16)
    cre_bd, cim_bd = _bdiag_out(ssm_c_re[0]).astype(BF16), _bdiag_out(ssm_c_im[0]).astype(BF16)

    h0 = lax.dynamic_update_slice(jnp.pad(xs, ((FRONT, 0), (0, 0))), meta_full, (FRONT - NMETA, 0))
    tgt = jnp.pad(loss_target[0], ((FRONT, 0), (0, 0)))
    small_wmv = [_pack([args[p + n] for n in small])[None] for p in ("", "m_", "v_")]
    early_work = [h0, tgt, bre_bd, bim_bd, cre_bd, cim_bd] + [shards[n] for n in grp_b + grp_c] + small_wmv
    sh_a, land_a = _chips_wait(ga_send, ga_recv, sh_a, land_a, early_work, True, "gather_ffn1_wait")
    gw = dict(zip(grp_a, _pass_halves(land_a, "pass_ffn1", sh_a)))
    (h1, a1, b1), got = _ffn_fwd(h0, ffn1_norm, gw["ffn1_w1"], gw["ffn1_w3"], gw["ffn1_w2"], "ffn1_fwd",
                                 _gather_half_behind([shard(n) for n in grp_b]))
    w_in_f = _pass_halves(got[:1], "pass_w_in")[0]
    (vg, uf, gate), got1 = _mix_in_fwd(h1, mix_norm, w_in_f, b_gate, "mix_in_fwd",
                                       _join(_gather_half_behind([shard("ffn2_w1")]),
                                             _pass_halves_behind(list(got[1:]))))
    gw.update(zip(grp_b[1:], got1[1:]))
    wp_f, wv_f, wg_f = cols(gw["conv_proj"]), cols(gw["ssm_w_v"]), cols(gw["ssm_w_g"])
    wo_f = gw["w_out"].reshape(D, D)
    (z1,), got3 = _conv_fwd(vg, dw_pad, conv_dw_b, "conv_fwd", _gather_half_behind([shard("ffn2_w3")]))
    (yss,), got2 = _ssm_fwd(uf, bre_bd, bim_bd, cre_bd, cim_bd, lamp, ssm_d, "ssm_fwd",
                            _gather_half_behind([shard("ffn2_w2")]))
    (h2,), got_c = _merge_fwd(h1, z1, yss, gate, conv_ln_g, conv_ln_b, wp_f, wv_f, wg_f, wo_f, "merge_fwd",
                              _pass_halves_behind([got1[0], got3[0], got2[0]]))
    gw.update(zip(grp_c, got_c))
    dh3, a2, b2, loss_part, d_final = _ffn_fwd_loss(
        h2, ffn2_norm, gw["ffn2_w1"], gw["ffn2_w3"], gw["ffn2_w2"], final_norm.reshape(1, D), tgt, "ffn2_fwd_loss")

    gbig = {}
    core = lax.axis_index("c").astype(jnp.int32).reshape(1)

    def pair_sums(group, tag):
        gl = [gbig[n] for n in group]
        sib = _pair_exchange(gl, "pair_exchange_" + tag)
        out = [None] * len(group)
        for idx in _by_shape(gl):
            res = _add_pair([gl[i] for i in idx], [sib[i] for i in idx], core, "pair_" + group[idx[0]])
            for i, r in zip(idx, res):
                out[i] = r
        return out

    (dh2, da2, db2, s2, n2, d_ffn2_norm), _ = _ffn_bwd(
        h2, ffn2_norm, dh3, a2, b2, gw["ffn2_w1"], gw["ffn2_w3"], gw["ffn2_w2"], "ffn2_bwd")
    gbig["ffn2_w1"] = _wgrad(da2, n2, "ffn2_dw1")
    gbig["ffn2_w3"] = _wgrad(db2, n2, "ffn2_dw3")
    gbig["ffn2_w2"] = _wgrad(s2, dh3, "ffn2_dw2", 0.5)
    pair_c = pair_sums(grp_c, "ffn2")
    (m_b, dgate, dyc, z3, dz1, yg, dsv, dsg, dyss, d_b_gate, d_ln_g, d_ln_b) = _merge_bwd(
        dh2, z1, yss, gate, conv_ln_g, conv_ln_b, wp_f, wv_f, wg_f, wo_f, "merge_bwd")
    gbig["w_out"] = _wgrad(m_b, dh2, "dw_out").reshape(NSH, D // NSH, D)

    def shard_cols(gm):
        return gm.reshape(gm.shape[0], NSH, -1).transpose(1, 0, 2)

    gbig["conv_proj"] = shard_cols(_wgrad(z3, dyc, "dw_proj"))
    gbig["ssm_w_v"] = shard_cols(_wgrad(yg, dsv, "dw_v"))
    gbig["ssm_w_g"] = shard_cols(_wgrad(yg, dsg, "dw_g"))
    dv, dgl, ddw, d_dw_b = _conv_bwd(dz1, vg, dw_pad, "conv_bwd")
    (duf, dbre, dbim, dcre, dcim, dlam, d_ssm_d), recv_c = _ssm_bwd(
        uf, dyss, bre_bd, bim_bd, cre_bd, cim_bd, lamp, ssm_d, "ssm_bwd", _scatter_chips_behind(pair_c))
    dh1, u_b, dproj, d_mix_norm = _mix_in_bwd(h1, mix_norm, dh2, dv, dgl, duf, dgate, w_in_f, "mix_in_bwd")
    gbig["w_in"] = _wgrad(u_b, dproj, "dw_in")
    pair_b = pair_sums(grp_b, "mix")

    d_bbr = _diag_blocks(dbre.reshape(NQ, 8, H, 8, P)).transpose(0, 2, 1)
    d_bbi = _diag_blocks(dbim.reshape(NQ, 8, H, 8, P)).transpose(0, 2, 1)
    d_c_re = _diag_blocks(dcre.reshape(NQ, 8, P, 8, H)).transpose(0, 2, 1)
    d_c_im = _diag_blocks(dcim.reshape(NQ, 8, P, 8, H)).transpose(0, 2, 1)
    d_lbr = dlam[:, 0, :].reshape(G, P)
    d_lbi = dlam[:, 1, :].reshape(G, P)
    d_lam_re, d_lam_im, d_log_dt, d_b_re, d_b_im = disc_vjp((d_lbr, d_lbi, d_bbr, d_bbi))

    sg = {"mix_norm": d_mix_norm, "b_gate": d_b_gate, "conv_dw": ddw[:KW], "conv_dw_b": d_dw_b,
          "conv_ln_g": d_ln_g, "conv_ln_b": d_ln_b, "ssm_lam_re": d_lam_re, "ssm_lam_im": d_lam_im,
          "ssm_log_dt": d_log_dt, "ssm_b_re": d_b_re, "ssm_b_im": d_b_im, "ssm_c_re": d_c_re, "ssm_c_im": d_c_im,
          "ssm_d": d_ssm_d, "ffn2_norm": d_ffn2_norm, "final_norm": d_final}
    late = ["meta_tokens", "ffn1_norm"]
    early = [n for n in small if n not in late]

    (dh0, da1, db1, s1, n1, d_ffn1_norm), got = _ffn_bwd(
        h0, ffn1_norm, dh1, a1, b1, gw["ffn1_w1"], gw["ffn1_w3"], gw["ffn1_w2"], "ffn1_bwd",
        _join(_scatter_chips_behind(pair_b), _gather_all_behind(_pack([sg[n] for n in early]))))
    recv_b, early_all = got[:len(grp_b)], got[len(grp_b)]
    gbig["ffn1_w1"] = _wgrad(da1, n1, "ffn1_dw1")
    gbig["ffn1_w3"] = _wgrad(db1, n1, "ffn1_dw3")
    gbig["ffn1_w2"] = _wgrad(s1, dh1, "ffn1_dw2", 0.5)
    grad_x = dh0[FRONT:][None]
    sg["meta_tokens"] = dh0[FRONT - NMETA:FRONT]
    sg["ffn1_norm"] = d_ffn1_norm

    pair_a = pair_sums(grp_a, "ffn1")
    late_all = _gather_all(_pack([sg[n] for n in late]), "gather_late_grads")
    sa_send, sa_recv, pair_a, land_s, sa_token = _chips_start(
        pair_a, [jax.ShapeDtypeStruct(p.shape, p.dtype) for p in pair_a], False, "scatter_ffn1_start", [late_all])

    out_g, out_d, out_m, out_v = {}, {}, {}, {}

    def finish(group, recvs, tag, after=None):
        halves = [None] * len(group)
        for idx in _by_shape(recvs):
            res = _sum_slots([recvs[i] for i in idx], "sum_" + group[idx[0]], after)
            for i, r in zip(idx, res):
                halves[i] = r
        fours = [(view(args[n], n), f.reshape(1, f.shape[0] * f.shape[1], f.shape[2]), view(args["m_" + n], n),
                  view(args["v_" + n], n)) for n, f in zip(group, _swap_halves(halves, "swap_" + tag))]
        for idx in _by_shape([four[0] for four in fours]):
            for i, (g3, d3, m3, v3) in zip(idx, _adamw([fours[i] for i in idx], "adamw_" + group[idx[0]])):
                n = group[i]
                out_g[n], out_d[n], out_m[n], out_v[n] = (view(t, n) for t in (g3, d3, m3, v3))
                done.append(d3)

    done = []
    finish(grp_b + grp_c, list(recv_b) + list(recv_c), "mix_ffn2", sa_token)

    sgr = dict(zip(early, _unpack(_sum_slots([early_all], "sum_early", sa_token)[0], [sg[n].shape for n in early])))
    sgr.update(zip(late, _unpack(_sum_slots([late_all], "sum_late")[0], [sg[n].shape for n in late])))
    sgr["meta_tokens"] = lax.dynamic_slice_in_dim(sgr["meta_tokens"], chip * (D // NSH), D // NSH, axis=1)
    sgr["conv_dw"] = lax.dynamic_slice_in_dim(sgr["conv_dw"], chip * (DC // NSH), DC // NSH, axis=1)
    pshapes = [args[n].shape for n in small]
    _, d_s, m_s, v_s = _adamw([(small_wmv[0], _pack([sgr[n] for n in small])[None], small_wmv[1], small_wmv[2])],
                              "adamw_small")[0]
    for n, g_, d_, m_, v_ in zip(small, [sgr[n] for n in small], _unpack(d_s[0], pshapes),
                                 _unpack(m_s[0], pshapes), _unpack(v_s[0], pshapes)):
        out_g[n], out_d[n], out_m[n], out_v[n] = g_.reshape(args[n].shape), d_, m_, v_

    loss = lax.psum(loss_part[0, 0], ("x", "y", "c"))
    pair_a, recv_a = _chips_wait(sa_send, sa_recv, pair_a, land_s, [d_s, grad_x] + done, False,
                                 "scatter_ffn1_wait")
    finish(grp_a, _fill_own(pair_a, recv_a, "own_ffn1"), "ffn1")
    return (loss, grad_x, *[out_g[n] for n in names], *[out_d[n] for n in names],
            *[out_m[n] for n in names], *[out_v[n] for n in names])
```

```python
import math

import jax
import jax.numpy as jnp
from jax import lax
from jax.experimental import pallas as pl
from jax.experimental.pallas import tpu as pltpu

F32 = jnp.float32
BF16 = jnp.bfloat16

D = 1024
NSH = 4
F = 2816
FS = F // NSH
DC = 512
DS = 512
DIN = 2 * DC + DS + 2 * D
WS = DIN // NSH
KW = 31
KWP = 32
CONV_ROWS = 64
NMETA = 16
FRONT = 128
G, P, H = 32, 64, 16
NST = G * P
NQ = 4
QS = NST // NQ
QU = DS // NQ
NSEG = 32
NGRP = NSEG // 8
NCH = 8
SOFF = 8
EPS = 1e-6
LR, B1, B2, AEPS, WD, STEP = 1e-3, 0.9, 0.999, 1e-8, 0.01, 10
VMEM_LIMIT = 58 * 1024 * 1024
MESH = pl.DeviceIdType.MESH
ANY = pl.BlockSpec(memory_space=pl.ANY)


def _params(*sem):
    return pltpu.CompilerParams(dimension_semantics=sem, vmem_limit_bytes=VMEM_LIMIT)


def _res(shape):
    nd = len(shape)
    return pl.BlockSpec(shape, lambda *_: (0,) * nd, pipeline_mode=pl.Buffered(1))


def _tile(n, cap, mult=16):
    best = None
    for t in range(mult, min(n, cap) + 1, mult):
        if n % t == 0:
            best = t
    assert best is not None, (n, cap, mult)
    return best


def _dot(a, b):
    return jnp.dot(a, b, preferred_element_type=F32)


def _dot_nt(a, b):
    return lax.dot_general(a, b, (((1,), (1,)), ((), ())), preferred_element_type=F32)


def _dot_tn(a, b):
    return lax.dot_general(a, b, (((0,), (0,)), ((), ())), preferred_element_type=F32)


def _sigmoid(x):
    return 1.0 / (1.0 + jnp.exp(-x))


_GC = math.sqrt(2.0 / math.pi)
_GA = 0.044715


def _gelu(x):
    return 0.5 * x * (1.0 + jnp.tanh(_GC * (x + _GA * x * x * x)))


def _gelu_grad(x):
    t = jnp.tanh(_GC * (x + _GA * x * x * x))
    return 0.5 * (1.0 + t) + 0.5 * x * (1.0 - t * t) * _GC * (1.0 + 3.0 * _GA * x * x)


def _rms(hv, g):
    r = lax.rsqrt(jnp.mean(hv * hv, axis=-1, keepdims=True) + EPS)
    return hv * r * g, r


def _rms_bwd(dn, hv, r, g):
    xh = hv * r
    dxh = dn * g
    return r * (dxh - xh * jnp.mean(dxh * xh, axis=-1, keepdims=True)), xh


def _acc_rows(ref, part, first):
    @pl.when(first)
    def _():
        ref[...] = part

    @pl.when(jnp.logical_not(first))
    def _():
        ref[...] += part


def _coords():
    return lax.axis_index("x"), lax.axis_index("y"), lax.axis_index("c")


def _flip(v, d):
    return 1 - v if d else v


def _run(local, remote):
    for cp in local + remote:
        cp.start()
    for cp in remote:
        cp.wait()
    for cp in local:
        cp.wait()


def _via_vmem(src, dst, stage, sems, i):
    return (pltpu.make_async_copy(src, stage, sems.at[2 * i]), pltpu.make_async_copy(stage, dst, sems.at[2 * i + 1]))


def _run_staged(staged, remote):
    for load, _ in staged:
        load.start()
    for cp in remote:
        cp.start()
    for load, store in staged:
        load.wait()
        store.start()
    for cp in remote:
        cp.wait()
    for _, store in staged:
        store.wait()


_REL3 = ((1, 0), (0, 1), (1, 1))


class _Behind:
    def __init__(self, arrays, out_shapes, scratch, build, alias_pairs=()):
        self.arrays, self.out_shapes, self.scratch, self.build = list(arrays), list(out_shapes), list(scratch), build
        self.alias_pairs = list(alias_pairs)

    def aliases(self):
        return self.alias_pairs

    def start(self, ins, outs, scr):
        staged, remote = self.build(ins, outs, scr)
        for load, _ in staged:
            load.start()
        for cp in remote:
            cp.start()

    def finish(self, ins, outs, scr):
        staged, remote = self.build(ins, outs, scr)
        for load, store in staged:
            load.wait()
            store.start()
        for cp in remote:
            cp.wait()
        for _, store in staged:
            store.wait()


def _call(body, comm, *, name, grid, in_specs, out_specs, out_shape, scratch_shapes=(), params):
    in_specs, out_specs, out_shape = list(in_specs), list(out_specs), list(out_shape)
    scratch_shapes = list(scratch_shapes)
    if comm is None:
        f = pl.pallas_call(body, name=name, grid=grid, in_specs=in_specs, out_specs=out_specs,
                           out_shape=out_shape, scratch_shapes=scratch_shapes, compiler_params=params)
        return lambda *args: (f(*args), [])
    ni, no, ns = len(in_specs), len(out_specs), len(scratch_shapes)
    ci, co = len(comm.arrays), len(comm.out_shapes)

    def hosted(*refs):
        ins, cin = refs[:ni], refs[ni:ni + ci]
        outs, cout = refs[ni + ci:ni + ci + no], refs[ni + ci + no:ni + ci + no + co]
        scr, cscr = refs[ni + ci + no + co:ni + ci + no + co + ns], refs[ni + ci + no + co + ns:]
        first = last = None
        for axis, size in enumerate(grid):
            i = pl.program_id(axis)
            first = (i == 0) if first is None else jnp.logical_and(first, i == 0)
            last = (i == size - 1) if last is None else jnp.logical_and(last, i == size - 1)

        @pl.when(first)
        def _():
            comm.start(cin, cout, cscr)

        body(*ins, *outs, *scr)

        @pl.when(last)
        def _():
            comm.finish(cin, cout, cscr)

    f = pl.pallas_call(hosted, name=name, grid=grid, in_specs=in_specs + [ANY] * ci,
                       out_specs=out_specs + [ANY] * co, out_shape=out_shape + comm.out_shapes,
                       scratch_shapes=scratch_shapes + comm.scratch,
                       input_output_aliases={ni + a: no + b for a, b in comm.aliases()},
                       compiler_params=_params(*(("arbitrary",) * len(grid))))

    def run(*args):
        res = f(*args, *comm.arrays)
        return res[:no], res[no:]

    return run


def _gather_half_behind(shards):
    n = len(shards)

    def build(ins, outs, scr):
        send, recv, loc = scr[:3]
        stage = scr[3:]
        x, y, c = _coords()
        me = 2 * x + y
        staged = [_via_vmem(ins[t], outs[t].at[me], stage[t], loc, t) for t in range(n)]
        remote = []
        for t in range(n):
            half = shards[t].shape[0] // 2
            mine = pl.ds(c * half, half)
            for k, (dx, dy) in enumerate(_REL3):
                remote.append(pltpu.make_async_remote_copy(
                    src_ref=ins[t].at[mine], dst_ref=outs[t].at[me, mine],
                    send_sem=send.at[3 * t + k], recv_sem=recv.at[3 * t + k],
                    device_id=(_flip(x, dx), _flip(y, dy), c), device_id_type=MESH))
        return staged, remote

    return _Behind(shards, [jax.ShapeDtypeStruct((NSH,) + s.shape, s.dtype) for s in shards],
                   [pltpu.SemaphoreType.DMA((3 * n,)), pltpu.SemaphoreType.DMA((3 * n,)),
                    pltpu.SemaphoreType.DMA((2 * n,))] + [pltpu.VMEM(s.shape, s.dtype) for s in shards], build)


def _pass_halves(gathered, name, own=()):
    n, m = len(gathered), len(own)

    def body(*refs):
        shards, outs = refs[n:n + m], refs[n + m:2 * n + m]
        send, recv, loc = refs[2 * n + m:2 * n + m + 3]
        stage = refs[2 * n + m + 3:]
        x, y, c = _coords()
        staged = [_via_vmem(shards[t], outs[t].at[2 * x + y], stage[t], loc, t) for t in range(m)]
        remote = []
        for t in range(n):
            half = gathered[t].shape[1] // 2
            mine = pl.ds(c * half, half)
            for k, (dx, dy) in enumerate(_REL3):
                slot = 2 * _flip(x, dx) + _flip(y, dy)
                remote.append(pltpu.make_async_remote_copy(
                    src_ref=outs[t].at[slot, mine], dst_ref=outs[t].at[slot, mine],
                    send_sem=send.at[3 * t + k], recv_sem=recv.at[3 * t + k],
                    device_id=(x, y, 1 - c), device_id_type=MESH))
        _run_staged(staged, remote)

    return pl.pallas_call(
        body, name=name,
        out_shape=[jax.ShapeDtypeStruct(g.shape, g.dtype) for g in gathered],
        in_specs=[ANY] * (n + m), out_specs=[ANY] * n, input_output_aliases={t: t for t in range(n)},
        scratch_shapes=[pltpu.SemaphoreType.DMA((3 * n,)), pltpu.SemaphoreType.DMA((3 * n,)),
                        pltpu.SemaphoreType.DMA((max(2 * m, 1),))] + [pltpu.VMEM(s.shape, s.dtype) for s in own],
        compiler_params=pltpu.CompilerParams(vmem_limit_bytes=VMEM_LIMIT),
    )(*gathered, *own)


def _fill_own(sums, recvs, name):
    n = len(sums)

    def body(*refs):
        ins, outs = refs[:n], refs[2 * n:3 * n]
        loc = refs[3 * n]
        stage = refs[3 * n + 1:]
        x, y, _ = _coords()
        me = 2 * x + y
        _run_staged([_via_vmem(ins[t].at[me], outs[t].at[me], stage[t], loc, t) for t in range(n)], [])

    return pl.pallas_call(
        body, name=name,
        out_shape=[jax.ShapeDtypeStruct(r.shape, r.dtype) for r in recvs],
        in_specs=[ANY] * (2 * n), out_specs=[ANY] * n, input_output_aliases={n + t: t for t in range(n)},
        scratch_shapes=[pltpu.SemaphoreType.DMA((2 * n,))] + [pltpu.VMEM(s.shape[1:], s.dtype) for s in sums],
        compiler_params=pltpu.CompilerParams(vmem_limit_bytes=VMEM_LIMIT),
    )(*sums, *recvs)


def _scatter_chips_behind(sums):
    n = len(sums)

    def build(ins, outs, scr):
        send, recv, loc = scr[:3]
        stage = scr[3:]
        x, y, c = _coords()
        me = 2 * x + y
        staged = [_via_vmem(ins[t].at[me], outs[t].at[me], stage[t], loc, t) for t in range(n)]
        remote = []
        for t in range(n):
            for k, (dx, dy) in enumerate(_REL3):
                px, py = _flip(x, dx), _flip(y, dy)
                remote.append(pltpu.make_async_remote_copy(
                    src_ref=ins[t].at[2 * px + py], dst_ref=outs[t].at[me],
                    send_sem=send.at[3 * t + k], recv_sem=recv.at[3 * t + k],
                    device_id=(px, py, c), device_id_type=MESH))
        return staged, remote

    return _Behind(sums, [jax.ShapeDtypeStruct(s.shape, s.dtype) for s in sums],
                   [pltpu.SemaphoreType.DMA((3 * n,)), pltpu.SemaphoreType.DMA((3 * n,)),
                    pltpu.SemaphoreType.DMA((2 * n,))] + [pltpu.VMEM(s.shape[1:], s.dtype) for s in sums], build)


def _gather_all_behind(a):
    def build(ins, outs, scr):
        send, recv, loc, stage = scr
        x, y, c = _coords()
        me = 4 * x + 2 * y + c
        staged = [_via_vmem(ins[0], outs[0].at[me], stage, loc, 0)]
        remote = [pltpu.make_async_remote_copy(
            src_ref=ins[0], dst_ref=outs[0].at[me], send_sem=send.at[k], recv_sem=recv.at[k],
            device_id=(_flip(x, dx), _flip(y, dy), _flip(c, dc)), device_id_type=MESH)
            for k, (dx, dy, dc) in enumerate(_REL7)]
        return staged, remote

    return _Behind([a], [jax.ShapeDtypeStruct((8,) + a.shape, a.dtype)],
                   [pltpu.SemaphoreType.DMA((7,)), pltpu.SemaphoreType.DMA((7,)), pltpu.SemaphoreType.DMA((2,)),
                    pltpu.VMEM(a.shape, a.dtype)], build)


HBM = pl.BlockSpec(memory_space=pltpu.HBM)
SEM = pl.BlockSpec(memory_space=pltpu.SEMAPHORE)
EFFECT = pltpu.SideEffectType.DATAFLOW_SIDE_EFFECTING


def _chip_copies(srcs, lands, send, recv, gather):
    x, y, c = _coords()
    me = 2 * x + y
    cps = []
    for t in range(len(srcs)):
        for k, (dx, dy) in enumerate(_REL3):
            px, py = _flip(x, dx), _flip(y, dy)
            if gather:
                half = srcs[t].shape[0] // 2
                mine = pl.ds(c * half, half)
                src, dst = srcs[t].at[mine], lands[t].at[me, mine]
            else:
                src, dst = srcs[t].at[2 * px + py], lands[t].at[me]
            cps.append(pltpu.make_async_remote_copy(
                src_ref=src, dst_ref=dst, send_sem=send.at[3 * t + k], recv_sem=recv.at[3 * t + k],
                device_id=(px, py, c), device_id_type=MESH))
    return cps


def _chips_start(arrays, land_shapes, gather, name, after=()):
    n = len(arrays)

    def body(*refs):
        srcs, lands = refs[:n], refs[n:2 * n]
        send, recv = refs[2 * n + len(after)], refs[2 * n + len(after) + 1]
        token = refs[-1]
        for cp in _chip_copies(srcs, lands, send, recv, gather):
            cp.start()
        token[...] = jnp.zeros_like(token)

    lands = [lax.empty(s.shape, s.dtype) for s in land_shapes]
    thru = [pltpu.HBM(a.shape, a.dtype) for a in arrays] + [pltpu.HBM(s.shape, s.dtype) for s in land_shapes]
    res = pl.pallas_call(
        body, name=name,
        out_shape=(pltpu.SemaphoreType.DMA((3 * n,)), pltpu.SemaphoreType.DMA((3 * n,)), *thru,
                   jax.ShapeDtypeStruct((8, 128), F32)),
        in_specs=[HBM] * (2 * n) + [ANY] * len(after),
        out_specs=(SEM, SEM, *([HBM] * (2 * n)), pl.BlockSpec(memory_space=pltpu.VMEM)),
        input_output_aliases={t: 2 + t for t in range(2 * n)},
        compiler_params=pltpu.CompilerParams(has_side_effects=EFFECT),
    )(*[pltpu.with_memory_space_constraint(a, pltpu.HBM) for a in arrays],
      *[pltpu.with_memory_space_constraint(z, pltpu.HBM) for z in lands], *after)
    return res[0], res[1], list(res[2:2 + n]), list(res[2 + n:2 + 2 * n]), res[-1]


def _chips_wait(send, recv, arrays, lands, after, gather, name):
    n = len(arrays)

    def body(*refs):
        srcs, ls = refs[:n], refs[n:2 * n]
        sd, rv = refs[2 * n], refs[2 * n + 1]
        for cp in _chip_copies(srcs, ls, sd, rv, gather):
            cp.wait_send()
            cp.wait_recv()

    res = pl.pallas_call(
        body, name=name,
        out_shape=[pltpu.HBM(a.shape, a.dtype) for a in arrays] + [pltpu.HBM(z.shape, z.dtype) for z in lands],
        in_specs=[HBM] * (2 * n) + [SEM, SEM] + [ANY] * len(after), out_specs=[HBM] * (2 * n),
        input_output_aliases={t: t for t in range(2 * n)},
        compiler_params=pltpu.CompilerParams(has_side_effects=EFFECT),
    )(*arrays, *lands, send, recv, *after)
    return list(res[:n]), list(res[n:])


def _join(*parts):
    def cut(seq, key):
        res, o = [], 0
        for p in parts:
            k = len(getattr(p, key))
            res.append(seq[o:o + k])
            o += k
        return res

    def build(ins, outs, scr):
        staged, remote = [], []
        for p, i, o, s in zip(parts, cut(ins, "arrays"), cut(outs, "out_shapes"), cut(scr, "scratch")):
            st, rm = p.build(i, o, s)
            staged += st
            remote += rm
        return staged, remote

    pairs, ai, oi = [], 0, 0
    for p in parts:
        pairs += [(ai + a, oi + b) for a, b in p.alias_pairs]
        ai, oi = ai + len(p.arrays), oi + len(p.out_shapes)
    return _Behind(sum((p.arrays for p in parts), []), sum((p.out_shapes for p in parts), []),
                   sum((p.scratch for p in parts), []), build, pairs)


def _pair_exchange_behind(grads):
    n = len(grads)

    def build(ins, outs, scr):
        send, recv = scr
        x, y, c = _coords()
        remote = []
        for t in range(n):
            half = grads[t].shape[1] // 2
            remote.append(pltpu.make_async_remote_copy(
                src_ref=ins[t].at[:, pl.ds((1 - c) * half, half)], dst_ref=outs[t],
                send_sem=send.at[t], recv_sem=recv.at[t],
                device_id=(x, y, 1 - c), device_id_type=MESH))
        return [], remote

    return _Behind(grads, [jax.ShapeDtypeStruct((NSH, g.shape[1] // 2, g.shape[2]), g.dtype) for g in grads],
                   [pltpu.SemaphoreType.DMA((n,)), pltpu.SemaphoreType.DMA((n,))], build)


def _pass_halves_behind(gathered):
    n = len(gathered)

    def build(ins, outs, scr):
        send, recv = scr
        x, y, c = _coords()
        remote = []
        for t in range(n):
            half = gathered[t].shape[1] // 2
            mine = pl.ds(c * half, half)
            for k, (dx, dy) in enumerate(_REL3):
                slot = 2 * _flip(x, dx) + _flip(y, dy)
                remote.append(pltpu.make_async_remote_copy(
                    src_ref=outs[t].at[slot, mine], dst_ref=outs[t].at[slot, mine],
                    send_sem=send.at[3 * t + k], recv_sem=recv.at[3 * t + k],
                    device_id=(x, y, 1 - c), device_id_type=MESH))
        return [], remote

    return _Behind(gathered, [jax.ShapeDtypeStruct(g.shape, g.dtype) for g in gathered],
                   [pltpu.SemaphoreType.DMA((3 * n,)), pltpu.SemaphoreType.DMA((3 * n,))], build,
                   [(t, t) for t in range(n)])


_REL7 = tuple((dx, dy, dc) for dx in (0, 1) for dy in (0, 1) for dc in (0, 1))[1:]


def _gather_all(a, name):
    def body(a_ref, o_ref, send, recv, loc):
        x, y, c = _coords()
        me = 4 * x + 2 * y + c
        local = [pltpu.make_async_copy(a_ref, o_ref.at[me], loc.at[0])]
        remote = [pltpu.make_async_remote_copy(
            src_ref=a_ref, dst_ref=o_ref.at[me], send_sem=send.at[k], recv_sem=recv.at[k],
            device_id=(_flip(x, dx), _flip(y, dy), _flip(c, dc)), device_id_type=MESH)
            for k, (dx, dy, dc) in enumerate(_REL7)]
        _run(local, remote)

    return pl.pallas_call(
        body, name=name,
        out_shape=jax.ShapeDtypeStruct((8,) + a.shape, a.dtype),
        in_specs=[ANY], out_specs=ANY,
        scratch_shapes=[pltpu.SemaphoreType.DMA((7,)), pltpu.SemaphoreType.DMA((7,)),
                        pltpu.SemaphoreType.DMA((1,))],
    )(a)


def _pair_exchange(grads, name):
    n = len(grads)

    def body(*refs):
        ins, outs = refs[:n], refs[n:2 * n]
        send, recv = refs[2 * n:]
        x, y, c = _coords()
        remote = []
        for t in range(n):
            half = grads[t].shape[1] // 2
            remote.append(pltpu.make_async_remote_copy(
                src_ref=ins[t].at[:, pl.ds((1 - c) * half, half)], dst_ref=outs[t],
                send_sem=send.at[t], recv_sem=recv.at[t],
                device_id=(x, y, 1 - c), device_id_type=MESH))
        _run([], remote)

    return pl.pallas_call(
        body, name=name,
        out_shape=[jax.ShapeDtypeStruct((NSH, g.shape[1] // 2, g.shape[2]), g.dtype) for g in grads],
        in_specs=[ANY] * n, out_specs=[ANY] * n,
        scratch_shapes=[pltpu.SemaphoreType.DMA((n,)), pltpu.SemaphoreType.DMA((n,))],
    )(*grads)


def _swap_halves(halves, name):
    n = len(halves)

    def body(*refs):
        ins, outs = refs[:n], refs[n:2 * n]
        send, recv, loc = refs[2 * n:2 * n + 3]
        stage = refs[2 * n + 3:]
        x, y, c = _coords()
        local = [_via_vmem(ins[t], outs[t].at[c], stage[t], loc, t) for t in range(n)]
        remote = [pltpu.make_async_remote_copy(
            src_ref=ins[t], dst_ref=outs[t].at[c], send_sem=send.at[t], recv_sem=recv.at[t],
            device_id=(x, y, 1 - c), device_id_type=MESH) for t in range(n)]
        _run_staged(local, remote)

    return pl.pallas_call(
        body, name=name,
        out_shape=[jax.ShapeDtypeStruct((2,) + h.shape, h.dtype) for h in halves],
        in_specs=[ANY] * n, out_specs=[ANY] * n,
        scratch_shapes=[pltpu.SemaphoreType.DMA((n,)), pltpu.SemaphoreType.DMA((n,)),
                        pltpu.SemaphoreType.DMA((2 * n,))]
        + [pltpu.VMEM(h.shape, h.dtype) for h in halves],
        compiler_params=pltpu.CompilerParams(vmem_limit_bytes=VMEM_LIMIT),
    )(*halves)


def _by_shape(arrays):
    groups = {}
    for i, a in enumerate(arrays):
        groups.setdefault((a.shape, a.dtype), []).append(i)
    return list(groups.values())


def _sum_slots(rs, name, after=None):
    n = len(rs)
    K, R, C = rs[0].shape
    tr = _tile(R, max(16, (1 << 22) // (n * K * C)), 8 * (4 // rs[0].dtype.itemsize))

    def body(*refs):
        for r_ref, o_ref in zip(refs[:n], refs[len(refs) - n:]):
            acc = r_ref[0].astype(F32)
            for k in range(1, K):
                acc = acc + r_ref[k].astype(F32)
            o_ref[...] = acc

    dep = [] if after is None else [after]
    return pl.pallas_call(
        body, name=name, grid=(R // tr,),
        out_shape=[jax.ShapeDtypeStruct((R, C), F32)] * n,
        in_specs=[pl.BlockSpec((K, tr, C), lambda i: (0, i, 0))] * n + [ANY] * len(dep),
        out_specs=[pl.BlockSpec((tr, C), lambda i: (i, 0))] * n,
        compiler_params=_params("parallel"),
    )(*rs, *dep)


def _add_pair(gs, ss, core, name):
    n = len(gs)
    _, half, C = ss[0].shape
    tr = _tile(half, max(16, (1 << 21) // (n * C)))
    nb = half // tr

    def body(c_ref, *refs):
        for g_ref, s_ref, o_ref in zip(refs[:n], refs[n:2 * n], refs[2 * n:]):
            o_ref[...] = (g_ref[...].astype(F32) + s_ref[...].astype(F32)).astype(BF16)

    spec = pl.BlockSpec((1, tr, C), lambda j, i, c_ref: (j, i, 0))
    return pl.pallas_call(
        body, name=name,
        grid_spec=pltpu.PrefetchScalarGridSpec(
            num_scalar_prefetch=1, grid=(NSH, nb),
            in_specs=[pl.BlockSpec((1, tr, C), lambda j, i, c_ref: (j, c_ref[0] * nb + i, 0))] * n + [spec] * n,
            out_specs=[spec] * n),
        out_shape=[jax.ShapeDtypeStruct(ss[0].shape, BF16)] * n,
        compiler_params=_params("parallel", "parallel"),
    )(core, *gs, *ss)


def _adamw(wgmv, name):
    n = len(wgmv)
    _, R, C = wgmv[0][0].shape
    tr = _tile(R, max(8, (1 << 18) // (n * C)), 8)
    c1 = 1.0 / (1.0 - B1 ** STEP)
    c2 = 1.0 / (1.0 - B2 ** STEP)

    def body(*refs):
        for t in range(n):
            w_ref, g_ref, m_ref, v_ref = refs[4 * t:4 * t + 4]
            go_ref, d_ref, nm_ref, nv_ref = refs[4 * n + 4 * t:4 * n + 4 * t + 4]
            gv = g_ref[...]
            go_ref[...] = gv
            nm = B1 * m_ref[...] + (1.0 - B1) * gv
            nv = B2 * v_ref[...] + (1.0 - B2) * gv * gv
            nm_ref[...] = nm
            nv_ref[...] = nv
            d_ref[...] = -LR * ((nm * c1) / (jnp.sqrt(nv * c2) + AEPS) + WD * w_ref[...])

    spec = pl.BlockSpec((1, tr, C), lambda i: (0, i, 0))
    res = pl.pallas_call(
        body, name=name, grid=(R // tr,),
        out_shape=[jax.ShapeDtypeStruct((1, R, C), F32)] * (4 * n),
        in_specs=[spec] * (4 * n), out_specs=[spec] * (4 * n),
        compiler_params=_params("parallel"),
    )(*[a for four in wgmv for a in four])
    return [tuple(res[4 * t:4 * t + 4]) for t in range(n)]


def _ffn_fwd(h, g, w1, w3, w2, name, comm=None):
    L = h.shape[0]
    tm = _tile(L, 704)

    def body(h_ref, g_ref, w1_ref, w3_ref, w2_ref, o_ref, a_ref, b_ref, n_s, acc_s):
        j = pl.program_id(1)

        @pl.when(j == 0)
        def _():
            hv = h_ref[...]
            n, _ = _rms(hv, g_ref[...])
            n_s[...] = n.astype(BF16)
            acc_s[...] = hv

        n = n_s[...]
        a = _dot_nt(n, w1_ref[0])
        b = _dot_nt(n, w3_ref[0])
        a_ref[0] = a.astype(BF16)
        b_ref[0] = b.astype(BF16)
        s = (a * _sigmoid(a) * b).astype(BF16)
        acc_s[...] += 0.5 * _dot(s, w2_ref[0])

        @pl.when(j == NSH - 1)
        def _():
            o_ref[...] = acc_s[...]

    row = pl.BlockSpec((tm, D), lambda i, j: (i, 0))
    hid = pl.BlockSpec((1, tm, FS), lambda i, j: (j, i, 0))
    wsp = pl.BlockSpec((1, FS, D), lambda i, j: (j, 0, 0))
    return _call(
        body, comm, name=name, grid=(L // tm, NSH),
        out_shape=[jax.ShapeDtypeStruct((L, D), F32),
                   jax.ShapeDtypeStruct((NSH, L, FS), BF16), jax.ShapeDtypeStruct((NSH, L, FS), BF16)],
        in_specs=[row, _res((1, D)), wsp, wsp, wsp],
        out_specs=[row, hid, hid],
        scratch_shapes=[pltpu.VMEM((tm, D), BF16), pltpu.VMEM((tm, D), F32)],
        params=_params("arbitrary", "arbitrary"),
    )(h, g, w1, w3, w2)


def _loss_head(hv, gv, tv, row0):
    y, r = _rms(hv, gv)
    row = row0 + lax.broadcasted_iota(jnp.int32, (hv.shape[0], 1), 0)
    e = jnp.where(row >= FRONT, y - tv, 0.0)
    dy = e * (1.0 / D)
    part = 0.5 * jnp.sum(jnp.sum(e * dy, axis=1, keepdims=True), axis=0, keepdims=True)
    dx, xh = _rms_bwd(dy, hv, r, gv)
    return dx, part, jnp.sum(dy * xh, axis=0, keepdims=True)


def _ffn_fwd_loss(h, g, w1, w3, w2, gf, tgt, name):
    L = h.shape[0]
    tm = _tile(L, 704)

    def body(h_ref, g_ref, w1_ref, w3_ref, w2_ref, gf_ref, t_ref, o_ref, a_ref, b_ref, loss_ref, dgf_ref,
             n_s, acc_s):
        i, j = pl.program_id(0), pl.program_id(1)

        @pl.when(j == 0)
        def _():
            hv = h_ref[...]
            n, _ = _rms(hv, g_ref[...])
            n_s[...] = n.astype(BF16)
            acc_s[...] = hv

        n = n_s[...]
        a = _dot_nt(n, w1_ref[0])
        b = _dot_nt(n, w3_ref[0])
        a_ref[0] = a.astype(BF16)
        b_ref[0] = b.astype(BF16)
        s = (a * _sigmoid(a) * b).astype(BF16)
        acc_s[...] += 0.5 * _dot(s, w2_ref[0])

        @pl.when(j == NSH - 1)
        def _():
            dx, part, dgf = _loss_head(acc_s[...], gf_ref[...], t_ref[...], i * tm)
            o_ref[...] = dx
            _acc_rows(loss_ref, part, i == 0)
            _acc_rows(dgf_ref, dgf, i == 0)

    row = pl.BlockSpec((tm, D), lambda i, j: (i, 0))
    hid = pl.BlockSpec((1, tm, FS), lambda i, j: (j, i, 0))
    wsp = pl.BlockSpec((1, FS, D), lambda i, j: (j, 0, 0))
    return pl.pallas_call(
        body, name=name, grid=(L // tm, NSH),
        out_shape=[jax.ShapeDtypeStruct((L, D), F32),
                   jax.ShapeDtypeStruct((NSH, L, FS), BF16), jax.ShapeDtypeStruct((NSH, L, FS), BF16),
                   jax.ShapeDtypeStruct((1, 1), F32), jax.ShapeDtypeStruct((1, D), F32)],
        in_specs=[row, _res((1, D)), wsp, wsp, wsp, _res((1, D)), row],
        out_specs=[row, hid, hid, pl.BlockSpec((1, 1), lambda i, j: (0, 0)),
                   pl.BlockSpec((1, D), lambda i, j: (0, 0))],
        scratch_shapes=[pltpu.VMEM((tm, D), BF16), pltpu.VMEM((tm, D), F32)],
        compiler_params=_params("arbitrary", "arbitrary"),
    )(h, g, w1, w3, w2, gf, tgt)


def _ffn_bwd(h, g, dout, a, b, w1, w3, w2, name, comm=None):
    L = h.shape[0]
    tm = _tile(L, 528)

    def body(h_ref, g_ref, do_ref, a_ref, b_ref, w1_ref, w3_ref, w2_ref,
             dh_ref, da_ref, db_ref, s_ref, n_ref, dg_ref, dob_s, dn_s):
        i, j = pl.program_id(0), pl.program_id(1)

        @pl.when(j == 0)
        def _():
            n, _ = _rms(h_ref[...], g_ref[...])
            n_ref[...] = n.astype(BF16)
            dob_s[...] = (0.5 * do_ref[...]).astype(BF16)
            dn_s[...] = jnp.zeros_like(dn_s)

        av = a_ref[0].astype(F32)
        bv = b_ref[0].astype(F32)
        sig = _sigmoid(av)
        sa = av * sig
        ds = _dot_nt(dob_s[...], w2_ref[0])
        s_ref[0] = (sa * bv).astype(BF16)
        da = (ds * bv * (sig + sa * (1.0 - sig))).astype(BF16)
        db = (ds * sa).astype(BF16)
        da_ref[0] = da
        db_ref[0] = db
        dn_s[...] += _dot(da, w1_ref[0]) + _dot(db, w3_ref[0])

        @pl.when(j == NSH - 1)
        def _():
            hv = h_ref[...]
            gv = g_ref[...]
            r = lax.rsqrt(jnp.mean(hv * hv, axis=-1, keepdims=True) + EPS)
            dn = dn_s[...]
            dx, xh = _rms_bwd(dn, hv, r, gv)
            dh_ref[...] = do_ref[...] + dx
            _acc_rows(dg_ref, jnp.sum(dn * xh, axis=0, keepdims=True), i == 0)

    row = pl.BlockSpec((tm, D), lambda i, j: (i, 0))
    hid = pl.BlockSpec((1, tm, FS), lambda i, j: (j, i, 0))
    wsp = pl.BlockSpec((1, FS, D), lambda i, j: (j, 0, 0))
    return _call(
        body, comm, name=name, grid=(L // tm, NSH),
        out_shape=[jax.ShapeDtypeStruct((L, D), F32)]
        + [jax.ShapeDtypeStruct((NSH, L, FS), BF16)] * 3
        + [jax.ShapeDtypeStruct((L, D), BF16), jax.ShapeDtypeStruct((1, D), F32)],
        in_specs=[row, _res((1, D)), row, hid, hid,
                  wsp, wsp, wsp],
        out_specs=[row, hid, hid, hid, row, pl.BlockSpec((1, D), lambda i, j: (0, 0))],
        scratch_shapes=[pltpu.VMEM((tm, D), BF16), pltpu.VMEM((tm, D), F32)],
        params=_params("arbitrary", "arbitrary"),
    )(h, g, dout, a, b, w1, w3, w2)


def _wgrad(xm, ym, name, scale=1.0):
    xs, ys = xm.ndim == 3, ym.ndim == 3
    assert not (xs and ys)
    L = xm.shape[-2]
    K, N = xm.shape[-1], ym.shape[-1]
    tl = _tile(L, 2112)
    nl = L // tl
    if xs or ys:
        tn, grid_n = N, NSH
    else:
        tn = _tile(N, 1024, 128)
        grid_n = N // tn

    def body(x_ref, y_ref, o_ref, acc_s):
        l = pl.program_id(1)
        xv = x_ref[0] if xs else x_ref[...]
        yv = y_ref[0] if ys else y_ref[...]
        part = _dot_tn(xv.astype(BF16), yv.astype(BF16))
        _acc_rows(acc_s, part, l == 0)

        @pl.when(l == nl - 1)
        def _():
            res = (acc_s[...] * scale).astype(BF16)
            if xs or ys:
                o_ref[0] = res
            else:
                o_ref[...] = res

    if xs:
        x_spec = pl.BlockSpec((1, tl, K), lambda n, l: (n, l, 0))
        y_spec = pl.BlockSpec((tl, N), lambda n, l: (l, 0))
        o_spec = pl.BlockSpec((1, K, N), lambda n, l: (n, 0, 0))
        o_shape = (NSH, K, N)
    elif ys:
        x_spec = pl.BlockSpec((tl, K), lambda n, l: (l, 0))
        y_spec = pl.BlockSpec((1, tl, N), lambda n, l: (n, l, 0))
        o_spec = pl.BlockSpec((1, K, N), lambda n, l: (n, 0, 0))
        o_shape = (NSH, K, N)
    else:
        x_spec = pl.BlockSpec((tl, K), lambda n, l: (l, 0))
        y_spec = pl.BlockSpec((tl, tn), lambda n, l: (l, n))
        o_spec = pl.BlockSpec((K, tn), lambda n, l: (0, n))
        o_shape = (K, N)
    return pl.pallas_call(
        body, name=name, grid=(grid_n, nl),
        out_shape=jax.ShapeDtypeStruct(o_shape, BF16),
        in_specs=[x_spec, y_spec], out_specs=o_spec,
        scratch_shapes=[pltpu.VMEM((K, tn), F32)],
        compiler_params=_params("parallel", "arbitrary"),
    )(xm, ym)


def _mix_in_fwd(h, g, w_in, b_gate, name, comm=None):
    L = h.shape[0]
    tm = _tile(L, 528)

    def body(h_ref, g_ref, w_ref, bg_ref, vg_ref, uf_ref, gt_ref):
        u, _ = _rms(h_ref[...], g_ref[...])
        ub = u.astype(BF16)
        p = [_dot(ub, w_ref[j]) for j in range(NSH)]
        a0, a1 = 2 * DC - WS, 2 * DC + DS - WS
        vg_ref[:, 0:WS] = p[0].astype(BF16)
        vg_ref[:, WS:2 * DC] = p[1][:, 0:a0].astype(BF16)
        uf_ref[...] = p[1][:, a0:a1].astype(BF16)
        gin = jnp.concatenate([p[1][:, a1:], p[2], p[3]], axis=1)
        gt_ref[...] = _sigmoid(gin + bg_ref[...]).astype(BF16)

    def row(n):
        return pl.BlockSpec((tm, n), lambda i: (i, 0))

    return _call(
        body, comm, name=name, grid=(L // tm,),
        out_shape=[jax.ShapeDtypeStruct((L, 2 * DC), BF16), jax.ShapeDtypeStruct((L, DS), BF16),
                   jax.ShapeDtypeStruct((L, 2 * D), BF16)],
        in_specs=[row(D), _res((1, D)), _res((NSH, D, WS)), _res((1, 2 * D))],
        out_specs=[row(2 * DC), row(DS), row(2 * D)],
        params=_params("parallel"),
    )(h, g, w_in, b_gate)


def _mix_in_bwd(h, g, dres, dv, dgl, duf, dgate, w_in, name):
    L = h.shape[0]
    tm = _tile(L, 528)

    def body(h_ref, g_ref, dr_ref, dv_ref, dgl_ref, duf_ref, dgt_ref, w_ref, dh_ref, u_ref, dp_ref, dgm_ref):
        i = pl.program_id(0)
        hv = h_ref[...]
        gv = g_ref[...]
        u, r = _rms(hv, gv)
        u_ref[...] = u.astype(BF16)
        a0, a1 = 2 * DC - WS, 2 * DC + DS - WS
        b0 = WS - a1
        dp = [jnp.concatenate([dv_ref[...], dgl_ref[:, 0:WS - DC]], axis=1),
              jnp.concatenate([dgl_ref[:, WS - DC:], duf_ref[...], dgt_ref[:, 0:b0]], axis=1),
              dgt_ref[:, b0:b0 + WS], dgt_ref[:, b0 + WS:]]
        du = jnp.zeros((tm, D), F32)
        for j in range(NSH):
            dp_ref[j] = dp[j]
            du = du + _dot_nt(dp[j], w_ref[j])
        dx, xh = _rms_bwd(du, hv, r, gv)
        dh_ref[...] = dr_ref[...] + dx
        _acc_rows(dgm_ref, jnp.sum(du * xh, axis=0, keepdims=True), i == 0)

    def row(n):
        return pl.BlockSpec((tm, n), lambda i: (i, 0))

    return pl.pallas_call(
        body, name=name, grid=(L // tm,),
        out_shape=[jax.ShapeDtypeStruct((L, D), F32), jax.ShapeDtypeStruct((L, D), BF16),
                   jax.ShapeDtypeStruct((NSH, L, WS), BF16), jax.ShapeDtypeStruct((1, D), F32)],
        in_specs=[row(D), _res((1, D)), row(D), row(DC), row(DC), row(DS), row(2 * D), _res((NSH, D, WS))],
        out_specs=[row(D), row(D), pl.BlockSpec((NSH, tm, WS), lambda i: (0, i, 0)),
                   pl.BlockSpec((1, D), lambda i: (0, 0))],
        compiler_params=_params("arbitrary"),
    )(h, g, dres, dv, dgl, duf, dgate, w_in)


def _conv_fwd(vg, dw, dwb, name, comm=None):
    L = vg.shape[0]
    nc = DC // 128

    def body(v_ref, g_ref, dw_ref, dwb_ref, z_ref, zp_s):
        zp_s[0:KWP, :] = jnp.zeros((KWP, 128), F32)
        zp_s[KWP:, :] = v_ref[...].astype(F32) * _sigmoid(g_ref[...].astype(F32))
        for r0 in range(0, L, CONV_ROWS):
            acc = jnp.broadcast_to(dwb_ref[...], (CONV_ROWS, 128))
            for k in range(KW):
                acc = acc + dw_ref[k:k + 1, :] * zp_s[pl.ds(r0 + k + 2, CONV_ROWS), :]
            z_ref[pl.ds(r0, CONV_ROWS), :] = acc

    return _call(
        body, comm, name=name, grid=(nc,),
        out_shape=[jax.ShapeDtypeStruct((L, DC), F32)],
        in_specs=[pl.BlockSpec((L, 128), lambda c: (0, c)), pl.BlockSpec((L, 128), lambda c: (0, nc + c)),
                  pl.BlockSpec((KWP, 128), lambda c: (0, c)), pl.BlockSpec((1, 128), lambda c: (0, c))],
        out_specs=[pl.BlockSpec((L, 128), lambda c: (0, c))],
        scratch_shapes=[pltpu.VMEM((L + KWP, 128), F32)],
        params=_params("parallel"),
    )(vg, vg, dw, dwb)


def _conv_bwd(dz1, vg, dw, name):
    L = vg.shape[0]
    nc = DC // 128

    def body(dz_ref, v_ref, g_ref, dw_ref, dv_ref, dg_ref, ddw_ref, ddwb_ref, zp_s, dzp_s):
        vv = v_ref[...].astype(F32)
        sg = _sigmoid(g_ref[...].astype(F32))
        zp_s[0:KWP, :] = jnp.zeros((KWP, 128), F32)
        zp_s[KWP:, :] = vv * sg
        dz = dz_ref[...]
        dzp_s[0:L, :] = dz
        dzp_s[L:, :] = jnp.zeros((KWP, 128), F32)
        ddwb_ref[...] = jnp.sum(dz, axis=0, keepdims=True)
        part = [jnp.zeros((8, 128), F32) for _ in range(KW)]
        for r0 in range(0, L, CONV_ROWS):
            rows = pl.ds(r0, CONV_ROWS)
            dzc = dz_ref[rows, :]
            acc = jnp.zeros((CONV_ROWS, 128), F32)
            for k in range(KW):
                acc = acc + dw_ref[k:k + 1, :] * dzp_s[pl.ds(r0 + KW - 1 - k, CONV_ROWS), :]
                prod = dzc * zp_s[pl.ds(r0 + k + 2, CONV_ROWS), :]
                for q in range(CONV_ROWS // 8):
                    part[k] = part[k] + prod[8 * q:8 * (q + 1), :]
            vc = v_ref[rows, :].astype(F32)
            sc = _sigmoid(g_ref[rows, :].astype(F32))
            dv_ref[rows, :] = (acc * sc).astype(BF16)
            dg_ref[rows, :] = (acc * vc * sc * (1.0 - sc)).astype(BF16)
        for k in range(KW):
            ddw_ref[k:k + 1, :] = jnp.sum(part[k], axis=0, keepdims=True)
        ddw_ref[KW:KWP, :] = jnp.zeros((KWP - KW, 128), F32)

    col = pl.BlockSpec((L, 128), lambda c: (0, c))
    return pl.pallas_call(
        body, name=name, grid=(nc,),
        out_shape=[jax.ShapeDtypeStruct((L, DC), BF16), jax.ShapeDtypeStruct((L, DC), BF16),
                   jax.ShapeDtypeStruct((KWP, DC), F32), jax.ShapeDtypeStruct((1, DC), F32)],
        in_specs=[col, col, pl.BlockSpec((L, 128), lambda c: (0, nc + c)),
                  pl.BlockSpec((KWP, 128), lambda c: (0, c))],
        out_specs=[col, col, pl.BlockSpec((KWP, 128), lambda c: (0, c)), pl.BlockSpec((1, 128), lambda c: (0, c))],
        scratch_shapes=[pltpu.VMEM((L + KWP, 128), F32), pltpu.VMEM((L + KWP, 128), F32)],
        compiler_params=_params("parallel"),
    )(dz1, vg, vg, dw)


NLB = QS // 128


def _lb_store(ref, rows, val):
    for cb in range(NLB):
        ref[cb, rows, :] = val[:, cb * 128:(cb + 1) * 128]


def _lb_load(ref, rows):
    return jnp.concatenate([ref[cb, rows, :] for cb in range(NLB)], axis=1)


def _scan(xr_ref, xi_ref, base, T, ar, ai, atr, ati, reverse):
    W = ar.shape[1]
    ar, ai, atr, ati = (jnp.broadcast_to(v, (8, W)) for v in (ar, ai, atr, ati))
    zero = jnp.zeros((8, W), F32)

    def rows(t, g):
        tt = T - 1 - t if reverse else t
        return pl.ds(base + g * 8 * T + tt, 8, stride=T)

    def make_step(store):
        def step(t, carry):
            out = []
            for g in range(NGRP):
                sr, si = carry[2 * g], carry[2 * g + 1]
                idx = rows(t, g)
                nr = ar * sr - ai * si + _lb_load(xr_ref, idx)
                ni = ar * si + ai * sr + _lb_load(xi_ref, idx)
                if store:
                    _lb_store(xr_ref, idx, nr)
                    _lb_store(xi_ref, idx, ni)
                out += [nr, ni]
            return tuple(out)
        return step

    ends = lax.fori_loop(0, T, make_step(False), (zero,) * (2 * NGRP))
    sub = lax.broadcasted_iota(jnp.int32, (8, W), 0)
    edge = sub == (7 if reverse else 0)
    shift, last = (7, 0) if reverse else (1, 7)
    inr, ini = jnp.zeros((1, W), F32), jnp.zeros((1, W), F32)
    starts = [None] * (2 * NGRP)
    for g in (reversed(range(NGRP)) if reverse else range(NGRP)):
        er, ei = ends[2 * g], ends[2 * g + 1]
        cr, ci = jnp.where(edge, inr, 0.0), jnp.where(edge, ini, 0.0)
        for _ in range(7):
            nr = atr * cr - ati * ci + er
            ni = atr * ci + ati * cr + ei
            cr = jnp.where(edge, inr, pltpu.roll(nr, shift, 0))
            ci = jnp.where(edge, ini, pltpu.roll(ni, shift, 0))
        starts[2 * g], starts[2 * g + 1] = cr, ci
        inr = (atr * cr - ati * ci + er)[last:last + 1]
        ini = (atr * ci + ati * cr + ei)[last:last + 1]
    lax.fori_loop(0, T, make_step(True), tuple(starts))


def _ssm_fwd(uf, bre, bim, cre, cim, lamp, dsk, name, comm=None):
    L = uf.shape[0]
    T = L // NSEG
    tc = L // NCH

    def body(u_ref, bre_ref, bim_ref, cre_ref, cim_ref, lam_ref, d_ref, y_ref, sr_s, si_s):
        for k in range(NCH):
            sl = slice(k * tc, (k + 1) * tc)
            uk = u_ref[sl, :]
            _lb_store(sr_s, sl, _dot(uk, bre_ref[0]))
            _lb_store(si_s, sl, _dot(uk, bim_ref[0]))
        _scan(sr_s, si_s, 0, T, lam_ref[0:1, :], lam_ref[1:2, :], lam_ref[2:3, :], lam_ref[3:4, :], False)
        for k in range(NCH):
            sl = slice(k * tc, (k + 1) * tc)
            y_ref[sl, :] = (_dot(_lb_load(sr_s, sl).astype(BF16), cre_ref[0])
                            - _dot(_lb_load(si_s, sl).astype(BF16), cim_ref[0])
                            + d_ref[...] * u_ref[sl, :].astype(F32))

    return _call(
        body, comm, name=name, grid=(NQ,),
        out_shape=[jax.ShapeDtypeStruct((L, DS), F32)],
        in_specs=[pl.BlockSpec((L, QU), lambda q: (0, q)),
                  pl.BlockSpec((1, QU, QS), lambda q: (q, 0, 0)), pl.BlockSpec((1, QU, QS), lambda q: (q, 0, 0)),
                  pl.BlockSpec((1, QS, QU), lambda q: (q, 0, 0)), pl.BlockSpec((1, QS, QU), lambda q: (q, 0, 0)),
                  pl.BlockSpec((8, QS), lambda q: (0, q)), pl.BlockSpec((1, QU), lambda q: (0, q))],
        out_specs=[pl.BlockSpec((L, QU), lambda q: (0, q))],
        scratch_shapes=[pltpu.VMEM((NLB, L, 128), F32), pltpu.VMEM((NLB, L, 128), F32)],
        params=_params("parallel"),
    )(uf, bre, bim, cre, cim, lamp, dsk)


def _ssm_bwd(uf, dyss, bre, bim, cre, cim, lamp, dsk, name, comm=None):
    L = uf.shape[0]
    T = L // NSEG
    tc = L // NCH

    def body(u_ref, dy_ref, bre_ref, bim_ref, cre_ref, cim_ref, lam_ref, d_ref,
             du_ref, dbre_ref, dbim_ref, dcre_ref, dcim_ref, dlam_ref, dd_ref, sr_s, si_s, gr_s, gi_s):
        _lb_store(sr_s, slice(0, SOFF), jnp.zeros((SOFF, QS), F32))
        _lb_store(si_s, slice(0, SOFF), jnp.zeros((SOFF, QS), F32))
        for k in range(NCH):
            sl = slice(k * tc, (k + 1) * tc)
            ss = slice(SOFF + k * tc, SOFF + (k + 1) * tc)
            uk = u_ref[sl, :]
            dyk = dy_ref[sl, :].astype(BF16)
            _lb_store(sr_s, ss, _dot(uk, bre_ref[0]))
            _lb_store(si_s, ss, _dot(uk, bim_ref[0]))
            _lb_store(gr_s, sl, _dot_nt(dyk, cre_ref[0]))
            _lb_store(gi_s, sl, -_dot_nt(dyk, cim_ref[0]))
        ar, ai, atr, ati = lam_ref[0:1, :], lam_ref[1:2, :], lam_ref[2:3, :], lam_ref[3:4, :]
        _scan(sr_s, si_s, SOFF, T, ar, ai, atr, ati, False)
        _scan(gr_s, gi_s, 0, T, ar, -ai, atr, -ati, True)
        dbre = jnp.zeros((QU, QS), F32)
        dbim = jnp.zeros((QU, QS), F32)
        dcre = jnp.zeros((QS, QU), F32)
        dcim = jnp.zeros((QS, QU), F32)
        dd = jnp.zeros((1, QU), F32)
        qr = jnp.zeros((1, QS), F32)
        qi = jnp.zeros((1, QS), F32)
        for k in range(NCH):
            sl = slice(k * tc, (k + 1) * tc)
            ss = slice(SOFF + k * tc, SOFF + (k + 1) * tc)
            sp = slice(SOFF - 1 + k * tc, SOFF - 1 + (k + 1) * tc)
            uk = u_ref[sl, :]
            dyk = dy_ref[sl, :]
            dyb = dyk.astype(BF16)
            gr, gi = _lb_load(gr_s, sl), _lb_load(gi_s, sl)
            pr, pi = _lb_load(sr_s, sp), _lb_load(si_s, sp)
            qr = qr + jnp.sum(gr * pr + gi * pi, axis=0, keepdims=True)
            qi = qi + jnp.sum(gi * pr - gr * pi, axis=0, keepdims=True)
            grb, gib = gr.astype(BF16), gi.astype(BF16)
            du_ref[sl, :] = (_dot_nt(grb, bre_ref[0]) + _dot_nt(gib, bim_ref[0])
                             + dyk * d_ref[...]).astype(BF16)
            dbre = dbre + _dot_tn(uk, grb)
            dbim = dbim + _dot_tn(uk, gib)
            dcre = dcre + _dot_tn(_lb_load(sr_s, ss).astype(BF16), dyb)
            dcim = dcim - _dot_tn(_lb_load(si_s, ss).astype(BF16), dyb)
            dd = dd + jnp.sum(dyk * uk.astype(F32), axis=0, keepdims=True)
        dlam_ref[0] = jnp.concatenate([qr, qi, jnp.zeros((6, QS), F32)], axis=0)
        dbre_ref[0] = dbre
        dbim_ref[0] = dbim
        dcre_ref[0] = dcre
        dcim_ref[0] = dcim
        dd_ref[...] = dd

    col = pl.BlockSpec((L, QU), lambda q: (0, q))
    bsp = pl.BlockSpec((1, QU, QS), lambda q: (q, 0, 0))
    csp = pl.BlockSpec((1, QS, QU), lambda q: (q, 0, 0))
    return _call(
        body, comm, name=name, grid=(NQ,),
        out_shape=[jax.ShapeDtypeStruct((L, DS), BF16),
                   jax.ShapeDtypeStruct((NQ, QU, QS), F32), jax.ShapeDtypeStruct((NQ, QU, QS), F32),
                   jax.ShapeDtypeStruct((NQ, QS, QU), F32), jax.ShapeDtypeStruct((NQ, QS, QU), F32),
                   jax.ShapeDtypeStruct((NQ, 8, QS), F32), jax.ShapeDtypeStruct((1, DS), F32)],
        in_specs=[col, col, bsp, bsp, csp, csp,
                  pl.BlockSpec((8, QS), lambda q: (0, q)), pl.BlockSpec((1, QU), lambda q: (0, q))],
        out_specs=[col,
                   pl.BlockSpec((1, QU, QS), lambda q: (q, 0, 0)), pl.BlockSpec((1, QU, QS), lambda q: (q, 0, 0)),
                   pl.BlockSpec((1, QS, QU), lambda q: (q, 0, 0)), pl.BlockSpec((1, QS, QU), lambda q: (q, 0, 0)),
                   pl.BlockSpec((1, 8, QS), lambda q: (q, 0, 0)), pl.BlockSpec((1, QU), lambda q: (0, q))],
        scratch_shapes=[pltpu.VMEM((NLB, L + SOFF, 128), F32), pltpu.VMEM((NLB, L + SOFF, 128), F32),
                        pltpu.VMEM((NLB, L, 128), F32), pltpu.VMEM((NLB, L, 128), F32)],
        params=_params("parallel"),
    )(uf, dyss, bre, bim, cre, cim, lamp, dsk)


def _branches(z1_ref, yss_ref, gt_ref, lng_ref, lnb_ref, wp_ref, wv_ref, wg_ref):
    zf = z1_ref[...]
    mu = jnp.mean(zf, axis=-1, keepdims=True)
    zc = zf - mu
    rstd = lax.rsqrt(jnp.mean(zc * zc, axis=-1, keepdims=True) + EPS)
    zn = zc * rstd
    z2 = zn * lng_ref[...] + lnb_ref[...]
    sz = _sigmoid(z2)
    z3 = (z2 * sz).astype(BF16)
    y_conv = _dot(z3, wp_ref[...])
    yss = yss_ref[...]
    yg = _gelu(yss).astype(BF16)
    sv = _dot(yg, wv_ref[...])
    sig = _sigmoid(_dot(yg, wg_ref[...]))
    y_ssm = sv * sig
    gc = gt_ref[:, 0:D].astype(F32)
    gs = gt_ref[:, D:2 * D].astype(F32)
    m = gc * y_conv + gs * y_ssm
    return dict(rstd=rstd, zn=zn, z2=z2, sz=sz, z3=z3, y_conv=y_conv, yss=yss, yg=yg, sv=sv, sig=sig,
                y_ssm=y_ssm, gc=gc, gs=gs, m=m)


def _merge_fwd(h, z1, yss, gate, lng, lnb, wp, wv, wg, wo, name, comm=None):
    L = h.shape[0]
    tm = _tile(L, 528)

    def body(h_ref, z1_ref, yss_ref, gt_ref, lng_ref, lnb_ref, wp_ref, wv_ref, wg_ref, wo_ref, o_ref):
        f = _branches(z1_ref, yss_ref, gt_ref, lng_ref, lnb_ref, wp_ref, wv_ref, wg_ref)
        o_ref[...] = h_ref[...] + _dot(f["m"].astype(BF16), wo_ref[...])

    def row(n):
        return pl.BlockSpec((tm, n), lambda i: (i, 0))

    return _call(
        body, comm, name=name, grid=(L // tm,),
        out_shape=[jax.ShapeDtypeStruct((L, D), F32)],
        in_specs=[row(D), row(DC), row(DS), row(2 * D), _res((1, DC)), _res((1, DC)),
                  _res((DC, D)), _res((DS, D)), _res((DS, D)), _res((D, D))],
        out_specs=[row(D)],
        params=_params("parallel"),
    )(h, z1, yss, gate, lng, lnb, wp, wv, wg, wo)


def _merge_bwd(dh, z1, yss, gate, lng, lnb, wp, wv, wg, wo, name, comm=None):
    L = dh.shape[0]
    tm = _tile(L, 352)

    def body(dh_ref, z1_ref, yss_ref, gt_ref, lng_ref, lnb_ref, wp_ref, wv_ref, wg_ref, wo_ref,
             m_ref, dgt_ref, dyc_ref, z3_ref, dz1_ref, yg_ref, dsv_ref, dsg_ref, dyss_ref,
             dbg_ref, dlng_ref, dlnb_ref):
        i = pl.program_id(0)
        f = _branches(z1_ref, yss_ref, gt_ref, lng_ref, lnb_ref, wp_ref, wv_ref, wg_ref)
        gc, gs, sig, sv = f["gc"], f["gs"], f["sig"], f["sv"]
        m_ref[...] = f["m"].astype(BF16)
        z3_ref[...] = f["z3"]
        yg_ref[...] = f["yg"]
        dm = _dot_nt(dh_ref[...].astype(BF16), wo_ref[...])
        dgc = (dm * f["y_conv"] * gc * (1.0 - gc)).astype(BF16)
        dgs = (dm * f["y_ssm"] * gs * (1.0 - gs)).astype(BF16)
        dgt_ref[:, 0:D] = dgc
        dgt_ref[:, D:2 * D] = dgs
        part = jnp.concatenate([jnp.sum(dgc.astype(F32), axis=0, keepdims=True),
                                jnp.sum(dgs.astype(F32), axis=0, keepdims=True)], axis=1)
        _acc_rows(dbg_ref, part, i == 0)
        dyc = (dm * gc).astype(BF16)
        dyc_ref[...] = dyc
        dys = dm * gs
        dsv = (dys * sig).astype(BF16)
        dsg = (dys * sv * sig * (1.0 - sig)).astype(BF16)
        dsv_ref[...] = dsv
        dsg_ref[...] = dsg
        dyg = _dot_nt(dsv, wv_ref[...]) + _dot_nt(dsg, wg_ref[...])
        dyss_ref[...] = dyg * _gelu_grad(f["yss"])
        dz3 = _dot_nt(dyc, wp_ref[...])
        z2, sz, zn = f["z2"], f["sz"], f["zn"]
        dz2 = dz3 * sz * (1.0 + z2 * (1.0 - sz))
        _acc_rows(dlng_ref, jnp.sum(dz2 * zn, axis=0, keepdims=True), i == 0)
        _acc_rows(dlnb_ref, jnp.sum(dz2, axis=0, keepdims=True), i == 0)
        dzn = dz2 * lng_ref[...]
        dz1_ref[...] = f["rstd"] * (dzn - jnp.mean(dzn, axis=-1, keepdims=True)
                                    - zn * jnp.mean(dzn * zn, axis=-1, keepdims=True))

    def row(n):
        return pl.BlockSpec((tm, n), lambda i: (i, 0))

    def tot(n):
        return pl.BlockSpec((1, n), lambda i: (0, 0))

    return _call(
        body, comm, name=name, grid=(L // tm,),
        out_shape=[jax.ShapeDtypeStruct((L, D), BF16), jax.ShapeDtypeStruct((L, 2 * D), BF16),
                   jax.ShapeDtypeStruct((L, D), BF16), jax.ShapeDtypeStruct((L, DC), BF16),
                   jax.ShapeDtypeStruct((L, DC), F32), jax.ShapeDtypeStruct((L, DS), BF16),
                   jax.ShapeDtypeStruct((L, D), BF16), jax.ShapeDtypeStruct((L, D), BF16),
                   jax.ShapeDtypeStruct((L, DS), F32),
                   jax.ShapeDtypeStruct((1, 2 * D), F32), jax.ShapeDtypeStruct((1, DC), F32),
                   jax.ShapeDtypeStruct((1, DC), F32)],
        in_specs=[row(D), row(DC), row(DS), row(2 * D), _res((1, DC)), _res((1, DC)),
                  _res((DC, D)), _res((DS, D)), _res((DS, D)), _res((D, D))],
        out_specs=[row(D), row(2 * D), row(D), row(DC), row(DC), row(DS), row(D), row(D), row(DS),
                   tot(2 * D), tot(DC), tot(DC)],
        params=_params("arbitrary"),
    )(dh, z1, yss, gate, lng, lnb, wp, wv, wg, wo)


def _ssm_disc(lam_re, lam_im, log_dt, b_re, b_im):
    lam = lax.complex(lam_re, lam_im)
    dt = jnp.exp(log_dt)[:, None]
    lam_bar = jnp.exp(lam * dt)
    bbar = ((lam_bar - 1.0) / lam)[..., None] * lax.complex(b_re, b_im)
    return jnp.real(lam_bar), jnp.imag(lam_bar), jnp.real(bbar), jnp.imag(bbar)


def _bdiag_in(m):
    m4 = m.reshape(NQ, G // NQ, P, H)
    return jnp.einsum("qgph,gk->qghkp", m4, jnp.eye(G // NQ, dtype=m.dtype)).reshape(NQ, QU, QS)


def _bdiag_out(m):
    m4 = m.reshape(NQ, G // NQ, H, P)
    return jnp.einsum("qghp,gk->qgpkh", m4, jnp.eye(G // NQ, dtype=m.dtype)).reshape(NQ, QS, QU)


def _diag_blocks(m4):
    return jnp.einsum("qiaib->qiab", m4).reshape(G, m4.shape[2], m4.shape[4])


def _pack(parts, rows_mult=8):
    flat = jnp.concatenate([p.reshape(-1).astype(F32) for p in parts])
    n = flat.shape[0]
    tot = -(-n // (128 * rows_mult)) * (128 * rows_mult)
    return jnp.pad(flat, (0, tot - n)).reshape(tot // 128, 128)


def _unpack(buf, shapes):
    flat = buf.reshape(-1)
    out, o = [], 0
    for s in shapes:
        n = math.prod(s)
        out.append(flat[o:o + n].reshape(s))
        o += n
    return out


def kernel(x, meta_tokens, ffn1_norm, ffn1_w1, ffn1_w3, ffn1_w2, mix_norm, w_in, b_gate, conv_dw, conv_dw_b, conv_ln_g, conv_ln_b, conv_proj, ssm_lam_re, ssm_lam_im, ssm_log_dt, ssm_b_re, ssm_b_im, ssm_c_re, ssm_c_im, ssm_d, ssm_w_v, ssm_w_g, w_out, ffn2_norm, ffn2_w1, ffn2_w3, ffn2_w2, final_norm, loss_target, m_meta_tokens, m_ffn1_norm, m_ffn1_w1, m_ffn1_w3, m_ffn1_w2, m_mix_norm, m_w_in, m_b_gate, m_conv_dw, m_conv_dw_b, m_conv_ln_g, m_conv_ln_b, m_conv_proj, m_ssm_lam_re, m_ssm_lam_im, m_ssm_log_dt, m_ssm_b_re, m_ssm_b_im, m_ssm_c_re, m_ssm_c_im, m_ssm_d, m_ssm_w_v, m_ssm_w_g, m_w_out, m_ffn2_norm, m_ffn2_w1, m_ffn2_w3, m_ffn2_w2, m_final_norm, v_meta_tokens, v_ffn1_norm, v_ffn1_w1, v_ffn1_w3, v_ffn1_w2, v_mix_norm, v_w_in, v_b_gate, v_conv_dw, v_conv_dw_b, v_conv_ln_g, v_conv_ln_b, v_conv_proj, v_ssm_lam_re, v_ssm_lam_im, v_ssm_log_dt, v_ssm_b_re, v_ssm_b_im, v_ssm_c_re, v_ssm_c_im, v_ssm_d, v_ssm_w_v, v_ssm_w_g, v_w_out, v_ffn2_norm, v_ffn2_w1, v_ffn2_w3, v_ffn2_w2, v_final_norm):
    args = dict(locals())
    names = ["meta_tokens", "ffn1_norm", "ffn1_w1", "ffn1_w3", "ffn1_w2", "mix_norm", "w_in", "b_gate",
             "conv_dw", "conv_dw_b", "conv_ln_g", "conv_ln_b", "conv_proj", "ssm_lam_re", "ssm_lam_im",
             "ssm_log_dt", "ssm_b_re", "ssm_b_im", "ssm_c_re", "ssm_c_im", "ssm_d", "ssm_w_v", "ssm_w_g",
             "w_out", "ffn2_norm", "ffn2_w1", "ffn2_w3", "ffn2_w2", "final_norm"]
    big = ["ffn1_w1", "ffn1_w3", "ffn1_w2", "w_in", "conv_proj", "ssm_w_v", "ssm_w_g", "w_out",
           "ffn2_w1", "ffn2_w3", "ffn2_w2"]
    small = [n for n in names if n not in big]

    xs = x[0]
    S = xs.shape[0]
    L = FRONT + S
    T = L // NSEG
    jx, jy = lax.axis_index("x"), lax.axis_index("y")
    chip = 2 * jx + jy

    small_all = _gather_all(_pack([meta_tokens, conv_dw[0]]), "gather_small")
    sm = small_all[0::2].reshape(NSH, -1)
    nmt = NMETA * (D // NSH)
    ndw = KW * (DC // NSH)
    meta_full = sm[:, :nmt].reshape(NSH, NMETA, D // NSH).transpose(1, 0, 2).reshape(NMETA, D)
    dw_full = sm[:, nmt:nmt + ndw].reshape(NSH, KW, DC // NSH).transpose(1, 0, 2).reshape(KW, DC)
    dw_pad = jnp.pad(dw_full, ((0, KWP - KW), (0, 0)))
    tposed = ("ffn1_w1", "ffn1_w3", "ffn2_w1", "ffn2_w3")

    def view(a, n):
        return jnp.swapaxes(a, 1, 2) if n in tposed else a

    grp_a = ["ffn1_w1", "ffn1_w3", "ffn1_w2"]
    grp_b = ["w_in", "conv_proj", "ssm_w_v", "ssm_w_g", "w_out"]
    grp_c = ["ffn2_w1", "ffn2_w3", "ffn2_w2"]

    shards = {n: view(args[n], n)[0].astype(BF16) for n in big}

    def shard(n):
        return shards[n]

    sh_a = [shard(n) for n in grp_a]
    ga_send, ga_recv, sh_a, land_a, _ = _chips_start(
        sh_a, [jax.ShapeDtypeStruct((NSH,) + s.shape, s.dtype) for s in sh_a], True, "gather_ffn1_start",
        [small_all])

    def cols(w):
        return w.transpose(1, 0, 2).reshape(w.shape[1], -1)

    disc_in = (ssm_lam_re[0], ssm_lam_im[0], ssm_log_dt[0], ssm_b_re[0], ssm_b_im[0])
    (lbr, lbi, bbr, bbi), disc_vjp = jax.vjp(_ssm_disc, *disc_in)
    lam_t = jnp.exp(lax.complex(ssm_lam_re[0], ssm_lam_im[0]) * (jnp.exp(ssm_log_dt[0])[:, None] * T))
    lamp = jnp.concatenate([lbr.reshape(1, NST), lbi.reshape(1, NST), jnp.real(lam_t).reshape(1, NST),
                            jnp.imag(lam_t).reshape(1, NST), jnp.zeros((4, NST), F32)], axis=0)
    bre_bd, bim_bd = _bdiag_in(bbr).astype(BF16), _bdiag_in(bbi).astype(BF16)
    cre_bd, cim_bd = _bdiag_out(ssm_c_re[0]).astype(BF16), _bdiag_out(ssm_c_im[0]).astype(BF16)

    h0 = lax.dynamic_update_slice(jnp.pad(xs, ((FRONT, 0), (0, 0))), meta_full, (FRONT - NMETA, 0))
    tgt = jnp.pad(loss_target[0], ((FRONT, 0), (0, 0)))
    small_wmv = [_pack([args[p + n] for n in small])[None] for p in ("", "m_", "v_")]
    early_work = [h0, tgt, bre_bd, bim_bd, cre_bd, cim_bd] + [shards[n] for n in grp_b + grp_c] + small_wmv
    sh_a, land_a = _chips_wait(ga_send, ga_recv, sh_a, land_a, early_work, True, "gather_ffn1_wait")
    gw = dict(zip(grp_a, _pass_halves(land_a, "pass_ffn1", sh_a)))
    (h1, a1, b1), got = _ffn_fwd(h0, ffn1_norm, gw["ffn1_w1"], gw["ffn1_w3"], gw["ffn1_w2"], "ffn1_fwd",
                                 _gather_half_behind([shard(n) for n in grp_b]))
    w_in_f = _pass_halves(got[:1], "pass_w_in")[0]
    (vg, uf, gate), got1 = _mix_in_fwd(h1, mix_norm, w_in_f, b_gate, "mix_in_fwd",
                                       _join(_gather_half_behind([shard("ffn2_w1")]),
                                             _pass_halves_behind(list(got[1:]))))
    gw.update(zip(grp_b[1:], got1[1:]))
    wp_f, wv_f, wg_f = cols(gw["conv_proj"]), cols(gw["ssm_w_v"]), cols(gw["ssm_w_g"])
    wo_f = gw["w_out"].reshape(D, D)
    (z1,), got3 = _conv_fwd(vg, dw_pad, conv_dw_b, "conv_fwd", _gather_half_behind([shard("ffn2_w3")]))
    (yss,), got2 = _ssm_fwd(uf, bre_bd, bim_bd, cre_bd, cim_bd, lamp, ssm_d, "ssm_fwd",
                            _gather_half_behind([shard("ffn2_w2")]))
    (h2,), got_c = _merge_fwd(h1, z1, yss, gate, conv_ln_g, conv_ln_b, wp_f, wv_f, wg_f, wo_f, "merge_fwd",
                              _pass_halves_behind([got1[0], got3[0], got2[0]]))
    gw.update(zip(grp_c, got_c))
    dh3, a2, b2, loss_part, d_final = _ffn_fwd_loss(
        h2, ffn2_norm, gw["ffn2_w1"], gw["ffn2_w3"], gw["ffn2_w2"], final_norm.reshape(1, D), tgt, "ffn2_fwd_loss")

    gbig = {}
    core = lax.axis_index("c").astype(jnp.int32).reshape(1)

    def pair_sums(group, tag, sib=None):
        gl = [gbig[n] for n in group]
        if sib is None:
            sib = _pair_exchange(gl, "pair_exchange_" + tag)
        out = [None] * len(group)
        for idx in _by_shape(gl):
            res = _add_pair([gl[i] for i in idx], [sib[i] for i in idx], core, "pair_" + group[idx[0]])
            for i, r in zip(idx, res):
                out[i] = r
        return out

    (dh2, da2, db2, s2, n2, d_ffn2_norm), _ = _ffn_bwd(
        h2, ffn2_norm, dh3, a2, b2, gw["ffn2_w1"], gw["ffn2_w3"], gw["ffn2_w2"], "ffn2_bwd")
    gbig["ffn2_w1"] = _wgrad(da2, n2, "ffn2_dw1")
    gbig["ffn2_w3"] = _wgrad(db2, n2, "ffn2_dw3")
    gbig["ffn2_w2"] = _wgrad(s2, dh3, "ffn2_dw2", 0.5)
    (m_b, dgate, dyc, z3, dz1, yg, dsv, dsg, dyss, d_b_gate, d_ln_g, d_ln_b), sib_c = _merge_bwd(
        dh2, z1, yss, gate, conv_ln_g, conv_ln_b, wp_f, wv_f, wg_f, wo_f, "merge_bwd",
        _pair_exchange_behind([gbig[n] for n in grp_c]))
    pair_c = pair_sums(grp_c, "ffn2", list(sib_c))
    gbig["w_out"] = _wgrad(m_b, dh2, "dw_out").reshape(NSH, D // NSH, D)

    def shard_cols(gm):
        return gm.reshape(gm.shape[0], NSH, -1).transpose(1, 0, 2)

    gbig["conv_proj"] = shard_cols(_wgrad(z3, dyc, "dw_proj"))
    gbig["ssm_w_v"] = shard_cols(_wgrad(yg, dsv, "dw_v"))
    gbig["ssm_w_g"] = shard_cols(_wgrad(yg, dsg, "dw_g"))
    dv, dgl, ddw, d_dw_b = _conv_bwd(dz1, vg, dw_pad, "conv_bwd")
    (duf, dbre, dbim, dcre, dcim, dlam, d_ssm_d), recv_c = _ssm_bwd(
        uf, dyss, bre_bd, bim_bd, cre_bd, cim_bd, lamp, ssm_d, "ssm_bwd", _scatter_chips_behind(pair_c))
    dh1, u_b, dproj, d_mix_norm = _mix_in_bwd(h1, mix_norm, dh2, dv, dgl, duf, dgate, w_in_f, "mix_in_bwd")
    gbig["w_in"] = _wgrad(u_b, dproj, "dw_in")
    pair_b = pair_sums(grp_b, "mix")

    d_bbr = _diag_blocks(dbre.reshape(NQ, 8, H, 8, P)).transpose(0, 2, 1)
    d_bbi = _diag_blocks(dbim.reshape(NQ, 8, H, 8, P)).transpose(0, 2, 1)
    d_c_re = _diag_blocks(dcre.reshape(NQ, 8, P, 8, H)).transpose(0, 2, 1)
    d_c_im = _diag_blocks(dcim.reshape(NQ, 8, P, 8, H)).transpose(0, 2, 1)
    d_lbr = dlam[:, 0, :].reshape(G, P)
    d_lbi = dlam[:, 1, :].reshape(G, P)
    d_lam_re, d_lam_im, d_log_dt, d_b_re, d_b_im = disc_vjp((d_lbr, d_lbi, d_bbr, d_bbi))

    sg = {"mix_norm": d_mix_norm, "b_gate": d_b_gate, "conv_dw": ddw[:KW], "conv_dw_b": d_dw_b,
          "conv_ln_g": d_ln_g, "conv_ln_b": d_ln_b, "ssm_lam_re": d_lam_re, "ssm_lam_im": d_lam_im,
          "ssm_log_dt": d_log_dt, "ssm_b_re": d_b_re, "ssm_b_im": d_b_im, "ssm_c_re": d_c_re, "ssm_c_im": d_c_im,
          "ssm_d": d_ssm_d, "ffn2_norm": d_ffn2_norm, "final_norm": d_final}
    late = ["meta_tokens", "ffn1_norm"]
    early = [n for n in small if n not in late]

    (dh0, da1, db1, s1, n1, d_ffn1_norm), got = _ffn_bwd(
        h0, ffn1_norm, dh1, a1, b1, gw["ffn1_w1"], gw["ffn1_w3"], gw["ffn1_w2"], "ffn1_bwd",
        _join(_scatter_chips_behind(pair_b), _gather_all_behind(_pack([sg[n] for n in early]))))
    recv_b, early_all = got[:len(grp_b)], got[len(grp_b)]
    gbig["ffn1_w1"] = _wgrad(da1, n1, "ffn1_dw1")
    gbig["ffn1_w3"] = _wgrad(db1, n1, "ffn1_dw3")
    gbig["ffn1_w2"] = _wgrad(s1, dh1, "ffn1_dw2", 0.5)
    grad_x = dh0[FRONT:][None]
    sg["meta_tokens"] = dh0[FRONT - NMETA:FRONT]
    sg["ffn1_norm"] = d_ffn1_norm

    pair_a = pair_sums(grp_a, "ffn1")
    late_all = _gather_all(_pack([sg[n] for n in late]), "gather_late_grads")
    sa_send, sa_recv, pair_a, land_s, sa_token = _chips_start(
        pair_a, [jax.ShapeDtypeStruct(p.shape, p.dtype) for p in pair_a], False, "scatter_ffn1_start", [late_all])

    out_g, out_d, out_m, out_v = {}, {}, {}, {}

    def finish(group, recvs, tag, after=None):
        halves = [None] * len(group)
        for idx in _by_shape(recvs):
            res = _sum_slots([recvs[i] for i in idx], "sum_" + group[idx[0]], after)
            for i, r in zip(idx, res):
                halves[i] = r
        fours = [(view(args[n], n), f.reshape(1, f.shape[0] * f.shape[1], f.shape[2]), view(args["m_" + n], n),
                  view(args["v_" + n], n)) for n, f in zip(group, _swap_halves(halves, "swap_" + tag))]
        for idx in _by_shape([four[0] for four in fours]):
            for i, (g3, d3, m3, v3) in zip(idx, _adamw([fours[i] for i in idx], "adamw_" + group[idx[0]])):
                n = group[i]
                out_g[n], out_d[n], out_m[n], out_v[n] = (view(t, n) for t in (g3, d3, m3, v3))
                done.append(d3)

    done = []
    finish(grp_b + grp_c, list(recv_b) + list(recv_c), "mix_ffn2", sa_token)

    sgr = dict(zip(early, _unpack(_sum_slots([early_all], "sum_early", sa_token)[0], [sg[n].shape for n in early])))
    sgr.update(zip(late, _unpack(_sum_slots([late_all], "sum_late")[0], [sg[n].shape for n in late])))
    sgr["meta_tokens"] = lax.dynamic_slice_in_dim(sgr["meta_tokens"], chip * (D // NSH), D // NSH, axis=1)
    sgr["conv_dw"] = lax.dynamic_slice_in_dim(sgr["conv_dw"], chip * (DC // NSH), DC // NSH, axis=1)
    pshapes = [args[n].shape for n in small]
    _, d_s, m_s, v_s = _adamw([(small_wmv[0], _pack([sgr[n] for n in small])[None], small_wmv[1], small_wmv[2])],
                              "adamw_small")[0]
    for n, g_, d_, m_, v_ in zip(small, [sgr[n] for n in small], _unpack(d_s[0], pshapes),
                                 _unpack(m_s[0], pshapes), _unpack(v_s[0], pshapes)):
        out_g[n], out_d[n], out_m[n], out_v[n] = g_.reshape(args[n].shape), d_, m_, v_

    loss = lax.psum(loss_part[0, 0], ("x", "y", "c"))
    pair_a, recv_a = _chips_wait(sa_send, sa_recv, pair_a, land_s, [d_s, grad_x] + done, False,
                                 "scatter_ffn1_wait")
    finish(grp_a, _fill_own(pair_a, recv_a, "own_ffn1"), "ffn1")
    return (loss, grad_x, *[out_g[n] for n in names], *[out_d[n] for n in names],
            *[out_m[n] for n in names], *[out_v[n] for n in names])
```
